```python
import jax
import jax.numpy as jnp
from jax import lax
import numpy as np

D_MODEL = 1024
BATCH = 8
SEQ = 4096
DEPTH = 2

GLA_HEADS = 4
GLA_DK = 64
GLA_DV = 128
GLA_GATE_RANK = 16
GLA_TAU = 16.0
GLA_CHUNK = 64
MLSTM_HEADS = 4
MLSTM_DH = 128
MLSTM_CONV = 4
MLSTM_CHUNK = 64
SB_HEADS = 8
SB_DH = 64
SB_BLOCK = 128
DIL_PAIRS = ((128, 1), (512, 4), (2048, 16))
DIL_HEADS_PER_GROUP = 4
DIL_DH = 64
DIL_HEADS = len(DIL_PAIRS) * DIL_HEADS_PER_GROUP
N_REL_BUCKETS = 32
REL_MAX_DIST = 2048
D_FF = ((8 * D_MODEL + 3 * 256 - 1) // (3 * 256)) * 256
LN_EPS = 1e-5
RES_ALPHA = (2 * DEPTH) ** 0.25
OUT_BETA = (8 * DEPTH) ** -0.25
ADA_SCALE = 0.1
NEG_INF = -1e30
N_EVEN = (DEPTH + 1) // 2
N_ODD = DEPTH // 2

GLA_QK = GLA_HEADS * GLA_DK
GLA_V = GLA_HEADS * GLA_DV
ML_W = MLSTM_HEADS * MLSTM_DH
AB_SPLITS = (GLA_QK, GLA_QK, GLA_V, GLA_V, GLA_GATE_RANK, 2 * ML_W, ML_W, MLSTM_HEADS, MLSTM_HEADS, ML_W)
AB_IN = sum(AB_SPLITS)
AB_OUT = GLA_V + ML_W
SB_W = SB_HEADS * SB_DH
DIL_W = DIL_HEADS * DIL_DH
CD_SPLITS = (SB_W, SB_W, SB_W, DIL_W, DIL_W, DIL_W)
CD_IN = sum(CD_SPLITS)
CD_OUT = SB_W + DIL_HEADS_PER_GROUP * DIL_DH

kernel_name = 'hybrid_gla_mlstm_stickbreak_dilated'


def split_cols(t, sizes):
    idx = [int(s) for s in np.cumsum(sizes)[:-1]]
    return jnp.split(t, idx, axis=-1)


def layer_norm(x, g, b):
    xf = x.astype(jnp.float32)
    mu = jnp.mean(xf, -1, keepdims=True)
    var = jnp.mean(jnp.square(xf - mu), -1, keepdims=True)
    y = (xf - mu) * lax.rsqrt(var + LN_EPS)
    return (y * g.astype(jnp.float32) + b.astype(jnp.float32)).astype(x.dtype)


def head_norm(x, g):
    mu = jnp.mean(x, -1, keepdims=True)
    var = jnp.mean(jnp.square(x - mu), -1, keepdims=True)
    return (x - mu) * lax.rsqrt(var + LN_EPS) * g.astype(jnp.float32).reshape(x.shape[-2], x.shape[-1])


def causal_dwconv(x, w, b):
    k_w, ch = w.shape
    y = lax.conv_general_dilated(x, w.astype(x.dtype)[:, None, :], window_strides=(1,),
                                 padding=((k_w - 1, 0),), dimension_numbers=('NWC', 'WIO', 'NWC'),
                                 feature_group_count=ch)
    return y + b.astype(x.dtype)


def gla_chunked(q, k, v, log_a):
    bsz, s_len, h, dk = q.shape
    dv = v.shape[-1]
    cl = GLA_CHUNK
    nc = s_len // cl
    q, k, v, log_a = [t.reshape(bsz, nc, cl, h, t.shape[-1]) for t in (q, k, v, log_a)]
    b = jnp.cumsum(log_a, axis=2)
    b_last = b[:, :, -1]
    q_in = q * jnp.exp(b)
    k_in = k * jnp.exp(-b)
    tril = np.tril(np.ones((cl, cl), dtype=bool))
    scores = jnp.where(tril, jnp.einsum('bnihk,bnjhk->bnhij', q_in, k_in), 0.0)
    o_intra = jnp.einsum('bnhij,bnjhv->bnihv', scores, v)
    kv_chunk = jnp.einsum('bnjhk,bnjhv->bnhkv', k * jnp.exp(b_last[:, :, None] - b), v)
    decay = jnp.exp(b_last)

    def step(state, inp):
        dec, kv = inp
        return dec[..., None] * state + kv, state

    init = jnp.zeros((bsz, h, dk, dv), jnp.float32)
    _, s_before = lax.scan(step, init, (jnp.moveaxis(decay, 1, 0), jnp.moveaxis(kv_chunk, 1, 0)))
    o_inter = jnp.einsum('bnihk,nbhkv->bnihv', q_in, s_before)
    return (o_intra + o_inter).reshape(bsz, s_len, h, dv)


def mlstm_chunked(q, k, v, i_pre, log_f):
    bsz, s_len, h, d = q.shape
    cl = MLSTM_CHUNK
    nc = s_len // cl
    q, k, v = [t.reshape(bsz, nc, cl, h, d) for t in (q, k, v)]
    i_pre, log_f = [t.reshape(bsz, nc, cl, h) for t in (i_pre, log_f)]
    b = jnp.cumsum(log_f, axis=2)
    b_last = b[:, :, -1]
    tril = np.tril(np.ones((cl, cl), dtype=bool))[:, :, None]
    d_log = jnp.where(tril, b[:, :, :, None, :] - b[:, :, None, :, :] + i_pre[:, :, None, :, :], NEG_INF)
    w_end = b_last[:, :, None, :] - b + i_pre

    def step(carry, inp):
        c_bar, n_bar, m = carry
        k_c, v_c, w_c, bl = inp
        m_new = jnp.maximum(bl + m, jnp.max(w_c, axis=1))
        sc = jnp.exp(bl + m - m_new)
        wj = jnp.exp(w_c - m_new[:, None, :])
        c_new = sc[..., None, None] * c_bar + jnp.einsum('bjh,bjhv,bjhk->bhvk', wj, v_c, k_c)
        n_new = sc[..., None] * n_bar + jnp.einsum('bjh,bjhk->bhk', wj, k_c)
        return (c_new, n_new, m_new), (c_bar, n_bar, m)

    init = (jnp.zeros((bsz, h, d, d), jnp.float32), jnp.zeros((bsz, h, d), jnp.float32),
            jnp.zeros((bsz, h), jnp.float32))
    xs = (jnp.moveaxis(k, 1, 0), jnp.moveaxis(v, 1, 0), jnp.moveaxis(w_end, 1, 0), jnp.moveaxis(b_last, 1, 0))
    _, (c_prev, n_prev, m_prev) = lax.scan(step, init, xs)
    inter_log = b + jnp.moveaxis(m_prev, 0, 1)[:, :, None, :]
    m_i = jnp.maximum(inter_log, jnp.max(d_log, axis=3))
    w_intra = jnp.exp(d_log - m_i[:, :, :, None, :])
    s_inter = jnp.exp(inter_log - m_i)
    a = w_intra * jnp.einsum('bnihd,bnjhd->bnijh', q, k)
    num = jnp.einsum('bnijh,bnjhd->bnihd', a, v) + s_inter[..., None] * jnp.einsum('bnihk,nbhvk->bnihv', q, c_prev)
    den = jnp.sum(a, axis=3) + s_inter * jnp.einsum('bnihk,nbhk->bnih', q, n_prev)
    hid = num / jnp.maximum(jnp.abs(den), jnp.exp(-m_i))[..., None]
    return hid.reshape(bsz, s_len, h, d)


def stick_breaking(q, k, v):
    bsz, s_len, h, dh = q.shape
    nq = s_len // SB_BLOCK
    scale = dh ** -0.5
    qb = q.reshape(bsz, nq, SB_BLOCK, h, dh).transpose(1, 0, 2, 3, 4)
    s_pos = jnp.arange(s_len)

    def block(args):
        q_blk, i = args
        z = jnp.einsum('bqhd,bkhd->bhqk', q_blk, k) * scale
        t_pos = i * SB_BLOCK + jnp.arange(SB_BLOCK)
        causal = s_pos[None, :] < t_pos[:, None]
        log_1m = jnp.where(causal, jax.nn.log_sigmoid(-z), 0.0)
        suffix = lax.cumsum(log_1m, axis=3, reverse=True)
        excl = jnp.pad(suffix[..., 1:], ((0, 0), (0, 0), (0, 0), (0, 1)))
        att = jnp.where(causal, jnp.exp(jax.nn.log_sigmoid(z) + excl), 0.0)
        return jnp.einsum('bhqk,bkhd->bqhd', att, v)

    out = lax.map(block, (qb, jnp.arange(nq)))
    return out.transpose(1, 0, 2, 3, 4).reshape(bsz, s_len, h, dh)


def t5_bucket(dist):
    max_exact = N_REL_BUCKETS // 2
    dd = np.maximum(dist, 1).astype(np.float64)
    large = max_exact + (np.log(dd / max_exact) / np.log(REL_MAX_DIST / max_exact)
                         * (N_REL_BUCKETS - max_exact)).astype(np.int32)
    large = np.minimum(large, N_REL_BUCKETS - 1)
    return np.where(dist < max_exact, dist, large).astype(np.int32)


def dilated_group(q, k, v, bias_tab, window, dil):
    bsz, s_len, h, dh = q.shape
    blk = window // dil
    unit = blk * dil
    sp = -(-s_len // unit) * unit
    n_sub = sp // dil
    nb = n_sub // blk

    def to_sub(t):
        t = jnp.pad(t, ((0, 0), (0, sp - s_len), (0, 0), (0, 0)))
        t = t.reshape(bsz, n_sub, dil, h, dh).transpose(0, 2, 1, 3, 4)
        return t.reshape(bsz, dil, nb, blk, h, dh)

    def with_prev(t):
        prev = jnp.pad(t[:, :, :-1], ((0, 0), (0, 0), (1, 0), (0, 0), (0, 0), (0, 0)))
        return jnp.concatenate([prev, t], axis=3)

    qs = to_sub(q)
    kk = with_prev(to_sub(k))
    vv = with_prev(to_sub(v))
    qi = np.arange(blk)[:, None]
    kj = np.arange(2 * blk)[None, :]
    delta = qi - kj + blk
    in_win = (delta >= 0) & (delta <= blk)
    valid = in_win[None] & ((np.arange(nb)[:, None, None] > 0) | (kj[None] >= blk))
    bucket = t5_bucket(np.clip(delta, 0, None) * dil)
    bias = jnp.transpose(bias_tab[bucket].astype(jnp.float32), (2, 0, 1))
    logits = jnp.einsum('brnqhd,brnkhd->brnhqk', qs, kk) * dh ** -0.5 + bias
    logits = jnp.where(valid[:, None], logits, NEG_INF)
    m = jnp.max(logits, axis=-1, keepdims=True)
    p = jnp.exp(logits - m)
    den = jnp.sum(p, axis=-1)
    o = jnp.einsum('brnhqk,brnkhd->brnqhd', p, vv) / jnp.swapaxes(den, -1, -2)[..., None]
    lse = jnp.swapaxes(m[..., 0] + jnp.log(den), -1, -2)

    def from_sub(t):
        rest = t.shape[4:]
        t = jnp.swapaxes(t.reshape((bsz, dil, n_sub) + rest), 1, 2)
        return t.reshape((bsz, sp) + rest)[:, :s_len]

    return from_sub(o), from_sub(lse)


def dilated_mixture(q, k, v, rel_bias):
    outs, lses = [], []
    for g, (window, dil) in enumerate(DIL_PAIRS):
        tab = rel_bias[:, g * DIL_HEADS_PER_GROUP:(g + 1) * DIL_HEADS_PER_GROUP]
        o, l = dilated_group(q[:, :, g], k[:, :, g], v[:, :, g], tab, window, dil)
        outs.append(o)
        lses.append(l)
    wts = jax.nn.softmax(jnp.stack(lses, 0), axis=0)
    return jnp.sum(wts[..., None] * jnp.stack(outs, 0), axis=0)


def gla_mlstm_mixer(h, w_in, wa_up, ba, gla_g, conv_w, conv_b, b_i, b_f, ml_g, w_out):
    bsz, s_len, _ = h.shape
    proj = (h @ w_in).astype(jnp.float32)
    qa, ka, va, ra, aa, qkb, vb, ib, fb, ob = split_cols(proj, AB_SPLITS)
    qa = qa.reshape(bsz, s_len, GLA_HEADS, GLA_DK) * GLA_DK ** -0.5
    ka = ka.reshape(bsz, s_len, GLA_HEADS, GLA_DK)
    va = va.reshape(bsz, s_len, GLA_HEADS, GLA_DV)
    log_a = jax.nn.log_sigmoid(aa @ wa_up.astype(jnp.float32) + ba.astype(jnp.float32)) / GLA_TAU
    oa = gla_chunked(qa, ka, va, log_a.reshape(bsz, s_len, GLA_HEADS, GLA_DK))
    oa = (head_norm(oa, gla_g).reshape(bsz, s_len, GLA_V)) * jax.nn.silu(ra)
    qkb = jax.nn.silu(causal_dwconv(qkb, conv_w, conv_b))
    qb_, kb_ = jnp.split(qkb, 2, axis=-1)
    qb_ = qb_.reshape(bsz, s_len, MLSTM_HEADS, MLSTM_DH)
    kb_ = kb_.reshape(bsz, s_len, MLSTM_HEADS, MLSTM_DH) * MLSTM_DH ** -0.5
    vb = vb.reshape(bsz, s_len, MLSTM_HEADS, MLSTM_DH)
    i_pre = ib + b_i.astype(jnp.float32)
    log_f = jax.nn.log_sigmoid(fb + b_f.astype(jnp.float32))
    hb = mlstm_chunked(qb_, kb_, vb, i_pre, log_f)
    hb = jax.nn.sigmoid(ob) * head_norm(hb, ml_g).reshape(bsz, s_len, ML_W)
    y = jnp.concatenate([oa, hb], axis=-1) @ w_out
    return y.astype(h.dtype)


def sb_dilated_mixer(h, w_in, rel_bias, w_out):
    bsz, s_len, _ = h.shape
    proj = (h @ w_in).astype(jnp.float32)
    qc, kc, vc, qd, kd, vd = split_cols(proj, CD_SPLITS)
    hc = lambda t: t.reshape(bsz, s_len, SB_HEADS, SB_DH)
    oc = stick_breaking(hc(qc), hc(kc), hc(vc)).reshape(bsz, s_len, SB_W)
    hd = lambda t: t.reshape(bsz, s_len, len(DIL_PAIRS), DIL_HEADS_PER_GROUP, DIL_DH)
    od = dilated_mixture(hd(qd), hd(kd), hd(vd), rel_bias).reshape(bsz, s_len, DIL_HEADS_PER_GROUP * DIL_DH)
    y = jnp.concatenate([oc, od], axis=-1) @ w_out
    return y.astype(h.dtype)


def swiglu(h, w1, w3, w2):
    return (jax.nn.silu(h @ w1) * (h @ w3)) @ w2


def setup_inputs(seed: int = 0) -> dict:
    key = jax.random.key(seed)
    ks = jax.random.split(key, 22)
    nrm = lambda k, shape, s: jax.random.normal(k, shape, jnp.float32) * s
    d = D_MODEL
    return {
        'x': nrm(ks[0], (BATCH, SEQ, d), 1.0),
        'c': nrm(ks[1], (BATCH, d), 1.0),
        'ada_w': nrm(ks[2], (DEPTH, d, 6 * d), ADA_SCALE * d ** -0.5),
        'ada_b': nrm(ks[3], (DEPTH, 6 * d), 0.01),
        'ln_g': 1.0 + nrm(ks[4], (DEPTH, 2, d), 0.02),
        'ln_b': nrm(ks[5], (DEPTH, 2, d), 0.02),
        'ab_w_in': nrm(ks[6], (N_EVEN, d, AB_IN), d ** -0.5),
        'gla_wa_up': nrm(ks[7], (N_EVEN, GLA_GATE_RANK, GLA_QK), GLA_GATE_RANK ** -0.5),
        'gla_ba': nrm(ks[8], (N_EVEN, GLA_QK), 0.1),
        'gla_norm_g': 1.0 + nrm(ks[9], (N_EVEN, GLA_V), 0.02),
        'ml_conv_w': nrm(ks[10], (N_EVEN, MLSTM_CONV, 2 * ML_W), MLSTM_CONV ** -0.5),
        'ml_conv_b': nrm(ks[11], (N_EVEN, 2 * ML_W), 0.01),
        'ml_b_i': nrm(ks[12], (N_EVEN, MLSTM_HEADS), 0.1),
        'ml_b_f': jnp.linspace(3.0, 6.0, MLSTM_HEADS)[None, :] + nrm(ks[13], (N_EVEN, MLSTM_HEADS), 0.1),
        'ml_norm_g': 1.0 + nrm(ks[14], (N_EVEN, ML_W), 0.02),
        'ab_w_out': nrm(ks[15], (N_EVEN, AB_OUT, d), OUT_BETA * AB_OUT ** -0.5),
        'cd_w_in': nrm(ks[16], (N_ODD, d, CD_IN), d ** -0.5),
        'rel_bias': nrm(ks[17], (N_REL_BUCKETS, DIL_HEADS), 0.5),
        'cd_w_out': nrm(ks[18], (N_ODD, CD_OUT, d), OUT_BETA * CD_OUT ** -0.5),
        'ffn_w1': nrm(ks[19], (DEPTH, d, D_FF), d ** -0.5),
        'ffn_w3': nrm(ks[20], (DEPTH, d, D_FF), d ** -0.5),
        'ffn_w2': nrm(ks[21], (DEPTH, D_FF, d), OUT_BETA * D_FF ** -0.5),
    }


def reference(x, c, ada_w, ada_b, ln_g, ln_b, ab_w_in, gla_wa_up, gla_ba, gla_norm_g, ml_conv_w, ml_conv_b,
              ml_b_i, ml_b_f, ml_norm_g, ab_w_out, cd_w_in, rel_bias, cd_w_out, ffn_w1, ffn_w3, ffn_w2):
    c_act = jax.nn.silu(c)
    for layer in range(DEPTH):
        mod = c_act @ ada_w[layer] + ada_b[layer]
        sh_m, sc_m, g_m, sh_f, sc_f, g_f = [t[:, None, :] for t in jnp.split(mod, 6, axis=-1)]
        hm = x * (1.0 + sc_m) + sh_m
        j = layer // 2
        if layer % 2 == 0:
            y = gla_mlstm_mixer(hm, ab_w_in[j], gla_wa_up[j], gla_ba[j], gla_norm_g[j], ml_conv_w[j], ml_conv_b[j],
                                ml_b_i[j], ml_b_f[j], ml_norm_g[j], ab_w_out[j])
        else:
            y = sb_dilated_mixer(hm, cd_w_in[j], rel_bias, cd_w_out[j])
        x = layer_norm(RES_ALPHA * x + (1.0 + g_m) * y, ln_g[layer, 0], ln_b[layer, 0])
        hf = x * (1.0 + sc_f) + sh_f
        yf = swiglu(hf, ffn_w1[layer], ffn_w3[layer], ffn_w2[layer])
        x = layer_norm(RES_ALPHA * x + (1.0 + g_f) * yf, ln_g[layer, 1], ln_b[layer, 1])
    return x
```

```python
import functools

import numpy as np
import jax
import jax.numpy as jnp
from jax import lax
from jax.experimental import pallas as pl
from jax.experimental.pallas import tpu as pltpu

F32 = jnp.float32
BF16 = jnp.bfloat16

D_MODEL = 1024
DEPTH = 2
GLA_HEADS = 4
GLA_DK = 64
GLA_DV = 128
GLA_GATE_RANK = 16
GLA_TAU = 16.0
GLA_CHUNK = 64
MLSTM_HEADS = 4
MLSTM_DH = 128
MLSTM_CONV = 4
SB_HEADS = 8
SB_DH = 64
DIL_PAIRS = ((128, 1), (512, 4), (2048, 16))
DIL_HEADS_PER_GROUP = 4
DIL_DH = 64
DIL_BLK = 128
N_REL_BUCKETS = 32
REL_MAX_DIST = 2048
D_FF = ((8 * D_MODEL + 3 * 256 - 1) // (3 * 256)) * 256
LN_EPS = 1e-5
RES_ALPHA = (2 * DEPTH) ** 0.25
NEG_INF = -1e30

GLA_QK = GLA_HEADS * GLA_DK
GLA_V = GLA_HEADS * GLA_DV
ML_W = MLSTM_HEADS * MLSTM_DH
SB_W = SB_HEADS * SB_DH
DIL_W = len(DIL_PAIRS) * DIL_HEADS_PER_GROUP * DIL_DH
DIL_OUT = DIL_HEADS_PER_GROUP * DIL_DH

LANES = 128
VMEM_LIMIT = 56 * 1024 * 1024

P0_QA = 0
P0_KA = P0_QA + GLA_QK
P0_VA = P0_KA + GLA_QK
P0_RA = P0_VA + GLA_V
P0_QKB = P0_RA + GLA_V
P0_VB = P0_QKB + 2 * ML_W
P0_OB = P0_VB + ML_W
P0_G = P0_OB + ML_W
P0_N = P0_G + LANES
G_I = 0
G_F = MLSTM_HEADS
G_A = 2 * MLSTM_HEADS

MIX_TILE = 256
FF_CHUNK = 256
SB_T = 128
_CHUNK_SHIFT = 6
_HEAD_SHIFT = 6
assert GLA_CHUNK == GLA_DK == 1 << _CHUNK_SHIFT and SB_DH == DIL_DH == 1 << _HEAD_SHIFT


def _dot(a, b):
    return jnp.dot(a, b, preferred_element_type=F32)


def _dot_nt(a, b):
    return lax.dot_general(a, b, (((1,), (1,)), ((), ())), preferred_element_type=F32)


def _dot_tn(a, b):
    return lax.dot_general(a, b, (((0,), (0,)), ((), ())), preferred_element_type=F32)


def _dot_split(t, x, terms):
    acc = None
    rem = x
    for i in range(terms):
        part = rem.astype(BF16)
        d = _dot(t, part)
        acc = d if acc is None else acc + d
        if i + 1 < terms:
            rem = rem - part.astype(F32)
    return acc


def _log_sigmoid(x):
    return jnp.minimum(x, 0.0) - jnp.log(1.0 + jnp.exp(-jnp.abs(x)))


def _silu(x):
    return x * jax.nn.sigmoid(x)


def _layer_norm(r, g, b):
    mu = jnp.mean(r, axis=-1, keepdims=True)
    d = r - mu
    var = jnp.mean(d * d, axis=-1, keepdims=True)
    return d * lax.rsqrt(var + LN_EPS) * g + b


def _head_norm(x, g):
    mu = jnp.mean(x, axis=-1, keepdims=True)
    d = x - mu
    var = jnp.mean(d * d, axis=-1, keepdims=True)
    return d * lax.rsqrt(var + LN_EPS) * g


def _iota(shape, dim):
    return lax.broadcasted_iota(jnp.int32, shape, dim)


def _params(*sem):
    return pltpu.CompilerParams(dimension_semantics=sem, vmem_limit_bytes=VMEM_LIMIT)


def _ada_kernel(c_ref, w_ref, b_ref, o_ref):
    ca = _silu(c_ref[...]).astype(BF16)
    o_ref[0] = _dot(ca, w_ref[0].astype(BF16)) + b_ref[0]


def _ada_mod(c, ada_w, ada_b):
    bsz, d = c.shape
    n = ada_w.shape[-1]
    tn = n // 4
    return pl.pallas_call(
        _ada_kernel,
        out_shape=jax.ShapeDtypeStruct((DEPTH, bsz, n), F32),
        grid=(DEPTH, n // tn),
        in_specs=[
            pl.BlockSpec((bsz, d), lambda l, j: (0, 0)),
            pl.BlockSpec((1, d, tn), lambda l, j: (l, 0, j)),
            pl.BlockSpec((1, 1, tn), lambda l, j: (l, 0, j)),
        ],
        out_specs=pl.BlockSpec((1, bsz, tn), lambda l, j: (l, 0, j)),
        compiler_params=_params("arbitrary", "arbitrary"),
        name="ada_mod",
    )(c, ada_w, ada_b.reshape(DEPTH, 1, n))


def _inproj_kernel(x_ref, mod_ref, w_ref, *o_refs, shift_row, scale_row):
    sh = mod_ref[0, shift_row:shift_row + 1, :]
    sc = mod_ref[0, scale_row:scale_row + 1, :]
    hm = (x_ref[0] * (1.0 + sc) + sh).astype(BF16)
    col = 0
    for o_ref in o_refs:
        n = o_ref.shape[-1]
        o_ref[0] = _dot(hm, w_ref[:, col:col + n]).astype(o_ref.dtype)
        col += n


def _inproj(x, mod, w, outs, tm, shift_row, scale_row, name):
    bsz, s, d = x.shape
    n = w.shape[1]
    assert sum(o[0] for o in outs) == n
    return pl.pallas_call(
        functools.partial(_inproj_kernel, shift_row=shift_row, scale_row=scale_row),
        out_shape=[jax.ShapeDtypeStruct((bsz, s, o[0]), o[1]) for o in outs],
        grid=(bsz, s // tm),
        in_specs=[
            pl.BlockSpec((1, tm, d), lambda b, i: (b, i, 0)),
            pl.BlockSpec((1, 6, d), lambda b, i: (b, 0, 0)),
            pl.BlockSpec((d, n), lambda b, i: (0, 0)),
        ],
        out_specs=[pl.BlockSpec((1, tm, o[0]), lambda b, i: (b, i, 0)) for o in outs],
        compiler_params=_params("parallel", "arbitrary"),
        name=name,
    )(x, mod, w)


def _outproj_kernel(*refs, n_act, gate_row):
    act_refs = refs[:n_act]
    w_refs = refs[n_act:2 * n_act]
    x_ref, mod_ref, g_ref, b_ref, o_ref = refs[2 * n_act:]
    y = None
    for a_ref, w_ref in zip(act_refs, w_refs):
        t = _dot(a_ref[0], w_ref[...])
        y = t if y is None else y + t
    gate = mod_ref[0, gate_row:gate_row + 1, :]
    r = RES_ALPHA * x_ref[0] + (1.0 + gate) * y
    o_ref[0] = _layer_norm(r, g_ref[...], b_ref[...])


def _outproj_ln(acts, ws, x, mod, ln_g, ln_b, tm, gate_row, name):
    bsz, s, d = x.shape
    n_act = len(acts)
    in_specs = [pl.BlockSpec((1, tm, a.shape[-1]), lambda b, i: (b, i, 0)) for a in acts]
    in_specs += [pl.BlockSpec(w.shape, lambda b, i: (0, 0)) for w in ws]
    in_specs += [
        pl.BlockSpec((1, tm, d), lambda b, i: (b, i, 0)),
        pl.BlockSpec((1, 6, d), lambda b, i: (b, 0, 0)),
        pl.BlockSpec((1, d), lambda b, i: (0, 0)),
        pl.BlockSpec((1, d), lambda b, i: (0, 0)),
    ]
    return pl.pallas_call(
        functools.partial(_outproj_kernel, n_act=n_act, gate_row=gate_row),
        out_shape=jax.ShapeDtypeStruct((bsz, s, d), F32),
        grid=(bsz, s // tm),
        in_specs=in_specs,
        out_specs=pl.BlockSpec((1, tm, d), lambda b, i: (b, i, 0)),
        compiler_params=_params("parallel", "arbitrary"),
        name=name,
    )(*acts, *ws, x, mod, ln_g.reshape(1, d), ln_b.reshape(1, d))


def _ffn_kernel(x_ref, mod_ref, w1_ref, w3_ref, w2_ref, g_ref, b_ref, o_ref, acc_ref):
    x = x_ref[0]
    sh = mod_ref[0, 3:4, :]
    sc = mod_ref[0, 4:5, :]
    gate = mod_ref[0, 5:6, :]
    hf = (x * (1.0 + sc) + sh).astype(BF16)
    acc_ref[...] = jnp.zeros_like(acc_ref)

    def body(c, carry):
        h1 = _dot(hf, w1_ref[c])
        h3 = _dot(hf, w3_ref[c])
        gm = (_silu(h1) * h3).astype(BF16)
        acc_ref[...] += _dot(gm, w2_ref[c])
        return carry

    lax.fori_loop(0, w1_ref.shape[0], body, 0)
    r = RES_ALPHA * x + (1.0 + gate) * acc_ref[...]
    o_ref[0] = _layer_norm(r, g_ref[...], b_ref[...])


def _ffn(x, mod, w1, w3, w2, ln_g, ln_b, tm):
    bsz, s, d = x.shape
    nc = D_FF // FF_CHUNK
    w1r = w1.astype(BF16).reshape(d, nc, FF_CHUNK).transpose(1, 0, 2)
    w3r = w3.astype(BF16).reshape(d, nc, FF_CHUNK).transpose(1, 0, 2)
    w2r = w2.astype(BF16).reshape(nc, FF_CHUNK, d)
    return pl.pallas_call(
        _ffn_kernel,
        out_shape=jax.ShapeDtypeStruct((bsz, s, d), F32),
        grid=(bsz, s // tm),
        in_specs=[
            pl.BlockSpec((1, tm, d), lambda b, i: (b, i, 0)),
            pl.BlockSpec((1, 6, d), lambda b, i: (b, 0, 0)),
            pl.BlockSpec((nc, d, FF_CHUNK), lambda b, i: (0, 0, 0)),
            pl.BlockSpec((nc, d, FF_CHUNK), lambda b, i: (0, 0, 0)),
            pl.BlockSpec((nc, FF_CHUNK, d), lambda b, i: (0, 0, 0)),
            pl.BlockSpec((1, d), lambda b, i: (0, 0)),
            pl.BlockSpec((1, d), lambda b, i: (0, 0)),
        ],
        out_specs=pl.BlockSpec((1, tm, d), lambda b, i: (b, i, 0)),
        scratch_shapes=[pltpu.VMEM((tm, d), F32)],
        compiler_params=_params("parallel", "arbitrary"),
        name="ffn",
    )(x, mod, w1r, w3r, w2r, ln_g.reshape(1, d), ln_b.reshape(1, d))


def _mixer0_kernel(p_ref, wa_ref, ba_ref, gg_ref, cw_ref, cb_ref, gb_ref, mg_ref, o_ref,
                   st_ref, cst_ref, m_ref, xc_ref):
    L = MIX_TILE
    i = pl.program_id(1)

    @pl.when(i == 0)
    def _():
        st_ref[...] = jnp.zeros_like(st_ref)
        cst_ref[...] = jnp.zeros_like(cst_ref)
        m_ref[...] = jnp.zeros_like(m_ref)
        xc_ref[0:8, :] = jnp.zeros((8, 2 * ML_W), F32)

    lane = _iota((1, LANES), 1)
    g_raw = p_ref[0, :, P0_G:P0_G + LANES]

    u = _dot(g_raw.astype(BF16), wa_ref[...]) + ba_ref[...]
    la = _log_sigmoid(u) * (1.0 / GLA_TAU)
    row = _iota((L, L), 0)
    col = _iota((L, L), 1)
    same_chunk = (row >> _CHUNK_SHIFT) == (col >> _CHUNK_SHIFT)
    tri_blk = jnp.where(same_chunk & (col <= row), 1.0, 0.0).astype(BF16)
    bcs = _dot_split(tri_blk, la, 3)
    q_in = p_ref[0, :, P0_QA:P0_QA + GLA_QK] * (GLA_DK ** -0.5) * jnp.exp(bcs)
    k_raw = p_ref[0, :, P0_KA:P0_KA + GLA_QK]
    k_in = k_raw * jnp.exp(-bcs)
    lane_qk = _iota((1, GLA_QK), 1)
    head_masks = [(lane_qk >> _CHUNK_SHIFT) == h for h in range(GLA_HEADS)]
    r4 = _iota((GLA_HEADS * GLA_CHUNK, GLA_CHUNK), 0)
    c4 = _iota((GLA_HEADS * GLA_CHUNK, GLA_CHUNK), 1)
    tril4 = c4 <= (r4 & (GLA_CHUNK - 1))
    oa_chunks = []
    for c in range(L // GLA_CHUNK):
        r0, r1 = c * GLA_CHUNK, (c + 1) * GLA_CHUNK
        b_c = bcs[r0:r1]
        bl = b_c[GLA_CHUNK - 1:GLA_CHUNK, :]
        q_c = q_in[r0:r1]
        k_c = k_in[r0:r1].astype(BF16)
        k_end = k_raw[r0:r1] * jnp.exp(bl - b_c)
        v_c = p_ref[0, r0:r1, P0_VA:P0_VA + GLA_V]
        q_exp = jnp.concatenate([jnp.where(hm, q_c, 0.0) for hm in head_masks], axis=0).astype(BF16)
        k_exp = jnp.concatenate([jnp.where(hm, k_end, 0.0) for hm in head_masks], axis=0).astype(BF16)
        sc = jnp.where(tril4, _dot_nt(q_exp, k_c), 0.0)
        intra = _dot(sc.astype(BF16), v_c.astype(BF16))
        inter = _dot_nt(q_exp, st_ref[...].astype(BF16))
        o_heads = []
        for h in range(GLA_HEADS):
            h0, h1 = h * GLA_CHUNK, (h + 1) * GLA_CHUNK
            o_heads.append(intra[h0:h1, h * GLA_DV:(h + 1) * GLA_DV] + inter[h0:h1])
        oa_chunks.append(o_heads)
        v_cat = jnp.concatenate([v_c[:, h * GLA_DV:(h + 1) * GLA_DV] for h in range(GLA_HEADS)], axis=0)
        st_ref[...] = st_ref[...] * jnp.exp(bl) + _dot_tn(v_cat.astype(BF16), k_exp)
    for h in range(GLA_HEADS):
        o_h = jnp.concatenate([oc[h] for oc in oa_chunks], axis=0)
        cs = slice(h * GLA_DV, (h + 1) * GLA_DV)
        ra = p_ref[0, :, P0_RA + h * GLA_DV:P0_RA + (h + 1) * GLA_DV]
        o_ref[0, :, cs] = (_head_norm(o_h, gg_ref[:, cs]) * _silu(ra)).astype(o_ref.dtype)

    xc_ref[8:8 + L, :] = p_ref[0, :, P0_QKB:P0_QKB + 2 * ML_W]
    conv = cb_ref[...]
    for kk in range(MLSTM_CONV):
        conv = conv + cw_ref[kk:kk + 1, :] * xc_ref[8 - (MLSTM_CONV - 1) + kk:8 - (MLSTM_CONV - 1) + kk + L, :]
    xc_ref[0:8, :] = xc_ref[L:L + 8, :]
    qk_b = _silu(conv)

    gb = jnp.where(lane < G_A, g_raw + gb_ref[...], 0.0)
    ipre = gb
    logf = _log_sigmoid(pltpu.roll(gb, LANES - G_F, axis=1))
    logf = jnp.where(lane < MLSTM_HEADS, logf, 0.0)
    tri = jnp.where(col <= row, 1.0, 0.0).astype(BF16)
    bcum = _dot_split(tri, logf, 3)
    blast = bcum[L - 1:L, :]
    wend = blast - bcum + ipre
    m_prev = m_ref[...]
    m_new = jnp.maximum(blast + m_prev, jnp.max(wend, axis=0, keepdims=True))
    scl = jnp.exp(blast + m_prev - m_new)
    wj = jnp.exp(wend - m_new)
    inter_log = bcum + m_prev
    rows_src = jnp.where(lane < MLSTM_HEADS, ipre, pltpu.roll(bcum, MLSTM_HEADS, axis=1))
    rows_t = rows_src.T
    causal = col <= row
    e0 = jnp.where(_iota((L, MLSTM_DH), 1) == 0, 1.0, 0.0)
    for h in range(MLSTM_HEADS):
        hs = slice(h * MLSTM_DH, (h + 1) * MLSTM_DH)
        b_col = bcum[:, h:h + 1]
        ip_row = rows_t[h:h + 1, :]
        b_row = rows_t[MLSTM_HEADS + h:MLSTM_HEADS + h + 1, :]
        dlog = jnp.where(causal, b_col - b_row + ip_row, NEG_INF)
        il = inter_log[:, h:h + 1]
        m_i = jnp.maximum(il, jnp.max(dlog, axis=1, keepdims=True))
        w_intra = jnp.exp(dlog - m_i)
        s_inter = jnp.exp(il - m_i)
        q_h = qk_b[:, h * MLSTM_DH:(h + 1) * MLSTM_DH].astype(BF16)
        k_h = (qk_b[:, ML_W + h * MLSTM_DH:ML_W + (h + 1) * MLSTM_DH] * (MLSTM_DH ** -0.5)).astype(BF16)
        v_h = p_ref[0, :, P0_VB + h * MLSTM_DH:P0_VB + (h + 1) * MLSTM_DH]
        v_aug = jnp.concatenate([v_h, e0], axis=1)
        a = (w_intra * _dot_nt(q_h, k_h)).astype(BF16)
        c_prev = cst_ref[h]
        tot = _dot(a, v_aug.astype(BF16)) + s_inter * _dot_nt(q_h, c_prev.astype(BF16))
        num = tot[:, :MLSTM_DH]
        den = tot[:, MLSTM_DH:MLSTM_DH + 1]
        hid = num / jnp.maximum(jnp.abs(den), jnp.exp(-m_i))
        w_aug = (v_aug * wj[:, h:h + 1]).astype(BF16)
        cst_ref[h] = scl[:, h:h + 1] * c_prev + _dot_tn(w_aug, k_h)
        ob = p_ref[0, :, P0_OB + h * MLSTM_DH:P0_OB + (h + 1) * MLSTM_DH]
        o_ref[0, :, GLA_V + h * MLSTM_DH:GLA_V + (h + 1) * MLSTM_DH] = (
            jax.nn.sigmoid(ob) * _head_norm(hid, mg_ref[:, hs])).astype(o_ref.dtype)
    m_ref[...] = m_new


def _mixer0(proj, wa_up, ba, gla_g, conv_w, conv_b, b_i, b_f, ml_g):
    bsz, s, n = proj.shape
    L = MIX_TILE
    wa_pad = jnp.zeros((LANES, GLA_QK), F32).at[G_A:G_A + GLA_GATE_RANK].set(wa_up).astype(BF16)
    gbias = jnp.zeros((1, LANES), F32).at[0, G_I:G_I + MLSTM_HEADS].set(b_i).at[0, G_F:G_F + MLSTM_HEADS].set(b_f)
    full = lambda shape: pl.BlockSpec(shape, lambda b, i: (0,) * len(shape))
    return pl.pallas_call(
        _mixer0_kernel,
        out_shape=jax.ShapeDtypeStruct((bsz, s, GLA_V + ML_W), BF16),
        grid=(bsz, s // L),
        in_specs=[
            pl.BlockSpec((1, L, n), lambda b, i: (b, i, 0)),
            full((LANES, GLA_QK)),
            full((1, GLA_QK)),
            full((1, GLA_V)),
            full((MLSTM_CONV, 2 * ML_W)),
            full((1, 2 * ML_W)),
            full((1, LANES)),
            full((1, ML_W)),
        ],
        out_specs=pl.BlockSpec((1, L, GLA_V + ML_W), lambda b, i: (b, i, 0)),
        scratch_shapes=[
            pltpu.VMEM((GLA_DV, GLA_QK), F32),
            pltpu.VMEM((MLSTM_HEADS, 2 * MLSTM_DH, MLSTM_DH), F32),
            pltpu.VMEM((1, LANES), F32),
            pltpu.VMEM((8 + L, 2 * ML_W), F32),
        ],
        compiler_params=_params("parallel", "arbitrary"),
        name="mixer0",
    )(proj, wa_pad, ba.reshape(1, GLA_QK), gla_g.reshape(1, GLA_V), conv_w, conv_b.reshape(1, 2 * ML_W),
      gbias, ml_g.reshape(1, ML_W))


def _sb_kernel(q_ref, k_ref, v_ref, o_ref):
    T = SB_T
    i = pl.program_id(2)
    lane = _iota((1, LANES), 1)
    q = q_ref[0].astype(F32) * (SB_DH ** -0.5)
    row = _iota((T, T), 0)
    col = _iota((T, T), 1)
    r2 = _iota((T, 2 * T), 0)
    c2 = _iota((T, 2 * T), 1)
    m2 = jnp.where(c2 >= T, 1.0, jnp.where(r2 > c2, -1.0, 0.0)).astype(BF16)
    strict = col < row

    def tile(qm, r0, carry, acc, mask):
        kt = k_ref[0, pl.ds(r0, T), :]
        vt = v_ref[0, pl.ds(r0, T), :]
        z = _dot_nt(qm, kt)
        sp = jnp.maximum(z, 0.0) + jnp.log(1.0 + jnp.exp(-jnp.abs(z)))
        if mask is not None:
            sp = jnp.where(mask, sp, 0.0)
        hi = sp.astype(BF16)
        lo = (sp - hi.astype(F32)).astype(BF16)
        res = _dot(hi, m2) + _dot(lo, m2)
        att = jnp.exp(z - sp + res[:, :T] - carry)
        if mask is not None:
            att = jnp.where(mask, att, 0.0)
        acc = acc + _dot(att.astype(BF16), vt)
        return carry + res[:, T:], acc

    outs = []
    for h in range(2):
        qm = jnp.where((lane >> _HEAD_SHIFT) == h, q, 0.0).astype(BF16)
        zero = jnp.zeros((T, LANES), F32)
        carry, acc = tile(qm, pl.multiple_of(i * T, T), zero, zero, strict)

        def body(jj, ca, qm=qm):
            r0 = pl.multiple_of((i - jj) * T, T)
            return tile(qm, r0, ca[0], ca[1], None)

        carry, acc = lax.fori_loop(1, i + 1, body, (carry, acc))
        outs.append(acc)
    o_ref[0] = jnp.where((lane >> _HEAD_SHIFT) == 0, outs[0], outs[1]).astype(o_ref.dtype)


def _stick_breaking(pc):
    bsz, s, _ = pc.shape
    npair = SB_W // LANES
    return pl.pallas_call(
        _sb_kernel,
        out_shape=jax.ShapeDtypeStruct((bsz, s, SB_W), BF16),
        grid=(bsz, npair, s // SB_T),
        in_specs=[
            pl.BlockSpec((1, SB_T, LANES), lambda b, p, i: (b, i, p)),
            pl.BlockSpec((1, s, LANES), lambda b, p, i: (b, 0, npair + p)),
            pl.BlockSpec((1, s, LANES), lambda b, p, i: (b, 0, 2 * npair + p)),
        ],
        out_specs=pl.BlockSpec((1, SB_T, LANES), lambda b, p, i: (b, i, p)),
        compiler_params=_params("parallel", "parallel", "arbitrary"),
        name="stick_breaking",
    )(pc, pc, pc)


def _t5_bucket(dist):
    max_exact = N_REL_BUCKETS // 2
    dd = np.maximum(dist, 1).astype(np.float64)
    large = max_exact + (np.log(dd / max_exact) / np.log(REL_MAX_DIST / max_exact)
                         * (N_REL_BUCKETS - max_exact)).astype(np.int32)
    large = np.minimum(large, N_REL_BUCKETS - 1)
    return np.where(dist < max_exact, dist, large).astype(np.int32)


def _dil_buckets():
    qi = np.arange(DIL_BLK)[:, None]
    kj = np.arange(2 * DIL_BLK)[None, :]
    delta = qi - kj + DIL_BLK
    in_win = (delta >= 0) & (delta <= DIL_BLK)
    tabs = []
    for window, dil in DIL_PAIRS:
        assert window // dil == DIL_BLK
        bucket = _t5_bucket(np.clip(delta, 0, None) * dil)
        tabs.append(np.where(in_win, bucket, -1).astype(np.int32))
    return np.stack(tabs, 0)


def _dil_kernel(q_ref, k_ref, v_ref, bkt_ref, tab_ref, o_ref, m_sc, l_sc, acc_sc):
    s_len = q_ref.shape[1]
    hp = pl.program_id(1)
    g = pl.program_id(2)
    blk = DIL_BLK
    lane = _iota((1, LANES), 1)
    lane_head = lane >> _HEAD_SHIFT

    @pl.when(g == 0)
    def _():
        m_sc[...] = jnp.full_like(m_sc, NEG_INF)
        l_sc[...] = jnp.zeros_like(l_sc)
        acc_sc[...] = jnp.zeros_like(acc_sc)

    kcol = _iota((blk, 2 * blk), 1)

    def group(gi, dil):
        nb = s_len // (blk * dil)
        bkt = bkt_ref[gi]
        biases = []
        for hl in range(2):
            head = gi * DIL_HEADS_PER_GROUP + hp * 2 + hl
            bias = jnp.full((blk, 2 * blk), NEG_INF, F32)
            for bk in range(N_REL_BUCKETS):
                bias = jnp.where(bkt == bk, tab_ref[bk, head], bias)
            biases.append(bias)

        def block(t, carry):
            r = t // nb
            n = t % nb
            q_start = r + dil * blk * n
            p_start = r + dil * blk * jnp.maximum(n - 1, 0)
            rows_q = pl.ds(q_start, blk, stride=dil) if dil > 1 else pl.ds(q_start, blk)
            rows_p = pl.ds(p_start, blk, stride=dil) if dil > 1 else pl.ds(p_start, blk)
            qb = q_ref[0, rows_q, :] * (DIL_DH ** -0.5)
            kk = jnp.concatenate([k_ref[0, rows_p, :], k_ref[0, rows_q, :]], axis=0).astype(BF16)
            vv = jnp.concatenate([v_ref[0, rows_p, :], v_ref[0, rows_q, :]], axis=0).astype(BF16)
            first = jnp.logical_and(n == 0, kcol < blk)
            m_old = m_sc[rows_q, :]
            l_old = l_sc[rows_q, :]
            a_old = acc_sc[rows_q, :]
            m_new = m_old
            logit_heads, m_cols = [], []
            for hl in range(2):
                qm = jnp.where(lane_head == hl, qb, 0.0).astype(BF16)
                logits = _dot_nt(qm, kk) + biases[hl]
                logits = jnp.where(first, NEG_INF, logits)
                m_col = jnp.maximum(m_old[:, hl * DIL_DH:hl * DIL_DH + 1], jnp.max(logits, axis=1, keepdims=True))
                m_new = jnp.where(lane_head == hl, m_col, m_new)
                logit_heads.append(logits)
                m_cols.append(m_col)
            alpha = jnp.exp(m_old - m_new)
            l_new = alpha * l_old
            a_new = alpha * a_old
            for hl in range(2):
                sel = lane_head == hl
                p = jnp.exp(logit_heads[hl] - m_cols[hl])
                l_new = l_new + jnp.where(sel, jnp.sum(p, axis=1, keepdims=True), 0.0)
                a_new = a_new + jnp.where(sel, _dot(p.astype(BF16), vv), 0.0)
            m_sc[rows_q, :] = m_new
            l_sc[rows_q, :] = l_new
            acc_sc[rows_q, :] = a_new
            return carry

        lax.fori_loop(0, s_len // blk, block, 0)

    for gi, (_, dil) in enumerate(DIL_PAIRS):
        pl.when(g == gi)(functools.partial(group, gi, dil))

    @pl.when(g == len(DIL_PAIRS) - 1)
    def _():
        o_ref[0] = (acc_sc[...] / l_sc[...]).astype(o_ref.dtype)


def _dilated(pd, rel_bias):
    bsz, s, _ = pd.shape
    ng = len(DIL_PAIRS)
    gw = DIL_HEADS_PER_GROUP * DIL_DH // LANES
    nq = DIL_W // LANES
    bkt = jnp.asarray(_dil_buckets())
    blk_spec = lambda base: pl.BlockSpec((1, s, LANES), lambda b, p, g: (b, 0, base + g * gw + p))
    return pl.pallas_call(
        _dil_kernel,
        out_shape=jax.ShapeDtypeStruct((bsz, s, DIL_OUT), BF16),
        grid=(bsz, gw, ng),
        in_specs=[
            blk_spec(0),
            blk_spec(nq),
            blk_spec(2 * nq),
            pl.BlockSpec((ng, DIL_BLK, 2 * DIL_BLK), lambda b, p, g: (0, 0, 0)),
            pl.BlockSpec(memory_space=pltpu.SMEM),
        ],
        out_specs=pl.BlockSpec((1, s, LANES), lambda b, p, g: (b, 0, p)),
        scratch_shapes=[pltpu.VMEM((s, LANES), F32)] * 3,
        compiler_params=_params("parallel", "parallel", "arbitrary"),
        name="dilated",
    )(pd, pd, pd, bkt, rel_bias)


def _ab_weight(w):
    o = np.cumsum((0,) + (GLA_QK, GLA_QK, GLA_V, GLA_V, GLA_GATE_RANK, 2 * ML_W, ML_W, MLSTM_HEADS, MLSTM_HEADS, ML_W))
    qa, ka, va, ra, aa, qkb, vb, ib, fb, ob = [w[:, o[j]:o[j + 1]] for j in range(10)]
    pad = jnp.zeros((w.shape[0], LANES - 2 * MLSTM_HEADS - GLA_GATE_RANK), w.dtype)
    return jnp.concatenate([qa, ka, va, ra, qkb, vb, ob, ib, fb, aa, pad], axis=1).astype(BF16)


def kernel(x, c, ada_w, ada_b, ln_g, ln_b, ab_w_in, gla_wa_up, gla_ba, gla_norm_g, ml_conv_w, ml_conv_b,
           ml_b_i, ml_b_f, ml_norm_g, ab_w_out, cd_w_in, rel_bias, cd_w_out, ffn_w1, ffn_w3, ffn_w2):
    bsz, s, d = x.shape
    mod_all = _ada_mod(c, ada_w, ada_b).reshape(DEPTH, bsz, 6, d)
    tm = 512
    for layer in range(DEPTH):
        mod = mod_all[layer]
        j = layer // 2
        if layer % 2 == 0:
            (proj,) = _inproj(x, mod, _ab_weight(ab_w_in[j]), [(P0_N, F32)], tm, 0, 1, "inproj0")
            cat = _mixer0(proj, gla_wa_up[j], gla_ba[j], gla_norm_g[j], ml_conv_w[j], ml_conv_b[j],
                          ml_b_i[j], ml_b_f[j], ml_norm_g[j])
            x = _outproj_ln([cat], [ab_w_out[j].astype(BF16)], x, mod, ln_g[layer, 0], ln_b[layer, 0],
                            tm, 2, "outproj0")
        else:
            pc, pd = _inproj(x, mod, cd_w_in[j].astype(BF16), [(3 * SB_W, BF16), (3 * DIL_W, F32)],
                             tm, 0, 1, "inproj1")
            oc = _stick_breaking(pc)
            od = _dilated(pd, rel_bias)
            w_out = cd_w_out[j].astype(BF16)
            x = _outproj_ln([oc, od], [w_out[:SB_W], w_out[SB_W:]], x, mod, ln_g[layer, 0], ln_b[layer, 0],
                            tm, 2, "outproj1")
        x = _ffn(x, mod, ffn_w1[layer], ffn_w3[layer], ffn_w2[layer], ln_g[layer, 1], ln_b[layer, 1], tm)
    return x
```

```python
import functools

import numpy as np
import jax
import jax.numpy as jnp
from jax import lax
from jax.experimental import pallas as pl
from jax.experimental.pallas import tpu as pltpu

F32 = jnp.float32
BF16 = jnp.bfloat16

D_MODEL = 1024
DEPTH = 2
GLA_HEADS = 4
GLA_DK = 64
GLA_DV = 128
GLA_GATE_RANK = 16
GLA_TAU = 16.0
GLA_CHUNK = 64
MLSTM_HEADS = 4
MLSTM_DH = 128
MLSTM_CONV = 4
SB_HEADS = 8
SB_DH = 64
DIL_PAIRS = ((128, 1), (512, 4), (2048, 16))
DIL_HEADS_PER_GROUP = 4
DIL_DH = 64
DIL_BLK = 128
N_REL_BUCKETS = 32
REL_MAX_DIST = 2048
D_FF = ((8 * D_MODEL + 3 * 256 - 1) // (3 * 256)) * 256
LN_EPS = 1e-5
RES_ALPHA = (2 * DEPTH) ** 0.25
NEG_INF = -1e30

GLA_QK = GLA_HEADS * GLA_DK
GLA_V = GLA_HEADS * GLA_DV
ML_W = MLSTM_HEADS * MLSTM_DH
SB_W = SB_HEADS * SB_DH
DIL_W = len(DIL_PAIRS) * DIL_HEADS_PER_GROUP * DIL_DH
DIL_OUT = DIL_HEADS_PER_GROUP * DIL_DH

LANES = 128
VMEM_LIMIT = 56 * 1024 * 1024

P0_QA = 0
P0_KA = P0_QA + GLA_QK
P0_VA = P0_KA + GLA_QK
P0_RA = P0_VA + GLA_V
P0_QKB = P0_RA + GLA_V
P0_VB = P0_QKB + 2 * ML_W
P0_OB = P0_VB + ML_W
P0_G = P0_OB + ML_W
P0_N = P0_G + LANES
G_I = 0
G_F = MLSTM_HEADS
G_A = 2 * MLSTM_HEADS

MIX_TILE = 256
FF_CHUNK = 256
SB_TQ = 512
SB_TK = 256
LOG2E = 1.4426950408889634
_CHUNK_SHIFT = 6
_HEAD_SHIFT = 6
assert GLA_CHUNK == GLA_DK == 1 << _CHUNK_SHIFT and SB_DH == DIL_DH == 1 << _HEAD_SHIFT


def _dot(a, b):
    return jnp.dot(a, b, preferred_element_type=F32)


def _dot_nt(a, b):
    return lax.dot_general(a, b, (((1,), (1,)), ((), ())), preferred_element_type=F32)


def _dot_tn(a, b):
    return lax.dot_general(a, b, (((0,), (0,)), ((), ())), preferred_element_type=F32)


def _dot_split(t, x, terms):
    acc = None
    rem = x
    for i in range(terms):
        part = rem.astype(BF16)
        d = _dot(t, part)
        acc = d if acc is None else acc + d
        if i + 1 < terms:
            rem = rem - part.astype(F32)
    return acc


def _log_sigmoid(x):
    return jnp.minimum(x, 0.0) - jnp.log(1.0 + jnp.exp(-jnp.abs(x)))


def _silu(x):
    return x * jax.nn.sigmoid(x)


def _layer_norm(r, g, b):
    mu = jnp.mean(r, axis=-1, keepdims=True)
    d = r - mu
    var = jnp.mean(d * d, axis=-1, keepdims=True)
    return d * lax.rsqrt(var + LN_EPS) * g + b


def _head_norm(x, g):
    mu = jnp.mean(x, axis=-1, keepdims=True)
    d = x - mu
    var = jnp.mean(d * d, axis=-1, keepdims=True)
    return d * lax.rsqrt(var + LN_EPS) * g


def _iota(shape, dim):
    return lax.broadcasted_iota(jnp.int32, shape, dim)


def _params(*sem):
    return pltpu.CompilerParams(dimension_semantics=sem, vmem_limit_bytes=VMEM_LIMIT)


def _ada_kernel(c_ref, w_ref, b_ref, o_ref):
    ca = _silu(c_ref[...]).astype(BF16)
    o_ref[0] = _dot(ca, w_ref[0].astype(BF16)) + b_ref[0]


def _ada_mod(c, ada_w, ada_b):
    bsz, d = c.shape
    n = ada_w.shape[-1]
    tn = n // 4
    return pl.pallas_call(
        _ada_kernel,
        out_shape=jax.ShapeDtypeStruct((DEPTH, bsz, n), F32),
        grid=(DEPTH, n // tn),
        in_specs=[
            pl.BlockSpec((bsz, d), lambda l, j: (0, 0)),
            pl.BlockSpec((1, d, tn), lambda l, j: (l, 0, j)),
            pl.BlockSpec((1, 1, tn), lambda l, j: (l, 0, j)),
        ],
        out_specs=pl.BlockSpec((1, bsz, tn), lambda l, j: (l, 0, j)),
        compiler_params=_params("arbitrary", "arbitrary"),
        name="ada_mod",
    )(c, ada_w, ada_b.reshape(DEPTH, 1, n))


def _inproj_kernel(x_ref, mod_ref, w_ref, *o_refs, shift_row, scale_row):
    sh = mod_ref[0, shift_row:shift_row + 1, :]
    sc = mod_ref[0, scale_row:scale_row + 1, :]
    hm = (x_ref[0] * (1.0 + sc) + sh).astype(BF16)
    col = 0
    for o_ref in o_refs:
        n = o_ref.shape[-1]
        o_ref[0] = _dot(hm, w_ref[:, col:col + n]).astype(o_ref.dtype)
        col += n


def _inproj(x, mod, w, outs, tm, shift_row, scale_row, name):
    bsz, s, d = x.shape
    n = w.shape[1]
    assert sum(o[0] for o in outs) == n
    return pl.pallas_call(
        functools.partial(_inproj_kernel, shift_row=shift_row, scale_row=scale_row),
        out_shape=[jax.ShapeDtypeStruct((bsz, s, o[0]), o[1]) for o in outs],
        grid=(bsz, s // tm),
        in_specs=[
            pl.BlockSpec((1, tm, d), lambda b, i: (b, i, 0)),
            pl.BlockSpec((1, 6, d), lambda b, i: (b, 0, 0)),
            pl.BlockSpec((d, n), lambda b, i: (0, 0)),
        ],
        out_specs=[pl.BlockSpec((1, tm, o[0]), lambda b, i: (b, i, 0)) for o in outs],
        compiler_params=_params("parallel", "arbitrary"),
        name=name,
    )(x, mod, w)


def _outproj_kernel(*refs, n_act, gate_row):
    act_refs = refs[:n_act]
    w_refs = refs[n_act:2 * n_act]
    x_ref, mod_ref, g_ref, b_ref, o_ref = refs[2 * n_act:]
    y = None
    for a_ref, w_ref in zip(act_refs, w_refs):
        t = _dot(a_ref[0], w_ref[...])
        y = t if y is None else y + t
    gate = mod_ref[0, gate_row:gate_row + 1, :]
    r = RES_ALPHA * x_ref[0] + (1.0 + gate) * y
    o_ref[0] = _layer_norm(r, g_ref[...], b_ref[...])


def _outproj_ln(acts, ws, x, mod, ln_g, ln_b, tm, gate_row, name):
    bsz, s, d = x.shape
    n_act = len(acts)
    in_specs = [pl.BlockSpec((1, tm, a.shape[-1]), lambda b, i: (b, i, 0)) for a in acts]
    in_specs += [pl.BlockSpec(w.shape, lambda b, i: (0, 0)) for w in ws]
    in_specs += [
        pl.BlockSpec((1, tm, d), lambda b, i: (b, i, 0)),
        pl.BlockSpec((1, 6, d), lambda b, i: (b, 0, 0)),
        pl.BlockSpec((1, d), lambda b, i: (0, 0)),
        pl.BlockSpec((1, d), lambda b, i: (0, 0)),
    ]
    return pl.pallas_call(
        functools.partial(_outproj_kernel, n_act=n_act, gate_row=gate_row),
        out_shape=jax.ShapeDtypeStruct((bsz, s, d), F32),
        grid=(bsz, s // tm),
        in_specs=in_specs,
        out_specs=pl.BlockSpec((1, tm, d), lambda b, i: (b, i, 0)),
        compiler_params=_params("parallel", "arbitrary"),
        name=name,
    )(*acts, *ws, x, mod, ln_g.reshape(1, d), ln_b.reshape(1, d))


def _ffn_kernel(x_ref, mod_ref, w1_ref, w3_ref, w2_ref, g_ref, b_ref, o_ref, acc_ref):
    x = x_ref[0]
    sh = mod_ref[0, 3:4, :]
    sc = mod_ref[0, 4:5, :]
    gate = mod_ref[0, 5:6, :]
    hf = (x * (1.0 + sc) + sh).astype(BF16)
    acc_ref[...] = jnp.zeros_like(acc_ref)

    def body(c, carry):
        h1 = _dot(hf, w1_ref[c])
        h3 = _dot(hf, w3_ref[c])
        gm = (_silu(h1) * h3).astype(BF16)
        acc_ref[...] += _dot(gm, w2_ref[c])
        return carry

    lax.fori_loop(0, w1_ref.shape[0], body, 0)
    r = RES_ALPHA * x + (1.0 + gate) * acc_ref[...]
    o_ref[0] = _layer_norm(r, g_ref[...], b_ref[...])


def _ffn(x, mod, w1, w3, w2, ln_g, ln_b, tm):
    bsz, s, d = x.shape
    nc = D_FF // FF_CHUNK
    w1r = w1.astype(BF16).reshape(d, nc, FF_CHUNK).transpose(1, 0, 2)
    w3r = w3.astype(BF16).reshape(d, nc, FF_CHUNK).transpose(1, 0, 2)
    w2r = w2.astype(BF16).reshape(nc, FF_CHUNK, d)
    return pl.pallas_call(
        _ffn_kernel,
        out_shape=jax.ShapeDtypeStruct((bsz, s, d), F32),
        grid=(bsz, s // tm),
        in_specs=[
            pl.BlockSpec((1, tm, d), lambda b, i: (b, i, 0)),
            pl.BlockSpec((1, 6, d), lambda b, i: (b, 0, 0)),
            pl.BlockSpec((nc, d, FF_CHUNK), lambda b, i: (0, 0, 0)),
            pl.BlockSpec((nc, d, FF_CHUNK), lambda b, i: (0, 0, 0)),
            pl.BlockSpec((nc, FF_CHUNK, d), lambda b, i: (0, 0, 0)),
            pl.BlockSpec((1, d), lambda b, i: (0, 0)),
            pl.BlockSpec((1, d), lambda b, i: (0, 0)),
        ],
        out_specs=pl.BlockSpec((1, tm, d), lambda b, i: (b, i, 0)),
        scratch_shapes=[pltpu.VMEM((tm, d), F32)],
        compiler_params=_params("parallel", "arbitrary"),
        name="ffn",
    )(x, mod, w1r, w3r, w2r, ln_g.reshape(1, d), ln_b.reshape(1, d))


def _mixer0_kernel(p_ref, wa_ref, ba_ref, gg_ref, cw_ref, cb_ref, gb_ref, mg_ref, o_ref,
                   st_ref, cst_ref, m_ref, xc_ref):
    L = MIX_TILE
    i = pl.program_id(1)

    @pl.when(i == 0)
    def _():
        st_ref[...] = jnp.zeros_like(st_ref)
        cst_ref[...] = jnp.zeros_like(cst_ref)
        m_ref[...] = jnp.zeros_like(m_ref)
        xc_ref[0:8, :] = jnp.zeros((8, 2 * ML_W), F32)

    lane = _iota((1, LANES), 1)
    g_raw = p_ref[0, :, P0_G:P0_G + LANES]

    u = _dot(g_raw.astype(BF16), wa_ref[...]) + ba_ref[...]
    la = _log_sigmoid(u) * (1.0 / GLA_TAU)
    row = _iota((L, L), 0)
    col = _iota((L, L), 1)
    same_chunk = (row >> _CHUNK_SHIFT) == (col >> _CHUNK_SHIFT)
    tri_blk = jnp.where(same_chunk & (col <= row), 1.0, 0.0).astype(BF16)
    bcs = _dot_split(tri_blk, la, 3)
    q_in = p_ref[0, :, P0_QA:P0_QA + GLA_QK] * (GLA_DK ** -0.5) * jnp.exp(bcs)
    k_raw = p_ref[0, :, P0_KA:P0_KA + GLA_QK]
    k_in = k_raw * jnp.exp(-bcs)
    lane_qk = _iota((1, GLA_QK), 1)
    head_masks = [(lane_qk >> _CHUNK_SHIFT) == h for h in range(GLA_HEADS)]
    r4 = _iota((GLA_HEADS * GLA_CHUNK, GLA_CHUNK), 0)
    c4 = _iota((GLA_HEADS * GLA_CHUNK, GLA_CHUNK), 1)
    tril4 = c4 <= (r4 & (GLA_CHUNK - 1))
    oa_chunks = []
    for c in range(L // GLA_CHUNK):
        r0, r1 = c * GLA_CHUNK, (c + 1) * GLA_CHUNK
        b_c = bcs[r0:r1]
        bl = b_c[GLA_CHUNK - 1:GLA_CHUNK, :]
        q_c = q_in[r0:r1]
        k_c = k_in[r0:r1].astype(BF16)
        k_end = k_raw[r0:r1] * jnp.exp(bl - b_c)
        v_c = p_ref[0, r0:r1, P0_VA:P0_VA + GLA_V]
        q_exp = jnp.concatenate([jnp.where(hm, q_c, 0.0) for hm in head_masks], axis=0).astype(BF16)
        k_exp = jnp.concatenate([jnp.where(hm, k_end, 0.0) for hm in head_masks], axis=0).astype(BF16)
        sc = jnp.where(tril4, _dot_nt(q_exp, k_c), 0.0)
        intra = _dot(sc.astype(BF16), v_c.astype(BF16))
        inter = _dot_nt(q_exp, st_ref[...].astype(BF16))
        o_heads = []
        for h in range(GLA_HEADS):
            h0, h1 = h * GLA_CHUNK, (h + 1) * GLA_CHUNK
            o_heads.append(intra[h0:h1, h * GLA_DV:(h + 1) * GLA_DV] + inter[h0:h1])
        oa_chunks.append(o_heads)
        v_cat = jnp.concatenate([v_c[:, h * GLA_DV:(h + 1) * GLA_DV] for h in range(GLA_HEADS)], axis=0)
        st_ref[...] = st_ref[...] * jnp.exp(bl) + _dot_tn(v_cat.astype(BF16), k_exp)
    for h in range(GLA_HEADS):
        o_h = jnp.concatenate([oc[h] for oc in oa_chunks], axis=0)
        cs = slice(h * GLA_DV, (h + 1) * GLA_DV)
        ra = p_ref[0, :, P0_RA + h * GLA_DV:P0_RA + (h + 1) * GLA_DV]
        o_ref[0, :, cs] = (_head_norm(o_h, gg_ref[:, cs]) * _silu(ra)).astype(o_ref.dtype)

    xc_ref[8:8 + L, :] = p_ref[0, :, P0_QKB:P0_QKB + 2 * ML_W]
    conv = cb_ref[...]
    for kk in range(MLSTM_CONV):
        conv = conv + cw_ref[kk:kk + 1, :] * xc_ref[8 - (MLSTM_CONV - 1) + kk:8 - (MLSTM_CONV - 1) + kk + L, :]
    xc_ref[0:8, :] = xc_ref[L:L + 8, :]
    qk_b = _silu(conv)

    gb = jnp.where(lane < G_A, g_raw + gb_ref[...], 0.0)
    ipre = gb
    logf = _log_sigmoid(pltpu.roll(gb, LANES - G_F, axis=1))
    logf = jnp.where(lane < MLSTM_HEADS, logf, 0.0)
    tri = jnp.where(col <= row, 1.0, 0.0).astype(BF16)
    bcum = _dot_split(tri, logf, 3)
    blast = bcum[L - 1:L, :]
    wend = blast - bcum + ipre
    m_prev = m_ref[...]
    m_new = jnp.maximum(blast + m_prev, jnp.max(wend, axis=0, keepdims=True))
    scl = jnp.exp(blast + m_prev - m_new)
    wj = jnp.exp(wend - m_new)
    inter_log = bcum + m_prev
    rows_src = jnp.where(lane < MLSTM_HEADS, ipre, pltpu.roll(bcum, MLSTM_HEADS, axis=1))
    rows_t = rows_src.T
    causal = col <= row
    e0 = jnp.where(_iota((L, MLSTM_DH), 1) == 0, 1.0, 0.0)
    for h in range(MLSTM_HEADS):
        hs = slice(h * MLSTM_DH, (h + 1) * MLSTM_DH)
        b_col = bcum[:, h:h + 1]
        ip_row = rows_t[h:h + 1, :]
        b_row = rows_t[MLSTM_HEADS + h:MLSTM_HEADS + h + 1, :]
        dlog = jnp.where(causal, b_col - b_row + ip_row, NEG_INF)
        il = inter_log[:, h:h + 1]
        m_i = jnp.maximum(il, jnp.max(dlog, axis=1, keepdims=True))
        w_intra = jnp.exp(dlog - m_i)
        s_inter = jnp.exp(il - m_i)
        q_h = qk_b[:, h * MLSTM_DH:(h + 1) * MLSTM_DH].astype(BF16)
        k_h = (qk_b[:, ML_W + h * MLSTM_DH:ML_W + (h + 1) * MLSTM_DH] * (MLSTM_DH ** -0.5)).astype(BF16)
        v_h = p_ref[0, :, P0_VB + h * MLSTM_DH:P0_VB + (h + 1) * MLSTM_DH]
        v_aug = jnp.concatenate([v_h, e0], axis=1)
        a = (w_intra * _dot_nt(q_h, k_h)).astype(BF16)
        c_prev = cst_ref[h]
        tot = _dot(a, v_aug.astype(BF16)) + s_inter * _dot_nt(q_h, c_prev.astype(BF16))
        num = tot[:, :MLSTM_DH]
        den = tot[:, MLSTM_DH:MLSTM_DH + 1]
        hid = num / jnp.maximum(jnp.abs(den), jnp.exp(-m_i))
        w_aug = (v_aug * wj[:, h:h + 1]).astype(BF16)
        cst_ref[h] = scl[:, h:h + 1] * c_prev + _dot_tn(w_aug, k_h)
        ob = p_ref[0, :, P0_OB + h * MLSTM_DH:P0_OB + (h + 1) * MLSTM_DH]
        o_ref[0, :, GLA_V + h * MLSTM_DH:GLA_V + (h + 1) * MLSTM_DH] = (
            jax.nn.sigmoid(ob) * _head_norm(hid, mg_ref[:, hs])).astype(o_ref.dtype)
    m_ref[...] = m_new


def _mixer0(proj, wa_up, ba, gla_g, conv_w, conv_b, b_i, b_f, ml_g):
    bsz, s, n = proj.shape
    L = MIX_TILE
    wa_pad = jnp.zeros((LANES, GLA_QK), F32).at[G_A:G_A + GLA_GATE_RANK].set(wa_up).astype(BF16)
    gbias = jnp.zeros((1, LANES), F32).at[0, G_I:G_I + MLSTM_HEADS].set(b_i).at[0, G_F:G_F + MLSTM_HEADS].set(b_f)
    full = lambda shape: pl.BlockSpec(shape, lambda b, i: (0,) * len(shape))
    return pl.pallas_call(
        _mixer0_kernel,
        out_shape=jax.ShapeDtypeStruct((bsz, s, GLA_V + ML_W), BF16),
        grid=(bsz, s // L),
        in_specs=[
            pl.BlockSpec((1, L, n), lambda b, i: (b, i, 0)),
            full((LANES, GLA_QK)),
            full((1, GLA_QK)),
            full((1, GLA_V)),
            full((MLSTM_CONV, 2 * ML_W)),
            full((1, 2 * ML_W)),
            full((1, LANES)),
            full((1, ML_W)),
        ],
        out_specs=pl.BlockSpec((1, L, GLA_V + ML_W), lambda b, i: (b, i, 0)),
        scratch_shapes=[
            pltpu.VMEM((GLA_DV, GLA_QK), F32),
            pltpu.VMEM((MLSTM_HEADS, 2 * MLSTM_DH, MLSTM_DH), F32),
            pltpu.VMEM((1, LANES), F32),
            pltpu.VMEM((8 + L, 2 * ML_W), F32),
        ],
        compiler_params=_params("parallel", "arbitrary"),
        name="mixer0",
    )(proj, wa_pad, ba.reshape(1, GLA_QK), gla_g.reshape(1, GLA_V), conv_w, conv_b.reshape(1, 2 * ML_W),
      gbias, ml_g.reshape(1, ML_W))


def _sb_kernel(q_ref, k_ref, v_ref, o_ref, qs_ref, acc_ref, car_ref):
    TQ, TK = SB_TQ, SB_TK
    R = 2 * TQ
    i = pl.program_id(2)
    lane = _iota((1, LANES), 1)
    q = q_ref[0].astype(F32) * (SB_DH ** -0.5 * LOG2E)
    qs_ref[0:TQ, :] = jnp.where((lane >> _HEAD_SHIFT) == 0, q, 0.0).astype(BF16)
    qs_ref[TQ:R, :] = jnp.where((lane >> _HEAD_SHIFT) == 1, q, 0.0).astype(BF16)
    acc_ref[...] = jnp.zeros_like(acc_ref)
    car_ref[...] = jnp.zeros_like(car_ref)
    tri = jnp.where(_iota((TK, TK), 0) > _iota((TK, TK), 1), -1.0, 0.0).astype(BF16)

    def tile(kb, masked):
        r0 = pl.multiple_of(kb * TK, TK)
        kt = k_ref[0, pl.ds(r0, TK), :]
        vt = v_ref[0, pl.ds(r0, TK), :]
        z = _dot_nt(qs_ref[...], kt)
        sp = jnp.maximum(z, 0.0) + jnp.log(1.0 + jnp.exp2(-jnp.abs(z))) * LOG2E
        if masked:
            t_idx = i * TQ + (_iota((R, TK), 0) & (TQ - 1))
            s_idx = kb * TK + _iota((R, TK), 1)
            mask = s_idx < t_idx
            sp = jnp.where(mask, sp, 0.0)
        hi = sp.astype(BF16)
        lo = (sp - hi.astype(F32)).astype(BF16)
        res = _dot(hi, tri) + _dot(lo, tri)
        car = car_ref[...]
        att = jnp.exp2(z - sp + res - jnp.concatenate([car] * (TK // LANES), axis=1))
        if masked:
            att = jnp.where(mask, att, 0.0)
        acc_ref[...] += _dot(att.astype(BF16), vt)
        car_ref[...] = car + jnp.sum(sp, axis=1, keepdims=True)

    nd = TQ // TK
    for d in range(nd):
        tile(nd * i + nd - 1 - d, True)

    def body(jj, carry):
        tile(nd * i - 1 - jj, False)
        return carry

    lax.fori_loop(0, nd * i, body, 0)
    o_ref[0] = jnp.where((lane >> _HEAD_SHIFT) == 0, acc_ref[0:TQ, :], acc_ref[TQ:R, :]).astype(o_ref.dtype)


def _stick_breaking(pc):
    bsz, s, _ = pc.shape
    npair = SB_W // LANES
    return pl.pallas_call(
        _sb_kernel,
        out_shape=jax.ShapeDtypeStruct((bsz, s, SB_W), BF16),
        grid=(bsz, npair, s // SB_TQ),
        in_specs=[
            pl.BlockSpec((1, SB_TQ, LANES), lambda b, p, i: (b, i, p)),
            pl.BlockSpec((1, s, LANES), lambda b, p, i: (b, 0, npair + p)),
            pl.BlockSpec((1, s, LANES), lambda b, p, i: (b, 0, 2 * npair + p)),
        ],
        out_specs=pl.BlockSpec((1, SB_TQ, LANES), lambda b, p, i: (b, i, p)),
        scratch_shapes=[
            pltpu.VMEM((2 * SB_TQ, LANES), BF16),
            pltpu.VMEM((2 * SB_TQ, LANES), F32),
            pltpu.VMEM((2 * SB_TQ, LANES), F32),
        ],
        compiler_params=_params("parallel", "parallel", "arbitrary"),
        name="stick_breaking",
    )(pc, pc, pc)


def _t5_bucket(dist):
    max_exact = N_REL_BUCKETS // 2
    dd = np.maximum(dist, 1).astype(np.float64)
    large = max_exact + (np.log(dd / max_exact) / np.log(REL_MAX_DIST / max_exact)
                         * (N_REL_BUCKETS - max_exact)).astype(np.int32)
    large = np.minimum(large, N_REL_BUCKETS - 1)
    return np.where(dist < max_exact, dist, large).astype(np.int32)


def _dil_buckets():
    qi = np.arange(DIL_BLK)[:, None]
    kj = np.arange(2 * DIL_BLK)[None, :]
    delta = qi - kj + DIL_BLK
    in_win = (delta >= 0) & (delta <= DIL_BLK)
    tabs = []
    for window, dil in DIL_PAIRS:
        assert window // dil == DIL_BLK
        bucket = _t5_bucket(np.clip(delta, 0, None) * dil)
        tabs.append(np.where(in_win, bucket, -1).astype(np.int32))
    return np.stack(tabs, 0)


def _dil_kernel(q_ref, k_ref, v_ref, bkt_ref, tab_ref, o_ref, m_sc, l_sc, acc_sc):
    s_len = q_ref.shape[1]
    hp = pl.program_id(1)
    g = pl.program_id(2)
    blk = DIL_BLK
    lane = _iota((1, LANES), 1)
    lane_head = lane >> _HEAD_SHIFT

    @pl.when(g == 0)
    def _():
        m_sc[...] = jnp.full_like(m_sc, NEG_INF)
        l_sc[...] = jnp.zeros_like(l_sc)
        acc_sc[...] = jnp.zeros_like(acc_sc)

    kcol = _iota((blk, 2 * blk), 1)

    def group(gi, dil):
        nb = s_len // (blk * dil)
        bkt = bkt_ref[gi]
        biases = []
        for hl in range(2):
            head = gi * DIL_HEADS_PER_GROUP + hp * 2 + hl
            bias = jnp.full((blk, 2 * blk), NEG_INF, F32)
            for bk in range(N_REL_BUCKETS):
                bias = jnp.where(bkt == bk, tab_ref[bk, head], bias)
            biases.append(bias)

        def block(t, carry):
            r = t // nb
            n = t % nb
            q_start = r + dil * blk * n
            p_start = r + dil * blk * jnp.maximum(n - 1, 0)
            rows_q = pl.ds(q_start, blk, stride=dil) if dil > 1 else pl.ds(q_start, blk)
            rows_p = pl.ds(p_start, blk, stride=dil) if dil > 1 else pl.ds(p_start, blk)
            qb = q_ref[0, rows_q, :] * (DIL_DH ** -0.5)
            kk = jnp.concatenate([k_ref[0, rows_p, :], k_ref[0, rows_q, :]], axis=0).astype(BF16)
            vv = jnp.concatenate([v_ref[0, rows_p, :], v_ref[0, rows_q, :]], axis=0).astype(BF16)
            first = jnp.logical_and(n == 0, kcol < blk)
            m_old = m_sc[rows_q, :]
            l_old = l_sc[rows_q, :]
            a_old = acc_sc[rows_q, :]
            m_new = m_old
            logit_heads, m_cols = [], []
            for hl in range(2):
                qm = jnp.where(lane_head == hl, qb, 0.0).astype(BF16)
                logits = _dot_nt(qm, kk) + biases[hl]
                logits = jnp.where(first, NEG_INF, logits)
                m_col = jnp.maximum(m_old[:, hl * DIL_DH:hl * DIL_DH + 1], jnp.max(logits, axis=1, keepdims=True))
                m_new = jnp.where(lane_head == hl, m_col, m_new)
                logit_heads.append(logits)
                m_cols.append(m_col)
            alpha = jnp.exp(m_old - m_new)
            l_new = alpha * l_old
            a_new = alpha * a_old
            for hl in range(2):
                sel = lane_head == hl
                p = jnp.exp(logit_heads[hl] - m_cols[hl])
                l_new = l_new + jnp.where(sel, jnp.sum(p, axis=1, keepdims=True), 0.0)
                a_new = a_new + jnp.where(sel, _dot(p.astype(BF16), vv), 0.0)
            m_sc[rows_q, :] = m_new
            l_sc[rows_q, :] = l_new
            acc_sc[rows_q, :] = a_new
            return carry

        lax.fori_loop(0, s_len // blk, block, 0)

    for gi, (_, dil) in enumerate(DIL_PAIRS):
        pl.when(g == gi)(functools.partial(group, gi, dil))

    @pl.when(g == len(DIL_PAIRS) - 1)
    def _():
        o_ref[0] = (acc_sc[...] / l_sc[...]).astype(o_ref.dtype)


def _dilated(pd, rel_bias):
    bsz, s, _ = pd.shape
    ng = len(DIL_PAIRS)
    gw = DIL_HEADS_PER_GROUP * DIL_DH // LANES
    nq = DIL_W // LANES
    bkt = jnp.asarray(_dil_buckets())
    blk_spec = lambda base: pl.BlockSpec((1, s, LANES), lambda b, p, g: (b, 0, base + g * gw + p))
    return pl.pallas_call(
        _dil_kernel,
        out_shape=jax.ShapeDtypeStruct((bsz, s, DIL_OUT), BF16),
        grid=(bsz, gw, ng),
        in_specs=[
            blk_spec(0),
            blk_spec(nq),
            blk_spec(2 * nq),
            pl.BlockSpec((ng, DIL_BLK, 2 * DIL_BLK), lambda b, p, g: (0, 0, 0)),
            pl.BlockSpec(memory_space=pltpu.SMEM),
        ],
        out_specs=pl.BlockSpec((1, s, LANES), lambda b, p, g: (b, 0, p)),
        scratch_shapes=[pltpu.VMEM((s, LANES), F32)] * 3,
        compiler_params=_params("parallel", "parallel", "arbitrary"),
        name="dilated",
    )(pd, pd, pd, bkt, rel_bias)


def _ab_weight(w):
    o = np.cumsum((0,) + (GLA_QK, GLA_QK, GLA_V, GLA_V, GLA_GATE_RANK, 2 * ML_W, ML_W, MLSTM_HEADS, MLSTM_HEADS, ML_W))
    qa, ka, va, ra, aa, qkb, vb, ib, fb, ob = [w[:, o[j]:o[j + 1]] for j in range(10)]
    pad = jnp.zeros((w.shape[0], LANES - 2 * MLSTM_HEADS - GLA_GATE_RANK), w.dtype)
    return jnp.concatenate([qa, ka, va, ra, qkb, vb, ob, ib, fb, aa, pad], axis=1).astype(BF16)


def kernel(x, c, ada_w, ada_b, ln_g, ln_b, ab_w_in, gla_wa_up, gla_ba, gla_norm_g, ml_conv_w, ml_conv_b,
           ml_b_i, ml_b_f, ml_norm_g, ab_w_out, cd_w_in, rel_bias, cd_w_out, ffn_w1, ffn_w3, ffn_w2):
    bsz, s, d = x.shape
    mod_all = _ada_mod(c, ada_w, ada_b).reshape(DEPTH, bsz, 6, d)
    tm = 512
    for layer in range(DEPTH):
        mod = mod_all[layer]
        j = layer // 2
        if layer % 2 == 0:
            (proj,) = _inproj(x, mod, _ab_weight(ab_w_in[j]), [(P0_N, F32)], tm, 0, 1, "inproj0")
            cat = _mixer0(proj, gla_wa_up[j], gla_ba[j], gla_norm_g[j], ml_conv_w[j], ml_conv_b[j],
                          ml_b_i[j], ml_b_f[j], ml_norm_g[j])
            x = _outproj_ln([cat], [ab_w_out[j].astype(BF16)], x, mod, ln_g[layer, 0], ln_b[layer, 0],
                            tm, 2, "outproj0")
        else:
            pc, pd = _inproj(x, mod, cd_w_in[j].astype(BF16), [(3 * SB_W, BF16), (3 * DIL_W, F32)],
                             tm, 0, 1, "inproj1")
            oc = _stick_breaking(pc)
            od = _dilated(pd, rel_bias)
            w_out = cd_w_out[j].astype(BF16)
            x = _outproj_ln([oc, od], [w_out[:SB_W], w_out[SB_W:]], x, mod, ln_g[layer, 0], ln_b[layer, 0],
                            tm, 2, "outproj1")
        x = _ffn(x, mod, ffn_w1[layer], ffn_w3[layer], ffn_w2[layer], ln_g[layer, 1], ln_b[layer, 1], tm)
    return x
```

```python
import functools

import numpy as np
import jax
import jax.numpy as jnp
from jax import lax
from jax.experimental import pallas as pl
from jax.experimental.pallas import tpu as pltpu

F32 = jnp.float32
BF16 = jnp.bfloat16

D_MODEL = 1024
DEPTH = 2
GLA_HEADS = 4
GLA_DK = 64
GLA_DV = 128
GLA_GATE_RANK = 16
GLA_TAU = 16.0
GLA_CHUNK = 64
MLSTM_HEADS = 4
MLSTM_DH = 128
MLSTM_CONV = 4
SB_HEADS = 8
SB_DH = 64
DIL_PAIRS = ((128, 1), (512, 4), (2048, 16))
DIL_HEADS_PER_GROUP = 4
DIL_DH = 64
DIL_BLK = 128
DIL_UNROLL = 4
N_REL_BUCKETS = 32
REL_MAX_DIST = 2048
D_FF = ((8 * D_MODEL + 3 * 256 - 1) // (3 * 256)) * 256
LN_EPS = 1e-5
RES_ALPHA = (2 * DEPTH) ** 0.25
NEG_INF = -1e30

GLA_QK = GLA_HEADS * GLA_DK
GLA_V = GLA_HEADS * GLA_DV
ML_W = MLSTM_HEADS * MLSTM_DH
SB_W = SB_HEADS * SB_DH
DIL_W = len(DIL_PAIRS) * DIL_HEADS_PER_GROUP * DIL_DH
DIL_OUT = DIL_HEADS_PER_GROUP * DIL_DH

LANES = 128
VMEM_LIMIT = 56 * 1024 * 1024

P0_QA = 0
P0_KA = P0_QA + GLA_QK
P0_VA = P0_KA + GLA_QK
P0_RA = P0_VA + GLA_V
P0_QKB = P0_RA + GLA_V
P0_VB = P0_QKB + 2 * ML_W
P0_OB = P0_VB + ML_W
P0_G = P0_OB + ML_W
P0_N = P0_G + LANES
G_I = 0
G_F = MLSTM_HEADS
G_A = 2 * MLSTM_HEADS

MIX_TILE = 256
FF_CHUNK = 256
SB_TQ = 512
SB_TK = 256
assert (SB_TQ // SB_TK) % 2 == 0
LOG2E = 1.4426950408889634
_CHUNK_SHIFT = 6
_HEAD_SHIFT = 6
assert GLA_CHUNK == GLA_DK == 1 << _CHUNK_SHIFT and SB_DH == DIL_DH == 1 << _HEAD_SHIFT


def _dot(a, b):
    return jnp.dot(a, b, preferred_element_type=F32)


def _dot_nt(a, b):
    return lax.dot_general(a, b, (((1,), (1,)), ((), ())), preferred_element_type=F32)


def _dot_tn(a, b):
    return lax.dot_general(a, b, (((0,), (0,)), ((), ())), preferred_element_type=F32)


def _dot_split(t, x, terms):
    acc = None
    rem = x
    for i in range(terms):
        part = rem.astype(BF16)
        d = _dot(t, part)
        acc = d if acc is None else acc + d
        if i + 1 < terms:
            rem = rem - part.astype(F32)
    return acc


def _log_sigmoid(x):
    return jnp.minimum(x, 0.0) - jnp.log(1.0 + jnp.exp(-jnp.abs(x)))


def _silu(x):
    return x * jax.nn.sigmoid(x)


def _layer_norm(r, g, b):
    mu = jnp.mean(r, axis=-1, keepdims=True)
    d = r - mu
    var = jnp.mean(d * d, axis=-1, keepdims=True)
    return d * lax.rsqrt(var + LN_EPS) * g + b


def _head_norm(x, g):
    mu = jnp.mean(x, axis=-1, keepdims=True)
    d = x - mu
    var = jnp.mean(d * d, axis=-1, keepdims=True)
    return d * lax.rsqrt(var + LN_EPS) * g


def _iota(shape, dim):
    return lax.broadcasted_iota(jnp.int32, shape, dim)


def _params(*sem):
    return pltpu.CompilerParams(dimension_semantics=sem, vmem_limit_bytes=VMEM_LIMIT)


def _ada_kernel(c_ref, w_ref, b_ref, o_ref):
    ca = _silu(c_ref[...]).astype(BF16)
    o_ref[0] = _dot(ca, w_ref[0].astype(BF16)) + b_ref[0]


def _ada_mod(c, ada_w, ada_b):
    bsz, d = c.shape
    n = ada_w.shape[-1]
    tn = n // 4
    return pl.pallas_call(
        _ada_kernel,
        out_shape=jax.ShapeDtypeStruct((DEPTH, bsz, n), F32),
        grid=(DEPTH, n // tn),
        in_specs=[
            pl.BlockSpec((bsz, d), lambda l, j: (0, 0)),
            pl.BlockSpec((1, d, tn), lambda l, j: (l, 0, j)),
            pl.BlockSpec((1, 1, tn), lambda l, j: (l, 0, j)),
        ],
        out_specs=pl.BlockSpec((1, bsz, tn), lambda l, j: (l, 0, j)),
        compiler_params=_params("arbitrary", "arbitrary"),
        name="ada_mod",
    )(c, ada_w, ada_b.reshape(DEPTH, 1, n))


def _inproj_kernel(x_ref, mod_ref, w_ref, *o_refs, shift_row, scale_row):
    sh = mod_ref[0, shift_row:shift_row + 1, :]
    sc = mod_ref[0, scale_row:scale_row + 1, :]
    hm = (x_ref[0] * (1.0 + sc) + sh).astype(BF16)
    col = 0
    for o_ref in o_refs:
        n = o_ref.shape[-1]
        o_ref[0] = _dot(hm, w_ref[:, col:col + n]).astype(o_ref.dtype)
        col += n


def _inproj(x, mod, w, outs, tm, shift_row, scale_row, name):
    bsz, s, d = x.shape
    n = w.shape[1]
    assert sum(o[0] for o in outs) == n
    return pl.pallas_call(
        functools.partial(_inproj_kernel, shift_row=shift_row, scale_row=scale_row),
        out_shape=[jax.ShapeDtypeStruct((bsz, s, o[0]), o[1]) for o in outs],
        grid=(bsz, s // tm),
        in_specs=[
            pl.BlockSpec((1, tm, d), lambda b, i: (b, i, 0)),
            pl.BlockSpec((1, 6, d), lambda b, i: (b, 0, 0)),
            pl.BlockSpec((d, n), lambda b, i: (0, 0)),
        ],
        out_specs=[pl.BlockSpec((1, tm, o[0]), lambda b, i: (b, i, 0)) for o in outs],
        compiler_params=_params("parallel", "arbitrary"),
        name=name,
    )(x, mod, w)


def _outproj_kernel(*refs, n_act, gate_row):
    act_refs = refs[:n_act]
    w_refs = refs[n_act:2 * n_act]
    x_ref, mod_ref, g_ref, b_ref, o_ref = refs[2 * n_act:]
    y = None
    for a_ref, w_ref in zip(act_refs, w_refs):
        t = _dot(a_ref[0], w_ref[...])
        y = t if y is None else y + t
    gate = mod_ref[0, gate_row:gate_row + 1, :]
    r = RES_ALPHA * x_ref[0] + (1.0 + gate) * y
    o_ref[0] = _layer_norm(r, g_ref[...], b_ref[...])


def _outproj_ln(acts, ws, x, mod, ln_g, ln_b, tm, gate_row, name):
    bsz, s, d = x.shape
    n_act = len(acts)
    in_specs = [pl.BlockSpec((1, tm, a.shape[-1]), lambda b, i: (b, i, 0)) for a in acts]
    in_specs += [pl.BlockSpec(w.shape, lambda b, i: (0, 0)) for w in ws]
    in_specs += [
        pl.BlockSpec((1, tm, d), lambda b, i: (b, i, 0)),
        pl.BlockSpec((1, 6, d), lambda b, i: (b, 0, 0)),
        pl.BlockSpec((1, d), lambda b, i: (0, 0)),
        pl.BlockSpec((1, d), lambda b, i: (0, 0)),
    ]
    return pl.pallas_call(
        functools.partial(_outproj_kernel, n_act=n_act, gate_row=gate_row),
        out_shape=jax.ShapeDtypeStruct((bsz, s, d), F32),
        grid=(bsz, s // tm),
        in_specs=in_specs,
        out_specs=pl.BlockSpec((1, tm, d), lambda b, i: (b, i, 0)),
        compiler_params=_params("parallel", "arbitrary"),
        name=name,
    )(*acts, *ws, x, mod, ln_g.reshape(1, d), ln_b.reshape(1, d))


def _ffn_kernel(x_ref, mod_ref, w1_ref, w3_ref, w2_ref, g_ref, b_ref, o_ref, acc_ref):
    x = x_ref[0]
    sh = mod_ref[0, 3:4, :]
    sc = mod_ref[0, 4:5, :]
    gate = mod_ref[0, 5:6, :]
    hf = (x * (1.0 + sc) + sh).astype(BF16)
    acc_ref[...] = jnp.zeros_like(acc_ref)

    def body(c, carry):
        h1 = _dot(hf, w1_ref[c])
        h3 = _dot(hf, w3_ref[c])
        gm = (_silu(h1) * h3).astype(BF16)
        acc_ref[...] += _dot(gm, w2_ref[c])
        return carry

    lax.fori_loop(0, w1_ref.shape[0], body, 0)
    r = RES_ALPHA * x + (1.0 + gate) * acc_ref[...]
    o_ref[0] = _layer_norm(r, g_ref[...], b_ref[...])


def _ffn(x, mod, w1, w3, w2, ln_g, ln_b, tm):
    bsz, s, d = x.shape
    nc = D_FF // FF_CHUNK
    w1r = w1.astype(BF16).reshape(d, nc, FF_CHUNK).transpose(1, 0, 2)
    w3r = w3.astype(BF16).reshape(d, nc, FF_CHUNK).transpose(1, 0, 2)
    w2r = w2.astype(BF16).reshape(nc, FF_CHUNK, d)
    return pl.pallas_call(
        _ffn_kernel,
        out_shape=jax.ShapeDtypeStruct((bsz, s, d), F32),
        grid=(bsz, s // tm),
        in_specs=[
            pl.BlockSpec((1, tm, d), lambda b, i: (b, i, 0)),
            pl.BlockSpec((1, 6, d), lambda b, i: (b, 0, 0)),
            pl.BlockSpec((nc, d, FF_CHUNK), lambda b, i: (0, 0, 0)),
            pl.BlockSpec((nc, d, FF_CHUNK), lambda b, i: (0, 0, 0)),
            pl.BlockSpec((nc, FF_CHUNK, d), lambda b, i: (0, 0, 0)),
            pl.BlockSpec((1, d), lambda b, i: (0, 0)),
            pl.BlockSpec((1, d), lambda b, i: (0, 0)),
        ],
        out_specs=pl.BlockSpec((1, tm, d), lambda b, i: (b, i, 0)),
        scratch_shapes=[pltpu.VMEM((tm, d), F32)],
        compiler_params=_params("parallel", "arbitrary"),
        name="ffn",
    )(x, mod, w1r, w3r, w2r, ln_g.reshape(1, d), ln_b.reshape(1, d))


def _mixer0_kernel(p_ref, wa_ref, ba_ref, gg_ref, cw_ref, cb_ref, gb_ref, mg_ref, o_ref,
                   st_ref, cst_ref, m_ref, xc_ref):
    L = MIX_TILE
    i = pl.program_id(1)

    @pl.when(i == 0)
    def _():
        st_ref[...] = jnp.zeros_like(st_ref)
        cst_ref[...] = jnp.zeros_like(cst_ref)
        m_ref[...] = jnp.zeros_like(m_ref)
        xc_ref[0:8, :] = jnp.zeros((8, 2 * ML_W), F32)

    lane = _iota((1, LANES), 1)
    g_raw = p_ref[0, :, P0_G:P0_G + LANES]

    u = _dot(g_raw.astype(BF16), wa_ref[...]) + ba_ref[...]
    la = _log_sigmoid(u) * (1.0 / GLA_TAU)
    row = _iota((L, L), 0)
    col = _iota((L, L), 1)
    same_chunk = (row >> _CHUNK_SHIFT) == (col >> _CHUNK_SHIFT)
    tri_blk = jnp.where(same_chunk & (col <= row), 1.0, 0.0).astype(BF16)
    bcs = _dot_split(tri_blk, la, 3)
    q_in = p_ref[0, :, P0_QA:P0_QA + GLA_QK] * (GLA_DK ** -0.5) * jnp.exp(bcs)
    k_raw = p_ref[0, :, P0_KA:P0_KA + GLA_QK]
    k_in = k_raw * jnp.exp(-bcs)
    lane_qk = _iota((1, GLA_QK), 1)
    head_masks = [(lane_qk >> _CHUNK_SHIFT) == h for h in range(GLA_HEADS)]
    r4 = _iota((GLA_HEADS * GLA_CHUNK, GLA_CHUNK), 0)
    c4 = _iota((GLA_HEADS * GLA_CHUNK, GLA_CHUNK), 1)
    tril4 = c4 <= (r4 & (GLA_CHUNK - 1))
    oa_chunks = []
    for c in range(L // GLA_CHUNK):
        r0, r1 = c * GLA_CHUNK, (c + 1) * GLA_CHUNK
        b_c = bcs[r0:r1]
        bl = b_c[GLA_CHUNK - 1:GLA_CHUNK, :]
        q_c = q_in[r0:r1]
        k_c = k_in[r0:r1].astype(BF16)
        k_end = k_raw[r0:r1] * jnp.exp(bl - b_c)
        v_c = p_ref[0, r0:r1, P0_VA:P0_VA + GLA_V]
        q_exp = jnp.concatenate([jnp.where(hm, q_c, 0.0) for hm in head_masks], axis=0).astype(BF16)
        k_exp = jnp.concatenate([jnp.where(hm, k_end, 0.0) for hm in head_masks], axis=0).astype(BF16)
        sc = jnp.where(tril4, _dot_nt(q_exp, k_c), 0.0)
        intra = _dot(sc.astype(BF16), v_c.astype(BF16))
        inter = _dot_nt(q_exp, st_ref[...].astype(BF16))
        o_heads = []
        for h in range(GLA_HEADS):
            h0, h1 = h * GLA_CHUNK, (h + 1) * GLA_CHUNK
            o_heads.append(intra[h0:h1, h * GLA_DV:(h + 1) * GLA_DV] + inter[h0:h1])
        oa_chunks.append(o_heads)
        v_cat = jnp.concatenate([v_c[:, h * GLA_DV:(h + 1) * GLA_DV] for h in range(GLA_HEADS)], axis=0)
        st_ref[...] = st_ref[...] * jnp.exp(bl) + _dot_tn(v_cat.astype(BF16), k_exp)
    for h in range(GLA_HEADS):
        o_h = jnp.concatenate([oc[h] for oc in oa_chunks], axis=0)
        cs = slice(h * GLA_DV, (h + 1) * GLA_DV)
        ra = p_ref[0, :, P0_RA + h * GLA_DV:P0_RA + (h + 1) * GLA_DV]
        o_ref[0, :, cs] = (_head_norm(o_h, gg_ref[:, cs]) * _silu(ra)).astype(o_ref.dtype)

    xc_ref[8:8 + L, :] = p_ref[0, :, P0_QKB:P0_QKB + 2 * ML_W]
    conv = cb_ref[...]
    for kk in range(MLSTM_CONV):
        conv = conv + cw_ref[kk:kk + 1, :] * xc_ref[8 - (MLSTM_CONV - 1) + kk:8 - (MLSTM_CONV - 1) + kk + L, :]
    xc_ref[0:8, :] = xc_ref[L:L + 8, :]
    qk_b = _silu(conv)

    gb = jnp.where(lane < G_A, g_raw + gb_ref[...], 0.0)
    ipre = gb
    logf = _log_sigmoid(pltpu.roll(gb, LANES - G_F, axis=1))
    logf = jnp.where(lane < MLSTM_HEADS, logf, 0.0)
    tri = jnp.where(col <= row, 1.0, 0.0).astype(BF16)
    bcum = _dot_split(tri, logf, 3)
    blast = bcum[L - 1:L, :]
    wend = blast - bcum + ipre
    m_prev = m_ref[...]
    m_new = jnp.maximum(blast + m_prev, jnp.max(wend, axis=0, keepdims=True))
    scl = jnp.exp(blast + m_prev - m_new)
    wj = jnp.exp(wend - m_new)
    inter_log = bcum + m_prev
    rows_src = jnp.where(lane < MLSTM_HEADS, ipre, pltpu.roll(bcum, MLSTM_HEADS, axis=1))
    rows_t = rows_src.T
    causal = col <= row
    e0 = jnp.where(_iota((L, MLSTM_DH), 1) == 0, 1.0, 0.0)
    for h in range(MLSTM_HEADS):
        hs = slice(h * MLSTM_DH, (h + 1) * MLSTM_DH)
        b_col = bcum[:, h:h + 1]
        ip_row = rows_t[h:h + 1, :]
        b_row = rows_t[MLSTM_HEADS + h:MLSTM_HEADS + h + 1, :]
        dlog = jnp.where(causal, b_col - b_row + ip_row, NEG_INF)
        il = inter_log[:, h:h + 1]
        m_i = jnp.maximum(il, jnp.max(dlog, axis=1, keepdims=True))
        w_intra = jnp.exp(dlog - m_i)
        s_inter = jnp.exp(il - m_i)
        q_h = qk_b[:, h * MLSTM_DH:(h + 1) * MLSTM_DH].astype(BF16)
        k_h = (qk_b[:, ML_W + h * MLSTM_DH:ML_W + (h + 1) * MLSTM_DH] * (MLSTM_DH ** -0.5)).astype(BF16)
        v_h = p_ref[0, :, P0_VB + h * MLSTM_DH:P0_VB + (h + 1) * MLSTM_DH]
        v_aug = jnp.concatenate([v_h, e0], axis=1)
        a = (w_intra * _dot_nt(q_h, k_h)).astype(BF16)
        c_prev = cst_ref[h]
        tot = _dot(a, v_aug.astype(BF16)) + s_inter * _dot_nt(q_h, c_prev.astype(BF16))
        num = tot[:, :MLSTM_DH]
        den = tot[:, MLSTM_DH:MLSTM_DH + 1]
        hid = num / jnp.maximum(jnp.abs(den), jnp.exp(-m_i))
        w_aug = (v_aug * wj[:, h:h + 1]).astype(BF16)
        cst_ref[h] = scl[:, h:h + 1] * c_prev + _dot_tn(w_aug, k_h)
        ob = p_ref[0, :, P0_OB + h * MLSTM_DH:P0_OB + (h + 1) * MLSTM_DH]
        o_ref[0, :, GLA_V + h * MLSTM_DH:GLA_V + (h + 1) * MLSTM_DH] = (
            jax.nn.sigmoid(ob) * _head_norm(hid, mg_ref[:, hs])).astype(o_ref.dtype)
    m_ref[...] = m_new


def _mixer0(proj, wa_up, ba, gla_g, conv_w, conv_b, b_i, b_f, ml_g):
    bsz, s, n = proj.shape
    L = MIX_TILE
    wa_pad = jnp.zeros((LANES, GLA_QK), F32).at[G_A:G_A + GLA_GATE_RANK].set(wa_up).astype(BF16)
    gbias = jnp.zeros((1, LANES), F32).at[0, G_I:G_I + MLSTM_HEADS].set(b_i).at[0, G_F:G_F + MLSTM_HEADS].set(b_f)
    full = lambda shape: pl.BlockSpec(shape, lambda b, i: (0,) * len(shape))
    return pl.pallas_call(
        _mixer0_kernel,
        out_shape=jax.ShapeDtypeStruct((bsz, s, GLA_V + ML_W), BF16),
        grid=(bsz, s // L),
        in_specs=[
            pl.BlockSpec((1, L, n), lambda b, i: (b, i, 0)),
            full((LANES, GLA_QK)),
            full((1, GLA_QK)),
            full((1, GLA_V)),
            full((MLSTM_CONV, 2 * ML_W)),
            full((1, 2 * ML_W)),
            full((1, LANES)),
            full((1, ML_W)),
        ],
        out_specs=pl.BlockSpec((1, L, GLA_V + ML_W), lambda b, i: (b, i, 0)),
        scratch_shapes=[
            pltpu.VMEM((GLA_DV, GLA_QK), F32),
            pltpu.VMEM((MLSTM_HEADS, 2 * MLSTM_DH, MLSTM_DH), F32),
            pltpu.VMEM((1, LANES), F32),
            pltpu.VMEM((8 + L, 2 * ML_W), F32),
        ],
        compiler_params=_params("parallel", "arbitrary"),
        name="mixer0",
    )(proj, wa_pad, ba.reshape(1, GLA_QK), gla_g.reshape(1, GLA_V), conv_w, conv_b.reshape(1, 2 * ML_W),
      gbias, ml_g.reshape(1, ML_W))


def _sb_kernel(q_ref, k_ref, v_ref, o_ref, qs_ref, acc_ref, car_ref, zb_ref):
    TQ, TK = SB_TQ, SB_TK
    nd = TQ // TK
    RB = 2 * TK
    i = pl.program_id(2)
    lane = _iota((1, LANES), 1)
    for a in range(nd):
        qa = q_ref[0, a * TK:(a + 1) * TK, :].astype(F32) * (SB_DH ** -0.5 * LOG2E)
        qs_ref[a * RB:a * RB + TK, :] = jnp.where((lane >> _HEAD_SHIFT) == 0, qa, 0.0).astype(BF16)
        qs_ref[a * RB + TK:(a + 1) * RB, :] = jnp.where((lane >> _HEAD_SHIFT) == 1, qa, 0.0).astype(BF16)
    acc_ref[...] = jnp.zeros_like(acc_ref)
    car_ref[...] = jnp.zeros_like(car_ref)
    tri = jnp.where(_iota((TK, TK), 0) >= _iota((TK, TK), 1), -1.0, 0.0).astype(BF16)

    def logits(kb, a_lo, a_hi):
        r0 = pl.multiple_of(kb * TK, TK)
        return _dot_nt(qs_ref[a_lo * RB:a_hi * RB, :], k_ref[0, pl.ds(r0, TK), :])

    def tile(z, kb, a_lo, a_hi, diagonal):
        rows = slice(a_lo * RB, a_hi * RB)
        n = (a_hi - a_lo) * RB
        vt = v_ref[0, pl.ds(pl.multiple_of(kb * TK, TK), TK), :]
        sp = jnp.maximum(z, 0.0) + jnp.log(1.0 + jnp.exp2(-jnp.abs(z))) * LOG2E
        if diagonal:
            mask = _iota((n, TK), 1) < (_iota((n, TK), 0) & (TK - 1))
            sp = jnp.where(mask, sp, 0.0)
        hi = sp.astype(BF16)
        lo = (sp - hi.astype(F32)).astype(BF16)
        res = _dot(hi, tri) + _dot(lo, tri)
        car = car_ref[rows, :]
        att = jnp.exp2(z + res - jnp.concatenate([car] * (TK // LANES), axis=1))
        if diagonal:
            att = jnp.where(mask, att, 0.0)
        acc_ref[rows, :] += _dot(att.astype(BF16), vt)
        car_ref[rows, :] = car + jnp.sum(sp, axis=1, keepdims=True)

    for d in reversed(range(nd)):
        tile(logits(nd * i + d, d, d + 1), nd * i + d, d, d + 1, True)
        if d + 1 < nd:
            tile(logits(nd * i + d, d + 1, nd), nd * i + d, d + 1, nd, False)

    n_off = nd * i
    zb_ref[...] = logits(jnp.maximum(n_off - 1, 0), 0, nd)

    def body(jj, carry):
        kb_a = n_off - 1 - 2 * jj
        z_a = zb_ref[...]
        z_b = logits(kb_a - 1, 0, nd)
        tile(z_a, kb_a, 0, nd, False)
        zb_ref[...] = logits(jnp.maximum(kb_a - 2, 0), 0, nd)
        tile(z_b, kb_a - 1, 0, nd, False)
        return carry

    lax.fori_loop(0, n_off // 2, body, 0)
    for a in range(nd):
        o_ref[0, a * TK:(a + 1) * TK, :] = jnp.where(
            (lane >> _HEAD_SHIFT) == 0, acc_ref[a * RB:a * RB + TK, :], acc_ref[a * RB + TK:(a + 1) * RB, :]
        ).astype(o_ref.dtype)


def _stick_breaking(pc):
    bsz, s, _ = pc.shape
    npair = SB_W // LANES
    return pl.pallas_call(
        _sb_kernel,
        out_shape=jax.ShapeDtypeStruct((bsz, s, SB_W), BF16),
        grid=(bsz, npair, s // SB_TQ),
        in_specs=[
            pl.BlockSpec((1, SB_TQ, LANES), lambda b, p, i: (b, i, p)),
            pl.BlockSpec((1, s, LANES), lambda b, p, i: (b, 0, npair + p)),
            pl.BlockSpec((1, s, LANES), lambda b, p, i: (b, 0, 2 * npair + p)),
        ],
        out_specs=pl.BlockSpec((1, SB_TQ, LANES), lambda b, p, i: (b, i, p)),
        scratch_shapes=[
            pltpu.VMEM((2 * SB_TQ, LANES), BF16),
            pltpu.VMEM((2 * SB_TQ, LANES), F32),
            pltpu.VMEM((2 * SB_TQ, LANES), F32),
            pltpu.VMEM((2 * SB_TQ, SB_TK), F32),
        ],
        compiler_params=_params("parallel", "parallel", "arbitrary"),
        name="stick_breaking",
    )(pc, pc, pc)


def _t5_bucket(dist):
    max_exact = N_REL_BUCKETS // 2
    dd = np.maximum(dist, 1).astype(np.float64)
    large = max_exact + (np.log(dd / max_exact) / np.log(REL_MAX_DIST / max_exact)
                         * (N_REL_BUCKETS - max_exact)).astype(np.int32)
    large = np.minimum(large, N_REL_BUCKETS - 1)
    return np.where(dist < max_exact, dist, large).astype(np.int32)


def _dil_buckets():
    qi = np.arange(DIL_BLK)[:, None]
    kj = np.arange(2 * DIL_BLK)[None, :]
    delta = qi - kj + DIL_BLK
    in_win = (delta >= 0) & (delta <= DIL_BLK)
    tabs = []
    for window, dil in DIL_PAIRS:
        assert window // dil == DIL_BLK
        bucket = _t5_bucket(np.clip(delta, 0, None) * dil)
        tabs.append(np.where(in_win, bucket, -1).astype(np.int32))
    return np.stack(tabs, 0)


def _dil_kernel(q_ref, k_ref, v_ref, bkt_ref, tab_ref, o_ref, m_sc, l_sc, acc_sc):
    s_len = q_ref.shape[1]
    hp = pl.program_id(1)
    g = pl.program_id(2)
    blk = DIL_BLK
    lane = _iota((1, LANES), 1)
    lane_head = lane >> _HEAD_SHIFT

    @pl.when(g == 0)
    def _():
        m_sc[...] = jnp.full_like(m_sc, NEG_INF)
        l_sc[...] = jnp.zeros_like(l_sc)
        acc_sc[...] = jnp.zeros_like(acc_sc)

    kcol = _iota((2 * blk, 2 * blk), 1)

    def group(gi, dil):
        nb = s_len // (blk * dil)
        bkt = bkt_ref[gi]
        biases = []
        for hl in range(2):
            head = gi * DIL_HEADS_PER_GROUP + hp * 2 + hl
            bias = jnp.full((blk, 2 * blk), NEG_INF, F32)
            for bk in range(N_REL_BUCKETS):
                bias = jnp.where(bkt == bk, tab_ref[bk, head], bias)
            biases.append(bias)
        bias2 = jnp.concatenate(biases, axis=0)
        sel0 = lane_head == 0

        def block(t):
            r = t // nb
            n = t % nb
            q_start = r + dil * blk * n
            p_start = r + dil * blk * jnp.maximum(n - 1, 0)
            rows_q = pl.ds(q_start, blk, stride=dil) if dil > 1 else pl.ds(q_start, blk)
            rows_p = pl.ds(p_start, blk, stride=dil) if dil > 1 else pl.ds(p_start, blk)
            qb = q_ref[0, rows_q, :] * (DIL_DH ** -0.5)
            q2 = jnp.concatenate([jnp.where(sel0, qb, 0.0), jnp.where(sel0, 0.0, qb)], axis=0).astype(BF16)
            kk = jnp.concatenate([k_ref[0, rows_p, :], k_ref[0, rows_q, :]], axis=0).astype(BF16)
            vv = jnp.concatenate([v_ref[0, rows_p, :], v_ref[0, rows_q, :]], axis=0).astype(BF16)
            logits = _dot_nt(q2, kk) + bias2
            logits = jnp.where(jnp.logical_and(n == 0, kcol < blk), NEG_INF, logits)
            m_old = m_sc[rows_q, :]
            m_old_col = jnp.concatenate([m_old[:, 0:1], m_old[:, DIL_DH:DIL_DH + 1]], axis=0)
            m_col = jnp.maximum(m_old_col, jnp.max(logits, axis=1, keepdims=True))
            p = jnp.exp(logits - m_col)
            ps = jnp.sum(p, axis=1, keepdims=True)
            pv = _dot(p.astype(BF16), vv)
            m_new = jnp.where(sel0, m_col[0:blk], m_col[blk:])
            alpha = jnp.exp(m_old - m_new)
            l_new = alpha * l_sc[rows_q, :] + jnp.where(sel0, ps[0:blk], ps[blk:])
            a_new = alpha * acc_sc[rows_q, :] + jnp.where(sel0, pv[0:blk], pv[blk:])
            return rows_q, m_new, l_new, a_new

        def body(tt, carry):
            results = [block(tt * DIL_UNROLL + u) for u in range(DIL_UNROLL)]
            for rows_q, m_new, l_new, a_new in results:
                m_sc[rows_q, :] = m_new
                l_sc[rows_q, :] = l_new
                acc_sc[rows_q, :] = a_new
            return carry

        lax.fori_loop(0, s_len // (blk * DIL_UNROLL), body, 0)

    for gi, (_, dil) in enumerate(DIL_PAIRS):
        pl.when(g == gi)(functools.partial(group, gi, dil))

    @pl.when(g == len(DIL_PAIRS) - 1)
    def _():
        o_ref[0] = (acc_sc[...] / l_sc[...]).astype(o_ref.dtype)


def _dilated(pd, rel_bias):
    bsz, s, _ = pd.shape
    ng = len(DIL_PAIRS)
    gw = DIL_HEADS_PER_GROUP * DIL_DH // LANES
    nq = DIL_W // LANES
    bkt = jnp.asarray(_dil_buckets())
    blk_spec = lambda base: pl.BlockSpec((1, s, LANES), lambda b, p, g: (b, 0, base + g * gw + p))
    return pl.pallas_call(
        _dil_kernel,
        out_shape=jax.ShapeDtypeStruct((bsz, s, DIL_OUT), BF16),
        grid=(bsz, gw, ng),
        in_specs=[
            blk_spec(0),
            blk_spec(nq),
            blk_spec(2 * nq),
            pl.BlockSpec((ng, DIL_BLK, 2 * DIL_BLK), lambda b, p, g: (0, 0, 0)),
            pl.BlockSpec(memory_space=pltpu.SMEM),
        ],
        out_specs=pl.BlockSpec((1, s, LANES), lambda b, p, g: (b, 0, p)),
        scratch_shapes=[pltpu.VMEM((s, LANES), F32)] * 3,
        compiler_params=_params("parallel", "parallel", "arbitrary"),
        name="dilated",
    )(pd, pd, pd, bkt, rel_bias)


def _ab_weight(w):
    o = np.cumsum((0,) + (GLA_QK, GLA_QK, GLA_V, GLA_V, GLA_GATE_RANK, 2 * ML_W, ML_W, MLSTM_HEADS, MLSTM_HEADS, ML_W))
    qa, ka, va, ra, aa, qkb, vb, ib, fb, ob = [w[:, o[j]:o[j + 1]] for j in range(10)]
    pad = jnp.zeros((w.shape[0], LANES - 2 * MLSTM_HEADS - GLA_GATE_RANK), w.dtype)
    return jnp.concatenate([qa, ka, va, ra, qkb, vb, ob, ib, fb, aa, pad], axis=1).astype(BF16)


def kernel(x, c, ada_w, ada_b, ln_g, ln_b, ab_w_in, gla_wa_up, gla_ba, gla_norm_g, ml_conv_w, ml_conv_b,
           ml_b_i, ml_b_f, ml_norm_g, ab_w_out, cd_w_in, rel_bias, cd_w_out, ffn_w1, ffn_w3, ffn_w2):
    bsz, s, d = x.shape
    mod_all = _ada_mod(c, ada_w, ada_b).reshape(DEPTH, bsz, 6, d)
    tm = 512
    for layer in range(DEPTH):
        mod = mod_all[layer]
        j = layer // 2
        if layer % 2 == 0:
            (proj,) = _inproj(x, mod, _ab_weight(ab_w_in[j]), [(P0_N, F32)], tm, 0, 1, "inproj0")
            cat = _mixer0(proj, gla_wa_up[j], gla_ba[j], gla_norm_g[j], ml_conv_w[j], ml_conv_b[j],
                          ml_b_i[j], ml_b_f[j], ml_norm_g[j])
            x = _outproj_ln([cat], [ab_w_out[j].astype(BF16)], x, mod, ln_g[layer, 0], ln_b[layer, 0],
                            tm, 2, "outproj0")
        else:
            pc, pd = _inproj(x, mod, cd_w_in[j].astype(BF16), [(3 * SB_W, BF16), (3 * DIL_W, F32)],
                             tm, 0, 1, "inproj1")
            oc = _stick_breaking(pc)
            od = _dilated(pd, rel_bias)
            w_out = cd_w_out[j].astype(BF16)
            x = _outproj_ln([oc, od], [w_out[:SB_W], w_out[SB_W:]], x, mod, ln_g[layer, 0], ln_b[layer, 0],
                            tm, 2, "outproj1")
        x = _ffn(x, mod, ffn_w1[layer], ffn_w3[layer], ffn_w2[layer], ln_g[layer, 1], ln_b[layer, 1], tm)
    return x
```

```python
import functools

import numpy as np
import jax
import jax.numpy as jnp
from jax import lax
from jax.experimental import pallas as pl
from jax.experimental.pallas import tpu as pltpu

F32 = jnp.float32
BF16 = jnp.bfloat16

D_MODEL = 1024
DEPTH = 2
GLA_HEADS = 4
GLA_DK = 64
GLA_DV = 128
GLA_GATE_RANK = 16
GLA_TAU = 16.0
GLA_CHUNK = 64
MLSTM_HEADS = 4
MLSTM_DH = 128
MLSTM_CONV = 4
SB_HEADS = 8
SB_DH = 64
DIL_PAIRS = ((128, 1), (512, 4), (2048, 16))
DIL_HEADS_PER_GROUP = 4
DIL_DH = 64
DIL_BLK = 128
DIL_UNROLL = 4
N_REL_BUCKETS = 32
REL_MAX_DIST = 2048
D_FF = ((8 * D_MODEL + 3 * 256 - 1) // (3 * 256)) * 256
LN_EPS = 1e-5
RES_ALPHA = (2 * DEPTH) ** 0.25
NEG_INF = -1e30

GLA_QK = GLA_HEADS * GLA_DK
GLA_V = GLA_HEADS * GLA_DV
ML_W = MLSTM_HEADS * MLSTM_DH
SB_W = SB_HEADS * SB_DH
DIL_W = len(DIL_PAIRS) * DIL_HEADS_PER_GROUP * DIL_DH
DIL_OUT = DIL_HEADS_PER_GROUP * DIL_DH

LANES = 128
VMEM_LIMIT = 56 * 1024 * 1024

P0_QA = 0
P0_KA = P0_QA + GLA_QK
P0_VA = P0_KA + GLA_QK
P0_RA = P0_VA + GLA_V
P0_QKB = P0_RA + GLA_V
P0_VB = P0_QKB + 2 * ML_W
P0_OB = P0_VB + ML_W
P0_G = P0_OB + ML_W
P0_N = P0_G + LANES
G_I = 0
G_F = MLSTM_HEADS
G_A = 2 * MLSTM_HEADS

MIX_TILE = 256
FF_CHUNK = 256
SB_TQ = 512
SB_TK = 256
SB_ZERO_BITS = 160.0
SB_BOUND_SLACK = 1.01
LOG2E = 1.4426950408889634
_CHUNK_SHIFT = 6
_HEAD_SHIFT = 6
assert GLA_CHUNK == GLA_DK == 1 << _CHUNK_SHIFT and SB_DH == DIL_DH == 1 << _HEAD_SHIFT


def _dot(a, b):
    return jnp.dot(a, b, preferred_element_type=F32)


def _dot_nt(a, b):
    return lax.dot_general(a, b, (((1,), (1,)), ((), ())), preferred_element_type=F32)


def _dot_tn(a, b):
    return lax.dot_general(a, b, (((0,), (0,)), ((), ())), preferred_element_type=F32)


def _dot_split(t, x, terms):
    acc = None
    rem = x
    for i in range(terms):
        part = rem.astype(BF16)
        d = _dot(t, part)
        acc = d if acc is None else acc + d
        if i + 1 < terms:
            rem = rem - part.astype(F32)
    return acc


def _log_sigmoid(x):
    return jnp.minimum(x, 0.0) - jnp.log(1.0 + jnp.exp(-jnp.abs(x)))


def _silu(x):
    return x * jax.nn.sigmoid(x)


def _layer_norm(r, g, b):
    mu = jnp.mean(r, axis=-1, keepdims=True)
    d = r - mu
    var = jnp.mean(d * d, axis=-1, keepdims=True)
    return d * lax.rsqrt(var + LN_EPS) * g + b


def _head_norm(x, g):
    mu = jnp.mean(x, axis=-1, keepdims=True)
    d = x - mu
    var = jnp.mean(d * d, axis=-1, keepdims=True)
    return d * lax.rsqrt(var + LN_EPS) * g


def _iota(shape, dim):
    return lax.broadcasted_iota(jnp.int32, shape, dim)


def _params(*sem):
    return pltpu.CompilerParams(dimension_semantics=sem, vmem_limit_bytes=VMEM_LIMIT)


def _ada_kernel(c_ref, w_ref, b_ref, o_ref):
    ca = _silu(c_ref[...]).astype(BF16)
    o_ref[0] = _dot(ca, w_ref[0].astype(BF16)) + b_ref[0]


def _ada_mod(c, ada_w, ada_b):
    bsz, d = c.shape
    n = ada_w.shape[-1]
    tn = n // 4
    return pl.pallas_call(
        _ada_kernel,
        out_shape=jax.ShapeDtypeStruct((DEPTH, bsz, n), F32),
        grid=(DEPTH, n // tn),
        in_specs=[
            pl.BlockSpec((bsz, d), lambda l, j: (0, 0)),
            pl.BlockSpec((1, d, tn), lambda l, j: (l, 0, j)),
            pl.BlockSpec((1, 1, tn), lambda l, j: (l, 0, j)),
        ],
        out_specs=pl.BlockSpec((1, bsz, tn), lambda l, j: (l, 0, j)),
        compiler_params=_params("arbitrary", "arbitrary"),
        name="ada_mod",
    )(c, ada_w, ada_b.reshape(DEPTH, 1, n))


def _inproj_kernel(x_ref, mod_ref, w_ref, *o_refs, shift_row, scale_row):
    sh = mod_ref[0, shift_row:shift_row + 1, :]
    sc = mod_ref[0, scale_row:scale_row + 1, :]
    hm = (x_ref[0] * (1.0 + sc) + sh).astype(BF16)
    col = 0
    for o_ref in o_refs:
        n = o_ref.shape[-1]
        o_ref[0] = _dot(hm, w_ref[:, col:col + n]).astype(o_ref.dtype)
        col += n


def _inproj(x, mod, w, outs, tm, shift_row, scale_row, name):
    bsz, s, d = x.shape
    n = w.shape[1]
    assert sum(o[0] for o in outs) == n
    return pl.pallas_call(
        functools.partial(_inproj_kernel, shift_row=shift_row, scale_row=scale_row),
        out_shape=[jax.ShapeDtypeStruct((bsz, s, o[0]), o[1]) for o in outs],
        grid=(bsz, s // tm),
        in_specs=[
            pl.BlockSpec((1, tm, d), lambda b, i: (b, i, 0)),
            pl.BlockSpec((1, 6, d), lambda b, i: (b, 0, 0)),
            pl.BlockSpec((d, n), lambda b, i: (0, 0)),
        ],
        out_specs=[pl.BlockSpec((1, tm, o[0]), lambda b, i: (b, i, 0)) for o in outs],
        compiler_params=_params("parallel", "arbitrary"),
        name=name,
    )(x, mod, w)


def _outproj_kernel(*refs, n_act, gate_row):
    act_refs = refs[:n_act]
    w_refs = refs[n_act:2 * n_act]
    x_ref, mod_ref, g_ref, b_ref, o_ref = refs[2 * n_act:]
    y = None
    for a_ref, w_ref in zip(act_refs, w_refs):
        t = _dot(a_ref[0], w_ref[...])
        y = t if y is None else y + t
    gate = mod_ref[0, gate_row:gate_row + 1, :]
    r = RES_ALPHA * x_ref[0] + (1.0 + gate) * y
    o_ref[0] = _layer_norm(r, g_ref[...], b_ref[...])


def _outproj_ln(acts, ws, x, mod, ln_g, ln_b, tm, gate_row, name):
    bsz, s, d = x.shape
    n_act = len(acts)
    in_specs = [pl.BlockSpec((1, tm, a.shape[-1]), lambda b, i: (b, i, 0)) for a in acts]
    in_specs += [pl.BlockSpec(w.shape, lambda b, i: (0, 0)) for w in ws]
    in_specs += [
        pl.BlockSpec((1, tm, d), lambda b, i: (b, i, 0)),
        pl.BlockSpec((1, 6, d), lambda b, i: (b, 0, 0)),
        pl.BlockSpec((1, d), lambda b, i: (0, 0)),
        pl.BlockSpec((1, d), lambda b, i: (0, 0)),
    ]
    return pl.pallas_call(
        functools.partial(_outproj_kernel, n_act=n_act, gate_row=gate_row),
        out_shape=jax.ShapeDtypeStruct((bsz, s, d), F32),
        grid=(bsz, s // tm),
        in_specs=in_specs,
        out_specs=pl.BlockSpec((1, tm, d), lambda b, i: (b, i, 0)),
        compiler_params=_params("parallel", "arbitrary"),
        name=name,
    )(*acts, *ws, x, mod, ln_g.reshape(1, d), ln_b.reshape(1, d))


def _ffn_kernel(x_ref, mod_ref, w1_ref, w3_ref, w2_ref, g_ref, b_ref, o_ref, acc_ref):
    x = x_ref[0]
    sh = mod_ref[0, 3:4, :]
    sc = mod_ref[0, 4:5, :]
    gate = mod_ref[0, 5:6, :]
    hf = (x * (1.0 + sc) + sh).astype(BF16)
    acc_ref[...] = jnp.zeros_like(acc_ref)

    def body(c, carry):
        h1 = _dot(hf, w1_ref[c])
        h3 = _dot(hf, w3_ref[c])
        gm = (_silu(h1) * h3).astype(BF16)
        acc_ref[...] += _dot(gm, w2_ref[c])
        return carry

    lax.fori_loop(0, w1_ref.shape[0], body, 0)
    r = RES_ALPHA * x + (1.0 + gate) * acc_ref[...]
    o_ref[0] = _layer_norm(r, g_ref[...], b_ref[...])


def _ffn(x, mod, w1, w3, w2, ln_g, ln_b, tm):
    bsz, s, d = x.shape
    nc = D_FF // FF_CHUNK
    w1r = w1.astype(BF16).reshape(d, nc, FF_CHUNK).transpose(1, 0, 2)
    w3r = w3.astype(BF16).reshape(d, nc, FF_CHUNK).transpose(1, 0, 2)
    w2r = w2.astype(BF16).reshape(nc, FF_CHUNK, d)
    return pl.pallas_call(
        _ffn_kernel,
        out_shape=jax.ShapeDtypeStruct((bsz, s, d), F32),
        grid=(bsz, s // tm),
        in_specs=[
            pl.BlockSpec((1, tm, d), lambda b, i: (b, i, 0)),
            pl.BlockSpec((1, 6, d), lambda b, i: (b, 0, 0)),
            pl.BlockSpec((nc, d, FF_CHUNK), lambda b, i: (0, 0, 0)),
            pl.BlockSpec((nc, d, FF_CHUNK), lambda b, i: (0, 0, 0)),
            pl.BlockSpec((nc, FF_CHUNK, d), lambda b, i: (0, 0, 0)),
            pl.BlockSpec((1, d), lambda b, i: (0, 0)),
            pl.BlockSpec((1, d), lambda b, i: (0, 0)),
        ],
        out_specs=pl.BlockSpec((1, tm, d), lambda b, i: (b, i, 0)),
        scratch_shapes=[pltpu.VMEM((tm, d), F32)],
        compiler_params=_params("parallel", "arbitrary"),
        name="ffn",
    )(x, mod, w1r, w3r, w2r, ln_g.reshape(1, d), ln_b.reshape(1, d))


def _mixer0_kernel(p_ref, wa_ref, ba_ref, gg_ref, cw_ref, cb_ref, gb_ref, mg_ref, o_ref,
                   st_ref, cst_ref, m_ref, xc_ref):
    L = MIX_TILE
    i = pl.program_id(1)

    @pl.when(i == 0)
    def _():
        st_ref[...] = jnp.zeros_like(st_ref)
        cst_ref[...] = jnp.zeros_like(cst_ref)
        m_ref[...] = jnp.zeros_like(m_ref)
        xc_ref[0:8, :] = jnp.zeros((8, 2 * ML_W), F32)

    lane = _iota((1, LANES), 1)
    g_raw = p_ref[0, :, P0_G:P0_G + LANES]

    u = _dot(g_raw.astype(BF16), wa_ref[...]) + ba_ref[...]
    la = _log_sigmoid(u) * (1.0 / GLA_TAU)
    row = _iota((L, L), 0)
    col = _iota((L, L), 1)
    same_chunk = (row >> _CHUNK_SHIFT) == (col >> _CHUNK_SHIFT)
    tri_blk = jnp.where(same_chunk & (col <= row), 1.0, 0.0).astype(BF16)
    bcs = _dot_split(tri_blk, la, 3)
    q_in = p_ref[0, :, P0_QA:P0_QA + GLA_QK] * (GLA_DK ** -0.5) * jnp.exp(bcs)
    k_raw = p_ref[0, :, P0_KA:P0_KA + GLA_QK]
    k_in = k_raw * jnp.exp(-bcs)
    lane_qk = _iota((1, GLA_QK), 1)
    head_masks = [(lane_qk >> _CHUNK_SHIFT) == h for h in range(GLA_HEADS)]
    r4 = _iota((GLA_HEADS * GLA_CHUNK, GLA_CHUNK), 0)
    c4 = _iota((GLA_HEADS * GLA_CHUNK, GLA_CHUNK), 1)
    tril4 = c4 <= (r4 & (GLA_CHUNK - 1))
    oa_chunks = []
    for c in range(L // GLA_CHUNK):
        r0, r1 = c * GLA_CHUNK, (c + 1) * GLA_CHUNK
        b_c = bcs[r0:r1]
        bl = b_c[GLA_CHUNK - 1:GLA_CHUNK, :]
        q_c = q_in[r0:r1]
        k_c = k_in[r0:r1].astype(BF16)
        k_end = k_raw[r0:r1] * jnp.exp(bl - b_c)
        v_c = p_ref[0, r0:r1, P0_VA:P0_VA + GLA_V]
        q_exp = jnp.concatenate([jnp.where(hm, q_c, 0.0) for hm in head_masks], axis=0).astype(BF16)
        k_exp = jnp.concatenate([jnp.where(hm, k_end, 0.0) for hm in head_masks], axis=0).astype(BF16)
        sc = jnp.where(tril4, _dot_nt(q_exp, k_c), 0.0)
        intra = _dot(sc.astype(BF16), v_c.astype(BF16))
        inter = _dot_nt(q_exp, st_ref[...].astype(BF16))
        o_heads = []
        for h in range(GLA_HEADS):
            h0, h1 = h * GLA_CHUNK, (h + 1) * GLA_CHUNK
            o_heads.append(intra[h0:h1, h * GLA_DV:(h + 1) * GLA_DV] + inter[h0:h1])
        oa_chunks.append(o_heads)
        v_cat = jnp.concatenate([v_c[:, h * GLA_DV:(h + 1) * GLA_DV] for h in range(GLA_HEADS)], axis=0)
        st_ref[...] = st_ref[...] * jnp.exp(bl) + _dot_tn(v_cat.astype(BF16), k_exp)
    for h in range(GLA_HEADS):
        o_h = jnp.concatenate([oc[h] for oc in oa_chunks], axis=0)
        cs = slice(h * GLA_DV, (h + 1) * GLA_DV)
        ra = p_ref[0, :, P0_RA + h * GLA_DV:P0_RA + (h + 1) * GLA_DV]
        o_ref[0, :, cs] = (_head_norm(o_h, gg_ref[:, cs]) * _silu(ra)).astype(o_ref.dtype)

    xc_ref[8:8 + L, :] = p_ref[0, :, P0_QKB:P0_QKB + 2 * ML_W]
    conv = cb_ref[...]
    for kk in range(MLSTM_CONV):
        conv = conv + cw_ref[kk:kk + 1, :] * xc_ref[8 - (MLSTM_CONV - 1) + kk:8 - (MLSTM_CONV - 1) + kk + L, :]
    xc_ref[0:8, :] = xc_ref[L:L + 8, :]
    qk_b = _silu(conv)

    gb = jnp.where(lane < G_A, g_raw + gb_ref[...], 0.0)
    ipre = gb
    logf = _log_sigmoid(pltpu.roll(gb, LANES - G_F, axis=1))
    logf = jnp.where(lane < MLSTM_HEADS, logf, 0.0)
    tri = jnp.where(col <= row, 1.0, 0.0).astype(BF16)
    bcum = _dot_split(tri, logf, 3)
    blast = bcum[L - 1:L, :]
    wend = blast - bcum + ipre
    m_prev = m_ref[...]
    m_new = jnp.maximum(blast + m_prev, jnp.max(wend, axis=0, keepdims=True))
    scl = jnp.exp(blast + m_prev - m_new)
    wj = jnp.exp(wend - m_new)
    inter_log = bcum + m_prev
    rows_src = jnp.where(lane < MLSTM_HEADS, ipre, pltpu.roll(bcum, MLSTM_HEADS, axis=1))
    rows_t = rows_src.T
    causal = col <= row
    e0 = jnp.where(_iota((L, MLSTM_DH), 1) == 0, 1.0, 0.0)
    for h in range(MLSTM_HEADS):
        hs = slice(h * MLSTM_DH, (h + 1) * MLSTM_DH)
        b_col = bcum[:, h:h + 1]
        ip_row = rows_t[h:h + 1, :]
        b_row = rows_t[MLSTM_HEADS + h:MLSTM_HEADS + h + 1, :]
        dlog = jnp.where(causal, b_col - b_row + ip_row, NEG_INF)
        il = inter_log[:, h:h + 1]
        m_i = jnp.maximum(il, jnp.max(dlog, axis=1, keepdims=True))
        w_intra = jnp.exp(dlog - m_i)
        s_inter = jnp.exp(il - m_i)
        q_h = qk_b[:, h * MLSTM_DH:(h + 1) * MLSTM_DH].astype(BF16)
        k_h = (qk_b[:, ML_W + h * MLSTM_DH:ML_W + (h + 1) * MLSTM_DH] * (MLSTM_DH ** -0.5)).astype(BF16)
        v_h = p_ref[0, :, P0_VB + h * MLSTM_DH:P0_VB + (h + 1) * MLSTM_DH]
        v_aug = jnp.concatenate([v_h, e0], axis=1)
        a = (w_intra * _dot_nt(q_h, k_h)).astype(BF16)
        c_prev = cst_ref[h]
        tot = _dot(a, v_aug.astype(BF16)) + s_inter * _dot_nt(q_h, c_prev.astype(BF16))
        num = tot[:, :MLSTM_DH]
        den = tot[:, MLSTM_DH:MLSTM_DH + 1]
        hid = num / jnp.maximum(jnp.abs(den), jnp.exp(-m_i))
        w_aug = (v_aug * wj[:, h:h + 1]).astype(BF16)
        cst_ref[h] = scl[:, h:h + 1] * c_prev + _dot_tn(w_aug, k_h)
        ob = p_ref[0, :, P0_OB + h * MLSTM_DH:P0_OB + (h + 1) * MLSTM_DH]
        o_ref[0, :, GLA_V + h * MLSTM_DH:GLA_V + (h + 1) * MLSTM_DH] = (
            jax.nn.sigmoid(ob) * _head_norm(hid, mg_ref[:, hs])).astype(o_ref.dtype)
    m_ref[...] = m_new


def _mixer0(proj, wa_up, ba, gla_g, conv_w, conv_b, b_i, b_f, ml_g):
    bsz, s, n = proj.shape
    L = MIX_TILE
    wa_pad = jnp.zeros((LANES, GLA_QK), F32).at[G_A:G_A + GLA_GATE_RANK].set(wa_up).astype(BF16)
    gbias = jnp.zeros((1, LANES), F32).at[0, G_I:G_I + MLSTM_HEADS].set(b_i).at[0, G_F:G_F + MLSTM_HEADS].set(b_f)
    full = lambda shape: pl.BlockSpec(shape, lambda b, i: (0,) * len(shape))
    return pl.pallas_call(
        _mixer0_kernel,
        out_shape=jax.ShapeDtypeStruct((bsz, s, GLA_V + ML_W), BF16),
        grid=(bsz, s // L),
        in_specs=[
            pl.BlockSpec((1, L, n), lambda b, i: (b, i, 0)),
            full((LANES, GLA_QK)),
            full((1, GLA_QK)),
            full((1, GLA_V)),
            full((MLSTM_CONV, 2 * ML_W)),
            full((1, 2 * ML_W)),
            full((1, LANES)),
            full((1, ML_W)),
        ],
        out_specs=pl.BlockSpec((1, L, GLA_V + ML_W), lambda b, i: (b, i, 0)),
        scratch_shapes=[
            pltpu.VMEM((GLA_DV, GLA_QK), F32),
            pltpu.VMEM((MLSTM_HEADS, 2 * MLSTM_DH, MLSTM_DH), F32),
            pltpu.VMEM((1, LANES), F32),
            pltpu.VMEM((8 + L, 2 * ML_W), F32),
        ],
        compiler_params=_params("parallel", "arbitrary"),
        name="mixer0",
    )(proj, wa_pad, ba.reshape(1, GLA_QK), gla_g.reshape(1, GLA_V), conv_w, conv_b.reshape(1, 2 * ML_W),
      gbias, ml_g.reshape(1, ML_W))


def _sb_kernel(q_ref, k_ref, v_ref, o_ref, qs_ref, acc_ref, car_ref, mar_ref, kn_ref):
    TQ, TK = SB_TQ, SB_TK
    nd = TQ // TK
    RB = 2 * TK
    i = pl.program_id(2)
    lane = _iota((1, LANES), 1)
    head0 = (lane >> _HEAD_SHIFT) == 0

    @pl.when(i == 0)
    def _():
        k2 = jnp.square(k_ref[0].astype(F32))
        for h in range(2):
            n2 = jnp.sum(jnp.where(head0 if h == 0 else ~head0, k2, 0.0), axis=1, keepdims=True)
            kn_ref[h:h + 1, :] = jnp.broadcast_to(jnp.sqrt(jnp.max(n2, axis=0, keepdims=True)), (1, LANES))

    for a in range(nd):
        qa = q_ref[0, a * TK:(a + 1) * TK, :].astype(F32) * (SB_DH ** -0.5 * LOG2E)
        for h in range(2):
            rows = slice(a * RB + h * TK, a * RB + (h + 1) * TK)
            qh = jnp.where(head0 if h == 0 else ~head0, qa, 0.0).astype(BF16)
            qs_ref[rows, :] = qh
            qn = jnp.sqrt(jnp.sum(jnp.square(qh.astype(F32)), axis=1, keepdims=True))
            mar_ref[rows, :] = qn * kn_ref[h:h + 1, :] * SB_BOUND_SLACK + SB_ZERO_BITS
    acc_ref[...] = jnp.zeros_like(acc_ref)
    car_ref[...] = jnp.zeros_like(car_ref)
    tri = jnp.where(_iota((TK, TK), 0) >= _iota((TK, TK), 1), -1.0, 0.0).astype(BF16)

    def logits(kb, a_lo, a_hi):
        r0 = pl.multiple_of(kb * TK, TK)
        return _dot_nt(qs_ref[a_lo * RB:a_hi * RB, :], k_ref[0, pl.ds(r0, TK), :])

    def tile(z, kb, a_lo, a_hi, diagonal):
        rows = slice(a_lo * RB, a_hi * RB)
        n = (a_hi - a_lo) * RB
        vt = v_ref[0, pl.ds(pl.multiple_of(kb * TK, TK), TK), :]
        sp = jnp.maximum(z, 0.0) + jnp.log(1.0 + jnp.exp2(-jnp.abs(z))) * LOG2E
        if diagonal:
            mask = _iota((n, TK), 1) < (_iota((n, TK), 0) & (TK - 1))
            sp = jnp.where(mask, sp, 0.0)
        hi = sp.astype(BF16)
        lo = (sp - hi.astype(F32)).astype(BF16)
        res = _dot(hi, tri) + _dot(lo, tri)
        car = car_ref[rows, :]
        att = jnp.exp2(z + res - jnp.concatenate([car] * (TK // LANES), axis=1))
        if diagonal:
            att = jnp.where(mask, att, 0.0)
        acc_ref[rows, :] += _dot(att.astype(BF16), vt)
        car_ref[rows, :] = car + jnp.sum(sp, axis=1, keepdims=True)

    n_off = nd * i

    def near_tiles(with_off):
        jobs = []
        for d in reversed(range(nd)):
            jobs.append((nd * i + d, d, d + 1, True))
            if d + 1 < nd:
                jobs.append((nd * i + d, d + 1, nd, False))
        if with_off:
            jobs.append((n_off - 1, 0, nd, False))
        zs = [logits(kb, a_lo, a_hi) for kb, a_lo, a_hi, _ in jobs]
        for z, (kb, a_lo, a_hi, diagonal) in zip(zs, jobs):
            tile(z, kb, a_lo, a_hi, diagonal)

    pl.when(i == 0)(functools.partial(near_tiles, False))
    pl.when(i > 0)(functools.partial(near_tiles, True))

    def pending():
        return jnp.max(mar_ref[...] - car_ref[...]) > 0.0

    def far_body(c):
        kb, _ = c
        tile(logits(kb, 0, nd), kb, 0, nd, False)
        return kb - 1, pending()

    lax.while_loop(lambda c: jnp.logical_and(c[0] >= 0, c[1]), far_body, (n_off - 2, pending()))
    for a in range(nd):
        o_ref[0, a * TK:(a + 1) * TK, :] = jnp.where(
            (lane >> _HEAD_SHIFT) == 0, acc_ref[a * RB:a * RB + TK, :], acc_ref[a * RB + TK:(a + 1) * RB, :]
        ).astype(o_ref.dtype)


def _stick_breaking(pc):
    bsz, s, _ = pc.shape
    npair = SB_W // LANES
    return pl.pallas_call(
        _sb_kernel,
        out_shape=jax.ShapeDtypeStruct((bsz, s, SB_W), BF16),
        grid=(bsz, npair, s // SB_TQ),
        in_specs=[
            pl.BlockSpec((1, SB_TQ, LANES), lambda b, p, i: (b, i, p)),
            pl.BlockSpec((1, s, LANES), lambda b, p, i: (b, 0, npair + p)),
            pl.BlockSpec((1, s, LANES), lambda b, p, i: (b, 0, 2 * npair + p)),
        ],
        out_specs=pl.BlockSpec((1, SB_TQ, LANES), lambda b, p, i: (b, i, p)),
        scratch_shapes=[
            pltpu.VMEM((2 * SB_TQ, LANES), BF16),
            pltpu.VMEM((2 * SB_TQ, LANES), F32),
            pltpu.VMEM((2 * SB_TQ, LANES), F32),
            pltpu.VMEM((2 * SB_TQ, LANES), F32),
            pltpu.VMEM((8, LANES), F32),
        ],
        compiler_params=_params("parallel", "parallel", "arbitrary"),
        name="stick_breaking",
    )(pc, pc, pc)


def _t5_bucket(dist):
    max_exact = N_REL_BUCKETS // 2
    dd = np.maximum(dist, 1).astype(np.float64)
    large = max_exact + (np.log(dd / max_exact) / np.log(REL_MAX_DIST / max_exact)
                         * (N_REL_BUCKETS - max_exact)).astype(np.int32)
    large = np.minimum(large, N_REL_BUCKETS - 1)
    return np.where(dist < max_exact, dist, large).astype(np.int32)


def _dil_buckets():
    qi = np.arange(DIL_BLK)[:, None]
    kj = np.arange(2 * DIL_BLK)[None, :]
    delta = qi - kj + DIL_BLK
    in_win = (delta >= 0) & (delta <= DIL_BLK)
    tabs = []
    for window, dil in DIL_PAIRS:
        assert window // dil == DIL_BLK
        bucket = _t5_bucket(np.clip(delta, 0, None) * dil)
        tabs.append(np.where(in_win, bucket, -1).astype(np.int32))
    return np.stack(tabs, 0)


def _dil_kernel(q_ref, k_ref, v_ref, bkt_ref, tab_ref, o_ref, m_sc, l_sc, acc_sc):
    s_len = q_ref.shape[1]
    hp = pl.program_id(1)
    g = pl.program_id(2)
    blk = DIL_BLK
    lane = _iota((1, LANES), 1)
    lane_head = lane >> _HEAD_SHIFT

    @pl.when(g == 0)
    def _():
        m_sc[...] = jnp.full_like(m_sc, NEG_INF)
        l_sc[...] = jnp.zeros_like(l_sc)
        acc_sc[...] = jnp.zeros_like(acc_sc)

    kcol = _iota((2 * blk, 2 * blk), 1)

    def group(gi, dil):
        nb = s_len // (blk * dil)
        bkt = bkt_ref[gi]
        biases = []
        for hl in range(2):
            head = gi * DIL_HEADS_PER_GROUP + hp * 2 + hl
            bias = jnp.full((blk, 2 * blk), NEG_INF, F32)
            for bk in range(N_REL_BUCKETS):
                bias = jnp.where(bkt == bk, tab_ref[bk, head], bias)
            biases.append(bias)
        bias2 = jnp.concatenate(biases, axis=0)
        sel0 = lane_head == 0

        def stage_logits(t):
            r = t // nb
            n = t % nb
            q_start = r + dil * blk * n
            p_start = r + dil * blk * jnp.maximum(n - 1, 0)
            rows_q = pl.ds(q_start, blk, stride=dil) if dil > 1 else pl.ds(q_start, blk)
            rows_p = pl.ds(p_start, blk, stride=dil) if dil > 1 else pl.ds(p_start, blk)
            qb = q_ref[0, rows_q, :] * (DIL_DH ** -0.5)
            q2 = jnp.concatenate([jnp.where(sel0, qb, 0.0), jnp.where(sel0, 0.0, qb)], axis=0).astype(BF16)
            kk = jnp.concatenate([k_ref[0, rows_p, :], k_ref[0, rows_q, :]], axis=0).astype(BF16)
            logits = _dot_nt(q2, kk) + bias2
            logits = jnp.where(jnp.logical_and(n == 0, kcol < blk), NEG_INF, logits)
            return rows_q, rows_p, logits

        def stage_max(rows_q, logits):
            m_old = m_sc[rows_q, :]
            m_old_col = jnp.concatenate([m_old[:, 0:1], m_old[:, DIL_DH:DIL_DH + 1]], axis=0)
            m_col = jnp.maximum(m_old_col, jnp.max(logits, axis=1, keepdims=True))
            return m_old, m_col

        def stage_pv(rows_q, rows_p, logits, m_col):
            p = jnp.exp(logits - m_col)
            vv = jnp.concatenate([v_ref[0, rows_p, :], v_ref[0, rows_q, :]], axis=0).astype(BF16)
            return jnp.sum(p, axis=1, keepdims=True), _dot(p.astype(BF16), vv)

        def body(tt, carry):
            ts = [tt * DIL_UNROLL + u for u in range(DIL_UNROLL)]
            s1 = [stage_logits(t) for t in ts]
            s2 = [stage_max(rq, lg) for rq, _, lg in s1]
            s3 = [stage_pv(rq, rp, lg, mc) for (rq, rp, lg), (_, mc) in zip(s1, s2)]
            outs = []
            for (rows_q, _, _), (m_old, m_col), (ps, pv) in zip(s1, s2, s3):
                m_new = jnp.where(sel0, m_col[0:blk], m_col[blk:])
                alpha = jnp.exp(m_old - m_new)
                l_new = alpha * l_sc[rows_q, :] + jnp.where(sel0, ps[0:blk], ps[blk:])
                a_new = alpha * acc_sc[rows_q, :] + jnp.where(sel0, pv[0:blk], pv[blk:])
                outs.append((rows_q, m_new, l_new, a_new))
            for rows_q, m_new, l_new, a_new in outs:
                m_sc[rows_q, :] = m_new
                l_sc[rows_q, :] = l_new
                acc_sc[rows_q, :] = a_new
            return carry

        lax.fori_loop(0, s_len // (blk * DIL_UNROLL), body, 0)

    for gi, (_, dil) in enumerate(DIL_PAIRS):
        pl.when(g == gi)(functools.partial(group, gi, dil))

    @pl.when(g == len(DIL_PAIRS) - 1)
    def _():
        o_ref[0] = (acc_sc[...] / l_sc[...]).astype(o_ref.dtype)


def _dilated(pd, rel_bias):
    bsz, s, _ = pd.shape
    ng = len(DIL_PAIRS)
    gw = DIL_HEADS_PER_GROUP * DIL_DH // LANES
    nq = DIL_W // LANES
    bkt = jnp.asarray(_dil_buckets())
    blk_spec = lambda base: pl.BlockSpec((1, s, LANES), lambda b, p, g: (b, 0, base + g * gw + p))
    return pl.pallas_call(
        _dil_kernel,
        out_shape=jax.ShapeDtypeStruct((bsz, s, DIL_OUT), BF16),
        grid=(bsz, gw, ng),
        in_specs=[
            blk_spec(0),
            blk_spec(nq),
            blk_spec(2 * nq),
            pl.BlockSpec((ng, DIL_BLK, 2 * DIL_BLK), lambda b, p, g: (0, 0, 0)),
            pl.BlockSpec(memory_space=pltpu.SMEM),
        ],
        out_specs=pl.BlockSpec((1, s, LANES), lambda b, p, g: (b, 0, p)),
        scratch_shapes=[pltpu.VMEM((s, LANES), F32)] * 3,
        compiler_params=_params("parallel", "parallel", "arbitrary"),
        name="dilated",
    )(pd, pd, pd, bkt, rel_bias)


def _ab_weight(w):
    o = np.cumsum((0,) + (GLA_QK, GLA_QK, GLA_V, GLA_V, GLA_GATE_RANK, 2 * ML_W, ML_W, MLSTM_HEADS, MLSTM_HEADS, ML_W))
    qa, ka, va, ra, aa, qkb, vb, ib, fb, ob = [w[:, o[j]:o[j + 1]] for j in range(10)]
    pad = jnp.zeros((w.shape[0], LANES - 2 * MLSTM_HEADS - GLA_GATE_RANK), w.dtype)
    return jnp.concatenate([qa, ka, va, ra, qkb, vb, ob, ib, fb, aa, pad], axis=1).astype(BF16)


def kernel(x, c, ada_w, ada_b, ln_g, ln_b, ab_w_in, gla_wa_up, gla_ba, gla_norm_g, ml_conv_w, ml_conv_b,
           ml_b_i, ml_b_f, ml_norm_g, ab_w_out, cd_w_in, rel_bias, cd_w_out, ffn_w1, ffn_w3, ffn_w2):
    bsz, s, d = x.shape
    mod_all = _ada_mod(c, ada_w, ada_b).reshape(DEPTH, bsz, 6, d)
    tm = 512
    for layer in range(DEPTH):
        mod = mod_all[layer]
        j = layer // 2
        if layer % 2 == 0:
            (proj,) = _inproj(x, mod, _ab_weight(ab_w_in[j]), [(P0_N, F32)], tm, 0, 1, "inproj0")
            cat = _mixer0(proj, gla_wa_up[j], gla_ba[j], gla_norm_g[j], ml_conv_w[j], ml_conv_b[j],
                          ml_b_i[j], ml_b_f[j], ml_norm_g[j])
            x = _outproj_ln([cat], [ab_w_out[j].astype(BF16)], x, mod, ln_g[layer, 0], ln_b[layer, 0],
                            tm, 2, "outproj0")
        else:
            pc, pd = _inproj(x, mod, cd_w_in[j].astype(BF16), [(3 * SB_W, BF16), (3 * DIL_W, F32)],
                             tm, 0, 1, "inproj1")
            oc = _stick_breaking(pc)
            od = _dilated(pd, rel_bias)
            w_out = cd_w_out[j].astype(BF16)
            x = _outproj_ln([oc, od], [w_out[:SB_W], w_out[SB_W:]], x, mod, ln_g[layer, 0], ln_b[layer, 0],
                            tm, 2, "outproj1")
        x = _ffn(x, mod, ffn_w1[layer], ffn_w3[layer], ffn_w2[layer], ln_g[layer, 1], ln_b[layer, 1], tm)
    return x
```

```python
import functools

import numpy as np
import jax
import jax.numpy as jnp
from jax import lax
from jax.experimental import pallas as pl
from jax.experimental.pallas import tpu as pltpu

F32 = jnp.float32
BF16 = jnp.bfloat16

D_MODEL = 1024
DEPTH = 2
GLA_HEADS = 4
GLA_DK = 64
GLA_DV = 128
GLA_GATE_RANK = 16
GLA_TAU = 16.0
GLA_CHUNK = 64
MLSTM_HEADS = 4
MLSTM_DH = 128
MLSTM_CONV = 4
SB_HEADS = 8
SB_DH = 64
DIL_PAIRS = ((128, 1), (512, 4), (2048, 16))
DIL_HEADS_PER_GROUP = 4
DIL_DH = 64
DIL_BLK = 128
DIL_UNROLL = 4
N_REL_BUCKETS = 32
REL_MAX_DIST = 2048
D_FF = ((8 * D_MODEL + 3 * 256 - 1) // (3 * 256)) * 256
LN_EPS = 1e-5
RES_ALPHA = (2 * DEPTH) ** 0.25
NEG_INF = -1e30

GLA_QK = GLA_HEADS * GLA_DK
GLA_V = GLA_HEADS * GLA_DV
ML_W = MLSTM_HEADS * MLSTM_DH
SB_W = SB_HEADS * SB_DH
DIL_W = len(DIL_PAIRS) * DIL_HEADS_PER_GROUP * DIL_DH
DIL_OUT = DIL_HEADS_PER_GROUP * DIL_DH

LANES = 128
VMEM_LIMIT = 56 * 1024 * 1024

P0_QA = 0
P0_KA = P0_QA + GLA_QK
P0_VA = P0_KA + GLA_QK
P0_RA = P0_VA + GLA_V
P0_QKB = P0_RA + GLA_V
P0_VB = P0_QKB + 2 * ML_W
P0_OB = P0_VB + ML_W
P0_G = P0_OB + ML_W
P0_N = P0_G + LANES
G_I = 0
G_F = MLSTM_HEADS
G_A = 2 * MLSTM_HEADS

MIX_TILE = 256
FF_CHUNK = 256
FFN_TM = 512
SB_TQ = 512
SB_TK = 256
SB_ZERO_BITS = 160.0
SB_BOUND_SLACK = 1.01
LOG2E = 1.4426950408889634
_CHUNK_SHIFT = 6
_HEAD_SHIFT = 6
assert GLA_CHUNK == GLA_DK == 1 << _CHUNK_SHIFT and SB_DH == DIL_DH == 1 << _HEAD_SHIFT


def _dot(a, b):
    return jnp.dot(a, b, preferred_element_type=F32)


def _dot_nt(a, b):
    return lax.dot_general(a, b, (((1,), (1,)), ((), ())), preferred_element_type=F32)


def _dot_tn(a, b):
    return lax.dot_general(a, b, (((0,), (0,)), ((), ())), preferred_element_type=F32)


def _dot_split(t, x, terms):
    acc = None
    rem = x
    for i in range(terms):
        part = rem.astype(BF16)
        d = _dot(t, part)
        acc = d if acc is None else acc + d
        if i + 1 < terms:
            rem = rem - part.astype(F32)
    return acc


def _log_sigmoid(x):
    return jnp.minimum(x, 0.0) - jnp.log(1.0 + jnp.exp(-jnp.abs(x)))


def _silu(x):
    return x * jax.nn.sigmoid(x)


def _layer_norm(r, g, b):
    mu = jnp.mean(r, axis=-1, keepdims=True)
    d = r - mu
    var = jnp.mean(d * d, axis=-1, keepdims=True)
    return d * lax.rsqrt(var + LN_EPS) * g + b


def _head_norm(x, g):
    mu = jnp.mean(x, axis=-1, keepdims=True)
    d = x - mu
    var = jnp.mean(d * d, axis=-1, keepdims=True)
    return d * lax.rsqrt(var + LN_EPS) * g


def _iota(shape, dim):
    return lax.broadcasted_iota(jnp.int32, shape, dim)


def _params(*sem):
    return pltpu.CompilerParams(dimension_semantics=sem, vmem_limit_bytes=VMEM_LIMIT)


def _ada_kernel(c_ref, w_ref, b_ref, o_ref):
    ca = _silu(c_ref[...]).astype(BF16)
    o_ref[0] = _dot(ca, w_ref[0].astype(BF16)) + b_ref[0]


def _ada_mod(c, ada_w, ada_b):
    bsz, d = c.shape
    n = ada_w.shape[-1]
    tn = n // 4
    return pl.pallas_call(
        _ada_kernel,
        out_shape=jax.ShapeDtypeStruct((DEPTH, bsz, n), F32),
        grid=(DEPTH, n // tn),
        in_specs=[
            pl.BlockSpec((bsz, d), lambda l, j: (0, 0)),
            pl.BlockSpec((1, d, tn), lambda l, j: (l, 0, j)),
            pl.BlockSpec((1, 1, tn), lambda l, j: (l, 0, j)),
        ],
        out_specs=pl.BlockSpec((1, bsz, tn), lambda l, j: (l, 0, j)),
        compiler_params=_params("arbitrary", "arbitrary"),
        name="ada_mod",
    )(c, ada_w, ada_b.reshape(DEPTH, 1, n))


def _inproj_kernel(x_ref, mod_ref, w_ref, *o_refs, shift_row, scale_row):
    sh = mod_ref[0, shift_row:shift_row + 1, :]
    sc = mod_ref[0, scale_row:scale_row + 1, :]
    hm = (x_ref[0] * (1.0 + sc) + sh).astype(BF16)
    col = 0
    for o_ref in o_refs:
        n = o_ref.shape[-1]
        o_ref[0] = _dot(hm, w_ref[:, col:col + n]).astype(o_ref.dtype)
        col += n


def _inproj(x, mod, w, outs, tm, shift_row, scale_row, name):
    bsz, s, d = x.shape
    n = w.shape[1]
    assert sum(o[0] for o in outs) == n
    return pl.pallas_call(
        functools.partial(_inproj_kernel, shift_row=shift_row, scale_row=scale_row),
        out_shape=[jax.ShapeDtypeStruct((bsz, s, o[0]), o[1]) for o in outs],
        grid=(bsz, s // tm),
        in_specs=[
            pl.BlockSpec((1, tm, d), lambda b, i: (b, i, 0)),
            pl.BlockSpec((1, 6, d), lambda b, i: (b, 0, 0)),
            pl.BlockSpec((d, n), lambda b, i: (0, 0)),
        ],
        out_specs=[pl.BlockSpec((1, tm, o[0]), lambda b, i: (b, i, 0)) for o in outs],
        compiler_params=_params("parallel", "arbitrary"),
        name=name,
    )(x, mod, w)


def _outproj_kernel(*refs, n_act, gate_row):
    act_refs = refs[:n_act]
    w_refs = refs[n_act:2 * n_act]
    x_ref, mod_ref, g_ref, b_ref, o_ref = refs[2 * n_act:]
    y = None
    for a_ref, w_ref in zip(act_refs, w_refs):
        t = _dot(a_ref[0], w_ref[...])
        y = t if y is None else y + t
    gate = mod_ref[0, gate_row:gate_row + 1, :]
    r = RES_ALPHA * x_ref[0] + (1.0 + gate) * y
    o_ref[0] = _layer_norm(r, g_ref[...], b_ref[...])


def _outproj_ln(acts, ws, x, mod, ln_g, ln_b, tm, gate_row, name):
    bsz, s, d = x.shape
    n_act = len(acts)
    in_specs = [pl.BlockSpec((1, tm, a.shape[-1]), lambda b, i: (b, i, 0)) for a in acts]
    in_specs += [pl.BlockSpec(w.shape, lambda b, i: (0, 0)) for w in ws]
    in_specs += [
        pl.BlockSpec((1, tm, d), lambda b, i: (b, i, 0)),
        pl.BlockSpec((1, 6, d), lambda b, i: (b, 0, 0)),
        pl.BlockSpec((1, d), lambda b, i: (0, 0)),
        pl.BlockSpec((1, d), lambda b, i: (0, 0)),
    ]
    return pl.pallas_call(
        functools.partial(_outproj_kernel, n_act=n_act, gate_row=gate_row),
        out_shape=jax.ShapeDtypeStruct((bsz, s, d), F32),
        grid=(bsz, s // tm),
        in_specs=in_specs,
        out_specs=pl.BlockSpec((1, tm, d), lambda b, i: (b, i, 0)),
        compiler_params=_params("parallel", "arbitrary"),
        name=name,
    )(*acts, *ws, x, mod, ln_g.reshape(1, d), ln_b.reshape(1, d))


def _ffn_kernel(x_ref, mod_ref, w1_ref, w3_ref, w2_ref, g_ref, b_ref, o_ref, gm_ref):
    x = x_ref[0]
    sh = mod_ref[0, 3:4, :]
    sc = mod_ref[0, 4:5, :]
    gate = mod_ref[0, 5:6, :]
    hf = (x * (1.0 + sc) + sh).astype(BF16)
    for c in range(D_FF // FF_CHUNK):
        cs = slice(c * FF_CHUNK, (c + 1) * FF_CHUNK)
        gm_ref[:, cs] = (_silu(_dot(hf, w1_ref[:, cs])) * _dot(hf, w3_ref[:, cs])).astype(BF16)
    y = _dot(gm_ref[...], w2_ref[...])
    r = RES_ALPHA * x + (1.0 + gate) * y
    o_ref[0] = _layer_norm(r, g_ref[...], b_ref[...])


def _ffn(x, mod, w1, w3, w2, ln_g, ln_b, tm):
    bsz, s, d = x.shape
    const = lambda shape: pl.BlockSpec(shape, lambda b, i: (0,) * len(shape), pipeline_mode=pl.Buffered(1))
    return pl.pallas_call(
        _ffn_kernel,
        out_shape=jax.ShapeDtypeStruct((bsz, s, d), F32),
        grid=(bsz, s // tm),
        in_specs=[
            pl.BlockSpec((1, tm, d), lambda b, i: (b, i, 0)),
            pl.BlockSpec((1, 6, d), lambda b, i: (b, 0, 0)),
            const((d, D_FF)),
            const((d, D_FF)),
            const((D_FF, d)),
            const((1, d)),
            const((1, d)),
        ],
        out_specs=pl.BlockSpec((1, tm, d), lambda b, i: (b, i, 0)),
        scratch_shapes=[pltpu.VMEM((tm, D_FF), BF16)],
        compiler_params=_params("parallel", "arbitrary"),
        name="ffn",
    )(x, mod, w1.astype(BF16), w3.astype(BF16), w2.astype(BF16), ln_g.reshape(1, d), ln_b.reshape(1, d))


def _mixer0_kernel(p_ref, wa_ref, ba_ref, gg_ref, cw_ref, cb_ref, gb_ref, mg_ref, o_ref,
                   st_ref, cst_ref, m_ref, xc_ref):
    L = MIX_TILE
    i = pl.program_id(1)

    @pl.when(i == 0)
    def _():
        st_ref[...] = jnp.zeros_like(st_ref)
        cst_ref[...] = jnp.zeros_like(cst_ref)
        m_ref[...] = jnp.zeros_like(m_ref)
        xc_ref[0:8, :] = jnp.zeros((8, 2 * ML_W), F32)

    lane = _iota((1, LANES), 1)
    g_raw = p_ref[0, :, P0_G:P0_G + LANES]

    u = _dot(g_raw.astype(BF16), wa_ref[...]) + ba_ref[...]
    la = _log_sigmoid(u) * (1.0 / GLA_TAU)
    row = _iota((L, L), 0)
    col = _iota((L, L), 1)
    same_chunk = (row >> _CHUNK_SHIFT) == (col >> _CHUNK_SHIFT)
    tri_blk = jnp.where(same_chunk & (col <= row), 1.0, 0.0).astype(BF16)
    bcs = _dot_split(tri_blk, la, 3)
    q_in = p_ref[0, :, P0_QA:P0_QA + GLA_QK] * (GLA_DK ** -0.5) * jnp.exp(bcs)
    k_raw = p_ref[0, :, P0_KA:P0_KA + GLA_QK]
    k_in = k_raw * jnp.exp(-bcs)
    lane_qk = _iota((1, GLA_QK), 1)
    head_masks = [(lane_qk >> _CHUNK_SHIFT) == h for h in range(GLA_HEADS)]
    r4 = _iota((GLA_HEADS * GLA_CHUNK, GLA_CHUNK), 0)
    c4 = _iota((GLA_HEADS * GLA_CHUNK, GLA_CHUNK), 1)
    tril4 = c4 <= (r4 & (GLA_CHUNK - 1))
    oa_chunks = []
    for c in range(L // GLA_CHUNK):
        r0, r1 = c * GLA_CHUNK, (c + 1) * GLA_CHUNK
        b_c = bcs[r0:r1]
        bl = b_c[GLA_CHUNK - 1:GLA_CHUNK, :]
        q_c = q_in[r0:r1]
        k_c = k_in[r0:r1].astype(BF16)
        k_end = k_raw[r0:r1] * jnp.exp(bl - b_c)
        v_c = p_ref[0, r0:r1, P0_VA:P0_VA + GLA_V]
        q_exp = jnp.concatenate([jnp.where(hm, q_c, 0.0) for hm in head_masks], axis=0).astype(BF16)
        k_exp = jnp.concatenate([jnp.where(hm, k_end, 0.0) for hm in head_masks], axis=0).astype(BF16)
        sc = jnp.where(tril4, _dot_nt(q_exp, k_c), 0.0)
        intra = _dot(sc.astype(BF16), v_c.astype(BF16))
        inter = _dot_nt(q_exp, st_ref[...].astype(BF16))
        o_heads = []
        for h in range(GLA_HEADS):
            h0, h1 = h * GLA_CHUNK, (h + 1) * GLA_CHUNK
            o_heads.append(intra[h0:h1, h * GLA_DV:(h + 1) * GLA_DV] + inter[h0:h1])
        oa_chunks.append(o_heads)
        v_cat = jnp.concatenate([v_c[:, h * GLA_DV:(h + 1) * GLA_DV] for h in range(GLA_HEADS)], axis=0)
        st_ref[...] = st_ref[...] * jnp.exp(bl) + _dot_tn(v_cat.astype(BF16), k_exp)
    for h in range(GLA_HEADS):
        o_h = jnp.concatenate([oc[h] for oc in oa_chunks], axis=0)
        cs = slice(h * GLA_DV, (h + 1) * GLA_DV)
        ra = p_ref[0, :, P0_RA + h * GLA_DV:P0_RA + (h + 1) * GLA_DV]
        o_ref[0, :, cs] = (_head_norm(o_h, gg_ref[:, cs]) * _silu(ra)).astype(o_ref.dtype)

    xc_ref[8:8 + L, :] = p_ref[0, :, P0_QKB:P0_QKB + 2 * ML_W]
    conv = cb_ref[...]
    for kk in range(MLSTM_CONV):
        conv = conv + cw_ref[kk:kk + 1, :] * xc_ref[8 - (MLSTM_CONV - 1) + kk:8 - (MLSTM_CONV - 1) + kk + L, :]
    xc_ref[0:8, :] = xc_ref[L:L + 8, :]
    qk_b = _silu(conv)

    gb = jnp.where(lane < G_A, g_raw + gb_ref[...], 0.0)
    ipre = gb
    logf = _log_sigmoid(pltpu.roll(gb, LANES - G_F, axis=1))
    logf = jnp.where(lane < MLSTM_HEADS, logf, 0.0)
    tri = jnp.where(col <= row, 1.0, 0.0).astype(BF16)
    bcum = _dot_split(tri, logf, 3)
    blast = bcum[L - 1:L, :]
    wend = blast - bcum + ipre
    m_prev = m_ref[...]
    m_new = jnp.maximum(blast + m_prev, jnp.max(wend, axis=0, keepdims=True))
    scl = jnp.exp(blast + m_prev - m_new)
    wj = jnp.exp(wend - m_new)
    inter_log = bcum + m_prev
    rows_src = jnp.where(lane < MLSTM_HEADS, ipre, pltpu.roll(bcum, MLSTM_HEADS, axis=1))
    rows_t = rows_src.T
    causal = col <= row
    e0 = jnp.where(_iota((L, MLSTM_DH), 1) == 0, 1.0, 0.0)
    for h in range(MLSTM_HEADS):
        hs = slice(h * MLSTM_DH, (h + 1) * MLSTM_DH)
        b_col = bcum[:, h:h + 1]
        ip_row = rows_t[h:h + 1, :]
        b_row = rows_t[MLSTM_HEADS + h:MLSTM_HEADS + h + 1, :]
        dlog = jnp.where(causal, b_col - b_row + ip_row, NEG_INF)
        il = inter_log[:, h:h + 1]
        m_i = jnp.maximum(il, jnp.max(dlog, axis=1, keepdims=True))
        w_intra = jnp.exp(dlog - m_i)
        s_inter = jnp.exp(il - m_i)
        q_h = qk_b[:, h * MLSTM_DH:(h + 1) * MLSTM_DH].astype(BF16)
        k_h = (qk_b[:, ML_W + h * MLSTM_DH:ML_W + (h + 1) * MLSTM_DH] * (MLSTM_DH ** -0.5)).astype(BF16)
        v_h = p_ref[0, :, P0_VB + h * MLSTM_DH:P0_VB + (h + 1) * MLSTM_DH]
        v_aug = jnp.concatenate([v_h, e0], axis=1)
        a = (w_intra * _dot_nt(q_h, k_h)).astype(BF16)
        c_prev = cst_ref[h]
        tot = _dot(a, v_aug.astype(BF16)) + s_inter * _dot_nt(q_h, c_prev.astype(BF16))
        num = tot[:, :MLSTM_DH]
        den = tot[:, MLSTM_DH:MLSTM_DH + 1]
        hid = num / jnp.maximum(jnp.abs(den), jnp.exp(-m_i))
        w_aug = (v_aug * wj[:, h:h + 1]).astype(BF16)
        cst_ref[h] = scl[:, h:h + 1] * c_prev + _dot_tn(w_aug, k_h)
        ob = p_ref[0, :, P0_OB + h * MLSTM_DH:P0_OB + (h + 1) * MLSTM_DH]
        o_ref[0, :, GLA_V + h * MLSTM_DH:GLA_V + (h + 1) * MLSTM_DH] = (
            jax.nn.sigmoid(ob) * _head_norm(hid, mg_ref[:, hs])).astype(o_ref.dtype)
    m_ref[...] = m_new


def _mixer0(proj, wa_up, ba, gla_g, conv_w, conv_b, b_i, b_f, ml_g):
    bsz, s, n = proj.shape
    L = MIX_TILE
    wa_pad = jnp.zeros((LANES, GLA_QK), F32).at[G_A:G_A + GLA_GATE_RANK].set(wa_up).astype(BF16)
    gbias = jnp.zeros((1, LANES), F32).at[0, G_I:G_I + MLSTM_HEADS].set(b_i).at[0, G_F:G_F + MLSTM_HEADS].set(b_f)
    full = lambda shape: pl.BlockSpec(shape, lambda b, i: (0,) * len(shape))
    return pl.pallas_call(
        _mixer0_kernel,
        out_shape=jax.ShapeDtypeStruct((bsz, s, GLA_V + ML_W), BF16),
        grid=(bsz, s // L),
        in_specs=[
            pl.BlockSpec((1, L, n), lambda b, i: (b, i, 0)),
            full((LANES, GLA_QK)),
            full((1, GLA_QK)),
            full((1, GLA_V)),
            full((MLSTM_CONV, 2 * ML_W)),
            full((1, 2 * ML_W)),
            full((1, LANES)),
            full((1, ML_W)),
        ],
        out_specs=pl.BlockSpec((1, L, GLA_V + ML_W), lambda b, i: (b, i, 0)),
        scratch_shapes=[
            pltpu.VMEM((GLA_DV, GLA_QK), F32),
            pltpu.VMEM((MLSTM_HEADS, 2 * MLSTM_DH, MLSTM_DH), F32),
            pltpu.VMEM((1, LANES), F32),
            pltpu.VMEM((8 + L, 2 * ML_W), F32),
        ],
        compiler_params=_params("parallel", "arbitrary"),
        name="mixer0",
    )(proj, wa_pad, ba.reshape(1, GLA_QK), gla_g.reshape(1, GLA_V), conv_w, conv_b.reshape(1, 2 * ML_W),
      gbias, ml_g.reshape(1, ML_W))


def _sb_kernel(q_ref, k_ref, v_ref, o_ref, qs_ref, acc_ref, car_ref, mar_ref, kn_ref):
    TQ, TK = SB_TQ, SB_TK
    nd = TQ // TK
    RB = 2 * TK
    i = pl.program_id(2)
    lane = _iota((1, LANES), 1)
    head0 = (lane >> _HEAD_SHIFT) == 0

    @pl.when(i == 0)
    def _():
        k2 = jnp.square(k_ref[0].astype(F32))
        for h in range(2):
            n2 = jnp.sum(jnp.where(head0 if h == 0 else ~head0, k2, 0.0), axis=1, keepdims=True)
            kn_ref[h:h + 1, :] = jnp.broadcast_to(jnp.sqrt(jnp.max(n2, axis=0, keepdims=True)), (1, LANES))

    for a in range(nd):
        qa = q_ref[0, a * TK:(a + 1) * TK, :].astype(F32) * (SB_DH ** -0.5 * LOG2E)
        for h in range(2):
            rows = slice(a * RB + h * TK, a * RB + (h + 1) * TK)
            qh = jnp.where(head0 if h == 0 else ~head0, qa, 0.0).astype(BF16)
            qs_ref[rows, :] = qh
            qn = jnp.sqrt(jnp.sum(jnp.square(qh.astype(F32)), axis=1, keepdims=True))
            mar_ref[rows, :] = qn * kn_ref[h:h + 1, :] * SB_BOUND_SLACK + SB_ZERO_BITS
    acc_ref[...] = jnp.zeros_like(acc_ref)
    car_ref[...] = jnp.zeros_like(car_ref)
    tri = jnp.where(_iota((TK, TK), 0) >= _iota((TK, TK), 1), -1.0, 0.0).astype(BF16)

    def logits(kb, a_lo, a_hi):
        r0 = pl.multiple_of(kb * TK, TK)
        return _dot_nt(qs_ref[a_lo * RB:a_hi * RB, :], k_ref[0, pl.ds(r0, TK), :])

    def tile(z, kb, a_lo, a_hi, diagonal):
        rows = slice(a_lo * RB, a_hi * RB)
        n = (a_hi - a_lo) * RB
        vt = v_ref[0, pl.ds(pl.multiple_of(kb * TK, TK), TK), :]
        sp = jnp.maximum(z, 0.0) + jnp.log(1.0 + jnp.exp2(-jnp.abs(z))) * LOG2E
        if diagonal:
            mask = _iota((n, TK), 1) < (_iota((n, TK), 0) & (TK - 1))
            sp = jnp.where(mask, sp, 0.0)
        hi = sp.astype(BF16)
        lo = (sp - hi.astype(F32)).astype(BF16)
        res = _dot(hi, tri) + _dot(lo, tri)
        car = car_ref[rows, :]
        att = jnp.exp2(z + res - jnp.concatenate([car] * (TK // LANES), axis=1))
        if diagonal:
            att = jnp.where(mask, att, 0.0)
        acc_ref[rows, :] += _dot(att.astype(BF16), vt)
        car_ref[rows, :] = car + jnp.sum(sp, axis=1, keepdims=True)

    n_off = nd * i

    def near_tiles(with_off):
        jobs = []
        for d in reversed(range(nd)):
            jobs.append((nd * i + d, d, d + 1, True))
            if d + 1 < nd:
                jobs.append((nd * i + d, d + 1, nd, False))
        if with_off:
            jobs.append((n_off - 1, 0, nd, False))
        zs = [logits(kb, a_lo, a_hi) for kb, a_lo, a_hi, _ in jobs]
        for z, (kb, a_lo, a_hi, diagonal) in zip(zs, jobs):
            tile(z, kb, a_lo, a_hi, diagonal)

    pl.when(i == 0)(functools.partial(near_tiles, False))
    pl.when(i > 0)(functools.partial(near_tiles, True))

    def pending():
        return jnp.max(mar_ref[...] - car_ref[...]) > 0.0

    def far_body(c):
        kb, _ = c
        tile(logits(kb, 0, nd), kb, 0, nd, False)
        return kb - 1, pending()

    lax.while_loop(lambda c: jnp.logical_and(c[0] >= 0, c[1]), far_body, (n_off - 2, pending()))
    for a in range(nd):
        o_ref[0, a * TK:(a + 1) * TK, :] = jnp.where(
            (lane >> _HEAD_SHIFT) == 0, acc_ref[a * RB:a * RB + TK, :], acc_ref[a * RB + TK:(a + 1) * RB, :]
        ).astype(o_ref.dtype)


def _stick_breaking(pc):
    bsz, s, _ = pc.shape
    npair = SB_W // LANES
    return pl.pallas_call(
        _sb_kernel,
        out_shape=jax.ShapeDtypeStruct((bsz, s, SB_W), BF16),
        grid=(bsz, npair, s // SB_TQ),
        in_specs=[
            pl.BlockSpec((1, SB_TQ, LANES), lambda b, p, i: (b, i, p)),
            pl.BlockSpec((1, s, LANES), lambda b, p, i: (b, 0, npair + p)),
            pl.BlockSpec((1, s, LANES), lambda b, p, i: (b, 0, 2 * npair + p)),
        ],
        out_specs=pl.BlockSpec((1, SB_TQ, LANES), lambda b, p, i: (b, i, p)),
        scratch_shapes=[
            pltpu.VMEM((2 * SB_TQ, LANES), BF16),
            pltpu.VMEM((2 * SB_TQ, LANES), F32),
            pltpu.VMEM((2 * SB_TQ, LANES), F32),
            pltpu.VMEM((2 * SB_TQ, LANES), F32),
            pltpu.VMEM((8, LANES), F32),
        ],
        compiler_params=_params("parallel", "parallel", "arbitrary"),
        name="stick_breaking",
    )(pc, pc, pc)


def _t5_bucket(dist):
    max_exact = N_REL_BUCKETS // 2
    dd = np.maximum(dist, 1).astype(np.float64)
    large = max_exact + (np.log(dd / max_exact) / np.log(REL_MAX_DIST / max_exact)
                         * (N_REL_BUCKETS - max_exact)).astype(np.int32)
    large = np.minimum(large, N_REL_BUCKETS - 1)
    return np.where(dist < max_exact, dist, large).astype(np.int32)


def _dil_buckets():
    qi = np.arange(DIL_BLK)[:, None]
    kj = np.arange(2 * DIL_BLK)[None, :]
    delta = qi - kj + DIL_BLK
    in_win = (delta >= 0) & (delta <= DIL_BLK)
    tabs = []
    for window, dil in DIL_PAIRS:
        assert window // dil == DIL_BLK
        bucket = _t5_bucket(np.clip(delta, 0, None) * dil)
        tabs.append(np.where(in_win, bucket, -1).astype(np.int32))
    return np.stack(tabs, 0)


def _dil_kernel(q_ref, k_ref, v_ref, bkt_ref, tab_ref, o_ref, m_sc, l_sc, acc_sc):
    s_len = q_ref.shape[1]
    hp = pl.program_id(1)
    g = pl.program_id(2)
    blk = DIL_BLK
    lane = _iota((1, LANES), 1)
    lane_head = lane >> _HEAD_SHIFT

    @pl.when(g == 0)
    def _():
        m_sc[...] = jnp.full_like(m_sc, NEG_INF)
        l_sc[...] = jnp.zeros_like(l_sc)
        acc_sc[...] = jnp.zeros_like(acc_sc)

    kcol = _iota((2 * blk, 2 * blk), 1)

    def group(gi, dil):
        nb = s_len // (blk * dil)
        bkt = bkt_ref[gi]
        biases = []
        for hl in range(2):
            head = gi * DIL_HEADS_PER_GROUP + hp * 2 + hl
            bias = jnp.full((blk, 2 * blk), NEG_INF, F32)
            for bk in range(N_REL_BUCKETS):
                bias = jnp.where(bkt == bk, tab_ref[bk, head], bias)
            biases.append(bias)
        bias2 = jnp.concatenate(biases, axis=0)
        sel0 = lane_head == 0

        def stage_logits(t):
            r = t // nb
            n = t % nb
            q_start = r + dil * blk * n
            p_start = r + dil * blk * jnp.maximum(n - 1, 0)
            rows_q = pl.ds(q_start, blk, stride=dil) if dil > 1 else pl.ds(q_start, blk)
            rows_p = pl.ds(p_start, blk, stride=dil) if dil > 1 else pl.ds(p_start, blk)
            qb = q_ref[0, rows_q, :] * (DIL_DH ** -0.5)
            q2 = jnp.concatenate([jnp.where(sel0, qb, 0.0), jnp.where(sel0, 0.0, qb)], axis=0).astype(BF16)
            kk = jnp.concatenate([k_ref[0, rows_p, :], k_ref[0, rows_q, :]], axis=0).astype(BF16)
            logits = _dot_nt(q2, kk) + bias2
            logits = jnp.where(jnp.logical_and(n == 0, kcol < blk), NEG_INF, logits)
            return rows_q, rows_p, logits

        def stage_max(rows_q, logits):
            m_old = m_sc[rows_q, :]
            m_old_col = jnp.concatenate([m_old[:, 0:1], m_old[:, DIL_DH:DIL_DH + 1]], axis=0)
            m_col = jnp.maximum(m_old_col, jnp.max(logits, axis=1, keepdims=True))
            return m_old, m_col

        def stage_pv(rows_q, rows_p, logits, m_col):
            p = jnp.exp(logits - m_col)
            vv = jnp.concatenate([v_ref[0, rows_p, :], v_ref[0, rows_q, :]], axis=0).astype(BF16)
            return jnp.sum(p, axis=1, keepdims=True), _dot(p.astype(BF16), vv)

        def body(tt, carry):
            ts = [tt * DIL_UNROLL + u for u in range(DIL_UNROLL)]
            s1 = [stage_logits(t) for t in ts]
            s2 = [stage_max(rq, lg) for rq, _, lg in s1]
            s3 = [stage_pv(rq, rp, lg, mc) for (rq, rp, lg), (_, mc) in zip(s1, s2)]
            outs = []
            for (rows_q, _, _), (m_old, m_col), (ps, pv) in zip(s1, s2, s3):
                m_new = jnp.where(sel0, m_col[0:blk], m_col[blk:])
                alpha = jnp.exp(m_old - m_new)
                l_new = alpha * l_sc[rows_q, :] + jnp.where(sel0, ps[0:blk], ps[blk:])
                a_new = alpha * acc_sc[rows_q, :] + jnp.where(sel0, pv[0:blk], pv[blk:])
                outs.append((rows_q, m_new, l_new, a_new))
            for rows_q, m_new, l_new, a_new in outs:
                m_sc[rows_q, :] = m_new
                l_sc[rows_q, :] = l_new
                acc_sc[rows_q, :] = a_new
            return carry

        lax.fori_loop(0, s_len // (blk * DIL_UNROLL), body, 0)

    for gi, (_, dil) in enumerate(DIL_PAIRS):
        pl.when(g == gi)(functools.partial(group, gi, dil))

    @pl.when(g == len(DIL_PAIRS) - 1)
    def _():
        o_ref[0] = (acc_sc[...] / l_sc[...]).astype(o_ref.dtype)


def _dilated(pd, rel_bias):
    bsz, s, _ = pd.shape
    ng = len(DIL_PAIRS)
    gw = DIL_HEADS_PER_GROUP * DIL_DH // LANES
    nq = DIL_W // LANES
    bkt = jnp.asarray(_dil_buckets())
    blk_spec = lambda base: pl.BlockSpec((1, s, LANES), lambda b, p, g: (b, 0, base + g * gw + p))
    return pl.pallas_call(
        _dil_kernel,
        out_shape=jax.ShapeDtypeStruct((bsz, s, DIL_OUT), BF16),
        grid=(bsz, gw, ng),
        in_specs=[
            blk_spec(0),
            blk_spec(nq),
            blk_spec(2 * nq),
            pl.BlockSpec((ng, DIL_BLK, 2 * DIL_BLK), lambda b, p, g: (0, 0, 0)),
            pl.BlockSpec(memory_space=pltpu.SMEM),
        ],
        out_specs=pl.BlockSpec((1, s, LANES), lambda b, p, g: (b, 0, p)),
        scratch_shapes=[pltpu.VMEM((s, LANES), F32)] * 3,
        compiler_params=_params("parallel", "parallel", "arbitrary"),
        name="dilated",
    )(pd, pd, pd, bkt, rel_bias)


def _ab_weight(w):
    o = np.cumsum((0,) + (GLA_QK, GLA_QK, GLA_V, GLA_V, GLA_GATE_RANK, 2 * ML_W, ML_W, MLSTM_HEADS, MLSTM_HEADS, ML_W))
    qa, ka, va, ra, aa, qkb, vb, ib, fb, ob = [w[:, o[j]:o[j + 1]] for j in range(10)]
    pad = jnp.zeros((w.shape[0], LANES - 2 * MLSTM_HEADS - GLA_GATE_RANK), w.dtype)
    return jnp.concatenate([qa, ka, va, ra, qkb, vb, ob, ib, fb, aa, pad], axis=1).astype(BF16)


def kernel(x, c, ada_w, ada_b, ln_g, ln_b, ab_w_in, gla_wa_up, gla_ba, gla_norm_g, ml_conv_w, ml_conv_b,
           ml_b_i, ml_b_f, ml_norm_g, ab_w_out, cd_w_in, rel_bias, cd_w_out, ffn_w1, ffn_w3, ffn_w2):
    bsz, s, d = x.shape
    mod_all = _ada_mod(c, ada_w, ada_b).reshape(DEPTH, bsz, 6, d)
    tm = 512
    for layer in range(DEPTH):
        mod = mod_all[layer]
        j = layer // 2
        if layer % 2 == 0:
            (proj,) = _inproj(x, mod, _ab_weight(ab_w_in[j]), [(P0_N, F32)], tm, 0, 1, "inproj0")
            cat = _mixer0(proj, gla_wa_up[j], gla_ba[j], gla_norm_g[j], ml_conv_w[j], ml_conv_b[j],
                          ml_b_i[j], ml_b_f[j], ml_norm_g[j])
            x = _outproj_ln([cat], [ab_w_out[j].astype(BF16)], x, mod, ln_g[layer, 0], ln_b[layer, 0],
                            tm, 2, "outproj0")
        else:
            pc, pd = _inproj(x, mod, cd_w_in[j].astype(BF16), [(3 * SB_W, BF16), (3 * DIL_W, F32)],
                             tm, 0, 1, "inproj1")
            oc = _stick_breaking(pc)
            od = _dilated(pd, rel_bias)
            w_out = cd_w_out[j].astype(BF16)
            x = _outproj_ln([oc, od], [w_out[:SB_W], w_out[SB_W:]], x, mod, ln_g[layer, 0], ln_b[layer, 0],
                            tm, 2, "outproj1")
        x = _ffn(x, mod, ffn_w1[layer], ffn_w3[layer], ffn_w2[layer], ln_g[layer, 1], ln_b[layer, 1], FFN_TM)
    return x
```

```python
import functools
import itertools

import numpy as np
import jax
import jax.numpy as jnp
from jax import lax
from jax.experimental import pallas as pl
from jax.experimental.pallas import tpu as pltpu

F32 = jnp.float32
BF16 = jnp.bfloat16

D_MODEL = 1024
DEPTH = 2
GLA_HEADS = 4
GLA_DK = 64
GLA_DV = 128
GLA_GATE_RANK = 16
GLA_TAU = 16.0
GLA_CHUNK = 64
MLSTM_HEADS = 4
MLSTM_DH = 128
MLSTM_CONV = 4
SB_HEADS = 8
SB_DH = 64
DIL_PAIRS = ((128, 1), (512, 4), (2048, 16))
DIL_HEADS_PER_GROUP = 4
DIL_DH = 64
DIL_BLK = 128
DIL_UNROLL = 4
N_REL_BUCKETS = 32
REL_MAX_DIST = 2048
D_FF = ((8 * D_MODEL + 3 * 256 - 1) // (3 * 256)) * 256
LN_EPS = 1e-5
RES_ALPHA = (2 * DEPTH) ** 0.25
NEG_INF = -1e30

GLA_QK = GLA_HEADS * GLA_DK
GLA_V = GLA_HEADS * GLA_DV
ML_W = MLSTM_HEADS * MLSTM_DH
SB_W = SB_HEADS * SB_DH
DIL_W = len(DIL_PAIRS) * DIL_HEADS_PER_GROUP * DIL_DH
DIL_OUT = DIL_HEADS_PER_GROUP * DIL_DH

LANES = 128
VMEM_LIMIT = 56 * 1024 * 1024

P0_QA = 0
P0_KA = P0_QA + GLA_QK
P0_VA = P0_KA + GLA_QK
P0_RA = P0_VA + GLA_V
P0_QKB = P0_RA + GLA_V
P0_VB = P0_QKB + 2 * ML_W
P0_OB = P0_VB + ML_W
P0_G = P0_OB + ML_W
P0_N = P0_G + LANES
G_I = 0
G_F = MLSTM_HEADS
G_A = 2 * MLSTM_HEADS

MIX_TILE = 256
FF_CHUNK = 256
FFN_TM = 512
SB_TQ = 512
SB_TK = 256
SB_ZERO_BITS = 160.0
SB_BOUND_SLACK = 1.01
LOG2E = 1.4426950408889634
_CHUNK_SHIFT = 6
_HEAD_SHIFT = 6
assert GLA_CHUNK == GLA_DK == 1 << _CHUNK_SHIFT and SB_DH == DIL_DH == 1 << _HEAD_SHIFT


def _dot(a, b):
    return jnp.dot(a, b, preferred_element_type=F32)


def _dot_nt(a, b):
    return lax.dot_general(a, b, (((1,), (1,)), ((), ())), preferred_element_type=F32)


def _dot_tn(a, b):
    return lax.dot_general(a, b, (((0,), (0,)), ((), ())), preferred_element_type=F32)


def _dot_split(t, x, terms):
    acc = None
    rem = x
    for i in range(terms):
        part = rem.astype(BF16)
        d = _dot(t, part)
        acc = d if acc is None else acc + d
        if i + 1 < terms:
            rem = rem - part.astype(F32)
    return acc


def _log_sigmoid(x):
    return jnp.minimum(x, 0.0) - jnp.log(1.0 + jnp.exp(-jnp.abs(x)))


def _silu(x):
    return x * jax.nn.sigmoid(x)


def _layer_norm(r, g, b):
    mu = jnp.mean(r, axis=-1, keepdims=True)
    d = r - mu
    var = jnp.mean(d * d, axis=-1, keepdims=True)
    return d * lax.rsqrt(var + LN_EPS) * g + b


def _head_norm(x, g):
    mu = jnp.mean(x, axis=-1, keepdims=True)
    d = x - mu
    var = jnp.mean(d * d, axis=-1, keepdims=True)
    return d * lax.rsqrt(var + LN_EPS) * g


def _iota(shape, dim):
    return lax.broadcasted_iota(jnp.int32, shape, dim)


def _params(*sem):
    return pltpu.CompilerParams(dimension_semantics=sem, vmem_limit_bytes=VMEM_LIMIT)


def _ada_kernel(c_ref, w_ref, b_ref, o_ref):
    ca = _silu(c_ref[...]).astype(BF16)
    o_ref[0] = _dot(ca, w_ref[0].astype(BF16)) + b_ref[0]


def _ada_mod(c, ada_w, ada_b):
    bsz, d = c.shape
    n = ada_w.shape[-1]
    tn = n // 4
    return pl.pallas_call(
        _ada_kernel,
        out_shape=jax.ShapeDtypeStruct((DEPTH, bsz, n), F32),
        grid=(DEPTH, n // tn),
        in_specs=[
            pl.BlockSpec((bsz, d), lambda l, j: (0, 0)),
            pl.BlockSpec((1, d, tn), lambda l, j: (l, 0, j)),
            pl.BlockSpec((1, 1, tn), lambda l, j: (l, 0, j)),
        ],
        out_specs=pl.BlockSpec((1, bsz, tn), lambda l, j: (l, 0, j)),
        compiler_params=_params("arbitrary", "arbitrary"),
        name="ada_mod",
    )(c, ada_w, ada_b.reshape(DEPTH, 1, n))


def _inproj_kernel(x_ref, mod_ref, w_ref, *o_refs, shift_row, scale_row):
    sh = mod_ref[0, shift_row:shift_row + 1, :]
    sc = mod_ref[0, scale_row:scale_row + 1, :]
    hm = (x_ref[0] * (1.0 + sc) + sh).astype(BF16)
    col = 0
    for o_ref in o_refs:
        n = o_ref.shape[-1]
        o_ref[0] = _dot(hm, w_ref[:, col:col + n]).astype(o_ref.dtype)
        col += n


def _inproj(x, mod, w, outs, tm, shift_row, scale_row, name):
    bsz, s, d = x.shape
    n = w.shape[1]
    assert sum(o[0] for o in outs) == n
    return pl.pallas_call(
        functools.partial(_inproj_kernel, shift_row=shift_row, scale_row=scale_row),
        out_shape=[jax.ShapeDtypeStruct((bsz, s, o[0]), o[1]) for o in outs],
        grid=(bsz, s // tm),
        in_specs=[
            pl.BlockSpec((1, tm, d), lambda b, i: (b, i, 0)),
            pl.BlockSpec((1, 6, d), lambda b, i: (b, 0, 0)),
            pl.BlockSpec((d, n), lambda b, i: (0, 0)),
        ],
        out_specs=[pl.BlockSpec((1, tm, o[0]), lambda b, i: (b, i, 0)) for o in outs],
        compiler_params=_params("parallel", "arbitrary"),
        name=name,
    )(x, mod, w)


def _post_kernel(*refs, n_act):
    act_refs = refs[:n_act]
    wo_refs = refs[n_act:2 * n_act]
    x_ref, mod_ref, w1_ref, w3_ref, w2_ref, g_ref, b_ref, o_ref, gm_ref = refs[2 * n_act:]
    y = None
    for a_ref, w_ref in zip(act_refs, wo_refs):
        t = _dot(a_ref[0], w_ref[...])
        y = t if y is None else y + t
    r = RES_ALPHA * x_ref[0] + (1.0 + mod_ref[0, 2:3, :]) * y
    xm = _layer_norm(r, g_ref[0:1, :], b_ref[0:1, :])
    hf = (xm * (1.0 + mod_ref[0, 4:5, :]) + mod_ref[0, 3:4, :]).astype(BF16)
    for c in range(D_FF // FF_CHUNK):
        cs = slice(c * FF_CHUNK, (c + 1) * FF_CHUNK)
        gm_ref[:, cs] = (_silu(_dot(hf, w1_ref[:, cs])) * _dot(hf, w3_ref[:, cs])).astype(BF16)
    y2 = _dot(gm_ref[...], w2_ref[...])
    r2 = RES_ALPHA * xm + (1.0 + mod_ref[0, 5:6, :]) * y2
    o_ref[0] = _layer_norm(r2, g_ref[1:2, :], b_ref[1:2, :])


def _post(acts, w_outs, x, mod, w1, w3, w2, ln_g, ln_b, tm):
    bsz, s, d = x.shape
    n_act = len(acts)
    const = lambda shape: pl.BlockSpec(shape, lambda b, i: (0,) * len(shape), pipeline_mode=pl.Buffered(1))
    in_specs = [pl.BlockSpec((1, tm, a.shape[-1]), lambda b, i: (b, i, 0)) for a in acts]
    in_specs += [const(w.shape) for w in w_outs]
    in_specs += [
        pl.BlockSpec((1, tm, d), lambda b, i: (b, i, 0)),
        pl.BlockSpec((1, 6, d), lambda b, i: (b, 0, 0)),
        const((d, D_FF)),
        const((d, D_FF)),
        const((D_FF, d)),
        const((2, d)),
        const((2, d)),
    ]
    return pl.pallas_call(
        functools.partial(_post_kernel, n_act=n_act),
        out_shape=jax.ShapeDtypeStruct((bsz, s, d), F32),
        grid=(bsz, s // tm),
        in_specs=in_specs,
        out_specs=pl.BlockSpec((1, tm, d), lambda b, i: (b, i, 0)),
        scratch_shapes=[pltpu.VMEM((tm, D_FF), BF16)],
        compiler_params=_params("parallel", "arbitrary"),
        name="post",
    )(*acts, *w_outs, x, mod, w1.astype(BF16), w3.astype(BF16), w2.astype(BF16), ln_g, ln_b)


def _mixer0_kernel(p_ref, wa_ref, ba_ref, gg_ref, cw_ref, cb_ref, gb_ref, mg_ref, o_ref,
                   st_ref, cst_ref, m_ref, xc_ref):
    L = MIX_TILE
    i = pl.program_id(1)

    @pl.when(i == 0)
    def _():
        st_ref[...] = jnp.zeros_like(st_ref)
        cst_ref[...] = jnp.zeros_like(cst_ref)
        m_ref[...] = jnp.zeros_like(m_ref)
        xc_ref[0:8, :] = jnp.zeros((8, 2 * ML_W), F32)

    lane = _iota((1, LANES), 1)
    g_raw = p_ref[0, :, P0_G:P0_G + LANES]

    u = _dot(g_raw.astype(BF16), wa_ref[...]) + ba_ref[...]
    la = _log_sigmoid(u) * (1.0 / GLA_TAU)
    row = _iota((L, L), 0)
    col = _iota((L, L), 1)
    same_chunk = (row >> _CHUNK_SHIFT) == (col >> _CHUNK_SHIFT)
    tri_blk = jnp.where(same_chunk & (col <= row), 1.0, 0.0).astype(BF16)
    bcs = _dot_split(tri_blk, la, 3)
    q_in = p_ref[0, :, P0_QA:P0_QA + GLA_QK] * (GLA_DK ** -0.5) * jnp.exp(bcs)
    k_raw = p_ref[0, :, P0_KA:P0_KA + GLA_QK]
    k_in = k_raw * jnp.exp(-bcs)
    lane_qk = _iota((1, GLA_QK), 1)
    head_masks = [(lane_qk >> _CHUNK_SHIFT) == h for h in range(GLA_HEADS)]
    r4 = _iota((GLA_HEADS * GLA_CHUNK, GLA_CHUNK), 0)
    c4 = _iota((GLA_HEADS * GLA_CHUNK, GLA_CHUNK), 1)
    tril4 = c4 <= (r4 & (GLA_CHUNK - 1))
    oa_chunks = []
    for c in range(L // GLA_CHUNK):
        r0, r1 = c * GLA_CHUNK, (c + 1) * GLA_CHUNK
        b_c = bcs[r0:r1]
        bl = b_c[GLA_CHUNK - 1:GLA_CHUNK, :]
        q_c = q_in[r0:r1]
        k_c = k_in[r0:r1].astype(BF16)
        k_end = k_raw[r0:r1] * jnp.exp(bl - b_c)
        v_c = p_ref[0, r0:r1, P0_VA:P0_VA + GLA_V]
        q_exp = jnp.concatenate([jnp.where(hm, q_c, 0.0) for hm in head_masks], axis=0).astype(BF16)
        k_exp = jnp.concatenate([jnp.where(hm, k_end, 0.0) for hm in head_masks], axis=0).astype(BF16)
        sc = jnp.where(tril4, _dot_nt(q_exp, k_c), 0.0)
        intra = _dot(sc.astype(BF16), v_c.astype(BF16))
        inter = _dot_nt(q_exp, st_ref[...].astype(BF16))
        o_heads = []
        for h in range(GLA_HEADS):
            h0, h1 = h * GLA_CHUNK, (h + 1) * GLA_CHUNK
            o_heads.append(intra[h0:h1, h * GLA_DV:(h + 1) * GLA_DV] + inter[h0:h1])
        oa_chunks.append(o_heads)
        v_cat = jnp.concatenate([v_c[:, h * GLA_DV:(h + 1) * GLA_DV] for h in range(GLA_HEADS)], axis=0)
        st_ref[...] = st_ref[...] * jnp.exp(bl) + _dot_tn(v_cat.astype(BF16), k_exp)
    for h in range(GLA_HEADS):
        o_h = jnp.concatenate([oc[h] for oc in oa_chunks], axis=0)
        cs = slice(h * GLA_DV, (h + 1) * GLA_DV)
        ra = p_ref[0, :, P0_RA + h * GLA_DV:P0_RA + (h + 1) * GLA_DV]
        o_ref[0, :, cs] = (_head_norm(o_h, gg_ref[:, cs]) * _silu(ra)).astype(o_ref.dtype)

    xc_ref[8:8 + L, :] = p_ref[0, :, P0_QKB:P0_QKB + 2 * ML_W]
    conv = cb_ref[...]
    for kk in range(MLSTM_CONV):
        conv = conv + cw_ref[kk:kk + 1, :] * xc_ref[8 - (MLSTM_CONV - 1) + kk:8 - (MLSTM_CONV - 1) + kk + L, :]
    xc_ref[0:8, :] = xc_ref[L:L + 8, :]
    qk_b = _silu(conv)

    gb = jnp.where(lane < G_A, g_raw + gb_ref[...], 0.0)
    ipre = gb
    logf = _log_sigmoid(pltpu.roll(gb, LANES - G_F, axis=1))
    logf = jnp.where(lane < MLSTM_HEADS, logf, 0.0)
    tri = jnp.where(col <= row, 1.0, 0.0).astype(BF16)
    bcum = _dot_split(tri, logf, 3)
    blast = bcum[L - 1:L, :]
    wend = blast - bcum + ipre
    m_prev = m_ref[...]
    m_new = jnp.maximum(blast + m_prev, jnp.max(wend, axis=0, keepdims=True))
    scl = jnp.exp(blast + m_prev - m_new)
    wj = jnp.exp(wend - m_new)
    inter_log = bcum + m_prev
    rows_src = jnp.where(lane < MLSTM_HEADS, ipre, pltpu.roll(bcum, MLSTM_HEADS, axis=1))
    rows_t = rows_src.T
    causal = col <= row
    e0 = jnp.where(_iota((L, MLSTM_DH), 1) == 0, 1.0, 0.0)
    for h in range(MLSTM_HEADS):
        hs = slice(h * MLSTM_DH, (h + 1) * MLSTM_DH)
        b_col = bcum[:, h:h + 1]
        ip_row = rows_t[h:h + 1, :]
        b_row = rows_t[MLSTM_HEADS + h:MLSTM_HEADS + h + 1, :]
        dlog = jnp.where(causal, b_col - b_row + ip_row, NEG_INF)
        il = inter_log[:, h:h + 1]
        m_i = jnp.maximum(il, jnp.max(dlog, axis=1, keepdims=True))
        w_intra = jnp.exp(dlog - m_i)
        s_inter = jnp.exp(il - m_i)
        q_h = qk_b[:, h * MLSTM_DH:(h + 1) * MLSTM_DH].astype(BF16)
        k_h = (qk_b[:, ML_W + h * MLSTM_DH:ML_W + (h + 1) * MLSTM_DH] * (MLSTM_DH ** -0.5)).astype(BF16)
        v_h = p_ref[0, :, P0_VB + h * MLSTM_DH:P0_VB + (h + 1) * MLSTM_DH]
        v_aug = jnp.concatenate([v_h, e0], axis=1)
        a = (w_intra * _dot_nt(q_h, k_h)).astype(BF16)
        c_prev = cst_ref[h]
        tot = _dot(a, v_aug.astype(BF16)) + s_inter * _dot_nt(q_h, c_prev.astype(BF16))
        num = tot[:, :MLSTM_DH]
        den = tot[:, MLSTM_DH:MLSTM_DH + 1]
        hid = num / jnp.maximum(jnp.abs(den), jnp.exp(-m_i))
        w_aug = (v_aug * wj[:, h:h + 1]).astype(BF16)
        cst_ref[h] = scl[:, h:h + 1] * c_prev + _dot_tn(w_aug, k_h)
        ob = p_ref[0, :, P0_OB + h * MLSTM_DH:P0_OB + (h + 1) * MLSTM_DH]
        o_ref[0, :, GLA_V + h * MLSTM_DH:GLA_V + (h + 1) * MLSTM_DH] = (
            jax.nn.sigmoid(ob) * _head_norm(hid, mg_ref[:, hs])).astype(o_ref.dtype)
    m_ref[...] = m_new


def _mixer0(proj, wa_up, ba, gla_g, conv_w, conv_b, b_i, b_f, ml_g):
    bsz, s, n = proj.shape
    L = MIX_TILE
    wa_pad = jnp.zeros((LANES, GLA_QK), F32).at[G_A:G_A + GLA_GATE_RANK].set(wa_up).astype(BF16)
    gbias = jnp.zeros((1, LANES), F32).at[0, G_I:G_I + MLSTM_HEADS].set(b_i).at[0, G_F:G_F + MLSTM_HEADS].set(b_f)
    full = lambda shape: pl.BlockSpec(shape, lambda b, i: (0,) * len(shape))
    return pl.pallas_call(
        _mixer0_kernel,
        out_shape=jax.ShapeDtypeStruct((bsz, s, GLA_V + ML_W), BF16),
        grid=(bsz, s // L),
        in_specs=[
            pl.BlockSpec((1, L, n), lambda b, i: (b, i, 0)),
            full((LANES, GLA_QK)),
            full((1, GLA_QK)),
            full((1, GLA_V)),
            full((MLSTM_CONV, 2 * ML_W)),
            full((1, 2 * ML_W)),
            full((1, LANES)),
            full((1, ML_W)),
        ],
        out_specs=pl.BlockSpec((1, L, GLA_V + ML_W), lambda b, i: (b, i, 0)),
        scratch_shapes=[
            pltpu.VMEM((GLA_DV, GLA_QK), F32),
            pltpu.VMEM((MLSTM_HEADS, 2 * MLSTM_DH, MLSTM_DH), F32),
            pltpu.VMEM((1, LANES), F32),
            pltpu.VMEM((8 + L, 2 * ML_W), F32),
        ],
        compiler_params=_params("parallel", "arbitrary"),
        name="mixer0",
    )(proj, wa_pad, ba.reshape(1, GLA_QK), gla_g.reshape(1, GLA_V), conv_w, conv_b.reshape(1, 2 * ML_W),
      gbias, ml_g.reshape(1, ML_W))


def _sb_kernel(q_ref, k_ref, v_ref, o_ref, qs_ref, acc_ref, car_ref, mar_ref, kn_ref):
    TQ, TK = SB_TQ, SB_TK
    nd = TQ // TK
    RB = 2 * TK
    i = pl.program_id(2)
    lane = _iota((1, LANES), 1)
    head0 = (lane >> _HEAD_SHIFT) == 0

    @pl.when(i == 0)
    def _():
        k2 = jnp.square(k_ref[0].astype(F32))
        for h in range(2):
            n2 = jnp.sum(jnp.where(head0 if h == 0 else ~head0, k2, 0.0), axis=1, keepdims=True)
            kn_ref[h:h + 1, :] = jnp.broadcast_to(jnp.sqrt(jnp.max(n2, axis=0, keepdims=True)), (1, LANES))

    for a in range(nd):
        qa = q_ref[0, a * TK:(a + 1) * TK, :].astype(F32) * (SB_DH ** -0.5 * LOG2E)
        for h in range(2):
            rows = slice(a * RB + h * TK, a * RB + (h + 1) * TK)
            qh = jnp.where(head0 if h == 0 else ~head0, qa, 0.0).astype(BF16)
            qs_ref[rows, :] = qh
            qn = jnp.sqrt(jnp.sum(jnp.square(qh.astype(F32)), axis=1, keepdims=True))
            mar_ref[rows, :] = qn * kn_ref[h:h + 1, :] * SB_BOUND_SLACK + SB_ZERO_BITS
    tri = jnp.where(_iota((TK, TK), 0) >= _iota((TK, TK), 1), -1.0, 0.0).astype(BF16)

    def logits(kb, a_lo, a_hi):
        r0 = pl.multiple_of(kb * TK, TK)
        return _dot_nt(qs_ref[a_lo * RB:a_hi * RB, :], k_ref[0, pl.ds(r0, TK), :])

    def values(kb):
        return v_ref[0, pl.ds(pl.multiple_of(kb * TK, TK), TK), :]

    def softplus2(z, diagonal):
        sp = jnp.maximum(z, 0.0) + jnp.log(1.0 + jnp.exp2(-jnp.abs(z))) * LOG2E
        if not diagonal:
            return sp, None
        mask = _iota(z.shape, 1) < (_iota(z.shape, 0) & (TK - 1))
        return jnp.where(mask, sp, 0.0), mask

    def suffix(sp):
        hi = sp.astype(BF16)
        lo = (sp - hi.astype(F32)).astype(BF16)
        return _dot(hi, tri) + _dot(lo, tri), jnp.sum(sp, axis=1, keepdims=True)

    def weights(z, res, car, mask):
        att = jnp.exp2(z + res if car is None else z + res - car)
        if mask is not None:
            att = jnp.where(mask, att, 0.0)
        return att.astype(BF16)

    def tile(kb, a_lo, a_hi):
        rows = slice(a_lo * RB, a_hi * RB)
        z = logits(kb, a_lo, a_hi)
        sp, _ = softplus2(z, False)
        res, tot = suffix(sp)
        car = car_ref[rows, :]
        att = weights(z, res, jnp.concatenate([car] * (TK // LANES), axis=1), None)
        acc_ref[rows, :] += _dot(att, values(kb))
        car_ref[rows, :] = car + tot

    n_off = nd * i

    def near_tiles(with_off):
        chains = []
        for a in range(nd):
            chain = [(nd * i + a - j, a, j == 0) for j in range(a + 1)]
            if with_off and a == 0:
                chain.append((n_off - 1, a, False))
            chains.append(chain)
        jobs = [job for step in itertools.zip_longest(*chains) for job in step if job is not None]
        zs = [logits(kb, a, a + 1) for kb, a, _ in jobs]
        sps = [softplus2(z, diagonal) for z, (_, _, diagonal) in zip(zs, jobs)]
        sufs = [suffix(sp) for sp, _ in sps]
        car = [None] * nd
        atts = []
        for z, (sp, mask), (res, tot), (kb, a, _) in zip(zs, sps, sufs, jobs):
            atts.append(weights(z, res, car[a], mask))
            car[a] = tot if car[a] is None else car[a] + tot
        acc = [None] * nd
        for att, (kb, a, _) in zip(atts, jobs):
            t = _dot(att, values(kb))
            acc[a] = t if acc[a] is None else acc[a] + t
        for a in range(nd):
            acc_ref[a * RB:(a + 1) * RB, :] = acc[a]
            car_ref[a * RB:(a + 1) * RB, :] = jnp.broadcast_to(car[a], (RB, LANES))

    pl.when(i == 0)(functools.partial(near_tiles, False))
    pl.when(i > 0)(functools.partial(near_tiles, True))

    def pending():
        return jnp.max(mar_ref[...] - car_ref[...]) > 0.0

    def far_body(c):
        kb, _ = c
        tile(kb + 1, 1, nd)
        tile(kb, 0, 1)
        return kb - 1, pending()

    _, go = lax.while_loop(lambda c: jnp.logical_and(c[0] >= 0, c[1]), far_body, (n_off - 2, pending()))

    @pl.when(jnp.logical_and(go, n_off > 0))
    def _():
        tile(0, 1, nd)

    for a in range(nd):
        o_ref[0, a * TK:(a + 1) * TK, :] = jnp.where(
            (lane >> _HEAD_SHIFT) == 0, acc_ref[a * RB:a * RB + TK, :], acc_ref[a * RB + TK:(a + 1) * RB, :]
        ).astype(o_ref.dtype)


def _stick_breaking(pc):
    bsz, s, _ = pc.shape
    npair = SB_W // LANES
    return pl.pallas_call(
        _sb_kernel,
        out_shape=jax.ShapeDtypeStruct((bsz, s, SB_W), BF16),
        grid=(bsz, npair, s // SB_TQ),
        in_specs=[
            pl.BlockSpec((1, SB_TQ, LANES), lambda b, p, i: (b, i, p)),
            pl.BlockSpec((1, s, LANES), lambda b, p, i: (b, 0, npair + p)),
            pl.BlockSpec((1, s, LANES), lambda b, p, i: (b, 0, 2 * npair + p)),
        ],
        out_specs=pl.BlockSpec((1, SB_TQ, LANES), lambda b, p, i: (b, i, p)),
        scratch_shapes=[
            pltpu.VMEM((2 * SB_TQ, LANES), BF16),
            pltpu.VMEM((2 * SB_TQ, LANES), F32),
            pltpu.VMEM((2 * SB_TQ, LANES), F32),
            pltpu.VMEM((2 * SB_TQ, LANES), F32),
            pltpu.VMEM((8, LANES), F32),
        ],
        compiler_params=_params("parallel", "parallel", "arbitrary"),
        name="stick_breaking",
    )(pc, pc, pc)


def _t5_bucket(dist):
    max_exact = N_REL_BUCKETS // 2
    dd = np.maximum(dist, 1).astype(np.float64)
    large = max_exact + (np.log(dd / max_exact) / np.log(REL_MAX_DIST / max_exact)
                         * (N_REL_BUCKETS - max_exact)).astype(np.int32)
    large = np.minimum(large, N_REL_BUCKETS - 1)
    return np.where(dist < max_exact, dist, large).astype(np.int32)


def _dil_buckets():
    qi = np.arange(DIL_BLK)[:, None]
    kj = np.arange(2 * DIL_BLK)[None, :]
    delta = qi - kj + DIL_BLK
    in_win = (delta >= 0) & (delta <= DIL_BLK)
    tabs = []
    for window, dil in DIL_PAIRS:
        assert window // dil == DIL_BLK
        bucket = _t5_bucket(np.clip(delta, 0, None) * dil)
        tabs.append(np.where(in_win, bucket, -1).astype(np.int32))
    return np.stack(tabs, 0)


def _dil_kernel(q_ref, k_ref, v_ref, bkt_ref, tab_ref, o_ref, m_sc, l_sc, acc_sc):
    s_len = q_ref.shape[1]
    hp = pl.program_id(1)
    g = pl.program_id(2)
    blk = DIL_BLK
    lane = _iota((1, LANES), 1)
    lane_head = lane >> _HEAD_SHIFT

    @pl.when(g == 0)
    def _():
        m_sc[...] = jnp.full_like(m_sc, NEG_INF)
        l_sc[...] = jnp.zeros_like(l_sc)
        acc_sc[...] = jnp.zeros_like(acc_sc)

    kcol = _iota((2 * blk, 2 * blk), 1)

    def group(gi, dil):
        nb = s_len // (blk * dil)
        bkt = bkt_ref[gi]
        biases = []
        for hl in range(2):
            head = gi * DIL_HEADS_PER_GROUP + hp * 2 + hl
            bias = jnp.full((blk, 2 * blk), NEG_INF, F32)
            for bk in range(N_REL_BUCKETS):
                bias = jnp.where(bkt == bk, tab_ref[bk, head], bias)
            biases.append(bias)
        bias2 = jnp.concatenate(biases, axis=0)
        sel0 = lane_head == 0

        def stage_logits(t):
            r = t // nb
            n = t % nb
            q_start = r + dil * blk * n
            p_start = r + dil * blk * jnp.maximum(n - 1, 0)
            rows_q = pl.ds(q_start, blk, stride=dil) if dil > 1 else pl.ds(q_start, blk)
            rows_p = pl.ds(p_start, blk, stride=dil) if dil > 1 else pl.ds(p_start, blk)
            qb = q_ref[0, rows_q, :] * (DIL_DH ** -0.5)
            q2 = jnp.concatenate([jnp.where(sel0, qb, 0.0), jnp.where(sel0, 0.0, qb)], axis=0).astype(BF16)
            kk = jnp.concatenate([k_ref[0, rows_p, :], k_ref[0, rows_q, :]], axis=0).astype(BF16)
            logits = _dot_nt(q2, kk) + bias2
            logits = jnp.where(jnp.logical_and(n == 0, kcol < blk), NEG_INF, logits)
            return rows_q, rows_p, logits

        def stage_max(rows_q, logits):
            m_old = m_sc[rows_q, :]
            m_old_col = jnp.concatenate([m_old[:, 0:1], m_old[:, DIL_DH:DIL_DH + 1]], axis=0)
            m_col = jnp.maximum(m_old_col, jnp.max(logits, axis=1, keepdims=True))
            return m_old, m_col

        def stage_pv(rows_q, rows_p, logits, m_col):
            p = jnp.exp(logits - m_col)
            vv = jnp.concatenate([v_ref[0, rows_p, :], v_ref[0, rows_q, :]], axis=0)
            vv = jnp.concatenate([vv, jnp.ones_like(vv)], axis=1).astype(BF16)
            pv = _dot(p.astype(BF16), vv)
            return pv[:, LANES:], pv[:, :LANES]

        def body(tt, carry):
            ts = [tt * DIL_UNROLL + u for u in range(DIL_UNROLL)]
            s1 = [stage_logits(t) for t in ts]
            s2 = [stage_max(rq, lg) for rq, _, lg in s1]
            s3 = [stage_pv(rq, rp, lg, mc) for (rq, rp, lg), (_, mc) in zip(s1, s2)]
            outs = []
            for (rows_q, _, _), (m_old, m_col), (ps, pv) in zip(s1, s2, s3):
                m_new = jnp.where(sel0, m_col[0:blk], m_col[blk:])
                alpha = jnp.exp(m_old - m_new)
                l_new = alpha * l_sc[rows_q, :] + jnp.where(sel0, ps[0:blk], ps[blk:])
                a_new = alpha * acc_sc[rows_q, :] + jnp.where(sel0, pv[0:blk], pv[blk:])
                outs.append((rows_q, m_new, l_new, a_new))
            for rows_q, m_new, l_new, a_new in outs:
                m_sc[rows_q, :] = m_new
                l_sc[rows_q, :] = l_new
                acc_sc[rows_q, :] = a_new
            return carry

        lax.fori_loop(0, s_len // (blk * DIL_UNROLL), body, 0)

    for gi, (_, dil) in enumerate(DIL_PAIRS):
        pl.when(g == gi)(functools.partial(group, gi, dil))

    @pl.when(g == len(DIL_PAIRS) - 1)
    def _():
        o_ref[0] = (acc_sc[...] / l_sc[...]).astype(o_ref.dtype)


def _dilated(pd, rel_bias):
    bsz, s, _ = pd.shape
    ng = len(DIL_PAIRS)
    gw = DIL_HEADS_PER_GROUP * DIL_DH // LANES
    nq = DIL_W // LANES
    bkt = jnp.asarray(_dil_buckets())
    blk_spec = lambda base: pl.BlockSpec((1, s, LANES), lambda b, p, g: (b, 0, base + g * gw + p))
    return pl.pallas_call(
        _dil_kernel,
        out_shape=jax.ShapeDtypeStruct((bsz, s, DIL_OUT), BF16),
        grid=(bsz, gw, ng),
        in_specs=[
            blk_spec(0),
            blk_spec(nq),
            blk_spec(2 * nq),
            pl.BlockSpec((ng, DIL_BLK, 2 * DIL_BLK), lambda b, p, g: (0, 0, 0)),
            pl.BlockSpec(memory_space=pltpu.SMEM),
        ],
        out_specs=pl.BlockSpec((1, s, LANES), lambda b, p, g: (b, 0, p)),
        scratch_shapes=[pltpu.VMEM((s, LANES), F32)] * 3,
        compiler_params=_params("parallel", "parallel", "arbitrary"),
        name="dilated",
    )(pd, pd, pd, bkt, rel_bias)


def _ab_weight(w):
    o = np.cumsum((0,) + (GLA_QK, GLA_QK, GLA_V, GLA_V, GLA_GATE_RANK, 2 * ML_W, ML_W, MLSTM_HEADS, MLSTM_HEADS, ML_W))
    qa, ka, va, ra, aa, qkb, vb, ib, fb, ob = [w[:, o[j]:o[j + 1]] for j in range(10)]
    pad = jnp.zeros((w.shape[0], LANES - 2 * MLSTM_HEADS - GLA_GATE_RANK), w.dtype)
    return jnp.concatenate([qa, ka, va, ra, qkb, vb, ob, ib, fb, aa, pad], axis=1).astype(BF16)


def kernel(x, c, ada_w, ada_b, ln_g, ln_b, ab_w_in, gla_wa_up, gla_ba, gla_norm_g, ml_conv_w, ml_conv_b,
           ml_b_i, ml_b_f, ml_norm_g, ab_w_out, cd_w_in, rel_bias, cd_w_out, ffn_w1, ffn_w3, ffn_w2):
    bsz, s, d = x.shape
    mod_all = _ada_mod(c, ada_w, ada_b).reshape(DEPTH, bsz, 6, d)
    tm = 512
    for layer in range(DEPTH):
        mod = mod_all[layer]
        j = layer // 2
        if layer % 2 == 0:
            (proj,) = _inproj(x, mod, _ab_weight(ab_w_in[j]), [(P0_N, F32)], tm, 0, 1, "inproj0")
            cat = _mixer0(proj, gla_wa_up[j], gla_ba[j], gla_norm_g[j], ml_conv_w[j], ml_conv_b[j],
                          ml_b_i[j], ml_b_f[j], ml_norm_g[j])
            acts, w_outs = [cat], [ab_w_out[j].astype(BF16)]
        else:
            pc, pd = _inproj(x, mod, cd_w_in[j].astype(BF16), [(3 * SB_W, BF16), (3 * DIL_W, F32)],
                             tm, 0, 1, "inproj1")
            oc = _stick_breaking(pc)
            od = _dilated(pd, rel_bias)
            w_out = cd_w_out[j].astype(BF16)
            acts, w_outs = [oc, od], [w_out[:SB_W], w_out[SB_W:]]
        x = _post(acts, w_outs, x, mod, ffn_w1[layer], ffn_w3[layer], ffn_w2[layer], ln_g[layer], ln_b[layer], FFN_TM)
    return x
```

```python
import functools
import itertools

import numpy as np
import jax
import jax.numpy as jnp
from jax import lax
from jax.experimental import pallas as pl
from jax.experimental.pallas import tpu as pltpu

F32 = jnp.float32
BF16 = jnp.bfloat16

D_MODEL = 1024
DEPTH = 2
GLA_HEADS = 4
GLA_DK = 64
GLA_DV = 128
GLA_GATE_RANK = 16
GLA_TAU = 16.0
GLA_CHUNK = 64
MLSTM_HEADS = 4
MLSTM_DH = 128
MLSTM_CONV = 4
SB_HEADS = 8
SB_DH = 64
DIL_PAIRS = ((128, 1), (512, 4), (2048, 16))
DIL_HEADS_PER_GROUP = 4
DIL_DH = 64
DIL_BLK = 128
DIL_UNROLL = 4
N_REL_BUCKETS = 32
REL_MAX_DIST = 2048
D_FF = ((8 * D_MODEL + 3 * 256 - 1) // (3 * 256)) * 256
LN_EPS = 1e-5
RES_ALPHA = (2 * DEPTH) ** 0.25
NEG_INF = -1e30

GLA_QK = GLA_HEADS * GLA_DK
GLA_V = GLA_HEADS * GLA_DV
ML_W = MLSTM_HEADS * MLSTM_DH
SB_W = SB_HEADS * SB_DH
DIL_W = len(DIL_PAIRS) * DIL_HEADS_PER_GROUP * DIL_DH
DIL_OUT = DIL_HEADS_PER_GROUP * DIL_DH

LANES = 128
VMEM_LIMIT = 56 * 1024 * 1024

P0_QA = 0
P0_KA = P0_QA + GLA_QK
P0_VA = P0_KA + GLA_QK
P0_RA = P0_VA + GLA_V
P0_QKB = P0_RA + GLA_V
P0_VB = P0_QKB + 2 * ML_W
P0_OB = P0_VB + ML_W
P0_G = P0_OB + ML_W
P0_N = P0_G + LANES
G_I = 0
G_F = MLSTM_HEADS
G_A = 2 * MLSTM_HEADS

MIX_TILE = 256
FF_CHUNK = 256
FFN_TM = 512
SB_TQ = 512
SB_TK = 256
SB_ZERO_BITS = 160.0
SB_BOUND_SLACK = 1.01
LOG2E = 1.4426950408889634
_CHUNK_SHIFT = 6
_HEAD_SHIFT = 6
assert GLA_CHUNK == GLA_DK == 1 << _CHUNK_SHIFT and SB_DH == DIL_DH == 1 << _HEAD_SHIFT


def _dot(a, b):
    return jnp.dot(a, b, preferred_element_type=F32)


def _dot_nt(a, b):
    return lax.dot_general(a, b, (((1,), (1,)), ((), ())), preferred_element_type=F32)


def _dot_tn(a, b):
    return lax.dot_general(a, b, (((0,), (0,)), ((), ())), preferred_element_type=F32)


def _dot_split(t, x, terms):
    acc = None
    rem = x
    for i in range(terms):
        part = rem.astype(BF16)
        d = _dot(t, part)
        acc = d if acc is None else acc + d
        if i + 1 < terms:
            rem = rem - part.astype(F32)
    return acc


def _log_sigmoid(x):
    return jnp.minimum(x, 0.0) - jnp.log(1.0 + jnp.exp(-jnp.abs(x)))


def _silu(x):
    return x * jax.nn.sigmoid(x)


def _layer_norm(r, g, b):
    mu = jnp.mean(r, axis=-1, keepdims=True)
    d = r - mu
    var = jnp.mean(d * d, axis=-1, keepdims=True)
    return d * lax.rsqrt(var + LN_EPS) * g + b


def _head_norm(x, g):
    mu = jnp.mean(x, axis=-1, keepdims=True)
    d = x - mu
    var = jnp.mean(d * d, axis=-1, keepdims=True)
    return d * lax.rsqrt(var + LN_EPS) * g


def _iota(shape, dim):
    return lax.broadcasted_iota(jnp.int32, shape, dim)


def _params(*sem):
    return pltpu.CompilerParams(dimension_semantics=sem, vmem_limit_bytes=VMEM_LIMIT)


def _ada_kernel(c_ref, w_ref, b_ref, o_ref):
    ca = _silu(c_ref[...]).astype(BF16)
    o_ref[0] = _dot(ca, w_ref[0].astype(BF16)) + b_ref[0]


def _ada_mod(c, ada_w, ada_b):
    bsz, d = c.shape
    n = ada_w.shape[-1]
    tn = n // 4
    return pl.pallas_call(
        _ada_kernel,
        out_shape=jax.ShapeDtypeStruct((DEPTH, bsz, n), F32),
        grid=(DEPTH, n // tn),
        in_specs=[
            pl.BlockSpec((bsz, d), lambda l, j: (0, 0)),
            pl.BlockSpec((1, d, tn), lambda l, j: (l, 0, j)),
            pl.BlockSpec((1, 1, tn), lambda l, j: (l, 0, j)),
        ],
        out_specs=pl.BlockSpec((1, bsz, tn), lambda l, j: (l, 0, j)),
        compiler_params=_params("arbitrary", "arbitrary"),
        name="ada_mod",
    )(c, ada_w, ada_b.reshape(DEPTH, 1, n))


def _inproj_kernel(x_ref, mod_ref, w_ref, *o_refs, shift_row, scale_row):
    sh = mod_ref[0, shift_row:shift_row + 1, :]
    sc = mod_ref[0, scale_row:scale_row + 1, :]
    hm = (x_ref[0] * (1.0 + sc) + sh).astype(BF16)
    col = 0
    for o_ref in o_refs:
        n = o_ref.shape[-1]
        o_ref[0] = _dot(hm, w_ref[:, col:col + n]).astype(o_ref.dtype)
        col += n


def _inproj(x, mod, w, outs, tm, shift_row, scale_row, name):
    bsz, s, d = x.shape
    n = w.shape[1]
    assert sum(o[0] for o in outs) == n
    return pl.pallas_call(
        functools.partial(_inproj_kernel, shift_row=shift_row, scale_row=scale_row),
        out_shape=[jax.ShapeDtypeStruct((bsz, s, o[0]), o[1]) for o in outs],
        grid=(bsz, s // tm),
        in_specs=[
            pl.BlockSpec((1, tm, d), lambda b, i: (b, i, 0)),
            pl.BlockSpec((1, 6, d), lambda b, i: (b, 0, 0)),
            pl.BlockSpec((d, n), lambda b, i: (0, 0)),
        ],
        out_specs=[pl.BlockSpec((1, tm, o[0]), lambda b, i: (b, i, 0)) for o in outs],
        compiler_params=_params("parallel", "arbitrary"),
        name=name,
    )(x, mod, w)


def _post_kernel(*refs, n_act):
    act_refs = refs[:n_act]
    wo_refs = refs[n_act:2 * n_act]
    x_ref, mod_ref, w1_ref, w3_ref, w2_ref, g_ref, b_ref, o_ref, gm_ref = refs[2 * n_act:]
    y = None
    for a_ref, w_ref in zip(act_refs, wo_refs):
        t = _dot(a_ref[0], w_ref[...])
        y = t if y is None else y + t
    r = RES_ALPHA * x_ref[0] + (1.0 + mod_ref[0, 2:3, :]) * y
    xm = _layer_norm(r, g_ref[0:1, :], b_ref[0:1, :])
    hf = (xm * (1.0 + mod_ref[0, 4:5, :]) + mod_ref[0, 3:4, :]).astype(BF16)
    for c in range(D_FF // FF_CHUNK):
        cs = slice(c * FF_CHUNK, (c + 1) * FF_CHUNK)
        gm_ref[:, cs] = (_silu(_dot(hf, w1_ref[:, cs])) * _dot(hf, w3_ref[:, cs])).astype(BF16)
    y2 = _dot(gm_ref[...], w2_ref[...])
    r2 = RES_ALPHA * xm + (1.0 + mod_ref[0, 5:6, :]) * y2
    o_ref[0] = _layer_norm(r2, g_ref[1:2, :], b_ref[1:2, :])


def _post(acts, w_outs, x, mod, w1, w3, w2, ln_g, ln_b, tm):
    bsz, s, d = x.shape
    n_act = len(acts)
    const = lambda shape: pl.BlockSpec(shape, lambda b, i: (0,) * len(shape), pipeline_mode=pl.Buffered(1))
    in_specs = [pl.BlockSpec((1, tm, a.shape[-1]), lambda b, i: (b, i, 0)) for a in acts]
    in_specs += [const(w.shape) for w in w_outs]
    in_specs += [
        pl.BlockSpec((1, tm, d), lambda b, i: (b, i, 0)),
        pl.BlockSpec((1, 6, d), lambda b, i: (b, 0, 0)),
        const((d, D_FF)),
        const((d, D_FF)),
        const((D_FF, d)),
        const((2, d)),
        const((2, d)),
    ]
    return pl.pallas_call(
        functools.partial(_post_kernel, n_act=n_act),
        out_shape=jax.ShapeDtypeStruct((bsz, s, d), F32),
        grid=(bsz, s // tm),
        in_specs=in_specs,
        out_specs=pl.BlockSpec((1, tm, d), lambda b, i: (b, i, 0)),
        scratch_shapes=[pltpu.VMEM((tm, D_FF), BF16)],
        compiler_params=_params("parallel", "arbitrary"),
        name="post",
    )(*acts, *w_outs, x, mod, w1.astype(BF16), w3.astype(BF16), w2.astype(BF16), ln_g, ln_b)


def _mixer0_kernel(p_ref, wa_ref, ba_ref, gg_ref, cw_ref, cb_ref, gb_ref, mg_ref, o_ref,
                   st_ref, cst_ref, m_ref, xc_ref):
    L = MIX_TILE
    i = pl.program_id(1)

    @pl.when(i == 0)
    def _():
        st_ref[...] = jnp.zeros_like(st_ref)
        cst_ref[...] = jnp.zeros_like(cst_ref)
        m_ref[...] = jnp.zeros_like(m_ref)
        xc_ref[0:8, :] = jnp.zeros((8, 2 * ML_W), F32)

    lane = _iota((1, LANES), 1)
    g_raw = p_ref[0, :, P0_G:P0_G + LANES]

    u = _dot(g_raw.astype(BF16), wa_ref[...]) + ba_ref[...]
    la = _log_sigmoid(u) * (1.0 / GLA_TAU)
    row = _iota((L, L), 0)
    col = _iota((L, L), 1)
    same_chunk = (row >> _CHUNK_SHIFT) == (col >> _CHUNK_SHIFT)
    tri_blk = jnp.where(same_chunk & (col <= row), 1.0, 0.0).astype(BF16)
    bcs = _dot_split(tri_blk, la, 3)
    q_in = p_ref[0, :, P0_QA:P0_QA + GLA_QK] * (GLA_DK ** -0.5) * jnp.exp(bcs)
    k_raw = p_ref[0, :, P0_KA:P0_KA + GLA_QK]
    k_in = k_raw * jnp.exp(-bcs)
    lane_qk = _iota((1, GLA_QK), 1)
    head_masks = [(lane_qk >> _CHUNK_SHIFT) == h for h in range(GLA_HEADS)]
    r4 = _iota((GLA_HEADS * GLA_CHUNK, GLA_CHUNK), 0)
    c4 = _iota((GLA_HEADS * GLA_CHUNK, GLA_CHUNK), 1)
    tril4 = c4 <= (r4 & (GLA_CHUNK - 1))
    oa_chunks = []
    for c in range(L // GLA_CHUNK):
        r0, r1 = c * GLA_CHUNK, (c + 1) * GLA_CHUNK
        b_c = bcs[r0:r1]
        bl = b_c[GLA_CHUNK - 1:GLA_CHUNK, :]
        q_c = q_in[r0:r1]
        k_c = k_in[r0:r1].astype(BF16)
        k_end = k_raw[r0:r1] * jnp.exp(bl - b_c)
        v_c = p_ref[0, r0:r1, P0_VA:P0_VA + GLA_V]
        q_exp = jnp.concatenate([jnp.where(hm, q_c, 0.0) for hm in head_masks], axis=0).astype(BF16)
        k_exp = jnp.concatenate([jnp.where(hm, k_end, 0.0) for hm in head_masks], axis=0).astype(BF16)
        sc = jnp.where(tril4, _dot_nt(q_exp, k_c), 0.0)
        intra = _dot(sc.astype(BF16), v_c.astype(BF16))
        inter = _dot_nt(q_exp, st_ref[...].astype(BF16))
        o_heads = []
        for h in range(GLA_HEADS):
            h0, h1 = h * GLA_CHUNK, (h + 1) * GLA_CHUNK
            o_heads.append(intra[h0:h1, h * GLA_DV:(h + 1) * GLA_DV] + inter[h0:h1])
        oa_chunks.append(o_heads)
        v_cat = jnp.concatenate([v_c[:, h * GLA_DV:(h + 1) * GLA_DV] for h in range(GLA_HEADS)], axis=0)
        st_ref[...] = st_ref[...] * jnp.exp(bl) + _dot_tn(v_cat.astype(BF16), k_exp)
    for h in range(GLA_HEADS):
        o_h = jnp.concatenate([oc[h] for oc in oa_chunks], axis=0)
        cs = slice(h * GLA_DV, (h + 1) * GLA_DV)
        ra = p_ref[0, :, P0_RA + h * GLA_DV:P0_RA + (h + 1) * GLA_DV]
        o_ref[0, :, cs] = (_head_norm(o_h, gg_ref[:, cs]) * _silu(ra)).astype(o_ref.dtype)

    xc_ref[8:8 + L, :] = p_ref[0, :, P0_QKB:P0_QKB + 2 * ML_W]
    conv = cb_ref[...]
    for kk in range(MLSTM_CONV):
        conv = conv + cw_ref[kk:kk + 1, :] * xc_ref[8 - (MLSTM_CONV - 1) + kk:8 - (MLSTM_CONV - 1) + kk + L, :]
    xc_ref[0:8, :] = xc_ref[L:L + 8, :]
    qk_b = _silu(conv)

    gb = jnp.where(lane < G_A, g_raw + gb_ref[...], 0.0)
    ipre = gb
    logf = _log_sigmoid(pltpu.roll(gb, LANES - G_F, axis=1))
    logf = jnp.where(lane < MLSTM_HEADS, logf, 0.0)
    tri = jnp.where(col <= row, 1.0, 0.0).astype(BF16)
    bcum = _dot_split(tri, logf, 3)
    blast = bcum[L - 1:L, :]
    wend = blast - bcum + ipre
    m_prev = m_ref[...]
    m_new = jnp.maximum(blast + m_prev, jnp.max(wend, axis=0, keepdims=True))
    scl = jnp.exp(blast + m_prev - m_new)
    wj = jnp.exp(wend - m_new)
    rows_src = jnp.where(lane < MLSTM_HEADS, ipre, pltpu.roll(bcum, MLSTM_HEADS, axis=1))
    rows_t = rows_src.T
    causal = col <= row
    ones = jnp.ones((L, MLSTM_DH), F32)
    for h in range(MLSTM_HEADS):
        hs = slice(h * MLSTM_DH, (h + 1) * MLSTM_DH)
        b_col = bcum[:, h:h + 1]
        ip_row = rows_t[h:h + 1, :]
        b_row = rows_t[MLSTM_HEADS + h:MLSTM_HEADS + h + 1, :]
        dlog = jnp.where(causal, b_col - b_row + ip_row, NEG_INF)
        il = b_col + m_prev[:, h:h + 1]
        m_i = jnp.maximum(il, jnp.max(dlog, axis=1, keepdims=True))
        w_intra = jnp.exp(dlog - m_i)
        s_inter = jnp.exp(il - m_i)
        q_h = qk_b[:, h * MLSTM_DH:(h + 1) * MLSTM_DH].astype(BF16)
        k_h = (qk_b[:, ML_W + h * MLSTM_DH:ML_W + (h + 1) * MLSTM_DH] * (MLSTM_DH ** -0.5)).astype(BF16)
        v_h = p_ref[0, :, P0_VB + h * MLSTM_DH:P0_VB + (h + 1) * MLSTM_DH]
        v_aug = jnp.concatenate([v_h, ones], axis=1)
        a = (w_intra * _dot_nt(q_h, k_h)).astype(BF16)
        c_prev = cst_ref[h]
        tot = _dot(a, v_aug.astype(BF16)) + s_inter * _dot_nt(q_h, c_prev.astype(BF16))
        num = tot[:, :MLSTM_DH]
        den = tot[:, MLSTM_DH:]
        hid = num / jnp.maximum(jnp.abs(den), jnp.exp(-m_i))
        w_aug = (v_aug * wj[:, h:h + 1]).astype(BF16)
        cst_ref[h] = scl[:, h:h + 1] * c_prev + _dot_tn(w_aug, k_h)
        ob = p_ref[0, :, P0_OB + h * MLSTM_DH:P0_OB + (h + 1) * MLSTM_DH]
        o_ref[0, :, GLA_V + h * MLSTM_DH:GLA_V + (h + 1) * MLSTM_DH] = (
            jax.nn.sigmoid(ob) * _head_norm(hid, mg_ref[:, hs])).astype(o_ref.dtype)
    m_ref[...] = m_new


def _mixer0(proj, wa_up, ba, gla_g, conv_w, conv_b, b_i, b_f, ml_g):
    bsz, s, n = proj.shape
    L = MIX_TILE
    wa_pad = jnp.zeros((LANES, GLA_QK), F32).at[G_A:G_A + GLA_GATE_RANK].set(wa_up).astype(BF16)
    gbias = jnp.zeros((1, LANES), F32).at[0, G_I:G_I + MLSTM_HEADS].set(b_i).at[0, G_F:G_F + MLSTM_HEADS].set(b_f)
    full = lambda shape: pl.BlockSpec(shape, lambda b, i: (0,) * len(shape))
    return pl.pallas_call(
        _mixer0_kernel,
        out_shape=jax.ShapeDtypeStruct((bsz, s, GLA_V + ML_W), BF16),
        grid=(bsz, s // L),
        in_specs=[
            pl.BlockSpec((1, L, n), lambda b, i: (b, i, 0)),
            full((LANES, GLA_QK)),
            full((1, GLA_QK)),
            full((1, GLA_V)),
            full((MLSTM_CONV, 2 * ML_W)),
            full((1, 2 * ML_W)),
            full((1, LANES)),
            full((1, ML_W)),
        ],
        out_specs=pl.BlockSpec((1, L, GLA_V + ML_W), lambda b, i: (b, i, 0)),
        scratch_shapes=[
            pltpu.VMEM((GLA_DV, GLA_QK), F32),
            pltpu.VMEM((MLSTM_HEADS, 2 * MLSTM_DH, MLSTM_DH), F32),
            pltpu.VMEM((1, LANES), F32),
            pltpu.VMEM((8 + L, 2 * ML_W), F32),
        ],
        compiler_params=_params("parallel", "arbitrary"),
        name="mixer0",
    )(proj, wa_pad, ba.reshape(1, GLA_QK), gla_g.reshape(1, GLA_V), conv_w, conv_b.reshape(1, 2 * ML_W),
      gbias, ml_g.reshape(1, ML_W))


def _sb_kernel(q_ref, k_ref, v_ref, o_ref, qs_ref, acc_ref, car_ref, mar_ref, kn_ref):
    TQ, TK = SB_TQ, SB_TK
    nd = TQ // TK
    RB = 2 * TK
    i = pl.program_id(2)
    lane = _iota((1, LANES), 1)
    head0 = (lane >> _HEAD_SHIFT) == 0

    @pl.when(i == 0)
    def _():
        k2 = jnp.square(k_ref[0].astype(F32))
        for h in range(2):
            n2 = jnp.sum(jnp.where(head0 if h == 0 else ~head0, k2, 0.0), axis=1, keepdims=True)
            kn_ref[h:h + 1, :] = jnp.broadcast_to(jnp.sqrt(jnp.max(n2, axis=0, keepdims=True)), (1, LANES))

    for a in range(nd):
        qa = q_ref[0, a * TK:(a + 1) * TK, :].astype(F32) * (SB_DH ** -0.5 * LOG2E)
        for h in range(2):
            rows = slice(a * RB + h * TK, a * RB + (h + 1) * TK)
            qh = jnp.where(head0 if h == 0 else ~head0, qa, 0.0).astype(BF16)
            qs_ref[rows, :] = qh
            qn = jnp.sqrt(jnp.sum(jnp.square(qh.astype(F32)), axis=1, keepdims=True))
            mar_ref[rows, :] = qn * kn_ref[h:h + 1, :] * SB_BOUND_SLACK + SB_ZERO_BITS
    tri = jnp.where(_iota((TK, TK), 0) >= _iota((TK, TK), 1), -1.0, 0.0).astype(BF16)

    def logits(kb, a_lo, a_hi):
        r0 = pl.multiple_of(kb * TK, TK)
        return _dot_nt(qs_ref[a_lo * RB:a_hi * RB, :], k_ref[0, pl.ds(r0, TK), :])

    def values(kb):
        return v_ref[0, pl.ds(pl.multiple_of(kb * TK, TK), TK), :]

    def softplus2(z, diagonal):
        sp = jnp.maximum(z, 0.0) + jnp.log(1.0 + jnp.exp2(-jnp.abs(z))) * LOG2E
        if not diagonal:
            return sp, None
        mask = _iota(z.shape, 1) < (_iota(z.shape, 0) & (TK - 1))
        return jnp.where(mask, sp, 0.0), mask

    def suffix(sp):
        return _dot(sp.astype(BF16), tri), jnp.sum(sp, axis=1, keepdims=True)

    def weights(z, res, car, mask):
        att = jnp.exp2(z + res if car is None else z + res - car)
        if mask is not None:
            att = jnp.where(mask, att, 0.0)
        return att.astype(BF16)

    def tile(kb, a_lo, a_hi):
        rows = slice(a_lo * RB, a_hi * RB)
        z = logits(kb, a_lo, a_hi)
        sp, _ = softplus2(z, False)
        res, tot = suffix(sp)
        car = car_ref[rows, :]
        att = weights(z, res, jnp.concatenate([car] * (TK // LANES), axis=1), None)
        acc_ref[rows, :] += _dot(att, values(kb))
        car_ref[rows, :] = car + tot

    n_off = nd * i

    def near_tiles(with_off):
        chains = []
        for a in range(nd):
            chain = [(nd * i + a - j, a, j == 0) for j in range(a + 1)]
            if with_off and a == 0:
                chain.append((n_off - 1, a, False))
            chains.append(chain)
        jobs = [job for step in itertools.zip_longest(*chains) for job in step if job is not None]
        zs = [logits(kb, a, a + 1) for kb, a, _ in jobs]
        sps = [softplus2(z, diagonal) for z, (_, _, diagonal) in zip(zs, jobs)]
        sufs = [suffix(sp) for sp, _ in sps]
        car = [None] * nd
        atts = []
        for z, (sp, mask), (res, tot), (kb, a, _) in zip(zs, sps, sufs, jobs):
            atts.append(weights(z, res, car[a], mask))
            car[a] = tot if car[a] is None else car[a] + tot
        acc = [None] * nd
        for att, (kb, a, _) in zip(atts, jobs):
            t = _dot(att, values(kb))
            acc[a] = t if acc[a] is None else acc[a] + t
        for a in range(nd):
            acc_ref[a * RB:(a + 1) * RB, :] = acc[a]
            car_ref[a * RB:(a + 1) * RB, :] = jnp.broadcast_to(car[a], (RB, LANES))

    pl.when(i == 0)(functools.partial(near_tiles, False))
    pl.when(i > 0)(functools.partial(near_tiles, True))

    def pending():
        return jnp.max(mar_ref[...] - car_ref[...]) > 0.0

    def far_body(c):
        kb, _ = c
        tile(kb + 1, 1, nd)
        tile(kb, 0, 1)
        return kb - 1, pending()

    _, go = lax.while_loop(lambda c: jnp.logical_and(c[0] >= 0, c[1]), far_body, (n_off - 2, pending()))

    @pl.when(jnp.logical_and(go, n_off > 0))
    def _():
        tile(0, 1, nd)

    for a in range(nd):
        o_ref[0, a * TK:(a + 1) * TK, :] = jnp.where(
            (lane >> _HEAD_SHIFT) == 0, acc_ref[a * RB:a * RB + TK, :], acc_ref[a * RB + TK:(a + 1) * RB, :]
        ).astype(o_ref.dtype)


def _stick_breaking(pc):
    bsz, s, _ = pc.shape
    npair = SB_W // LANES
    return pl.pallas_call(
        _sb_kernel,
        out_shape=jax.ShapeDtypeStruct((bsz, s, SB_W), BF16),
        grid=(bsz, npair, s // SB_TQ),
        in_specs=[
            pl.BlockSpec((1, SB_TQ, LANES), lambda b, p, i: (b, i, p)),
            pl.BlockSpec((1, s, LANES), lambda b, p, i: (b, 0, npair + p)),
            pl.BlockSpec((1, s, LANES), lambda b, p, i: (b, 0, 2 * npair + p)),
        ],
        out_specs=pl.BlockSpec((1, SB_TQ, LANES), lambda b, p, i: (b, i, p)),
        scratch_shapes=[
            pltpu.VMEM((2 * SB_TQ, LANES), BF16),
            pltpu.VMEM((2 * SB_TQ, LANES), F32),
            pltpu.VMEM((2 * SB_TQ, LANES), F32),
            pltpu.VMEM((2 * SB_TQ, LANES), F32),
            pltpu.VMEM((8, LANES), F32),
        ],
        compiler_params=_params("parallel", "parallel", "arbitrary"),
        name="stick_breaking",
    )(pc, pc, pc)


def _t5_bucket(dist):
    max_exact = N_REL_BUCKETS // 2
    dd = np.maximum(dist, 1).astype(np.float64)
    large = max_exact + (np.log(dd / max_exact) / np.log(REL_MAX_DIST / max_exact)
                         * (N_REL_BUCKETS - max_exact)).astype(np.int32)
    large = np.minimum(large, N_REL_BUCKETS - 1)
    return np.where(dist < max_exact, dist, large).astype(np.int32)


def _dil_buckets():
    qi = np.arange(DIL_BLK)[:, None]
    kj = np.arange(2 * DIL_BLK)[None, :]
    delta = qi - kj + DIL_BLK
    in_win = (delta >= 0) & (delta <= DIL_BLK)
    tabs = []
    for window, dil in DIL_PAIRS:
        assert window // dil == DIL_BLK
        bucket = _t5_bucket(np.clip(delta, 0, None) * dil)
        tabs.append(np.where(in_win, bucket, -1).astype(np.int32))
    return np.stack(tabs, 0)


def _dil_kernel(q_ref, k_ref, v_ref, bkt_ref, tab_ref, o_ref, m_sc, l_sc, acc_sc):
    s_len = q_ref.shape[1]
    hp = pl.program_id(1)
    g = pl.program_id(2)
    blk = DIL_BLK
    lane = _iota((1, LANES), 1)
    lane_head = lane >> _HEAD_SHIFT

    @pl.when(g == 0)
    def _():
        m_sc[...] = jnp.full_like(m_sc, NEG_INF)
        l_sc[...] = jnp.zeros_like(l_sc)
        acc_sc[...] = jnp.zeros_like(acc_sc)

    kcol = _iota((2 * blk, 2 * blk), 1)

    def group(gi, dil):
        nb = s_len // (blk * dil)
        bkt = bkt_ref[gi]
        biases = []
        for hl in range(2):
            head = gi * DIL_HEADS_PER_GROUP + hp * 2 + hl
            bias = jnp.full((blk, 2 * blk), NEG_INF, F32)
            for bk in range(N_REL_BUCKETS):
                bias = jnp.where(bkt == bk, tab_ref[bk, head], bias)
            biases.append(bias)
        bias2 = jnp.concatenate(biases, axis=0)
        sel0 = lane_head == 0

        def stage_logits(t):
            r = t // nb
            n = t % nb
            q_start = r + dil * blk * n
            p_start = r + dil * blk * jnp.maximum(n - 1, 0)
            rows_q = pl.ds(q_start, blk, stride=dil) if dil > 1 else pl.ds(q_start, blk)
            rows_p = pl.ds(p_start, blk, stride=dil) if dil > 1 else pl.ds(p_start, blk)
            qb = q_ref[0, rows_q, :] * (DIL_DH ** -0.5)
            q2 = jnp.concatenate([jnp.where(sel0, qb, 0.0), jnp.where(sel0, 0.0, qb)], axis=0).astype(BF16)
            kk = jnp.concatenate([k_ref[0, rows_p, :], k_ref[0, rows_q, :]], axis=0).astype(BF16)
            logits = _dot_nt(q2, kk) + bias2
            logits = jnp.where(jnp.logical_and(n == 0, kcol < blk), NEG_INF, logits)
            return rows_q, rows_p, logits

        def stage_max(rows_q, logits):
            m_old = m_sc[rows_q, :]
            m_old_col = jnp.concatenate([m_old[:, 0:1], m_old[:, DIL_DH:DIL_DH + 1]], axis=0)
            m_col = jnp.maximum(m_old_col, jnp.max(logits, axis=1, keepdims=True))
            return m_old, m_col

        def stage_pv(rows_q, rows_p, logits, m_col):
            p = jnp.exp(logits - m_col)
            vv = jnp.concatenate([v_ref[0, rows_p, :], v_ref[0, rows_q, :]], axis=0)
            vv = jnp.concatenate([vv, jnp.ones_like(vv)], axis=1).astype(BF16)
            pv = _dot(p.astype(BF16), vv)
            return pv[:, LANES:], pv[:, :LANES]

        def body(tt, carry):
            ts = [tt * DIL_UNROLL + u for u in range(DIL_UNROLL)]
            s1 = [stage_logits(t) for t in ts]
            s2 = [stage_max(rq, lg) for rq, _, lg in s1]
            s3 = [stage_pv(rq, rp, lg, mc) for (rq, rp, lg), (_, mc) in zip(s1, s2)]
            outs = []
            for (rows_q, _, _), (m_old, m_col), (ps, pv) in zip(s1, s2, s3):
                m_new = jnp.where(sel0, m_col[0:blk], m_col[blk:])
                alpha = jnp.exp(m_old - m_new)
                l_new = alpha * l_sc[rows_q, :] + jnp.where(sel0, ps[0:blk], ps[blk:])
                a_new = alpha * acc_sc[rows_q, :] + jnp.where(sel0, pv[0:blk], pv[blk:])
                outs.append((rows_q, m_new, l_new, a_new))
            for rows_q, m_new, l_new, a_new in outs:
                m_sc[rows_q, :] = m_new
                l_sc[rows_q, :] = l_new
                acc_sc[rows_q, :] = a_new
            return carry

        lax.fori_loop(0, s_len // (blk * DIL_UNROLL), body, 0)

    for gi, (_, dil) in enumerate(DIL_PAIRS):
        pl.when(g == gi)(functools.partial(group, gi, dil))

    @pl.when(g == len(DIL_PAIRS) - 1)
    def _():
        o_ref[0] = (acc_sc[...] / l_sc[...]).astype(o_ref.dtype)


def _dilated(pd, rel_bias):
    bsz, s, _ = pd.shape
    ng = len(DIL_PAIRS)
    gw = DIL_HEADS_PER_GROUP * DIL_DH // LANES
    nq = DIL_W // LANES
    bkt = jnp.asarray(_dil_buckets())
    blk_spec = lambda base: pl.BlockSpec((1, s, LANES), lambda b, p, g: (b, 0, base + g * gw + p))
    return pl.pallas_call(
        _dil_kernel,
        out_shape=jax.ShapeDtypeStruct((bsz, s, DIL_OUT), BF16),
        grid=(bsz, gw, ng),
        in_specs=[
            blk_spec(0),
            blk_spec(nq),
            blk_spec(2 * nq),
            pl.BlockSpec((ng, DIL_BLK, 2 * DIL_BLK), lambda b, p, g: (0, 0, 0)),
            pl.BlockSpec(memory_space=pltpu.SMEM),
        ],
        out_specs=pl.BlockSpec((1, s, LANES), lambda b, p, g: (b, 0, p)),
        scratch_shapes=[pltpu.VMEM((s, LANES), F32)] * 3,
        compiler_params=_params("parallel", "parallel", "arbitrary"),
        name="dilated",
    )(pd, pd, pd, bkt, rel_bias)


def _ab_weight(w):
    o = np.cumsum((0,) + (GLA_QK, GLA_QK, GLA_V, GLA_V, GLA_GATE_RANK, 2 * ML_W, ML_W, MLSTM_HEADS, MLSTM_HEADS, ML_W))
    qa, ka, va, ra, aa, qkb, vb, ib, fb, ob = [w[:, o[j]:o[j + 1]] for j in range(10)]
    pad = jnp.zeros((w.shape[0], LANES - 2 * MLSTM_HEADS - GLA_GATE_RANK), w.dtype)
    return jnp.concatenate([qa, ka, va, ra, qkb, vb, ob, ib, fb, aa, pad], axis=1).astype(BF16)


def kernel(x, c, ada_w, ada_b, ln_g, ln_b, ab_w_in, gla_wa_up, gla_ba, gla_norm_g, ml_conv_w, ml_conv_b,
           ml_b_i, ml_b_f, ml_norm_g, ab_w_out, cd_w_in, rel_bias, cd_w_out, ffn_w1, ffn_w3, ffn_w2):
    bsz, s, d = x.shape
    mod_all = _ada_mod(c, ada_w, ada_b).reshape(DEPTH, bsz, 6, d)
    tm = 512
    for layer in range(DEPTH):
        mod = mod_all[layer]
        j = layer // 2
        if layer % 2 == 0:
            (proj,) = _inproj(x, mod, _ab_weight(ab_w_in[j]), [(P0_N, F32)], tm, 0, 1, "inproj0")
            cat = _mixer0(proj, gla_wa_up[j], gla_ba[j], gla_norm_g[j], ml_conv_w[j], ml_conv_b[j],
                          ml_b_i[j], ml_b_f[j], ml_norm_g[j])
            acts, w_outs = [cat], [ab_w_out[j].astype(BF16)]
        else:
            pc, pd = _inproj(x, mod, cd_w_in[j].astype(BF16), [(3 * SB_W, BF16), (3 * DIL_W, F32)],
                             tm, 0, 1, "inproj1")
            oc = _stick_breaking(pc)
            od = _dilated(pd, rel_bias)
            w_out = cd_w_out[j].astype(BF16)
            acts, w_outs = [oc, od], [w_out[:SB_W], w_out[SB_W:]]
        x = _post(acts, w_outs, x, mod, ffn_w1[layer], ffn_w3[layer], ffn_w2[layer], ln_g[layer], ln_b[layer], FFN_TM)
    return x
```

```python
import functools
import itertools

import numpy as np
import jax
import jax.numpy as jnp
from jax import lax
from jax.experimental import pallas as pl
from jax.experimental.pallas import tpu as pltpu

F32 = jnp.float32
BF16 = jnp.bfloat16

D_MODEL = 1024
DEPTH = 2
GLA_HEADS = 4
GLA_DK = 64
GLA_DV = 128
GLA_GATE_RANK = 16
GLA_TAU = 16.0
GLA_CHUNK = 64
MLSTM_HEADS = 4
MLSTM_DH = 128
MLSTM_CONV = 4
SB_HEADS = 8
SB_DH = 64
DIL_PAIRS = ((128, 1), (512, 4), (2048, 16))
DIL_HEADS_PER_GROUP = 4
DIL_DH = 64
DIL_BLK = 128
DIL_UNROLL = 4
N_REL_BUCKETS = 32
REL_MAX_DIST = 2048
D_FF = ((8 * D_MODEL + 3 * 256 - 1) // (3 * 256)) * 256
LN_EPS = 1e-5
RES_ALPHA = (2 * DEPTH) ** 0.25
NEG_INF = -1e30

GLA_QK = GLA_HEADS * GLA_DK
GLA_V = GLA_HEADS * GLA_DV
ML_W = MLSTM_HEADS * MLSTM_DH
SB_W = SB_HEADS * SB_DH
DIL_W = len(DIL_PAIRS) * DIL_HEADS_PER_GROUP * DIL_DH
DIL_OUT = DIL_HEADS_PER_GROUP * DIL_DH

LANES = 128
VMEM_LIMIT = 56 * 1024 * 1024

P0_QA = 0
P0_KA = P0_QA + GLA_QK
P0_VA = P0_KA + GLA_QK
P0_RA = P0_VA + GLA_V
P0_QKB = P0_RA + GLA_V
P0_VB = P0_QKB + 2 * ML_W
P0_OB = P0_VB + ML_W
P0_G = P0_OB + ML_W
P0_N = P0_G + LANES
G_I = 0
G_F = MLSTM_HEADS
G_A = 2 * MLSTM_HEADS

MIX_TILE = 256
FF_CHUNK = 256
FFN_TM = 512
SB_TQ = 512
SB_SUB = 128
SB_WIDE = 256
SB_ZERO_BITS = 160.0
SB_BOUND_SLACK = 1.01
LOG2E = 1.4426950408889634
_CHUNK_SHIFT = 6
_HEAD_SHIFT = 6
assert GLA_CHUNK == GLA_DK == 1 << _CHUNK_SHIFT and SB_DH == DIL_DH == 1 << _HEAD_SHIFT


def _dot(a, b):
    return jnp.dot(a, b, preferred_element_type=F32)


def _dot_nt(a, b):
    return lax.dot_general(a, b, (((1,), (1,)), ((), ())), preferred_element_type=F32)


def _dot_tn(a, b):
    return lax.dot_general(a, b, (((0,), (0,)), ((), ())), preferred_element_type=F32)


def _dot_split(t, x, terms):
    acc = None
    rem = x
    for i in range(terms):
        part = rem.astype(BF16)
        d = _dot(t, part)
        acc = d if acc is None else acc + d
        if i + 1 < terms:
            rem = rem - part.astype(F32)
    return acc


def _log_sigmoid(x):
    return jnp.minimum(x, 0.0) - jnp.log(1.0 + jnp.exp(-jnp.abs(x)))


def _silu(x):
    return x * jax.nn.sigmoid(x)


def _layer_norm(r, g, b):
    mu = jnp.mean(r, axis=-1, keepdims=True)
    d = r - mu
    var = jnp.mean(d * d, axis=-1, keepdims=True)
    return d * lax.rsqrt(var + LN_EPS) * g + b


def _head_norm(x, g):
    mu = jnp.mean(x, axis=-1, keepdims=True)
    d = x - mu
    var = jnp.mean(d * d, axis=-1, keepdims=True)
    return d * lax.rsqrt(var + LN_EPS) * g


def _iota(shape, dim):
    return lax.broadcasted_iota(jnp.int32, shape, dim)


def _params(*sem):
    return pltpu.CompilerParams(dimension_semantics=sem, vmem_limit_bytes=VMEM_LIMIT)


def _ada_kernel(c_ref, w_ref, b_ref, o_ref):
    ca = _silu(c_ref[...]).astype(BF16)
    o_ref[0] = _dot(ca, w_ref[0].astype(BF16)) + b_ref[0]


def _ada_mod(c, ada_w, ada_b):
    bsz, d = c.shape
    n = ada_w.shape[-1]
    tn = n // 4
    return pl.pallas_call(
        _ada_kernel,
        out_shape=jax.ShapeDtypeStruct((DEPTH, bsz, n), F32),
        grid=(DEPTH, n // tn),
        in_specs=[
            pl.BlockSpec((bsz, d), lambda l, j: (0, 0)),
            pl.BlockSpec((1, d, tn), lambda l, j: (l, 0, j)),
            pl.BlockSpec((1, 1, tn), lambda l, j: (l, 0, j)),
        ],
        out_specs=pl.BlockSpec((1, bsz, tn), lambda l, j: (l, 0, j)),
        compiler_params=_params("arbitrary", "arbitrary"),
        name="ada_mod",
    )(c, ada_w, ada_b.reshape(DEPTH, 1, n))


def _inproj_kernel(x_ref, mod_ref, w_ref, *o_refs, shift_row, scale_row):
    sh = mod_ref[0, shift_row:shift_row + 1, :]
    sc = mod_ref[0, scale_row:scale_row + 1, :]
    hm = (x_ref[0] * (1.0 + sc) + sh).astype(BF16)
    col = 0
    for o_ref in o_refs:
        n = o_ref.shape[-1]
        o_ref[0] = _dot(hm, w_ref[:, col:col + n]).astype(o_ref.dtype)
        col += n


def _inproj(x, mod, w, outs, tm, shift_row, scale_row, name):
    bsz, s, d = x.shape
    n = w.shape[1]
    assert sum(o[0] for o in outs) == n
    return pl.pallas_call(
        functools.partial(_inproj_kernel, shift_row=shift_row, scale_row=scale_row),
        out_shape=[jax.ShapeDtypeStruct((bsz, s, o[0]), o[1]) for o in outs],
        grid=(bsz, s // tm),
        in_specs=[
            pl.BlockSpec((1, tm, d), lambda b, i: (b, i, 0)),
            pl.BlockSpec((1, 6, d), lambda b, i: (b, 0, 0)),
            pl.BlockSpec((d, n), lambda b, i: (0, 0)),
        ],
        out_specs=[pl.BlockSpec((1, tm, o[0]), lambda b, i: (b, i, 0)) for o in outs],
        compiler_params=_params("parallel", "arbitrary"),
        name=name,
    )(x, mod, w)


def _post_kernel(*refs, n_act):
    act_refs = refs[:n_act]
    wo_refs = refs[n_act:2 * n_act]
    x_ref, mod_ref, w1_ref, w3_ref, w2_ref, g_ref, b_ref, o_ref, gm_ref = refs[2 * n_act:]
    y = None
    for a_ref, w_ref in zip(act_refs, wo_refs):
        t = _dot(a_ref[0], w_ref[...])
        y = t if y is None else y + t
    r = RES_ALPHA * x_ref[0] + (1.0 + mod_ref[0, 2:3, :]) * y
    xm = _layer_norm(r, g_ref[0:1, :], b_ref[0:1, :])
    hf = (xm * (1.0 + mod_ref[0, 4:5, :]) + mod_ref[0, 3:4, :]).astype(BF16)
    for c in range(D_FF // FF_CHUNK):
        cs = slice(c * FF_CHUNK, (c + 1) * FF_CHUNK)
        gm_ref[:, cs] = (_silu(_dot(hf, w1_ref[:, cs])) * _dot(hf, w3_ref[:, cs])).astype(BF16)
    y2 = _dot(gm_ref[...], w2_ref[...])
    r2 = RES_ALPHA * xm + (1.0 + mod_ref[0, 5:6, :]) * y2
    o_ref[0] = _layer_norm(r2, g_ref[1:2, :], b_ref[1:2, :])


def _post(acts, w_outs, x, mod, w1, w3, w2, ln_g, ln_b, tm):
    bsz, s, d = x.shape
    n_act = len(acts)
    const = lambda shape: pl.BlockSpec(shape, lambda b, i: (0,) * len(shape), pipeline_mode=pl.Buffered(1))
    in_specs = [pl.BlockSpec((1, tm, a.shape[-1]), lambda b, i: (b, i, 0)) for a in acts]
    in_specs += [const(w.shape) for w in w_outs]
    in_specs += [
        pl.BlockSpec((1, tm, d), lambda b, i: (b, i, 0)),
        pl.BlockSpec((1, 6, d), lambda b, i: (b, 0, 0)),
        const((d, D_FF)),
        const((d, D_FF)),
        const((D_FF, d)),
        const((2, d)),
        const((2, d)),
    ]
    return pl.pallas_call(
        functools.partial(_post_kernel, n_act=n_act),
        out_shape=jax.ShapeDtypeStruct((bsz, s, d), F32),
        grid=(bsz, s // tm),
        in_specs=in_specs,
        out_specs=pl.BlockSpec((1, tm, d), lambda b, i: (b, i, 0)),
        scratch_shapes=[pltpu.VMEM((tm, D_FF), BF16)],
        compiler_params=_params("parallel", "arbitrary"),
        name="post",
    )(*acts, *w_outs, x, mod, w1.astype(BF16), w3.astype(BF16), w2.astype(BF16), ln_g, ln_b)


def _mixer0_kernel(p_ref, wa_ref, ba_ref, gg_ref, cw_ref, cb_ref, gb_ref, mg_ref, o_ref,
                   st_ref, cst_ref, m_ref, xc_ref):
    L = MIX_TILE
    i = pl.program_id(1)

    @pl.when(i == 0)
    def _():
        st_ref[...] = jnp.zeros_like(st_ref)
        cst_ref[...] = jnp.zeros_like(cst_ref)
        m_ref[...] = jnp.zeros_like(m_ref)
        xc_ref[0:8, :] = jnp.zeros((8, 2 * ML_W), F32)

    lane = _iota((1, LANES), 1)
    g_raw = p_ref[0, :, P0_G:P0_G + LANES]

    u = _dot(g_raw.astype(BF16), wa_ref[...]) + ba_ref[...]
    la = _log_sigmoid(u) * (1.0 / GLA_TAU)
    row = _iota((L, L), 0)
    col = _iota((L, L), 1)
    same_chunk = (row >> _CHUNK_SHIFT) == (col >> _CHUNK_SHIFT)
    tri_blk = jnp.where(same_chunk & (col <= row), 1.0, 0.0).astype(BF16)
    bcs = _dot_split(tri_blk, la, 3)
    q_in = p_ref[0, :, P0_QA:P0_QA + GLA_QK] * (GLA_DK ** -0.5) * jnp.exp(bcs)
    k_raw = p_ref[0, :, P0_KA:P0_KA + GLA_QK]
    k_in = k_raw * jnp.exp(-bcs)
    lane_qk = _iota((1, GLA_QK), 1)
    head_masks = [(lane_qk >> _CHUNK_SHIFT) == h for h in range(GLA_HEADS)]
    r4 = _iota((GLA_HEADS * GLA_CHUNK, GLA_CHUNK), 0)
    c4 = _iota((GLA_HEADS * GLA_CHUNK, GLA_CHUNK), 1)
    tril4 = c4 <= (r4 & (GLA_CHUNK - 1))
    oa_chunks = []
    for c in range(L // GLA_CHUNK):
        r0, r1 = c * GLA_CHUNK, (c + 1) * GLA_CHUNK
        b_c = bcs[r0:r1]
        bl = b_c[GLA_CHUNK - 1:GLA_CHUNK, :]
        q_c = q_in[r0:r1]
        k_c = k_in[r0:r1].astype(BF16)
        k_end = k_raw[r0:r1] * jnp.exp(bl - b_c)
        v_c = p_ref[0, r0:r1, P0_VA:P0_VA + GLA_V]
        q_exp = jnp.concatenate([jnp.where(hm, q_c, 0.0) for hm in head_masks], axis=0).astype(BF16)
        k_exp = jnp.concatenate([jnp.where(hm, k_end, 0.0) for hm in head_masks], axis=0).astype(BF16)
        sc = jnp.where(tril4, _dot_nt(q_exp, k_c), 0.0)
        intra = _dot(sc.astype(BF16), v_c.astype(BF16))
        inter = _dot_nt(q_exp, st_ref[...].astype(BF16))
        o_heads = []
        for h in range(GLA_HEADS):
            h0, h1 = h * GLA_CHUNK, (h + 1) * GLA_CHUNK
            o_heads.append(intra[h0:h1, h * GLA_DV:(h + 1) * GLA_DV] + inter[h0:h1])
        oa_chunks.append(o_heads)
        v_cat = jnp.concatenate([v_c[:, h * GLA_DV:(h + 1) * GLA_DV] for h in range(GLA_HEADS)], axis=0)
        st_ref[...] = st_ref[...] * jnp.exp(bl) + _dot_tn(v_cat.astype(BF16), k_exp)
    for h in range(GLA_HEADS):
        o_h = jnp.concatenate([oc[h] for oc in oa_chunks], axis=0)
        cs = slice(h * GLA_DV, (h + 1) * GLA_DV)
        ra = p_ref[0, :, P0_RA + h * GLA_DV:P0_RA + (h + 1) * GLA_DV]
        o_ref[0, :, cs] = (_head_norm(o_h, gg_ref[:, cs]) * _silu(ra)).astype(o_ref.dtype)

    xc_ref[8:8 + L, :] = p_ref[0, :, P0_QKB:P0_QKB + 2 * ML_W]
    conv = cb_ref[...]
    for kk in range(MLSTM_CONV):
        conv = conv + cw_ref[kk:kk + 1, :] * xc_ref[8 - (MLSTM_CONV - 1) + kk:8 - (MLSTM_CONV - 1) + kk + L, :]
    xc_ref[0:8, :] = xc_ref[L:L + 8, :]
    qk_b = _silu(conv)

    gb = jnp.where(lane < G_A, g_raw + gb_ref[...], 0.0)
    ipre = gb
    logf = _log_sigmoid(pltpu.roll(gb, LANES - G_F, axis=1))
    logf = jnp.where(lane < MLSTM_HEADS, logf, 0.0)
    tri = jnp.where(col <= row, 1.0, 0.0).astype(BF16)
    bcum = _dot_split(tri, logf, 3)
    blast = bcum[L - 1:L, :]
    wend = blast - bcum + ipre
    m_prev = m_ref[...]
    m_new = jnp.maximum(blast + m_prev, jnp.max(wend, axis=0, keepdims=True))
    scl = jnp.exp(blast + m_prev - m_new)
    wj = jnp.exp(wend - m_new)
    rows_src = jnp.where(lane < MLSTM_HEADS, ipre, pltpu.roll(bcum, MLSTM_HEADS, axis=1))
    rows_t = rows_src.T
    causal = col <= row
    ones = jnp.ones((L, MLSTM_DH), F32)
    for h in range(MLSTM_HEADS):
        hs = slice(h * MLSTM_DH, (h + 1) * MLSTM_DH)
        b_col = bcum[:, h:h + 1]
        ip_row = rows_t[h:h + 1, :]
        b_row = rows_t[MLSTM_HEADS + h:MLSTM_HEADS + h + 1, :]
        dlog = jnp.where(causal, b_col - b_row + ip_row, NEG_INF)
        il = b_col + m_prev[:, h:h + 1]
        m_i = jnp.maximum(il, jnp.max(dlog, axis=1, keepdims=True))
        w_intra = jnp.exp(dlog - m_i)
        s_inter = jnp.exp(il - m_i)
        q_h = qk_b[:, h * MLSTM_DH:(h + 1) * MLSTM_DH].astype(BF16)
        k_h = (qk_b[:, ML_W + h * MLSTM_DH:ML_W + (h + 1) * MLSTM_DH] * (MLSTM_DH ** -0.5)).astype(BF16)
        v_h = p_ref[0, :, P0_VB + h * MLSTM_DH:P0_VB + (h + 1) * MLSTM_DH]
        v_aug = jnp.concatenate([v_h, ones], axis=1)
        a = (w_intra * _dot_nt(q_h, k_h)).astype(BF16)
        c_prev = cst_ref[h]
        tot = _dot(a, v_aug.astype(BF16)) + s_inter * _dot_nt(q_h, c_prev.astype(BF16))
        num = tot[:, :MLSTM_DH]
        den = tot[:, MLSTM_DH:]
        hid = num / jnp.maximum(jnp.abs(den), jnp.exp(-m_i))
        w_aug = (v_aug * wj[:, h:h + 1]).astype(BF16)
        cst_ref[h] = scl[:, h:h + 1] * c_prev + _dot_tn(w_aug, k_h)
        ob = p_ref[0, :, P0_OB + h * MLSTM_DH:P0_OB + (h + 1) * MLSTM_DH]
        o_ref[0, :, GLA_V + h * MLSTM_DH:GLA_V + (h + 1) * MLSTM_DH] = (
            jax.nn.sigmoid(ob) * _head_norm(hid, mg_ref[:, hs])).astype(o_ref.dtype)
    m_ref[...] = m_new


def _mixer0(proj, wa_up, ba, gla_g, conv_w, conv_b, b_i, b_f, ml_g):
    bsz, s, n = proj.shape
    L = MIX_TILE
    wa_pad = jnp.zeros((LANES, GLA_QK), F32).at[G_A:G_A + GLA_GATE_RANK].set(wa_up).astype(BF16)
    gbias = jnp.zeros((1, LANES), F32).at[0, G_I:G_I + MLSTM_HEADS].set(b_i).at[0, G_F:G_F + MLSTM_HEADS].set(b_f)
    full = lambda shape: pl.BlockSpec(shape, lambda b, i: (0,) * len(shape))
    return pl.pallas_call(
        _mixer0_kernel,
        out_shape=jax.ShapeDtypeStruct((bsz, s, GLA_V + ML_W), BF16),
        grid=(bsz, s // L),
        in_specs=[
            pl.BlockSpec((1, L, n), lambda b, i: (b, i, 0)),
            full((LANES, GLA_QK)),
            full((1, GLA_QK)),
            full((1, GLA_V)),
            full((MLSTM_CONV, 2 * ML_W)),
            full((1, 2 * ML_W)),
            full((1, LANES)),
            full((1, ML_W)),
        ],
        out_specs=pl.BlockSpec((1, L, GLA_V + ML_W), lambda b, i: (b, i, 0)),
        scratch_shapes=[
            pltpu.VMEM((GLA_DV, GLA_QK), F32),
            pltpu.VMEM((MLSTM_HEADS, 2 * MLSTM_DH, MLSTM_DH), F32),
            pltpu.VMEM((1, LANES), F32),
            pltpu.VMEM((8 + L, 2 * ML_W), F32),
        ],
        compiler_params=_params("parallel", "arbitrary"),
        name="mixer0",
    )(proj, wa_pad, ba.reshape(1, GLA_QK), gla_g.reshape(1, GLA_V), conv_w, conv_b.reshape(1, 2 * ML_W),
      gbias, ml_g.reshape(1, ML_W))


def _sb_kernel(q_ref, k_ref, v_ref, o_ref, qs_ref, acc_ref, car_ref, mar_ref, kn_ref):
    TQ, SUB, WIDE = SB_TQ, SB_SUB, SB_WIDE
    ns = TQ // SUB
    RB = 2 * SUB
    i = pl.program_id(2)
    lane = _iota((1, LANES), 1)
    head0 = (lane >> _HEAD_SHIFT) == 0

    @pl.when(i == 0)
    def _():
        k2 = jnp.square(k_ref[0].astype(F32))
        for h in range(2):
            n2 = jnp.sum(jnp.where(head0 if h == 0 else ~head0, k2, 0.0), axis=1, keepdims=True)
            kn_ref[h:h + 1, :] = jnp.broadcast_to(jnp.sqrt(jnp.max(n2, axis=0, keepdims=True)), (1, LANES))

    for a in range(ns):
        qa = q_ref[0, a * SUB:(a + 1) * SUB, :].astype(F32) * (SB_DH ** -0.5 * LOG2E)
        for h in range(2):
            rows = slice(a * RB + h * SUB, a * RB + (h + 1) * SUB)
            qh = jnp.where(head0 if h == 0 else ~head0, qa, 0.0).astype(BF16)
            qs_ref[rows, :] = qh
            qn = jnp.sqrt(jnp.sum(jnp.square(qh.astype(F32)), axis=1, keepdims=True))
            mar_ref[rows, :] = qn * kn_ref[h:h + 1, :] * SB_BOUND_SLACK + SB_ZERO_BITS
    tri = jnp.where(_iota((WIDE, WIDE), 0) >= _iota((WIDE, WIDE), 1), -1.0, 0.0).astype(BF16)

    def key_rows(ku, nk):
        r0 = ku * SUB
        return pl.ds(r0 if isinstance(r0, int) else pl.multiple_of(r0, SUB), nk)

    def logits(a, ku, nk):
        return _dot_nt(qs_ref[a * RB:(a + 1) * RB, :], k_ref[0, key_rows(ku, nk), :])

    def softplus2(z, diagonal):
        sp = jnp.maximum(z, 0.0) + jnp.log(1.0 + jnp.exp2(-jnp.abs(z))) * LOG2E
        if not diagonal:
            return sp, None
        mask = _iota(z.shape, 1) < (_iota(z.shape, 0) & (SUB - 1))
        return jnp.where(mask, sp, 0.0), mask

    def suffix(sp):
        nk = sp.shape[1]
        return _dot(sp.astype(BF16), tri[:nk, :nk]), jnp.sum(sp, axis=1, keepdims=True)

    def weights(z, res, car, mask):
        att = jnp.exp2(z + res if car is None else z + res - car)
        if mask is not None:
            att = jnp.where(mask, att, 0.0)
        return att.astype(BF16)

    def far_tile(a, ku):
        rows = slice(a * RB, (a + 1) * RB)
        z = logits(a, ku, SUB)
        sp, _ = softplus2(z, False)
        res, tot = suffix(sp)
        car = car_ref[rows, :]
        acc_ref[rows, :] += _dot(weights(z, res, car, None), v_ref[0, key_rows(ku, SUB), :])
        car_ref[rows, :] = car + tot

    def near_tiles(first_block):
        chains = []
        for a in range(ns):
            g = a if first_block else ns * i + a
            chain = [(a, g, SUB, True)]
            if not first_block or a >= 2:
                chain.append((a, g - 2, WIDE, False))
            elif a == 1:
                chain.append((a, 0, SUB, False))
            chains.append(chain)
        jobs = [job for step in itertools.zip_longest(*chains) for job in step if job is not None]
        zs = [logits(a, ku, nk) for a, ku, nk, _ in jobs]
        sps = [softplus2(z, diagonal) for z, (_, _, _, diagonal) in zip(zs, jobs)]
        sufs = [suffix(sp) for sp, _ in sps]
        car = [None] * ns
        atts = []
        for z, (sp, mask), (res, tot), (a, _, _, _) in zip(zs, sps, sufs, jobs):
            atts.append(weights(z, res, car[a], mask))
            car[a] = tot if car[a] is None else car[a] + tot
        acc = [None] * ns
        for att, (a, ku, nk, _) in zip(atts, jobs):
            t = _dot(att, v_ref[0, key_rows(ku, nk), :])
            acc[a] = t if acc[a] is None else acc[a] + t
        for a in range(ns):
            acc_ref[a * RB:(a + 1) * RB, :] = acc[a]
            car_ref[a * RB:(a + 1) * RB, :] = jnp.broadcast_to(car[a], (RB, LANES))

    pl.when(i == 0)(functools.partial(near_tiles, True))
    pl.when(i > 0)(functools.partial(near_tiles, False))

    def pending():
        return jnp.max(mar_ref[...] - car_ref[...]) > 0.0

    def far_body(c):
        t, _ = c
        for a in range(ns):
            ku = ns * i + a - 3 - t
            pl.when(ku >= 0)(functools.partial(far_tile, a, ku))
        return t + 1, pending()

    lax.while_loop(lambda c: jnp.logical_and(ns * i + ns - 4 - c[0] >= 0, c[1]), far_body, (0, pending()))

    for a in range(ns):
        o_ref[0, a * SUB:(a + 1) * SUB, :] = jnp.where(
            head0, acc_ref[a * RB:a * RB + SUB, :], acc_ref[a * RB + SUB:(a + 1) * RB, :]).astype(o_ref.dtype)


def _stick_breaking(pc):
    bsz, s, _ = pc.shape
    npair = SB_W // LANES
    return pl.pallas_call(
        _sb_kernel,
        out_shape=jax.ShapeDtypeStruct((bsz, s, SB_W), BF16),
        grid=(bsz, npair, s // SB_TQ),
        in_specs=[
            pl.BlockSpec((1, SB_TQ, LANES), lambda b, p, i: (b, i, p)),
            pl.BlockSpec((1, s, LANES), lambda b, p, i: (b, 0, npair + p)),
            pl.BlockSpec((1, s, LANES), lambda b, p, i: (b, 0, 2 * npair + p)),
        ],
        out_specs=pl.BlockSpec((1, SB_TQ, LANES), lambda b, p, i: (b, i, p)),
        scratch_shapes=[
            pltpu.VMEM((2 * SB_TQ, LANES), BF16),
            pltpu.VMEM((2 * SB_TQ, LANES), F32),
            pltpu.VMEM((2 * SB_TQ, LANES), F32),
            pltpu.VMEM((2 * SB_TQ, LANES), F32),
            pltpu.VMEM((8, LANES), F32),
        ],
        compiler_params=_params("parallel", "parallel", "arbitrary"),
        name="stick_breaking",
    )(pc, pc, pc)


def _t5_bucket(dist):
    max_exact = N_REL_BUCKETS // 2
    dd = np.maximum(dist, 1).astype(np.float64)
    large = max_exact + (np.log(dd / max_exact) / np.log(REL_MAX_DIST / max_exact)
                         * (N_REL_BUCKETS - max_exact)).astype(np.int32)
    large = np.minimum(large, N_REL_BUCKETS - 1)
    return np.where(dist < max_exact, dist, large).astype(np.int32)


def _dil_buckets():
    qi = np.arange(DIL_BLK)[:, None]
    kj = np.arange(2 * DIL_BLK)[None, :]
    delta = qi - kj + DIL_BLK
    in_win = (delta >= 0) & (delta <= DIL_BLK)
    tabs = []
    for window, dil in DIL_PAIRS:
        assert window // dil == DIL_BLK
        bucket = _t5_bucket(np.clip(delta, 0, None) * dil)
        tabs.append(np.where(in_win, bucket, -1).astype(np.int32))
    return np.stack(tabs, 0)


def _dil_kernel(q_ref, k_ref, v_ref, bkt_ref, tab_ref, o_ref, m_sc, l_sc, acc_sc):
    s_len = q_ref.shape[1]
    hp = pl.program_id(1)
    g = pl.program_id(2)
    blk = DIL_BLK
    lane = _iota((1, LANES), 1)
    lane_head = lane >> _HEAD_SHIFT

    @pl.when(g == 0)
    def _():
        m_sc[...] = jnp.full_like(m_sc, NEG_INF)
        l_sc[...] = jnp.zeros_like(l_sc)
        acc_sc[...] = jnp.zeros_like(acc_sc)

    kcol = _iota((2 * blk, 2 * blk), 1)

    def group(gi, dil):
        nb = s_len // (blk * dil)
        bkt = bkt_ref[gi]
        biases = []
        for hl in range(2):
            head = gi * DIL_HEADS_PER_GROUP + hp * 2 + hl
            bias = jnp.full((blk, 2 * blk), NEG_INF, F32)
            for bk in range(N_REL_BUCKETS):
                bias = jnp.where(bkt == bk, tab_ref[bk, head], bias)
            biases.append(bias)
        bias2 = jnp.concatenate(biases, axis=0)
        sel0 = lane_head == 0

        def stage_logits(t):
            r = t // nb
            n = t % nb
            q_start = r + dil * blk * n
            p_start = r + dil * blk * jnp.maximum(n - 1, 0)
            rows_q = pl.ds(q_start, blk, stride=dil) if dil > 1 else pl.ds(q_start, blk)
            rows_p = pl.ds(p_start, blk, stride=dil) if dil > 1 else pl.ds(p_start, blk)
            qb = q_ref[0, rows_q, :] * (DIL_DH ** -0.5)
            q2 = jnp.concatenate([jnp.where(sel0, qb, 0.0), jnp.where(sel0, 0.0, qb)], axis=0).astype(BF16)
            kk = jnp.concatenate([k_ref[0, rows_p, :], k_ref[0, rows_q, :]], axis=0).astype(BF16)
            logits = _dot_nt(q2, kk) + bias2
            logits = jnp.where(jnp.logical_and(n == 0, kcol < blk), NEG_INF, logits)
            return rows_q, rows_p, logits

        def stage_max(rows_q, logits):
            m_old = m_sc[rows_q, :]
            m_old_col = jnp.concatenate([m_old[:, 0:1], m_old[:, DIL_DH:DIL_DH + 1]], axis=0)
            m_col = jnp.maximum(m_old_col, jnp.max(logits, axis=1, keepdims=True))
            return m_old, m_col

        def stage_pv(rows_q, rows_p, logits, m_col):
            p = jnp.exp(logits - m_col)
            vv = jnp.concatenate([v_ref[0, rows_p, :], v_ref[0, rows_q, :]], axis=0)
            vv = jnp.concatenate([vv, jnp.ones_like(vv)], axis=1).astype(BF16)
            pv = _dot(p.astype(BF16), vv)
            return pv[:, LANES:], pv[:, :LANES]

        def body(tt, carry):
            ts = [tt * DIL_UNROLL + u for u in range(DIL_UNROLL)]
            s1 = [stage_logits(t) for t in ts]
            s2 = [stage_max(rq, lg) for rq, _, lg in s1]
            s3 = [stage_pv(rq, rp, lg, mc) for (rq, rp, lg), (_, mc) in zip(s1, s2)]
            outs = []
            for (rows_q, _, _), (m_old, m_col), (ps, pv) in zip(s1, s2, s3):
                m_new = jnp.where(sel0, m_col[0:blk], m_col[blk:])
                alpha = jnp.exp(m_old - m_new)
                l_new = alpha * l_sc[rows_q, :] + jnp.where(sel0, ps[0:blk], ps[blk:])
                a_new = alpha * acc_sc[rows_q, :] + jnp.where(sel0, pv[0:blk], pv[blk:])
                outs.append((rows_q, m_new, l_new, a_new))
            for rows_q, m_new, l_new, a_new in outs:
                m_sc[rows_q, :] = m_new
                l_sc[rows_q, :] = l_new
                acc_sc[rows_q, :] = a_new
            return carry

        lax.fori_loop(0, s_len // (blk * DIL_UNROLL), body, 0)

    for gi, (_, dil) in enumerate(DIL_PAIRS):
        pl.when(g == gi)(functools.partial(group, gi, dil))

    @pl.when(g == len(DIL_PAIRS) - 1)
    def _():
        o_ref[0] = (acc_sc[...] / l_sc[...]).astype(o_ref.dtype)


def _dilated(pd, rel_bias):
    bsz, s, _ = pd.shape
    ng = len(DIL_PAIRS)
    gw = DIL_HEADS_PER_GROUP * DIL_DH // LANES
    nq = DIL_W // LANES
    bkt = jnp.asarray(_dil_buckets())
    blk_spec = lambda base: pl.BlockSpec((1, s, LANES), lambda b, p, g: (b, 0, base + g * gw + p))
    return pl.pallas_call(
        _dil_kernel,
        out_shape=jax.ShapeDtypeStruct((bsz, s, DIL_OUT), BF16),
        grid=(bsz, gw, ng),
        in_specs=[
            blk_spec(0),
            blk_spec(nq),
            blk_spec(2 * nq),
            pl.BlockSpec((ng, DIL_BLK, 2 * DIL_BLK), lambda b, p, g: (0, 0, 0)),
            pl.BlockSpec(memory_space=pltpu.SMEM),
        ],
        out_specs=pl.BlockSpec((1, s, LANES), lambda b, p, g: (b, 0, p)),
        scratch_shapes=[pltpu.VMEM((s, LANES), F32)] * 3,
        compiler_params=_params("parallel", "parallel", "arbitrary"),
        name="dilated",
    )(pd, pd, pd, bkt, rel_bias)


def _ab_weight(w):
    o = np.cumsum((0,) + (GLA_QK, GLA_QK, GLA_V, GLA_V, GLA_GATE_RANK, 2 * ML_W, ML_W, MLSTM_HEADS, MLSTM_HEADS, ML_W))
    qa, ka, va, ra, aa, qkb, vb, ib, fb, ob = [w[:, o[j]:o[j + 1]] for j in range(10)]
    pad = jnp.zeros((w.shape[0], LANES - 2 * MLSTM_HEADS - GLA_GATE_RANK), w.dtype)
    return jnp.concatenate([qa, ka, va, ra, qkb, vb, ob, ib, fb, aa, pad], axis=1).astype(BF16)


def kernel(x, c, ada_w, ada_b, ln_g, ln_b, ab_w_in, gla_wa_up, gla_ba, gla_norm_g, ml_conv_w, ml_conv_b,
           ml_b_i, ml_b_f, ml_norm_g, ab_w_out, cd_w_in, rel_bias, cd_w_out, ffn_w1, ffn_w3, ffn_w2):
    bsz, s, d = x.shape
    mod_all = _ada_mod(c, ada_w, ada_b).reshape(DEPTH, bsz, 6, d)
    tm = 512
    for layer in range(DEPTH):
        mod = mod_all[layer]
        j = layer // 2
        if layer % 2 == 0:
            (proj,) = _inproj(x, mod, _ab_weight(ab_w_in[j]), [(P0_N, F32)], tm, 0, 1, "inproj0")
            cat = _mixer0(proj, gla_wa_up[j], gla_ba[j], gla_norm_g[j], ml_conv_w[j], ml_conv_b[j],
                          ml_b_i[j], ml_b_f[j], ml_norm_g[j])
            acts, w_outs = [cat], [ab_w_out[j].astype(BF16)]
        else:
            pc, pd = _inproj(x, mod, cd_w_in[j].astype(BF16), [(3 * SB_W, BF16), (3 * DIL_W, F32)],
                             tm, 0, 1, "inproj1")
            oc = _stick_breaking(pc)
            od = _dilated(pd, rel_bias)
            w_out = cd_w_out[j].astype(BF16)
            acts, w_outs = [oc, od], [w_out[:SB_W], w_out[SB_W:]]
        x = _post(acts, w_outs, x, mod, ffn_w1[layer], ffn_w3[layer], ffn_w2[layer], ln_g[layer], ln_b[layer], FFN_TM)
    return x
```

```python
import functools
import itertools

import numpy as np
import jax
import jax.numpy as jnp
from jax import lax
from jax.experimental import pallas as pl
from jax.experimental.pallas import tpu as pltpu

F32 = jnp.float32
BF16 = jnp.bfloat16

D_MODEL = 1024
DEPTH = 2
GLA_HEADS = 4
GLA_DK = 64
GLA_DV = 128
GLA_GATE_RANK = 16
GLA_TAU = 16.0
GLA_CHUNK = 64
MLSTM_HEADS = 4
MLSTM_DH = 128
MLSTM_CONV = 4
SB_HEADS = 8
SB_DH = 64
DIL_PAIRS = ((128, 1), (512, 4), (2048, 16))
DIL_HEADS_PER_GROUP = 4
DIL_DH = 64
DIL_BLK = 128
DIL_UNROLL = 4
N_REL_BUCKETS = 32
REL_MAX_DIST = 2048
D_FF = ((8 * D_MODEL + 3 * 256 - 1) // (3 * 256)) * 256
LN_EPS = 1e-5
RES_ALPHA = (2 * DEPTH) ** 0.25
NEG_INF = -1e30

GLA_QK = GLA_HEADS * GLA_DK
GLA_V = GLA_HEADS * GLA_DV
ML_W = MLSTM_HEADS * MLSTM_DH
SB_W = SB_HEADS * SB_DH
DIL_W = len(DIL_PAIRS) * DIL_HEADS_PER_GROUP * DIL_DH
DIL_OUT = DIL_HEADS_PER_GROUP * DIL_DH

LANES = 128
VMEM_LIMIT = 56 * 1024 * 1024

P0_QA = 0
P0_KA = P0_QA + GLA_QK
P0_VA = P0_KA + GLA_QK
P0_RA = P0_VA + GLA_V
P0_QKB = P0_RA + GLA_V
P0_VB = P0_QKB + 2 * ML_W
P0_OB = P0_VB + ML_W
P0_G = P0_OB + ML_W
P0_N = P0_G + LANES
G_I = 0
G_F = MLSTM_HEADS
G_A = 2 * MLSTM_HEADS

MIX_TILE = 256
FF_CHUNK = 256
FFN_TM = 512
SB_TQ = 1024
SB_SUB = 128
SB_WIDE = 256
SB_ZERO_BITS = 160.0
SB_BOUND_SLACK = 1.01
LOG2E = 1.4426950408889634
_CHUNK_SHIFT = 6
_HEAD_SHIFT = 6
assert GLA_CHUNK == GLA_DK == 1 << _CHUNK_SHIFT and SB_DH == DIL_DH == 1 << _HEAD_SHIFT


def _dot(a, b):
    return jnp.dot(a, b, preferred_element_type=F32)


def _dot_nt(a, b):
    return lax.dot_general(a, b, (((1,), (1,)), ((), ())), preferred_element_type=F32)


def _dot_tn(a, b):
    return lax.dot_general(a, b, (((0,), (0,)), ((), ())), preferred_element_type=F32)


def _dot_split(t, x, terms):
    acc = None
    rem = x
    for i in range(terms):
        part = rem.astype(BF16)
        d = _dot(t, part)
        acc = d if acc is None else acc + d
        if i + 1 < terms:
            rem = rem - part.astype(F32)
    return acc


def _log_sigmoid(x):
    return jnp.minimum(x, 0.0) - jnp.log(1.0 + jnp.exp(-jnp.abs(x)))


def _silu(x):
    return x * jax.nn.sigmoid(x)


def _layer_norm(r, g, b):
    mu = jnp.mean(r, axis=-1, keepdims=True)
    d = r - mu
    var = jnp.mean(d * d, axis=-1, keepdims=True)
    return d * lax.rsqrt(var + LN_EPS) * g + b


def _head_norm(x, g):
    mu = jnp.mean(x, axis=-1, keepdims=True)
    d = x - mu
    var = jnp.mean(d * d, axis=-1, keepdims=True)
    return d * lax.rsqrt(var + LN_EPS) * g


def _iota(shape, dim):
    return lax.broadcasted_iota(jnp.int32, shape, dim)


def _params(*sem):
    return pltpu.CompilerParams(dimension_semantics=sem, vmem_limit_bytes=VMEM_LIMIT)


def _ada_kernel(c_ref, w_ref, b_ref, o_ref):
    ca = _silu(c_ref[...]).astype(BF16)
    o_ref[0] = _dot(ca, w_ref[0].astype(BF16)) + b_ref[0]


def _ada_mod(c, ada_w, ada_b):
    bsz, d = c.shape
    n = ada_w.shape[-1]
    tn = n // 4
    return pl.pallas_call(
        _ada_kernel,
        out_shape=jax.ShapeDtypeStruct((DEPTH, bsz, n), F32),
        grid=(DEPTH, n // tn),
        in_specs=[
            pl.BlockSpec((bsz, d), lambda l, j: (0, 0)),
            pl.BlockSpec((1, d, tn), lambda l, j: (l, 0, j)),
            pl.BlockSpec((1, 1, tn), lambda l, j: (l, 0, j)),
        ],
        out_specs=pl.BlockSpec((1, bsz, tn), lambda l, j: (l, 0, j)),
        compiler_params=_params("arbitrary", "arbitrary"),
        name="ada_mod",
    )(c, ada_w, ada_b.reshape(DEPTH, 1, n))


def _inproj_kernel(x_ref, mod_ref, w_ref, *o_refs, shift_row, scale_row):
    sh = mod_ref[0, shift_row:shift_row + 1, :]
    sc = mod_ref[0, scale_row:scale_row + 1, :]
    hm = (x_ref[0] * (1.0 + sc) + sh).astype(BF16)
    col = 0
    for o_ref in o_refs:
        n = o_ref.shape[-1]
        o_ref[0] = _dot(hm, w_ref[:, col:col + n]).astype(o_ref.dtype)
        col += n


def _inproj(x, mod, w, outs, tm, shift_row, scale_row, name):
    bsz, s, d = x.shape
    n = w.shape[1]
    assert sum(o[0] for o in outs) == n
    return pl.pallas_call(
        functools.partial(_inproj_kernel, shift_row=shift_row, scale_row=scale_row),
        out_shape=[jax.ShapeDtypeStruct((bsz, s, o[0]), o[1]) for o in outs],
        grid=(bsz, s // tm),
        in_specs=[
            pl.BlockSpec((1, tm, d), lambda b, i: (b, i, 0)),
            pl.BlockSpec((1, 6, d), lambda b, i: (b, 0, 0)),
            pl.BlockSpec((d, n), lambda b, i: (0, 0)),
        ],
        out_specs=[pl.BlockSpec((1, tm, o[0]), lambda b, i: (b, i, 0)) for o in outs],
        compiler_params=_params("parallel", "arbitrary"),
        name=name,
    )(x, mod, w)


def _post_kernel(*refs, n_act):
    act_refs = refs[:n_act]
    wo_refs = refs[n_act:2 * n_act]
    x_ref, mod_ref, w1_ref, w3_ref, w2_ref, g_ref, b_ref, o_ref, gm_ref = refs[2 * n_act:]
    y = None
    for a_ref, w_ref in zip(act_refs, wo_refs):
        t = _dot(a_ref[0], w_ref[...])
        y = t if y is None else y + t
    r = RES_ALPHA * x_ref[0] + (1.0 + mod_ref[0, 2:3, :]) * y
    xm = _layer_norm(r, g_ref[0:1, :], b_ref[0:1, :])
    hf = (xm * (1.0 + mod_ref[0, 4:5, :]) + mod_ref[0, 3:4, :]).astype(BF16)
    for c in range(D_FF // FF_CHUNK):
        cs = slice(c * FF_CHUNK, (c + 1) * FF_CHUNK)
        gm_ref[:, cs] = (_silu(_dot(hf, w1_ref[:, cs])) * _dot(hf, w3_ref[:, cs])).astype(BF16)
    y2 = _dot(gm_ref[...], w2_ref[...])
    r2 = RES_ALPHA * xm + (1.0 + mod_ref[0, 5:6, :]) * y2
    o_ref[0] = _layer_norm(r2, g_ref[1:2, :], b_ref[1:2, :])


def _post(acts, w_outs, x, mod, w1, w3, w2, ln_g, ln_b, tm):
    bsz, s, d = x.shape
    n_act = len(acts)
    const = lambda shape: pl.BlockSpec(shape, lambda b, i: (0,) * len(shape), pipeline_mode=pl.Buffered(1))
    in_specs = [pl.BlockSpec((1, tm, a.shape[-1]), lambda b, i: (b, i, 0)) for a in acts]
    in_specs += [const(w.shape) for w in w_outs]
    in_specs += [
        pl.BlockSpec((1, tm, d), lambda b, i: (b, i, 0)),
        pl.BlockSpec((1, 6, d), lambda b, i: (b, 0, 0)),
        const((d, D_FF)),
        const((d, D_FF)),
        const((D_FF, d)),
        const((2, d)),
        const((2, d)),
    ]
    return pl.pallas_call(
        functools.partial(_post_kernel, n_act=n_act),
        out_shape=jax.ShapeDtypeStruct((bsz, s, d), F32),
        grid=(bsz, s // tm),
        in_specs=in_specs,
        out_specs=pl.BlockSpec((1, tm, d), lambda b, i: (b, i, 0)),
        scratch_shapes=[pltpu.VMEM((tm, D_FF), BF16)],
        compiler_params=_params("parallel", "arbitrary"),
        name="post",
    )(*acts, *w_outs, x, mod, w1.astype(BF16), w3.astype(BF16), w2.astype(BF16), ln_g, ln_b)


def _mixer0_kernel(p_ref, wa_ref, ba_ref, gg_ref, cw_ref, cb_ref, gb_ref, mg_ref, o_ref,
                   st_ref, cst_ref, m_ref, xc_ref):
    L = MIX_TILE
    i = pl.program_id(1)

    @pl.when(i == 0)
    def _():
        st_ref[...] = jnp.zeros_like(st_ref)
        cst_ref[...] = jnp.zeros_like(cst_ref)
        m_ref[...] = jnp.zeros_like(m_ref)
        xc_ref[0:8, :] = jnp.zeros((8, 2 * ML_W), F32)

    lane = _iota((1, LANES), 1)
    g_raw = p_ref[0, :, P0_G:P0_G + LANES]

    u = _dot(g_raw.astype(BF16), wa_ref[...]) + ba_ref[...]
    la = _log_sigmoid(u) * (1.0 / GLA_TAU)
    row = _iota((L, L), 0)
    col = _iota((L, L), 1)
    same_chunk = (row >> _CHUNK_SHIFT) == (col >> _CHUNK_SHIFT)
    tri_blk = jnp.where(same_chunk & (col <= row), 1.0, 0.0).astype(BF16)
    bcs = _dot_split(tri_blk, la, 3)
    q_in = p_ref[0, :, P0_QA:P0_QA + GLA_QK] * (GLA_DK ** -0.5) * jnp.exp(bcs)
    k_raw = p_ref[0, :, P0_KA:P0_KA + GLA_QK]
    k_in = k_raw * jnp.exp(-bcs)
    lane_qk = _iota((1, GLA_QK), 1)
    head_masks = [(lane_qk >> _CHUNK_SHIFT) == h for h in range(GLA_HEADS)]
    r4 = _iota((GLA_HEADS * GLA_CHUNK, GLA_CHUNK), 0)
    c4 = _iota((GLA_HEADS * GLA_CHUNK, GLA_CHUNK), 1)
    tril4 = c4 <= (r4 & (GLA_CHUNK - 1))
    oa_chunks = []
    for c in range(L // GLA_CHUNK):
        r0, r1 = c * GLA_CHUNK, (c + 1) * GLA_CHUNK
        b_c = bcs[r0:r1]
        bl = b_c[GLA_CHUNK - 1:GLA_CHUNK, :]
        q_c = q_in[r0:r1]
        k_c = k_in[r0:r1].astype(BF16)
        k_end = k_raw[r0:r1] * jnp.exp(bl - b_c)
        v_c = p_ref[0, r0:r1, P0_VA:P0_VA + GLA_V]
        q_exp = jnp.concatenate([jnp.where(hm, q_c, 0.0) for hm in head_masks], axis=0).astype(BF16)
        k_exp = jnp.concatenate([jnp.where(hm, k_end, 0.0) for hm in head_masks], axis=0).astype(BF16)
        sc = jnp.where(tril4, _dot_nt(q_exp, k_c), 0.0)
        intra = _dot(sc.astype(BF16), v_c.astype(BF16))
        inter = _dot_nt(q_exp, st_ref[...].astype(BF16))
        o_heads = []
        for h in range(GLA_HEADS):
            h0, h1 = h * GLA_CHUNK, (h + 1) * GLA_CHUNK
            o_heads.append(intra[h0:h1, h * GLA_DV:(h + 1) * GLA_DV] + inter[h0:h1])
        oa_chunks.append(o_heads)
        v_cat = jnp.concatenate([v_c[:, h * GLA_DV:(h + 1) * GLA_DV] for h in range(GLA_HEADS)], axis=0)
        st_ref[...] = st_ref[...] * jnp.exp(bl) + _dot_tn(v_cat.astype(BF16), k_exp)
    for h in range(GLA_HEADS):
        o_h = jnp.concatenate([oc[h] for oc in oa_chunks], axis=0)
        cs = slice(h * GLA_DV, (h + 1) * GLA_DV)
        ra = p_ref[0, :, P0_RA + h * GLA_DV:P0_RA + (h + 1) * GLA_DV]
        o_ref[0, :, cs] = (_head_norm(o_h, gg_ref[:, cs]) * _silu(ra)).astype(o_ref.dtype)

    xc_ref[8:8 + L, :] = p_ref[0, :, P0_QKB:P0_QKB + 2 * ML_W]
    conv = cb_ref[...]
    for kk in range(MLSTM_CONV):
        conv = conv + cw_ref[kk:kk + 1, :] * xc_ref[8 - (MLSTM_CONV - 1) + kk:8 - (MLSTM_CONV - 1) + kk + L, :]
    xc_ref[0:8, :] = xc_ref[L:L + 8, :]
    qk_b = _silu(conv)

    gb = jnp.where(lane < G_A, g_raw + gb_ref[...], 0.0)
    ipre = gb
    logf = _log_sigmoid(pltpu.roll(gb, LANES - G_F, axis=1))
    logf = jnp.where(lane < MLSTM_HEADS, logf, 0.0)
    tri = jnp.where(col <= row, 1.0, 0.0).astype(BF16)
    bcum = _dot_split(tri, logf, 3)
    blast = bcum[L - 1:L, :]
    wend = blast - bcum + ipre
    m_prev = m_ref[...]
    m_new = jnp.maximum(blast + m_prev, jnp.max(wend, axis=0, keepdims=True))
    scl = jnp.exp(blast + m_prev - m_new)
    wj = jnp.exp(wend - m_new)
    rows_src = jnp.where(lane < MLSTM_HEADS, ipre, pltpu.roll(bcum, MLSTM_HEADS, axis=1))
    rows_t = rows_src.T
    causal = col <= row
    ones = jnp.ones((L, MLSTM_DH), F32)
    for h in range(MLSTM_HEADS):
        hs = slice(h * MLSTM_DH, (h + 1) * MLSTM_DH)
        b_col = bcum[:, h:h + 1]
        ip_row = rows_t[h:h + 1, :]
        b_row = rows_t[MLSTM_HEADS + h:MLSTM_HEADS + h + 1, :]
        dlog = jnp.where(causal, b_col - b_row + ip_row, NEG_INF)
        il = b_col + m_prev[:, h:h + 1]
        m_i = jnp.maximum(il, jnp.max(dlog, axis=1, keepdims=True))
        w_intra = jnp.exp(dlog - m_i)
        s_inter = jnp.exp(il - m_i)
        q_h = qk_b[:, h * MLSTM_DH:(h + 1) * MLSTM_DH].astype(BF16)
        k_h = (qk_b[:, ML_W + h * MLSTM_DH:ML_W + (h + 1) * MLSTM_DH] * (MLSTM_DH ** -0.5)).astype(BF16)
        v_h = p_ref[0, :, P0_VB + h * MLSTM_DH:P0_VB + (h + 1) * MLSTM_DH]
        v_aug = jnp.concatenate([v_h, ones], axis=1)
        a = (w_intra * _dot_nt(q_h, k_h)).astype(BF16)
        c_prev = cst_ref[h]
        tot = _dot(a, v_aug.astype(BF16)) + s_inter * _dot_nt(q_h, c_prev.astype(BF16))
        num = tot[:, :MLSTM_DH]
        den = tot[:, MLSTM_DH:]
        hid = num / jnp.maximum(jnp.abs(den), jnp.exp(-m_i))
        w_aug = (v_aug * wj[:, h:h + 1]).astype(BF16)
        cst_ref[h] = scl[:, h:h + 1] * c_prev + _dot_tn(w_aug, k_h)
        ob = p_ref[0, :, P0_OB + h * MLSTM_DH:P0_OB + (h + 1) * MLSTM_DH]
        o_ref[0, :, GLA_V + h * MLSTM_DH:GLA_V + (h + 1) * MLSTM_DH] = (
            jax.nn.sigmoid(ob) * _head_norm(hid, mg_ref[:, hs])).astype(o_ref.dtype)
    m_ref[...] = m_new


def _mixer0(proj, wa_up, ba, gla_g, conv_w, conv_b, b_i, b_f, ml_g):
    bsz, s, n = proj.shape
    L = MIX_TILE
    wa_pad = jnp.zeros((LANES, GLA_QK), F32).at[G_A:G_A + GLA_GATE_RANK].set(wa_up).astype(BF16)
    gbias = jnp.zeros((1, LANES), F32).at[0, G_I:G_I + MLSTM_HEADS].set(b_i).at[0, G_F:G_F + MLSTM_HEADS].set(b_f)
    full = lambda shape: pl.BlockSpec(shape, lambda b, i: (0,) * len(shape))
    return pl.pallas_call(
        _mixer0_kernel,
        out_shape=jax.ShapeDtypeStruct((bsz, s, GLA_V + ML_W), BF16),
        grid=(bsz, s // L),
        in_specs=[
            pl.BlockSpec((1, L, n), lambda b, i: (b, i, 0)),
            full((LANES, GLA_QK)),
            full((1, GLA_QK)),
            full((1, GLA_V)),
            full((MLSTM_CONV, 2 * ML_W)),
            full((1, 2 * ML_W)),
            full((1, LANES)),
            full((1, ML_W)),
        ],
        out_specs=pl.BlockSpec((1, L, GLA_V + ML_W), lambda b, i: (b, i, 0)),
        scratch_shapes=[
            pltpu.VMEM((GLA_DV, GLA_QK), F32),
            pltpu.VMEM((MLSTM_HEADS, 2 * MLSTM_DH, MLSTM_DH), F32),
            pltpu.VMEM((1, LANES), F32),
            pltpu.VMEM((8 + L, 2 * ML_W), F32),
        ],
        compiler_params=_params("parallel", "arbitrary"),
        name="mixer0",
    )(proj, wa_pad, ba.reshape(1, GLA_QK), gla_g.reshape(1, GLA_V), conv_w, conv_b.reshape(1, 2 * ML_W),
      gbias, ml_g.reshape(1, ML_W))


def _sb_kernel(q_ref, k_ref, v_ref, o_ref, qs_ref, acc_ref, car_ref, mar_ref, kn_ref):
    TQ, SUB, WIDE = SB_TQ, SB_SUB, SB_WIDE
    ns = TQ // SUB
    RB = 2 * SUB
    i = pl.program_id(2)
    lane = _iota((1, LANES), 1)
    head0 = (lane >> _HEAD_SHIFT) == 0

    @pl.when(i == 0)
    def _():
        k2 = jnp.square(k_ref[0].astype(F32))
        for h in range(2):
            n2 = jnp.sum(jnp.where(head0 if h == 0 else ~head0, k2, 0.0), axis=1, keepdims=True)
            kn_ref[h:h + 1, :] = jnp.broadcast_to(jnp.sqrt(jnp.max(n2, axis=0, keepdims=True)), (1, LANES))

    for a in range(ns):
        qa = q_ref[0, a * SUB:(a + 1) * SUB, :].astype(F32) * (SB_DH ** -0.5 * LOG2E)
        for h in range(2):
            rows = slice(a * RB + h * SUB, a * RB + (h + 1) * SUB)
            qh = jnp.where(head0 if h == 0 else ~head0, qa, 0.0).astype(BF16)
            qs_ref[rows, :] = qh
            qn = jnp.sqrt(jnp.sum(jnp.square(qh.astype(F32)), axis=1, keepdims=True))
            mar_ref[rows, :] = qn * kn_ref[h:h + 1, :] * SB_BOUND_SLACK + SB_ZERO_BITS
    tri = jnp.where(_iota((WIDE, WIDE), 0) >= _iota((WIDE, WIDE), 1), -1.0, 0.0).astype(BF16)

    def key_rows(ku, nk):
        r0 = ku * SUB
        return pl.ds(r0 if isinstance(r0, int) else pl.multiple_of(r0, SUB), nk)

    def logits(a, ku, nk):
        return _dot_nt(qs_ref[a * RB:(a + 1) * RB, :], k_ref[0, key_rows(ku, nk), :])

    def softplus2(z, diagonal):
        sp = jnp.maximum(z, 0.0) + jnp.log(1.0 + jnp.exp2(-jnp.abs(z))) * LOG2E
        if not diagonal:
            return sp, None
        mask = _iota(z.shape, 1) < (_iota(z.shape, 0) & (SUB - 1))
        return jnp.where(mask, sp, 0.0), mask

    def suffix(sp):
        nk = sp.shape[1]
        return _dot(sp.astype(BF16), tri[:nk, :nk]), jnp.sum(sp, axis=1, keepdims=True)

    def weights(z, res, car, mask):
        att = jnp.exp2(z + res if car is None else z + res - car)
        if mask is not None:
            att = jnp.where(mask, att, 0.0)
        return att.astype(BF16)

    def far_tile(a, ku):
        rows = slice(a * RB, (a + 1) * RB)
        z = logits(a, ku, SUB)
        sp, _ = softplus2(z, False)
        res, tot = suffix(sp)
        car = car_ref[rows, :]
        acc_ref[rows, :] += _dot(weights(z, res, car, None), v_ref[0, key_rows(ku, SUB), :])
        car_ref[rows, :] = car + tot

    def near_tiles(first_block):
        chains = []
        for a in range(ns):
            g = a if first_block else ns * i + a
            chain = [(a, g, SUB, True)]
            if not first_block or a >= 2:
                chain.append((a, g - 2, WIDE, False))
            elif a == 1:
                chain.append((a, 0, SUB, False))
            chains.append(chain)
        jobs = [job for step in itertools.zip_longest(*chains) for job in step if job is not None]
        zs = [logits(a, ku, nk) for a, ku, nk, _ in jobs]
        sps = [softplus2(z, diagonal) for z, (_, _, _, diagonal) in zip(zs, jobs)]
        sufs = [suffix(sp) for sp, _ in sps]
        car = [None] * ns
        atts = []
        for z, (sp, mask), (res, tot), (a, _, _, _) in zip(zs, sps, sufs, jobs):
            atts.append(weights(z, res, car[a], mask))
            car[a] = tot if car[a] is None else car[a] + tot
        acc = [None] * ns
        for att, (a, ku, nk, _) in zip(atts, jobs):
            t = _dot(att, v_ref[0, key_rows(ku, nk), :])
            acc[a] = t if acc[a] is None else acc[a] + t
        for a in range(ns):
            acc_ref[a * RB:(a + 1) * RB, :] = acc[a]
            car_ref[a * RB:(a + 1) * RB, :] = jnp.broadcast_to(car[a], (RB, LANES))

    pl.when(i == 0)(functools.partial(near_tiles, True))
    pl.when(i > 0)(functools.partial(near_tiles, False))

    def pending():
        return jnp.max(mar_ref[...] - car_ref[...]) > 0.0

    def far_body(c):
        t, _ = c
        for a in range(ns):
            ku = ns * i + a - 3 - t
            pl.when(ku >= 0)(functools.partial(far_tile, a, ku))
        return t + 1, pending()

    lax.while_loop(lambda c: jnp.logical_and(ns * i + ns - 4 - c[0] >= 0, c[1]), far_body, (0, pending()))

    for a in range(ns):
        o_ref[0, a * SUB:(a + 1) * SUB, :] = jnp.where(
            head0, acc_ref[a * RB:a * RB + SUB, :], acc_ref[a * RB + SUB:(a + 1) * RB, :]).astype(o_ref.dtype)


def _stick_breaking(pc):
    bsz, s, _ = pc.shape
    npair = SB_W // LANES
    return pl.pallas_call(
        _sb_kernel,
        out_shape=jax.ShapeDtypeStruct((bsz, s, SB_W), BF16),
        grid=(bsz, npair, s // SB_TQ),
        in_specs=[
            pl.BlockSpec((1, SB_TQ, LANES), lambda b, p, i: (b, i, p)),
            pl.BlockSpec((1, s, LANES), lambda b, p, i: (b, 0, npair + p)),
            pl.BlockSpec((1, s, LANES), lambda b, p, i: (b, 0, 2 * npair + p)),
        ],
        out_specs=pl.BlockSpec((1, SB_TQ, LANES), lambda b, p, i: (b, i, p)),
        scratch_shapes=[
            pltpu.VMEM((2 * SB_TQ, LANES), BF16),
            pltpu.VMEM((2 * SB_TQ, LANES), F32),
            pltpu.VMEM((2 * SB_TQ, LANES), F32),
            pltpu.VMEM((2 * SB_TQ, LANES), F32),
            pltpu.VMEM((8, LANES), F32),
        ],
        compiler_params=_params("parallel", "parallel", "arbitrary"),
        name="stick_breaking",
    )(pc, pc, pc)


def _t5_bucket(dist):
    max_exact = N_REL_BUCKETS // 2
    dd = np.maximum(dist, 1).astype(np.float64)
    large = max_exact + (np.log(dd / max_exact) / np.log(REL_MAX_DIST / max_exact)
                         * (N_REL_BUCKETS - max_exact)).astype(np.int32)
    large = np.minimum(large, N_REL_BUCKETS - 1)
    return np.where(dist < max_exact, dist, large).astype(np.int32)


def _dil_buckets():
    qi = np.arange(DIL_BLK)[:, None]
    kj = np.arange(2 * DIL_BLK)[None, :]
    delta = qi - kj + DIL_BLK
    in_win = (delta >= 0) & (delta <= DIL_BLK)
    tabs = []
    for window, dil in DIL_PAIRS:
        assert window // dil == DIL_BLK
        bucket = _t5_bucket(np.clip(delta, 0, None) * dil)
        tabs.append(np.where(in_win, bucket, -1).astype(np.int32))
    return np.stack(tabs, 0)


def _dil_kernel(q_ref, k_ref, v_ref, bkt_ref, tab_ref, o_ref, m_sc, l_sc, acc_sc):
    s_len = q_ref.shape[1]
    hp = pl.program_id(1)
    g = pl.program_id(2)
    blk = DIL_BLK
    lane = _iota((1, LANES), 1)
    lane_head = lane >> _HEAD_SHIFT

    @pl.when(g == 0)
    def _():
        m_sc[...] = jnp.full_like(m_sc, NEG_INF)
        l_sc[...] = jnp.zeros_like(l_sc)
        acc_sc[...] = jnp.zeros_like(acc_sc)

    kcol = _iota((2 * blk, 2 * blk), 1)

    def group(gi, dil):
        nb = s_len // (blk * dil)
        bkt = bkt_ref[gi]
        biases = []
        for hl in range(2):
            head = gi * DIL_HEADS_PER_GROUP + hp * 2 + hl
            bias = jnp.full((blk, 2 * blk), NEG_INF, F32)
            for bk in range(N_REL_BUCKETS):
                bias = jnp.where(bkt == bk, tab_ref[bk, head], bias)
            biases.append(bias)
        bias2 = jnp.concatenate(biases, axis=0)
        sel0 = lane_head == 0

        def stage_logits(t):
            r = t // nb
            n = t % nb
            q_start = r + dil * blk * n
            p_start = r + dil * blk * jnp.maximum(n - 1, 0)
            rows_q = pl.ds(q_start, blk, stride=dil) if dil > 1 else pl.ds(q_start, blk)
            rows_p = pl.ds(p_start, blk, stride=dil) if dil > 1 else pl.ds(p_start, blk)
            qb = q_ref[0, rows_q, :] * (DIL_DH ** -0.5)
            q2 = jnp.concatenate([jnp.where(sel0, qb, 0.0), jnp.where(sel0, 0.0, qb)], axis=0).astype(BF16)
            kk = jnp.concatenate([k_ref[0, rows_p, :], k_ref[0, rows_q, :]], axis=0).astype(BF16)
            logits = _dot_nt(q2, kk) + bias2
            logits = jnp.where(jnp.logical_and(n == 0, kcol < blk), NEG_INF, logits)
            return rows_q, rows_p, logits

        def stage_max(rows_q, logits):
            m_old = m_sc[rows_q, :]
            m_old_col = jnp.concatenate([m_old[:, 0:1], m_old[:, DIL_DH:DIL_DH + 1]], axis=0)
            m_col = jnp.maximum(m_old_col, jnp.max(logits, axis=1, keepdims=True))
            return m_old, m_col

        def stage_pv(rows_q, rows_p, logits, m_col):
            p = jnp.exp(logits - m_col)
            vv = jnp.concatenate([v_ref[0, rows_p, :], v_ref[0, rows_q, :]], axis=0)
            vv = jnp.concatenate([vv, jnp.ones_like(vv)], axis=1).astype(BF16)
            pv = _dot(p.astype(BF16), vv)
            return pv[:, LANES:], pv[:, :LANES]

        def body(tt, carry):
            ts = [tt * DIL_UNROLL + u for u in range(DIL_UNROLL)]
            s1 = [stage_logits(t) for t in ts]
            s2 = [stage_max(rq, lg) for rq, _, lg in s1]
            s3 = [stage_pv(rq, rp, lg, mc) for (rq, rp, lg), (_, mc) in zip(s1, s2)]
            outs = []
            for (rows_q, _, _), (m_old, m_col), (ps, pv) in zip(s1, s2, s3):
                m_new = jnp.where(sel0, m_col[0:blk], m_col[blk:])
                alpha = jnp.exp(m_old - m_new)
                l_new = alpha * l_sc[rows_q, :] + jnp.where(sel0, ps[0:blk], ps[blk:])
                a_new = alpha * acc_sc[rows_q, :] + jnp.where(sel0, pv[0:blk], pv[blk:])
                outs.append((rows_q, m_new, l_new, a_new))
            for rows_q, m_new, l_new, a_new in outs:
                m_sc[rows_q, :] = m_new
                l_sc[rows_q, :] = l_new
                acc_sc[rows_q, :] = a_new
            return carry

        lax.fori_loop(0, s_len // (blk * DIL_UNROLL), body, 0)

    for gi, (_, dil) in enumerate(DIL_PAIRS):
        pl.when(g == gi)(functools.partial(group, gi, dil))

    @pl.when(g == len(DIL_PAIRS) - 1)
    def _():
        o_ref[0] = (acc_sc[...] / l_sc[...]).astype(o_ref.dtype)


def _dilated(pd, rel_bias):
    bsz, s, _ = pd.shape
    ng = len(DIL_PAIRS)
    gw = DIL_HEADS_PER_GROUP * DIL_DH // LANES
    nq = DIL_W // LANES
    bkt = jnp.asarray(_dil_buckets())
    blk_spec = lambda base: pl.BlockSpec((1, s, LANES), lambda b, p, g: (b, 0, base + g * gw + p))
    return pl.pallas_call(
        _dil_kernel,
        out_shape=jax.ShapeDtypeStruct((bsz, s, DIL_OUT), BF16),
        grid=(bsz, gw, ng),
        in_specs=[
            blk_spec(0),
            blk_spec(nq),
            blk_spec(2 * nq),
            pl.BlockSpec((ng, DIL_BLK, 2 * DIL_BLK), lambda b, p, g: (0, 0, 0)),
            pl.BlockSpec(memory_space=pltpu.SMEM),
        ],
        out_specs=pl.BlockSpec((1, s, LANES), lambda b, p, g: (b, 0, p)),
        scratch_shapes=[pltpu.VMEM((s, LANES), F32)] * 3,
        compiler_params=_params("parallel", "parallel", "arbitrary"),
        name="dilated",
    )(pd, pd, pd, bkt, rel_bias)


def _ab_weight(w):
    o = np.cumsum((0,) + (GLA_QK, GLA_QK, GLA_V, GLA_V, GLA_GATE_RANK, 2 * ML_W, ML_W, MLSTM_HEADS, MLSTM_HEADS, ML_W))
    qa, ka, va, ra, aa, qkb, vb, ib, fb, ob = [w[:, o[j]:o[j + 1]] for j in range(10)]
    pad = jnp.zeros((w.shape[0], LANES - 2 * MLSTM_HEADS - GLA_GATE_RANK), w.dtype)
    return jnp.concatenate([qa, ka, va, ra, qkb, vb, ob, ib, fb, aa, pad], axis=1).astype(BF16)


def kernel(x, c, ada_w, ada_b, ln_g, ln_b, ab_w_in, gla_wa_up, gla_ba, gla_norm_g, ml_conv_w, ml_conv_b,
           ml_b_i, ml_b_f, ml_norm_g, ab_w_out, cd_w_in, rel_bias, cd_w_out, ffn_w1, ffn_w3, ffn_w2):
    bsz, s, d = x.shape
    mod_all = _ada_mod(c, ada_w, ada_b).reshape(DEPTH, bsz, 6, d)
    tm = 512
    for layer in range(DEPTH):
        mod = mod_all[layer]
        j = layer // 2
        if layer % 2 == 0:
            (proj,) = _inproj(x, mod, _ab_weight(ab_w_in[j]), [(P0_N, F32)], tm, 0, 1, "inproj0")
            cat = _mixer0(proj, gla_wa_up[j], gla_ba[j], gla_norm_g[j], ml_conv_w[j], ml_conv_b[j],
                          ml_b_i[j], ml_b_f[j], ml_norm_g[j])
            acts, w_outs = [cat], [ab_w_out[j].astype(BF16)]
        else:
            pc, pd = _inproj(x, mod, cd_w_in[j].astype(BF16), [(3 * SB_W, BF16), (3 * DIL_W, F32)],
                             tm, 0, 1, "inproj1")
            oc = _stick_breaking(pc)
            od = _dilated(pd, rel_bias)
            w_out = cd_w_out[j].astype(BF16)
            acts, w_outs = [oc, od], [w_out[:SB_W], w_out[SB_W:]]
        x = _post(acts, w_outs, x, mod, ffn_w1[layer], ffn_w3[layer], ffn_w2[layer], ln_g[layer], ln_b[layer], FFN_TM)
    return x
```

```python
import functools
import itertools

import numpy as np
import jax
import jax.numpy as jnp
from jax import lax
from jax.experimental import pallas as pl
from jax.experimental.pallas import tpu as pltpu

F32 = jnp.float32
BF16 = jnp.bfloat16

D_MODEL = 1024
DEPTH = 2
GLA_HEADS = 4
GLA_DK = 64
GLA_DV = 128
GLA_GATE_RANK = 16
GLA_TAU = 16.0
GLA_CHUNK = 64
MLSTM_HEADS = 4
MLSTM_DH = 128
MLSTM_CONV = 4
SB_HEADS = 8
SB_DH = 64
DIL_PAIRS = ((128, 1), (512, 4), (2048, 16))
DIL_HEADS_PER_GROUP = 4
DIL_DH = 64
DIL_BLK = 128
DIL_UNROLL = 4
N_REL_BUCKETS = 32
REL_MAX_DIST = 2048
D_FF = ((8 * D_MODEL + 3 * 256 - 1) // (3 * 256)) * 256
LN_EPS = 1e-5
RES_ALPHA = (2 * DEPTH) ** 0.25
NEG_INF = -1e30

GLA_QK = GLA_HEADS * GLA_DK
GLA_V = GLA_HEADS * GLA_DV
ML_W = MLSTM_HEADS * MLSTM_DH
SB_W = SB_HEADS * SB_DH
DIL_W = len(DIL_PAIRS) * DIL_HEADS_PER_GROUP * DIL_DH
DIL_OUT = DIL_HEADS_PER_GROUP * DIL_DH

LANES = 128
VMEM_LIMIT = 56 * 1024 * 1024

P0_QA = 0
P0_KA = P0_QA + GLA_QK
P0_VA = P0_KA + GLA_QK
P0_RA = P0_VA + GLA_V
P0_QKB = P0_RA + GLA_V
P0_VB = P0_QKB + 2 * ML_W
P0_OB = P0_VB + ML_W
P0_G = P0_OB + ML_W
P0_N = P0_G + LANES
G_I = 0
G_F = MLSTM_HEADS
G_A = 2 * MLSTM_HEADS

MIX_TILE = 256
FF_CHUNK = 256
FFN_TM = 512
SB_TQ = 1024
SB_SUB = 128
SB_WIDE = 256
SB_ZERO_BITS = 160.0
SB_BOUND_SLACK = 1.01
LOG2E = 1.4426950408889634
_CHUNK_SHIFT = 6
_HEAD_SHIFT = 6
assert GLA_CHUNK == GLA_DK == 1 << _CHUNK_SHIFT and SB_DH == DIL_DH == 1 << _HEAD_SHIFT


def _dot(a, b):
    return jnp.dot(a, b, preferred_element_type=F32)


def _dot_nt(a, b):
    return lax.dot_general(a, b, (((1,), (1,)), ((), ())), preferred_element_type=F32)


def _dot_tn(a, b):
    return lax.dot_general(a, b, (((0,), (0,)), ((), ())), preferred_element_type=F32)


def _dot_split(t, x, terms):
    acc = None
    rem = x
    for i in range(terms):
        part = rem.astype(BF16)
        d = _dot(t, part)
        acc = d if acc is None else acc + d
        if i + 1 < terms:
            rem = rem - part.astype(F32)
    return acc


def _log_sigmoid(x):
    return jnp.minimum(x, 0.0) - jnp.log(1.0 + jnp.exp(-jnp.abs(x)))


def _silu(x):
    return x * jax.nn.sigmoid(x)


def _layer_norm(r, g, b):
    mu = jnp.mean(r, axis=-1, keepdims=True)
    d = r - mu
    var = jnp.mean(d * d, axis=-1, keepdims=True)
    return d * lax.rsqrt(var + LN_EPS) * g + b


def _head_norm(x, g):
    mu = jnp.mean(x, axis=-1, keepdims=True)
    d = x - mu
    var = jnp.mean(d * d, axis=-1, keepdims=True)
    return d * lax.rsqrt(var + LN_EPS) * g


def _iota(shape, dim):
    return lax.broadcasted_iota(jnp.int32, shape, dim)


def _params(*sem):
    return pltpu.CompilerParams(dimension_semantics=sem, vmem_limit_bytes=VMEM_LIMIT)


def _ada_kernel(c_ref, w_ref, b_ref, o_ref):
    ca = _silu(c_ref[...]).astype(BF16)
    o_ref[0] = _dot(ca, w_ref[0].astype(BF16)) + b_ref[0]


def _ada_mod(c, ada_w, ada_b):
    bsz, d = c.shape
    n = ada_w.shape[-1]
    tn = n // 4
    return pl.pallas_call(
        _ada_kernel,
        out_shape=jax.ShapeDtypeStruct((DEPTH, bsz, n), F32),
        grid=(DEPTH, n // tn),
        in_specs=[
            pl.BlockSpec((bsz, d), lambda l, j: (0, 0)),
            pl.BlockSpec((1, d, tn), lambda l, j: (l, 0, j)),
            pl.BlockSpec((1, 1, tn), lambda l, j: (l, 0, j)),
        ],
        out_specs=pl.BlockSpec((1, bsz, tn), lambda l, j: (l, 0, j)),
        compiler_params=_params("arbitrary", "arbitrary"),
        name="ada_mod",
    )(c, ada_w, ada_b.reshape(DEPTH, 1, n))


def _inproj_kernel(x_ref, mod_ref, w_ref, *o_refs, shift_row, scale_row):
    sh = mod_ref[0, shift_row:shift_row + 1, :]
    sc = mod_ref[0, scale_row:scale_row + 1, :]
    hm = (x_ref[0] * (1.0 + sc) + sh).astype(BF16)
    col = 0
    for o_ref in o_refs:
        n = o_ref.shape[-1]
        o_ref[0] = _dot(hm, w_ref[:, col:col + n]).astype(o_ref.dtype)
        col += n


def _inproj(x, mod, w, outs, tm, shift_row, scale_row, name):
    bsz, s, d = x.shape
    n = w.shape[1]
    assert sum(o[0] for o in outs) == n
    return pl.pallas_call(
        functools.partial(_inproj_kernel, shift_row=shift_row, scale_row=scale_row),
        out_shape=[jax.ShapeDtypeStruct((bsz, s, o[0]), o[1]) for o in outs],
        grid=(bsz, s // tm),
        in_specs=[
            pl.BlockSpec((1, tm, d), lambda b, i: (b, i, 0)),
            pl.BlockSpec((1, 6, d), lambda b, i: (b, 0, 0)),
            pl.BlockSpec((d, n), lambda b, i: (0, 0)),
        ],
        out_specs=[pl.BlockSpec((1, tm, o[0]), lambda b, i: (b, i, 0)) for o in outs],
        compiler_params=_params("parallel", "arbitrary"),
        name=name,
    )(x, mod, w)


def _post_kernel(*refs, n_act):
    act_refs = refs[:n_act]
    wo_refs = refs[n_act:2 * n_act]
    x_ref, mod_ref, w1_ref, w3_ref, w2_ref, g_ref, b_ref, o_ref, gm_ref = refs[2 * n_act:]
    y = None
    for a_ref, w_ref in zip(act_refs, wo_refs):
        t = _dot(a_ref[0], w_ref[...])
        y = t if y is None else y + t
    r = RES_ALPHA * x_ref[0] + (1.0 + mod_ref[0, 2:3, :]) * y
    xm = _layer_norm(r, g_ref[0:1, :], b_ref[0:1, :])
    hf = (xm * (1.0 + mod_ref[0, 4:5, :]) + mod_ref[0, 3:4, :]).astype(BF16)
    for c in range(D_FF // FF_CHUNK):
        cs = slice(c * FF_CHUNK, (c + 1) * FF_CHUNK)
        gm_ref[:, cs] = (_silu(_dot(hf, w1_ref[:, cs])) * _dot(hf, w3_ref[:, cs])).astype(BF16)
    y2 = _dot(gm_ref[...], w2_ref[...])
    r2 = RES_ALPHA * xm + (1.0 + mod_ref[0, 5:6, :]) * y2
    o_ref[0] = _layer_norm(r2, g_ref[1:2, :], b_ref[1:2, :])


def _post(acts, w_outs, x, mod, w1, w3, w2, ln_g, ln_b, tm):
    bsz, s, d = x.shape
    n_act = len(acts)
    const = lambda shape: pl.BlockSpec(shape, lambda b, i: (0,) * len(shape), pipeline_mode=pl.Buffered(1))
    in_specs = [pl.BlockSpec((1, tm, a.shape[-1]), lambda b, i: (b, i, 0)) for a in acts]
    in_specs += [const(w.shape) for w in w_outs]
    in_specs += [
        pl.BlockSpec((1, tm, d), lambda b, i: (b, i, 0)),
        pl.BlockSpec((1, 6, d), lambda b, i: (b, 0, 0)),
        const((d, D_FF)),
        const((d, D_FF)),
        const((D_FF, d)),
        const((2, d)),
        const((2, d)),
    ]
    return pl.pallas_call(
        functools.partial(_post_kernel, n_act=n_act),
        out_shape=jax.ShapeDtypeStruct((bsz, s, d), F32),
        grid=(bsz, s // tm),
        in_specs=in_specs,
        out_specs=pl.BlockSpec((1, tm, d), lambda b, i: (b, i, 0)),
        scratch_shapes=[pltpu.VMEM((tm, D_FF), BF16)],
        compiler_params=_params("parallel", "arbitrary"),
        name="post",
    )(*acts, *w_outs, x, mod, w1.astype(BF16), w3.astype(BF16), w2.astype(BF16), ln_g, ln_b)


def _mixer0_kernel(p_ref, wa_ref, ba_ref, gg_ref, cw_ref, cb_ref, gb_ref, mg_ref, o_ref,
                   st_ref, cst_ref, m_ref, xc_ref):
    L = MIX_TILE
    i = pl.program_id(1)

    @pl.when(i == 0)
    def _():
        st_ref[...] = jnp.zeros_like(st_ref)
        cst_ref[...] = jnp.zeros_like(cst_ref)
        m_ref[...] = jnp.zeros_like(m_ref)
        xc_ref[0:8, :] = jnp.zeros((8, 2 * ML_W), F32)

    lane = _iota((1, LANES), 1)
    g_raw = p_ref[0, :, P0_G:P0_G + LANES]

    u = _dot(g_raw.astype(BF16), wa_ref[...]) + ba_ref[...]
    la = _log_sigmoid(u) * (1.0 / GLA_TAU)
    row = _iota((L, L), 0)
    col = _iota((L, L), 1)
    same_chunk = (row >> _CHUNK_SHIFT) == (col >> _CHUNK_SHIFT)
    tri_blk = jnp.where(same_chunk & (col <= row), 1.0, 0.0).astype(BF16)
    bcs = _dot_split(tri_blk, la, 3)
    q_in = p_ref[0, :, P0_QA:P0_QA + GLA_QK] * (GLA_DK ** -0.5) * jnp.exp(bcs)
    k_raw = p_ref[0, :, P0_KA:P0_KA + GLA_QK]
    k_in = k_raw * jnp.exp(-bcs)
    lane_qk = _iota((1, GLA_QK), 1)
    head_masks = [(lane_qk >> _CHUNK_SHIFT) == h for h in range(GLA_HEADS)]
    r4 = _iota((GLA_HEADS * GLA_CHUNK, GLA_CHUNK), 0)
    c4 = _iota((GLA_HEADS * GLA_CHUNK, GLA_CHUNK), 1)
    tril4 = c4 <= (r4 & (GLA_CHUNK - 1))
    oa_chunks = []
    for c in range(L // GLA_CHUNK):
        r0, r1 = c * GLA_CHUNK, (c + 1) * GLA_CHUNK
        b_c = bcs[r0:r1]
        bl = b_c[GLA_CHUNK - 1:GLA_CHUNK, :]
        q_c = q_in[r0:r1]
        k_c = k_in[r0:r1].astype(BF16)
        k_end = k_raw[r0:r1] * jnp.exp(bl - b_c)
        v_c = p_ref[0, r0:r1, P0_VA:P0_VA + GLA_V]
        q_exp = jnp.concatenate([jnp.where(hm, q_c, 0.0) for hm in head_masks], axis=0).astype(BF16)
        k_exp = jnp.concatenate([jnp.where(hm, k_end, 0.0) for hm in head_masks], axis=0).astype(BF16)
        sc = jnp.where(tril4, _dot_nt(q_exp, k_c), 0.0)
        intra = _dot(sc.astype(BF16), v_c.astype(BF16))
        inter = _dot_nt(q_exp, st_ref[...].astype(BF16))
        o_heads = []
        for h in range(GLA_HEADS):
            h0, h1 = h * GLA_CHUNK, (h + 1) * GLA_CHUNK
            o_heads.append(intra[h0:h1, h * GLA_DV:(h + 1) * GLA_DV] + inter[h0:h1])
        oa_chunks.append(o_heads)
        v_cat = jnp.concatenate([v_c[:, h * GLA_DV:(h + 1) * GLA_DV] for h in range(GLA_HEADS)], axis=0)
        st_ref[...] = st_ref[...] * jnp.exp(bl) + _dot_tn(v_cat.astype(BF16), k_exp)
    for h in range(GLA_HEADS):
        o_h = jnp.concatenate([oc[h] for oc in oa_chunks], axis=0)
        cs = slice(h * GLA_DV, (h + 1) * GLA_DV)
        ra = p_ref[0, :, P0_RA + h * GLA_DV:P0_RA + (h + 1) * GLA_DV]
        o_ref[0, :, cs] = (_head_norm(o_h, gg_ref[:, cs]) * _silu(ra)).astype(o_ref.dtype)

    xc_ref[8:8 + L, :] = p_ref[0, :, P0_QKB:P0_QKB + 2 * ML_W]
    conv = cb_ref[...]
    for kk in range(MLSTM_CONV):
        conv = conv + cw_ref[kk:kk + 1, :] * xc_ref[8 - (MLSTM_CONV - 1) + kk:8 - (MLSTM_CONV - 1) + kk + L, :]
    xc_ref[0:8, :] = xc_ref[L:L + 8, :]
    qk_b = _silu(conv)

    gb = jnp.where(lane < G_A, g_raw + gb_ref[...], 0.0)
    ipre = gb
    logf = _log_sigmoid(pltpu.roll(gb, LANES - G_F, axis=1))
    logf = jnp.where(lane < MLSTM_HEADS, logf, 0.0)
    tri = jnp.where(col <= row, 1.0, 0.0).astype(BF16)
    bcum = _dot_split(tri, logf, 3)
    blast = bcum[L - 1:L, :]
    wend = blast - bcum + ipre
    m_prev = m_ref[...]
    m_new = jnp.maximum(blast + m_prev, jnp.max(wend, axis=0, keepdims=True))
    scl = jnp.exp(blast + m_prev - m_new)
    wj = jnp.exp(wend - m_new)
    rows_src = jnp.where(lane < MLSTM_HEADS, ipre, pltpu.roll(bcum, MLSTM_HEADS, axis=1))
    rows_t = rows_src.T
    causal = col <= row
    ones = jnp.ones((L, MLSTM_DH), F32)
    for h in range(MLSTM_HEADS):
        hs = slice(h * MLSTM_DH, (h + 1) * MLSTM_DH)
        b_col = bcum[:, h:h + 1]
        ip_row = rows_t[h:h + 1, :]
        b_row = rows_t[MLSTM_HEADS + h:MLSTM_HEADS + h + 1, :]
        dlog = jnp.where(causal, b_col - b_row + ip_row, NEG_INF)
        il = b_col + m_prev[:, h:h + 1]
        m_i = jnp.maximum(il, jnp.max(dlog, axis=1, keepdims=True))
        w_intra = jnp.exp(dlog - m_i)
        s_inter = jnp.exp(il - m_i)
        q_h = qk_b[:, h * MLSTM_DH:(h + 1) * MLSTM_DH].astype(BF16)
        k_h = (qk_b[:, ML_W + h * MLSTM_DH:ML_W + (h + 1) * MLSTM_DH] * (MLSTM_DH ** -0.5)).astype(BF16)
        v_h = p_ref[0, :, P0_VB + h * MLSTM_DH:P0_VB + (h + 1) * MLSTM_DH]
        v_aug = jnp.concatenate([v_h, ones], axis=1)
        a = (w_intra * _dot_nt(q_h, k_h)).astype(BF16)
        c_prev = cst_ref[h]
        tot = _dot(a, v_aug.astype(BF16)) + s_inter * _dot_nt(q_h, c_prev.astype(BF16))
        num = tot[:, :MLSTM_DH]
        den = tot[:, MLSTM_DH:]
        hid = num / jnp.maximum(jnp.abs(den), jnp.exp(-m_i))
        w_aug = (v_aug * wj[:, h:h + 1]).astype(BF16)
        cst_ref[h] = scl[:, h:h + 1] * c_prev + _dot_tn(w_aug, k_h)
        ob = p_ref[0, :, P0_OB + h * MLSTM_DH:P0_OB + (h + 1) * MLSTM_DH]
        o_ref[0, :, GLA_V + h * MLSTM_DH:GLA_V + (h + 1) * MLSTM_DH] = (
            jax.nn.sigmoid(ob) * _head_norm(hid, mg_ref[:, hs])).astype(o_ref.dtype)
    m_ref[...] = m_new


def _mixer0(proj, wa_up, ba, gla_g, conv_w, conv_b, b_i, b_f, ml_g):
    bsz, s, n = proj.shape
    L = MIX_TILE
    wa_pad = jnp.zeros((LANES, GLA_QK), F32).at[G_A:G_A + GLA_GATE_RANK].set(wa_up).astype(BF16)
    gbias = jnp.zeros((1, LANES), F32).at[0, G_I:G_I + MLSTM_HEADS].set(b_i).at[0, G_F:G_F + MLSTM_HEADS].set(b_f)
    full = lambda shape: pl.BlockSpec(shape, lambda b, i: (0,) * len(shape))
    return pl.pallas_call(
        _mixer0_kernel,
        out_shape=jax.ShapeDtypeStruct((bsz, s, GLA_V + ML_W), BF16),
        grid=(bsz, s // L),
        in_specs=[
            pl.BlockSpec((1, L, n), lambda b, i: (b, i, 0)),
            full((LANES, GLA_QK)),
            full((1, GLA_QK)),
            full((1, GLA_V)),
            full((MLSTM_CONV, 2 * ML_W)),
            full((1, 2 * ML_W)),
            full((1, LANES)),
            full((1, ML_W)),
        ],
        out_specs=pl.BlockSpec((1, L, GLA_V + ML_W), lambda b, i: (b, i, 0)),
        scratch_shapes=[
            pltpu.VMEM((GLA_DV, GLA_QK), F32),
            pltpu.VMEM((MLSTM_HEADS, 2 * MLSTM_DH, MLSTM_DH), F32),
            pltpu.VMEM((1, LANES), F32),
            pltpu.VMEM((8 + L, 2 * ML_W), F32),
        ],
        compiler_params=_params("parallel", "arbitrary"),
        name="mixer0",
    )(proj, wa_pad, ba.reshape(1, GLA_QK), gla_g.reshape(1, GLA_V), conv_w, conv_b.reshape(1, 2 * ML_W),
      gbias, ml_g.reshape(1, ML_W))


def _sb_kernel(q_ref, k_ref, v_ref, o_ref, qs_ref, acc_ref, car_ref, mar_ref, kn_ref):
    TQ, SUB, WIDE = SB_TQ, SB_SUB, SB_WIDE
    ns = TQ // SUB
    RB = 2 * SUB
    i = pl.program_id(2)
    lane = _iota((1, LANES), 1)
    head0 = (lane >> _HEAD_SHIFT) == 0

    @pl.when(i == 0)
    def _():
        k2 = jnp.square(k_ref[0].astype(F32))
        for h in range(2):
            n2 = jnp.sum(jnp.where(head0 if h == 0 else ~head0, k2, 0.0), axis=1, keepdims=True)
            kn_ref[h:h + 1, :] = jnp.broadcast_to(jnp.sqrt(jnp.max(n2, axis=0, keepdims=True)), (1, LANES))

    for a in range(ns):
        qa = q_ref[0, a * SUB:(a + 1) * SUB, :].astype(F32) * (SB_DH ** -0.5 * LOG2E)
        for h in range(2):
            rows = slice(a * RB + h * SUB, a * RB + (h + 1) * SUB)
            qh = jnp.where(head0 if h == 0 else ~head0, qa, 0.0).astype(BF16)
            qs_ref[rows, :] = qh
            qn = jnp.sqrt(jnp.sum(jnp.square(qh.astype(F32)), axis=1, keepdims=True))
            mar_ref[rows, :] = qn * kn_ref[h:h + 1, :] * SB_BOUND_SLACK + SB_ZERO_BITS
    tri = jnp.where(_iota((WIDE, WIDE), 0) >= _iota((WIDE, WIDE), 1), -1.0, 0.0).astype(BF16)

    def key_rows(ku, nk):
        r0 = ku * SUB
        return pl.ds(r0 if isinstance(r0, int) else pl.multiple_of(r0, SUB), nk)

    def logits(a, ku, nk):
        return _dot_nt(qs_ref[a * RB:(a + 1) * RB, :], k_ref[0, key_rows(ku, nk), :])

    def softplus2(z, diagonal):
        sp = jnp.maximum(z, 0.0) + jnp.log(1.0 + jnp.exp2(-jnp.abs(z))) * LOG2E
        if not diagonal:
            return sp, None
        mask = _iota(z.shape, 1) < (_iota(z.shape, 0) & (SUB - 1))
        return jnp.where(mask, sp, 0.0), mask

    def suffix(sp):
        nk = sp.shape[1]
        return _dot(sp.astype(BF16), tri[:nk, :nk]), jnp.sum(sp, axis=1, keepdims=True)

    def weights(z, res, car, mask):
        att = jnp.exp2(z + res if car is None else z + res - car)
        if mask is not None:
            att = jnp.where(mask, att, 0.0)
        return att.astype(BF16)

    def far_tile(a, ku):
        rows = slice(a * RB, (a + 1) * RB)
        z = logits(a, ku, SUB)
        sp, _ = softplus2(z, False)
        res, tot = suffix(sp)
        car = car_ref[rows, :]
        acc_ref[rows, :] += _dot(weights(z, res, car, None), v_ref[0, key_rows(ku, SUB), :])
        car_ref[rows, :] = car + tot

    def near_tiles(first_block):
        chains = []
        for a in range(ns):
            g = a if first_block else ns * i + a
            chain = [(a, g, SUB, True)]
            if not first_block or a >= 2:
                chain.append((a, g - 2, WIDE, False))
            elif a == 1:
                chain.append((a, 0, SUB, False))
            chains.append(chain)
        jobs = [job for step in itertools.zip_longest(*chains) for job in step if job is not None]
        zs = [logits(a, ku, nk) for a, ku, nk, _ in jobs]
        sps = [softplus2(z, diagonal) for z, (_, _, _, diagonal) in zip(zs, jobs)]
        sufs = [suffix(sp) for sp, _ in sps]
        car = [None] * ns
        atts = []
        for z, (sp, mask), (res, tot), (a, _, _, _) in zip(zs, sps, sufs, jobs):
            atts.append(weights(z, res, car[a], mask))
            car[a] = tot if car[a] is None else car[a] + tot
        acc = [None] * ns
        for att, (a, ku, nk, _) in zip(atts, jobs):
            t = _dot(att, v_ref[0, key_rows(ku, nk), :])
            acc[a] = t if acc[a] is None else acc[a] + t
        for a in range(ns):
            acc_ref[a * RB:(a + 1) * RB, :] = acc[a]
            car_ref[a * RB:(a + 1) * RB, :] = jnp.broadcast_to(car[a], (RB, LANES))

    pl.when(i == 0)(functools.partial(near_tiles, True))
    pl.when(i > 0)(functools.partial(near_tiles, False))

    def next_unit(a, t):
        return ns * i + a - 3 - t

    def pending(t):
        gap = [jnp.where(next_unit(a, t) >= 0, mar_ref[a * RB:(a + 1) * RB, :] - car_ref[a * RB:(a + 1) * RB, :], -1.0)
               for a in range(ns)]
        return jnp.max(functools.reduce(jnp.maximum, gap)) > 0.0

    def far_body(c):
        t, _ = c
        for a in range(ns):
            ku = next_unit(a, t)
            pl.when(ku >= 0)(functools.partial(far_tile, a, ku))
        return t + 1, pending(t + 1)

    lax.while_loop(lambda c: c[1], far_body, (0, pending(0)))

    for a in range(ns):
        o_ref[0, a * SUB:(a + 1) * SUB, :] = jnp.where(
            head0, acc_ref[a * RB:a * RB + SUB, :], acc_ref[a * RB + SUB:(a + 1) * RB, :]).astype(o_ref.dtype)


def _stick_breaking(pc):
    bsz, s, _ = pc.shape
    npair = SB_W // LANES
    return pl.pallas_call(
        _sb_kernel,
        out_shape=jax.ShapeDtypeStruct((bsz, s, SB_W), BF16),
        grid=(bsz, npair, s // SB_TQ),
        in_specs=[
            pl.BlockSpec((1, SB_TQ, LANES), lambda b, p, i: (b, i, p)),
            pl.BlockSpec((1, s, LANES), lambda b, p, i: (b, 0, npair + p)),
            pl.BlockSpec((1, s, LANES), lambda b, p, i: (b, 0, 2 * npair + p)),
        ],
        out_specs=pl.BlockSpec((1, SB_TQ, LANES), lambda b, p, i: (b, i, p)),
        scratch_shapes=[
            pltpu.VMEM((2 * SB_TQ, LANES), BF16),
            pltpu.VMEM((2 * SB_TQ, LANES), F32),
            pltpu.VMEM((2 * SB_TQ, LANES), F32),
            pltpu.VMEM((2 * SB_TQ, LANES), F32),
            pltpu.VMEM((8, LANES), F32),
        ],
        compiler_params=_params("parallel", "parallel", "arbitrary"),
        name="stick_breaking",
    )(pc, pc, pc)


def _t5_bucket(dist):
    max_exact = N_REL_BUCKETS // 2
    dd = np.maximum(dist, 1).astype(np.float64)
    large = max_exact + (np.log(dd / max_exact) / np.log(REL_MAX_DIST / max_exact)
                         * (N_REL_BUCKETS - max_exact)).astype(np.int32)
    large = np.minimum(large, N_REL_BUCKETS - 1)
    return np.where(dist < max_exact, dist, large).astype(np.int32)


def _dil_buckets():
    qi = np.arange(DIL_BLK)[:, None]
    kj = np.arange(2 * DIL_BLK)[None, :]
    delta = qi - kj + DIL_BLK
    in_win = (delta >= 0) & (delta <= DIL_BLK)
    tabs = []
    for window, dil in DIL_PAIRS:
        assert window // dil == DIL_BLK
        bucket = _t5_bucket(np.clip(delta, 0, None) * dil)
        tabs.append(np.where(in_win, bucket, -1).astype(np.int32))
    return np.stack(tabs, 0)


def _dil_kernel(q_ref, k_ref, v_ref, bkt_ref, tab_ref, o_ref, m_sc, l_sc, acc_sc):
    s_len = q_ref.shape[1]
    hp = pl.program_id(1)
    g = pl.program_id(2)
    blk = DIL_BLK
    lane = _iota((1, LANES), 1)
    lane_head = lane >> _HEAD_SHIFT

    @pl.when(g == 0)
    def _():
        m_sc[...] = jnp.full_like(m_sc, NEG_INF)
        l_sc[...] = jnp.zeros_like(l_sc)
        acc_sc[...] = jnp.zeros_like(acc_sc)

    kcol = _iota((2 * blk, 2 * blk), 1)

    def group(gi, dil):
        nb = s_len // (blk * dil)
        bkt = bkt_ref[gi]
        biases = []
        for hl in range(2):
            head = gi * DIL_HEADS_PER_GROUP + hp * 2 + hl
            bias = jnp.full((blk, 2 * blk), NEG_INF, F32)
            for bk in range(N_REL_BUCKETS):
                bias = jnp.where(bkt == bk, tab_ref[bk, head], bias)
            biases.append(bias)
        bias2 = jnp.concatenate(biases, axis=0)
        sel0 = lane_head == 0

        def stage_logits(t):
            r = t // nb
            n = t % nb
            q_start = r + dil * blk * n
            p_start = r + dil * blk * jnp.maximum(n - 1, 0)
            rows_q = pl.ds(q_start, blk, stride=dil) if dil > 1 else pl.ds(q_start, blk)
            rows_p = pl.ds(p_start, blk, stride=dil) if dil > 1 else pl.ds(p_start, blk)
            qb = q_ref[0, rows_q, :] * (DIL_DH ** -0.5)
            q2 = jnp.concatenate([jnp.where(sel0, qb, 0.0), jnp.where(sel0, 0.0, qb)], axis=0).astype(BF16)
            kk = jnp.concatenate([k_ref[0, rows_p, :], k_ref[0, rows_q, :]], axis=0).astype(BF16)
            logits = _dot_nt(q2, kk) + bias2
            logits = jnp.where(jnp.logical_and(n == 0, kcol < blk), NEG_INF, logits)
            return rows_q, rows_p, logits

        def stage_max(rows_q, logits):
            m_old = m_sc[rows_q, :]
            m_old_col = jnp.concatenate([m_old[:, 0:1], m_old[:, DIL_DH:DIL_DH + 1]], axis=0)
            m_col = jnp.maximum(m_old_col, jnp.max(logits, axis=1, keepdims=True))
            return m_old, m_col

        def stage_pv(rows_q, rows_p, logits, m_col):
            p = jnp.exp(logits - m_col)
            vv = jnp.concatenate([v_ref[0, rows_p, :], v_ref[0, rows_q, :]], axis=0)
            vv = jnp.concatenate([vv, jnp.ones_like(vv)], axis=1).astype(BF16)
            pv = _dot(p.astype(BF16), vv)
            return pv[:, LANES:], pv[:, :LANES]

        def body(tt, carry):
            ts = [tt * DIL_UNROLL + u for u in range(DIL_UNROLL)]
            s1 = [stage_logits(t) for t in ts]
            s2 = [stage_max(rq, lg) for rq, _, lg in s1]
            s3 = [stage_pv(rq, rp, lg, mc) for (rq, rp, lg), (_, mc) in zip(s1, s2)]
            outs = []
            for (rows_q, _, _), (m_old, m_col), (ps, pv) in zip(s1, s2, s3):
                m_new = jnp.where(sel0, m_col[0:blk], m_col[blk:])
                alpha = jnp.exp(m_old - m_new)
                l_new = alpha * l_sc[rows_q, :] + jnp.where(sel0, ps[0:blk], ps[blk:])
                a_new = alpha * acc_sc[rows_q, :] + jnp.where(sel0, pv[0:blk], pv[blk:])
                outs.append((rows_q, m_new, l_new, a_new))
            for rows_q, m_new, l_new, a_new in outs:
                m_sc[rows_q, :] = m_new
                l_sc[rows_q, :] = l_new
                acc_sc[rows_q, :] = a_new
            return carry

        lax.fori_loop(0, s_len // (blk * DIL_UNROLL), body, 0)

    for gi, (_, dil) in enumerate(DIL_PAIRS):
        pl.when(g == gi)(functools.partial(group, gi, dil))

    @pl.when(g == len(DIL_PAIRS) - 1)
    def _():
        o_ref[0] = (acc_sc[...] / l_sc[...]).astype(o_ref.dtype)


def _dilated(pd, rel_bias):
    bsz, s, _ = pd.shape
    ng = len(DIL_PAIRS)
    gw = DIL_HEADS_PER_GROUP * DIL_DH // LANES
    nq = DIL_W // LANES
    bkt = jnp.asarray(_dil_buckets())
    blk_spec = lambda base: pl.BlockSpec((1, s, LANES), lambda b, p, g: (b, 0, base + g * gw + p))
    return pl.pallas_call(
        _dil_kernel,
        out_shape=jax.ShapeDtypeStruct((bsz, s, DIL_OUT), BF16),
        grid=(bsz, gw, ng),
        in_specs=[
            blk_spec(0),
            blk_spec(nq),
            blk_spec(2 * nq),
            pl.BlockSpec((ng, DIL_BLK, 2 * DIL_BLK), lambda b, p, g: (0, 0, 0)),
            pl.BlockSpec(memory_space=pltpu.SMEM),
        ],
        out_specs=pl.BlockSpec((1, s, LANES), lambda b, p, g: (b, 0, p)),
        scratch_shapes=[pltpu.VMEM((s, LANES), F32)] * 3,
        compiler_params=_params("parallel", "parallel", "arbitrary"),
        name="dilated",
    )(pd, pd, pd, bkt, rel_bias)


def _ab_weight(w):
    o = np.cumsum((0,) + (GLA_QK, GLA_QK, GLA_V, GLA_V, GLA_GATE_RANK, 2 * ML_W, ML_W, MLSTM_HEADS, MLSTM_HEADS, ML_W))
    qa, ka, va, ra, aa, qkb, vb, ib, fb, ob = [w[:, o[j]:o[j + 1]] for j in range(10)]
    pad = jnp.zeros((w.shape[0], LANES - 2 * MLSTM_HEADS - GLA_GATE_RANK), w.dtype)
    return jnp.concatenate([qa, ka, va, ra, qkb, vb, ob, ib, fb, aa, pad], axis=1).astype(BF16)


def kernel(x, c, ada_w, ada_b, ln_g, ln_b, ab_w_in, gla_wa_up, gla_ba, gla_norm_g, ml_conv_w, ml_conv_b,
           ml_b_i, ml_b_f, ml_norm_g, ab_w_out, cd_w_in, rel_bias, cd_w_out, ffn_w1, ffn_w3, ffn_w2):
    bsz, s, d = x.shape
    mod_all = _ada_mod(c, ada_w, ada_b).reshape(DEPTH, bsz, 6, d)
    tm = 512
    for layer in range(DEPTH):
        mod = mod_all[layer]
        j = layer // 2
        if layer % 2 == 0:
            (proj,) = _inproj(x, mod, _ab_weight(ab_w_in[j]), [(P0_N, F32)], tm, 0, 1, "inproj0")
            cat = _mixer0(proj, gla_wa_up[j], gla_ba[j], gla_norm_g[j], ml_conv_w[j], ml_conv_b[j],
                          ml_b_i[j], ml_b_f[j], ml_norm_g[j])
            acts, w_outs = [cat], [ab_w_out[j].astype(BF16)]
        else:
            pc, pd = _inproj(x, mod, cd_w_in[j].astype(BF16), [(3 * SB_W, BF16), (3 * DIL_W, F32)],
                             tm, 0, 1, "inproj1")
            oc = _stick_breaking(pc)
            od = _dilated(pd, rel_bias)
            w_out = cd_w_out[j].astype(BF16)
            acts, w_outs = [oc, od], [w_out[:SB_W], w_out[SB_W:]]
        x = _post(acts, w_outs, x, mod, ffn_w1[layer], ffn_w3[layer], ffn_w2[layer], ln_g[layer], ln_b[layer], FFN_TM)
    return x
```

```python
import functools
import itertools

import numpy as np
import jax
import jax.numpy as jnp
from jax import lax
from jax.experimental import pallas as pl
from jax.experimental.pallas import tpu as pltpu

F32 = jnp.float32
BF16 = jnp.bfloat16

D_MODEL = 1024
DEPTH = 2
GLA_HEADS = 4
GLA_DK = 64
GLA_DV = 128
GLA_GATE_RANK = 16
GLA_TAU = 16.0
GLA_CHUNK = 64
MLSTM_HEADS = 4
MLSTM_DH = 128
MLSTM_CONV = 4
SB_HEADS = 8
SB_DH = 64
DIL_PAIRS = ((128, 1), (512, 4), (2048, 16))
DIL_HEADS_PER_GROUP = 4
DIL_DH = 64
DIL_BLK = 128
DIL_UNROLL = 4
N_REL_BUCKETS = 32
REL_MAX_DIST = 2048
D_FF = ((8 * D_MODEL + 3 * 256 - 1) // (3 * 256)) * 256
LN_EPS = 1e-5
RES_ALPHA = (2 * DEPTH) ** 0.25
NEG_INF = -1e30

GLA_QK = GLA_HEADS * GLA_DK
GLA_V = GLA_HEADS * GLA_DV
ML_W = MLSTM_HEADS * MLSTM_DH
SB_W = SB_HEADS * SB_DH
DIL_W = len(DIL_PAIRS) * DIL_HEADS_PER_GROUP * DIL_DH
DIL_OUT = DIL_HEADS_PER_GROUP * DIL_DH

LANES = 128
VMEM_LIMIT = 56 * 1024 * 1024

P0_QA = 0
P0_KA = P0_QA + GLA_QK
P0_VA = P0_KA + GLA_QK
P0_RA = P0_VA + GLA_V
P0_QKB = P0_RA + GLA_V
P0_VB = P0_QKB + 2 * ML_W
P0_OB = P0_VB + ML_W
P0_G = P0_OB + ML_W
P0_N = P0_G + LANES
G_I = 0
G_F = MLSTM_HEADS
G_A = 2 * MLSTM_HEADS

MIX_TILE = 256
FF_CHUNK = 256
FFN_TM = 512
SB_TQ = 1024
SB_SUB = 128
SB_WIDE = 256
SB_ZERO_BITS = 160.0
SB_BOUND_SLACK = 1.01
LOG2E = 1.4426950408889634
_CHUNK_SHIFT = 6
_HEAD_SHIFT = 6
assert GLA_CHUNK == GLA_DK == 1 << _CHUNK_SHIFT and SB_DH == DIL_DH == 1 << _HEAD_SHIFT


def _dot(a, b):
    return jnp.dot(a, b, preferred_element_type=F32)


def _dot_nt(a, b):
    return lax.dot_general(a, b, (((1,), (1,)), ((), ())), preferred_element_type=F32)


def _dot_tn(a, b):
    return lax.dot_general(a, b, (((0,), (0,)), ((), ())), preferred_element_type=F32)


def _dot_split(t, x, terms):
    acc = None
    rem = x
    for i in range(terms):
        part = rem.astype(BF16)
        d = _dot(t, part)
        acc = d if acc is None else acc + d
        if i + 1 < terms:
            rem = rem - part.astype(F32)
    return acc


def _log_sigmoid(x):
    return jnp.minimum(x, 0.0) - jnp.log(1.0 + jnp.exp(-jnp.abs(x)))


def _silu(x):
    return x * jax.nn.sigmoid(x)


def _layer_norm(r, g, b):
    mu = jnp.mean(r, axis=-1, keepdims=True)
    d = r - mu
    var = jnp.mean(d * d, axis=-1, keepdims=True)
    return d * lax.rsqrt(var + LN_EPS) * g + b


def _head_norm(x, g):
    mu = jnp.mean(x, axis=-1, keepdims=True)
    d = x - mu
    var = jnp.mean(d * d, axis=-1, keepdims=True)
    return d * lax.rsqrt(var + LN_EPS) * g


def _iota(shape, dim):
    return lax.broadcasted_iota(jnp.int32, shape, dim)


def _params(*sem):
    return pltpu.CompilerParams(dimension_semantics=sem, vmem_limit_bytes=VMEM_LIMIT)


def _ada_kernel(c_ref, w_ref, b_ref, o_ref):
    ca = _silu(c_ref[...]).astype(BF16)
    o_ref[0] = _dot(ca, w_ref[0].astype(BF16)) + b_ref[0]


def _ada_mod(c, ada_w, ada_b):
    bsz, d = c.shape
    n = ada_w.shape[-1]
    tn = n // 4
    return pl.pallas_call(
        _ada_kernel,
        out_shape=jax.ShapeDtypeStruct((DEPTH, bsz, n), F32),
        grid=(DEPTH, n // tn),
        in_specs=[
            pl.BlockSpec((bsz, d), lambda l, j: (0, 0)),
            pl.BlockSpec((1, d, tn), lambda l, j: (l, 0, j)),
            pl.BlockSpec((1, 1, tn), lambda l, j: (l, 0, j)),
        ],
        out_specs=pl.BlockSpec((1, bsz, tn), lambda l, j: (l, 0, j)),
        compiler_params=_params("arbitrary", "arbitrary"),
        name="ada_mod",
    )(c, ada_w, ada_b.reshape(DEPTH, 1, n))


def _inproj_kernel(x_ref, mod_ref, w_ref, *o_refs, shift_row, scale_row):
    sh = mod_ref[0, shift_row:shift_row + 1, :]
    sc = mod_ref[0, scale_row:scale_row + 1, :]
    hm = (x_ref[0] * (1.0 + sc) + sh).astype(BF16)
    col = 0
    for o_ref in o_refs:
        n = o_ref.shape[-1]
        o_ref[0] = _dot(hm, w_ref[:, col:col + n]).astype(o_ref.dtype)
        col += n


def _inproj(x, mod, w, outs, tm, shift_row, scale_row, name):
    bsz, s, d = x.shape
    n = w.shape[1]
    assert sum(o[0] for o in outs) == n
    return pl.pallas_call(
        functools.partial(_inproj_kernel, shift_row=shift_row, scale_row=scale_row),
        out_shape=[jax.ShapeDtypeStruct((bsz, s, o[0]), o[1]) for o in outs],
        grid=(bsz, s // tm),
        in_specs=[
            pl.BlockSpec((1, tm, d), lambda b, i: (b, i, 0)),
            pl.BlockSpec((1, 6, d), lambda b, i: (b, 0, 0)),
            pl.BlockSpec((d, n), lambda b, i: (0, 0)),
        ],
        out_specs=[pl.BlockSpec((1, tm, o[0]), lambda b, i: (b, i, 0)) for o in outs],
        compiler_params=_params("parallel", "arbitrary"),
        name=name,
    )(x, mod, w)


def _post_kernel(*refs, n_act):
    act_refs = refs[:n_act]
    wo_refs = refs[n_act:2 * n_act]
    x_ref, mod_ref, w1_ref, w3_ref, w2_ref, g_ref, b_ref, o_ref, gm_ref = refs[2 * n_act:]
    y = None
    for a_ref, w_ref in zip(act_refs, wo_refs):
        t = _dot(a_ref[0], w_ref[...])
        y = t if y is None else y + t
    r = RES_ALPHA * x_ref[0] + (1.0 + mod_ref[0, 2:3, :]) * y
    xm = _layer_norm(r, g_ref[0:1, :], b_ref[0:1, :])
    hf = (xm * (1.0 + mod_ref[0, 4:5, :]) + mod_ref[0, 3:4, :]).astype(BF16)
    for c in range(D_FF // FF_CHUNK):
        cs = slice(c * FF_CHUNK, (c + 1) * FF_CHUNK)
        gm_ref[:, cs] = (_silu(_dot(hf, w1_ref[:, cs])) * _dot(hf, w3_ref[:, cs])).astype(BF16)
    y2 = _dot(gm_ref[...], w2_ref[...])
    r2 = RES_ALPHA * xm + (1.0 + mod_ref[0, 5:6, :]) * y2
    o_ref[0] = _layer_norm(r2, g_ref[1:2, :], b_ref[1:2, :])


def _post(acts, w_outs, x, mod, w1, w3, w2, ln_g, ln_b, tm):
    bsz, s, d = x.shape
    n_act = len(acts)
    const = lambda shape: pl.BlockSpec(shape, lambda b, i: (0,) * len(shape), pipeline_mode=pl.Buffered(1))
    in_specs = [pl.BlockSpec((1, tm, a.shape[-1]), lambda b, i: (b, i, 0)) for a in acts]
    in_specs += [const(w.shape) for w in w_outs]
    in_specs += [
        pl.BlockSpec((1, tm, d), lambda b, i: (b, i, 0)),
        pl.BlockSpec((1, 6, d), lambda b, i: (b, 0, 0)),
        const((d, D_FF)),
        const((d, D_FF)),
        const((D_FF, d)),
        const((2, d)),
        const((2, d)),
    ]
    return pl.pallas_call(
        functools.partial(_post_kernel, n_act=n_act),
        out_shape=jax.ShapeDtypeStruct((bsz, s, d), F32),
        grid=(bsz, s // tm),
        in_specs=in_specs,
        out_specs=pl.BlockSpec((1, tm, d), lambda b, i: (b, i, 0)),
        scratch_shapes=[pltpu.VMEM((tm, D_FF), BF16)],
        compiler_params=_params("parallel", "arbitrary"),
        name="post",
    )(*acts, *w_outs, x, mod, w1.astype(BF16), w3.astype(BF16), w2.astype(BF16), ln_g, ln_b)


def _mixer0_kernel(p_ref, wa_ref, ba_ref, gg_ref, cw_ref, cb_ref, gb_ref, mg_ref, o_ref,
                   st_ref, cst_ref, m_ref, xc_ref):
    L = MIX_TILE
    i = pl.program_id(1)

    @pl.when(i == 0)
    def _():
        st_ref[...] = jnp.zeros_like(st_ref)
        cst_ref[...] = jnp.zeros_like(cst_ref)
        m_ref[...] = jnp.zeros_like(m_ref)
        xc_ref[0:8, :] = jnp.zeros((8, 2 * ML_W), F32)

    lane = _iota((1, LANES), 1)
    g_raw = p_ref[0, :, P0_G:P0_G + LANES]

    u = _dot(g_raw.astype(BF16), wa_ref[...]) + ba_ref[...]
    la = _log_sigmoid(u) * (1.0 / GLA_TAU)
    row = _iota((L, L), 0)
    col = _iota((L, L), 1)
    same_chunk = (row >> _CHUNK_SHIFT) == (col >> _CHUNK_SHIFT)
    tri_blk = jnp.where(same_chunk & (col <= row), 1.0, 0.0).astype(BF16)
    bcs = _dot_split(tri_blk, la, 3)
    q_in = p_ref[0, :, P0_QA:P0_QA + GLA_QK] * (GLA_DK ** -0.5) * jnp.exp(bcs)
    k_raw = p_ref[0, :, P0_KA:P0_KA + GLA_QK]
    k_in = k_raw * jnp.exp(-bcs)
    lane_qk = _iota((1, GLA_QK), 1)
    head_masks = [(lane_qk >> _CHUNK_SHIFT) == h for h in range(GLA_HEADS)]
    r4 = _iota((GLA_HEADS * GLA_CHUNK, GLA_CHUNK), 0)
    c4 = _iota((GLA_HEADS * GLA_CHUNK, GLA_CHUNK), 1)
    tril4 = c4 <= (r4 & (GLA_CHUNK - 1))
    oa_chunks = []
    for c in range(L // GLA_CHUNK):
        r0, r1 = c * GLA_CHUNK, (c + 1) * GLA_CHUNK
        b_c = bcs[r0:r1]
        bl = b_c[GLA_CHUNK - 1:GLA_CHUNK, :]
        q_c = q_in[r0:r1]
        k_c = k_in[r0:r1].astype(BF16)
        k_end = k_raw[r0:r1] * jnp.exp(bl - b_c)
        v_c = p_ref[0, r0:r1, P0_VA:P0_VA + GLA_V]
        q_exp = jnp.concatenate([jnp.where(hm, q_c, 0.0) for hm in head_masks], axis=0).astype(BF16)
        k_exp = jnp.concatenate([jnp.where(hm, k_end, 0.0) for hm in head_masks], axis=0).astype(BF16)
        sc = jnp.where(tril4, _dot_nt(q_exp, k_c), 0.0)
        intra = _dot(sc.astype(BF16), v_c.astype(BF16))
        inter = _dot_nt(q_exp, st_ref[...].astype(BF16))
        o_heads = []
        for h in range(GLA_HEADS):
            h0, h1 = h * GLA_CHUNK, (h + 1) * GLA_CHUNK
            o_heads.append(intra[h0:h1, h * GLA_DV:(h + 1) * GLA_DV] + inter[h0:h1])
        oa_chunks.append(o_heads)
        v_cat = jnp.concatenate([v_c[:, h * GLA_DV:(h + 1) * GLA_DV] for h in range(GLA_HEADS)], axis=0)
        st_ref[...] = st_ref[...] * jnp.exp(bl) + _dot_tn(v_cat.astype(BF16), k_exp)
    for h in range(GLA_HEADS):
        o_h = jnp.concatenate([oc[h] for oc in oa_chunks], axis=0)
        cs = slice(h * GLA_DV, (h + 1) * GLA_DV)
        ra = p_ref[0, :, P0_RA + h * GLA_DV:P0_RA + (h + 1) * GLA_DV]
        o_ref[0, :, cs] = (_head_norm(o_h, gg_ref[:, cs]) * _silu(ra)).astype(o_ref.dtype)

    xc_ref[8:8 + L, :] = p_ref[0, :, P0_QKB:P0_QKB + 2 * ML_W]
    conv = cb_ref[...]
    for kk in range(MLSTM_CONV):
        conv = conv + cw_ref[kk:kk + 1, :] * xc_ref[8 - (MLSTM_CONV - 1) + kk:8 - (MLSTM_CONV - 1) + kk + L, :]
    xc_ref[0:8, :] = xc_ref[L:L + 8, :]
    qk_b = _silu(conv)

    gb = jnp.where(lane < G_A, g_raw + gb_ref[...], 0.0)
    ipre = gb
    logf = _log_sigmoid(pltpu.roll(gb, LANES - G_F, axis=1))
    logf = jnp.where(lane < MLSTM_HEADS, logf, 0.0)
    tri = jnp.where(col <= row, 1.0, 0.0).astype(BF16)
    bcum = _dot_split(tri, logf, 3)
    blast = bcum[L - 1:L, :]
    wend = blast - bcum + ipre
    m_prev = m_ref[...]
    m_new = jnp.maximum(blast + m_prev, jnp.max(wend, axis=0, keepdims=True))
    scl = jnp.exp(blast + m_prev - m_new)
    wj = jnp.exp(wend - m_new)
    rows_src = jnp.where(lane < MLSTM_HEADS, ipre, pltpu.roll(bcum, MLSTM_HEADS, axis=1))
    rows_t = rows_src.T
    causal = col <= row
    ones = jnp.ones((L, MLSTM_DH), F32)
    for h in range(MLSTM_HEADS):
        hs = slice(h * MLSTM_DH, (h + 1) * MLSTM_DH)
        b_col = bcum[:, h:h + 1]
        ip_row = rows_t[h:h + 1, :]
        b_row = rows_t[MLSTM_HEADS + h:MLSTM_HEADS + h + 1, :]
        dlog = jnp.where(causal, b_col - b_row + ip_row, NEG_INF)
        il = b_col + m_prev[:, h:h + 1]
        m_i = jnp.maximum(il, jnp.max(dlog, axis=1, keepdims=True))
        w_intra = jnp.exp(dlog - m_i)
        s_inter = jnp.exp(il - m_i)
        q_h = qk_b[:, h * MLSTM_DH:(h + 1) * MLSTM_DH].astype(BF16)
        k_h = (qk_b[:, ML_W + h * MLSTM_DH:ML_W + (h + 1) * MLSTM_DH] * (MLSTM_DH ** -0.5)).astype(BF16)
        v_h = p_ref[0, :, P0_VB + h * MLSTM_DH:P0_VB + (h + 1) * MLSTM_DH]
        v_aug = jnp.concatenate([v_h, ones], axis=1)
        a = (w_intra * _dot_nt(q_h, k_h)).astype(BF16)
        c_prev = cst_ref[h]
        tot = _dot(a, v_aug.astype(BF16)) + s_inter * _dot_nt(q_h, c_prev.astype(BF16))
        num = tot[:, :MLSTM_DH]
        den = tot[:, MLSTM_DH:]
        hid = num / jnp.maximum(jnp.abs(den), jnp.exp(-m_i))
        w_aug = (v_aug * wj[:, h:h + 1]).astype(BF16)
        cst_ref[h] = scl[:, h:h + 1] * c_prev + _dot_tn(w_aug, k_h)
        ob = p_ref[0, :, P0_OB + h * MLSTM_DH:P0_OB + (h + 1) * MLSTM_DH]
        o_ref[0, :, GLA_V + h * MLSTM_DH:GLA_V + (h + 1) * MLSTM_DH] = (
            jax.nn.sigmoid(ob) * _head_norm(hid, mg_ref[:, hs])).astype(o_ref.dtype)
    m_ref[...] = m_new


def _mixer0(proj, wa_up, ba, gla_g, conv_w, conv_b, b_i, b_f, ml_g):
    bsz, s, n = proj.shape
    L = MIX_TILE
    wa_pad = jnp.zeros((LANES, GLA_QK), F32).at[G_A:G_A + GLA_GATE_RANK].set(wa_up).astype(BF16)
    gbias = jnp.zeros((1, LANES), F32).at[0, G_I:G_I + MLSTM_HEADS].set(b_i).at[0, G_F:G_F + MLSTM_HEADS].set(b_f)
    full = lambda shape: pl.BlockSpec(shape, lambda b, i: (0,) * len(shape))
    return pl.pallas_call(
        _mixer0_kernel,
        out_shape=jax.ShapeDtypeStruct((bsz, s, GLA_V + ML_W), BF16),
        grid=(bsz, s // L),
        in_specs=[
            pl.BlockSpec((1, L, n), lambda b, i: (b, i, 0)),
            full((LANES, GLA_QK)),
            full((1, GLA_QK)),
            full((1, GLA_V)),
            full((MLSTM_CONV, 2 * ML_W)),
            full((1, 2 * ML_W)),
            full((1, LANES)),
            full((1, ML_W)),
        ],
        out_specs=pl.BlockSpec((1, L, GLA_V + ML_W), lambda b, i: (b, i, 0)),
        scratch_shapes=[
            pltpu.VMEM((GLA_DV, GLA_QK), F32),
            pltpu.VMEM((MLSTM_HEADS, 2 * MLSTM_DH, MLSTM_DH), F32),
            pltpu.VMEM((1, LANES), F32),
            pltpu.VMEM((8 + L, 2 * ML_W), F32),
        ],
        compiler_params=_params("parallel", "arbitrary"),
        name="mixer0",
    )(proj, wa_pad, ba.reshape(1, GLA_QK), gla_g.reshape(1, GLA_V), conv_w, conv_b.reshape(1, 2 * ML_W),
      gbias, ml_g.reshape(1, ML_W))


def _sb_kernel(q_ref, k_ref, v_ref, o_ref, qs_ref, acc_ref, car_ref, mar_ref, kn_ref):
    TQ, SUB, WIDE = SB_TQ, SB_SUB, SB_WIDE
    ns = TQ // SUB
    RB = 2 * SUB
    i = pl.program_id(2)
    lane = _iota((1, LANES), 1)
    head0 = (lane >> _HEAD_SHIFT) == 0

    @pl.when(i == 0)
    def _():
        k2 = jnp.square(k_ref[0].astype(F32))
        for h in range(2):
            n2 = jnp.sum(jnp.where(head0 if h == 0 else ~head0, k2, 0.0), axis=1, keepdims=True)
            kn_ref[h:h + 1, :] = jnp.broadcast_to(jnp.sqrt(jnp.max(n2, axis=0, keepdims=True)), (1, LANES))

    for a in range(ns):
        qa = q_ref[0, a * SUB:(a + 1) * SUB, :].astype(F32) * (SB_DH ** -0.5 * LOG2E)
        for h in range(2):
            rows = slice(a * RB + h * SUB, a * RB + (h + 1) * SUB)
            qh = jnp.where(head0 if h == 0 else ~head0, qa, 0.0).astype(BF16)
            qs_ref[rows, :] = qh
            qn = jnp.sqrt(jnp.sum(jnp.square(qh.astype(F32)), axis=1, keepdims=True))
            mar_ref[rows, :] = qn * kn_ref[h:h + 1, :] * SB_BOUND_SLACK + SB_ZERO_BITS
    tri = jnp.where(_iota((WIDE, WIDE), 0) >= _iota((WIDE, WIDE), 1), -1.0, 0.0).astype(BF16)

    def key_rows(ku, nk):
        r0 = ku * SUB
        return pl.ds(r0 if isinstance(r0, int) else pl.multiple_of(r0, SUB), nk)

    def logits(a, ku, nk):
        return _dot_nt(qs_ref[a * RB:(a + 1) * RB, :], k_ref[0, key_rows(ku, nk), :])

    def softplus2(z, diagonal):
        sp = jnp.maximum(z, 0.0) + jnp.log(1.0 + jnp.exp2(-jnp.abs(z))) * LOG2E
        if not diagonal:
            return sp, None
        mask = _iota(z.shape, 1) < (_iota(z.shape, 0) & (SUB - 1))
        return jnp.where(mask, sp, 0.0), mask

    def suffix(sp):
        nk = sp.shape[1]
        return _dot(sp.astype(BF16), tri[:nk, :nk]), jnp.sum(sp, axis=1, keepdims=True)

    def weights(z, res, car, mask):
        att = jnp.exp2(z + res if car is None else z + res - car)
        if mask is not None:
            att = jnp.where(mask, att, 0.0)
        return att.astype(BF16)

    def far_tile(a, ku):
        rows = slice(a * RB, (a + 1) * RB)
        z = logits(a, ku, SUB)
        sp, _ = softplus2(z, False)
        res, tot = suffix(sp)
        car = car_ref[rows, :]
        acc_ref[rows, :] += _dot(weights(z, res, car, None), v_ref[0, key_rows(ku, SUB), :])
        car_ref[rows, :] = car + tot

    def near_tiles(first_block):
        chains = []
        for a in range(ns):
            g = a if first_block else ns * i + a
            chain = [(a, g, SUB, True)]
            if not first_block or a >= 2:
                chain.append((a, g - 2, WIDE, False))
            elif a == 1:
                chain.append((a, 0, SUB, False))
            chains.append(chain)
        jobs = [job for step in itertools.zip_longest(*chains) for job in step if job is not None]
        zs = [logits(a, ku, nk) for a, ku, nk, _ in jobs]
        sps = [softplus2(z, diagonal) for z, (_, _, _, diagonal) in zip(zs, jobs)]
        sufs = [suffix(sp) for sp, _ in sps]
        car = [None] * ns
        atts = []
        for z, (sp, mask), (res, tot), (a, _, _, _) in zip(zs, sps, sufs, jobs):
            atts.append(weights(z, res, car[a], mask))
            car[a] = tot if car[a] is None else car[a] + tot
        acc = [None] * ns
        for att, (a, ku, nk, _) in zip(atts, jobs):
            t = _dot(att, v_ref[0, key_rows(ku, nk), :])
            acc[a] = t if acc[a] is None else acc[a] + t
        for a in range(ns):
            acc_ref[a * RB:(a + 1) * RB, :] = acc[a]
            car_ref[a * RB:(a + 1) * RB, :] = jnp.broadcast_to(car[a], (RB, LANES))

    pl.when(i == 0)(functools.partial(near_tiles, True))
    pl.when(i > 0)(functools.partial(near_tiles, False))

    def next_unit(a, t):
        return ns * i + a - 3 - t

    def pending(t):
        gap = [jnp.where(next_unit(a, t) >= 0, mar_ref[a * RB:(a + 1) * RB, :] - car_ref[a * RB:(a + 1) * RB, :], -1.0)
               for a in range(ns)]
        return jnp.max(functools.reduce(jnp.maximum, gap)) > 0.0

    def far_body(c):
        t, _ = c
        for a in range(ns):
            ku = next_unit(a, t)
            pl.when(ku >= 0)(functools.partial(far_tile, a, ku))
        return t + 1, pending(t + 1)

    lax.while_loop(lambda c: c[1], far_body, (0, pending(0)))

    for a in range(ns):
        o_ref[0, a * SUB:(a + 1) * SUB, :] = jnp.where(
            head0, acc_ref[a * RB:a * RB + SUB, :], acc_ref[a * RB + SUB:(a + 1) * RB, :]).astype(o_ref.dtype)


def _stick_breaking(pc):
    bsz, s, _ = pc.shape
    npair = SB_W // LANES
    return pl.pallas_call(
        _sb_kernel,
        out_shape=jax.ShapeDtypeStruct((bsz, s, SB_W), BF16),
        grid=(bsz, npair, s // SB_TQ),
        in_specs=[
            pl.BlockSpec((1, SB_TQ, LANES), lambda b, p, i: (b, i, p)),
            pl.BlockSpec((1, s, LANES), lambda b, p, i: (b, 0, npair + p)),
            pl.BlockSpec((1, s, LANES), lambda b, p, i: (b, 0, 2 * npair + p)),
        ],
        out_specs=pl.BlockSpec((1, SB_TQ, LANES), lambda b, p, i: (b, i, p)),
        scratch_shapes=[
            pltpu.VMEM((2 * SB_TQ, LANES), BF16),
            pltpu.VMEM((2 * SB_TQ, LANES), F32),
            pltpu.VMEM((2 * SB_TQ, LANES), F32),
            pltpu.VMEM((2 * SB_TQ, LANES), F32),
            pltpu.VMEM((8, LANES), F32),
        ],
        compiler_params=_params("parallel", "parallel", "arbitrary"),
        name="stick_breaking",
    )(pc, pc, pc)


def _t5_bucket(dist):
    max_exact = N_REL_BUCKETS // 2
    dd = np.maximum(dist, 1).astype(np.float64)
    large = max_exact + (np.log(dd / max_exact) / np.log(REL_MAX_DIST / max_exact)
                         * (N_REL_BUCKETS - max_exact)).astype(np.int32)
    large = np.minimum(large, N_REL_BUCKETS - 1)
    return np.where(dist < max_exact, dist, large).astype(np.int32)


def _dil_buckets():
    qi = np.arange(DIL_BLK)[:, None]
    kj = np.arange(2 * DIL_BLK)[None, :]
    delta = qi - kj + DIL_BLK
    in_win = (delta >= 0) & (delta <= DIL_BLK)
    tabs = []
    for window, dil in DIL_PAIRS:
        assert window // dil == DIL_BLK
        bucket = _t5_bucket(np.clip(delta, 0, None) * dil)
        tabs.append(np.where(in_win, bucket, -1).astype(np.int32))
    return np.stack(tabs, 0)


def _dil_kernel(q_ref, k_ref, v_ref, bkt_ref, bkt0_ref, tab_ref, o_ref, m_sc, l_sc, acc_sc):
    s_len = q_ref.shape[1]
    hp = pl.program_id(1)
    g = pl.program_id(2)
    blk = DIL_BLK
    ng = len(DIL_PAIRS)
    lane = _iota((1, LANES), 1)
    lane_head = lane >> _HEAD_SHIFT
    sel0 = lane_head == 0
    kcol = _iota((2 * blk, 2 * blk), 1)

    def head_biases(gi, bkt):
        biases = []
        for hl in range(2):
            head = gi * DIL_HEADS_PER_GROUP + hp * 2 + hl
            bias = jnp.full(bkt.shape, NEG_INF, F32)
            for bk in range(N_REL_BUCKETS):
                bias = jnp.where(bkt == bk, tab_ref[bk, head], bias)
            biases.append(bias)
        return jnp.concatenate(biases, axis=0)

    def stack_heads(qb):
        return jnp.concatenate([jnp.where(sel0, qb, 0.0), jnp.where(sel0, 0.0, qb)], axis=0).astype(BF16)

    def with_ones(vv):
        return jnp.concatenate([vv, jnp.ones_like(vv)], axis=1).astype(BF16)

    def first_group(gi, dil):
        n_sub = s_len // dil
        bias2 = head_biases(gi, bkt0_ref[...])

        def residue(r):
            rows = pl.ds(r, n_sub, stride=dil)
            logits = _dot_nt(stack_heads(q_ref[0, rows, :] * (DIL_DH ** -0.5)), k_ref[0, rows, :].astype(BF16)) + bias2
            return rows, logits

        def body(tt, carry):
            s1 = [residue(tt * DIL_UNROLL + u) for u in range(DIL_UNROLL)]
            s2 = [jnp.max(lg, axis=1, keepdims=True) for _, lg in s1]
            s3 = [_dot(jnp.exp(lg - mc).astype(BF16), with_ones(v_ref[0, rows, :])) for (rows, lg), mc in zip(s1, s2)]
            for (rows, _), mc, pv in zip(s1, s2, s3):
                m_sc[rows, :] = jnp.where(sel0, mc[0:n_sub], mc[n_sub:])
                l_sc[rows, :] = jnp.where(sel0, pv[0:n_sub, LANES:], pv[n_sub:, LANES:])
                acc_sc[rows, :] = jnp.where(sel0, pv[0:n_sub, :LANES], pv[n_sub:, :LANES])
            return carry

        lax.fori_loop(0, dil // DIL_UNROLL, body, 0)

    def group(gi, dil):
        nb = s_len // (blk * dil)
        bias2 = head_biases(gi, bkt_ref[gi])

        def stage_logits(t):
            r = t // nb
            n = t % nb
            q_start = r + dil * blk * n
            p_start = r + dil * blk * jnp.maximum(n - 1, 0)
            rows_q = pl.ds(q_start, blk, stride=dil) if dil > 1 else pl.ds(q_start, blk)
            rows_p = pl.ds(p_start, blk, stride=dil) if dil > 1 else pl.ds(p_start, blk)
            q2 = stack_heads(q_ref[0, rows_q, :] * (DIL_DH ** -0.5))
            kk = jnp.concatenate([k_ref[0, rows_p, :], k_ref[0, rows_q, :]], axis=0).astype(BF16)
            logits = _dot_nt(q2, kk) + bias2
            logits = jnp.where(jnp.logical_and(n == 0, kcol < blk), NEG_INF, logits)
            return rows_q, rows_p, logits

        def stage_max(rows_q, logits):
            m_old = m_sc[rows_q, :]
            m_old_col = jnp.concatenate([m_old[:, 0:1], m_old[:, DIL_DH:DIL_DH + 1]], axis=0)
            m_col = jnp.maximum(m_old_col, jnp.max(logits, axis=1, keepdims=True))
            return m_old, m_col

        def stage_pv(rows_q, rows_p, logits, m_col):
            p = jnp.exp(logits - m_col)
            vv = with_ones(jnp.concatenate([v_ref[0, rows_p, :], v_ref[0, rows_q, :]], axis=0))
            pv = _dot(p.astype(BF16), vv)
            return pv[:, LANES:], pv[:, :LANES]

        def body(tt, carry):
            ts = [tt * DIL_UNROLL + u for u in range(DIL_UNROLL)]
            s1 = [stage_logits(t) for t in ts]
            s2 = [stage_max(rq, lg) for rq, _, lg in s1]
            s3 = [stage_pv(rq, rp, lg, mc) for (rq, rp, lg), (_, mc) in zip(s1, s2)]
            outs = []
            for (rows_q, _, _), (m_old, m_col), (ps, pv) in zip(s1, s2, s3):
                m_new = jnp.where(sel0, m_col[0:blk], m_col[blk:])
                alpha = jnp.exp(m_old - m_new)
                l_new = alpha * l_sc[rows_q, :] + jnp.where(sel0, ps[0:blk], ps[blk:])
                a_new = alpha * acc_sc[rows_q, :] + jnp.where(sel0, pv[0:blk], pv[blk:])
                outs.append((rows_q, m_new, l_new, a_new))
            for rows_q, m_new, l_new, a_new in outs:
                m_sc[rows_q, :] = m_new
                l_sc[rows_q, :] = l_new
                acc_sc[rows_q, :] = a_new
            return carry

        lax.fori_loop(0, s_len // (blk * DIL_UNROLL), body, 0)

    for gi, (_, dil) in enumerate(DIL_PAIRS):
        pl.when(g == ng - 1 - gi)(functools.partial(first_group if gi == ng - 1 else group, gi, dil))

    @pl.when(g == ng - 1)
    def _():
        o_ref[0] = (acc_sc[...] / l_sc[...]).astype(o_ref.dtype)


def _dilated(pd, rel_bias):
    bsz, s, _ = pd.shape
    ng = len(DIL_PAIRS)
    gw = DIL_HEADS_PER_GROUP * DIL_DH // LANES
    nq = DIL_W // LANES
    bkt = jnp.asarray(_dil_buckets())
    dil0 = DIL_PAIRS[-1][1]
    n_sub = s // dil0
    assert s % (DIL_BLK * dil0) == 0 and dil0 % DIL_UNROLL == 0
    delta = np.arange(n_sub)[:, None] - np.arange(n_sub)[None, :]
    bkt0 = jnp.asarray(np.where((delta >= 0) & (delta <= DIL_BLK), _t5_bucket(np.clip(delta, 0, None) * dil0), -1)
                       .astype(np.int32))
    blk_spec = lambda base: pl.BlockSpec((1, s, LANES), lambda b, p, g: (b, 0, base + (ng - 1 - g) * gw + p))
    return pl.pallas_call(
        _dil_kernel,
        out_shape=jax.ShapeDtypeStruct((bsz, s, DIL_OUT), BF16),
        grid=(bsz, gw, ng),
        in_specs=[
            blk_spec(0),
            blk_spec(nq),
            blk_spec(2 * nq),
            pl.BlockSpec((ng, DIL_BLK, 2 * DIL_BLK), lambda b, p, g: (0, 0, 0)),
            pl.BlockSpec((n_sub, n_sub), lambda b, p, g: (0, 0)),
            pl.BlockSpec(memory_space=pltpu.SMEM),
        ],
        out_specs=pl.BlockSpec((1, s, LANES), lambda b, p, g: (b, 0, p)),
        scratch_shapes=[pltpu.VMEM((s, LANES), F32)] * 3,
        compiler_params=_params("parallel", "parallel", "arbitrary"),
        name="dilated",
    )(pd, pd, pd, bkt, bkt0, rel_bias)


def _ab_weight(w):
    o = np.cumsum((0,) + (GLA_QK, GLA_QK, GLA_V, GLA_V, GLA_GATE_RANK, 2 * ML_W, ML_W, MLSTM_HEADS, MLSTM_HEADS, ML_W))
    qa, ka, va, ra, aa, qkb, vb, ib, fb, ob = [w[:, o[j]:o[j + 1]] for j in range(10)]
    pad = jnp.zeros((w.shape[0], LANES - 2 * MLSTM_HEADS - GLA_GATE_RANK), w.dtype)
    return jnp.concatenate([qa, ka, va, ra, qkb, vb, ob, ib, fb, aa, pad], axis=1).astype(BF16)


def kernel(x, c, ada_w, ada_b, ln_g, ln_b, ab_w_in, gla_wa_up, gla_ba, gla_norm_g, ml_conv_w, ml_conv_b,
           ml_b_i, ml_b_f, ml_norm_g, ab_w_out, cd_w_in, rel_bias, cd_w_out, ffn_w1, ffn_w3, ffn_w2):
    bsz, s, d = x.shape
    mod_all = _ada_mod(c, ada_w, ada_b).reshape(DEPTH, bsz, 6, d)
    tm = 512
    for layer in range(DEPTH):
        mod = mod_all[layer]
        j = layer // 2
        if layer % 2 == 0:
            (proj,) = _inproj(x, mod, _ab_weight(ab_w_in[j]), [(P0_N, F32)], tm, 0, 1, "inproj0")
            cat = _mixer0(proj, gla_wa_up[j], gla_ba[j], gla_norm_g[j], ml_conv_w[j], ml_conv_b[j],
                          ml_b_i[j], ml_b_f[j], ml_norm_g[j])
            acts, w_outs = [cat], [ab_w_out[j].astype(BF16)]
        else:
            pc, pd = _inproj(x, mod, cd_w_in[j].astype(BF16), [(3 * SB_W, BF16), (3 * DIL_W, F32)],
                             tm, 0, 1, "inproj1")
            oc = _stick_breaking(pc)
            od = _dilated(pd, rel_bias)
            w_out = cd_w_out[j].astype(BF16)
            acts, w_outs = [oc, od], [w_out[:SB_W], w_out[SB_W:]]
        x = _post(acts, w_outs, x, mod, ffn_w1[layer], ffn_w3[layer], ffn_w2[layer], ln_g[layer], ln_b[layer], FFN_TM)
    return x
```

```python
import functools
import itertools

import numpy as np
import jax
import jax.numpy as jnp
from jax import lax
from jax.experimental import pallas as pl
from jax.experimental.pallas import tpu as pltpu

F32 = jnp.float32
BF16 = jnp.bfloat16

D_MODEL = 1024
DEPTH = 2
GLA_HEADS = 4
GLA_DK = 64
GLA_DV = 128
GLA_GATE_RANK = 16
GLA_TAU = 16.0
GLA_CHUNK = 64
MLSTM_HEADS = 4
MLSTM_DH = 128
MLSTM_CONV = 4
SB_HEADS = 8
SB_DH = 64
DIL_PAIRS = ((128, 1), (512, 4), (2048, 16))
DIL_HEADS_PER_GROUP = 4
DIL_DH = 64
DIL_BLK = 128
DIL_UNROLL = 4
N_REL_BUCKETS = 32
REL_MAX_DIST = 2048
D_FF = ((8 * D_MODEL + 3 * 256 - 1) // (3 * 256)) * 256
LN_EPS = 1e-5
RES_ALPHA = (2 * DEPTH) ** 0.25
NEG_INF = -1e30

GLA_QK = GLA_HEADS * GLA_DK
GLA_V = GLA_HEADS * GLA_DV
ML_W = MLSTM_HEADS * MLSTM_DH
SB_W = SB_HEADS * SB_DH
DIL_W = len(DIL_PAIRS) * DIL_HEADS_PER_GROUP * DIL_DH
DIL_OUT = DIL_HEADS_PER_GROUP * DIL_DH

LANES = 128
VMEM_LIMIT = 56 * 1024 * 1024

P0_QA = 0
P0_KA = P0_QA + GLA_QK
P0_VA = P0_KA + GLA_QK
P0_RA = P0_VA + GLA_V
P0_QKB = P0_RA + GLA_V
P0_VB = P0_QKB + 2 * ML_W
P0_OB = P0_VB + ML_W
P0_G = P0_OB + ML_W
P0_N = P0_G + LANES
G_I = 0
G_F = MLSTM_HEADS
G_A = 2 * MLSTM_HEADS

MIX_TILE = 256
FF_CHUNK = 256
FFN_TM = 512
SB_TQ = 1024
SB_SUB = 128
SB_WIDE = 256
SB_ZERO_BITS = 160.0
SB_BOUND_SLACK = 1.01
LOG2E = 1.4426950408889634
_CHUNK_SHIFT = 6
_HEAD_SHIFT = 6
assert GLA_CHUNK == GLA_DK == 1 << _CHUNK_SHIFT and SB_DH == DIL_DH == 1 << _HEAD_SHIFT


def _dot(a, b):
    return jnp.dot(a, b, preferred_element_type=F32)


def _dot_nt(a, b):
    return lax.dot_general(a, b, (((1,), (1,)), ((), ())), preferred_element_type=F32)


def _dot_tn(a, b):
    return lax.dot_general(a, b, (((0,), (0,)), ((), ())), preferred_element_type=F32)


def _dot_split(t, x, terms):
    acc = None
    rem = x
    for i in range(terms):
        part = rem.astype(BF16)
        d = _dot(t, part)
        acc = d if acc is None else acc + d
        if i + 1 < terms:
            rem = rem - part.astype(F32)
    return acc


def _log_sigmoid(x):
    return jnp.minimum(x, 0.0) - jnp.log(1.0 + jnp.exp(-jnp.abs(x)))


def _silu(x):
    return x * jax.nn.sigmoid(x)


def _layer_norm(r, g, b):
    mu = jnp.mean(r, axis=-1, keepdims=True)
    d = r - mu
    var = jnp.mean(d * d, axis=-1, keepdims=True)
    return d * lax.rsqrt(var + LN_EPS) * g + b


def _head_norm(x, g):
    mu = jnp.mean(x, axis=-1, keepdims=True)
    d = x - mu
    var = jnp.mean(d * d, axis=-1, keepdims=True)
    return d * lax.rsqrt(var + LN_EPS) * g


def _iota(shape, dim):
    return lax.broadcasted_iota(jnp.int32, shape, dim)


def _params(*sem):
    return pltpu.CompilerParams(dimension_semantics=sem, vmem_limit_bytes=VMEM_LIMIT)


def _ada_kernel(c_ref, w_ref, b_ref, o_ref):
    ca = _silu(c_ref[...]).astype(BF16)
    o_ref[0] = _dot(ca, w_ref[0].astype(BF16)) + b_ref[0]


def _ada_mod(c, ada_w, ada_b):
    bsz, d = c.shape
    n = ada_w.shape[-1]
    tn = n // 4
    return pl.pallas_call(
        _ada_kernel,
        out_shape=jax.ShapeDtypeStruct((DEPTH, bsz, n), F32),
        grid=(DEPTH, n // tn),
        in_specs=[
            pl.BlockSpec((bsz, d), lambda l, j: (0, 0)),
            pl.BlockSpec((1, d, tn), lambda l, j: (l, 0, j)),
            pl.BlockSpec((1, 1, tn), lambda l, j: (l, 0, j)),
        ],
        out_specs=pl.BlockSpec((1, bsz, tn), lambda l, j: (l, 0, j)),
        compiler_params=_params("arbitrary", "arbitrary"),
        name="ada_mod",
    )(c, ada_w, ada_b.reshape(DEPTH, 1, n))


def _inproj_kernel(x_ref, mod_ref, w_ref, *o_refs, shift_row, scale_row):
    sh = mod_ref[0, shift_row:shift_row + 1, :]
    sc = mod_ref[0, scale_row:scale_row + 1, :]
    hm = (x_ref[0] * (1.0 + sc) + sh).astype(BF16)
    col = 0
    for o_ref in o_refs:
        n = o_ref.shape[-1]
        o_ref[0] = _dot(hm, w_ref[:, col:col + n]).astype(o_ref.dtype)
        col += n


def _inproj(x, mod, w, outs, tm, shift_row, scale_row, name):
    bsz, s, d = x.shape
    n = w.shape[1]
    assert sum(o[0] for o in outs) == n
    return pl.pallas_call(
        functools.partial(_inproj_kernel, shift_row=shift_row, scale_row=scale_row),
        out_shape=[jax.ShapeDtypeStruct((bsz, s, o[0]), o[1]) for o in outs],
        grid=(bsz, s // tm),
        in_specs=[
            pl.BlockSpec((1, tm, d), lambda b, i: (b, i, 0)),
            pl.BlockSpec((1, 6, d), lambda b, i: (b, 0, 0)),
            pl.BlockSpec((d, n), lambda b, i: (0, 0)),
        ],
        out_specs=[pl.BlockSpec((1, tm, o[0]), lambda b, i: (b, i, 0)) for o in outs],
        compiler_params=_params("parallel", "arbitrary"),
        name=name,
    )(x, mod, w)


def _post_kernel(*refs, n_act):
    act_refs = refs[:n_act]
    wo_refs = refs[n_act:2 * n_act]
    x_ref, mod_ref, w1_ref, w3_ref, w2_ref, g_ref, b_ref, o_ref, gm_ref = refs[2 * n_act:]
    y = None
    for a_ref, w_ref in zip(act_refs, wo_refs):
        t = _dot(a_ref[0], w_ref[...])
        y = t if y is None else y + t
    r = RES_ALPHA * x_ref[0] + (1.0 + mod_ref[0, 2:3, :]) * y
    xm = _layer_norm(r, g_ref[0:1, :], b_ref[0:1, :])
    hf = (xm * (1.0 + mod_ref[0, 4:5, :]) + mod_ref[0, 3:4, :]).astype(BF16)
    for c in range(D_FF // FF_CHUNK):
        cs = slice(c * FF_CHUNK, (c + 1) * FF_CHUNK)
        gm_ref[:, cs] = (_silu(_dot(hf, w1_ref[:, cs])) * _dot(hf, w3_ref[:, cs])).astype(BF16)
    y2 = _dot(gm_ref[...], w2_ref[...])
    r2 = RES_ALPHA * xm + (1.0 + mod_ref[0, 5:6, :]) * y2
    o_ref[0] = _layer_norm(r2, g_ref[1:2, :], b_ref[1:2, :])


def _post(acts, w_outs, x, mod, w1, w3, w2, ln_g, ln_b, tm):
    bsz, s, d = x.shape
    n_act = len(acts)
    const = lambda shape: pl.BlockSpec(shape, lambda b, i: (0,) * len(shape), pipeline_mode=pl.Buffered(1))
    in_specs = [pl.BlockSpec((1, tm, a.shape[-1]), lambda b, i: (b, i, 0)) for a in acts]
    in_specs += [const(w.shape) for w in w_outs]
    in_specs += [
        pl.BlockSpec((1, tm, d), lambda b, i: (b, i, 0)),
        pl.BlockSpec((1, 6, d), lambda b, i: (b, 0, 0)),
        const((d, D_FF)),
        const((d, D_FF)),
        const((D_FF, d)),
        const((2, d)),
        const((2, d)),
    ]
    return pl.pallas_call(
        functools.partial(_post_kernel, n_act=n_act),
        out_shape=jax.ShapeDtypeStruct((bsz, s, d), F32),
        grid=(bsz, s // tm),
        in_specs=in_specs,
        out_specs=pl.BlockSpec((1, tm, d), lambda b, i: (b, i, 0)),
        scratch_shapes=[pltpu.VMEM((tm, D_FF), BF16)],
        compiler_params=_params("parallel", "arbitrary"),
        name="post",
    )(*acts, *w_outs, x, mod, w1.astype(BF16), w3.astype(BF16), w2.astype(BF16), ln_g, ln_b)


def _mixer0_kernel(p_ref, wa_ref, ba_ref, gg_ref, cw_ref, cb_ref, gb_ref, mg_ref, o_ref,
                   st_ref, cst_ref, m_ref, xc_ref):
    L = MIX_TILE
    i = pl.program_id(1)

    @pl.when(i == 0)
    def _():
        st_ref[...] = jnp.zeros_like(st_ref)
        cst_ref[...] = jnp.zeros_like(cst_ref)
        m_ref[...] = jnp.zeros_like(m_ref)
        xc_ref[0:8, :] = jnp.zeros((8, 2 * ML_W), F32)

    lane = _iota((1, LANES), 1)
    g_raw = p_ref[0, :, P0_G:P0_G + LANES]

    u = _dot(g_raw.astype(BF16), wa_ref[...]) + ba_ref[...]
    la = _log_sigmoid(u) * (1.0 / GLA_TAU)
    row = _iota((L, L), 0)
    col = _iota((L, L), 1)
    same_chunk = (row >> _CHUNK_SHIFT) == (col >> _CHUNK_SHIFT)
    tri_blk = jnp.where(same_chunk & (col <= row), 1.0, 0.0).astype(BF16)
    bcs = _dot_split(tri_blk, la, 3)
    q_in = p_ref[0, :, P0_QA:P0_QA + GLA_QK] * (GLA_DK ** -0.5) * jnp.exp(bcs)
    k_raw = p_ref[0, :, P0_KA:P0_KA + GLA_QK]
    k_in = k_raw * jnp.exp(-bcs)
    lane_qk = _iota((1, GLA_QK), 1)
    head_masks = [(lane_qk >> _CHUNK_SHIFT) == h for h in range(GLA_HEADS)]
    r4 = _iota((GLA_HEADS * GLA_CHUNK, GLA_CHUNK), 0)
    c4 = _iota((GLA_HEADS * GLA_CHUNK, GLA_CHUNK), 1)
    tril4 = c4 <= (r4 & (GLA_CHUNK - 1))
    oa_chunks = []
    for c in range(L // GLA_CHUNK):
        r0, r1 = c * GLA_CHUNK, (c + 1) * GLA_CHUNK
        b_c = bcs[r0:r1]
        bl = b_c[GLA_CHUNK - 1:GLA_CHUNK, :]
        q_c = q_in[r0:r1]
        k_c = k_in[r0:r1].astype(BF16)
        k_end = k_raw[r0:r1] * jnp.exp(bl - b_c)
        v_c = p_ref[0, r0:r1, P0_VA:P0_VA + GLA_V]
        q_exp = jnp.concatenate([jnp.where(hm, q_c, 0.0) for hm in head_masks], axis=0).astype(BF16)
        k_exp = jnp.concatenate([jnp.where(hm, k_end, 0.0) for hm in head_masks], axis=0).astype(BF16)
        sc = jnp.where(tril4, _dot_nt(q_exp, k_c), 0.0)
        intra = _dot(sc.astype(BF16), v_c.astype(BF16))
        inter = _dot_nt(q_exp, st_ref[...].astype(BF16))
        o_heads = []
        for h in range(GLA_HEADS):
            h0, h1 = h * GLA_CHUNK, (h + 1) * GLA_CHUNK
            o_heads.append(intra[h0:h1, h * GLA_DV:(h + 1) * GLA_DV] + inter[h0:h1])
        oa_chunks.append(o_heads)
        v_cat = jnp.concatenate([v_c[:, h * GLA_DV:(h + 1) * GLA_DV] for h in range(GLA_HEADS)], axis=0)
        st_ref[...] = st_ref[...] * jnp.exp(bl) + _dot_tn(v_cat.astype(BF16), k_exp)
    for h in range(GLA_HEADS):
        o_h = jnp.concatenate([oc[h] for oc in oa_chunks], axis=0)
        cs = slice(h * GLA_DV, (h + 1) * GLA_DV)
        ra = p_ref[0, :, P0_RA + h * GLA_DV:P0_RA + (h + 1) * GLA_DV]
        o_ref[0, :, cs] = (_head_norm(o_h, gg_ref[:, cs]) * _silu(ra)).astype(o_ref.dtype)

    xc_ref[8:8 + L, :] = p_ref[0, :, P0_QKB:P0_QKB + 2 * ML_W]
    conv = cb_ref[...]
    for kk in range(MLSTM_CONV):
        conv = conv + cw_ref[kk:kk + 1, :] * xc_ref[8 - (MLSTM_CONV - 1) + kk:8 - (MLSTM_CONV - 1) + kk + L, :]
    xc_ref[0:8, :] = xc_ref[L:L + 8, :]
    qk_b = _silu(conv)

    gb = jnp.where(lane < G_A, g_raw + gb_ref[...], 0.0)
    ipre = gb
    logf = _log_sigmoid(pltpu.roll(gb, LANES - G_F, axis=1))
    logf = jnp.where(lane < MLSTM_HEADS, logf, 0.0)
    tri = jnp.where(col <= row, 1.0, 0.0).astype(BF16)
    bcum = _dot_split(tri, logf, 3)
    blast = bcum[L - 1:L, :]
    wend = blast - bcum + ipre
    m_prev = m_ref[...]
    m_new = jnp.maximum(blast + m_prev, jnp.max(wend, axis=0, keepdims=True))
    scl = jnp.exp(blast + m_prev - m_new)
    wj = jnp.exp(wend - m_new)
    rows_src = jnp.where(lane < MLSTM_HEADS, ipre, pltpu.roll(bcum, MLSTM_HEADS, axis=1))
    rows_t = rows_src.T
    causal = col <= row
    ones = jnp.ones((L, MLSTM_DH), F32)
    for h in range(MLSTM_HEADS):
        hs = slice(h * MLSTM_DH, (h + 1) * MLSTM_DH)
        b_col = bcum[:, h:h + 1]
        ip_row = rows_t[h:h + 1, :]
        b_row = rows_t[MLSTM_HEADS + h:MLSTM_HEADS + h + 1, :]
        dlog = jnp.where(causal, b_col - b_row + ip_row, NEG_INF)
        il = b_col + m_prev[:, h:h + 1]
        m_i = jnp.maximum(il, jnp.max(dlog, axis=1, keepdims=True))
        w_intra = jnp.exp(dlog - m_i)
        s_inter = jnp.exp(il - m_i)
        q_h = qk_b[:, h * MLSTM_DH:(h + 1) * MLSTM_DH].astype(BF16)
        k_h = (qk_b[:, ML_W + h * MLSTM_DH:ML_W + (h + 1) * MLSTM_DH] * (MLSTM_DH ** -0.5)).astype(BF16)
        v_h = p_ref[0, :, P0_VB + h * MLSTM_DH:P0_VB + (h + 1) * MLSTM_DH]
        v_aug = jnp.concatenate([v_h, ones], axis=1)
        a = (w_intra * _dot_nt(q_h, k_h)).astype(BF16)
        c_prev = cst_ref[h]
        tot = _dot(a, v_aug.astype(BF16)) + s_inter * _dot_nt(q_h, c_prev.astype(BF16))
        num = tot[:, :MLSTM_DH]
        den = tot[:, MLSTM_DH:]
        hid = num / jnp.maximum(jnp.abs(den), jnp.exp(-m_i))
        w_aug = (v_aug * wj[:, h:h + 1]).astype(BF16)
        cst_ref[h] = scl[:, h:h + 1] * c_prev + _dot_tn(w_aug, k_h)
        ob = p_ref[0, :, P0_OB + h * MLSTM_DH:P0_OB + (h + 1) * MLSTM_DH]
        o_ref[0, :, GLA_V + h * MLSTM_DH:GLA_V + (h + 1) * MLSTM_DH] = (
            jax.nn.sigmoid(ob) * _head_norm(hid, mg_ref[:, hs])).astype(o_ref.dtype)
    m_ref[...] = m_new


def _mixer0(proj, wa_up, ba, gla_g, conv_w, conv_b, b_i, b_f, ml_g):
    bsz, s, n = proj.shape
    L = MIX_TILE
    wa_pad = jnp.zeros((LANES, GLA_QK), F32).at[G_A:G_A + GLA_GATE_RANK].set(wa_up).astype(BF16)
    gbias = jnp.zeros((1, LANES), F32).at[0, G_I:G_I + MLSTM_HEADS].set(b_i).at[0, G_F:G_F + MLSTM_HEADS].set(b_f)
    full = lambda shape: pl.BlockSpec(shape, lambda b, i: (0,) * len(shape))
    return pl.pallas_call(
        _mixer0_kernel,
        out_shape=jax.ShapeDtypeStruct((bsz, s, GLA_V + ML_W), BF16),
        grid=(bsz, s // L),
        in_specs=[
            pl.BlockSpec((1, L, n), lambda b, i: (b, i, 0)),
            full((LANES, GLA_QK)),
            full((1, GLA_QK)),
            full((1, GLA_V)),
            full((MLSTM_CONV, 2 * ML_W)),
            full((1, 2 * ML_W)),
            full((1, LANES)),
            full((1, ML_W)),
        ],
        out_specs=pl.BlockSpec((1, L, GLA_V + ML_W), lambda b, i: (b, i, 0)),
        scratch_shapes=[
            pltpu.VMEM((GLA_DV, GLA_QK), F32),
            pltpu.VMEM((MLSTM_HEADS, 2 * MLSTM_DH, MLSTM_DH), F32),
            pltpu.VMEM((1, LANES), F32),
            pltpu.VMEM((8 + L, 2 * ML_W), F32),
        ],
        compiler_params=_params("parallel", "arbitrary"),
        name="mixer0",
    )(proj, wa_pad, ba.reshape(1, GLA_QK), gla_g.reshape(1, GLA_V), conv_w, conv_b.reshape(1, 2 * ML_W),
      gbias, ml_g.reshape(1, ML_W))


def _sb_kernel(q_ref, k_ref, v_ref, o_ref, qs_ref, acc_ref, car_ref, mar_ref, kn_ref):
    TQ, SUB, WIDE = SB_TQ, SB_SUB, SB_WIDE
    ns = TQ // SUB
    RB = 2 * SUB
    i = pl.program_id(2)
    lane = _iota((1, LANES), 1)
    head0 = (lane >> _HEAD_SHIFT) == 0

    @pl.when(i == 0)
    def _():
        k2 = jnp.square(k_ref[0].astype(F32))
        for h in range(2):
            n2 = jnp.sum(jnp.where(head0 if h == 0 else ~head0, k2, 0.0), axis=1, keepdims=True)
            kn_ref[h:h + 1, :] = jnp.broadcast_to(jnp.sqrt(jnp.max(n2, axis=0, keepdims=True)), (1, LANES))

    for a in range(ns):
        qa = q_ref[0, a * SUB:(a + 1) * SUB, :].astype(F32) * (SB_DH ** -0.5 * LOG2E)
        for h in range(2):
            rows = slice(a * RB + h * SUB, a * RB + (h + 1) * SUB)
            qh = jnp.where(head0 if h == 0 else ~head0, qa, 0.0).astype(BF16)
            qs_ref[rows, :] = qh
            qn = jnp.sqrt(jnp.sum(jnp.square(qh.astype(F32)), axis=1, keepdims=True))
            mar_ref[rows, :] = qn * kn_ref[h:h + 1, :] * SB_BOUND_SLACK + SB_ZERO_BITS
    tri = jnp.where(_iota((WIDE, WIDE), 0) >= _iota((WIDE, WIDE), 1), -1.0, 0.0).astype(BF16)

    def key_rows(ku, nk):
        r0 = ku * SUB
        return pl.ds(r0 if isinstance(r0, int) else pl.multiple_of(r0, SUB), nk)

    def logits(a, ku, nk):
        return _dot_nt(qs_ref[a * RB:(a + 1) * RB, :], k_ref[0, key_rows(ku, nk), :])

    def softplus2(z, diagonal):
        sp = jnp.maximum(z, 0.0) + jnp.log(1.0 + jnp.exp2(-jnp.abs(z))) * LOG2E
        if not diagonal:
            return sp, None
        mask = _iota(z.shape, 1) < (_iota(z.shape, 0) & (SUB - 1))
        return jnp.where(mask, sp, 0.0), mask

    def suffix(sp):
        nk = sp.shape[1]
        return _dot(sp.astype(BF16), tri[:nk, :nk]), jnp.sum(sp, axis=1, keepdims=True)

    def weights(z, res, car, mask):
        att = jnp.exp2(z + res if car is None else z + res - car)
        if mask is not None:
            att = jnp.where(mask, att, 0.0)
        return att.astype(BF16)

    def far_tile(a, ku):
        rows = slice(a * RB, (a + 1) * RB)
        z = logits(a, ku, SUB)
        sp, _ = softplus2(z, False)
        res, tot = suffix(sp)
        car = car_ref[rows, :]
        acc_ref[rows, :] += _dot(weights(z, res, car, None), v_ref[0, key_rows(ku, SUB), :])
        car_ref[rows, :] = car + tot

    def near_tiles(first_block):
        chains = []
        for a in range(ns):
            g = a if first_block else ns * i + a
            chain = [(a, g, SUB, True)]
            if not first_block or a >= 2:
                chain.append((a, g - 2, WIDE, False))
            elif a == 1:
                chain.append((a, 0, SUB, False))
            chains.append(chain)
        jobs = [job for step in itertools.zip_longest(*chains) for job in step if job is not None]
        zs = [logits(a, ku, nk) for a, ku, nk, _ in jobs]
        sps = [softplus2(z, diagonal) for z, (_, _, _, diagonal) in zip(zs, jobs)]
        sufs = [suffix(sp) for sp, _ in sps]
        car = [None] * ns
        atts = []
        for z, (sp, mask), (res, tot), (a, _, _, _) in zip(zs, sps, sufs, jobs):
            atts.append(weights(z, res, car[a], mask))
            car[a] = tot if car[a] is None else car[a] + tot
        acc = [None] * ns
        for att, (a, ku, nk, _) in zip(atts, jobs):
            t = _dot(att, v_ref[0, key_rows(ku, nk), :])
            acc[a] = t if acc[a] is None else acc[a] + t
        for a in range(ns):
            acc_ref[a * RB:(a + 1) * RB, :] = acc[a]
            car_ref[a * RB:(a + 1) * RB, :] = jnp.broadcast_to(car[a], (RB, LANES))

    pl.when(i == 0)(functools.partial(near_tiles, True))
    pl.when(i > 0)(functools.partial(near_tiles, False))

    def next_unit(a, t):
        return ns * i + a - 3 - t

    def pending(t):
        gap = [jnp.where(next_unit(a, t) >= 0, mar_ref[a * RB:(a + 1) * RB, :] - car_ref[a * RB:(a + 1) * RB, :], -1.0)
               for a in range(ns)]
        return jnp.max(functools.reduce(jnp.maximum, gap)) > 0.0

    def far_body(c):
        t, _ = c
        for a in range(ns):
            ku = next_unit(a, t)
            pl.when(ku >= 0)(functools.partial(far_tile, a, ku))
        return t + 1, pending(t + 1)

    lax.while_loop(lambda c: c[1], far_body, (0, pending(0)))

    for a in range(ns):
        o_ref[0, a * SUB:(a + 1) * SUB, :] = jnp.where(
            head0, acc_ref[a * RB:a * RB + SUB, :], acc_ref[a * RB + SUB:(a + 1) * RB, :]).astype(o_ref.dtype)


def _stick_breaking(pc):
    bsz, s, _ = pc.shape
    npair = SB_W // LANES
    return pl.pallas_call(
        _sb_kernel,
        out_shape=jax.ShapeDtypeStruct((bsz, s, SB_W), BF16),
        grid=(bsz, npair, s // SB_TQ),
        in_specs=[
            pl.BlockSpec((1, SB_TQ, LANES), lambda b, p, i: (b, i, p)),
            pl.BlockSpec((1, s, LANES), lambda b, p, i: (b, 0, npair + p)),
            pl.BlockSpec((1, s, LANES), lambda b, p, i: (b, 0, 2 * npair + p)),
        ],
        out_specs=pl.BlockSpec((1, SB_TQ, LANES), lambda b, p, i: (b, i, p)),
        scratch_shapes=[
            pltpu.VMEM((2 * SB_TQ, LANES), BF16),
            pltpu.VMEM((2 * SB_TQ, LANES), F32),
            pltpu.VMEM((2 * SB_TQ, LANES), F32),
            pltpu.VMEM((2 * SB_TQ, LANES), F32),
            pltpu.VMEM((8, LANES), F32),
        ],
        compiler_params=_params("parallel", "parallel", "arbitrary"),
        name="stick_breaking",
    )(pc, pc, pc)


def _t5_bucket(dist):
    max_exact = N_REL_BUCKETS // 2
    dd = np.maximum(dist, 1).astype(np.float64)
    large = max_exact + (np.log(dd / max_exact) / np.log(REL_MAX_DIST / max_exact)
                         * (N_REL_BUCKETS - max_exact)).astype(np.int32)
    large = np.minimum(large, N_REL_BUCKETS - 1)
    return np.where(dist < max_exact, dist, large).astype(np.int32)


def _dil_buckets():
    qi = np.arange(DIL_BLK)[:, None]
    kj = np.arange(2 * DIL_BLK)[None, :]
    delta = qi - kj + DIL_BLK
    in_win = (delta >= 0) & (delta <= DIL_BLK)
    tabs = []
    for window, dil in DIL_PAIRS:
        assert window // dil == DIL_BLK
        bucket = _t5_bucket(np.clip(delta, 0, None) * dil)
        tabs.append(np.where(in_win, bucket, -1).astype(np.int32))
    return np.stack(tabs, 0)


def _dil_kernel(q_ref, k_ref, v_ref, bkt_ref, bkt0_ref, tab_ref, o_ref, m_sc, l_sc, acc_sc):
    s_len = q_ref.shape[1]
    hp = pl.program_id(1)
    g = pl.program_id(2)
    blk = DIL_BLK
    ng = len(DIL_PAIRS)
    lane = _iota((1, LANES), 1)
    lane_head = lane >> _HEAD_SHIFT
    sel0 = lane_head == 0
    kcol = _iota((2 * blk, 2 * blk), 1)

    def head_biases(gi, bkt):
        biases = []
        for hl in range(2):
            head = gi * DIL_HEADS_PER_GROUP + hp * 2 + hl
            bias = jnp.full(bkt.shape, NEG_INF, F32)
            for bk in range(N_REL_BUCKETS):
                bias = jnp.where(bkt == bk, tab_ref[bk, head], bias)
            biases.append(bias)
        return jnp.concatenate(biases, axis=0)

    def stack_heads(qb):
        return jnp.concatenate([jnp.where(sel0, qb, 0.0), jnp.where(sel0, 0.0, qb)], axis=0).astype(BF16)

    def with_ones(vv):
        return jnp.concatenate([vv, jnp.ones_like(vv)], axis=1).astype(BF16)

    def first_group(gi, dil):
        n_sub = s_len // dil
        bias2 = head_biases(gi, bkt0_ref[...])

        def residue(r):
            rows = pl.ds(r, n_sub, stride=dil)
            logits = _dot_nt(stack_heads(q_ref[0, rows, :] * (DIL_DH ** -0.5)), k_ref[0, rows, :].astype(BF16)) + bias2
            return rows, logits

        def body(tt, carry):
            s1 = [residue(tt * DIL_UNROLL + u) for u in range(DIL_UNROLL)]
            s2 = [jnp.max(lg, axis=1, keepdims=True) for _, lg in s1]
            s3 = [_dot(jnp.exp(lg - mc).astype(BF16), with_ones(v_ref[0, rows, :])) for (rows, lg), mc in zip(s1, s2)]
            for (rows, _), mc, pv in zip(s1, s2, s3):
                m_sc[rows, :] = jnp.where(sel0, mc[0:n_sub], mc[n_sub:])
                l_sc[rows, :] = jnp.where(sel0, pv[0:n_sub, LANES:], pv[n_sub:, LANES:])
                acc_sc[rows, :] = jnp.where(sel0, pv[0:n_sub, :LANES], pv[n_sub:, :LANES])
            return carry

        lax.fori_loop(0, dil // DIL_UNROLL, body, 0)

    def group(gi, dil):
        nb = s_len // (blk * dil)
        bias2 = head_biases(gi, bkt_ref[gi])

        def stage_logits(t):
            r = t // nb
            n = t % nb
            q_start = r + dil * blk * n
            p_start = r + dil * blk * jnp.maximum(n - 1, 0)
            rows_q = pl.ds(q_start, blk, stride=dil) if dil > 1 else pl.ds(q_start, blk)
            rows_p = pl.ds(p_start, blk, stride=dil) if dil > 1 else pl.ds(p_start, blk)
            q2 = stack_heads(q_ref[0, rows_q, :] * (DIL_DH ** -0.5))
            kk = jnp.concatenate([k_ref[0, rows_p, :], k_ref[0, rows_q, :]], axis=0).astype(BF16)
            logits = _dot_nt(q2, kk) + bias2
            logits = jnp.where(jnp.logical_and(n == 0, kcol < blk), NEG_INF, logits)
            return rows_q, rows_p, logits

        def stage_pv(rows_q, rows_p, logits, m_col):
            p = jnp.exp(logits - m_col)
            vv = with_ones(jnp.concatenate([v_ref[0, rows_p, :], v_ref[0, rows_q, :]], axis=0))
            pv = _dot(p.astype(BF16), vv)
            return pv[:, LANES:], pv[:, :LANES]

        def body(tt, carry):
            ts = [tt * DIL_UNROLL + u for u in range(DIL_UNROLL)]
            s1 = [stage_logits(t) for t in ts]
            s2 = [jnp.max(lg, axis=1, keepdims=True) for _, _, lg in s1]
            s3 = [stage_pv(rq, rp, lg, mc) for (rq, rp, lg), mc in zip(s1, s2)]
            outs = []
            for (rows_q, _, _), m_col, (ps, pv) in zip(s1, s2, s3):
                m_old = m_sc[rows_q, :]
                m_blk = jnp.where(sel0, m_col[0:blk], m_col[blk:])
                m_new = jnp.maximum(m_old, m_blk)
                a_old = jnp.exp(m_old - m_new)
                a_blk = jnp.exp(m_blk - m_new)
                l_new = a_old * l_sc[rows_q, :] + a_blk * jnp.where(sel0, ps[0:blk], ps[blk:])
                a_new = a_old * acc_sc[rows_q, :] + a_blk * jnp.where(sel0, pv[0:blk], pv[blk:])
                outs.append((rows_q, m_new, l_new, a_new))
            for rows_q, m_new, l_new, a_new in outs:
                m_sc[rows_q, :] = m_new
                l_sc[rows_q, :] = l_new
                acc_sc[rows_q, :] = a_new
            return carry

        lax.fori_loop(0, s_len // (blk * DIL_UNROLL), body, 0)

    for gi, (_, dil) in enumerate(DIL_PAIRS):
        pl.when(g == ng - 1 - gi)(functools.partial(first_group if gi == ng - 1 else group, gi, dil))

    @pl.when(g == ng - 1)
    def _():
        o_ref[0] = (acc_sc[...] / l_sc[...]).astype(o_ref.dtype)


def _dilated(pd, rel_bias):
    bsz, s, _ = pd.shape
    ng = len(DIL_PAIRS)
    gw = DIL_HEADS_PER_GROUP * DIL_DH // LANES
    nq = DIL_W // LANES
    bkt = jnp.asarray(_dil_buckets())
    dil0 = DIL_PAIRS[-1][1]
    n_sub = s // dil0
    assert s % (DIL_BLK * dil0) == 0 and dil0 % DIL_UNROLL == 0
    delta = np.arange(n_sub)[:, None] - np.arange(n_sub)[None, :]
    bkt0 = jnp.asarray(np.where((delta >= 0) & (delta <= DIL_BLK), _t5_bucket(np.clip(delta, 0, None) * dil0), -1)
                       .astype(np.int32))
    blk_spec = lambda base: pl.BlockSpec((1, s, LANES), lambda b, p, g: (b, 0, base + (ng - 1 - g) * gw + p))
    return pl.pallas_call(
        _dil_kernel,
        out_shape=jax.ShapeDtypeStruct((bsz, s, DIL_OUT), BF16),
        grid=(bsz, gw, ng),
        in_specs=[
            blk_spec(0),
            blk_spec(nq),
            blk_spec(2 * nq),
            pl.BlockSpec((ng, DIL_BLK, 2 * DIL_BLK), lambda b, p, g: (0, 0, 0)),
            pl.BlockSpec((n_sub, n_sub), lambda b, p, g: (0, 0)),
            pl.BlockSpec(memory_space=pltpu.SMEM),
        ],
        out_specs=pl.BlockSpec((1, s, LANES), lambda b, p, g: (b, 0, p)),
        scratch_shapes=[pltpu.VMEM((s, LANES), F32)] * 3,
        compiler_params=_params("parallel", "parallel", "arbitrary"),
        name="dilated",
    )(pd, pd, pd, bkt, bkt0, rel_bias)


def _ab_weight(w):
    o = np.cumsum((0,) + (GLA_QK, GLA_QK, GLA_V, GLA_V, GLA_GATE_RANK, 2 * ML_W, ML_W, MLSTM_HEADS, MLSTM_HEADS, ML_W))
    qa, ka, va, ra, aa, qkb, vb, ib, fb, ob = [w[:, o[j]:o[j + 1]] for j in range(10)]
    pad = jnp.zeros((w.shape[0], LANES - 2 * MLSTM_HEADS - GLA_GATE_RANK), w.dtype)
    return jnp.concatenate([qa, ka, va, ra, qkb, vb, ob, ib, fb, aa, pad], axis=1).astype(BF16)


def kernel(x, c, ada_w, ada_b, ln_g, ln_b, ab_w_in, gla_wa_up, gla_ba, gla_norm_g, ml_conv_w, ml_conv_b,
           ml_b_i, ml_b_f, ml_norm_g, ab_w_out, cd_w_in, rel_bias, cd_w_out, ffn_w1, ffn_w3, ffn_w2):
    bsz, s, d = x.shape
    mod_all = _ada_mod(c, ada_w, ada_b).reshape(DEPTH, bsz, 6, d)
    tm = 512
    for layer in range(DEPTH):
        mod = mod_all[layer]
        j = layer // 2
        if layer % 2 == 0:
            (proj,) = _inproj(x, mod, _ab_weight(ab_w_in[j]), [(P0_N, F32)], tm, 0, 1, "inproj0")
            cat = _mixer0(proj, gla_wa_up[j], gla_ba[j], gla_norm_g[j], ml_conv_w[j], ml_conv_b[j],
                          ml_b_i[j], ml_b_f[j], ml_norm_g[j])
            acts, w_outs = [cat], [ab_w_out[j].astype(BF16)]
        else:
            pc, pd = _inproj(x, mod, cd_w_in[j].astype(BF16), [(3 * SB_W, BF16), (3 * DIL_W, F32)],
                             tm, 0, 1, "inproj1")
            oc = _stick_breaking(pc)
            od = _dilated(pd, rel_bias)
            w_out = cd_w_out[j].astype(BF16)
            acts, w_outs = [oc, od], [w_out[:SB_W], w_out[SB_W:]]
        x = _post(acts, w_outs, x, mod, ffn_w1[layer], ffn_w3[layer], ffn_w2[layer], ln_g[layer], ln_b[layer], FFN_TM)
    return x
```

```python
import functools
import itertools

import numpy as np
import jax
import jax.numpy as jnp
from jax import lax
from jax.experimental import pallas as pl
from jax.experimental.pallas import tpu as pltpu

F32 = jnp.float32
BF16 = jnp.bfloat16

D_MODEL = 1024
DEPTH = 2
GLA_HEADS = 4
GLA_DK = 64
GLA_DV = 128
GLA_GATE_RANK = 16
GLA_TAU = 16.0
GLA_CHUNK = 64
MLSTM_HEADS = 4
MLSTM_DH = 128
MLSTM_CONV = 4
SB_HEADS = 8
SB_DH = 64
DIL_PAIRS = ((128, 1), (512, 4), (2048, 16))
DIL_HEADS_PER_GROUP = 4
DIL_DH = 64
DIL_BLK = 128
DIL_UNROLL = 4
N_REL_BUCKETS = 32
REL_MAX_DIST = 2048
D_FF = ((8 * D_MODEL + 3 * 256 - 1) // (3 * 256)) * 256
LN_EPS = 1e-5
RES_ALPHA = (2 * DEPTH) ** 0.25
NEG_INF = -1e30

GLA_QK = GLA_HEADS * GLA_DK
GLA_V = GLA_HEADS * GLA_DV
ML_W = MLSTM_HEADS * MLSTM_DH
SB_W = SB_HEADS * SB_DH
DIL_W = len(DIL_PAIRS) * DIL_HEADS_PER_GROUP * DIL_DH
DIL_OUT = DIL_HEADS_PER_GROUP * DIL_DH

LANES = 128
VMEM_LIMIT = 56 * 1024 * 1024

P0_QA = 0
P0_KA = P0_QA + GLA_QK
P0_VA = P0_KA + GLA_QK
P0_RA = P0_VA + GLA_V
P0_QKB = P0_RA + GLA_V
P0_VB = P0_QKB + 2 * ML_W
P0_OB = P0_VB + ML_W
P0_G = P0_OB + ML_W
P0_N = P0_G + LANES
G_I = 0
G_F = MLSTM_HEADS
G_A = 2 * MLSTM_HEADS

MIX_TILE = 256
FF_CHUNK = 256
CAST_ROWS = 512
FFN_TM = 512
SB_TQ = 1024
SB_SUB = 128
SB_WIDE = 256
SB_ZERO_BITS = 160.0
SB_BOUND_SLACK = 1.01
LOG2E = 1.4426950408889634
_CHUNK_SHIFT = 6
_HEAD_SHIFT = 6
assert GLA_CHUNK == GLA_DK == 1 << _CHUNK_SHIFT and SB_DH == DIL_DH == 1 << _HEAD_SHIFT


def _dot(a, b):
    return jnp.dot(a, b, preferred_element_type=F32)


def _dot_nt(a, b):
    return lax.dot_general(a, b, (((1,), (1,)), ((), ())), preferred_element_type=F32)


def _dot_tn(a, b):
    return lax.dot_general(a, b, (((0,), (0,)), ((), ())), preferred_element_type=F32)


def _dot_split(t, x, terms):
    acc = None
    rem = x
    for i in range(terms):
        part = rem.astype(BF16)
        d = _dot(t, part)
        acc = d if acc is None else acc + d
        if i + 1 < terms:
            rem = rem - part.astype(F32)
    return acc


def _log_sigmoid(x):
    return jnp.minimum(x, 0.0) - jnp.log(1.0 + jnp.exp(-jnp.abs(x)))


def _silu(x):
    return x * jax.nn.sigmoid(x)


def _layer_norm(r, g, b):
    mu = jnp.mean(r, axis=-1, keepdims=True)
    d = r - mu
    var = jnp.mean(d * d, axis=-1, keepdims=True)
    return d * lax.rsqrt(var + LN_EPS) * g + b


def _head_norm(x, g):
    mu = jnp.mean(x, axis=-1, keepdims=True)
    d = x - mu
    var = jnp.mean(d * d, axis=-1, keepdims=True)
    return d * lax.rsqrt(var + LN_EPS) * g


def _iota(shape, dim):
    return lax.broadcasted_iota(jnp.int32, shape, dim)


def _params(*sem):
    return pltpu.CompilerParams(dimension_semantics=sem, vmem_limit_bytes=VMEM_LIMIT)


def _cast_kernel(x_ref, o_ref):
    o_ref[...] = x_ref[...].astype(o_ref.dtype)


def _to_bf16(w):
    w2 = w.reshape(-1, w.shape[-1])
    r, c = w2.shape
    tr = CAST_ROWS if r % CAST_ROWS == 0 else r
    out = pl.pallas_call(
        _cast_kernel,
        out_shape=jax.ShapeDtypeStruct((r, c), BF16),
        grid=(r // tr,),
        in_specs=[pl.BlockSpec((tr, c), lambda i: (i, 0))],
        out_specs=pl.BlockSpec((tr, c), lambda i: (i, 0)),
        compiler_params=_params("parallel"),
        name="to_bf16",
    )(w2)
    return out.reshape(w.shape)


def _ada_kernel(c_ref, w_ref, b_ref, o_ref):
    ca = _silu(c_ref[...]).astype(BF16)
    o_ref[0] = _dot(ca, w_ref[0].astype(BF16)) + b_ref[0]


def _ada_mod(c, ada_w, ada_b):
    bsz, d = c.shape
    n = ada_w.shape[-1]
    tn = n // 4
    return pl.pallas_call(
        _ada_kernel,
        out_shape=jax.ShapeDtypeStruct((DEPTH, bsz, n), F32),
        grid=(DEPTH, n // tn),
        in_specs=[
            pl.BlockSpec((bsz, d), lambda l, j: (0, 0)),
            pl.BlockSpec((1, d, tn), lambda l, j: (l, 0, j)),
            pl.BlockSpec((1, 1, tn), lambda l, j: (l, 0, j)),
        ],
        out_specs=pl.BlockSpec((1, bsz, tn), lambda l, j: (l, 0, j)),
        compiler_params=_params("arbitrary", "arbitrary"),
        name="ada_mod",
    )(c, ada_w, ada_b.reshape(DEPTH, 1, n))


def _inproj_kernel(x_ref, mod_ref, w_ref, *o_refs, shift_row, scale_row):
    sh = mod_ref[0, shift_row:shift_row + 1, :]
    sc = mod_ref[0, scale_row:scale_row + 1, :]
    hm = (x_ref[0] * (1.0 + sc) + sh).astype(BF16)
    col = 0
    for o_ref in o_refs:
        n = o_ref.shape[-1]
        o_ref[0] = _dot(hm, w_ref[:, col:col + n]).astype(o_ref.dtype)
        col += n


def _inproj(x, mod, w, outs, tm, shift_row, scale_row, name):
    bsz, s, d = x.shape
    n = w.shape[1]
    assert sum(o[0] for o in outs) == n
    return pl.pallas_call(
        functools.partial(_inproj_kernel, shift_row=shift_row, scale_row=scale_row),
        out_shape=[jax.ShapeDtypeStruct((bsz, s, o[0]), o[1]) for o in outs],
        grid=(bsz, s // tm),
        in_specs=[
            pl.BlockSpec((1, tm, d), lambda b, i: (b, i, 0)),
            pl.BlockSpec((1, 6, d), lambda b, i: (b, 0, 0)),
            pl.BlockSpec((d, n), lambda b, i: (0, 0)),
        ],
        out_specs=[pl.BlockSpec((1, tm, o[0]), lambda b, i: (b, i, 0)) for o in outs],
        compiler_params=_params("parallel", "arbitrary"),
        name=name,
    )(x, mod, w)


def _post_kernel(*refs, n_act):
    act_refs = refs[:n_act]
    wo_refs = refs[n_act:2 * n_act]
    x_ref, mod_ref, w1_ref, w3_ref, w2_ref, g_ref, b_ref, o_ref, gm_ref = refs[2 * n_act:]
    y = None
    for a_ref, w_ref in zip(act_refs, wo_refs):
        t = _dot(a_ref[0], w_ref[...])
        y = t if y is None else y + t
    r = RES_ALPHA * x_ref[0] + (1.0 + mod_ref[0, 2:3, :]) * y
    xm = _layer_norm(r, g_ref[0:1, :], b_ref[0:1, :])
    hf = (xm * (1.0 + mod_ref[0, 4:5, :]) + mod_ref[0, 3:4, :]).astype(BF16)
    for c in range(D_FF // FF_CHUNK):
        cs = slice(c * FF_CHUNK, (c + 1) * FF_CHUNK)
        gm_ref[:, cs] = (_silu(_dot(hf, w1_ref[:, cs])) * _dot(hf, w3_ref[:, cs])).astype(BF16)
    y2 = _dot(gm_ref[...], w2_ref[...])
    r2 = RES_ALPHA * xm + (1.0 + mod_ref[0, 5:6, :]) * y2
    o_ref[0] = _layer_norm(r2, g_ref[1:2, :], b_ref[1:2, :])


def _post(acts, w_outs, x, mod, w1, w3, w2, layer, ln_g, ln_b, tm):
    bsz, s, d = x.shape
    n_act = len(acts)
    const = lambda shape: pl.BlockSpec(shape, lambda b, i: (0,) * len(shape), pipeline_mode=pl.Buffered(1))
    of_layer = lambda shape: pl.BlockSpec((None,) + shape, lambda b, i: (layer, 0, 0), pipeline_mode=pl.Buffered(1))
    in_specs = [pl.BlockSpec((1, tm, a.shape[-1]), lambda b, i: (b, i, 0)) for a in acts]
    in_specs += [const(w.shape) for w in w_outs]
    in_specs += [
        pl.BlockSpec((1, tm, d), lambda b, i: (b, i, 0)),
        pl.BlockSpec((1, 6, d), lambda b, i: (b, 0, 0)),
        of_layer((d, D_FF)),
        of_layer((d, D_FF)),
        of_layer((D_FF, d)),
        const((2, d)),
        const((2, d)),
    ]
    return pl.pallas_call(
        functools.partial(_post_kernel, n_act=n_act),
        out_shape=jax.ShapeDtypeStruct((bsz, s, d), F32),
        grid=(bsz, s // tm),
        in_specs=in_specs,
        out_specs=pl.BlockSpec((1, tm, d), lambda b, i: (b, i, 0)),
        scratch_shapes=[pltpu.VMEM((tm, D_FF), BF16)],
        compiler_params=_params("parallel", "arbitrary"),
        name="post",
    )(*acts, *w_outs, x, mod, w1, w3, w2, ln_g, ln_b)


def _mixer0_kernel(p_ref, wa_ref, ba_ref, gg_ref, cw_ref, cb_ref, gb_ref, mg_ref, o_ref,
                   st_ref, cst_ref, m_ref, xc_ref):
    L = MIX_TILE
    i = pl.program_id(1)

    @pl.when(i == 0)
    def _():
        st_ref[...] = jnp.zeros_like(st_ref)
        cst_ref[...] = jnp.zeros_like(cst_ref)
        m_ref[...] = jnp.zeros_like(m_ref)
        xc_ref[0:8, :] = jnp.zeros((8, 2 * ML_W), F32)

    lane = _iota((1, LANES), 1)
    g_raw = p_ref[0, :, P0_G:P0_G + LANES]

    u = _dot(g_raw.astype(BF16), wa_ref[...]) + ba_ref[...]
    la = _log_sigmoid(u) * (1.0 / GLA_TAU)
    row = _iota((L, L), 0)
    col = _iota((L, L), 1)
    same_chunk = (row >> _CHUNK_SHIFT) == (col >> _CHUNK_SHIFT)
    tri_blk = jnp.where(same_chunk & (col <= row), 1.0, 0.0).astype(BF16)
    bcs = _dot_split(tri_blk, la, 3)
    q_in = p_ref[0, :, P0_QA:P0_QA + GLA_QK] * (GLA_DK ** -0.5) * jnp.exp(bcs)
    k_raw = p_ref[0, :, P0_KA:P0_KA + GLA_QK]
    k_in = k_raw * jnp.exp(-bcs)
    lane_qk = _iota((1, GLA_QK), 1)
    head_masks = [(lane_qk >> _CHUNK_SHIFT) == h for h in range(GLA_HEADS)]
    r4 = _iota((GLA_HEADS * GLA_CHUNK, GLA_CHUNK), 0)
    c4 = _iota((GLA_HEADS * GLA_CHUNK, GLA_CHUNK), 1)
    tril4 = c4 <= (r4 & (GLA_CHUNK - 1))
    oa_chunks = []
    for c in range(L // GLA_CHUNK):
        r0, r1 = c * GLA_CHUNK, (c + 1) * GLA_CHUNK
        b_c = bcs[r0:r1]
        bl = b_c[GLA_CHUNK - 1:GLA_CHUNK, :]
        q_c = q_in[r0:r1]
        k_c = k_in[r0:r1].astype(BF16)
        k_end = k_raw[r0:r1] * jnp.exp(bl - b_c)
        v_c = p_ref[0, r0:r1, P0_VA:P0_VA + GLA_V]
        q_exp = jnp.concatenate([jnp.where(hm, q_c, 0.0) for hm in head_masks], axis=0).astype(BF16)
        k_exp = jnp.concatenate([jnp.where(hm, k_end, 0.0) for hm in head_masks], axis=0).astype(BF16)
        sc = jnp.where(tril4, _dot_nt(q_exp, k_c), 0.0)
        intra = _dot(sc.astype(BF16), v_c.astype(BF16))
        inter = _dot_nt(q_exp, st_ref[...].astype(BF16))
        o_heads = []
        for h in range(GLA_HEADS):
            h0, h1 = h * GLA_CHUNK, (h + 1) * GLA_CHUNK
            o_heads.append(intra[h0:h1, h * GLA_DV:(h + 1) * GLA_DV] + inter[h0:h1])
        oa_chunks.append(o_heads)
        v_cat = jnp.concatenate([v_c[:, h * GLA_DV:(h + 1) * GLA_DV] for h in range(GLA_HEADS)], axis=0)
        st_ref[...] = st_ref[...] * jnp.exp(bl) + _dot_tn(v_cat.astype(BF16), k_exp)
    for h in range(GLA_HEADS):
        o_h = jnp.concatenate([oc[h] for oc in oa_chunks], axis=0)
        cs = slice(h * GLA_DV, (h + 1) * GLA_DV)
        ra = p_ref[0, :, P0_RA + h * GLA_DV:P0_RA + (h + 1) * GLA_DV]
        o_ref[0, :, cs] = (_head_norm(o_h, gg_ref[:, cs]) * _silu(ra)).astype(o_ref.dtype)

    xc_ref[8:8 + L, :] = p_ref[0, :, P0_QKB:P0_QKB + 2 * ML_W]
    conv = cb_ref[...]
    for kk in range(MLSTM_CONV):
        conv = conv + cw_ref[kk:kk + 1, :] * xc_ref[8 - (MLSTM_CONV - 1) + kk:8 - (MLSTM_CONV - 1) + kk + L, :]
    xc_ref[0:8, :] = xc_ref[L:L + 8, :]
    qk_b = _silu(conv)

    gb = jnp.where(lane < G_A, g_raw + gb_ref[...], 0.0)
    ipre = gb
    logf = _log_sigmoid(pltpu.roll(gb, LANES - G_F, axis=1))
    logf = jnp.where(lane < MLSTM_HEADS, logf, 0.0)
    tri = jnp.where(col <= row, 1.0, 0.0).astype(BF16)
    bcum = _dot_split(tri, logf, 3)
    blast = bcum[L - 1:L, :]
    wend = blast - bcum + ipre
    m_prev = m_ref[...]
    m_new = jnp.maximum(blast + m_prev, jnp.max(wend, axis=0, keepdims=True))
    scl = jnp.exp(blast + m_prev - m_new)
    wj = jnp.exp(wend - m_new)
    rows_src = jnp.where(lane < MLSTM_HEADS, ipre, pltpu.roll(bcum, MLSTM_HEADS, axis=1))
    rows_t = rows_src.T
    causal = col <= row
    ones = jnp.ones((L, MLSTM_DH), F32)
    for h in range(MLSTM_HEADS):
        hs = slice(h * MLSTM_DH, (h + 1) * MLSTM_DH)
        b_col = bcum[:, h:h + 1]
        ip_row = rows_t[h:h + 1, :]
        b_row = rows_t[MLSTM_HEADS + h:MLSTM_HEADS + h + 1, :]
        dlog = jnp.where(causal, b_col - b_row + ip_row, NEG_INF)
        il = b_col + m_prev[:, h:h + 1]
        m_i = jnp.maximum(il, jnp.max(dlog, axis=1, keepdims=True))
        w_intra = jnp.exp(dlog - m_i)
        s_inter = jnp.exp(il - m_i)
        q_h = qk_b[:, h * MLSTM_DH:(h + 1) * MLSTM_DH].astype(BF16)
        k_h = (qk_b[:, ML_W + h * MLSTM_DH:ML_W + (h + 1) * MLSTM_DH] * (MLSTM_DH ** -0.5)).astype(BF16)
        v_h = p_ref[0, :, P0_VB + h * MLSTM_DH:P0_VB + (h + 1) * MLSTM_DH]
        v_aug = jnp.concatenate([v_h, ones], axis=1)
        a = (w_intra * _dot_nt(q_h, k_h)).astype(BF16)
        c_prev = cst_ref[h]
        tot = _dot(a, v_aug.astype(BF16)) + s_inter * _dot_nt(q_h, c_prev.astype(BF16))
        num = tot[:, :MLSTM_DH]
        den = tot[:, MLSTM_DH:]
        hid = num / jnp.maximum(jnp.abs(den), jnp.exp(-m_i))
        w_aug = (v_aug * wj[:, h:h + 1]).astype(BF16)
        cst_ref[h] = scl[:, h:h + 1] * c_prev + _dot_tn(w_aug, k_h)
        ob = p_ref[0, :, P0_OB + h * MLSTM_DH:P0_OB + (h + 1) * MLSTM_DH]
        o_ref[0, :, GLA_V + h * MLSTM_DH:GLA_V + (h + 1) * MLSTM_DH] = (
            jax.nn.sigmoid(ob) * _head_norm(hid, mg_ref[:, hs])).astype(o_ref.dtype)
    m_ref[...] = m_new


def _mixer0(proj, wa_up, ba, gla_g, conv_w, conv_b, b_i, b_f, ml_g):
    bsz, s, n = proj.shape
    L = MIX_TILE
    wa_pad = jnp.zeros((LANES, GLA_QK), F32).at[G_A:G_A + GLA_GATE_RANK].set(wa_up).astype(BF16)
    gbias = jnp.zeros((1, LANES), F32).at[0, G_I:G_I + MLSTM_HEADS].set(b_i).at[0, G_F:G_F + MLSTM_HEADS].set(b_f)
    full = lambda shape: pl.BlockSpec(shape, lambda b, i: (0,) * len(shape))
    return pl.pallas_call(
        _mixer0_kernel,
        out_shape=jax.ShapeDtypeStruct((bsz, s, GLA_V + ML_W), BF16),
        grid=(bsz, s // L),
        in_specs=[
            pl.BlockSpec((1, L, n), lambda b, i: (b, i, 0)),
            full((LANES, GLA_QK)),
            full((1, GLA_QK)),
            full((1, GLA_V)),
            full((MLSTM_CONV, 2 * ML_W)),
            full((1, 2 * ML_W)),
            full((1, LANES)),
            full((1, ML_W)),
        ],
        out_specs=pl.BlockSpec((1, L, GLA_V + ML_W), lambda b, i: (b, i, 0)),
        scratch_shapes=[
            pltpu.VMEM((GLA_DV, GLA_QK), F32),
            pltpu.VMEM((MLSTM_HEADS, 2 * MLSTM_DH, MLSTM_DH), F32),
            pltpu.VMEM((1, LANES), F32),
            pltpu.VMEM((8 + L, 2 * ML_W), F32),
        ],
        compiler_params=_params("parallel", "arbitrary"),
        name="mixer0",
    )(proj, wa_pad, ba.reshape(1, GLA_QK), gla_g.reshape(1, GLA_V), conv_w, conv_b.reshape(1, 2 * ML_W),
      gbias, ml_g.reshape(1, ML_W))


def _sb_kernel(q_ref, k_ref, v_ref, o_ref, qs_ref, acc_ref, car_ref, mar_ref, kn_ref):
    TQ, SUB, WIDE = SB_TQ, SB_SUB, SB_WIDE
    ns = TQ // SUB
    RB = 2 * SUB
    i = pl.program_id(2)
    lane = _iota((1, LANES), 1)
    head0 = (lane >> _HEAD_SHIFT) == 0

    @pl.when(i == 0)
    def _():
        k2 = jnp.square(k_ref[0].astype(F32))
        for h in range(2):
            n2 = jnp.sum(jnp.where(head0 if h == 0 else ~head0, k2, 0.0), axis=1, keepdims=True)
            kn_ref[h:h + 1, :] = jnp.broadcast_to(jnp.sqrt(jnp.max(n2, axis=0, keepdims=True)), (1, LANES))

    for a in range(ns):
        qa = q_ref[0, a * SUB:(a + 1) * SUB, :].astype(F32) * (SB_DH ** -0.5 * LOG2E)
        for h in range(2):
            rows = slice(a * RB + h * SUB, a * RB + (h + 1) * SUB)
            qh = jnp.where(head0 if h == 0 else ~head0, qa, 0.0).astype(BF16)
            qs_ref[rows, :] = qh
            qn = jnp.sqrt(jnp.sum(jnp.square(qh.astype(F32)), axis=1, keepdims=True))
            mar_ref[rows, :] = qn * kn_ref[h:h + 1, :] * SB_BOUND_SLACK + SB_ZERO_BITS
    tri = jnp.where(_iota((WIDE, WIDE), 0) >= _iota((WIDE, WIDE), 1), -1.0, 0.0).astype(BF16)

    def key_rows(ku, nk):
        r0 = ku * SUB
        return pl.ds(r0 if isinstance(r0, int) else pl.multiple_of(r0, SUB), nk)

    def logits(a, ku, nk):
        return _dot_nt(qs_ref[a * RB:(a + 1) * RB, :], k_ref[0, key_rows(ku, nk), :])

    def softplus2(z, diagonal):
        sp = jnp.maximum(z, 0.0) + jnp.log(1.0 + jnp.exp2(-jnp.abs(z))) * LOG2E
        if not diagonal:
            return sp, None
        mask = _iota(z.shape, 1) < (_iota(z.shape, 0) & (SUB - 1))
        return jnp.where(mask, sp, 0.0), mask

    def suffix(sp):
        nk = sp.shape[1]
        return _dot(sp.astype(BF16), tri[:nk, :nk]), jnp.sum(sp, axis=1, keepdims=True)

    def weights(z, res, car, mask):
        att = jnp.exp2(z + res if car is None else z + res - car)
        if mask is not None:
            att = jnp.where(mask, att, 0.0)
        return att.astype(BF16)

    def far_tile(a, ku):
        rows = slice(a * RB, (a + 1) * RB)
        z = logits(a, ku, SUB)
        sp, _ = softplus2(z, False)
        res, tot = suffix(sp)
        car = car_ref[rows, :]
        acc_ref[rows, :] += _dot(weights(z, res, car, None), v_ref[0, key_rows(ku, SUB), :])
        car_ref[rows, :] = car + tot

    def near_tiles(first_block):
        chains = []
        for a in range(ns):
            g = a if first_block else ns * i + a
            chain = [(a, g, SUB, True)]
            if not first_block or a >= 2:
                chain.append((a, g - 2, WIDE, False))
            elif a == 1:
                chain.append((a, 0, SUB, False))
            chains.append(chain)
        jobs = [job for step in itertools.zip_longest(*chains) for job in step if job is not None]
        zs = [logits(a, ku, nk) for a, ku, nk, _ in jobs]
        sps = [softplus2(z, diagonal) for z, (_, _, _, diagonal) in zip(zs, jobs)]
        sufs = [suffix(sp) for sp, _ in sps]
        car = [None] * ns
        atts = []
        for z, (sp, mask), (res, tot), (a, _, _, _) in zip(zs, sps, sufs, jobs):
            atts.append(weights(z, res, car[a], mask))
            car[a] = tot if car[a] is None else car[a] + tot
        acc = [None] * ns
        for att, (a, ku, nk, _) in zip(atts, jobs):
            t = _dot(att, v_ref[0, key_rows(ku, nk), :])
            acc[a] = t if acc[a] is None else acc[a] + t
        for a in range(ns):
            acc_ref[a * RB:(a + 1) * RB, :] = acc[a]
            car_ref[a * RB:(a + 1) * RB, :] = jnp.broadcast_to(car[a], (RB, LANES))

    pl.when(i == 0)(functools.partial(near_tiles, True))
    pl.when(i > 0)(functools.partial(near_tiles, False))

    def next_unit(a, t):
        return ns * i + a - 3 - t

    def pending(t):
        gap = [jnp.where(next_unit(a, t) >= 0, mar_ref[a * RB:(a + 1) * RB, :] - car_ref[a * RB:(a + 1) * RB, :], -1.0)
               for a in range(ns)]
        return jnp.max(functools.reduce(jnp.maximum, gap)) > 0.0

    def far_body(c):
        t, _ = c
        for a in range(ns):
            ku = next_unit(a, t)
            pl.when(ku >= 0)(functools.partial(far_tile, a, ku))
        return t + 1, pending(t + 1)

    lax.while_loop(lambda c: c[1], far_body, (0, pending(0)))

    for a in range(ns):
        o_ref[0, a * SUB:(a + 1) * SUB, :] = jnp.where(
            head0, acc_ref[a * RB:a * RB + SUB, :], acc_ref[a * RB + SUB:(a + 1) * RB, :]).astype(o_ref.dtype)


def _stick_breaking(pc):
    bsz, s, _ = pc.shape
    npair = SB_W // LANES
    return pl.pallas_call(
        _sb_kernel,
        out_shape=jax.ShapeDtypeStruct((bsz, s, SB_W), BF16),
        grid=(bsz, npair, s // SB_TQ),
        in_specs=[
            pl.BlockSpec((1, SB_TQ, LANES), lambda b, p, i: (b, i, p)),
            pl.BlockSpec((1, s, LANES), lambda b, p, i: (b, 0, npair + p)),
            pl.BlockSpec((1, s, LANES), lambda b, p, i: (b, 0, 2 * npair + p)),
        ],
        out_specs=pl.BlockSpec((1, SB_TQ, LANES), lambda b, p, i: (b, i, p)),
        scratch_shapes=[
            pltpu.VMEM((2 * SB_TQ, LANES), BF16),
            pltpu.VMEM((2 * SB_TQ, LANES), F32),
            pltpu.VMEM((2 * SB_TQ, LANES), F32),
            pltpu.VMEM((2 * SB_TQ, LANES), F32),
            pltpu.VMEM((8, LANES), F32),
        ],
        compiler_params=_params("parallel", "parallel", "arbitrary"),
        name="stick_breaking",
    )(pc, pc, pc)


def _t5_bucket(dist):
    max_exact = N_REL_BUCKETS // 2
    dd = np.maximum(dist, 1).astype(np.float64)
    large = max_exact + (np.log(dd / max_exact) / np.log(REL_MAX_DIST / max_exact)
                         * (N_REL_BUCKETS - max_exact)).astype(np.int32)
    large = np.minimum(large, N_REL_BUCKETS - 1)
    return np.where(dist < max_exact, dist, large).astype(np.int32)


def _dil_buckets():
    qi = np.arange(DIL_BLK)[:, None]
    kj = np.arange(2 * DIL_BLK)[None, :]
    delta = qi - kj + DIL_BLK
    in_win = (delta >= 0) & (delta <= DIL_BLK)
    tabs = []
    for window, dil in DIL_PAIRS:
        assert window // dil == DIL_BLK
        bucket = _t5_bucket(np.clip(delta, 0, None) * dil)
        tabs.append(np.where(in_win, bucket, -1).astype(np.int32))
    return np.stack(tabs, 0)


def _dil_kernel(q_ref, k_ref, v_ref, bkt_ref, bkt0_ref, tab_ref, o_ref, m_sc, l_sc, acc_sc):
    s_len = q_ref.shape[1]
    hp = pl.program_id(1)
    g = pl.program_id(2)
    blk = DIL_BLK
    ng = len(DIL_PAIRS)
    lane = _iota((1, LANES), 1)
    lane_head = lane >> _HEAD_SHIFT
    sel0 = lane_head == 0
    kcol = _iota((2 * blk, 2 * blk), 1)

    def head_biases(gi, bkt):
        biases = []
        for hl in range(2):
            head = gi * DIL_HEADS_PER_GROUP + hp * 2 + hl
            bias = jnp.full(bkt.shape, NEG_INF, F32)
            for bk in range(N_REL_BUCKETS):
                bias = jnp.where(bkt == bk, tab_ref[bk, head], bias)
            biases.append(bias)
        return jnp.concatenate(biases, axis=0)

    def stack_heads(qb):
        return jnp.concatenate([jnp.where(sel0, qb, 0.0), jnp.where(sel0, 0.0, qb)], axis=0).astype(BF16)

    def with_ones(vv):
        return jnp.concatenate([vv, jnp.ones_like(vv)], axis=1).astype(BF16)

    def first_group(gi, dil):
        n_sub = s_len // dil
        bias2 = head_biases(gi, bkt0_ref[...])

        def residue(r):
            rows = pl.ds(r, n_sub, stride=dil)
            logits = _dot_nt(stack_heads(q_ref[0, rows, :] * (DIL_DH ** -0.5)), k_ref[0, rows, :].astype(BF16)) + bias2
            return rows, logits

        def body(tt, carry):
            s1 = [residue(tt * DIL_UNROLL + u) for u in range(DIL_UNROLL)]
            s2 = [jnp.max(lg, axis=1, keepdims=True) for _, lg in s1]
            s3 = [_dot(jnp.exp(lg - mc).astype(BF16), with_ones(v_ref[0, rows, :])) for (rows, lg), mc in zip(s1, s2)]
            for (rows, _), mc, pv in zip(s1, s2, s3):
                m_sc[rows, :] = jnp.where(sel0, mc[0:n_sub], mc[n_sub:])
                l_sc[rows, :] = jnp.where(sel0, pv[0:n_sub, LANES:], pv[n_sub:, LANES:])
                acc_sc[rows, :] = jnp.where(sel0, pv[0:n_sub, :LANES], pv[n_sub:, :LANES])
            return carry

        lax.fori_loop(0, dil // DIL_UNROLL, body, 0)

    def group(gi, dil):
        nb = s_len // (blk * dil)
        bias2 = head_biases(gi, bkt_ref[gi])

        def stage_logits(t):
            r = t // nb
            n = t % nb
            q_start = r + dil * blk * n
            p_start = r + dil * blk * jnp.maximum(n - 1, 0)
            rows_q = pl.ds(q_start, blk, stride=dil) if dil > 1 else pl.ds(q_start, blk)
            rows_p = pl.ds(p_start, blk, stride=dil) if dil > 1 else pl.ds(p_start, blk)
            q2 = stack_heads(q_ref[0, rows_q, :] * (DIL_DH ** -0.5))
            kk = jnp.concatenate([k_ref[0, rows_p, :], k_ref[0, rows_q, :]], axis=0).astype(BF16)
            logits = _dot_nt(q2, kk) + bias2
            logits = jnp.where(jnp.logical_and(n == 0, kcol < blk), NEG_INF, logits)
            return rows_q, rows_p, logits

        def stage_pv(rows_q, rows_p, logits, m_col):
            p = jnp.exp(logits - m_col)
            vv = with_ones(jnp.concatenate([v_ref[0, rows_p, :], v_ref[0, rows_q, :]], axis=0))
            pv = _dot(p.astype(BF16), vv)
            return pv[:, LANES:], pv[:, :LANES]

        def body(tt, carry):
            ts = [tt * DIL_UNROLL + u for u in range(DIL_UNROLL)]
            s1 = [stage_logits(t) for t in ts]
            s2 = [jnp.max(lg, axis=1, keepdims=True) for _, _, lg in s1]
            s3 = [stage_pv(rq, rp, lg, mc) for (rq, rp, lg), mc in zip(s1, s2)]
            outs = []
            for (rows_q, _, _), m_col, (ps, pv) in zip(s1, s2, s3):
                m_old = m_sc[rows_q, :]
                m_blk = jnp.where(sel0, m_col[0:blk], m_col[blk:])
                m_new = jnp.maximum(m_old, m_blk)
                a_old = jnp.exp(m_old - m_new)
                a_blk = jnp.exp(m_blk - m_new)
                l_new = a_old * l_sc[rows_q, :] + a_blk * jnp.where(sel0, ps[0:blk], ps[blk:])
                a_new = a_old * acc_sc[rows_q, :] + a_blk * jnp.where(sel0, pv[0:blk], pv[blk:])
                outs.append((rows_q, m_new, l_new, a_new))
            for rows_q, m_new, l_new, a_new in outs:
                m_sc[rows_q, :] = m_new
                l_sc[rows_q, :] = l_new
                acc_sc[rows_q, :] = a_new
            return carry

        lax.fori_loop(0, s_len // (blk * DIL_UNROLL), body, 0)

    for gi, (_, dil) in enumerate(DIL_PAIRS):
        pl.when(g == ng - 1 - gi)(functools.partial(first_group if gi == ng - 1 else group, gi, dil))

    @pl.when(g == ng - 1)
    def _():
        o_ref[0] = (acc_sc[...] / l_sc[...]).astype(o_ref.dtype)


def _dilated(pd, rel_bias):
    bsz, s, _ = pd.shape
    ng = len(DIL_PAIRS)
    gw = DIL_HEADS_PER_GROUP * DIL_DH // LANES
    nq = DIL_W // LANES
    bkt = jnp.asarray(_dil_buckets())
    dil0 = DIL_PAIRS[-1][1]
    n_sub = s // dil0
    assert s % (DIL_BLK * dil0) == 0 and dil0 % DIL_UNROLL == 0
    delta = np.arange(n_sub)[:, None] - np.arange(n_sub)[None, :]
    bkt0 = jnp.asarray(np.where((delta >= 0) & (delta <= DIL_BLK), _t5_bucket(np.clip(delta, 0, None) * dil0), -1)
                       .astype(np.int32))
    blk_spec = lambda base: pl.BlockSpec((1, s, LANES), lambda b, p, g: (b, 0, base + (ng - 1 - g) * gw + p))
    return pl.pallas_call(
        _dil_kernel,
        out_shape=jax.ShapeDtypeStruct((bsz, s, DIL_OUT), BF16),
        grid=(bsz, gw, ng),
        in_specs=[
            blk_spec(0),
            blk_spec(nq),
            blk_spec(2 * nq),
            pl.BlockSpec((ng, DIL_BLK, 2 * DIL_BLK), lambda b, p, g: (0, 0, 0)),
            pl.BlockSpec((n_sub, n_sub), lambda b, p, g: (0, 0)),
            pl.BlockSpec(memory_space=pltpu.SMEM),
        ],
        out_specs=pl.BlockSpec((1, s, LANES), lambda b, p, g: (b, 0, p)),
        scratch_shapes=[pltpu.VMEM((s, LANES), F32)] * 3,
        compiler_params=_params("parallel", "parallel", "arbitrary"),
        name="dilated",
    )(pd, pd, pd, bkt, bkt0, rel_bias)


def _ab_weight(w):
    o = np.cumsum((0,) + (GLA_QK, GLA_QK, GLA_V, GLA_V, GLA_GATE_RANK, 2 * ML_W, ML_W, MLSTM_HEADS, MLSTM_HEADS, ML_W))
    qa, ka, va, ra, aa, qkb, vb, ib, fb, ob = [w[:, o[j]:o[j + 1]] for j in range(10)]
    pad = jnp.zeros((w.shape[0], LANES - 2 * MLSTM_HEADS - GLA_GATE_RANK), w.dtype)
    return jnp.concatenate([qa, ka, va, ra, qkb, vb, ob, ib, fb, aa, pad], axis=1).astype(BF16)


def kernel(x, c, ada_w, ada_b, ln_g, ln_b, ab_w_in, gla_wa_up, gla_ba, gla_norm_g, ml_conv_w, ml_conv_b,
           ml_b_i, ml_b_f, ml_norm_g, ab_w_out, cd_w_in, rel_bias, cd_w_out, ffn_w1, ffn_w3, ffn_w2):
    bsz, s, d = x.shape
    mod_all = _ada_mod(c, ada_w, ada_b).reshape(DEPTH, bsz, 6, d)
    w1, w3, w2 = _to_bf16(ffn_w1), _to_bf16(ffn_w3), _to_bf16(ffn_w2)
    ab_in, ab_out, cd_in, cd_out = _to_bf16(ab_w_in), _to_bf16(ab_w_out), _to_bf16(cd_w_in), _to_bf16(cd_w_out)
    tm = FFN_TM
    for layer in range(DEPTH):
        mod = mod_all[layer]
        j = layer // 2
        if layer % 2 == 0:
            (proj,) = _inproj(x, mod, _ab_weight(ab_in[j]), [(P0_N, F32)], tm, 0, 1, "inproj0")
            cat = _mixer0(proj, gla_wa_up[j], gla_ba[j], gla_norm_g[j], ml_conv_w[j], ml_conv_b[j],
                          ml_b_i[j], ml_b_f[j], ml_norm_g[j])
            acts, w_outs = [cat], [ab_out[j]]
        else:
            pc, pd = _inproj(x, mod, cd_in[j], [(3 * SB_W, BF16), (3 * DIL_W, F32)], tm, 0, 1, "inproj1")
            oc = _stick_breaking(pc)
            od = _dilated(pd, rel_bias)
            acts, w_outs = [oc, od], [cd_out[j, :SB_W], cd_out[j, SB_W:]]
        x = _post(acts, w_outs, x, mod, w1, w3, w2, layer, ln_g[layer], ln_b[layer], FFN_TM)
    return x
```

```python
import functools
import itertools

import numpy as np
import jax
import jax.numpy as jnp
from jax import lax
from jax.experimental import pallas as pl
from jax.experimental.pallas import tpu as pltpu

F32 = jnp.float32
BF16 = jnp.bfloat16

D_MODEL = 1024
DEPTH = 2
GLA_HEADS = 4
GLA_DK = 64
GLA_DV = 128
GLA_GATE_RANK = 16
GLA_TAU = 16.0
GLA_CHUNK = 64
MLSTM_HEADS = 4
MLSTM_DH = 128
MLSTM_CONV = 4
SB_HEADS = 8
SB_DH = 64
DIL_PAIRS = ((128, 1), (512, 4), (2048, 16))
DIL_HEADS_PER_GROUP = 4
DIL_DH = 64
DIL_BLK = 128
DIL_UNROLL = 4
N_REL_BUCKETS = 32
REL_MAX_DIST = 2048
D_FF = ((8 * D_MODEL + 3 * 256 - 1) // (3 * 256)) * 256
LN_EPS = 1e-5
RES_ALPHA = (2 * DEPTH) ** 0.25
NEG_INF = -1e30

GLA_QK = GLA_HEADS * GLA_DK
GLA_V = GLA_HEADS * GLA_DV
ML_W = MLSTM_HEADS * MLSTM_DH
SB_W = SB_HEADS * SB_DH
DIL_W = len(DIL_PAIRS) * DIL_HEADS_PER_GROUP * DIL_DH
DIL_OUT = DIL_HEADS_PER_GROUP * DIL_DH

LANES = 128
VMEM_LIMIT = 56 * 1024 * 1024

P0_QA = 0
P0_KA = P0_QA + GLA_QK
P0_VA = P0_KA + GLA_QK
P0_RA = P0_VA + GLA_V
P0_QKB = P0_RA + GLA_V
P0_VB = P0_QKB + 2 * ML_W
P0_OB = P0_VB + ML_W
P0_G = P0_OB + ML_W
P0_N = P0_G + LANES
G_I = 0
G_F = MLSTM_HEADS
G_A = 2 * MLSTM_HEADS

MIX_TILE = 256
FF_CHUNK = 256
FFN_TM = 512
SB_TQ = 1024
SB_SUB = 128
SB_WIDE = 256
SB_ZERO_BITS = 160.0
SB_BOUND_SLACK = 1.01
LOG2E = 1.4426950408889634
_CHUNK_SHIFT = 6
_HEAD_SHIFT = 6
assert GLA_CHUNK == GLA_DK == 1 << _CHUNK_SHIFT and SB_DH == DIL_DH == 1 << _HEAD_SHIFT


def _dot(a, b):
    return jnp.dot(a, b, preferred_element_type=F32)


def _dot_nt(a, b):
    return lax.dot_general(a, b, (((1,), (1,)), ((), ())), preferred_element_type=F32)


def _dot_tn(a, b):
    return lax.dot_general(a, b, (((0,), (0,)), ((), ())), preferred_element_type=F32)


def _dot_split(t, x, terms):
    acc = None
    rem = x
    for i in range(terms):
        part = rem.astype(BF16)
        d = _dot(t, part)
        acc = d if acc is None else acc + d
        if i + 1 < terms:
            rem = rem - part.astype(F32)
    return acc


def _log_sigmoid(x):
    return jnp.minimum(x, 0.0) - jnp.log(1.0 + jnp.exp(-jnp.abs(x)))


def _silu(x):
    return x * jax.nn.sigmoid(x)


def _layer_norm(r, g, b):
    mu = jnp.mean(r, axis=-1, keepdims=True)
    d = r - mu
    var = jnp.mean(d * d, axis=-1, keepdims=True)
    return d * lax.rsqrt(var + LN_EPS) * g + b


def _head_norm(x, g):
    mu = jnp.mean(x, axis=-1, keepdims=True)
    d = x - mu
    var = jnp.mean(d * d, axis=-1, keepdims=True)
    return d * lax.rsqrt(var + LN_EPS) * g


def _iota(shape, dim):
    return lax.broadcasted_iota(jnp.int32, shape, dim)


def _params(*sem):
    return pltpu.CompilerParams(dimension_semantics=sem, vmem_limit_bytes=VMEM_LIMIT)


def _ada_kernel(c_ref, w_ref, b_ref, o_ref):
    ca = _silu(c_ref[...]).astype(BF16)
    o_ref[0] = _dot(ca, w_ref[0].astype(BF16)) + b_ref[0]


def _ada_mod(c, ada_w, ada_b):
    bsz, d = c.shape
    n = ada_w.shape[-1]
    tn = n // 4
    return pl.pallas_call(
        _ada_kernel,
        out_shape=jax.ShapeDtypeStruct((DEPTH, bsz, n), F32),
        grid=(DEPTH, n // tn),
        in_specs=[
            pl.BlockSpec((bsz, d), lambda l, j: (0, 0)),
            pl.BlockSpec((1, d, tn), lambda l, j: (l, 0, j)),
            pl.BlockSpec((1, 1, tn), lambda l, j: (l, 0, j)),
        ],
        out_specs=pl.BlockSpec((1, bsz, tn), lambda l, j: (l, 0, j)),
        compiler_params=_params("arbitrary", "arbitrary"),
        name="ada_mod",
    )(c, ada_w, ada_b.reshape(DEPTH, 1, n))


def _inproj_kernel(x_ref, mod_ref, w_ref, *o_refs, shift_row, scale_row):
    sh = mod_ref[0, shift_row:shift_row + 1, :]
    sc = mod_ref[0, scale_row:scale_row + 1, :]
    hm = (x_ref[0] * (1.0 + sc) + sh).astype(BF16)
    col = 0
    for o_ref in o_refs:
        n = o_ref.shape[-1]
        o_ref[0] = _dot(hm, w_ref[:, col:col + n]).astype(o_ref.dtype)
        col += n


def _inproj(x, mod, w, outs, tm, shift_row, scale_row, name):
    bsz, s, d = x.shape
    n = w.shape[1]
    assert sum(o[0] for o in outs) == n
    return pl.pallas_call(
        functools.partial(_inproj_kernel, shift_row=shift_row, scale_row=scale_row),
        out_shape=[jax.ShapeDtypeStruct((bsz, s, o[0]), o[1]) for o in outs],
        grid=(bsz, s // tm),
        in_specs=[
            pl.BlockSpec((1, tm, d), lambda b, i: (b, i, 0)),
            pl.BlockSpec((1, 6, d), lambda b, i: (b, 0, 0)),
            pl.BlockSpec((d, n), lambda b, i: (0, 0)),
        ],
        out_specs=[pl.BlockSpec((1, tm, o[0]), lambda b, i: (b, i, 0)) for o in outs],
        compiler_params=_params("parallel", "arbitrary"),
        name=name,
    )(x, mod, w)


def _post_kernel(*refs, n_act):
    act_refs = refs[:n_act]
    wo_refs = refs[n_act:2 * n_act]
    x_ref, mod_ref, w1_ref, w3_ref, w2_ref, g_ref, b_ref, o_ref, gm_ref = refs[2 * n_act:]
    y = None
    for a_ref, w_ref in zip(act_refs, wo_refs):
        t = _dot(a_ref[0], w_ref[...])
        y = t if y is None else y + t
    r = RES_ALPHA * x_ref[0] + (1.0 + mod_ref[0, 2:3, :]) * y
    xm = _layer_norm(r, g_ref[0:1, :], b_ref[0:1, :])
    hf = (xm * (1.0 + mod_ref[0, 4:5, :]) + mod_ref[0, 3:4, :]).astype(BF16)
    for c in range(D_FF // FF_CHUNK):
        cs = slice(c * FF_CHUNK, (c + 1) * FF_CHUNK)
        gm_ref[:, cs] = (_silu(_dot(hf, w1_ref[:, cs])) * _dot(hf, w3_ref[:, cs])).astype(BF16)
    y2 = _dot(gm_ref[...], w2_ref[...])
    r2 = RES_ALPHA * xm + (1.0 + mod_ref[0, 5:6, :]) * y2
    o_ref[0] = _layer_norm(r2, g_ref[1:2, :], b_ref[1:2, :])


def _post(acts, w_outs, x, mod, w1, w3, w2, ln_g, ln_b, tm):
    bsz, s, d = x.shape
    n_act = len(acts)
    const = lambda shape: pl.BlockSpec(shape, lambda b, i: (0,) * len(shape), pipeline_mode=pl.Buffered(1))
    in_specs = [pl.BlockSpec((1, tm, a.shape[-1]), lambda b, i: (b, i, 0)) for a in acts]
    in_specs += [const(w.shape) for w in w_outs]
    in_specs += [
        pl.BlockSpec((1, tm, d), lambda b, i: (b, i, 0)),
        pl.BlockSpec((1, 6, d), lambda b, i: (b, 0, 0)),
        const((d, D_FF)),
        const((d, D_FF)),
        const((D_FF, d)),
        const((2, d)),
        const((2, d)),
    ]
    return pl.pallas_call(
        functools.partial(_post_kernel, n_act=n_act),
        out_shape=jax.ShapeDtypeStruct((bsz, s, d), F32),
        grid=(bsz, s // tm),
        in_specs=in_specs,
        out_specs=pl.BlockSpec((1, tm, d), lambda b, i: (b, i, 0)),
        scratch_shapes=[pltpu.VMEM((tm, D_FF), BF16)],
        compiler_params=_params("parallel", "arbitrary"),
        name="post",
    )(*acts, *w_outs, x, mod, w1.astype(BF16), w3.astype(BF16), w2.astype(BF16), ln_g, ln_b)


def _mixer0_kernel(p_ref, wa_ref, ba_ref, gg_ref, cw_ref, cb_ref, gb_ref, mg_ref, o_ref,
                   st_ref, cst_ref, m_ref, xc_ref):
    L = MIX_TILE
    i = pl.program_id(1)

    @pl.when(i == 0)
    def _():
        st_ref[...] = jnp.zeros_like(st_ref)
        cst_ref[...] = jnp.zeros_like(cst_ref)
        m_ref[...] = jnp.zeros_like(m_ref)
        xc_ref[0:8, :] = jnp.zeros((8, 2 * ML_W), F32)

    lane = _iota((1, LANES), 1)
    g_raw = p_ref[0, :, P0_G:P0_G + LANES]

    u = _dot(g_raw.astype(BF16), wa_ref[...]) + ba_ref[...]
    la = _log_sigmoid(u) * (1.0 / GLA_TAU)
    row = _iota((L, L), 0)
    col = _iota((L, L), 1)
    same_chunk = (row >> _CHUNK_SHIFT) == (col >> _CHUNK_SHIFT)
    tri_blk = jnp.where(same_chunk & (col <= row), 1.0, 0.0).astype(BF16)
    bcs = _dot_split(tri_blk, la, 3)
    q_in = p_ref[0, :, P0_QA:P0_QA + GLA_QK] * (GLA_DK ** -0.5) * jnp.exp(bcs)
    k_raw = p_ref[0, :, P0_KA:P0_KA + GLA_QK]
    k_in = k_raw * jnp.exp(-bcs)
    lane_qk = _iota((1, GLA_QK), 1)
    head_masks = [(lane_qk >> _CHUNK_SHIFT) == h for h in range(GLA_HEADS)]
    r4 = _iota((GLA_HEADS * GLA_CHUNK, GLA_CHUNK), 0)
    c4 = _iota((GLA_HEADS * GLA_CHUNK, GLA_CHUNK), 1)
    tril4 = c4 <= (r4 & (GLA_CHUNK - 1))
    oa_chunks = []
    for c in range(L // GLA_CHUNK):
        r0, r1 = c * GLA_CHUNK, (c + 1) * GLA_CHUNK
        b_c = bcs[r0:r1]
        bl = b_c[GLA_CHUNK - 1:GLA_CHUNK, :]
        q_c = q_in[r0:r1]
        k_c = k_in[r0:r1].astype(BF16)
        k_end = k_raw[r0:r1] * jnp.exp(bl - b_c)
        v_c = p_ref[0, r0:r1, P0_VA:P0_VA + GLA_V]
        q_exp = jnp.concatenate([jnp.where(hm, q_c, 0.0) for hm in head_masks], axis=0).astype(BF16)
        k_exp = jnp.concatenate([jnp.where(hm, k_end, 0.0) for hm in head_masks], axis=0).astype(BF16)
        sc = jnp.where(tril4, _dot_nt(q_exp, k_c), 0.0)
        intra = _dot(sc.astype(BF16), v_c.astype(BF16))
        inter = _dot_nt(q_exp, st_ref[...].astype(BF16))
        o_heads = []
        for h in range(GLA_HEADS):
            h0, h1 = h * GLA_CHUNK, (h + 1) * GLA_CHUNK
            o_heads.append(intra[h0:h1, h * GLA_DV:(h + 1) * GLA_DV] + inter[h0:h1])
        oa_chunks.append(o_heads)
        v_cat = jnp.concatenate([v_c[:, h * GLA_DV:(h + 1) * GLA_DV] for h in range(GLA_HEADS)], axis=0)
        st_ref[...] = st_ref[...] * jnp.exp(bl) + _dot_tn(v_cat.astype(BF16), k_exp)
    for h in range(GLA_HEADS):
        o_h = jnp.concatenate([oc[h] for oc in oa_chunks], axis=0)
        cs = slice(h * GLA_DV, (h + 1) * GLA_DV)
        ra = p_ref[0, :, P0_RA + h * GLA_DV:P0_RA + (h + 1) * GLA_DV]
        o_ref[0, :, cs] = (_head_norm(o_h, gg_ref[:, cs]) * _silu(ra)).astype(o_ref.dtype)

    xc_ref[8:8 + L, :] = p_ref[0, :, P0_QKB:P0_QKB + 2 * ML_W]
    conv = cb_ref[...]
    for kk in range(MLSTM_CONV):
        conv = conv + cw_ref[kk:kk + 1, :] * xc_ref[8 - (MLSTM_CONV - 1) + kk:8 - (MLSTM_CONV - 1) + kk + L, :]
    xc_ref[0:8, :] = xc_ref[L:L + 8, :]
    qk_b = _silu(conv)

    gb = jnp.where(lane < G_A, g_raw + gb_ref[...], 0.0)
    ipre = gb
    logf = _log_sigmoid(pltpu.roll(gb, LANES - G_F, axis=1))
    logf = jnp.where(lane < MLSTM_HEADS, logf, 0.0)
    tri = jnp.where(col <= row, 1.0, 0.0).astype(BF16)
    bcum = _dot_split(tri, logf, 3)
    blast = bcum[L - 1:L, :]
    wend = blast - bcum + ipre
    m_prev = m_ref[...]
    m_new = jnp.maximum(blast + m_prev, jnp.max(wend, axis=0, keepdims=True))
    scl = jnp.exp(blast + m_prev - m_new)
    wj = jnp.exp(wend - m_new)
    rows_src = jnp.where(lane < MLSTM_HEADS, ipre, pltpu.roll(bcum, MLSTM_HEADS, axis=1))
    rows_t = rows_src.T
    causal = col <= row
    ones = jnp.ones((L, MLSTM_DH), F32)
    for h in range(MLSTM_HEADS):
        hs = slice(h * MLSTM_DH, (h + 1) * MLSTM_DH)
        wide = lambda t: jnp.concatenate([t, t], axis=1)
        b_rep = jnp.broadcast_to(bcum[:, h:h + 1], (L, LANES))
        wj_rep = jnp.broadcast_to(wj[:, h:h + 1], (L, LANES))
        ip_row = rows_t[h:h + 1, :]
        b_row = rows_t[MLSTM_HEADS + h:MLSTM_HEADS + h + 1, :]
        dlog = jnp.where(causal, wide(b_rep) - b_row + ip_row, NEG_INF)
        il = b_rep + m_prev[:, h:h + 1]
        m_i = jnp.maximum(il, jnp.broadcast_to(jnp.max(dlog, axis=1, keepdims=True), (L, LANES)))
        w_intra = jnp.exp(dlog - wide(m_i))
        s_inter = jnp.exp(il - m_i)
        q_h = qk_b[:, h * MLSTM_DH:(h + 1) * MLSTM_DH].astype(BF16)
        k_h = (qk_b[:, ML_W + h * MLSTM_DH:ML_W + (h + 1) * MLSTM_DH] * (MLSTM_DH ** -0.5)).astype(BF16)
        v_h = p_ref[0, :, P0_VB + h * MLSTM_DH:P0_VB + (h + 1) * MLSTM_DH]
        v_aug = jnp.concatenate([v_h, ones], axis=1)
        a = (w_intra * _dot_nt(q_h, k_h)).astype(BF16)
        c_prev = cst_ref[h]
        tot = _dot(a, v_aug.astype(BF16)) + wide(s_inter) * _dot(q_h, c_prev.astype(BF16))
        num = tot[:, :MLSTM_DH]
        den = tot[:, MLSTM_DH:]
        hid = num / jnp.maximum(jnp.abs(den), jnp.exp(-m_i))
        w_aug = (v_aug * wide(wj_rep)).astype(BF16)
        cst_ref[h] = scl[:, h:h + 1] * c_prev + _dot_tn(k_h, w_aug)
        ob = p_ref[0, :, P0_OB + h * MLSTM_DH:P0_OB + (h + 1) * MLSTM_DH]
        o_ref[0, :, GLA_V + h * MLSTM_DH:GLA_V + (h + 1) * MLSTM_DH] = (
            jax.nn.sigmoid(ob) * _head_norm(hid, mg_ref[:, hs])).astype(o_ref.dtype)
    m_ref[...] = m_new


def _mixer0(proj, wa_up, ba, gla_g, conv_w, conv_b, b_i, b_f, ml_g):
    bsz, s, n = proj.shape
    L = MIX_TILE
    wa_pad = jnp.zeros((LANES, GLA_QK), F32).at[G_A:G_A + GLA_GATE_RANK].set(wa_up).astype(BF16)
    gbias = jnp.zeros((1, LANES), F32).at[0, G_I:G_I + MLSTM_HEADS].set(b_i).at[0, G_F:G_F + MLSTM_HEADS].set(b_f)
    full = lambda shape: pl.BlockSpec(shape, lambda b, i: (0,) * len(shape))
    return pl.pallas_call(
        _mixer0_kernel,
        out_shape=jax.ShapeDtypeStruct((bsz, s, GLA_V + ML_W), BF16),
        grid=(bsz, s // L),
        in_specs=[
            pl.BlockSpec((1, L, n), lambda b, i: (b, i, 0)),
            full((LANES, GLA_QK)),
            full((1, GLA_QK)),
            full((1, GLA_V)),
            full((MLSTM_CONV, 2 * ML_W)),
            full((1, 2 * ML_W)),
            full((1, LANES)),
            full((1, ML_W)),
        ],
        out_specs=pl.BlockSpec((1, L, GLA_V + ML_W), lambda b, i: (b, i, 0)),
        scratch_shapes=[
            pltpu.VMEM((GLA_DV, GLA_QK), F32),
            pltpu.VMEM((MLSTM_HEADS, MLSTM_DH, 2 * MLSTM_DH), F32),
            pltpu.VMEM((1, LANES), F32),
            pltpu.VMEM((8 + L, 2 * ML_W), F32),
        ],
        compiler_params=_params("parallel", "arbitrary"),
        name="mixer0",
    )(proj, wa_pad, ba.reshape(1, GLA_QK), gla_g.reshape(1, GLA_V), conv_w, conv_b.reshape(1, 2 * ML_W),
      gbias, ml_g.reshape(1, ML_W))


def _sb_kernel(q_ref, k_ref, v_ref, o_ref, qs_ref, acc_ref, car_ref, mar_ref, kn_ref):
    TQ, SUB, WIDE = SB_TQ, SB_SUB, SB_WIDE
    ns = TQ // SUB
    RB = 2 * SUB
    i = pl.program_id(2)
    lane = _iota((1, LANES), 1)
    head0 = (lane >> _HEAD_SHIFT) == 0

    @pl.when(i == 0)
    def _():
        k2 = jnp.square(k_ref[0].astype(F32))
        for h in range(2):
            n2 = jnp.sum(jnp.where(head0 if h == 0 else ~head0, k2, 0.0), axis=1, keepdims=True)
            kn_ref[h:h + 1, :] = jnp.broadcast_to(jnp.sqrt(jnp.max(n2, axis=0, keepdims=True)), (1, LANES))

    for a in range(ns):
        qa = q_ref[0, a * SUB:(a + 1) * SUB, :].astype(F32) * (SB_DH ** -0.5 * LOG2E)
        for h in range(2):
            rows = slice(a * RB + h * SUB, a * RB + (h + 1) * SUB)
            qh = jnp.where(head0 if h == 0 else ~head0, qa, 0.0).astype(BF16)
            qs_ref[rows, :] = qh
            qn = jnp.sqrt(jnp.sum(jnp.square(qh.astype(F32)), axis=1, keepdims=True))
            mar_ref[rows, :] = qn * kn_ref[h:h + 1, :] * SB_BOUND_SLACK + SB_ZERO_BITS
    tri = jnp.where(_iota((WIDE, WIDE), 0) >= _iota((WIDE, WIDE), 1), -1.0, 0.0).astype(BF16)

    def key_rows(ku, nk):
        r0 = ku * SUB
        return pl.ds(r0 if isinstance(r0, int) else pl.multiple_of(r0, SUB), nk)

    def logits(a, ku, nk):
        return _dot_nt(qs_ref[a * RB:(a + 1) * RB, :], k_ref[0, key_rows(ku, nk), :])

    def softplus2(z, diagonal):
        sp = jnp.maximum(z, 0.0) + jnp.log(1.0 + jnp.exp2(-jnp.abs(z))) * LOG2E
        if not diagonal:
            return sp, None
        mask = _iota(z.shape, 1) < (_iota(z.shape, 0) & (SUB - 1))
        return jnp.where(mask, sp, 0.0), mask

    def suffix(sp):
        nk = sp.shape[1]
        return _dot(sp.astype(BF16), tri[:nk, :nk]), jnp.sum(sp, axis=1, keepdims=True)

    def weights(z, res, car, mask):
        att = jnp.exp2(z + res if car is None else z + res - car)
        if mask is not None:
            att = jnp.where(mask, att, 0.0)
        return att.astype(BF16)

    def far_tile(a, ku):
        rows = slice(a * RB, (a + 1) * RB)
        z = logits(a, ku, SUB)
        sp, _ = softplus2(z, False)
        res, tot = suffix(sp)
        car = car_ref[rows, :]
        acc_ref[rows, :] += _dot(weights(z, res, car, None), v_ref[0, key_rows(ku, SUB), :])
        car_ref[rows, :] = car + tot

    def near_tiles(first_block):
        chains = []
        for a in range(ns):
            g = a if first_block else ns * i + a
            chain = [(a, g, SUB, True)]
            if not first_block or a >= 2:
                chain.append((a, g - 2, WIDE, False))
            elif a == 1:
                chain.append((a, 0, SUB, False))
            chains.append(chain)
        jobs = [job for step in itertools.zip_longest(*chains) for job in step if job is not None]
        zs = [logits(a, ku, nk) for a, ku, nk, _ in jobs]
        sps = [softplus2(z, diagonal) for z, (_, _, _, diagonal) in zip(zs, jobs)]
        sufs = [suffix(sp) for sp, _ in sps]
        car = [None] * ns
        atts = []
        for z, (sp, mask), (res, tot), (a, _, _, _) in zip(zs, sps, sufs, jobs):
            atts.append(weights(z, res, car[a], mask))
            car[a] = tot if car[a] is None else car[a] + tot
        acc = [None] * ns
        for att, (a, ku, nk, _) in zip(atts, jobs):
            t = _dot(att, v_ref[0, key_rows(ku, nk), :])
            acc[a] = t if acc[a] is None else acc[a] + t
        for a in range(ns):
            acc_ref[a * RB:(a + 1) * RB, :] = acc[a]
            car_ref[a * RB:(a + 1) * RB, :] = jnp.broadcast_to(car[a], (RB, LANES))

    pl.when(i == 0)(functools.partial(near_tiles, True))
    pl.when(i > 0)(functools.partial(near_tiles, False))

    def next_unit(a, t):
        return ns * i + a - 3 - t

    def pending(t):
        gap = [jnp.where(next_unit(a, t) >= 0, mar_ref[a * RB:(a + 1) * RB, :] - car_ref[a * RB:(a + 1) * RB, :], -1.0)
               for a in range(ns)]
        return jnp.max(functools.reduce(jnp.maximum, gap)) > 0.0

    def far_body(c):
        t, _ = c
        for a in range(ns):
            ku = next_unit(a, t)
            pl.when(ku >= 0)(functools.partial(far_tile, a, ku))
        return t + 1, pending(t + 1)

    lax.while_loop(lambda c: c[1], far_body, (0, pending(0)))

    for a in range(ns):
        o_ref[0, a * SUB:(a + 1) * SUB, :] = jnp.where(
            head0, acc_ref[a * RB:a * RB + SUB, :], acc_ref[a * RB + SUB:(a + 1) * RB, :]).astype(o_ref.dtype)


def _stick_breaking(pc):
    bsz, s, _ = pc.shape
    npair = SB_W // LANES
    return pl.pallas_call(
        _sb_kernel,
        out_shape=jax.ShapeDtypeStruct((bsz, s, SB_W), BF16),
        grid=(bsz, npair, s // SB_TQ),
        in_specs=[
            pl.BlockSpec((1, SB_TQ, LANES), lambda b, p, i: (b, i, p)),
            pl.BlockSpec((1, s, LANES), lambda b, p, i: (b, 0, npair + p)),
            pl.BlockSpec((1, s, LANES), lambda b, p, i: (b, 0, 2 * npair + p)),
        ],
        out_specs=pl.BlockSpec((1, SB_TQ, LANES), lambda b, p, i: (b, i, p)),
        scratch_shapes=[
            pltpu.VMEM((2 * SB_TQ, LANES), BF16),
            pltpu.VMEM((2 * SB_TQ, LANES), F32),
            pltpu.VMEM((2 * SB_TQ, LANES), F32),
            pltpu.VMEM((2 * SB_TQ, LANES), F32),
            pltpu.VMEM((8, LANES), F32),
        ],
        compiler_params=_params("parallel", "parallel", "arbitrary"),
        name="stick_breaking",
    )(pc, pc, pc)


def _t5_bucket(dist):
    max_exact = N_REL_BUCKETS // 2
    dd = np.maximum(dist, 1).astype(np.float64)
    large = max_exact + (np.log(dd / max_exact) / np.log(REL_MAX_DIST / max_exact)
                         * (N_REL_BUCKETS - max_exact)).astype(np.int32)
    large = np.minimum(large, N_REL_BUCKETS - 1)
    return np.where(dist < max_exact, dist, large).astype(np.int32)


def _dil_buckets():
    qi = np.arange(DIL_BLK)[:, None]
    kj = np.arange(2 * DIL_BLK)[None, :]
    delta = qi - kj + DIL_BLK
    in_win = (delta >= 0) & (delta <= DIL_BLK)
    tabs = []
    for window, dil in DIL_PAIRS:
        assert window // dil == DIL_BLK
        bucket = _t5_bucket(np.clip(delta, 0, None) * dil)
        tabs.append(np.where(in_win, bucket, -1).astype(np.int32))
    return np.stack(tabs, 0)


def _dil_kernel(q_ref, k_ref, v_ref, bkt_ref, bkt0_ref, tab_ref, o_ref, m_sc, l_sc, acc_sc):
    s_len = q_ref.shape[1]
    hp = pl.program_id(1)
    g = pl.program_id(2)
    blk = DIL_BLK
    ng = len(DIL_PAIRS)
    lane = _iota((1, LANES), 1)
    lane_head = lane >> _HEAD_SHIFT
    sel0 = lane_head == 0
    kcol = _iota((2 * blk, 2 * blk), 1)

    def head_biases(gi, bkt):
        biases = []
        for hl in range(2):
            head = gi * DIL_HEADS_PER_GROUP + hp * 2 + hl
            bias = jnp.full(bkt.shape, NEG_INF, F32)
            for bk in range(N_REL_BUCKETS):
                bias = jnp.where(bkt == bk, tab_ref[bk, head], bias)
            biases.append(bias)
        return jnp.concatenate(biases, axis=0)

    def stack_heads(qb):
        return jnp.concatenate([jnp.where(sel0, qb, 0.0), jnp.where(sel0, 0.0, qb)], axis=0).astype(BF16)

    def with_ones(vv):
        return jnp.concatenate([vv, jnp.ones_like(vv)], axis=1).astype(BF16)

    def first_group(gi, dil):
        n_sub = s_len // dil
        bias2 = head_biases(gi, bkt0_ref[...])

        def residue(r):
            rows = pl.ds(r, n_sub, stride=dil)
            logits = _dot_nt(stack_heads(q_ref[0, rows, :] * (DIL_DH ** -0.5)), k_ref[0, rows, :].astype(BF16)) + bias2
            return rows, logits

        def body(tt, carry):
            s1 = [residue(tt * DIL_UNROLL + u) for u in range(DIL_UNROLL)]
            s2 = [jnp.max(lg, axis=1, keepdims=True) for _, lg in s1]
            s3 = [_dot(jnp.exp(lg - mc).astype(BF16), with_ones(v_ref[0, rows, :])) for (rows, lg), mc in zip(s1, s2)]
            for (rows, _), mc, pv in zip(s1, s2, s3):
                m_sc[rows, :] = jnp.where(sel0, mc[0:n_sub], mc[n_sub:])
                l_sc[rows, :] = jnp.where(sel0, pv[0:n_sub, LANES:], pv[n_sub:, LANES:])
                acc_sc[rows, :] = jnp.where(sel0, pv[0:n_sub, :LANES], pv[n_sub:, :LANES])
            return carry

        lax.fori_loop(0, dil // DIL_UNROLL, body, 0)

    def group(gi, dil):
        nb = s_len // (blk * dil)
        bias2 = head_biases(gi, bkt_ref[gi])

        def stage_logits(t):
            r = t // nb
            n = t % nb
            q_start = r + dil * blk * n
            p_start = r + dil * blk * jnp.maximum(n - 1, 0)
            rows_q = pl.ds(q_start, blk, stride=dil) if dil > 1 else pl.ds(q_start, blk)
            rows_p = pl.ds(p_start, blk, stride=dil) if dil > 1 else pl.ds(p_start, blk)
            q2 = stack_heads(q_ref[0, rows_q, :] * (DIL_DH ** -0.5))
            kk = jnp.concatenate([k_ref[0, rows_p, :], k_ref[0, rows_q, :]], axis=0).astype(BF16)
            logits = _dot_nt(q2, kk) + bias2
            logits = jnp.where(jnp.logical_and(n == 0, kcol < blk), NEG_INF, logits)
            return rows_q, rows_p, logits

        def stage_pv(rows_q, rows_p, logits, m_col):
            p = jnp.exp(logits - m_col)
            vv = with_ones(jnp.concatenate([v_ref[0, rows_p, :], v_ref[0, rows_q, :]], axis=0))
            pv = _dot(p.astype(BF16), vv)
            return pv[:, LANES:], pv[:, :LANES]

        def body(tt, carry):
            ts = [tt * DIL_UNROLL + u for u in range(DIL_UNROLL)]
            s1 = [stage_logits(t) for t in ts]
            s2 = [jnp.max(lg, axis=1, keepdims=True) for _, _, lg in s1]
            s3 = [stage_pv(rq, rp, lg, mc) for (rq, rp, lg), mc in zip(s1, s2)]
            outs = []
            for (rows_q, _, _), m_col, (ps, pv) in zip(s1, s2, s3):
                m_old = m_sc[rows_q, :]
                m_blk = jnp.where(sel0, m_col[0:blk], m_col[blk:])
                m_new = jnp.maximum(m_old, m_blk)
                a_old = jnp.exp(m_old - m_new)
                a_blk = jnp.exp(m_blk - m_new)
                l_new = a_old * l_sc[rows_q, :] + a_blk * jnp.where(sel0, ps[0:blk], ps[blk:])
                a_new = a_old * acc_sc[rows_q, :] + a_blk * jnp.where(sel0, pv[0:blk], pv[blk:])
                outs.append((rows_q, m_new, l_new, a_new))
            for rows_q, m_new, l_new, a_new in outs:
                m_sc[rows_q, :] = m_new
                l_sc[rows_q, :] = l_new
                acc_sc[rows_q, :] = a_new
            return carry

        lax.fori_loop(0, s_len // (blk * DIL_UNROLL), body, 0)

    for gi, (_, dil) in enumerate(DIL_PAIRS):
        pl.when(g == ng - 1 - gi)(functools.partial(first_group if gi == ng - 1 else group, gi, dil))

    @pl.when(g == ng - 1)
    def _():
        o_ref[0] = (acc_sc[...] / l_sc[...]).astype(o_ref.dtype)


def _dilated(pd, rel_bias):
    bsz, s, _ = pd.shape
    ng = len(DIL_PAIRS)
    gw = DIL_HEADS_PER_GROUP * DIL_DH // LANES
    nq = DIL_W // LANES
    bkt = jnp.asarray(_dil_buckets())
    dil0 = DIL_PAIRS[-1][1]
    n_sub = s // dil0
    assert s % (DIL_BLK * dil0) == 0 and dil0 % DIL_UNROLL == 0
    delta = np.arange(n_sub)[:, None] - np.arange(n_sub)[None, :]
    bkt0 = jnp.asarray(np.where((delta >= 0) & (delta <= DIL_BLK), _t5_bucket(np.clip(delta, 0, None) * dil0), -1)
                       .astype(np.int32))
    blk_spec = lambda base: pl.BlockSpec((1, s, LANES), lambda b, p, g: (b, 0, base + (ng - 1 - g) * gw + p))
    return pl.pallas_call(
        _dil_kernel,
        out_shape=jax.ShapeDtypeStruct((bsz, s, DIL_OUT), BF16),
        grid=(bsz, gw, ng),
        in_specs=[
            blk_spec(0),
            blk_spec(nq),
            blk_spec(2 * nq),
            pl.BlockSpec((ng, DIL_BLK, 2 * DIL_BLK), lambda b, p, g: (0, 0, 0)),
            pl.BlockSpec((n_sub, n_sub), lambda b, p, g: (0, 0)),
            pl.BlockSpec(memory_space=pltpu.SMEM),
        ],
        out_specs=pl.BlockSpec((1, s, LANES), lambda b, p, g: (b, 0, p)),
        scratch_shapes=[pltpu.VMEM((s, LANES), F32)] * 3,
        compiler_params=_params("parallel", "parallel", "arbitrary"),
        name="dilated",
    )(pd, pd, pd, bkt, bkt0, rel_bias)


def _ab_weight(w):
    o = np.cumsum((0,) + (GLA_QK, GLA_QK, GLA_V, GLA_V, GLA_GATE_RANK, 2 * ML_W, ML_W, MLSTM_HEADS, MLSTM_HEADS, ML_W))
    qa, ka, va, ra, aa, qkb, vb, ib, fb, ob = [w[:, o[j]:o[j + 1]] for j in range(10)]
    pad = jnp.zeros((w.shape[0], LANES - 2 * MLSTM_HEADS - GLA_GATE_RANK), w.dtype)
    return jnp.concatenate([qa, ka, va, ra, qkb, vb, ob, ib, fb, aa, pad], axis=1).astype(BF16)


def kernel(x, c, ada_w, ada_b, ln_g, ln_b, ab_w_in, gla_wa_up, gla_ba, gla_norm_g, ml_conv_w, ml_conv_b,
           ml_b_i, ml_b_f, ml_norm_g, ab_w_out, cd_w_in, rel_bias, cd_w_out, ffn_w1, ffn_w3, ffn_w2):
    bsz, s, d = x.shape
    mod_all = _ada_mod(c, ada_w, ada_b).reshape(DEPTH, bsz, 6, d)
    tm = FFN_TM
    for layer in range(DEPTH):
        mod = mod_all[layer]
        j = layer // 2
        if layer % 2 == 0:
            (proj,) = _inproj(x, mod, _ab_weight(ab_w_in[j]), [(P0_N, F32)], tm, 0, 1, "inproj0")
            cat = _mixer0(proj, gla_wa_up[j], gla_ba[j], gla_norm_g[j], ml_conv_w[j], ml_conv_b[j],
                          ml_b_i[j], ml_b_f[j], ml_norm_g[j])
            acts, w_outs = [cat], [ab_w_out[j].astype(BF16)]
        else:
            pc, pd = _inproj(x, mod, cd_w_in[j].astype(BF16), [(3 * SB_W, BF16), (3 * DIL_W, F32)],
                             tm, 0, 1, "inproj1")
            oc = _stick_breaking(pc)
            od = _dilated(pd, rel_bias)
            w_out = cd_w_out[j].astype(BF16)
            acts, w_outs = [oc, od], [w_out[:SB_W], w_out[SB_W:]]
        x = _post(acts, w_outs, x, mod, ffn_w1[layer], ffn_w3[layer], ffn_w2[layer], ln_g[layer], ln_b[layer], FFN_TM)
    return x
```

```python
import functools
import itertools

import numpy as np
import jax
import jax.numpy as jnp
from jax import lax
from jax.experimental import pallas as pl
from jax.experimental.pallas import tpu as pltpu

F32 = jnp.float32
BF16 = jnp.bfloat16

D_MODEL = 1024
DEPTH = 2
GLA_HEADS = 4
GLA_DK = 64
GLA_DV = 128
GLA_GATE_RANK = 16
GLA_TAU = 16.0
GLA_CHUNK = 64
MLSTM_HEADS = 4
MLSTM_DH = 128
MLSTM_CONV = 4
SB_HEADS = 8
SB_DH = 64
DIL_PAIRS = ((128, 1), (512, 4), (2048, 16))
DIL_HEADS_PER_GROUP = 4
DIL_DH = 64
DIL_BLK = 128
DIL_UNROLL = 4
N_REL_BUCKETS = 32
REL_MAX_DIST = 2048
D_FF = ((8 * D_MODEL + 3 * 256 - 1) // (3 * 256)) * 256
LN_EPS = 1e-5
RES_ALPHA = (2 * DEPTH) ** 0.25
NEG_INF = -1e30

GLA_QK = GLA_HEADS * GLA_DK
GLA_V = GLA_HEADS * GLA_DV
ML_W = MLSTM_HEADS * MLSTM_DH
SB_W = SB_HEADS * SB_DH
DIL_W = len(DIL_PAIRS) * DIL_HEADS_PER_GROUP * DIL_DH
DIL_OUT = DIL_HEADS_PER_GROUP * DIL_DH

LANES = 128
VMEM_LIMIT = 56 * 1024 * 1024

P0_QA = 0
P0_KA = P0_QA + GLA_QK
P0_VA = P0_KA + GLA_QK
P0_RA = P0_VA + GLA_V
P0_QKB = P0_RA + GLA_V
P0_VB = P0_QKB + 2 * ML_W
P0_OB = P0_VB + ML_W
P0_G = P0_OB + ML_W
P0_N = P0_G + LANES
G_I = 0
G_F = MLSTM_HEADS
G_A = 2 * MLSTM_HEADS

MIX_TILE = 256
FF_CHUNK = 256
FFN_TM = 512
SB_TQ = 2048
SB_SUB = 128
SB_WIDE = 256
SB_ZERO_BITS = 160.0
SB_BOUND_SLACK = 1.01
LOG2E = 1.4426950408889634
_CHUNK_SHIFT = 6
_HEAD_SHIFT = 6
assert GLA_CHUNK == GLA_DK == 1 << _CHUNK_SHIFT and SB_DH == DIL_DH == 1 << _HEAD_SHIFT


def _dot(a, b):
    return jnp.dot(a, b, preferred_element_type=F32)


def _dot_nt(a, b):
    return lax.dot_general(a, b, (((1,), (1,)), ((), ())), preferred_element_type=F32)


def _dot_tn(a, b):
    return lax.dot_general(a, b, (((0,), (0,)), ((), ())), preferred_element_type=F32)


def _dot_split(t, x, terms):
    acc = None
    rem = x
    for i in range(terms):
        part = rem.astype(BF16)
        d = _dot(t, part)
        acc = d if acc is None else acc + d
        if i + 1 < terms:
            rem = rem - part.astype(F32)
    return acc


def _log_sigmoid(x):
    return jnp.minimum(x, 0.0) - jnp.log(1.0 + jnp.exp(-jnp.abs(x)))


def _silu(x):
    return x * jax.nn.sigmoid(x)


def _layer_norm(r, g, b):
    mu = jnp.mean(r, axis=-1, keepdims=True)
    d = r - mu
    var = jnp.mean(d * d, axis=-1, keepdims=True)
    return d * lax.rsqrt(var + LN_EPS) * g + b


def _head_norm(x, g):
    mu = jnp.mean(x, axis=-1, keepdims=True)
    d = x - mu
    var = jnp.mean(d * d, axis=-1, keepdims=True)
    return d * lax.rsqrt(var + LN_EPS) * g


def _iota(shape, dim):
    return lax.broadcasted_iota(jnp.int32, shape, dim)


def _params(*sem):
    return pltpu.CompilerParams(dimension_semantics=sem, vmem_limit_bytes=VMEM_LIMIT)


def _ada_kernel(c_ref, w_ref, b_ref, o_ref):
    ca = _silu(c_ref[...]).astype(BF16)
    o_ref[0] = _dot(ca, w_ref[0].astype(BF16)) + b_ref[0]


def _ada_mod(c, ada_w, ada_b):
    bsz, d = c.shape
    n = ada_w.shape[-1]
    tn = n // 4
    return pl.pallas_call(
        _ada_kernel,
        out_shape=jax.ShapeDtypeStruct((DEPTH, bsz, n), F32),
        grid=(DEPTH, n // tn),
        in_specs=[
            pl.BlockSpec((bsz, d), lambda l, j: (0, 0)),
            pl.BlockSpec((1, d, tn), lambda l, j: (l, 0, j)),
            pl.BlockSpec((1, 1, tn), lambda l, j: (l, 0, j)),
        ],
        out_specs=pl.BlockSpec((1, bsz, tn), lambda l, j: (l, 0, j)),
        compiler_params=_params("arbitrary", "arbitrary"),
        name="ada_mod",
    )(c, ada_w, ada_b.reshape(DEPTH, 1, n))


def _inproj_kernel(x_ref, mod_ref, w_ref, *o_refs, shift_row, scale_row):
    sh = mod_ref[0, shift_row:shift_row + 1, :]
    sc = mod_ref[0, scale_row:scale_row + 1, :]
    hm = (x_ref[0] * (1.0 + sc) + sh).astype(BF16)
    col = 0
    for o_ref in o_refs:
        n = o_ref.shape[-1]
        o_ref[0] = _dot(hm, w_ref[:, col:col + n]).astype(o_ref.dtype)
        col += n


def _inproj(x, mod, w, outs, tm, shift_row, scale_row, name):
    bsz, s, d = x.shape
    n = w.shape[1]
    assert sum(o[0] for o in outs) == n
    return pl.pallas_call(
        functools.partial(_inproj_kernel, shift_row=shift_row, scale_row=scale_row),
        out_shape=[jax.ShapeDtypeStruct((bsz, s, o[0]), o[1]) for o in outs],
        grid=(bsz, s // tm),
        in_specs=[
            pl.BlockSpec((1, tm, d), lambda b, i: (b, i, 0)),
            pl.BlockSpec((1, 6, d), lambda b, i: (b, 0, 0)),
            pl.BlockSpec((d, n), lambda b, i: (0, 0)),
        ],
        out_specs=[pl.BlockSpec((1, tm, o[0]), lambda b, i: (b, i, 0)) for o in outs],
        compiler_params=_params("parallel", "arbitrary"),
        name=name,
    )(x, mod, w)


def _post_kernel(*refs, n_act):
    act_refs = refs[:n_act]
    wo_refs = refs[n_act:2 * n_act]
    x_ref, mod_ref, w1_ref, w3_ref, w2_ref, g_ref, b_ref, o_ref, gm_ref = refs[2 * n_act:]
    y = None
    for a_ref, w_ref in zip(act_refs, wo_refs):
        t = _dot(a_ref[0], w_ref[...])
        y = t if y is None else y + t
    r = RES_ALPHA * x_ref[0] + (1.0 + mod_ref[0, 2:3, :]) * y
    xm = _layer_norm(r, g_ref[0:1, :], b_ref[0:1, :])
    hf = (xm * (1.0 + mod_ref[0, 4:5, :]) + mod_ref[0, 3:4, :]).astype(BF16)
    for c in range(D_FF // FF_CHUNK):
        cs = slice(c * FF_CHUNK, (c + 1) * FF_CHUNK)
        gm_ref[:, cs] = (_silu(_dot(hf, w1_ref[:, cs])) * _dot(hf, w3_ref[:, cs])).astype(BF16)
    y2 = _dot(gm_ref[...], w2_ref[...])
    r2 = RES_ALPHA * xm + (1.0 + mod_ref[0, 5:6, :]) * y2
    o_ref[0] = _layer_norm(r2, g_ref[1:2, :], b_ref[1:2, :])


def _post(acts, w_outs, x, mod, w1, w3, w2, ln_g, ln_b, tm):
    bsz, s, d = x.shape
    n_act = len(acts)
    const = lambda shape: pl.BlockSpec(shape, lambda b, i: (0,) * len(shape), pipeline_mode=pl.Buffered(1))
    in_specs = [pl.BlockSpec((1, tm, a.shape[-1]), lambda b, i: (b, i, 0)) for a in acts]
    in_specs += [const(w.shape) for w in w_outs]
    in_specs += [
        pl.BlockSpec((1, tm, d), lambda b, i: (b, i, 0)),
        pl.BlockSpec((1, 6, d), lambda b, i: (b, 0, 0)),
        const((d, D_FF)),
        const((d, D_FF)),
        const((D_FF, d)),
        const((2, d)),
        const((2, d)),
    ]
    return pl.pallas_call(
        functools.partial(_post_kernel, n_act=n_act),
        out_shape=jax.ShapeDtypeStruct((bsz, s, d), F32),
        grid=(bsz, s // tm),
        in_specs=in_specs,
        out_specs=pl.BlockSpec((1, tm, d), lambda b, i: (b, i, 0)),
        scratch_shapes=[pltpu.VMEM((tm, D_FF), BF16)],
        compiler_params=_params("parallel", "arbitrary"),
        name="post",
    )(*acts, *w_outs, x, mod, w1.astype(BF16), w3.astype(BF16), w2.astype(BF16), ln_g, ln_b)


def _mixer0_kernel(p_ref, wa_ref, ba_ref, gg_ref, cw_ref, cb_ref, gb_ref, mg_ref, o_ref,
                   st_ref, cst_ref, m_ref, xc_ref):
    L = MIX_TILE
    i = pl.program_id(1)

    @pl.when(i == 0)
    def _():
        st_ref[...] = jnp.zeros_like(st_ref)
        cst_ref[...] = jnp.zeros_like(cst_ref)
        m_ref[...] = jnp.zeros_like(m_ref)
        xc_ref[0:8, :] = jnp.zeros((8, 2 * ML_W), F32)

    lane = _iota((1, LANES), 1)
    g_raw = p_ref[0, :, P0_G:P0_G + LANES]

    u = _dot(g_raw.astype(BF16), wa_ref[...]) + ba_ref[...]
    la = _log_sigmoid(u) * (1.0 / GLA_TAU)
    row = _iota((L, L), 0)
    col = _iota((L, L), 1)
    same_chunk = (row >> _CHUNK_SHIFT) == (col >> _CHUNK_SHIFT)
    tri_blk = jnp.where(same_chunk & (col <= row), 1.0, 0.0).astype(BF16)
    bcs = _dot_split(tri_blk, la, 3)
    q_in = p_ref[0, :, P0_QA:P0_QA + GLA_QK] * (GLA_DK ** -0.5) * jnp.exp(bcs)
    k_raw = p_ref[0, :, P0_KA:P0_KA + GLA_QK]
    k_in = k_raw * jnp.exp(-bcs)
    lane_qk = _iota((1, GLA_QK), 1)
    head_masks = [(lane_qk >> _CHUNK_SHIFT) == h for h in range(GLA_HEADS)]
    r4 = _iota((GLA_HEADS * GLA_CHUNK, GLA_CHUNK), 0)
    c4 = _iota((GLA_HEADS * GLA_CHUNK, GLA_CHUNK), 1)
    tril4 = c4 <= (r4 & (GLA_CHUNK - 1))
    oa_chunks = []
    for c in range(L // GLA_CHUNK):
        r0, r1 = c * GLA_CHUNK, (c + 1) * GLA_CHUNK
        b_c = bcs[r0:r1]
        bl = b_c[GLA_CHUNK - 1:GLA_CHUNK, :]
        q_c = q_in[r0:r1]
        k_c = k_in[r0:r1].astype(BF16)
        k_end = k_raw[r0:r1] * jnp.exp(bl - b_c)
        v_c = p_ref[0, r0:r1, P0_VA:P0_VA + GLA_V]
        q_exp = jnp.concatenate([jnp.where(hm, q_c, 0.0) for hm in head_masks], axis=0).astype(BF16)
        k_exp = jnp.concatenate([jnp.where(hm, k_end, 0.0) for hm in head_masks], axis=0).astype(BF16)
        sc = jnp.where(tril4, _dot_nt(q_exp, k_c), 0.0)
        intra = _dot(sc.astype(BF16), v_c.astype(BF16))
        inter = _dot_nt(q_exp, st_ref[...].astype(BF16))
        o_heads = []
        for h in range(GLA_HEADS):
            h0, h1 = h * GLA_CHUNK, (h + 1) * GLA_CHUNK
            o_heads.append(intra[h0:h1, h * GLA_DV:(h + 1) * GLA_DV] + inter[h0:h1])
        oa_chunks.append(o_heads)
        v_cat = jnp.concatenate([v_c[:, h * GLA_DV:(h + 1) * GLA_DV] for h in range(GLA_HEADS)], axis=0)
        st_ref[...] = st_ref[...] * jnp.exp(bl) + _dot_tn(v_cat.astype(BF16), k_exp)
    for h in range(GLA_HEADS):
        o_h = jnp.concatenate([oc[h] for oc in oa_chunks], axis=0)
        cs = slice(h * GLA_DV, (h + 1) * GLA_DV)
        ra = p_ref[0, :, P0_RA + h * GLA_DV:P0_RA + (h + 1) * GLA_DV]
        o_ref[0, :, cs] = (_head_norm(o_h, gg_ref[:, cs]) * _silu(ra)).astype(o_ref.dtype)

    xc_ref[8:8 + L, :] = p_ref[0, :, P0_QKB:P0_QKB + 2 * ML_W]
    conv = cb_ref[...]
    for kk in range(MLSTM_CONV):
        conv = conv + cw_ref[kk:kk + 1, :] * xc_ref[8 - (MLSTM_CONV - 1) + kk:8 - (MLSTM_CONV - 1) + kk + L, :]
    xc_ref[0:8, :] = xc_ref[L:L + 8, :]
    qk_b = _silu(conv)

    gb = jnp.where(lane < G_A, g_raw + gb_ref[...], 0.0)
    ipre = gb
    logf = _log_sigmoid(pltpu.roll(gb, LANES - G_F, axis=1))
    logf = jnp.where(lane < MLSTM_HEADS, logf, 0.0)
    tri = jnp.where(col <= row, 1.0, 0.0).astype(BF16)
    bcum = _dot_split(tri, logf, 3)
    blast = bcum[L - 1:L, :]
    wend = blast - bcum + ipre
    m_prev = m_ref[...]
    m_new = jnp.maximum(blast + m_prev, jnp.max(wend, axis=0, keepdims=True))
    scl = jnp.exp(blast + m_prev - m_new)
    wj = jnp.exp(wend - m_new)
    rows_src = jnp.where(lane < MLSTM_HEADS, ipre, pltpu.roll(bcum, MLSTM_HEADS, axis=1))
    rows_t = rows_src.T
    causal = col <= row
    ones = jnp.ones((L, MLSTM_DH), F32)
    for h in range(MLSTM_HEADS):
        hs = slice(h * MLSTM_DH, (h + 1) * MLSTM_DH)
        wide = lambda t: jnp.concatenate([t, t], axis=1)
        b_rep = jnp.broadcast_to(bcum[:, h:h + 1], (L, LANES))
        wj_rep = jnp.broadcast_to(wj[:, h:h + 1], (L, LANES))
        ip_row = rows_t[h:h + 1, :]
        b_row = rows_t[MLSTM_HEADS + h:MLSTM_HEADS + h + 1, :]
        dlog = jnp.where(causal, wide(b_rep) - b_row + ip_row, NEG_INF)
        il = b_rep + m_prev[:, h:h + 1]
        m_i = jnp.maximum(il, jnp.broadcast_to(jnp.max(dlog, axis=1, keepdims=True), (L, LANES)))
        w_intra = jnp.exp(dlog - wide(m_i))
        s_inter = jnp.exp(il - m_i)
        q_h = qk_b[:, h * MLSTM_DH:(h + 1) * MLSTM_DH].astype(BF16)
        k_h = (qk_b[:, ML_W + h * MLSTM_DH:ML_W + (h + 1) * MLSTM_DH] * (MLSTM_DH ** -0.5)).astype(BF16)
        v_h = p_ref[0, :, P0_VB + h * MLSTM_DH:P0_VB + (h + 1) * MLSTM_DH]
        v_aug = jnp.concatenate([v_h, ones], axis=1)
        a = (w_intra * _dot_nt(q_h, k_h)).astype(BF16)
        c_prev = cst_ref[h]
        tot = _dot(a, v_aug.astype(BF16)) + wide(s_inter) * _dot(q_h, c_prev.astype(BF16))
        num = tot[:, :MLSTM_DH]
        den = tot[:, MLSTM_DH:]
        hid = num / jnp.maximum(jnp.abs(den), jnp.exp(-m_i))
        w_aug = (v_aug * wide(wj_rep)).astype(BF16)
        cst_ref[h] = scl[:, h:h + 1] * c_prev + _dot_tn(k_h, w_aug)
        ob = p_ref[0, :, P0_OB + h * MLSTM_DH:P0_OB + (h + 1) * MLSTM_DH]
        o_ref[0, :, GLA_V + h * MLSTM_DH:GLA_V + (h + 1) * MLSTM_DH] = (
            jax.nn.sigmoid(ob) * _head_norm(hid, mg_ref[:, hs])).astype(o_ref.dtype)
    m_ref[...] = m_new


def _mixer0(proj, wa_up, ba, gla_g, conv_w, conv_b, b_i, b_f, ml_g):
    bsz, s, n = proj.shape
    L = MIX_TILE
    wa_pad = jnp.zeros((LANES, GLA_QK), F32).at[G_A:G_A + GLA_GATE_RANK].set(wa_up).astype(BF16)
    gbias = jnp.zeros((1, LANES), F32).at[0, G_I:G_I + MLSTM_HEADS].set(b_i).at[0, G_F:G_F + MLSTM_HEADS].set(b_f)
    full = lambda shape: pl.BlockSpec(shape, lambda b, i: (0,) * len(shape))
    return pl.pallas_call(
        _mixer0_kernel,
        out_shape=jax.ShapeDtypeStruct((bsz, s, GLA_V + ML_W), BF16),
        grid=(bsz, s // L),
        in_specs=[
            pl.BlockSpec((1, L, n), lambda b, i: (b, i, 0)),
            full((LANES, GLA_QK)),
            full((1, GLA_QK)),
            full((1, GLA_V)),
            full((MLSTM_CONV, 2 * ML_W)),
            full((1, 2 * ML_W)),
            full((1, LANES)),
            full((1, ML_W)),
        ],
        out_specs=pl.BlockSpec((1, L, GLA_V + ML_W), lambda b, i: (b, i, 0)),
        scratch_shapes=[
            pltpu.VMEM((GLA_DV, GLA_QK), F32),
            pltpu.VMEM((MLSTM_HEADS, MLSTM_DH, 2 * MLSTM_DH), F32),
            pltpu.VMEM((1, LANES), F32),
            pltpu.VMEM((8 + L, 2 * ML_W), F32),
        ],
        compiler_params=_params("parallel", "arbitrary"),
        name="mixer0",
    )(proj, wa_pad, ba.reshape(1, GLA_QK), gla_g.reshape(1, GLA_V), conv_w, conv_b.reshape(1, 2 * ML_W),
      gbias, ml_g.reshape(1, ML_W))


def _sb_kernel(q_ref, k_ref, v_ref, o_ref, qs_ref, acc_ref, car_ref, mar_ref, kn_ref):
    TQ, SUB, WIDE = SB_TQ, SB_SUB, SB_WIDE
    ns = TQ // SUB
    RB = 2 * SUB
    i = pl.program_id(2)
    lane = _iota((1, LANES), 1)
    head0 = (lane >> _HEAD_SHIFT) == 0

    @pl.when(i == 0)
    def _():
        k2 = jnp.square(k_ref[0].astype(F32))
        for h in range(2):
            n2 = jnp.sum(jnp.where(head0 if h == 0 else ~head0, k2, 0.0), axis=1, keepdims=True)
            kn_ref[h:h + 1, :] = jnp.broadcast_to(jnp.sqrt(jnp.max(n2, axis=0, keepdims=True)), (1, LANES))

    for a in range(ns):
        qa = q_ref[0, a * SUB:(a + 1) * SUB, :].astype(F32) * (SB_DH ** -0.5 * LOG2E)
        for h in range(2):
            rows = slice(a * RB + h * SUB, a * RB + (h + 1) * SUB)
            qh = jnp.where(head0 if h == 0 else ~head0, qa, 0.0).astype(BF16)
            qs_ref[rows, :] = qh
            qn = jnp.sqrt(jnp.sum(jnp.square(qh.astype(F32)), axis=1, keepdims=True))
            mar_ref[rows, :] = qn * kn_ref[h:h + 1, :] * SB_BOUND_SLACK + SB_ZERO_BITS
    tri = jnp.where(_iota((WIDE, WIDE), 0) >= _iota((WIDE, WIDE), 1), -1.0, 0.0).astype(BF16)

    def key_rows(ku, nk):
        r0 = ku * SUB
        return pl.ds(r0 if isinstance(r0, int) else pl.multiple_of(r0, SUB), nk)

    def logits(a, ku, nk):
        return _dot_nt(qs_ref[a * RB:(a + 1) * RB, :], k_ref[0, key_rows(ku, nk), :])

    def softplus2(z, diagonal):
        sp = jnp.maximum(z, 0.0) + jnp.log(1.0 + jnp.exp2(-jnp.abs(z))) * LOG2E
        if not diagonal:
            return sp, None
        mask = _iota(z.shape, 1) < (_iota(z.shape, 0) & (SUB - 1))
        return jnp.where(mask, sp, 0.0), mask

    def suffix(sp):
        nk = sp.shape[1]
        return _dot(sp.astype(BF16), tri[:nk, :nk]), jnp.sum(sp, axis=1, keepdims=True)

    def weights(z, res, car, mask):
        att = jnp.exp2(z + res if car is None else z + res - car)
        if mask is not None:
            att = jnp.where(mask, att, 0.0)
        return att.astype(BF16)

    def far_tile(a, ku):
        rows = slice(a * RB, (a + 1) * RB)
        z = logits(a, ku, SUB)
        sp, _ = softplus2(z, False)
        res, tot = suffix(sp)
        car = car_ref[rows, :]
        acc_ref[rows, :] += _dot(weights(z, res, car, None), v_ref[0, key_rows(ku, SUB), :])
        car_ref[rows, :] = car + tot

    def near_tiles(first_block):
        chains = []
        for a in range(ns):
            g = a if first_block else ns * i + a
            chain = [(a, g, SUB, True)]
            if not first_block or a >= 2:
                chain.append((a, g - 2, WIDE, False))
            elif a == 1:
                chain.append((a, 0, SUB, False))
            chains.append(chain)
        jobs = [job for step in itertools.zip_longest(*chains) for job in step if job is not None]
        zs = [logits(a, ku, nk) for a, ku, nk, _ in jobs]
        sps = [softplus2(z, diagonal) for z, (_, _, _, diagonal) in zip(zs, jobs)]
        sufs = [suffix(sp) for sp, _ in sps]
        car = [None] * ns
        atts = []
        for z, (sp, mask), (res, tot), (a, _, _, _) in zip(zs, sps, sufs, jobs):
            atts.append(weights(z, res, car[a], mask))
            car[a] = tot if car[a] is None else car[a] + tot
        acc = [None] * ns
        for att, (a, ku, nk, _) in zip(atts, jobs):
            t = _dot(att, v_ref[0, key_rows(ku, nk), :])
            acc[a] = t if acc[a] is None else acc[a] + t
        for a in range(ns):
            acc_ref[a * RB:(a + 1) * RB, :] = acc[a]
            car_ref[a * RB:(a + 1) * RB, :] = jnp.broadcast_to(car[a], (RB, LANES))

    pl.when(i == 0)(functools.partial(near_tiles, True))
    pl.when(i > 0)(functools.partial(near_tiles, False))

    def next_unit(a, t):
        return ns * i + a - 3 - t

    def pending(t):
        gap = [jnp.where(next_unit(a, t) >= 0, mar_ref[a * RB:(a + 1) * RB, :] - car_ref[a * RB:(a + 1) * RB, :], -1.0)
               for a in range(ns)]
        return jnp.max(functools.reduce(jnp.maximum, gap)) > 0.0

    def far_body(c):
        t, _ = c
        for a in range(ns):
            ku = next_unit(a, t)
            pl.when(ku >= 0)(functools.partial(far_tile, a, ku))
        return t + 1, pending(t + 1)

    lax.while_loop(lambda c: c[1], far_body, (0, pending(0)))

    for a in range(ns):
        o_ref[0, a * SUB:(a + 1) * SUB, :] = jnp.where(
            head0, acc_ref[a * RB:a * RB + SUB, :], acc_ref[a * RB + SUB:(a + 1) * RB, :]).astype(o_ref.dtype)


def _stick_breaking(pc):
    bsz, s, _ = pc.shape
    npair = SB_W // LANES
    return pl.pallas_call(
        _sb_kernel,
        out_shape=jax.ShapeDtypeStruct((bsz, s, SB_W), BF16),
        grid=(bsz, npair, s // SB_TQ),
        in_specs=[
            pl.BlockSpec((1, SB_TQ, LANES), lambda b, p, i: (b, i, p)),
            pl.BlockSpec((1, s, LANES), lambda b, p, i: (b, 0, npair + p)),
            pl.BlockSpec((1, s, LANES), lambda b, p, i: (b, 0, 2 * npair + p)),
        ],
        out_specs=pl.BlockSpec((1, SB_TQ, LANES), lambda b, p, i: (b, i, p)),
        scratch_shapes=[
            pltpu.VMEM((2 * SB_TQ, LANES), BF16),
            pltpu.VMEM((2 * SB_TQ, LANES), F32),
            pltpu.VMEM((2 * SB_TQ, LANES), F32),
            pltpu.VMEM((2 * SB_TQ, LANES), F32),
            pltpu.VMEM((8, LANES), F32),
        ],
        compiler_params=_params("parallel", "parallel", "arbitrary"),
        name="stick_breaking",
    )(pc, pc, pc)


def _t5_bucket(dist):
    max_exact = N_REL_BUCKETS // 2
    dd = np.maximum(dist, 1).astype(np.float64)
    large = max_exact + (np.log(dd / max_exact) / np.log(REL_MAX_DIST / max_exact)
                         * (N_REL_BUCKETS - max_exact)).astype(np.int32)
    large = np.minimum(large, N_REL_BUCKETS - 1)
    return np.where(dist < max_exact, dist, large).astype(np.int32)


def _dil_buckets():
    qi = np.arange(DIL_BLK)[:, None]
    kj = np.arange(2 * DIL_BLK)[None, :]
    delta = qi - kj + DIL_BLK
    in_win = (delta >= 0) & (delta <= DIL_BLK)
    tabs = []
    for window, dil in DIL_PAIRS:
        assert window // dil == DIL_BLK
        bucket = _t5_bucket(np.clip(delta, 0, None) * dil)
        tabs.append(np.where(in_win, bucket, -1).astype(np.int32))
    return np.stack(tabs, 0)


def _dil_kernel(q_ref, k_ref, v_ref, bkt_ref, bkt0_ref, tab_ref, o_ref, m_sc, l_sc, acc_sc):
    s_len = q_ref.shape[1]
    hp = pl.program_id(1)
    g = pl.program_id(2)
    blk = DIL_BLK
    ng = len(DIL_PAIRS)
    lane = _iota((1, LANES), 1)
    lane_head = lane >> _HEAD_SHIFT
    sel0 = lane_head == 0
    kcol = _iota((2 * blk, 2 * blk), 1)

    def head_biases(gi, bkt):
        biases = []
        for hl in range(2):
            head = gi * DIL_HEADS_PER_GROUP + hp * 2 + hl
            bias = jnp.full(bkt.shape, NEG_INF, F32)
            for bk in range(N_REL_BUCKETS):
                bias = jnp.where(bkt == bk, tab_ref[bk, head], bias)
            biases.append(bias)
        return jnp.concatenate(biases, axis=0)

    def stack_heads(qb):
        return jnp.concatenate([jnp.where(sel0, qb, 0.0), jnp.where(sel0, 0.0, qb)], axis=0).astype(BF16)

    def with_ones(vv):
        return jnp.concatenate([vv, jnp.ones_like(vv)], axis=1).astype(BF16)

    def first_group(gi, dil):
        n_sub = s_len // dil
        bias2 = head_biases(gi, bkt0_ref[...])

        def residue(r):
            rows = pl.ds(r, n_sub, stride=dil)
            logits = _dot_nt(stack_heads(q_ref[0, rows, :] * (DIL_DH ** -0.5)), k_ref[0, rows, :].astype(BF16)) + bias2
            return rows, logits

        def body(tt, carry):
            s1 = [residue(tt * DIL_UNROLL + u) for u in range(DIL_UNROLL)]
            s2 = [jnp.max(lg, axis=1, keepdims=True) for _, lg in s1]
            s3 = [_dot(jnp.exp(lg - mc).astype(BF16), with_ones(v_ref[0, rows, :])) for (rows, lg), mc in zip(s1, s2)]
            for (rows, _), mc, pv in zip(s1, s2, s3):
                m_sc[rows, :] = jnp.where(sel0, mc[0:n_sub], mc[n_sub:])
                l_sc[rows, :] = jnp.where(sel0, pv[0:n_sub, LANES:], pv[n_sub:, LANES:])
                acc_sc[rows, :] = jnp.where(sel0, pv[0:n_sub, :LANES], pv[n_sub:, :LANES])
            return carry

        lax.fori_loop(0, dil // DIL_UNROLL, body, 0)

    def group(gi, dil):
        nb = s_len // (blk * dil)
        bias2 = head_biases(gi, bkt_ref[gi])

        def stage_logits(t):
            r = t // nb
            n = t % nb
            q_start = r + dil * blk * n
            p_start = r + dil * blk * jnp.maximum(n - 1, 0)
            rows_q = pl.ds(q_start, blk, stride=dil) if dil > 1 else pl.ds(q_start, blk)
            rows_p = pl.ds(p_start, blk, stride=dil) if dil > 1 else pl.ds(p_start, blk)
            q2 = stack_heads(q_ref[0, rows_q, :] * (DIL_DH ** -0.5))
            kk = jnp.concatenate([k_ref[0, rows_p, :], k_ref[0, rows_q, :]], axis=0).astype(BF16)
            logits = _dot_nt(q2, kk) + bias2
            logits = jnp.where(jnp.logical_and(n == 0, kcol < blk), NEG_INF, logits)
            return rows_q, rows_p, logits

        def stage_pv(rows_q, rows_p, logits, m_col):
            p = jnp.exp(logits - m_col)
            vv = with_ones(jnp.concatenate([v_ref[0, rows_p, :], v_ref[0, rows_q, :]], axis=0))
            pv = _dot(p.astype(BF16), vv)
            return pv[:, LANES:], pv[:, :LANES]

        def body(tt, carry):
            ts = [tt * DIL_UNROLL + u for u in range(DIL_UNROLL)]
            s1 = [stage_logits(t) for t in ts]
            s2 = [jnp.max(lg, axis=1, keepdims=True) for _, _, lg in s1]
            s3 = [stage_pv(rq, rp, lg, mc) for (rq, rp, lg), mc in zip(s1, s2)]
            outs = []
            for (rows_q, _, _), m_col, (ps, pv) in zip(s1, s2, s3):
                m_old = m_sc[rows_q, :]
                m_blk = jnp.where(sel0, m_col[0:blk], m_col[blk:])
                m_new = jnp.maximum(m_old, m_blk)
                a_old = jnp.exp(m_old - m_new)
                a_blk = jnp.exp(m_blk - m_new)
                l_new = a_old * l_sc[rows_q, :] + a_blk * jnp.where(sel0, ps[0:blk], ps[blk:])
                a_new = a_old * acc_sc[rows_q, :] + a_blk * jnp.where(sel0, pv[0:blk], pv[blk:])
                outs.append((rows_q, m_new, l_new, a_new))
            for rows_q, m_new, l_new, a_new in outs:
                m_sc[rows_q, :] = m_new
                l_sc[rows_q, :] = l_new
                acc_sc[rows_q, :] = a_new
            return carry

        lax.fori_loop(0, s_len // (blk * DIL_UNROLL), body, 0)

    for gi, (_, dil) in enumerate(DIL_PAIRS):
        pl.when(g == ng - 1 - gi)(functools.partial(first_group if gi == ng - 1 else group, gi, dil))

    @pl.when(g == ng - 1)
    def _():
        o_ref[0] = (acc_sc[...] / l_sc[...]).astype(o_ref.dtype)


def _dilated(pd, rel_bias):
    bsz, s, _ = pd.shape
    ng = len(DIL_PAIRS)
    gw = DIL_HEADS_PER_GROUP * DIL_DH // LANES
    nq = DIL_W // LANES
    bkt = jnp.asarray(_dil_buckets())
    dil0 = DIL_PAIRS[-1][1]
    n_sub = s // dil0
    assert s % (DIL_BLK * dil0) == 0 and dil0 % DIL_UNROLL == 0
    delta = np.arange(n_sub)[:, None] - np.arange(n_sub)[None, :]
    bkt0 = jnp.asarray(np.where((delta >= 0) & (delta <= DIL_BLK), _t5_bucket(np.clip(delta, 0, None) * dil0), -1)
                       .astype(np.int32))
    blk_spec = lambda base: pl.BlockSpec((1, s, LANES), lambda b, p, g: (b, 0, base + (ng - 1 - g) * gw + p))
    return pl.pallas_call(
        _dil_kernel,
        out_shape=jax.ShapeDtypeStruct((bsz, s, DIL_OUT), BF16),
        grid=(bsz, gw, ng),
        in_specs=[
            blk_spec(0),
            blk_spec(nq),
            blk_spec(2 * nq),
            pl.BlockSpec((ng, DIL_BLK, 2 * DIL_BLK), lambda b, p, g: (0, 0, 0)),
            pl.BlockSpec((n_sub, n_sub), lambda b, p, g: (0, 0)),
            pl.BlockSpec(memory_space=pltpu.SMEM),
        ],
        out_specs=pl.BlockSpec((1, s, LANES), lambda b, p, g: (b, 0, p)),
        scratch_shapes=[pltpu.VMEM((s, LANES), F32)] * 3,
        compiler_params=_params("parallel", "parallel", "arbitrary"),
        name="dilated",
    )(pd, pd, pd, bkt, bkt0, rel_bias)


def _ab_weight(w):
    o = np.cumsum((0,) + (GLA_QK, GLA_QK, GLA_V, GLA_V, GLA_GATE_RANK, 2 * ML_W, ML_W, MLSTM_HEADS, MLSTM_HEADS, ML_W))
    qa, ka, va, ra, aa, qkb, vb, ib, fb, ob = [w[:, o[j]:o[j + 1]] for j in range(10)]
    pad = jnp.zeros((w.shape[0], LANES - 2 * MLSTM_HEADS - GLA_GATE_RANK), w.dtype)
    return jnp.concatenate([qa, ka, va, ra, qkb, vb, ob, ib, fb, aa, pad], axis=1).astype(BF16)


def kernel(x, c, ada_w, ada_b, ln_g, ln_b, ab_w_in, gla_wa_up, gla_ba, gla_norm_g, ml_conv_w, ml_conv_b,
           ml_b_i, ml_b_f, ml_norm_g, ab_w_out, cd_w_in, rel_bias, cd_w_out, ffn_w1, ffn_w3, ffn_w2):
    bsz, s, d = x.shape
    mod_all = _ada_mod(c, ada_w, ada_b).reshape(DEPTH, bsz, 6, d)
    tm = FFN_TM
    for layer in range(DEPTH):
        mod = mod_all[layer]
        j = layer // 2
        if layer % 2 == 0:
            (proj,) = _inproj(x, mod, _ab_weight(ab_w_in[j]), [(P0_N, F32)], tm, 0, 1, "inproj0")
            cat = _mixer0(proj, gla_wa_up[j], gla_ba[j], gla_norm_g[j], ml_conv_w[j], ml_conv_b[j],
                          ml_b_i[j], ml_b_f[j], ml_norm_g[j])
            acts, w_outs = [cat], [ab_w_out[j].astype(BF16)]
        else:
            pc, pd = _inproj(x, mod, cd_w_in[j].astype(BF16), [(3 * SB_W, BF16), (3 * DIL_W, F32)],
                             tm, 0, 1, "inproj1")
            oc = _stick_breaking(pc)
            od = _dilated(pd, rel_bias)
            w_out = cd_w_out[j].astype(BF16)
            acts, w_outs = [oc, od], [w_out[:SB_W], w_out[SB_W:]]
        x = _post(acts, w_outs, x, mod, ffn_w1[layer], ffn_w3[layer], ffn_w2[layer], ln_g[layer], ln_b[layer], FFN_TM)
    return x
```

```python
import functools
import itertools

import numpy as np
import jax
import jax.numpy as jnp
from jax import lax
from jax.experimental import pallas as pl
from jax.experimental.pallas import tpu as pltpu

F32 = jnp.float32
BF16 = jnp.bfloat16

D_MODEL = 1024
DEPTH = 2
GLA_HEADS = 4
GLA_DK = 64
GLA_DV = 128
GLA_GATE_RANK = 16
GLA_TAU = 16.0
GLA_CHUNK = 64
MLSTM_HEADS = 4
MLSTM_DH = 128
MLSTM_CONV = 4
SB_HEADS = 8
SB_DH = 64
DIL_PAIRS = ((128, 1), (512, 4), (2048, 16))
DIL_HEADS_PER_GROUP = 4
DIL_DH = 64
DIL_BLK = 128
DIL_UNROLL = 4
N_REL_BUCKETS = 32
REL_MAX_DIST = 2048
D_FF = ((8 * D_MODEL + 3 * 256 - 1) // (3 * 256)) * 256
LN_EPS = 1e-5
RES_ALPHA = (2 * DEPTH) ** 0.25
NEG_INF = -1e30

GLA_QK = GLA_HEADS * GLA_DK
GLA_V = GLA_HEADS * GLA_DV
ML_W = MLSTM_HEADS * MLSTM_DH
SB_W = SB_HEADS * SB_DH
DIL_W = len(DIL_PAIRS) * DIL_HEADS_PER_GROUP * DIL_DH
DIL_OUT = DIL_HEADS_PER_GROUP * DIL_DH

LANES = 128
VMEM_LIMIT = 56 * 1024 * 1024

P0_QA = 0
P0_KA = P0_QA + GLA_QK
P0_VA = P0_KA + GLA_QK
P0_RA = P0_VA + GLA_V
P0_QKB = P0_RA + GLA_V
P0_VB = P0_QKB + 2 * ML_W
P0_OB = P0_VB + ML_W
P0_G = P0_OB + ML_W
P0_N = P0_G + LANES
G_I = 0
G_F = MLSTM_HEADS
G_A = 2 * MLSTM_HEADS

MIX_TILE = 256
FF_CHUNK = 256
FFN_TM = 512
SB_TQ = 2048
SB_SUB = 128
SB_WIDE = 256
SB_ZERO_BITS = 160.0
SB_BOUND_SLACK = 1.01
LOG2E = 1.4426950408889634
_CHUNK_SHIFT = 6
_HEAD_SHIFT = 6
assert GLA_CHUNK == GLA_DK == 1 << _CHUNK_SHIFT and SB_DH == DIL_DH == 1 << _HEAD_SHIFT


def _dot(a, b):
    return jnp.dot(a, b, preferred_element_type=F32)


def _dot_nt(a, b):
    return lax.dot_general(a, b, (((1,), (1,)), ((), ())), preferred_element_type=F32)


def _dot_tn(a, b):
    return lax.dot_general(a, b, (((0,), (0,)), ((), ())), preferred_element_type=F32)


def _dot_split(t, x, terms):
    acc = None
    rem = x
    for i in range(terms):
        part = rem.astype(BF16)
        d = _dot(t, part)
        acc = d if acc is None else acc + d
        if i + 1 < terms:
            rem = rem - part.astype(F32)
    return acc


def _log_sigmoid(x):
    return jnp.minimum(x, 0.0) - jnp.log(1.0 + jnp.exp(-jnp.abs(x)))


def _silu(x):
    return x * jax.nn.sigmoid(x)


def _layer_norm(r, g, b):
    mu = jnp.mean(r, axis=-1, keepdims=True)
    d = r - mu
    var = jnp.mean(d * d, axis=-1, keepdims=True)
    return d * lax.rsqrt(var + LN_EPS) * g + b


def _head_norm(x, g):
    mu = jnp.mean(x, axis=-1, keepdims=True)
    d = x - mu
    var = jnp.mean(d * d, axis=-1, keepdims=True)
    return d * lax.rsqrt(var + LN_EPS) * g


def _iota(shape, dim):
    return lax.broadcasted_iota(jnp.int32, shape, dim)


def _params(*sem):
    return pltpu.CompilerParams(dimension_semantics=sem, vmem_limit_bytes=VMEM_LIMIT)


def _ada_kernel(c_ref, w_ref, b_ref, o_ref):
    ca = _silu(c_ref[...]).astype(BF16)
    o_ref[0] = _dot(ca, w_ref[0].astype(BF16)) + b_ref[0]


def _ada_mod(c, ada_w, ada_b):
    bsz, d = c.shape
    n = ada_w.shape[-1]
    tn = n // 4
    return pl.pallas_call(
        _ada_kernel,
        out_shape=jax.ShapeDtypeStruct((DEPTH, bsz, n), F32),
        grid=(DEPTH, n // tn),
        in_specs=[
            pl.BlockSpec((bsz, d), lambda l, j: (0, 0)),
            pl.BlockSpec((1, d, tn), lambda l, j: (l, 0, j)),
            pl.BlockSpec((1, 1, tn), lambda l, j: (l, 0, j)),
        ],
        out_specs=pl.BlockSpec((1, bsz, tn), lambda l, j: (l, 0, j)),
        compiler_params=_params("arbitrary", "arbitrary"),
        name="ada_mod",
    )(c, ada_w, ada_b.reshape(DEPTH, 1, n))


def _inproj_kernel(x_ref, mod_ref, w_ref, *o_refs, shift_row, scale_row):
    sh = mod_ref[0, shift_row:shift_row + 1, :]
    sc = mod_ref[0, scale_row:scale_row + 1, :]
    hm = (x_ref[0] * (1.0 + sc) + sh).astype(BF16)
    col = 0
    for o_ref in o_refs:
        n = o_ref.shape[-1]
        o_ref[0] = _dot(hm, w_ref[:, col:col + n]).astype(o_ref.dtype)
        col += n


def _inproj(x, mod, w, outs, tm, shift_row, scale_row, name):
    bsz, s, d = x.shape
    n = w.shape[1]
    assert sum(o[0] for o in outs) == n
    return pl.pallas_call(
        functools.partial(_inproj_kernel, shift_row=shift_row, scale_row=scale_row),
        out_shape=[jax.ShapeDtypeStruct((bsz, s, o[0]), o[1]) for o in outs],
        grid=(bsz, s // tm),
        in_specs=[
            pl.BlockSpec((1, tm, d), lambda b, i: (b, i, 0)),
            pl.BlockSpec((1, 6, d), lambda b, i: (b, 0, 0)),
            pl.BlockSpec((d, n), lambda b, i: (0, 0)),
        ],
        out_specs=[pl.BlockSpec((1, tm, o[0]), lambda b, i: (b, i, 0)) for o in outs],
        compiler_params=_params("parallel", "arbitrary"),
        name=name,
    )(x, mod, w)


def _post_kernel(*refs, n_act):
    act_refs = refs[:n_act]
    wo_refs = refs[n_act:2 * n_act]
    x_ref, mod_ref, w1_ref, w3_ref, w2_ref, g_ref, b_ref, o_ref, gm_ref = refs[2 * n_act:]
    y = None
    for a_ref, w_ref in zip(act_refs, wo_refs):
        t = _dot(a_ref[0], w_ref[...])
        y = t if y is None else y + t
    r = RES_ALPHA * x_ref[0] + (1.0 + mod_ref[0, 2:3, :]) * y
    xm = _layer_norm(r, g_ref[0:1, :], b_ref[0:1, :])
    hf = (xm * (1.0 + mod_ref[0, 4:5, :]) + mod_ref[0, 3:4, :]).astype(BF16)
    for c in range(D_FF // FF_CHUNK):
        cs = slice(c * FF_CHUNK, (c + 1) * FF_CHUNK)
        gm_ref[:, cs] = (_silu(_dot(hf, w1_ref[:, cs])) * _dot(hf, w3_ref[:, cs])).astype(BF16)
    y2 = _dot(gm_ref[...], w2_ref[...])
    r2 = RES_ALPHA * xm + (1.0 + mod_ref[0, 5:6, :]) * y2
    o_ref[0] = _layer_norm(r2, g_ref[1:2, :], b_ref[1:2, :])


def _post(acts, w_outs, x, mod, w1, w3, w2, ln_g, ln_b, tm):
    bsz, s, d = x.shape
    n_act = len(acts)
    const = lambda shape: pl.BlockSpec(shape, lambda b, i: (0,) * len(shape), pipeline_mode=pl.Buffered(1))
    in_specs = [pl.BlockSpec((1, tm, a.shape[-1]), lambda b, i: (b, i, 0)) for a in acts]
    in_specs += [const(w.shape) for w in w_outs]
    in_specs += [
        pl.BlockSpec((1, tm, d), lambda b, i: (b, i, 0)),
        pl.BlockSpec((1, 6, d), lambda b, i: (b, 0, 0)),
        const((d, D_FF)),
        const((d, D_FF)),
        const((D_FF, d)),
        const((2, d)),
        const((2, d)),
    ]
    return pl.pallas_call(
        functools.partial(_post_kernel, n_act=n_act),
        out_shape=jax.ShapeDtypeStruct((bsz, s, d), F32),
        grid=(bsz, s // tm),
        in_specs=in_specs,
        out_specs=pl.BlockSpec((1, tm, d), lambda b, i: (b, i, 0)),
        scratch_shapes=[pltpu.VMEM((tm, D_FF), BF16)],
        compiler_params=_params("parallel", "arbitrary"),
        name="post",
    )(*acts, *w_outs, x, mod, w1.astype(BF16), w3.astype(BF16), w2.astype(BF16), ln_g, ln_b)


def _mixer0_kernel(p_ref, wa_ref, ba_ref, gg_ref, cw_ref, cb_ref, gb_ref, mg_ref, o_ref,
                   st_ref, cst_ref, m_ref, xc_ref):
    L = MIX_TILE
    i = pl.program_id(1)

    @pl.when(i == 0)
    def _():
        st_ref[...] = jnp.zeros_like(st_ref)
        cst_ref[...] = jnp.zeros_like(cst_ref)
        m_ref[...] = jnp.zeros_like(m_ref)
        xc_ref[0:8, :] = jnp.zeros((8, 2 * ML_W), F32)

    lane = _iota((1, LANES), 1)
    g_raw = p_ref[0, :, P0_G:P0_G + LANES]

    u = _dot(g_raw.astype(BF16), wa_ref[...]) + ba_ref[...]
    la = _log_sigmoid(u) * (1.0 / GLA_TAU)
    row = _iota((L, L), 0)
    col = _iota((L, L), 1)
    same_chunk = (row >> _CHUNK_SHIFT) == (col >> _CHUNK_SHIFT)
    tri_blk = jnp.where(same_chunk & (col <= row), 1.0, 0.0).astype(BF16)
    bcs = _dot_split(tri_blk, la, 3)
    q_in = p_ref[0, :, P0_QA:P0_QA + GLA_QK] * (GLA_DK ** -0.5) * jnp.exp(bcs)
    k_raw = p_ref[0, :, P0_KA:P0_KA + GLA_QK]
    k_in = k_raw * jnp.exp(-bcs)
    lane_qk = _iota((1, GLA_QK), 1)
    head_masks = [(lane_qk >> _CHUNK_SHIFT) == h for h in range(GLA_HEADS)]
    r4 = _iota((GLA_HEADS * GLA_CHUNK, GLA_CHUNK), 0)
    c4 = _iota((GLA_HEADS * GLA_CHUNK, GLA_CHUNK), 1)
    tril4 = c4 <= (r4 & (GLA_CHUNK - 1))
    oa_chunks = []
    for c in range(L // GLA_CHUNK):
        r0, r1 = c * GLA_CHUNK, (c + 1) * GLA_CHUNK
        b_c = bcs[r0:r1]
        bl = b_c[GLA_CHUNK - 1:GLA_CHUNK, :]
        q_c = q_in[r0:r1]
        k_c = k_in[r0:r1].astype(BF16)
        k_end = k_raw[r0:r1] * jnp.exp(bl - b_c)
        v_c = p_ref[0, r0:r1, P0_VA:P0_VA + GLA_V]
        q_exp = jnp.concatenate([jnp.where(hm, q_c, 0.0) for hm in head_masks], axis=0).astype(BF16)
        k_exp = jnp.concatenate([jnp.where(hm, k_end, 0.0) for hm in head_masks], axis=0).astype(BF16)
        sc = jnp.where(tril4, _dot_nt(q_exp, k_c), 0.0)
        intra = _dot(sc.astype(BF16), v_c.astype(BF16))
        inter = _dot_nt(q_exp, st_ref[...].astype(BF16))
        o_heads = []
        for h in range(GLA_HEADS):
            h0, h1 = h * GLA_CHUNK, (h + 1) * GLA_CHUNK
            o_heads.append(intra[h0:h1, h * GLA_DV:(h + 1) * GLA_DV] + inter[h0:h1])
        oa_chunks.append(o_heads)
        v_cat = jnp.concatenate([v_c[:, h * GLA_DV:(h + 1) * GLA_DV] for h in range(GLA_HEADS)], axis=0)
        st_ref[...] = st_ref[...] * jnp.exp(bl) + _dot_tn(v_cat.astype(BF16), k_exp)
    for h in range(GLA_HEADS):
        o_h = jnp.concatenate([oc[h] for oc in oa_chunks], axis=0)
        cs = slice(h * GLA_DV, (h + 1) * GLA_DV)
        ra = p_ref[0, :, P0_RA + h * GLA_DV:P0_RA + (h + 1) * GLA_DV]
        o_ref[0, :, cs] = (_head_norm(o_h, gg_ref[:, cs]) * _silu(ra)).astype(o_ref.dtype)

    xc_ref[8:8 + L, :] = p_ref[0, :, P0_QKB:P0_QKB + 2 * ML_W]
    conv = cb_ref[...]
    for kk in range(MLSTM_CONV):
        conv = conv + cw_ref[kk:kk + 1, :] * xc_ref[8 - (MLSTM_CONV - 1) + kk:8 - (MLSTM_CONV - 1) + kk + L, :]
    xc_ref[0:8, :] = xc_ref[L:L + 8, :]
    qk_b = _silu(conv)

    gb = jnp.where(lane < G_A, g_raw + gb_ref[...], 0.0)
    ipre = gb
    logf = _log_sigmoid(pltpu.roll(gb, LANES - G_F, axis=1))
    logf = jnp.where(lane < MLSTM_HEADS, logf, 0.0)
    tri = jnp.where(col <= row, 1.0, 0.0).astype(BF16)
    bcum = _dot_split(tri, logf, 3)
    blast = bcum[L - 1:L, :]
    wend = blast - bcum + ipre
    m_prev = m_ref[...]
    m_new = jnp.maximum(blast + m_prev, jnp.max(wend, axis=0, keepdims=True))
    scl = jnp.exp(blast + m_prev - m_new)
    wj = jnp.exp(wend - m_new)
    rows_src = jnp.where(lane < MLSTM_HEADS, ipre, pltpu.roll(bcum, MLSTM_HEADS, axis=1))
    rows_t = rows_src.T
    causal = col <= row
    ones = jnp.ones((L, MLSTM_DH), F32)
    for h in range(MLSTM_HEADS):
        hs = slice(h * MLSTM_DH, (h + 1) * MLSTM_DH)
        wide = lambda t: jnp.concatenate([t, t], axis=1)
        b_rep = jnp.broadcast_to(bcum[:, h:h + 1], (L, LANES))
        wj_rep = jnp.broadcast_to(wj[:, h:h + 1], (L, LANES))
        ip_row = rows_t[h:h + 1, :]
        b_row = rows_t[MLSTM_HEADS + h:MLSTM_HEADS + h + 1, :]
        dlog = jnp.where(causal, wide(b_rep) - b_row + ip_row, NEG_INF)
        il = b_rep + m_prev[:, h:h + 1]
        m_i = jnp.maximum(il, jnp.broadcast_to(jnp.max(dlog, axis=1, keepdims=True), (L, LANES)))
        w_intra = jnp.exp(dlog - wide(m_i))
        s_inter = jnp.exp(il - m_i)
        q_h = qk_b[:, h * MLSTM_DH:(h + 1) * MLSTM_DH].astype(BF16)
        k_h = (qk_b[:, ML_W + h * MLSTM_DH:ML_W + (h + 1) * MLSTM_DH] * (MLSTM_DH ** -0.5)).astype(BF16)
        v_h = p_ref[0, :, P0_VB + h * MLSTM_DH:P0_VB + (h + 1) * MLSTM_DH]
        v_aug = jnp.concatenate([v_h, ones], axis=1)
        a = (w_intra * _dot_nt(q_h, k_h)).astype(BF16)
        c_prev = cst_ref[h]
        tot = _dot(a, v_aug.astype(BF16)) + wide(s_inter) * _dot(q_h, c_prev.astype(BF16))
        num = tot[:, :MLSTM_DH]
        den = tot[:, MLSTM_DH:]
        hid = num / jnp.maximum(jnp.abs(den), jnp.exp(-m_i))
        w_aug = (v_aug * wide(wj_rep)).astype(BF16)
        cst_ref[h] = scl[:, h:h + 1] * c_prev + _dot_tn(k_h, w_aug)
        ob = p_ref[0, :, P0_OB + h * MLSTM_DH:P0_OB + (h + 1) * MLSTM_DH]
        o_ref[0, :, GLA_V + h * MLSTM_DH:GLA_V + (h + 1) * MLSTM_DH] = (
            jax.nn.sigmoid(ob) * _head_norm(hid, mg_ref[:, hs])).astype(o_ref.dtype)
    m_ref[...] = m_new


def _mixer0(proj, wa_up, ba, gla_g, conv_w, conv_b, b_i, b_f, ml_g):
    bsz, s, n = proj.shape
    L = MIX_TILE
    wa_pad = jnp.zeros((LANES, GLA_QK), F32).at[G_A:G_A + GLA_GATE_RANK].set(wa_up).astype(BF16)
    gbias = jnp.zeros((1, LANES), F32).at[0, G_I:G_I + MLSTM_HEADS].set(b_i).at[0, G_F:G_F + MLSTM_HEADS].set(b_f)
    full = lambda shape: pl.BlockSpec(shape, lambda b, i: (0,) * len(shape))
    return pl.pallas_call(
        _mixer0_kernel,
        out_shape=jax.ShapeDtypeStruct((bsz, s, GLA_V + ML_W), BF16),
        grid=(bsz, s // L),
        in_specs=[
            pl.BlockSpec((1, L, n), lambda b, i: (b, i, 0)),
            full((LANES, GLA_QK)),
            full((1, GLA_QK)),
            full((1, GLA_V)),
            full((MLSTM_CONV, 2 * ML_W)),
            full((1, 2 * ML_W)),
            full((1, LANES)),
            full((1, ML_W)),
        ],
        out_specs=pl.BlockSpec((1, L, GLA_V + ML_W), lambda b, i: (b, i, 0)),
        scratch_shapes=[
            pltpu.VMEM((GLA_DV, GLA_QK), F32),
            pltpu.VMEM((MLSTM_HEADS, MLSTM_DH, 2 * MLSTM_DH), F32),
            pltpu.VMEM((1, LANES), F32),
            pltpu.VMEM((8 + L, 2 * ML_W), F32),
        ],
        compiler_params=_params("parallel", "arbitrary"),
        name="mixer0",
    )(proj, wa_pad, ba.reshape(1, GLA_QK), gla_g.reshape(1, GLA_V), conv_w, conv_b.reshape(1, 2 * ML_W),
      gbias, ml_g.reshape(1, ML_W))


def _sb_kernel(q_ref, k_ref, v_ref, o_ref, qs_ref, acc_ref, car_ref, mar_ref, kn_ref):
    TQ, SUB, WIDE = SB_TQ, SB_SUB, SB_WIDE
    ns = TQ // SUB
    RB = 2 * SUB
    i = pl.program_id(2)
    lane = _iota((1, LANES), 1)
    head0 = (lane >> _HEAD_SHIFT) == 0

    @pl.when(i == 0)
    def _():
        k2 = jnp.square(k_ref[0].astype(F32))
        for h in range(2):
            n2 = jnp.sum(jnp.where(head0 if h == 0 else ~head0, k2, 0.0), axis=1, keepdims=True)
            kn_ref[h:h + 1, :] = jnp.broadcast_to(jnp.sqrt(jnp.max(n2, axis=0, keepdims=True)), (1, LANES))

    for a in range(ns):
        qa = q_ref[0, a * SUB:(a + 1) * SUB, :].astype(F32) * (SB_DH ** -0.5 * LOG2E)
        for h in range(2):
            rows = slice(a * RB + h * SUB, a * RB + (h + 1) * SUB)
            qh = jnp.where(head0 if h == 0 else ~head0, qa, 0.0).astype(BF16)
            qs_ref[rows, :] = qh
            qn = jnp.sqrt(jnp.sum(jnp.square(qh.astype(F32)), axis=1, keepdims=True))
            mar_ref[rows, :] = qn * kn_ref[h:h + 1, :] * SB_BOUND_SLACK + SB_ZERO_BITS
    tri = jnp.where(_iota((WIDE, WIDE), 0) >= _iota((WIDE, WIDE), 1), -1.0, 0.0).astype(BF16)

    def key_rows(ku, nk):
        r0 = ku * SUB
        return pl.ds(r0 if isinstance(r0, int) else pl.multiple_of(r0, SUB), nk)

    def logits(a, ku, nk):
        return _dot_nt(qs_ref[a * RB:(a + 1) * RB, :], k_ref[0, key_rows(ku, nk), :])

    def softplus2(z, diagonal):
        sp = jnp.maximum(z, 0.0) + jnp.log(1.0 + jnp.exp2(-jnp.abs(z))) * LOG2E
        if not diagonal:
            return sp, None
        mask = _iota(z.shape, 1) < (_iota(z.shape, 0) & (SUB - 1))
        return jnp.where(mask, sp, 0.0), mask

    def suffix(sp):
        nk = sp.shape[1]
        return _dot(sp.astype(BF16), tri[:nk, :nk]), jnp.sum(sp, axis=1, keepdims=True)

    def weights(z, res, car, mask):
        att = jnp.exp2(z + res if car is None else z + res - car)
        if mask is not None:
            att = jnp.where(mask, att, 0.0)
        return att.astype(BF16)

    def far_tile(a, ku):
        rows = slice(a * RB, (a + 1) * RB)
        z = logits(a, ku, SUB)
        sp, _ = softplus2(z, False)
        res, tot = suffix(sp)
        car = car_ref[rows, :]
        acc_ref[rows, :] += _dot(weights(z, res, car, None), v_ref[0, key_rows(ku, SUB), :])
        car_ref[rows, :] = car + tot

    def near_tiles(first_block):
        chains = []
        for a in range(ns):
            g = a if first_block else ns * i + a
            chain = [(a, g, SUB, True)]
            if not first_block or a >= 2:
                chain.append((a, g - 2, WIDE, False))
            elif a == 1:
                chain.append((a, 0, SUB, False))
            chains.append(chain)
        jobs = [job for step in itertools.zip_longest(*chains) for job in step if job is not None]
        zs = [logits(a, ku, nk) for a, ku, nk, _ in jobs]
        sps = [softplus2(z, diagonal) for z, (_, _, _, diagonal) in zip(zs, jobs)]
        sufs = [suffix(sp) for sp, _ in sps]
        car = [None] * ns
        atts = []
        for z, (sp, mask), (res, tot), (a, _, _, _) in zip(zs, sps, sufs, jobs):
            atts.append(weights(z, res, car[a], mask))
            car[a] = tot if car[a] is None else car[a] + tot
        acc = [None] * ns
        for att, (a, ku, nk, _) in zip(atts, jobs):
            t = _dot(att, v_ref[0, key_rows(ku, nk), :])
            acc[a] = t if acc[a] is None else acc[a] + t
        for a in range(ns):
            acc_ref[a * RB:(a + 1) * RB, :] = acc[a]
            car_ref[a * RB:(a + 1) * RB, :] = jnp.broadcast_to(car[a], (RB, LANES))

    pl.when(i == 0)(functools.partial(near_tiles, True))
    pl.when(i > 0)(functools.partial(near_tiles, False))

    def next_unit(a, t):
        return ns * i + a - 3 - t

    def pending(t):
        gap = [jnp.where(next_unit(a, t) >= 0, mar_ref[a * RB:(a + 1) * RB, :] - car_ref[a * RB:(a + 1) * RB, :], -1.0)
               for a in range(ns)]
        return jnp.max(functools.reduce(jnp.maximum, gap)) > 0.0

    def far_body(c):
        t, _ = c
        for a in range(ns):
            ku = next_unit(a, t)
            pl.when(ku >= 0)(functools.partial(far_tile, a, ku))
        return t + 1, pending(t + 1)

    lax.while_loop(lambda c: c[1], far_body, (0, pending(0)))

    for a in range(ns):
        o_ref[0, a * SUB:(a + 1) * SUB, :] = jnp.where(
            head0, acc_ref[a * RB:a * RB + SUB, :], acc_ref[a * RB + SUB:(a + 1) * RB, :]).astype(o_ref.dtype)


def _stick_breaking(pc):
    bsz, s, _ = pc.shape
    npair = SB_W // LANES
    return pl.pallas_call(
        _sb_kernel,
        out_shape=jax.ShapeDtypeStruct((bsz, s, SB_W), BF16),
        grid=(bsz, npair, s // SB_TQ),
        in_specs=[
            pl.BlockSpec((1, SB_TQ, LANES), lambda b, p, i: (b, i, p)),
            pl.BlockSpec((1, s, LANES), lambda b, p, i: (b, 0, npair + p)),
            pl.BlockSpec((1, s, LANES), lambda b, p, i: (b, 0, 2 * npair + p)),
        ],
        out_specs=pl.BlockSpec((1, SB_TQ, LANES), lambda b, p, i: (b, i, p)),
        scratch_shapes=[
            pltpu.VMEM((2 * SB_TQ, LANES), BF16),
            pltpu.VMEM((2 * SB_TQ, LANES), F32),
            pltpu.VMEM((2 * SB_TQ, LANES), F32),
            pltpu.VMEM((2 * SB_TQ, LANES), F32),
            pltpu.VMEM((8, LANES), F32),
        ],
        compiler_params=_params("parallel", "parallel", "arbitrary"),
        name="stick_breaking",
    )(pc, pc, pc)


def _t5_bucket(dist):
    max_exact = N_REL_BUCKETS // 2
    dd = np.maximum(dist, 1).astype(np.float64)
    large = max_exact + (np.log(dd / max_exact) / np.log(REL_MAX_DIST / max_exact)
                         * (N_REL_BUCKETS - max_exact)).astype(np.int32)
    large = np.minimum(large, N_REL_BUCKETS - 1)
    return np.where(dist < max_exact, dist, large).astype(np.int32)


def _dil_buckets():
    qi = np.arange(DIL_BLK)[:, None]
    kj = np.arange(2 * DIL_BLK)[None, :]
    delta = qi - kj + DIL_BLK
    in_win = (delta >= 0) & (delta <= DIL_BLK)
    tabs = []
    for window, dil in DIL_PAIRS:
        assert window // dil == DIL_BLK
        bucket = _t5_bucket(np.clip(delta, 0, None) * dil)
        tabs.append(np.where(in_win, bucket, -1).astype(np.int32))
    return np.stack(tabs, 0)


def _dil_kernel(q_ref, k_ref, v_ref, bkt_ref, bkt0_ref, tab_ref, o_ref, m_sc, l_sc, acc_sc, bias0_sc, bias_sc):
    s_len = q_ref.shape[1]
    first_batch = pl.program_id(0) == 0
    hp = pl.program_id(1)
    g = pl.program_id(2)
    blk = DIL_BLK
    ng = len(DIL_PAIRS)
    lane = _iota((1, LANES), 1)
    lane_head = lane >> _HEAD_SHIFT
    sel0 = lane_head == 0
    kcol = _iota((2 * blk, 2 * blk), 1)

    def head_biases(gi, bkt, dst):
        @pl.when(first_batch)
        def _():
            n = bkt.shape[0]
            for hl in range(2):
                head = gi * DIL_HEADS_PER_GROUP + hp * 2 + hl
                bias = jnp.full(bkt.shape, NEG_INF, F32)
                for bk in range(N_REL_BUCKETS):
                    bias = jnp.where(bkt == bk, tab_ref[bk, head], bias)
                dst[hl * n:(hl + 1) * n, :] = bias
        return dst[...]

    def stack_heads(qb):
        return jnp.concatenate([jnp.where(sel0, qb, 0.0), jnp.where(sel0, 0.0, qb)], axis=0).astype(BF16)

    def with_ones(vv):
        return jnp.concatenate([vv, jnp.ones_like(vv)], axis=1).astype(BF16)

    def first_group(gi, dil):
        n_sub = s_len // dil
        bias2 = head_biases(gi, bkt0_ref[...], bias0_sc.at[hp])

        def residue(r):
            rows = pl.ds(r, n_sub, stride=dil)
            logits = _dot_nt(stack_heads(q_ref[0, rows, :] * (DIL_DH ** -0.5)), k_ref[0, rows, :].astype(BF16)) + bias2
            return rows, logits

        def body(tt, carry):
            s1 = [residue(tt * DIL_UNROLL + u) for u in range(DIL_UNROLL)]
            s2 = [jnp.max(lg, axis=1, keepdims=True) for _, lg in s1]
            s3 = [_dot(jnp.exp(lg - mc).astype(BF16), with_ones(v_ref[0, rows, :])) for (rows, lg), mc in zip(s1, s2)]
            for (rows, _), mc, pv in zip(s1, s2, s3):
                m_sc[rows, :] = jnp.where(sel0, mc[0:n_sub], mc[n_sub:])
                l_sc[rows, :] = jnp.where(sel0, pv[0:n_sub, LANES:], pv[n_sub:, LANES:])
                acc_sc[rows, :] = jnp.where(sel0, pv[0:n_sub, :LANES], pv[n_sub:, :LANES])
            return carry

        lax.fori_loop(0, dil // DIL_UNROLL, body, 0)

    def group(gi, dil):
        nb = s_len // (blk * dil)
        bias2 = head_biases(gi, bkt_ref[gi], bias_sc.at[hp, gi])

        def stage_logits(t):
            r = t // nb
            n = t % nb
            q_start = r + dil * blk * n
            p_start = r + dil * blk * jnp.maximum(n - 1, 0)
            rows_q = pl.ds(q_start, blk, stride=dil) if dil > 1 else pl.ds(q_start, blk)
            rows_p = pl.ds(p_start, blk, stride=dil) if dil > 1 else pl.ds(p_start, blk)
            q2 = stack_heads(q_ref[0, rows_q, :] * (DIL_DH ** -0.5))
            kk = jnp.concatenate([k_ref[0, rows_p, :], k_ref[0, rows_q, :]], axis=0).astype(BF16)
            logits = _dot_nt(q2, kk) + bias2
            logits = jnp.where(jnp.logical_and(n == 0, kcol < blk), NEG_INF, logits)
            return rows_q, rows_p, logits

        def stage_pv(rows_q, rows_p, logits, m_col):
            p = jnp.exp(logits - m_col)
            vv = with_ones(jnp.concatenate([v_ref[0, rows_p, :], v_ref[0, rows_q, :]], axis=0))
            pv = _dot(p.astype(BF16), vv)
            return pv[:, LANES:], pv[:, :LANES]

        def body(tt, carry):
            ts = [tt * DIL_UNROLL + u for u in range(DIL_UNROLL)]
            s1 = [stage_logits(t) for t in ts]
            s2 = [jnp.max(lg, axis=1, keepdims=True) for _, _, lg in s1]
            s3 = [stage_pv(rq, rp, lg, mc) for (rq, rp, lg), mc in zip(s1, s2)]
            outs = []
            for (rows_q, _, _), m_col, (ps, pv) in zip(s1, s2, s3):
                m_old = m_sc[rows_q, :]
                m_blk = jnp.where(sel0, m_col[0:blk], m_col[blk:])
                m_new = jnp.maximum(m_old, m_blk)
                a_old = jnp.exp(m_old - m_new)
                a_blk = jnp.exp(m_blk - m_new)
                l_new = a_old * l_sc[rows_q, :] + a_blk * jnp.where(sel0, ps[0:blk], ps[blk:])
                a_new = a_old * acc_sc[rows_q, :] + a_blk * jnp.where(sel0, pv[0:blk], pv[blk:])
                outs.append((rows_q, m_new, l_new, a_new))
            for rows_q, m_new, l_new, a_new in outs:
                m_sc[rows_q, :] = m_new
                l_sc[rows_q, :] = l_new
                acc_sc[rows_q, :] = a_new
            return carry

        lax.fori_loop(0, s_len // (blk * DIL_UNROLL), body, 0)

    for gi, (_, dil) in enumerate(DIL_PAIRS):
        pl.when(g == ng - 1 - gi)(functools.partial(first_group if gi == ng - 1 else group, gi, dil))

    @pl.when(g == ng - 1)
    def _():
        o_ref[0] = (acc_sc[...] / l_sc[...]).astype(o_ref.dtype)


def _dilated(pd, rel_bias):
    bsz, s, _ = pd.shape
    ng = len(DIL_PAIRS)
    gw = DIL_HEADS_PER_GROUP * DIL_DH // LANES
    nq = DIL_W // LANES
    bkt = jnp.asarray(_dil_buckets())
    dil0 = DIL_PAIRS[-1][1]
    n_sub = s // dil0
    assert s % (DIL_BLK * dil0) == 0 and dil0 % DIL_UNROLL == 0
    delta = np.arange(n_sub)[:, None] - np.arange(n_sub)[None, :]
    bkt0 = jnp.asarray(np.where((delta >= 0) & (delta <= DIL_BLK), _t5_bucket(np.clip(delta, 0, None) * dil0), -1)
                       .astype(np.int32))
    blk_spec = lambda base: pl.BlockSpec((1, s, LANES), lambda b, p, g: (b, 0, base + (ng - 1 - g) * gw + p))
    return pl.pallas_call(
        _dil_kernel,
        out_shape=jax.ShapeDtypeStruct((bsz, s, DIL_OUT), BF16),
        grid=(bsz, gw, ng),
        in_specs=[
            blk_spec(0),
            blk_spec(nq),
            blk_spec(2 * nq),
            pl.BlockSpec((ng, DIL_BLK, 2 * DIL_BLK), lambda b, p, g: (0, 0, 0)),
            pl.BlockSpec((n_sub, n_sub), lambda b, p, g: (0, 0)),
            pl.BlockSpec(memory_space=pltpu.SMEM),
        ],
        out_specs=pl.BlockSpec((1, s, LANES), lambda b, p, g: (b, 0, p)),
        scratch_shapes=[pltpu.VMEM((s, LANES), F32)] * 3 + [
            pltpu.VMEM((gw, 2 * n_sub, n_sub), F32),
            pltpu.VMEM((gw, ng - 1, 2 * DIL_BLK, 2 * DIL_BLK), F32),
        ],
        compiler_params=_params("arbitrary", "arbitrary", "arbitrary"),
        name="dilated",
    )(pd, pd, pd, bkt, bkt0, rel_bias)


def _ab_weight(w):
    o = np.cumsum((0,) + (GLA_QK, GLA_QK, GLA_V, GLA_V, GLA_GATE_RANK, 2 * ML_W, ML_W, MLSTM_HEADS, MLSTM_HEADS, ML_W))
    qa, ka, va, ra, aa, qkb, vb, ib, fb, ob = [w[:, o[j]:o[j + 1]] for j in range(10)]
    pad = jnp.zeros((w.shape[0], LANES - 2 * MLSTM_HEADS - GLA_GATE_RANK), w.dtype)
    return jnp.concatenate([qa, ka, va, ra, qkb, vb, ob, ib, fb, aa, pad], axis=1).astype(BF16)


def kernel(x, c, ada_w, ada_b, ln_g, ln_b, ab_w_in, gla_wa_up, gla_ba, gla_norm_g, ml_conv_w, ml_conv_b,
           ml_b_i, ml_b_f, ml_norm_g, ab_w_out, cd_w_in, rel_bias, cd_w_out, ffn_w1, ffn_w3, ffn_w2):
    bsz, s, d = x.shape
    mod_all = _ada_mod(c, ada_w, ada_b).reshape(DEPTH, bsz, 6, d)
    tm = FFN_TM
    for layer in range(DEPTH):
        mod = mod_all[layer]
        j = layer // 2
        if layer % 2 == 0:
            (proj,) = _inproj(x, mod, _ab_weight(ab_w_in[j]), [(P0_N, F32)], tm, 0, 1, "inproj0")
            cat = _mixer0(proj, gla_wa_up[j], gla_ba[j], gla_norm_g[j], ml_conv_w[j], ml_conv_b[j],
                          ml_b_i[j], ml_b_f[j], ml_norm_g[j])
            acts, w_outs = [cat], [ab_w_out[j].astype(BF16)]
        else:
            pc, pd = _inproj(x, mod, cd_w_in[j].astype(BF16), [(3 * SB_W, BF16), (3 * DIL_W, F32)],
                             tm, 0, 1, "inproj1")
            oc = _stick_breaking(pc)
            od = _dilated(pd, rel_bias)
            w_out = cd_w_out[j].astype(BF16)
            acts, w_outs = [oc, od], [w_out[:SB_W], w_out[SB_W:]]
        x = _post(acts, w_outs, x, mod, ffn_w1[layer], ffn_w3[layer], ffn_w2[layer], ln_g[layer], ln_b[layer], FFN_TM)
    return x
```

```python
import functools
import itertools

import numpy as np
import jax
import jax.numpy as jnp
from jax import lax
from jax.experimental import pallas as pl
from jax.experimental.pallas import tpu as pltpu

F32 = jnp.float32
BF16 = jnp.bfloat16

D_MODEL = 1024
DEPTH = 2
GLA_HEADS = 4
GLA_DK = 64
GLA_DV = 128
GLA_GATE_RANK = 16
GLA_TAU = 16.0
GLA_CHUNK = 64
MLSTM_HEADS = 4
MLSTM_DH = 128
MLSTM_CONV = 4
SB_HEADS = 8
SB_DH = 64
DIL_PAIRS = ((128, 1), (512, 4), (2048, 16))
DIL_HEADS_PER_GROUP = 4
DIL_DH = 64
DIL_BLK = 128
DIL_UNROLL = 4
N_REL_BUCKETS = 32
REL_MAX_DIST = 2048
D_FF = ((8 * D_MODEL + 3 * 256 - 1) // (3 * 256)) * 256
LN_EPS = 1e-5
RES_ALPHA = (2 * DEPTH) ** 0.25
NEG_INF = -1e30

GLA_QK = GLA_HEADS * GLA_DK
GLA_V = GLA_HEADS * GLA_DV
ML_W = MLSTM_HEADS * MLSTM_DH
SB_W = SB_HEADS * SB_DH
DIL_W = len(DIL_PAIRS) * DIL_HEADS_PER_GROUP * DIL_DH
DIL_OUT = DIL_HEADS_PER_GROUP * DIL_DH

LANES = 128
VMEM_LIMIT = 56 * 1024 * 1024

P0_QA = 0
P0_KA = P0_QA + GLA_QK
P0_VA = P0_KA + GLA_QK
P0_RA = P0_VA + GLA_V
P0_QKB = P0_RA + GLA_V
P0_VB = P0_QKB + 2 * ML_W
P0_OB = P0_VB + ML_W
P0_G = P0_OB + ML_W
P0_N = P0_G + LANES
G_I = 0
G_F = MLSTM_HEADS
G_A = 2 * MLSTM_HEADS

MIX_TILE = 256
MIX_ROWS = 4
FF_CHUNK = 256
FFN_TM = 512
SB_TQ = 2048
SB_SUB = 128
SB_WIDE = 256
SB_ZERO_BITS = 160.0
SB_BOUND_SLACK = 1.01
LOG2E = 1.4426950408889634
_CHUNK_SHIFT = 6
_HEAD_SHIFT = 6
assert GLA_CHUNK == GLA_DK == 1 << _CHUNK_SHIFT and SB_DH == DIL_DH == 1 << _HEAD_SHIFT


def _dot(a, b):
    return jnp.dot(a, b, preferred_element_type=F32)


def _dot_nt(a, b):
    return lax.dot_general(a, b, (((1,), (1,)), ((), ())), preferred_element_type=F32)


def _dot_tn(a, b):
    return lax.dot_general(a, b, (((0,), (0,)), ((), ())), preferred_element_type=F32)


def _dot_split(t, x, terms):
    acc = None
    rem = x
    for i in range(terms):
        part = rem.astype(BF16)
        d = _dot(t, part)
        acc = d if acc is None else acc + d
        if i + 1 < terms:
            rem = rem - part.astype(F32)
    return acc


def _log_sigmoid(x):
    return jnp.minimum(x, 0.0) - jnp.log(1.0 + jnp.exp(-jnp.abs(x)))


def _silu(x):
    return x * jax.nn.sigmoid(x)


def _layer_norm(r, g, b):
    mu = jnp.mean(r, axis=-1, keepdims=True)
    d = r - mu
    var = jnp.mean(d * d, axis=-1, keepdims=True)
    return d * lax.rsqrt(var + LN_EPS) * g + b


def _head_norm(x, g):
    mu = jnp.mean(x, axis=-1, keepdims=True)
    d = x - mu
    var = jnp.mean(d * d, axis=-1, keepdims=True)
    return d * lax.rsqrt(var + LN_EPS) * g


def _iota(shape, dim):
    return lax.broadcasted_iota(jnp.int32, shape, dim)


def _params(*sem):
    return pltpu.CompilerParams(dimension_semantics=sem, vmem_limit_bytes=VMEM_LIMIT)


def _ada_kernel(c_ref, w_ref, b_ref, o_ref):
    ca = _silu(c_ref[...]).astype(BF16)
    o_ref[0] = _dot(ca, w_ref[0].astype(BF16)) + b_ref[0]


def _ada_mod(c, ada_w, ada_b):
    bsz, d = c.shape
    n = ada_w.shape[-1]
    tn = n // 4
    return pl.pallas_call(
        _ada_kernel,
        out_shape=jax.ShapeDtypeStruct((DEPTH, bsz, n), F32),
        grid=(DEPTH, n // tn),
        in_specs=[
            pl.BlockSpec((bsz, d), lambda l, j: (0, 0)),
            pl.BlockSpec((1, d, tn), lambda l, j: (l, 0, j)),
            pl.BlockSpec((1, 1, tn), lambda l, j: (l, 0, j)),
        ],
        out_specs=pl.BlockSpec((1, bsz, tn), lambda l, j: (l, 0, j)),
        compiler_params=_params("arbitrary", "arbitrary"),
        name="ada_mod",
    )(c, ada_w, ada_b.reshape(DEPTH, 1, n))


def _inproj_kernel(x_ref, mod_ref, w_ref, *o_refs, shift_row, scale_row):
    sh = mod_ref[0, shift_row:shift_row + 1, :]
    sc = mod_ref[0, scale_row:scale_row + 1, :]
    hm = (x_ref[0] * (1.0 + sc) + sh).astype(BF16)
    col = 0
    for o_ref in o_refs:
        n = o_ref.shape[-1]
        o_ref[0] = _dot(hm, w_ref[:, col:col + n]).astype(o_ref.dtype)
        col += n


def _inproj(x, mod, w, outs, tm, shift_row, scale_row, name):
    bsz, s, d = x.shape
    n = w.shape[1]
    assert sum(o[0] for o in outs) == n
    return pl.pallas_call(
        functools.partial(_inproj_kernel, shift_row=shift_row, scale_row=scale_row),
        out_shape=[jax.ShapeDtypeStruct((bsz, s, o[0]), o[1]) for o in outs],
        grid=(bsz, s // tm),
        in_specs=[
            pl.BlockSpec((1, tm, d), lambda b, i: (b, i, 0)),
            pl.BlockSpec((1, 6, d), lambda b, i: (b, 0, 0)),
            pl.BlockSpec((d, n), lambda b, i: (0, 0)),
        ],
        out_specs=[pl.BlockSpec((1, tm, o[0]), lambda b, i: (b, i, 0)) for o in outs],
        compiler_params=_params("parallel", "arbitrary"),
        name=name,
    )(x, mod, w)


def _post_kernel(*refs, n_act):
    act_refs = refs[:n_act]
    wo_refs = refs[n_act:2 * n_act]
    x_ref, mod_ref, w1_ref, w3_ref, w2_ref, g_ref, b_ref, o_ref, gm_ref = refs[2 * n_act:]
    y = None
    for a_ref, w_ref in zip(act_refs, wo_refs):
        t = _dot(a_ref[0], w_ref[...])
        y = t if y is None else y + t
    r = RES_ALPHA * x_ref[0] + (1.0 + mod_ref[0, 2:3, :]) * y
    xm = _layer_norm(r, g_ref[0:1, :], b_ref[0:1, :])
    hf = (xm * (1.0 + mod_ref[0, 4:5, :]) + mod_ref[0, 3:4, :]).astype(BF16)
    for c in range(D_FF // FF_CHUNK):
        cs = slice(c * FF_CHUNK, (c + 1) * FF_CHUNK)
        gm_ref[:, cs] = (_silu(_dot(hf, w1_ref[:, cs])) * _dot(hf, w3_ref[:, cs])).astype(BF16)
    y2 = _dot(gm_ref[...], w2_ref[...])
    r2 = RES_ALPHA * xm + (1.0 + mod_ref[0, 5:6, :]) * y2
    o_ref[0] = _layer_norm(r2, g_ref[1:2, :], b_ref[1:2, :])


def _post(acts, w_outs, x, mod, w1, w3, w2, ln_g, ln_b, tm):
    bsz, s, d = x.shape
    n_act = len(acts)
    const = lambda shape: pl.BlockSpec(shape, lambda b, i: (0,) * len(shape), pipeline_mode=pl.Buffered(1))
    in_specs = [pl.BlockSpec((1, tm, a.shape[-1]), lambda b, i: (b, i, 0)) for a in acts]
    in_specs += [const(w.shape) for w in w_outs]
    in_specs += [
        pl.BlockSpec((1, tm, d), lambda b, i: (b, i, 0)),
        pl.BlockSpec((1, 6, d), lambda b, i: (b, 0, 0)),
        const((d, D_FF)),
        const((d, D_FF)),
        const((D_FF, d)),
        const((2, d)),
        const((2, d)),
    ]
    return pl.pallas_call(
        functools.partial(_post_kernel, n_act=n_act),
        out_shape=jax.ShapeDtypeStruct((bsz, s, d), F32),
        grid=(bsz, s // tm),
        in_specs=in_specs,
        out_specs=pl.BlockSpec((1, tm, d), lambda b, i: (b, i, 0)),
        scratch_shapes=[pltpu.VMEM((tm, D_FF), BF16)],
        compiler_params=_params("parallel", "arbitrary"),
        name="post",
    )(*acts, *w_outs, x, mod, w1.astype(BF16), w3.astype(BF16), w2.astype(BF16), ln_g, ln_b)


def _mixer0_kernel(p_ref, wa_ref, ba_ref, gg_ref, cw_ref, cb_ref, gb_ref, mg_ref, o_ref,
                   st_ref, cst_ref, m_ref, xc_ref):
    @pl.when(pl.program_id(1) == 0)
    def _():
        st_ref[...] = jnp.zeros_like(st_ref)
        cst_ref[...] = jnp.zeros_like(cst_ref)
        m_ref[...] = jnp.zeros_like(m_ref)
        xc_ref[:, 0:8, :] = jnp.zeros((xc_ref.shape[0], 8, 2 * ML_W), F32)

    for b in range(p_ref.shape[0]):
        _mixer0_row(p_ref.at[pl.ds(b, 1)], wa_ref, ba_ref, gg_ref, cw_ref, cb_ref, gb_ref, mg_ref,
                    o_ref.at[pl.ds(b, 1)], st_ref.at[b], cst_ref.at[b], m_ref.at[b], xc_ref.at[b])


def _mixer0_row(p_ref, wa_ref, ba_ref, gg_ref, cw_ref, cb_ref, gb_ref, mg_ref, o_ref, st_ref, cst_ref, m_ref, xc_ref):
    L = MIX_TILE
    lane = _iota((1, LANES), 1)
    g_raw = p_ref[0, :, P0_G:P0_G + LANES]

    u = _dot(g_raw.astype(BF16), wa_ref[...]) + ba_ref[...]
    la = _log_sigmoid(u) * (1.0 / GLA_TAU)
    row = _iota((L, L), 0)
    col = _iota((L, L), 1)
    same_chunk = (row >> _CHUNK_SHIFT) == (col >> _CHUNK_SHIFT)
    tri_blk = jnp.where(same_chunk & (col <= row), 1.0, 0.0).astype(BF16)
    bcs = _dot_split(tri_blk, la, 3)
    q_in = p_ref[0, :, P0_QA:P0_QA + GLA_QK] * (GLA_DK ** -0.5) * jnp.exp(bcs)
    k_raw = p_ref[0, :, P0_KA:P0_KA + GLA_QK]
    k_in = k_raw * jnp.exp(-bcs)
    lane_qk = _iota((1, GLA_QK), 1)
    head_masks = [(lane_qk >> _CHUNK_SHIFT) == h for h in range(GLA_HEADS)]
    r4 = _iota((GLA_HEADS * GLA_CHUNK, GLA_CHUNK), 0)
    c4 = _iota((GLA_HEADS * GLA_CHUNK, GLA_CHUNK), 1)
    tril4 = c4 <= (r4 & (GLA_CHUNK - 1))
    oa_chunks = []
    for c in range(L // GLA_CHUNK):
        r0, r1 = c * GLA_CHUNK, (c + 1) * GLA_CHUNK
        b_c = bcs[r0:r1]
        bl = b_c[GLA_CHUNK - 1:GLA_CHUNK, :]
        q_c = q_in[r0:r1]
        k_c = k_in[r0:r1].astype(BF16)
        k_end = k_raw[r0:r1] * jnp.exp(bl - b_c)
        v_c = p_ref[0, r0:r1, P0_VA:P0_VA + GLA_V]
        q_exp = jnp.concatenate([jnp.where(hm, q_c, 0.0) for hm in head_masks], axis=0).astype(BF16)
        k_exp = jnp.concatenate([jnp.where(hm, k_end, 0.0) for hm in head_masks], axis=0).astype(BF16)
        sc = jnp.where(tril4, _dot_nt(q_exp, k_c), 0.0)
        intra = _dot(sc.astype(BF16), v_c.astype(BF16))
        inter = _dot_nt(q_exp, st_ref[...].astype(BF16))
        o_heads = []
        for h in range(GLA_HEADS):
            h0, h1 = h * GLA_CHUNK, (h + 1) * GLA_CHUNK
            o_heads.append(intra[h0:h1, h * GLA_DV:(h + 1) * GLA_DV] + inter[h0:h1])
        oa_chunks.append(o_heads)
        v_cat = jnp.concatenate([v_c[:, h * GLA_DV:(h + 1) * GLA_DV] for h in range(GLA_HEADS)], axis=0)
        st_ref[...] = st_ref[...] * jnp.exp(bl) + _dot_tn(v_cat.astype(BF16), k_exp)
    for h in range(GLA_HEADS):
        o_h = jnp.concatenate([oc[h] for oc in oa_chunks], axis=0)
        cs = slice(h * GLA_DV, (h + 1) * GLA_DV)
        ra = p_ref[0, :, P0_RA + h * GLA_DV:P0_RA + (h + 1) * GLA_DV]
        o_ref[0, :, cs] = (_head_norm(o_h, gg_ref[:, cs]) * _silu(ra)).astype(o_ref.dtype)

    xc_ref[8:8 + L, :] = p_ref[0, :, P0_QKB:P0_QKB + 2 * ML_W]
    conv = cb_ref[...]
    for kk in range(MLSTM_CONV):
        conv = conv + cw_ref[kk:kk + 1, :] * xc_ref[8 - (MLSTM_CONV - 1) + kk:8 - (MLSTM_CONV - 1) + kk + L, :]
    xc_ref[0:8, :] = xc_ref[L:L + 8, :]
    qk_b = _silu(conv)

    gb = jnp.where(lane < G_A, g_raw + gb_ref[...], 0.0)
    ipre = gb
    logf = _log_sigmoid(pltpu.roll(gb, LANES - G_F, axis=1))
    logf = jnp.where(lane < MLSTM_HEADS, logf, 0.0)
    tri = jnp.where(col <= row, 1.0, 0.0).astype(BF16)
    bcum = _dot_split(tri, logf, 3)
    blast = bcum[L - 1:L, :]
    wend = blast - bcum + ipre
    m_prev = m_ref[...]
    m_new = jnp.maximum(blast + m_prev, jnp.max(wend, axis=0, keepdims=True))
    scl = jnp.exp(blast + m_prev - m_new)
    wj = jnp.exp(wend - m_new)
    rows_src = jnp.where(lane < MLSTM_HEADS, ipre, pltpu.roll(bcum, MLSTM_HEADS, axis=1))
    rows_t = rows_src.T
    causal = col <= row
    ones = jnp.ones((L, MLSTM_DH), F32)
    for h in range(MLSTM_HEADS):
        hs = slice(h * MLSTM_DH, (h + 1) * MLSTM_DH)
        wide = lambda t: jnp.concatenate([t, t], axis=1)
        b_rep = jnp.broadcast_to(bcum[:, h:h + 1], (L, LANES))
        wj_rep = jnp.broadcast_to(wj[:, h:h + 1], (L, LANES))
        ip_row = rows_t[h:h + 1, :]
        b_row = rows_t[MLSTM_HEADS + h:MLSTM_HEADS + h + 1, :]
        dlog = jnp.where(causal, wide(b_rep) - b_row + ip_row, NEG_INF)
        il = b_rep + m_prev[:, h:h + 1]
        m_i = jnp.maximum(il, jnp.broadcast_to(jnp.max(dlog, axis=1, keepdims=True), (L, LANES)))
        w_intra = jnp.exp(dlog - wide(m_i))
        s_inter = jnp.exp(il - m_i)
        q_h = qk_b[:, h * MLSTM_DH:(h + 1) * MLSTM_DH].astype(BF16)
        k_h = (qk_b[:, ML_W + h * MLSTM_DH:ML_W + (h + 1) * MLSTM_DH] * (MLSTM_DH ** -0.5)).astype(BF16)
        v_h = p_ref[0, :, P0_VB + h * MLSTM_DH:P0_VB + (h + 1) * MLSTM_DH]
        v_aug = jnp.concatenate([v_h, ones], axis=1)
        a = (w_intra * _dot_nt(q_h, k_h)).astype(BF16)
        c_prev = cst_ref[h]
        tot = _dot(a, v_aug.astype(BF16)) + wide(s_inter) * _dot(q_h, c_prev.astype(BF16))
        num = tot[:, :MLSTM_DH]
        den = tot[:, MLSTM_DH:]
        hid = num / jnp.maximum(jnp.abs(den), jnp.exp(-m_i))
        w_aug = (v_aug * wide(wj_rep)).astype(BF16)
        cst_ref[h] = scl[:, h:h + 1] * c_prev + _dot_tn(k_h, w_aug)
        ob = p_ref[0, :, P0_OB + h * MLSTM_DH:P0_OB + (h + 1) * MLSTM_DH]
        o_ref[0, :, GLA_V + h * MLSTM_DH:GLA_V + (h + 1) * MLSTM_DH] = (
            jax.nn.sigmoid(ob) * _head_norm(hid, mg_ref[:, hs])).astype(o_ref.dtype)
    m_ref[...] = m_new


def _mixer0(proj, wa_up, ba, gla_g, conv_w, conv_b, b_i, b_f, ml_g):
    bsz, s, n = proj.shape
    L = MIX_TILE
    nr = MIX_ROWS if bsz % MIX_ROWS == 0 else 1
    wa_pad = jnp.zeros((LANES, GLA_QK), F32).at[G_A:G_A + GLA_GATE_RANK].set(wa_up).astype(BF16)
    gbias = jnp.zeros((1, LANES), F32).at[0, G_I:G_I + MLSTM_HEADS].set(b_i).at[0, G_F:G_F + MLSTM_HEADS].set(b_f)
    full = lambda shape: pl.BlockSpec(shape, lambda b, i: (0,) * len(shape))
    return pl.pallas_call(
        _mixer0_kernel,
        out_shape=jax.ShapeDtypeStruct((bsz, s, GLA_V + ML_W), BF16),
        grid=(bsz // nr, s // L),
        in_specs=[
            pl.BlockSpec((nr, L, n), lambda b, i: (b, i, 0)),
            full((LANES, GLA_QK)),
            full((1, GLA_QK)),
            full((1, GLA_V)),
            full((MLSTM_CONV, 2 * ML_W)),
            full((1, 2 * ML_W)),
            full((1, LANES)),
            full((1, ML_W)),
        ],
        out_specs=pl.BlockSpec((nr, L, GLA_V + ML_W), lambda b, i: (b, i, 0)),
        scratch_shapes=[
            pltpu.VMEM((nr, GLA_DV, GLA_QK), F32),
            pltpu.VMEM((nr, MLSTM_HEADS, MLSTM_DH, 2 * MLSTM_DH), F32),
            pltpu.VMEM((nr, 1, LANES), F32),
            pltpu.VMEM((nr, 8 + L, 2 * ML_W), F32),
        ],
        compiler_params=_params("parallel", "arbitrary"),
        name="mixer0",
    )(proj, wa_pad, ba.reshape(1, GLA_QK), gla_g.reshape(1, GLA_V), conv_w, conv_b.reshape(1, 2 * ML_W),
      gbias, ml_g.reshape(1, ML_W))


def _sb_kernel(q_ref, k_ref, v_ref, o_ref, qs_ref, acc_ref, car_ref, mar_ref, kn_ref):
    TQ, SUB, WIDE = SB_TQ, SB_SUB, SB_WIDE
    ns = TQ // SUB
    RB = 2 * SUB
    i = pl.program_id(2)
    lane = _iota((1, LANES), 1)
    head0 = (lane >> _HEAD_SHIFT) == 0

    @pl.when(i == 0)
    def _():
        k2 = jnp.square(k_ref[0].astype(F32))
        for h in range(2):
            n2 = jnp.sum(jnp.where(head0 if h == 0 else ~head0, k2, 0.0), axis=1, keepdims=True)
            kn_ref[h:h + 1, :] = jnp.broadcast_to(jnp.sqrt(jnp.max(n2, axis=0, keepdims=True)), (1, LANES))

    for a in range(ns):
        qa = q_ref[0, a * SUB:(a + 1) * SUB, :].astype(F32) * (SB_DH ** -0.5 * LOG2E)
        for h in range(2):
            rows = slice(a * RB + h * SUB, a * RB + (h + 1) * SUB)
            qh = jnp.where(head0 if h == 0 else ~head0, qa, 0.0).astype(BF16)
            qs_ref[rows, :] = qh
            qn = jnp.sqrt(jnp.sum(jnp.square(qh.astype(F32)), axis=1, keepdims=True))
            mar_ref[rows, :] = qn * kn_ref[h:h + 1, :] * SB_BOUND_SLACK + SB_ZERO_BITS
    tri = jnp.where(_iota((WIDE, WIDE), 0) >= _iota((WIDE, WIDE), 1), -1.0, 0.0).astype(BF16)

    def key_rows(ku, nk):
        r0 = ku * SUB
        return pl.ds(r0 if isinstance(r0, int) else pl.multiple_of(r0, SUB), nk)

    def logits(a, ku, nk):
        return _dot_nt(qs_ref[a * RB:(a + 1) * RB, :], k_ref[0, key_rows(ku, nk), :])

    def softplus2(z, diagonal):
        sp = jnp.maximum(z, 0.0) + jnp.log(1.0 + jnp.exp2(-jnp.abs(z))) * LOG2E
        if not diagonal:
            return sp, None
        mask = _iota(z.shape, 1) < (_iota(z.shape, 0) & (SUB - 1))
        return jnp.where(mask, sp, 0.0), mask

    def suffix(sp):
        nk = sp.shape[1]
        return _dot(sp.astype(BF16), tri[:nk, :nk]), jnp.sum(sp, axis=1, keepdims=True)

    def weights(z, res, car, mask):
        att = jnp.exp2(z + res if car is None else z + res - car)
        if mask is not None:
            att = jnp.where(mask, att, 0.0)
        return att.astype(BF16)

    def far_tile(a, ku):
        rows = slice(a * RB, (a + 1) * RB)
        z = logits(a, ku, SUB)
        sp, _ = softplus2(z, False)
        res, tot = suffix(sp)
        car = car_ref[rows, :]
        acc_ref[rows, :] += _dot(weights(z, res, car, None), v_ref[0, key_rows(ku, SUB), :])
        car_ref[rows, :] = car + tot

    def near_tiles(first_block):
        chains = []
        for a in range(ns):
            g = a if first_block else ns * i + a
            chain = [(a, g, SUB, True)]
            if not first_block or a >= 2:
                chain.append((a, g - 2, WIDE, False))
            elif a == 1:
                chain.append((a, 0, SUB, False))
            chains.append(chain)
        jobs = [job for step in itertools.zip_longest(*chains) for job in step if job is not None]
        zs = [logits(a, ku, nk) for a, ku, nk, _ in jobs]
        sps = [softplus2(z, diagonal) for z, (_, _, _, diagonal) in zip(zs, jobs)]
        sufs = [suffix(sp) for sp, _ in sps]
        car = [None] * ns
        atts = []
        for z, (sp, mask), (res, tot), (a, _, _, _) in zip(zs, sps, sufs, jobs):
            atts.append(weights(z, res, car[a], mask))
            car[a] = tot if car[a] is None else car[a] + tot
        acc = [None] * ns
        for att, (a, ku, nk, _) in zip(atts, jobs):
            t = _dot(att, v_ref[0, key_rows(ku, nk), :])
            acc[a] = t if acc[a] is None else acc[a] + t
        for a in range(ns):
            acc_ref[a * RB:(a + 1) * RB, :] = acc[a]
            car_ref[a * RB:(a + 1) * RB, :] = jnp.broadcast_to(car[a], (RB, LANES))

    pl.when(i == 0)(functools.partial(near_tiles, True))
    pl.when(i > 0)(functools.partial(near_tiles, False))

    def next_unit(a, t):
        return ns * i + a - 3 - t

    def pending(t):
        gap = [jnp.where(next_unit(a, t) >= 0, mar_ref[a * RB:(a + 1) * RB, :] - car_ref[a * RB:(a + 1) * RB, :], -1.0)
               for a in range(ns)]
        return jnp.max(functools.reduce(jnp.maximum, gap)) > 0.0

    def far_body(c):
        t, _ = c
        for a in range(ns):
            ku = next_unit(a, t)
            pl.when(ku >= 0)(functools.partial(far_tile, a, ku))
        return t + 1, pending(t + 1)

    lax.while_loop(lambda c: c[1], far_body, (0, pending(0)))

    for a in range(ns):
        o_ref[0, a * SUB:(a + 1) * SUB, :] = jnp.where(
            head0, acc_ref[a * RB:a * RB + SUB, :], acc_ref[a * RB + SUB:(a + 1) * RB, :]).astype(o_ref.dtype)


def _stick_breaking(pc):
    bsz, s, _ = pc.shape
    npair = SB_W // LANES
    return pl.pallas_call(
        _sb_kernel,
        out_shape=jax.ShapeDtypeStruct((bsz, s, SB_W), BF16),
        grid=(bsz, npair, s // SB_TQ),
        in_specs=[
            pl.BlockSpec((1, SB_TQ, LANES), lambda b, p, i: (b, i, p)),
            pl.BlockSpec((1, s, LANES), lambda b, p, i: (b, 0, npair + p)),
            pl.BlockSpec((1, s, LANES), lambda b, p, i: (b, 0, 2 * npair + p)),
        ],
        out_specs=pl.BlockSpec((1, SB_TQ, LANES), lambda b, p, i: (b, i, p)),
        scratch_shapes=[
            pltpu.VMEM((2 * SB_TQ, LANES), BF16),
            pltpu.VMEM((2 * SB_TQ, LANES), F32),
            pltpu.VMEM((2 * SB_TQ, LANES), F32),
            pltpu.VMEM((2 * SB_TQ, LANES), F32),
            pltpu.VMEM((8, LANES), F32),
        ],
        compiler_params=_params("parallel", "parallel", "arbitrary"),
        name="stick_breaking",
    )(pc, pc, pc)


def _t5_bucket(dist):
    max_exact = N_REL_BUCKETS // 2
    dd = np.maximum(dist, 1).astype(np.float64)
    large = max_exact + (np.log(dd / max_exact) / np.log(REL_MAX_DIST / max_exact)
                         * (N_REL_BUCKETS - max_exact)).astype(np.int32)
    large = np.minimum(large, N_REL_BUCKETS - 1)
    return np.where(dist < max_exact, dist, large).astype(np.int32)


def _dil_buckets():
    qi = np.arange(DIL_BLK)[:, None]
    kj = np.arange(2 * DIL_BLK)[None, :]
    delta = qi - kj + DIL_BLK
    in_win = (delta >= 0) & (delta <= DIL_BLK)
    tabs = []
    for window, dil in DIL_PAIRS:
        assert window // dil == DIL_BLK
        bucket = _t5_bucket(np.clip(delta, 0, None) * dil)
        tabs.append(np.where(in_win, bucket, -1).astype(np.int32))
    return np.stack(tabs, 0)


def _dil_kernel(q_ref, k_ref, v_ref, bkt_ref, bkt0_ref, tab_ref, o_ref, m_sc, l_sc, acc_sc, bias0_sc, bias_sc):
    s_len = q_ref.shape[1]
    first_batch = pl.program_id(0) == 0
    hp = pl.program_id(1)
    g = pl.program_id(2)
    blk = DIL_BLK
    ng = len(DIL_PAIRS)
    lane = _iota((1, LANES), 1)
    lane_head = lane >> _HEAD_SHIFT
    sel0 = lane_head == 0
    kcol = _iota((2 * blk, 2 * blk), 1)

    def head_biases(gi, bkt, dst):
        @pl.when(first_batch)
        def _():
            n = bkt.shape[0]
            for hl in range(2):
                head = gi * DIL_HEADS_PER_GROUP + hp * 2 + hl
                bias = jnp.full(bkt.shape, NEG_INF, F32)
                for bk in range(N_REL_BUCKETS):
                    bias = jnp.where(bkt == bk, tab_ref[bk, head], bias)
                dst[hl * n:(hl + 1) * n, :] = bias
        return dst[...]

    def stack_heads(qb):
        return jnp.concatenate([jnp.where(sel0, qb, 0.0), jnp.where(sel0, 0.0, qb)], axis=0).astype(BF16)

    def with_ones(vv):
        return jnp.concatenate([vv, jnp.ones_like(vv)], axis=1).astype(BF16)

    def first_group(gi, dil):
        n_sub = s_len // dil
        bias2 = head_biases(gi, bkt0_ref[...], bias0_sc.at[hp])

        def residue(r):
            rows = pl.ds(r, n_sub, stride=dil)
            logits = _dot_nt(stack_heads(q_ref[0, rows, :] * (DIL_DH ** -0.5)), k_ref[0, rows, :].astype(BF16)) + bias2
            return rows, logits

        def body(tt, carry):
            s1 = [residue(tt * DIL_UNROLL + u) for u in range(DIL_UNROLL)]
            s2 = [jnp.max(lg, axis=1, keepdims=True) for _, lg in s1]
            s3 = [_dot(jnp.exp(lg - mc).astype(BF16), with_ones(v_ref[0, rows, :])) for (rows, lg), mc in zip(s1, s2)]
            for (rows, _), mc, pv in zip(s1, s2, s3):
                m_sc[rows, :] = jnp.where(sel0, mc[0:n_sub], mc[n_sub:])
                l_sc[rows, :] = jnp.where(sel0, pv[0:n_sub, LANES:], pv[n_sub:, LANES:])
                acc_sc[rows, :] = jnp.where(sel0, pv[0:n_sub, :LANES], pv[n_sub:, :LANES])
            return carry

        lax.fori_loop(0, dil // DIL_UNROLL, body, 0)

    def group(gi, dil):
        nb = s_len // (blk * dil)
        bias2 = head_biases(gi, bkt_ref[gi], bias_sc.at[hp, gi])

        def stage_logits(t):
            r = t // nb
            n = t % nb
            q_start = r + dil * blk * n
            p_start = r + dil * blk * jnp.maximum(n - 1, 0)
            rows_q = pl.ds(q_start, blk, stride=dil) if dil > 1 else pl.ds(q_start, blk)
            rows_p = pl.ds(p_start, blk, stride=dil) if dil > 1 else pl.ds(p_start, blk)
            q2 = stack_heads(q_ref[0, rows_q, :] * (DIL_DH ** -0.5))
            kk = jnp.concatenate([k_ref[0, rows_p, :], k_ref[0, rows_q, :]], axis=0).astype(BF16)
            logits = _dot_nt(q2, kk) + bias2
            logits = jnp.where(jnp.logical_and(n == 0, kcol < blk), NEG_INF, logits)
            return rows_q, rows_p, logits

        def stage_pv(rows_q, rows_p, logits, m_col):
            p = jnp.exp(logits - m_col)
            vv = with_ones(jnp.concatenate([v_ref[0, rows_p, :], v_ref[0, rows_q, :]], axis=0))
            pv = _dot(p.astype(BF16), vv)
            return pv[:, LANES:], pv[:, :LANES]

        def body(tt, carry):
            ts = [tt * DIL_UNROLL + u for u in range(DIL_UNROLL)]
            s1 = [stage_logits(t) for t in ts]
            s2 = [jnp.max(lg, axis=1, keepdims=True) for _, _, lg in s1]
            s3 = [stage_pv(rq, rp, lg, mc) for (rq, rp, lg), mc in zip(s1, s2)]
            outs = []
            for (rows_q, _, _), m_col, (ps, pv) in zip(s1, s2, s3):
                m_old = m_sc[rows_q, :]
                m_blk = jnp.where(sel0, m_col[0:blk], m_col[blk:])
                m_new = jnp.maximum(m_old, m_blk)
                a_old = jnp.exp(m_old - m_new)
                a_blk = jnp.exp(m_blk - m_new)
                l_new = a_old * l_sc[rows_q, :] + a_blk * jnp.where(sel0, ps[0:blk], ps[blk:])
                a_new = a_old * acc_sc[rows_q, :] + a_blk * jnp.where(sel0, pv[0:blk], pv[blk:])
                outs.append((rows_q, m_new, l_new, a_new))
            for rows_q, m_new, l_new, a_new in outs:
                m_sc[rows_q, :] = m_new
                l_sc[rows_q, :] = l_new
                acc_sc[rows_q, :] = a_new
            return carry

        lax.fori_loop(0, s_len // (blk * DIL_UNROLL), body, 0)

    for gi, (_, dil) in enumerate(DIL_PAIRS):
        pl.when(g == ng - 1 - gi)(functools.partial(first_group if gi == ng - 1 else group, gi, dil))

    @pl.when(g == ng - 1)
    def _():
        o_ref[0] = (acc_sc[...] / l_sc[...]).astype(o_ref.dtype)


def _dilated(pd, rel_bias):
    bsz, s, _ = pd.shape
    ng = len(DIL_PAIRS)
    gw = DIL_HEADS_PER_GROUP * DIL_DH // LANES
    nq = DIL_W // LANES
    bkt = jnp.asarray(_dil_buckets())
    dil0 = DIL_PAIRS[-1][1]
    n_sub = s // dil0
    assert s % (DIL_BLK * dil0) == 0 and dil0 % DIL_UNROLL == 0
    delta = np.arange(n_sub)[:, None] - np.arange(n_sub)[None, :]
    bkt0 = jnp.asarray(np.where((delta >= 0) & (delta <= DIL_BLK), _t5_bucket(np.clip(delta, 0, None) * dil0), -1)
                       .astype(np.int32))
    blk_spec = lambda base: pl.BlockSpec((1, s, LANES), lambda b, p, g: (b, 0, base + (ng - 1 - g) * gw + p))
    return pl.pallas_call(
        _dil_kernel,
        out_shape=jax.ShapeDtypeStruct((bsz, s, DIL_OUT), BF16),
        grid=(bsz, gw, ng),
        in_specs=[
            blk_spec(0),
            blk_spec(nq),
            blk_spec(2 * nq),
            pl.BlockSpec((ng, DIL_BLK, 2 * DIL_BLK), lambda b, p, g: (0, 0, 0)),
            pl.BlockSpec((n_sub, n_sub), lambda b, p, g: (0, 0)),
            pl.BlockSpec(memory_space=pltpu.SMEM),
        ],
        out_specs=pl.BlockSpec((1, s, LANES), lambda b, p, g: (b, 0, p)),
        scratch_shapes=[pltpu.VMEM((s, LANES), F32)] * 3 + [
            pltpu.VMEM((gw, 2 * n_sub, n_sub), F32),
            pltpu.VMEM((gw, ng - 1, 2 * DIL_BLK, 2 * DIL_BLK), F32),
        ],
        compiler_params=_params("arbitrary", "arbitrary", "arbitrary"),
        name="dilated",
    )(pd, pd, pd, bkt, bkt0, rel_bias)


def _ab_weight(w):
    o = np.cumsum((0,) + (GLA_QK, GLA_QK, GLA_V, GLA_V, GLA_GATE_RANK, 2 * ML_W, ML_W, MLSTM_HEADS, MLSTM_HEADS, ML_W))
    qa, ka, va, ra, aa, qkb, vb, ib, fb, ob = [w[:, o[j]:o[j + 1]] for j in range(10)]
    pad = jnp.zeros((w.shape[0], LANES - 2 * MLSTM_HEADS - GLA_GATE_RANK), w.dtype)
    return jnp.concatenate([qa, ka, va, ra, qkb, vb, ob, ib, fb, aa, pad], axis=1).astype(BF16)


def kernel(x, c, ada_w, ada_b, ln_g, ln_b, ab_w_in, gla_wa_up, gla_ba, gla_norm_g, ml_conv_w, ml_conv_b,
           ml_b_i, ml_b_f, ml_norm_g, ab_w_out, cd_w_in, rel_bias, cd_w_out, ffn_w1, ffn_w3, ffn_w2):
    bsz, s, d = x.shape
    mod_all = _ada_mod(c, ada_w, ada_b).reshape(DEPTH, bsz, 6, d)
    tm = FFN_TM
    for layer in range(DEPTH):
        mod = mod_all[layer]
        j = layer // 2
        if layer % 2 == 0:
            (proj,) = _inproj(x, mod, _ab_weight(ab_w_in[j]), [(P0_N, F32)], tm, 0, 1, "inproj0")
            cat = _mixer0(proj, gla_wa_up[j], gla_ba[j], gla_norm_g[j], ml_conv_w[j], ml_conv_b[j],
                          ml_b_i[j], ml_b_f[j], ml_norm_g[j])
            acts, w_outs = [cat], [ab_w_out[j].astype(BF16)]
        else:
            pc, pd = _inproj(x, mod, cd_w_in[j].astype(BF16), [(3 * SB_W, BF16), (3 * DIL_W, F32)],
                             tm, 0, 1, "inproj1")
            oc = _stick_breaking(pc)
            od = _dilated(pd, rel_bias)
            w_out = cd_w_out[j].astype(BF16)
            acts, w_outs = [oc, od], [w_out[:SB_W], w_out[SB_W:]]
        x = _post(acts, w_outs, x, mod, ffn_w1[layer], ffn_w3[layer], ffn_w2[layer], ln_g[layer], ln_b[layer], FFN_TM)
    return x
```

```python
import functools
import itertools

import numpy as np
import jax
import jax.numpy as jnp
from jax import lax
from jax.experimental import pallas as pl
from jax.experimental.pallas import tpu as pltpu

F32 = jnp.float32
BF16 = jnp.bfloat16

D_MODEL = 1024
DEPTH = 2
GLA_HEADS = 4
GLA_DK = 64
GLA_DV = 128
GLA_GATE_RANK = 16
GLA_TAU = 16.0
GLA_CHUNK = 64
MLSTM_HEADS = 4
MLSTM_DH = 128
MLSTM_CONV = 4
SB_HEADS = 8
SB_DH = 64
DIL_PAIRS = ((128, 1), (512, 4), (2048, 16))
DIL_HEADS_PER_GROUP = 4
DIL_DH = 64
DIL_BLK = 128
DIL_UNROLL = 4
N_REL_BUCKETS = 32
REL_MAX_DIST = 2048
D_FF = ((8 * D_MODEL + 3 * 256 - 1) // (3 * 256)) * 256
LN_EPS = 1e-5
RES_ALPHA = (2 * DEPTH) ** 0.25
NEG_INF = -1e30

GLA_QK = GLA_HEADS * GLA_DK
GLA_V = GLA_HEADS * GLA_DV
ML_W = MLSTM_HEADS * MLSTM_DH
SB_W = SB_HEADS * SB_DH
DIL_W = len(DIL_PAIRS) * DIL_HEADS_PER_GROUP * DIL_DH
DIL_OUT = DIL_HEADS_PER_GROUP * DIL_DH

LANES = 128
VMEM_LIMIT = 56 * 1024 * 1024

P0_QA = 0
P0_KA = P0_QA + GLA_QK
P0_VA = P0_KA + GLA_QK
P0_RA = P0_VA + GLA_V
P0_QKB = P0_RA + GLA_V
P0_VB = P0_QKB + 2 * ML_W
P0_OB = P0_VB + ML_W
P0_G = P0_OB + ML_W
P0_N = P0_G + LANES
G_I = 0
G_F = MLSTM_HEADS
G_A = 2 * MLSTM_HEADS

MIX_TILE = 256
MIX_ROWS = 4
FF_CHUNK = 256
FFN_TM = 512
SB_TQ = 2048
SB_SUB = 128
SB_WIDE = 256
SB_ZERO_BITS = 160.0
SB_BOUND_SLACK = 1.01
LOG2E = 1.4426950408889634
_CHUNK_SHIFT = 6
_HEAD_SHIFT = 6
assert GLA_CHUNK == GLA_DK == 1 << _CHUNK_SHIFT and SB_DH == DIL_DH == 1 << _HEAD_SHIFT


def _dot(a, b):
    return jnp.dot(a, b, preferred_element_type=F32)


def _dot_nt(a, b):
    return lax.dot_general(a, b, (((1,), (1,)), ((), ())), preferred_element_type=F32)


def _dot_tn(a, b):
    return lax.dot_general(a, b, (((0,), (0,)), ((), ())), preferred_element_type=F32)


def _dot_split(t, x, terms):
    acc = None
    rem = x
    for i in range(terms):
        part = rem.astype(BF16)
        d = _dot(t, part)
        acc = d if acc is None else acc + d
        if i + 1 < terms:
            rem = rem - part.astype(F32)
    return acc


def _log_sigmoid(x):
    return jnp.minimum(x, 0.0) - jnp.log(1.0 + jnp.exp(-jnp.abs(x)))


def _silu(x):
    return x * jax.nn.sigmoid(x)


def _layer_norm(r, g, b):
    mu = jnp.mean(r, axis=-1, keepdims=True)
    d = r - mu
    var = jnp.mean(d * d, axis=-1, keepdims=True)
    return d * lax.rsqrt(var + LN_EPS) * g + b


def _head_norm(x, g):
    mu = jnp.mean(x, axis=-1, keepdims=True)
    d = x - mu
    var = jnp.mean(d * d, axis=-1, keepdims=True)
    return d * lax.rsqrt(var + LN_EPS) * g


def _iota(shape, dim):
    return lax.broadcasted_iota(jnp.int32, shape, dim)


def _params(*sem):
    return pltpu.CompilerParams(dimension_semantics=sem, vmem_limit_bytes=VMEM_LIMIT)


def _ada_kernel(c_ref, w_ref, b_ref, o_ref):
    ca = _silu(c_ref[...]).astype(BF16)
    o_ref[0] = _dot(ca, w_ref[0].astype(BF16)) + b_ref[0]


def _ada_mod(c, ada_w, ada_b):
    bsz, d = c.shape
    n = ada_w.shape[-1]
    tn = n // 4
    return pl.pallas_call(
        _ada_kernel,
        out_shape=jax.ShapeDtypeStruct((DEPTH, bsz, n), F32),
        grid=(DEPTH, n // tn),
        in_specs=[
            pl.BlockSpec((bsz, d), lambda l, j: (0, 0)),
            pl.BlockSpec((1, d, tn), lambda l, j: (l, 0, j)),
            pl.BlockSpec((1, 1, tn), lambda l, j: (l, 0, j)),
        ],
        out_specs=pl.BlockSpec((1, bsz, tn), lambda l, j: (l, 0, j)),
        compiler_params=_params("arbitrary", "arbitrary"),
        name="ada_mod",
    )(c, ada_w, ada_b.reshape(DEPTH, 1, n))


def _inproj_kernel(x_ref, mod_ref, w_ref, *o_refs, shift_row, scale_row):
    sh = mod_ref[0, shift_row:shift_row + 1, :]
    sc = mod_ref[0, scale_row:scale_row + 1, :]
    hm = (x_ref[0] * (1.0 + sc) + sh).astype(BF16)
    col = 0
    for o_ref in o_refs:
        n = o_ref.shape[-1]
        o_ref[0] = _dot(hm, w_ref[:, col:col + n]).astype(o_ref.dtype)
        col += n


def _inproj(x, mod, w, outs, tm, shift_row, scale_row, name):
    bsz, s, d = x.shape
    n = w.shape[1]
    assert sum(o[0] for o in outs) == n
    return pl.pallas_call(
        functools.partial(_inproj_kernel, shift_row=shift_row, scale_row=scale_row),
        out_shape=[jax.ShapeDtypeStruct((bsz, s, o[0]), o[1]) for o in outs],
        grid=(bsz, s // tm),
        in_specs=[
            pl.BlockSpec((1, tm, d), lambda b, i: (b, i, 0)),
            pl.BlockSpec((1, 6, d), lambda b, i: (b, 0, 0)),
            pl.BlockSpec((d, n), lambda b, i: (0, 0)),
        ],
        out_specs=[pl.BlockSpec((1, tm, o[0]), lambda b, i: (b, i, 0)) for o in outs],
        compiler_params=_params("parallel", "arbitrary"),
        name=name,
    )(x, mod, w)


def _post_kernel(*refs, n_act):
    act_refs = refs[:n_act]
    wo_refs = refs[n_act:2 * n_act]
    x_ref, mod_ref, w1_ref, w3_ref, w2_ref, g_ref, b_ref, o_ref, gm_ref = refs[2 * n_act:]
    y = None
    for a_ref, w_ref in zip(act_refs, wo_refs):
        t = _dot(a_ref[0], w_ref[...])
        y = t if y is None else y + t
    r = RES_ALPHA * x_ref[0] + (1.0 + mod_ref[0, 2:3, :]) * y
    xm = _layer_norm(r, g_ref[0:1, :], b_ref[0:1, :])
    hf = (xm * (1.0 + mod_ref[0, 4:5, :]) + mod_ref[0, 3:4, :]).astype(BF16)
    for c in range(D_FF // FF_CHUNK):
        cs = slice(c * FF_CHUNK, (c + 1) * FF_CHUNK)
        gm_ref[:, cs] = (_silu(_dot(hf, w1_ref[:, cs])) * _dot(hf, w3_ref[:, cs])).astype(BF16)
    y2 = _dot(gm_ref[...], w2_ref[...])
    r2 = RES_ALPHA * xm + (1.0 + mod_ref[0, 5:6, :]) * y2
    o_ref[0] = _layer_norm(r2, g_ref[1:2, :], b_ref[1:2, :])


def _post(acts, w_outs, x, mod, w1, w3, w2, ln_g, ln_b, tm):
    bsz, s, d = x.shape
    n_act = len(acts)
    const = lambda shape: pl.BlockSpec(shape, lambda b, i: (0,) * len(shape), pipeline_mode=pl.Buffered(1))
    in_specs = [pl.BlockSpec((1, tm, a.shape[-1]), lambda b, i: (b, i, 0)) for a in acts]
    in_specs += [const(w.shape) for w in w_outs]
    in_specs += [
        pl.BlockSpec((1, tm, d), lambda b, i: (b, i, 0)),
        pl.BlockSpec((1, 6, d), lambda b, i: (b, 0, 0)),
        const((d, D_FF)),
        const((d, D_FF)),
        const((D_FF, d)),
        const((2, d)),
        const((2, d)),
    ]
    return pl.pallas_call(
        functools.partial(_post_kernel, n_act=n_act),
        out_shape=jax.ShapeDtypeStruct((bsz, s, d), F32),
        grid=(bsz, s // tm),
        in_specs=in_specs,
        out_specs=pl.BlockSpec((1, tm, d), lambda b, i: (b, i, 0)),
        scratch_shapes=[pltpu.VMEM((tm, D_FF), BF16)],
        compiler_params=_params("parallel", "arbitrary"),
        name="post",
    )(*acts, *w_outs, x, mod, w1.astype(BF16), w3.astype(BF16), w2.astype(BF16), ln_g, ln_b)


def _mixer0_kernel(p_ref, wa_ref, ba_ref, gg_ref, cw_ref, cb_ref, gb_ref, mg_ref, o_ref,
                   st_ref, cst_ref, m_ref, xc_ref):
    @pl.when(pl.program_id(1) == 0)
    def _():
        st_ref[...] = jnp.zeros_like(st_ref)
        cst_ref[...] = jnp.zeros_like(cst_ref)
        m_ref[...] = jnp.zeros_like(m_ref)
        xc_ref[:, 0:8, :] = jnp.zeros((xc_ref.shape[0], 8, 2 * ML_W), F32)

    for b in range(p_ref.shape[0]):
        _mixer0_row(p_ref.at[pl.ds(b, 1)], wa_ref, ba_ref, gg_ref, cw_ref, cb_ref, gb_ref, mg_ref,
                    o_ref.at[pl.ds(b, 1)], st_ref.at[b], cst_ref.at[b], m_ref.at[b], xc_ref.at[b])


def _mixer0_row(p_ref, wa_ref, ba_ref, gg_ref, cw_ref, cb_ref, gb_ref, mg_ref, o_ref, st_ref, cst_ref, m_ref, xc_ref):
    L = MIX_TILE
    lane = _iota((1, LANES), 1)
    g_raw = p_ref[0, :, P0_G:P0_G + LANES]

    u = _dot(g_raw.astype(BF16), wa_ref[...]) + ba_ref[...]
    la = _log_sigmoid(u) * (1.0 / GLA_TAU)
    row = _iota((L, L), 0)
    col = _iota((L, L), 1)
    same_chunk = (row >> _CHUNK_SHIFT) == (col >> _CHUNK_SHIFT)
    tri_blk = jnp.where(same_chunk & (col <= row), 1.0, 0.0).astype(BF16)
    bcs = _dot_split(tri_blk, la, 3)
    q_in = p_ref[0, :, P0_QA:P0_QA + GLA_QK] * (GLA_DK ** -0.5) * jnp.exp(bcs)
    k_raw = p_ref[0, :, P0_KA:P0_KA + GLA_QK]
    k_in = k_raw * jnp.exp(-bcs)
    lane_qk = _iota((1, GLA_QK), 1)
    head_masks = [(lane_qk >> _CHUNK_SHIFT) == h for h in range(GLA_HEADS)]
    r4 = _iota((GLA_HEADS * GLA_CHUNK, GLA_CHUNK), 0)
    c4 = _iota((GLA_HEADS * GLA_CHUNK, GLA_CHUNK), 1)
    tril4 = c4 <= (r4 & (GLA_CHUNK - 1))
    oa_chunks = []
    for c in range(L // GLA_CHUNK):
        r0, r1 = c * GLA_CHUNK, (c + 1) * GLA_CHUNK
        b_c = bcs[r0:r1]
        bl = b_c[GLA_CHUNK - 1:GLA_CHUNK, :]
        q_c = q_in[r0:r1]
        k_c = k_in[r0:r1].astype(BF16)
        k_end = k_raw[r0:r1] * jnp.exp(bl - b_c)
        v_c = p_ref[0, r0:r1, P0_VA:P0_VA + GLA_V]
        q_exp = jnp.concatenate([jnp.where(hm, q_c, 0.0) for hm in head_masks], axis=0).astype(BF16)
        k_exp = jnp.concatenate([jnp.where(hm, k_end, 0.0) for hm in head_masks], axis=0).astype(BF16)
        sc = jnp.where(tril4, _dot_nt(q_exp, k_c), 0.0)
        intra = _dot(sc.astype(BF16), v_c.astype(BF16))
        inter = _dot_nt(q_exp, st_ref[...].astype(BF16))
        o_heads = []
        for h in range(GLA_HEADS):
            h0, h1 = h * GLA_CHUNK, (h + 1) * GLA_CHUNK
            o_heads.append(intra[h0:h1, h * GLA_DV:(h + 1) * GLA_DV] + inter[h0:h1])
        oa_chunks.append(o_heads)
        v_cat = jnp.concatenate([v_c[:, h * GLA_DV:(h + 1) * GLA_DV] for h in range(GLA_HEADS)], axis=0)
        st_ref[...] = st_ref[...] * jnp.exp(bl) + _dot_tn(v_cat.astype(BF16), k_exp)
    for h in range(GLA_HEADS):
        o_h = jnp.concatenate([oc[h] for oc in oa_chunks], axis=0)
        cs = slice(h * GLA_DV, (h + 1) * GLA_DV)
        ra = p_ref[0, :, P0_RA + h * GLA_DV:P0_RA + (h + 1) * GLA_DV]
        o_ref[0, :, cs] = (_head_norm(o_h, gg_ref[:, cs]) * _silu(ra)).astype(o_ref.dtype)

    xc_ref[8:8 + L, :] = p_ref[0, :, P0_QKB:P0_QKB + 2 * ML_W]
    conv = cb_ref[...]
    for kk in range(MLSTM_CONV):
        conv = conv + cw_ref[kk:kk + 1, :] * xc_ref[8 - (MLSTM_CONV - 1) + kk:8 - (MLSTM_CONV - 1) + kk + L, :]
    xc_ref[0:8, :] = xc_ref[L:L + 8, :]
    qk_b = _silu(conv)

    gb = jnp.where(lane < G_A, g_raw + gb_ref[...], 0.0)
    ipre = gb
    logf = _log_sigmoid(pltpu.roll(gb, LANES - G_F, axis=1))
    logf = jnp.where(lane < MLSTM_HEADS, logf, 0.0)
    tri = jnp.where(col <= row, 1.0, 0.0).astype(BF16)
    bcum = _dot_split(tri, logf, 3)
    blast = bcum[L - 1:L, :]
    wend = blast - bcum + ipre
    m_prev = m_ref[...]
    m_new = jnp.maximum(blast + m_prev, jnp.max(wend, axis=0, keepdims=True))
    scl = jnp.exp(blast + m_prev - m_new)
    wj = jnp.exp(wend - m_new)
    rows_src = jnp.where(lane < MLSTM_HEADS, ipre, pltpu.roll(bcum, MLSTM_HEADS, axis=1))
    rows_t = rows_src.T
    causal = col <= row
    ones = jnp.ones((L, MLSTM_DH), F32)
    for h in range(MLSTM_HEADS):
        hs = slice(h * MLSTM_DH, (h + 1) * MLSTM_DH)
        wide = lambda t: jnp.concatenate([t, t], axis=1)
        b_rep = jnp.broadcast_to(bcum[:, h:h + 1], (L, LANES))
        wj_rep = jnp.broadcast_to(wj[:, h:h + 1], (L, LANES))
        ip_row = rows_t[h:h + 1, :]
        b_row = rows_t[MLSTM_HEADS + h:MLSTM_HEADS + h + 1, :]
        dlog = jnp.where(causal, wide(b_rep) - b_row + ip_row, NEG_INF)
        il = b_rep + m_prev[:, h:h + 1]
        m_i = jnp.maximum(il, jnp.broadcast_to(jnp.max(dlog, axis=1, keepdims=True), (L, LANES)))
        w_intra = jnp.exp(dlog - wide(m_i))
        s_inter = jnp.exp(il - m_i)
        q_h = qk_b[:, h * MLSTM_DH:(h + 1) * MLSTM_DH].astype(BF16)
        k_h = (qk_b[:, ML_W + h * MLSTM_DH:ML_W + (h + 1) * MLSTM_DH] * (MLSTM_DH ** -0.5)).astype(BF16)
        v_h = p_ref[0, :, P0_VB + h * MLSTM_DH:P0_VB + (h + 1) * MLSTM_DH]
        v_aug = jnp.concatenate([v_h, ones], axis=1)
        a = (w_intra * _dot_nt(q_h, k_h)).astype(BF16)
        c_prev = cst_ref[h]
        tot = _dot(a, v_aug.astype(BF16)) + wide(s_inter) * _dot(q_h, c_prev.astype(BF16))
        num = tot[:, :MLSTM_DH]
        den = tot[:, MLSTM_DH:]
        hid = num / jnp.maximum(jnp.abs(den), jnp.exp(-m_i))
        w_aug = (v_aug * wide(wj_rep)).astype(BF16)
        cst_ref[h] = scl[:, h:h + 1] * c_prev + _dot_tn(k_h, w_aug)
        ob = p_ref[0, :, P0_OB + h * MLSTM_DH:P0_OB + (h + 1) * MLSTM_DH]
        o_ref[0, :, GLA_V + h * MLSTM_DH:GLA_V + (h + 1) * MLSTM_DH] = (
            jax.nn.sigmoid(ob) * _head_norm(hid, mg_ref[:, hs])).astype(o_ref.dtype)
    m_ref[...] = m_new


def _mixer0(proj, wa_up, ba, gla_g, conv_w, conv_b, b_i, b_f, ml_g):
    bsz, s, n = proj.shape
    L = MIX_TILE
    nr = MIX_ROWS if bsz % MIX_ROWS == 0 else 1
    wa_pad = jnp.zeros((LANES, GLA_QK), F32).at[G_A:G_A + GLA_GATE_RANK].set(wa_up).astype(BF16)
    gbias = jnp.zeros((1, LANES), F32).at[0, G_I:G_I + MLSTM_HEADS].set(b_i).at[0, G_F:G_F + MLSTM_HEADS].set(b_f)
    full = lambda shape: pl.BlockSpec(shape, lambda b, i: (0,) * len(shape))
    return pl.pallas_call(
        _mixer0_kernel,
        out_shape=jax.ShapeDtypeStruct((bsz, s, GLA_V + ML_W), BF16),
        grid=(bsz // nr, s // L),
        in_specs=[
            pl.BlockSpec((nr, L, n), lambda b, i: (b, i, 0)),
            full((LANES, GLA_QK)),
            full((1, GLA_QK)),
            full((1, GLA_V)),
            full((MLSTM_CONV, 2 * ML_W)),
            full((1, 2 * ML_W)),
            full((1, LANES)),
            full((1, ML_W)),
        ],
        out_specs=pl.BlockSpec((nr, L, GLA_V + ML_W), lambda b, i: (b, i, 0)),
        scratch_shapes=[
            pltpu.VMEM((nr, GLA_DV, GLA_QK), F32),
            pltpu.VMEM((nr, MLSTM_HEADS, MLSTM_DH, 2 * MLSTM_DH), F32),
            pltpu.VMEM((nr, 1, LANES), F32),
            pltpu.VMEM((nr, 8 + L, 2 * ML_W), F32),
        ],
        compiler_params=_params("parallel", "arbitrary"),
        name="mixer0",
    )(proj, wa_pad, ba.reshape(1, GLA_QK), gla_g.reshape(1, GLA_V), conv_w, conv_b.reshape(1, 2 * ML_W),
      gbias, ml_g.reshape(1, ML_W))


def _sb_kernel(q_ref, k_ref, v_ref, o_ref, qs_ref, acc_ref, car_ref, mar_ref, kn_ref):
    TQ, SUB, WIDE = SB_TQ, SB_SUB, SB_WIDE
    ns = TQ // SUB
    RB = 2 * SUB
    i = pl.program_id(2)
    lane = _iota((1, LANES), 1)
    head0 = (lane >> _HEAD_SHIFT) == 0

    @pl.when(i == 0)
    def _():
        k2 = jnp.square(k_ref[0].astype(F32))
        for h in range(2):
            n2 = jnp.sum(jnp.where(head0 if h == 0 else ~head0, k2, 0.0), axis=1, keepdims=True)
            kn_ref[h:h + 1, :] = jnp.broadcast_to(jnp.sqrt(jnp.max(n2, axis=0, keepdims=True)), (1, LANES))

    for a in range(ns):
        qa = q_ref[0, a * SUB:(a + 1) * SUB, :].astype(F32) * (SB_DH ** -0.5 * LOG2E)
        for h in range(2):
            rows = slice(a * RB + h * SUB, a * RB + (h + 1) * SUB)
            qh = jnp.where(head0 if h == 0 else ~head0, qa, 0.0).astype(BF16)
            qs_ref[rows, :] = qh
            qn = jnp.sqrt(jnp.sum(jnp.square(qh.astype(F32)), axis=1, keepdims=True))
            mar_ref[rows, :] = qn * kn_ref[h:h + 1, :] * SB_BOUND_SLACK + SB_ZERO_BITS
    tri = jnp.where(_iota((WIDE, WIDE), 0) >= _iota((WIDE, WIDE), 1), -1.0, 0.0).astype(BF16)

    def key_rows(ku, nk):
        r0 = ku * SUB
        return pl.ds(r0 if isinstance(r0, int) else pl.multiple_of(r0, SUB), nk)

    def logits(a, ku, nk):
        return _dot_nt(qs_ref[a * RB:(a + 1) * RB, :], k_ref[0, key_rows(ku, nk), :])

    def softplus2(z, diagonal):
        sp = jnp.maximum(z, 0.0) + jnp.log(1.0 + jnp.exp2(-jnp.abs(z))) * LOG2E
        if not diagonal:
            return sp, None
        mask = _iota(z.shape, 1) < (_iota(z.shape, 0) & (SUB - 1))
        return jnp.where(mask, sp, 0.0), mask

    def suffix(sp):
        nk = sp.shape[1]
        return _dot(sp.astype(BF16), tri[:nk, :nk]), jnp.sum(sp, axis=1, keepdims=True)

    def weights(z, res, car, mask):
        att = jnp.exp2(z + res if car is None else z + res - car)
        if mask is not None:
            att = jnp.where(mask, att, 0.0)
        return att.astype(BF16)

    def far_tile(a, ku):
        rows = slice(a * RB, (a + 1) * RB)
        z = logits(a, ku, SUB)
        sp, _ = softplus2(z, False)
        res, tot = suffix(sp)
        car = car_ref[rows, :]
        acc_ref[rows, :] += _dot(weights(z, res, car, None), v_ref[0, key_rows(ku, SUB), :])
        car_ref[rows, :] = car + tot

    def near_tiles(first_block):
        chains = []
        for a in range(ns):
            g = a if first_block else ns * i + a
            chain = [(a, g, SUB, True)]
            if not first_block or a >= 2:
                chain.append((a, g - 2, WIDE, False))
            elif a == 1:
                chain.append((a, 0, SUB, False))
            chains.append(chain)
        jobs = [job for step in itertools.zip_longest(*chains) for job in step if job is not None]
        zs = [logits(a, ku, nk) for a, ku, nk, _ in jobs]
        sps = [softplus2(z, diagonal) for z, (_, _, _, diagonal) in zip(zs, jobs)]
        sufs = [suffix(sp) for sp, _ in sps]
        car = [None] * ns
        atts = []
        for z, (sp, mask), (res, tot), (a, _, _, _) in zip(zs, sps, sufs, jobs):
            atts.append(weights(z, res, car[a], mask))
            car[a] = tot if car[a] is None else car[a] + tot
        acc = [None] * ns
        for att, (a, ku, nk, _) in zip(atts, jobs):
            t = _dot(att, v_ref[0, key_rows(ku, nk), :])
            acc[a] = t if acc[a] is None else acc[a] + t
        for a in range(ns):
            acc_ref[a * RB:(a + 1) * RB, :] = acc[a]
            car_ref[a * RB:(a + 1) * RB, :] = jnp.broadcast_to(car[a], (RB, LANES))

    pl.when(i == 0)(functools.partial(near_tiles, True))
    pl.when(i > 0)(functools.partial(near_tiles, False))

    def next_unit(a, t):
        return ns * i + a - 3 - t

    def pending(t):
        gap = [jnp.where(next_unit(a, t) >= 0, mar_ref[a * RB:(a + 1) * RB, :] - car_ref[a * RB:(a + 1) * RB, :], -1.0)
               for a in range(ns)]
        return jnp.max(functools.reduce(jnp.maximum, gap)) > 0.0

    def far_body(c):
        t, _ = c
        for a in range(ns):
            ku = next_unit(a, t)
            pl.when(ku >= 0)(functools.partial(far_tile, a, ku))
        return t + 1, pending(t + 1)

    lax.while_loop(lambda c: c[1], far_body, (0, pending(0)))

    for a in range(ns):
        o_ref[0, a * SUB:(a + 1) * SUB, :] = jnp.where(
            head0, acc_ref[a * RB:a * RB + SUB, :], acc_ref[a * RB + SUB:(a + 1) * RB, :]).astype(o_ref.dtype)


def _stick_breaking(pc):
    bsz, s, _ = pc.shape
    npair = SB_W // LANES
    return pl.pallas_call(
        _sb_kernel,
        out_shape=jax.ShapeDtypeStruct((bsz, s, SB_W), BF16),
        grid=(bsz, npair, s // SB_TQ),
        in_specs=[
            pl.BlockSpec((1, SB_TQ, LANES), lambda b, p, i: (b, i, p)),
            pl.BlockSpec((1, s, LANES), lambda b, p, i: (b, 0, npair + p)),
            pl.BlockSpec((1, s, LANES), lambda b, p, i: (b, 0, 2 * npair + p)),
        ],
        out_specs=pl.BlockSpec((1, SB_TQ, LANES), lambda b, p, i: (b, i, p)),
        scratch_shapes=[
            pltpu.VMEM((2 * SB_TQ, LANES), BF16),
            pltpu.VMEM((2 * SB_TQ, LANES), F32),
            pltpu.VMEM((2 * SB_TQ, LANES), F32),
            pltpu.VMEM((2 * SB_TQ, LANES), F32),
            pltpu.VMEM((8, LANES), F32),
        ],
        compiler_params=_params("parallel", "parallel", "arbitrary"),
        name="stick_breaking",
    )(pc, pc, pc)


def _t5_bucket(dist):
    max_exact = N_REL_BUCKETS // 2
    dd = np.maximum(dist, 1).astype(np.float64)
    large = max_exact + (np.log(dd / max_exact) / np.log(REL_MAX_DIST / max_exact)
                         * (N_REL_BUCKETS - max_exact)).astype(np.int32)
    large = np.minimum(large, N_REL_BUCKETS - 1)
    return np.where(dist < max_exact, dist, large).astype(np.int32)


def _dil_buckets():
    qi = np.arange(DIL_BLK)[:, None]
    kj = np.arange(2 * DIL_BLK)[None, :]
    delta = qi - kj + DIL_BLK
    in_win = (delta >= 0) & (delta <= DIL_BLK)
    tabs = []
    for window, dil in DIL_PAIRS:
        assert window // dil == DIL_BLK
        bucket = _t5_bucket(np.clip(delta, 0, None) * dil)
        tabs.append(np.where(in_win, bucket, -1).astype(np.int32))
    return np.stack(tabs, 0)


def _dil_kernel(q_ref, k_ref, v_ref, bkt_ref, bkt0_ref, tab_ref, o_ref, m_sc, l_sc, acc_sc, bias0_sc, bias_sc):
    s_len = q_ref.shape[1]
    first_batch = pl.program_id(0) == 0
    hp = pl.program_id(1)
    g = pl.program_id(2)
    blk = DIL_BLK
    ng = len(DIL_PAIRS)
    lane = _iota((1, LANES), 1)
    lane_head = lane >> _HEAD_SHIFT
    sel0 = lane_head == 0
    kcol = _iota((2 * blk, 2 * blk), 1)

    def head_biases(gi, bkt, dst):
        @pl.when(first_batch)
        def _():
            n = bkt.shape[0]
            for hl in range(2):
                head = gi * DIL_HEADS_PER_GROUP + hp * 2 + hl
                bias = jnp.full(bkt.shape, NEG_INF, F32)
                for bk in range(N_REL_BUCKETS):
                    bias = jnp.where(bkt == bk, tab_ref[bk, head], bias)
                dst[hl * n:(hl + 1) * n, :] = bias
        return dst[...]

    def stack_heads(qb):
        return jnp.concatenate([jnp.where(sel0, qb, 0.0), jnp.where(sel0, 0.0, qb)], axis=0).astype(BF16)

    def with_ones(vv):
        return jnp.concatenate([vv, jnp.ones_like(vv)], axis=1).astype(BF16)

    def first_group(gi, dil):
        n_sub = s_len // dil
        bias2 = head_biases(gi, bkt0_ref[...], bias0_sc.at[hp])

        def residue(r):
            rows = pl.ds(r, n_sub, stride=dil)
            logits = _dot_nt(stack_heads(q_ref[0, rows, :] * (DIL_DH ** -0.5)), k_ref[0, rows, :].astype(BF16)) + bias2
            return rows, logits

        def body(tt, carry):
            s1 = [residue(tt * DIL_UNROLL + u) for u in range(DIL_UNROLL)]
            s2 = [jnp.max(lg, axis=1, keepdims=True) for _, lg in s1]
            s3 = [_dot(jnp.exp(lg - mc).astype(BF16), with_ones(v_ref[0, rows, :])) for (rows, lg), mc in zip(s1, s2)]
            for (rows, _), mc, pv in zip(s1, s2, s3):
                m_sc[rows, :] = jnp.where(sel0, mc[0:n_sub], mc[n_sub:])
                l_sc[rows, :] = jnp.where(sel0, pv[0:n_sub, LANES:], pv[n_sub:, LANES:])
                acc_sc[rows, :] = jnp.where(sel0, pv[0:n_sub, :LANES], pv[n_sub:, :LANES])
            return carry

        lax.fori_loop(0, dil // DIL_UNROLL, body, 0)

    def group(gi, dil):
        nb = s_len // (blk * dil)
        bias2 = head_biases(gi, bkt_ref[gi], bias_sc.at[hp, gi])

        def stage_logits(t):
            r = t // nb
            n = t % nb
            q_start = r + dil * blk * n
            p_start = r + dil * blk * jnp.maximum(n - 1, 0)
            rows_q = pl.ds(q_start, blk, stride=dil) if dil > 1 else pl.ds(q_start, blk)
            rows_p = pl.ds(p_start, blk, stride=dil) if dil > 1 else pl.ds(p_start, blk)
            q2 = stack_heads(q_ref[0, rows_q, :] * (DIL_DH ** -0.5))
            kk = jnp.concatenate([k_ref[0, rows_p, :], k_ref[0, rows_q, :]], axis=0).astype(BF16)
            logits = _dot_nt(q2, kk) + bias2
            logits = jnp.where(jnp.logical_and(n == 0, kcol < blk), NEG_INF, logits)
            return rows_q, rows_p, logits

        def stage_pv(rows_q, rows_p, logits, m_col):
            p = jnp.exp(logits - m_col)
            vv = with_ones(jnp.concatenate([v_ref[0, rows_p, :], v_ref[0, rows_q, :]], axis=0))
            pv = _dot(p.astype(BF16), vv)
            return pv[:, LANES:], pv[:, :LANES]

        def body(tt, carry):
            ts = [tt * DIL_UNROLL + u for u in range(DIL_UNROLL)]
            s1 = [stage_logits(t) for t in ts]
            s2 = [jnp.max(lg, axis=1, keepdims=True) for _, _, lg in s1]
            s3 = [stage_pv(rq, rp, lg, mc) for (rq, rp, lg), mc in zip(s1, s2)]
            outs = []
            for (rows_q, _, _), m_col, (ps, pv) in zip(s1, s2, s3):
                m_old = m_sc[rows_q, :]
                m_blk = jnp.where(sel0, m_col[0:blk], m_col[blk:])
                m_new = jnp.maximum(m_old, m_blk)
                a_old = jnp.exp(m_old - m_new)
                a_blk = jnp.exp(m_blk - m_new)
                l_new = a_old * l_sc[rows_q, :] + a_blk * jnp.where(sel0, ps[0:blk], ps[blk:])
                a_new = a_old * acc_sc[rows_q, :] + a_blk * jnp.where(sel0, pv[0:blk], pv[blk:])
                outs.append((rows_q, m_new, l_new, a_new))
            for rows_q, m_new, l_new, a_new in outs:
                m_sc[rows_q, :] = m_new
                l_sc[rows_q, :] = l_new
                acc_sc[rows_q, :] = a_new
            return carry

        for tt in range(s_len // (blk * DIL_UNROLL)):
            body(tt, 0)

    for gi, (_, dil) in enumerate(DIL_PAIRS):
        pl.when(g == ng - 1 - gi)(functools.partial(first_group if gi == ng - 1 else group, gi, dil))

    @pl.when(g == ng - 1)
    def _():
        o_ref[0] = (acc_sc[...] / l_sc[...]).astype(o_ref.dtype)


def _dilated(pd, rel_bias):
    bsz, s, _ = pd.shape
    ng = len(DIL_PAIRS)
    gw = DIL_HEADS_PER_GROUP * DIL_DH // LANES
    nq = DIL_W // LANES
    bkt = jnp.asarray(_dil_buckets())
    dil0 = DIL_PAIRS[-1][1]
    n_sub = s // dil0
    assert s % (DIL_BLK * dil0) == 0 and dil0 % DIL_UNROLL == 0
    delta = np.arange(n_sub)[:, None] - np.arange(n_sub)[None, :]
    bkt0 = jnp.asarray(np.where((delta >= 0) & (delta <= DIL_BLK), _t5_bucket(np.clip(delta, 0, None) * dil0), -1)
                       .astype(np.int32))
    blk_spec = lambda base: pl.BlockSpec((1, s, LANES), lambda b, p, g: (b, 0, base + (ng - 1 - g) * gw + p))
    return pl.pallas_call(
        _dil_kernel,
        out_shape=jax.ShapeDtypeStruct((bsz, s, DIL_OUT), BF16),
        grid=(bsz, gw, ng),
        in_specs=[
            blk_spec(0),
            blk_spec(nq),
            blk_spec(2 * nq),
            pl.BlockSpec((ng, DIL_BLK, 2 * DIL_BLK), lambda b, p, g: (0, 0, 0)),
            pl.BlockSpec((n_sub, n_sub), lambda b, p, g: (0, 0)),
            pl.BlockSpec(memory_space=pltpu.SMEM),
        ],
        out_specs=pl.BlockSpec((1, s, LANES), lambda b, p, g: (b, 0, p)),
        scratch_shapes=[pltpu.VMEM((s, LANES), F32)] * 3 + [
            pltpu.VMEM((gw, 2 * n_sub, n_sub), F32),
            pltpu.VMEM((gw, ng - 1, 2 * DIL_BLK, 2 * DIL_BLK), F32),
        ],
        compiler_params=_params("arbitrary", "arbitrary", "arbitrary"),
        name="dilated",
    )(pd, pd, pd, bkt, bkt0, rel_bias)


def _ab_weight(w):
    o = np.cumsum((0,) + (GLA_QK, GLA_QK, GLA_V, GLA_V, GLA_GATE_RANK, 2 * ML_W, ML_W, MLSTM_HEADS, MLSTM_HEADS, ML_W))
    qa, ka, va, ra, aa, qkb, vb, ib, fb, ob = [w[:, o[j]:o[j + 1]] for j in range(10)]
    pad = jnp.zeros((w.shape[0], LANES - 2 * MLSTM_HEADS - GLA_GATE_RANK), w.dtype)
    return jnp.concatenate([qa, ka, va, ra, qkb, vb, ob, ib, fb, aa, pad], axis=1).astype(BF16)


def kernel(x, c, ada_w, ada_b, ln_g, ln_b, ab_w_in, gla_wa_up, gla_ba, gla_norm_g, ml_conv_w, ml_conv_b,
           ml_b_i, ml_b_f, ml_norm_g, ab_w_out, cd_w_in, rel_bias, cd_w_out, ffn_w1, ffn_w3, ffn_w2):
    bsz, s, d = x.shape
    mod_all = _ada_mod(c, ada_w, ada_b).reshape(DEPTH, bsz, 6, d)
    tm = FFN_TM
    for layer in range(DEPTH):
        mod = mod_all[layer]
        j = layer // 2
        if layer % 2 == 0:
            (proj,) = _inproj(x, mod, _ab_weight(ab_w_in[j]), [(P0_N, F32)], tm, 0, 1, "inproj0")
            cat = _mixer0(proj, gla_wa_up[j], gla_ba[j], gla_norm_g[j], ml_conv_w[j], ml_conv_b[j],
                          ml_b_i[j], ml_b_f[j], ml_norm_g[j])
            acts, w_outs = [cat], [ab_w_out[j].astype(BF16)]
        else:
            pc, pd = _inproj(x, mod, cd_w_in[j].astype(BF16), [(3 * SB_W, BF16), (3 * DIL_W, F32)],
                             tm, 0, 1, "inproj1")
            oc = _stick_breaking(pc)
            od = _dilated(pd, rel_bias)
            w_out = cd_w_out[j].astype(BF16)
            acts, w_outs = [oc, od], [w_out[:SB_W], w_out[SB_W:]]
        x = _post(acts, w_outs, x, mod, ffn_w1[layer], ffn_w3[layer], ffn_w2[layer], ln_g[layer], ln_b[layer], FFN_TM)
    return x
```

```python
import functools
import itertools

import numpy as np
import jax
import jax.numpy as jnp
from jax import lax
from jax.experimental import pallas as pl
from jax.experimental.pallas import tpu as pltpu

F32 = jnp.float32
BF16 = jnp.bfloat16

D_MODEL = 1024
DEPTH = 2
GLA_HEADS = 4
GLA_DK = 64
GLA_DV = 128
GLA_GATE_RANK = 16
GLA_TAU = 16.0
GLA_CHUNK = 64
MLSTM_HEADS = 4
MLSTM_DH = 128
MLSTM_CONV = 4
SB_HEADS = 8
SB_DH = 64
DIL_PAIRS = ((128, 1), (512, 4), (2048, 16))
DIL_HEADS_PER_GROUP = 4
DIL_DH = 64
DIL_BLK = 128
DIL_UNROLL = 4
N_REL_BUCKETS = 32
REL_MAX_DIST = 2048
D_FF = ((8 * D_MODEL + 3 * 256 - 1) // (3 * 256)) * 256
LN_EPS = 1e-5
RES_ALPHA = (2 * DEPTH) ** 0.25
NEG_INF = -1e30

GLA_QK = GLA_HEADS * GLA_DK
GLA_V = GLA_HEADS * GLA_DV
ML_W = MLSTM_HEADS * MLSTM_DH
SB_W = SB_HEADS * SB_DH
DIL_W = len(DIL_PAIRS) * DIL_HEADS_PER_GROUP * DIL_DH
DIL_OUT = DIL_HEADS_PER_GROUP * DIL_DH

LANES = 128
VMEM_LIMIT = 56 * 1024 * 1024

P0_QA = 0
P0_KA = P0_QA + GLA_QK
P0_VA = P0_KA + GLA_QK
P0_RA = P0_VA + GLA_V
P0_QKB = P0_RA + GLA_V
P0_VB = P0_QKB + 2 * ML_W
P0_OB = P0_VB + ML_W
P0_G = P0_OB + ML_W
P0_N = P0_G + LANES
G_I = 0
G_F = MLSTM_HEADS
G_A = 2 * MLSTM_HEADS

MIX_TILE = 256
MIX_ROWS = 4
FF_CHUNK = 256
IN_TM = 512
FFN_TM = 1024
POST_SUB = 512
SB_TQ = 2048
SB_SUB = 128
SB_WIDE = 256
SB_ZERO_BITS = 160.0
SB_BOUND_SLACK = 1.01
LOG2E = 1.4426950408889634
_CHUNK_SHIFT = 6
_HEAD_SHIFT = 6
assert GLA_CHUNK == GLA_DK == 1 << _CHUNK_SHIFT and SB_DH == DIL_DH == 1 << _HEAD_SHIFT


def _dot(a, b):
    return jnp.dot(a, b, preferred_element_type=F32)


def _dot_nt(a, b):
    return lax.dot_general(a, b, (((1,), (1,)), ((), ())), preferred_element_type=F32)


def _dot_tn(a, b):
    return lax.dot_general(a, b, (((0,), (0,)), ((), ())), preferred_element_type=F32)


def _dot_split(t, x, terms):
    acc = None
    rem = x
    for i in range(terms):
        part = rem.astype(BF16)
        d = _dot(t, part)
        acc = d if acc is None else acc + d
        if i + 1 < terms:
            rem = rem - part.astype(F32)
    return acc


def _log_sigmoid(x):
    return jnp.minimum(x, 0.0) - jnp.log(1.0 + jnp.exp(-jnp.abs(x)))


def _silu(x):
    return x * jax.nn.sigmoid(x)


def _layer_norm(r, g, b):
    mu = jnp.mean(r, axis=-1, keepdims=True)
    d = r - mu
    var = jnp.mean(d * d, axis=-1, keepdims=True)
    return d * lax.rsqrt(var + LN_EPS) * g + b


def _head_norm(x, g):
    mu = jnp.mean(x, axis=-1, keepdims=True)
    d = x - mu
    var = jnp.mean(d * d, axis=-1, keepdims=True)
    return d * lax.rsqrt(var + LN_EPS) * g


def _iota(shape, dim):
    return lax.broadcasted_iota(jnp.int32, shape, dim)


def _params(*sem):
    return pltpu.CompilerParams(dimension_semantics=sem, vmem_limit_bytes=VMEM_LIMIT)


def _ada_kernel(c_ref, w_ref, b_ref, o_ref):
    ca = _silu(c_ref[...]).astype(BF16)
    o_ref[0] = _dot(ca, w_ref[0].astype(BF16)) + b_ref[0]


def _ada_mod(c, ada_w, ada_b):
    bsz, d = c.shape
    n = ada_w.shape[-1]
    tn = n // 4
    return pl.pallas_call(
        _ada_kernel,
        out_shape=jax.ShapeDtypeStruct((DEPTH, bsz, n), F32),
        grid=(DEPTH, n // tn),
        in_specs=[
            pl.BlockSpec((bsz, d), lambda l, j: (0, 0)),
            pl.BlockSpec((1, d, tn), lambda l, j: (l, 0, j)),
            pl.BlockSpec((1, 1, tn), lambda l, j: (l, 0, j)),
        ],
        out_specs=pl.BlockSpec((1, bsz, tn), lambda l, j: (l, 0, j)),
        compiler_params=_params("arbitrary", "arbitrary"),
        name="ada_mod",
    )(c, ada_w, ada_b.reshape(DEPTH, 1, n))


def _inproj_kernel(x_ref, mod_ref, w_ref, *o_refs, shift_row, scale_row):
    sh = mod_ref[0, shift_row:shift_row + 1, :]
    sc = mod_ref[0, scale_row:scale_row + 1, :]
    hm = (x_ref[0] * (1.0 + sc) + sh).astype(BF16)
    col = 0
    for o_ref in o_refs:
        n = o_ref.shape[-1]
        o_ref[0] = _dot(hm, w_ref[:, col:col + n]).astype(o_ref.dtype)
        col += n


def _inproj(x, mod, w, outs, tm, shift_row, scale_row, name):
    bsz, s, d = x.shape
    n = w.shape[1]
    assert sum(o[0] for o in outs) == n
    return pl.pallas_call(
        functools.partial(_inproj_kernel, shift_row=shift_row, scale_row=scale_row),
        out_shape=[jax.ShapeDtypeStruct((bsz, s, o[0]), o[1]) for o in outs],
        grid=(bsz, s // tm),
        in_specs=[
            pl.BlockSpec((1, tm, d), lambda b, i: (b, i, 0)),
            pl.BlockSpec((1, 6, d), lambda b, i: (b, 0, 0)),
            pl.BlockSpec((d, n), lambda b, i: (0, 0)),
        ],
        out_specs=[pl.BlockSpec((1, tm, o[0]), lambda b, i: (b, i, 0)) for o in outs],
        compiler_params=_params("parallel", "arbitrary"),
        name=name,
    )(x, mod, w)


def _post_kernel(*refs, n_act):
    act_refs = refs[:n_act]
    wo_refs = refs[n_act:2 * n_act]
    x_ref, mod_ref, w1_ref, w3_ref, w2_ref, g_ref, b_ref, o_ref, gm_ref = refs[2 * n_act:]

    def mix(t):
        rows = slice(t * POST_SUB, (t + 1) * POST_SUB)
        y = None
        for a_ref, w_ref in zip(act_refs, wo_refs):
            part = _dot(a_ref[0, rows, :], w_ref[...])
            y = part if y is None else y + part
        r = RES_ALPHA * x_ref[0, rows, :] + (1.0 + mod_ref[0, 2:3, :]) * y
        xm = _layer_norm(r, g_ref[0:1, :], b_ref[0:1, :])
        return xm, (xm * (1.0 + mod_ref[0, 4:5, :]) + mod_ref[0, 3:4, :]).astype(BF16)

    def up(t, hf):
        for c in range(D_FF // FF_CHUNK):
            cs = slice(c * FF_CHUNK, (c + 1) * FF_CHUNK)
            gm_ref[t, :, cs] = (_silu(_dot(hf, w1_ref[:, cs])) * _dot(hf, w3_ref[:, cs])).astype(BF16)

    def down(t, xm):
        rows = slice(t * POST_SUB, (t + 1) * POST_SUB)
        r2 = RES_ALPHA * xm + (1.0 + mod_ref[0, 5:6, :]) * _dot(gm_ref[t], w2_ref[...])
        o_ref[0, rows, :] = _layer_norm(r2, g_ref[1:2, :], b_ref[1:2, :])

    n = x_ref.shape[1] // POST_SUB
    mixed = {}
    for step in range(n + 2):
        if step >= 2:
            down(step - 2, mixed[step - 2][0])
        if 1 <= step <= n:
            up(step - 1, mixed[step - 1][1])
        if step < n:
            mixed[step] = mix(step)


def _post(acts, w_outs, x, mod, w1, w3, w2, ln_g, ln_b, tm):
    bsz, s, d = x.shape
    n_act = len(acts)
    const = lambda shape: pl.BlockSpec(shape, lambda b, i: (0,) * len(shape), pipeline_mode=pl.Buffered(1))
    in_specs = [pl.BlockSpec((1, tm, a.shape[-1]), lambda b, i: (b, i, 0)) for a in acts]
    in_specs += [const(w.shape) for w in w_outs]
    in_specs += [
        pl.BlockSpec((1, tm, d), lambda b, i: (b, i, 0)),
        pl.BlockSpec((1, 6, d), lambda b, i: (b, 0, 0)),
        const((d, D_FF)),
        const((d, D_FF)),
        const((D_FF, d)),
        const((2, d)),
        const((2, d)),
    ]
    return pl.pallas_call(
        functools.partial(_post_kernel, n_act=n_act),
        out_shape=jax.ShapeDtypeStruct((bsz, s, d), F32),
        grid=(bsz, s // tm),
        in_specs=in_specs,
        out_specs=pl.BlockSpec((1, tm, d), lambda b, i: (b, i, 0)),
        scratch_shapes=[pltpu.VMEM((tm // POST_SUB, POST_SUB, D_FF), BF16)],
        compiler_params=_params("parallel", "arbitrary"),
        name="post",
    )(*acts, *w_outs, x, mod, w1.astype(BF16), w3.astype(BF16), w2.astype(BF16), ln_g, ln_b)


def _mixer0_kernel(p_ref, wa_ref, ba_ref, gg_ref, cw_ref, cb_ref, gb_ref, mg_ref, o_ref,
                   st_ref, cst_ref, m_ref, xc_ref):
    @pl.when(pl.program_id(1) == 0)
    def _():
        st_ref[...] = jnp.zeros_like(st_ref)
        cst_ref[...] = jnp.zeros_like(cst_ref)
        m_ref[...] = jnp.zeros_like(m_ref)
        xc_ref[:, 0:8, :] = jnp.zeros((xc_ref.shape[0], 8, 2 * ML_W), F32)

    for b in range(p_ref.shape[0]):
        _mixer0_row(p_ref.at[pl.ds(b, 1)], wa_ref, ba_ref, gg_ref, cw_ref, cb_ref, gb_ref, mg_ref,
                    o_ref.at[pl.ds(b, 1)], st_ref.at[b], cst_ref.at[b], m_ref.at[b], xc_ref.at[b])


def _mixer0_row(p_ref, wa_ref, ba_ref, gg_ref, cw_ref, cb_ref, gb_ref, mg_ref, o_ref, st_ref, cst_ref, m_ref, xc_ref):
    L = MIX_TILE
    lane = _iota((1, LANES), 1)
    g_raw = p_ref[0, :, P0_G:P0_G + LANES]

    u = _dot(g_raw.astype(BF16), wa_ref[...]) + ba_ref[...]
    la = _log_sigmoid(u) * (1.0 / GLA_TAU)
    row = _iota((L, L), 0)
    col = _iota((L, L), 1)
    same_chunk = (row >> _CHUNK_SHIFT) == (col >> _CHUNK_SHIFT)
    tri_blk = jnp.where(same_chunk & (col <= row), 1.0, 0.0).astype(BF16)
    bcs = _dot_split(tri_blk, la, 3)
    q_in = p_ref[0, :, P0_QA:P0_QA + GLA_QK] * (GLA_DK ** -0.5) * jnp.exp(bcs)
    k_raw = p_ref[0, :, P0_KA:P0_KA + GLA_QK]
    k_in = k_raw * jnp.exp(-bcs)
    lane_qk = _iota((1, GLA_QK), 1)
    head_masks = [(lane_qk >> _CHUNK_SHIFT) == h for h in range(GLA_HEADS)]
    r4 = _iota((GLA_HEADS * GLA_CHUNK, GLA_CHUNK), 0)
    c4 = _iota((GLA_HEADS * GLA_CHUNK, GLA_CHUNK), 1)
    tril4 = c4 <= (r4 & (GLA_CHUNK - 1))
    oa_chunks = []
    for c in range(L // GLA_CHUNK):
        r0, r1 = c * GLA_CHUNK, (c + 1) * GLA_CHUNK
        b_c = bcs[r0:r1]
        bl = b_c[GLA_CHUNK - 1:GLA_CHUNK, :]
        q_c = q_in[r0:r1]
        k_c = k_in[r0:r1].astype(BF16)
        k_end = k_raw[r0:r1] * jnp.exp(bl - b_c)
        v_c = p_ref[0, r0:r1, P0_VA:P0_VA + GLA_V]
        q_exp = jnp.concatenate([jnp.where(hm, q_c, 0.0) for hm in head_masks], axis=0).astype(BF16)
        k_exp = jnp.concatenate([jnp.where(hm, k_end, 0.0) for hm in head_masks], axis=0).astype(BF16)
        sc = jnp.where(tril4, _dot_nt(q_exp, k_c), 0.0)
        intra = _dot(sc.astype(BF16), v_c.astype(BF16))
        inter = _dot_nt(q_exp, st_ref[...].astype(BF16))
        o_heads = []
        for h in range(GLA_HEADS):
            h0, h1 = h * GLA_CHUNK, (h + 1) * GLA_CHUNK
            o_heads.append(intra[h0:h1, h * GLA_DV:(h + 1) * GLA_DV] + inter[h0:h1])
        oa_chunks.append(o_heads)
        v_cat = jnp.concatenate([v_c[:, h * GLA_DV:(h + 1) * GLA_DV] for h in range(GLA_HEADS)], axis=0)
        st_ref[...] = st_ref[...] * jnp.exp(bl) + _dot_tn(v_cat.astype(BF16), k_exp)
    for h in range(GLA_HEADS):
        o_h = jnp.concatenate([oc[h] for oc in oa_chunks], axis=0)
        cs = slice(h * GLA_DV, (h + 1) * GLA_DV)
        ra = p_ref[0, :, P0_RA + h * GLA_DV:P0_RA + (h + 1) * GLA_DV]
        o_ref[0, :, cs] = (_head_norm(o_h, gg_ref[:, cs]) * _silu(ra)).astype(o_ref.dtype)

    xc_ref[8:8 + L, :] = p_ref[0, :, P0_QKB:P0_QKB + 2 * ML_W]
    conv = cb_ref[...]
    for kk in range(MLSTM_CONV):
        conv = conv + cw_ref[kk:kk + 1, :] * xc_ref[8 - (MLSTM_CONV - 1) + kk:8 - (MLSTM_CONV - 1) + kk + L, :]
    xc_ref[0:8, :] = xc_ref[L:L + 8, :]
    qk_b = _silu(conv)

    gb = jnp.where(lane < G_A, g_raw + gb_ref[...], 0.0)
    ipre = gb
    logf = _log_sigmoid(pltpu.roll(gb, LANES - G_F, axis=1))
    logf = jnp.where(lane < MLSTM_HEADS, logf, 0.0)
    tri = jnp.where(col <= row, 1.0, 0.0).astype(BF16)
    bcum = _dot_split(tri, logf, 3)
    blast = bcum[L - 1:L, :]
    wend = blast - bcum + ipre
    m_prev = m_ref[...]
    m_new = jnp.maximum(blast + m_prev, jnp.max(wend, axis=0, keepdims=True))
    scl = jnp.exp(blast + m_prev - m_new)
    wj = jnp.exp(wend - m_new)
    rows_src = jnp.where(lane < MLSTM_HEADS, ipre, pltpu.roll(bcum, MLSTM_HEADS, axis=1))
    rows_t = rows_src.T
    causal = col <= row
    ones = jnp.ones((L, MLSTM_DH), F32)
    for h in range(MLSTM_HEADS):
        hs = slice(h * MLSTM_DH, (h + 1) * MLSTM_DH)
        wide = lambda t: jnp.concatenate([t, t], axis=1)
        b_rep = jnp.broadcast_to(bcum[:, h:h + 1], (L, LANES))
        wj_rep = jnp.broadcast_to(wj[:, h:h + 1], (L, LANES))
        ip_row = rows_t[h:h + 1, :]
        b_row = rows_t[MLSTM_HEADS + h:MLSTM_HEADS + h + 1, :]
        dlog = jnp.where(causal, wide(b_rep) - b_row + ip_row, NEG_INF)
        il = b_rep + m_prev[:, h:h + 1]
        m_i = jnp.maximum(il, jnp.broadcast_to(jnp.max(dlog, axis=1, keepdims=True), (L, LANES)))
        w_intra = jnp.exp(dlog - wide(m_i))
        s_inter = jnp.exp(il - m_i)
        q_h = qk_b[:, h * MLSTM_DH:(h + 1) * MLSTM_DH].astype(BF16)
        k_h = (qk_b[:, ML_W + h * MLSTM_DH:ML_W + (h + 1) * MLSTM_DH] * (MLSTM_DH ** -0.5)).astype(BF16)
        v_h = p_ref[0, :, P0_VB + h * MLSTM_DH:P0_VB + (h + 1) * MLSTM_DH]
        v_aug = jnp.concatenate([v_h, ones], axis=1)
        a = (w_intra * _dot_nt(q_h, k_h)).astype(BF16)
        c_prev = cst_ref[h]
        tot = _dot(a, v_aug.astype(BF16)) + wide(s_inter) * _dot(q_h, c_prev.astype(BF16))
        num = tot[:, :MLSTM_DH]
        den = tot[:, MLSTM_DH:]
        hid = num / jnp.maximum(jnp.abs(den), jnp.exp(-m_i))
        w_aug = (v_aug * wide(wj_rep)).astype(BF16)
        cst_ref[h] = scl[:, h:h + 1] * c_prev + _dot_tn(k_h, w_aug)
        ob = p_ref[0, :, P0_OB + h * MLSTM_DH:P0_OB + (h + 1) * MLSTM_DH]
        o_ref[0, :, GLA_V + h * MLSTM_DH:GLA_V + (h + 1) * MLSTM_DH] = (
            jax.nn.sigmoid(ob) * _head_norm(hid, mg_ref[:, hs])).astype(o_ref.dtype)
    m_ref[...] = m_new


def _mixer0(proj, wa_up, ba, gla_g, conv_w, conv_b, b_i, b_f, ml_g):
    bsz, s, n = proj.shape
    L = MIX_TILE
    nr = MIX_ROWS if bsz % MIX_ROWS == 0 else 1
    wa_pad = jnp.zeros((LANES, GLA_QK), F32).at[G_A:G_A + GLA_GATE_RANK].set(wa_up).astype(BF16)
    gbias = jnp.zeros((1, LANES), F32).at[0, G_I:G_I + MLSTM_HEADS].set(b_i).at[0, G_F:G_F + MLSTM_HEADS].set(b_f)
    full = lambda shape: pl.BlockSpec(shape, lambda b, i: (0,) * len(shape))
    return pl.pallas_call(
        _mixer0_kernel,
        out_shape=jax.ShapeDtypeStruct((bsz, s, GLA_V + ML_W), BF16),
        grid=(bsz // nr, s // L),
        in_specs=[
            pl.BlockSpec((nr, L, n), lambda b, i: (b, i, 0)),
            full((LANES, GLA_QK)),
            full((1, GLA_QK)),
            full((1, GLA_V)),
            full((MLSTM_CONV, 2 * ML_W)),
            full((1, 2 * ML_W)),
            full((1, LANES)),
            full((1, ML_W)),
        ],
        out_specs=pl.BlockSpec((nr, L, GLA_V + ML_W), lambda b, i: (b, i, 0)),
        scratch_shapes=[
            pltpu.VMEM((nr, GLA_DV, GLA_QK), F32),
            pltpu.VMEM((nr, MLSTM_HEADS, MLSTM_DH, 2 * MLSTM_DH), F32),
            pltpu.VMEM((nr, 1, LANES), F32),
            pltpu.VMEM((nr, 8 + L, 2 * ML_W), F32),
        ],
        compiler_params=_params("parallel", "arbitrary"),
        name="mixer0",
    )(proj, wa_pad, ba.reshape(1, GLA_QK), gla_g.reshape(1, GLA_V), conv_w, conv_b.reshape(1, 2 * ML_W),
      gbias, ml_g.reshape(1, ML_W))


def _sb_kernel(q_ref, k_ref, v_ref, o_ref, qs_ref, acc_ref, car_ref, mar_ref, kn_ref):
    TQ, SUB, WIDE = SB_TQ, SB_SUB, SB_WIDE
    ns = TQ // SUB
    RB = 2 * SUB
    i = pl.program_id(2)
    lane = _iota((1, LANES), 1)
    head0 = (lane >> _HEAD_SHIFT) == 0

    @pl.when(i == 0)
    def _():
        k2 = jnp.square(k_ref[0].astype(F32))
        for h in range(2):
            n2 = jnp.sum(jnp.where(head0 if h == 0 else ~head0, k2, 0.0), axis=1, keepdims=True)
            kn_ref[h:h + 1, :] = jnp.broadcast_to(jnp.sqrt(jnp.max(n2, axis=0, keepdims=True)), (1, LANES))

    for a in range(ns):
        qa = q_ref[0, a * SUB:(a + 1) * SUB, :].astype(F32) * (SB_DH ** -0.5 * LOG2E)
        for h in range(2):
            rows = slice(a * RB + h * SUB, a * RB + (h + 1) * SUB)
            qh = jnp.where(head0 if h == 0 else ~head0, qa, 0.0).astype(BF16)
            qs_ref[rows, :] = qh
            qn = jnp.sqrt(jnp.sum(jnp.square(qh.astype(F32)), axis=1, keepdims=True))
            mar_ref[rows, :] = qn * kn_ref[h:h + 1, :] * SB_BOUND_SLACK + SB_ZERO_BITS
    tri = jnp.where(_iota((WIDE, WIDE), 0) >= _iota((WIDE, WIDE), 1), -1.0, 0.0).astype(BF16)

    def key_rows(ku, nk):
        r0 = ku * SUB
        return pl.ds(r0 if isinstance(r0, int) else pl.multiple_of(r0, SUB), nk)

    def logits(a, ku, nk):
        return _dot_nt(qs_ref[a * RB:(a + 1) * RB, :], k_ref[0, key_rows(ku, nk), :])

    def softplus2(z, diagonal):
        sp = jnp.maximum(z, 0.0) + jnp.log(1.0 + jnp.exp2(-jnp.abs(z))) * LOG2E
        if not diagonal:
            return sp, None
        mask = _iota(z.shape, 1) < (_iota(z.shape, 0) & (SUB - 1))
        return jnp.where(mask, sp, 0.0), mask

    def suffix(sp):
        nk = sp.shape[1]
        return _dot(sp.astype(BF16), tri[:nk, :nk]), jnp.sum(sp, axis=1, keepdims=True)

    def weights(z, res, car, mask):
        att = jnp.exp2(z + res if car is None else z + res - car)
        if mask is not None:
            att = jnp.where(mask, att, 0.0)
        return att.astype(BF16)

    def far_tile(a, ku):
        rows = slice(a * RB, (a + 1) * RB)
        z = logits(a, ku, SUB)
        sp, _ = softplus2(z, False)
        res, tot = suffix(sp)
        car = car_ref[rows, :]
        acc_ref[rows, :] += _dot(weights(z, res, car, None), v_ref[0, key_rows(ku, SUB), :])
        car_ref[rows, :] = car + tot

    def near_tiles(first_block):
        chains = []
        for a in range(ns):
            g = a if first_block else ns * i + a
            chain = [(a, g, SUB, True)]
            if not first_block or a >= 2:
                chain.append((a, g - 2, WIDE, False))
            elif a == 1:
                chain.append((a, 0, SUB, False))
            chains.append(chain)
        jobs = [job for step in itertools.zip_longest(*chains) for job in step if job is not None]
        zs = [logits(a, ku, nk) for a, ku, nk, _ in jobs]
        sps = [softplus2(z, diagonal) for z, (_, _, _, diagonal) in zip(zs, jobs)]
        sufs = [suffix(sp) for sp, _ in sps]
        car = [None] * ns
        atts = []
        for z, (sp, mask), (res, tot), (a, _, _, _) in zip(zs, sps, sufs, jobs):
            atts.append(weights(z, res, car[a], mask))
            car[a] = tot if car[a] is None else car[a] + tot
        acc = [None] * ns
        for att, (a, ku, nk, _) in zip(atts, jobs):
            t = _dot(att, v_ref[0, key_rows(ku, nk), :])
            acc[a] = t if acc[a] is None else acc[a] + t
        for a in range(ns):
            acc_ref[a * RB:(a + 1) * RB, :] = acc[a]
            car_ref[a * RB:(a + 1) * RB, :] = jnp.broadcast_to(car[a], (RB, LANES))

    pl.when(i == 0)(functools.partial(near_tiles, True))
    pl.when(i > 0)(functools.partial(near_tiles, False))

    def next_unit(a, t):
        return ns * i + a - 3 - t

    def pending(t):
        gap = [jnp.where(next_unit(a, t) >= 0, mar_ref[a * RB:(a + 1) * RB, :] - car_ref[a * RB:(a + 1) * RB, :], -1.0)
               for a in range(ns)]
        return jnp.max(functools.reduce(jnp.maximum, gap)) > 0.0

    def far_body(c):
        t, _ = c
        for a in range(ns):
            ku = next_unit(a, t)
            pl.when(ku >= 0)(functools.partial(far_tile, a, ku))
        return t + 1, pending(t + 1)

    lax.while_loop(lambda c: c[1], far_body, (0, pending(0)))

    for a in range(ns):
        o_ref[0, a * SUB:(a + 1) * SUB, :] = jnp.where(
            head0, acc_ref[a * RB:a * RB + SUB, :], acc_ref[a * RB + SUB:(a + 1) * RB, :]).astype(o_ref.dtype)


def _stick_breaking(pc):
    bsz, s, _ = pc.shape
    npair = SB_W // LANES
    return pl.pallas_call(
        _sb_kernel,
        out_shape=jax.ShapeDtypeStruct((bsz, s, SB_W), BF16),
        grid=(bsz, npair, s // SB_TQ),
        in_specs=[
            pl.BlockSpec((1, SB_TQ, LANES), lambda b, p, i: (b, i, p)),
            pl.BlockSpec((1, s, LANES), lambda b, p, i: (b, 0, npair + p)),
            pl.BlockSpec((1, s, LANES), lambda b, p, i: (b, 0, 2 * npair + p)),
        ],
        out_specs=pl.BlockSpec((1, SB_TQ, LANES), lambda b, p, i: (b, i, p)),
        scratch_shapes=[
            pltpu.VMEM((2 * SB_TQ, LANES), BF16),
            pltpu.VMEM((2 * SB_TQ, LANES), F32),
            pltpu.VMEM((2 * SB_TQ, LANES), F32),
            pltpu.VMEM((2 * SB_TQ, LANES), F32),
            pltpu.VMEM((8, LANES), F32),
        ],
        compiler_params=_params("parallel", "parallel", "arbitrary"),
        name="stick_breaking",
    )(pc, pc, pc)


def _t5_bucket(dist):
    max_exact = N_REL_BUCKETS // 2
    dd = np.maximum(dist, 1).astype(np.float64)
    large = max_exact + (np.log(dd / max_exact) / np.log(REL_MAX_DIST / max_exact)
                         * (N_REL_BUCKETS - max_exact)).astype(np.int32)
    large = np.minimum(large, N_REL_BUCKETS - 1)
    return np.where(dist < max_exact, dist, large).astype(np.int32)


def _dil_buckets():
    qi = np.arange(DIL_BLK)[:, None]
    kj = np.arange(2 * DIL_BLK)[None, :]
    delta = qi - kj + DIL_BLK
    in_win = (delta >= 0) & (delta <= DIL_BLK)
    tabs = []
    for window, dil in DIL_PAIRS:
        assert window // dil == DIL_BLK
        bucket = _t5_bucket(np.clip(delta, 0, None) * dil)
        tabs.append(np.where(in_win, bucket, -1).astype(np.int32))
    return np.stack(tabs, 0)


def _dil_kernel(q_ref, k_ref, v_ref, bkt_ref, bkt0_ref, tab_ref, o_ref, m_sc, l_sc, acc_sc, bias0_sc, bias_sc):
    s_len = q_ref.shape[1]
    first_batch = pl.program_id(0) == 0
    hp = pl.program_id(1)
    g = pl.program_id(2)
    blk = DIL_BLK
    ng = len(DIL_PAIRS)
    lane = _iota((1, LANES), 1)
    lane_head = lane >> _HEAD_SHIFT
    sel0 = lane_head == 0
    kcol = _iota((2 * blk, 2 * blk), 1)

    def head_biases(gi, bkt, dst):
        @pl.when(first_batch)
        def _():
            n = bkt.shape[0]
            for hl in range(2):
                head = gi * DIL_HEADS_PER_GROUP + hp * 2 + hl
                bias = jnp.full(bkt.shape, NEG_INF, F32)
                for bk in range(N_REL_BUCKETS):
                    bias = jnp.where(bkt == bk, tab_ref[bk, head], bias)
                dst[hl * n:(hl + 1) * n, :] = bias
        return dst[...]

    def stack_heads(qb):
        return jnp.concatenate([jnp.where(sel0, qb, 0.0), jnp.where(sel0, 0.0, qb)], axis=0).astype(BF16)

    def with_ones(vv):
        return jnp.concatenate([vv, jnp.ones_like(vv)], axis=1).astype(BF16)

    def first_group(gi, dil):
        n_sub = s_len // dil
        bias2 = head_biases(gi, bkt0_ref[...], bias0_sc.at[hp])

        def residue(r):
            rows = pl.ds(r, n_sub, stride=dil)
            logits = _dot_nt(stack_heads(q_ref[0, rows, :] * (DIL_DH ** -0.5)), k_ref[0, rows, :].astype(BF16)) + bias2
            return rows, logits

        def body(tt, carry):
            s1 = [residue(tt * DIL_UNROLL + u) for u in range(DIL_UNROLL)]
            s2 = [jnp.max(lg, axis=1, keepdims=True) for _, lg in s1]
            s3 = [_dot(jnp.exp(lg - mc).astype(BF16), with_ones(v_ref[0, rows, :])) for (rows, lg), mc in zip(s1, s2)]
            for (rows, _), mc, pv in zip(s1, s2, s3):
                m_sc[rows, :] = jnp.where(sel0, mc[0:n_sub], mc[n_sub:])
                l_sc[rows, :] = jnp.where(sel0, pv[0:n_sub, LANES:], pv[n_sub:, LANES:])
                acc_sc[rows, :] = jnp.where(sel0, pv[0:n_sub, :LANES], pv[n_sub:, :LANES])
            return carry

        lax.fori_loop(0, dil // DIL_UNROLL, body, 0)

    def group(gi, dil):
        nb = s_len // (blk * dil)
        bias2 = head_biases(gi, bkt_ref[gi], bias_sc.at[hp, gi])

        def stage_logits(t):
            r = t // nb
            n = t % nb
            q_start = r + dil * blk * n
            p_start = r + dil * blk * jnp.maximum(n - 1, 0)
            rows_q = pl.ds(q_start, blk, stride=dil) if dil > 1 else pl.ds(q_start, blk)
            rows_p = pl.ds(p_start, blk, stride=dil) if dil > 1 else pl.ds(p_start, blk)
            q2 = stack_heads(q_ref[0, rows_q, :] * (DIL_DH ** -0.5))
            kk = jnp.concatenate([k_ref[0, rows_p, :], k_ref[0, rows_q, :]], axis=0).astype(BF16)
            logits = _dot_nt(q2, kk) + bias2
            logits = jnp.where(jnp.logical_and(n == 0, kcol < blk), NEG_INF, logits)
            return rows_q, rows_p, logits

        def stage_pv(rows_q, rows_p, logits, m_col):
            p = jnp.exp(logits - m_col)
            vv = with_ones(jnp.concatenate([v_ref[0, rows_p, :], v_ref[0, rows_q, :]], axis=0))
            pv = _dot(p.astype(BF16), vv)
            return pv[:, LANES:], pv[:, :LANES]

        def body(tt, carry):
            ts = [tt * DIL_UNROLL + u for u in range(DIL_UNROLL)]
            s1 = [stage_logits(t) for t in ts]
            s2 = [jnp.max(lg, axis=1, keepdims=True) for _, _, lg in s1]
            s3 = [stage_pv(rq, rp, lg, mc) for (rq, rp, lg), mc in zip(s1, s2)]
            outs = []
            for (rows_q, _, _), m_col, (ps, pv) in zip(s1, s2, s3):
                m_old = m_sc[rows_q, :]
                m_blk = jnp.where(sel0, m_col[0:blk], m_col[blk:])
                m_new = jnp.maximum(m_old, m_blk)
                a_old = jnp.exp(m_old - m_new)
                a_blk = jnp.exp(m_blk - m_new)
                l_new = a_old * l_sc[rows_q, :] + a_blk * jnp.where(sel0, ps[0:blk], ps[blk:])
                a_new = a_old * acc_sc[rows_q, :] + a_blk * jnp.where(sel0, pv[0:blk], pv[blk:])
                outs.append((rows_q, m_new, l_new, a_new))
            for rows_q, m_new, l_new, a_new in outs:
                m_sc[rows_q, :] = m_new
                l_sc[rows_q, :] = l_new
                acc_sc[rows_q, :] = a_new
            return carry

        for tt in range(s_len // (blk * DIL_UNROLL)):
            body(tt, 0)

    for gi, (_, dil) in enumerate(DIL_PAIRS):
        pl.when(g == ng - 1 - gi)(functools.partial(first_group if gi == ng - 1 else group, gi, dil))

    @pl.when(g == ng - 1)
    def _():
        o_ref[0] = (acc_sc[...] / l_sc[...]).astype(o_ref.dtype)


def _dilated(pd, rel_bias):
    bsz, s, _ = pd.shape
    ng = len(DIL_PAIRS)
    gw = DIL_HEADS_PER_GROUP * DIL_DH // LANES
    nq = DIL_W // LANES
    bkt = jnp.asarray(_dil_buckets())
    dil0 = DIL_PAIRS[-1][1]
    n_sub = s // dil0
    assert s % (DIL_BLK * dil0) == 0 and dil0 % DIL_UNROLL == 0
    delta = np.arange(n_sub)[:, None] - np.arange(n_sub)[None, :]
    bkt0 = jnp.asarray(np.where((delta >= 0) & (delta <= DIL_BLK), _t5_bucket(np.clip(delta, 0, None) * dil0), -1)
                       .astype(np.int32))
    blk_spec = lambda base: pl.BlockSpec((1, s, LANES), lambda b, p, g: (b, 0, base + (ng - 1 - g) * gw + p))
    return pl.pallas_call(
        _dil_kernel,
        out_shape=jax.ShapeDtypeStruct((bsz, s, DIL_OUT), BF16),
        grid=(bsz, gw, ng),
        in_specs=[
            blk_spec(0),
            blk_spec(nq),
            blk_spec(2 * nq),
            pl.BlockSpec((ng, DIL_BLK, 2 * DIL_BLK), lambda b, p, g: (0, 0, 0)),
            pl.BlockSpec((n_sub, n_sub), lambda b, p, g: (0, 0)),
            pl.BlockSpec(memory_space=pltpu.SMEM),
        ],
        out_specs=pl.BlockSpec((1, s, LANES), lambda b, p, g: (b, 0, p)),
        scratch_shapes=[pltpu.VMEM((s, LANES), F32)] * 3 + [
            pltpu.VMEM((gw, 2 * n_sub, n_sub), F32),
            pltpu.VMEM((gw, ng - 1, 2 * DIL_BLK, 2 * DIL_BLK), F32),
        ],
        compiler_params=_params("arbitrary", "arbitrary", "arbitrary"),
        name="dilated",
    )(pd, pd, pd, bkt, bkt0, rel_bias)


def _ab_weight(w):
    o = np.cumsum((0,) + (GLA_QK, GLA_QK, GLA_V, GLA_V, GLA_GATE_RANK, 2 * ML_W, ML_W, MLSTM_HEADS, MLSTM_HEADS, ML_W))
    qa, ka, va, ra, aa, qkb, vb, ib, fb, ob = [w[:, o[j]:o[j + 1]] for j in range(10)]
    pad = jnp.zeros((w.shape[0], LANES - 2 * MLSTM_HEADS - GLA_GATE_RANK), w.dtype)
    return jnp.concatenate([qa, ka, va, ra, qkb, vb, ob, ib, fb, aa, pad], axis=1).astype(BF16)


def kernel(x, c, ada_w, ada_b, ln_g, ln_b, ab_w_in, gla_wa_up, gla_ba, gla_norm_g, ml_conv_w, ml_conv_b,
           ml_b_i, ml_b_f, ml_norm_g, ab_w_out, cd_w_in, rel_bias, cd_w_out, ffn_w1, ffn_w3, ffn_w2):
    bsz, s, d = x.shape
    mod_all = _ada_mod(c, ada_w, ada_b).reshape(DEPTH, bsz, 6, d)
    tm = IN_TM
    for layer in range(DEPTH):
        mod = mod_all[layer]
        j = layer // 2
        if layer % 2 == 0:
            (proj,) = _inproj(x, mod, _ab_weight(ab_w_in[j]), [(P0_N, F32)], tm, 0, 1, "inproj0")
            cat = _mixer0(proj, gla_wa_up[j], gla_ba[j], gla_norm_g[j], ml_conv_w[j], ml_conv_b[j],
                          ml_b_i[j], ml_b_f[j], ml_norm_g[j])
            acts, w_outs = [cat], [ab_w_out[j].astype(BF16)]
        else:
            pc, pd = _inproj(x, mod, cd_w_in[j].astype(BF16), [(3 * SB_W, BF16), (3 * DIL_W, F32)],
                             tm, 0, 1, "inproj1")
            oc = _stick_breaking(pc)
            od = _dilated(pd, rel_bias)
            w_out = cd_w_out[j].astype(BF16)
            acts, w_outs = [oc, od], [w_out[:SB_W], w_out[SB_W:]]
        x = _post(acts, w_outs, x, mod, ffn_w1[layer], ffn_w3[layer], ffn_w2[layer], ln_g[layer], ln_b[layer], FFN_TM)
    return x
```

```python
import functools
import itertools

import numpy as np
import jax
import jax.numpy as jnp
from jax import lax
from jax.experimental import pallas as pl
from jax.experimental.pallas import tpu as pltpu

F32 = jnp.float32
BF16 = jnp.bfloat16

D_MODEL = 1024
DEPTH = 2
GLA_HEADS = 4
GLA_DK = 64
GLA_DV = 128
GLA_GATE_RANK = 16
GLA_TAU = 16.0
GLA_CHUNK = 64
MLSTM_HEADS = 4
MLSTM_DH = 128
MLSTM_CONV = 4
SB_HEADS = 8
SB_DH = 64
DIL_PAIRS = ((128, 1), (512, 4), (2048, 16))
DIL_HEADS_PER_GROUP = 4
DIL_DH = 64
DIL_BLK = 128
DIL_UNROLL = 4
N_REL_BUCKETS = 32
REL_MAX_DIST = 2048
D_FF = ((8 * D_MODEL + 3 * 256 - 1) // (3 * 256)) * 256
LN_EPS = 1e-5
RES_ALPHA = (2 * DEPTH) ** 0.25
NEG_INF = -1e30

GLA_QK = GLA_HEADS * GLA_DK
GLA_V = GLA_HEADS * GLA_DV
ML_W = MLSTM_HEADS * MLSTM_DH
SB_W = SB_HEADS * SB_DH
DIL_W = len(DIL_PAIRS) * DIL_HEADS_PER_GROUP * DIL_DH
DIL_OUT = DIL_HEADS_PER_GROUP * DIL_DH

LANES = 128
VMEM_LIMIT = 56 * 1024 * 1024

P0_QA = 0
P0_KA = P0_QA + GLA_QK
P0_VA = P0_KA + GLA_QK
P0_RA = P0_VA + GLA_V
P0_QKB = P0_RA + GLA_V
P0_VB = P0_QKB + 2 * ML_W
P0_OB = P0_VB + ML_W
P0_G = P0_OB + ML_W
P0_N = P0_G + LANES
G_I = 0
G_F = MLSTM_HEADS
G_A = 2 * MLSTM_HEADS

MIX_TILE = 256
MIX_ROWS = 4
FF_CHUNK = 256
IN_TM = 512
FFN_TM = 1024
POST_SUB = 512
SB_TQ = 2048
SB_SUB = 128
SB_WIDE = 256
SB_ZERO_BITS = 160.0
SB_BOUND_SLACK = 1.01
LOG2E = 1.4426950408889634
_CHUNK_SHIFT = 6
_HEAD_SHIFT = 6
assert GLA_CHUNK == GLA_DK == 1 << _CHUNK_SHIFT and SB_DH == DIL_DH == 1 << _HEAD_SHIFT


def _dot(a, b):
    return jnp.dot(a, b, preferred_element_type=F32)


def _dot_nt(a, b):
    return lax.dot_general(a, b, (((1,), (1,)), ((), ())), preferred_element_type=F32)


def _dot_tn(a, b):
    return lax.dot_general(a, b, (((0,), (0,)), ((), ())), preferred_element_type=F32)


def _dot_split(t, x, terms):
    acc = None
    rem = x
    for i in range(terms):
        part = rem.astype(BF16)
        d = _dot(t, part)
        acc = d if acc is None else acc + d
        if i + 1 < terms:
            rem = rem - part.astype(F32)
    return acc


def _log_sigmoid(x):
    return jnp.minimum(x, 0.0) - jnp.log(1.0 + jnp.exp(-jnp.abs(x)))


def _silu(x):
    return x * jax.nn.sigmoid(x)


def _layer_norm(r, g, b):
    mu = jnp.mean(r, axis=-1, keepdims=True)
    d = r - mu
    var = jnp.mean(d * d, axis=-1, keepdims=True)
    return d * lax.rsqrt(var + LN_EPS) * g + b


def _head_norm(x, g):
    mu = jnp.mean(x, axis=-1, keepdims=True)
    d = x - mu
    var = jnp.mean(d * d, axis=-1, keepdims=True)
    return d * lax.rsqrt(var + LN_EPS) * g


def _iota(shape, dim):
    return lax.broadcasted_iota(jnp.int32, shape, dim)


def _params(*sem):
    return pltpu.CompilerParams(dimension_semantics=sem, vmem_limit_bytes=VMEM_LIMIT)


def _ada_kernel(c_ref, w_ref, b_ref, o_ref):
    ca = _silu(c_ref[...]).astype(BF16)
    o_ref[0] = _dot(ca, w_ref[0].astype(BF16)) + b_ref[0]


def _ada_mod(c, ada_w, ada_b):
    bsz, d = c.shape
    n = ada_w.shape[-1]
    tn = n // 4
    return pl.pallas_call(
        _ada_kernel,
        out_shape=jax.ShapeDtypeStruct((DEPTH, bsz, n), F32),
        grid=(DEPTH, n // tn),
        in_specs=[
            pl.BlockSpec((bsz, d), lambda l, j: (0, 0)),
            pl.BlockSpec((1, d, tn), lambda l, j: (l, 0, j)),
            pl.BlockSpec((1, 1, tn), lambda l, j: (l, 0, j)),
        ],
        out_specs=pl.BlockSpec((1, bsz, tn), lambda l, j: (l, 0, j)),
        compiler_params=_params("arbitrary", "arbitrary"),
        name="ada_mod",
    )(c, ada_w, ada_b.reshape(DEPTH, 1, n))


def _inproj_kernel(x_ref, mod_ref, w_ref, *o_refs, shift_row, scale_row):
    sh = mod_ref[0, shift_row:shift_row + 1, :]
    sc = mod_ref[0, scale_row:scale_row + 1, :]
    hm = (x_ref[0] * (1.0 + sc) + sh).astype(BF16)
    col = 0
    for o_ref in o_refs:
        n = o_ref.shape[-1]
        o_ref[0] = _dot(hm, w_ref[:, col:col + n]).astype(o_ref.dtype)
        col += n


def _inproj(x, mod, w, outs, tm, shift_row, scale_row, name):
    bsz, s, d = x.shape
    n = w.shape[1]
    assert sum(o[0] for o in outs) == n
    return pl.pallas_call(
        functools.partial(_inproj_kernel, shift_row=shift_row, scale_row=scale_row),
        out_shape=[jax.ShapeDtypeStruct((bsz, s, o[0]), o[1]) for o in outs],
        grid=(bsz, s // tm),
        in_specs=[
            pl.BlockSpec((1, tm, d), lambda b, i: (b, i, 0)),
            pl.BlockSpec((1, 6, d), lambda b, i: (b, 0, 0)),
            pl.BlockSpec((d, n), lambda b, i: (0, 0)),
        ],
        out_specs=[pl.BlockSpec((1, tm, o[0]), lambda b, i: (b, i, 0)) for o in outs],
        compiler_params=_params("parallel", "arbitrary"),
        name=name,
    )(x, mod, w)


def _post_kernel(*refs, n_act):
    act_refs = refs[:n_act]
    wo_refs = refs[n_act:2 * n_act]
    x_ref, mod_ref, w1_ref, w3_ref, w2_ref, g_ref, b_ref, o_ref, gm_ref = refs[2 * n_act:]

    def mix(t):
        rows = slice(t * POST_SUB, (t + 1) * POST_SUB)
        y = None
        for a_ref, w_ref in zip(act_refs, wo_refs):
            part = _dot(a_ref[0, rows, :], w_ref[...])
            y = part if y is None else y + part
        r = RES_ALPHA * x_ref[0, rows, :] + (1.0 + mod_ref[0, 2:3, :]) * y
        xm = _layer_norm(r, g_ref[0:1, :], b_ref[0:1, :])
        return xm, (xm * (1.0 + mod_ref[0, 4:5, :]) + mod_ref[0, 3:4, :]).astype(BF16)

    def up(t, hf):
        for c in range(D_FF // FF_CHUNK):
            cs = slice(c * FF_CHUNK, (c + 1) * FF_CHUNK)
            gm_ref[t, :, cs] = (_silu(_dot(hf, w1_ref[:, cs])) * _dot(hf, w3_ref[:, cs])).astype(BF16)

    def down(t, xm):
        rows = slice(t * POST_SUB, (t + 1) * POST_SUB)
        r2 = RES_ALPHA * xm + (1.0 + mod_ref[0, 5:6, :]) * _dot(gm_ref[t], w2_ref[...])
        o_ref[0, rows, :] = _layer_norm(r2, g_ref[1:2, :], b_ref[1:2, :])

    n = x_ref.shape[1] // POST_SUB
    mixed = {}
    for step in range(n + 2):
        if step >= 2:
            down(step - 2, mixed[step - 2][0])
        if 1 <= step <= n:
            up(step - 1, mixed[step - 1][1])
        if step < n:
            mixed[step] = mix(step)


def _post(acts, w_outs, x, mod, w1, w3, w2, ln_g, ln_b, tm):
    bsz, s, d = x.shape
    n_act = len(acts)
    const = lambda shape: pl.BlockSpec(shape, lambda b, i: (0,) * len(shape), pipeline_mode=pl.Buffered(1))
    in_specs = [pl.BlockSpec((1, tm, a.shape[-1]), lambda b, i: (b, i, 0)) for a in acts]
    in_specs += [const(w.shape) for w in w_outs]
    in_specs += [
        pl.BlockSpec((1, tm, d), lambda b, i: (b, i, 0)),
        pl.BlockSpec((1, 6, d), lambda b, i: (b, 0, 0)),
        const((d, D_FF)),
        const((d, D_FF)),
        const((D_FF, d)),
        const((2, d)),
        const((2, d)),
    ]
    return pl.pallas_call(
        functools.partial(_post_kernel, n_act=n_act),
        out_shape=jax.ShapeDtypeStruct((bsz, s, d), F32),
        grid=(bsz, s // tm),
        in_specs=in_specs,
        out_specs=pl.BlockSpec((1, tm, d), lambda b, i: (b, i, 0)),
        scratch_shapes=[pltpu.VMEM((tm // POST_SUB, POST_SUB, D_FF), BF16)],
        compiler_params=_params("parallel", "arbitrary"),
        name="post",
    )(*acts, *w_outs, x, mod, w1.astype(BF16), w3.astype(BF16), w2.astype(BF16), ln_g, ln_b)


def _mixer0_kernel(p_ref, wa_ref, ba_ref, gg_ref, cw_ref, cb_ref, gb_ref, mg_ref, o_ref,
                   st_ref, cst_ref, m_ref, xc_ref):
    @pl.when(pl.program_id(1) == 0)
    def _():
        st_ref[...] = jnp.zeros_like(st_ref)
        cst_ref[...] = jnp.zeros_like(cst_ref)
        m_ref[...] = jnp.zeros_like(m_ref)
        xc_ref[:, 0:8, :] = jnp.zeros((xc_ref.shape[0], 8, 2 * ML_W), F32)

    L = MIX_TILE
    row = _iota((L, L), 0)
    col = _iota((L, L), 1)
    causal = col <= row
    same_chunk = (row >> _CHUNK_SHIFT) == (col >> _CHUNK_SHIFT)
    lane_qk = _iota((1, GLA_QK), 1)
    r4 = _iota((GLA_HEADS * GLA_CHUNK, GLA_CHUNK), 0)
    c4 = _iota((GLA_HEADS * GLA_CHUNK, GLA_CHUNK), 1)
    consts = dict(
        causal=causal,
        tri=jnp.where(causal, 1.0, 0.0).astype(BF16),
        tri_blk=jnp.where(same_chunk & causal, 1.0, 0.0).astype(BF16),
        head_masks=[(lane_qk >> _CHUNK_SHIFT) == h for h in range(GLA_HEADS)],
        tril4=c4 <= (r4 & (GLA_CHUNK - 1)),
    )
    for b in range(p_ref.shape[0]):
        _mixer0_row(consts, p_ref.at[pl.ds(b, 1)], wa_ref, ba_ref, gg_ref, cw_ref, cb_ref, gb_ref, mg_ref,
                    o_ref.at[pl.ds(b, 1)], st_ref.at[b], cst_ref.at[b], m_ref.at[b], xc_ref.at[b])


def _mixer0_row(consts, p_ref, wa_ref, ba_ref, gg_ref, cw_ref, cb_ref, gb_ref, mg_ref, o_ref,
                st_ref, cst_ref, m_ref, xc_ref):
    L = MIX_TILE
    causal, tri, tri_blk = consts["causal"], consts["tri"], consts["tri_blk"]
    head_masks, tril4 = consts["head_masks"], consts["tril4"]
    lane = _iota((1, LANES), 1)
    g_raw = p_ref[0, :, P0_G:P0_G + LANES]

    u = _dot(g_raw.astype(BF16), wa_ref[...]) + ba_ref[...]
    la = _log_sigmoid(u) * (1.0 / GLA_TAU)
    bcs = _dot_split(tri_blk, la, 3)
    q_in = p_ref[0, :, P0_QA:P0_QA + GLA_QK] * (GLA_DK ** -0.5) * jnp.exp(bcs)
    k_raw = p_ref[0, :, P0_KA:P0_KA + GLA_QK]
    k_in = k_raw * jnp.exp(-bcs)
    oa_chunks = []
    for c in range(L // GLA_CHUNK):
        r0, r1 = c * GLA_CHUNK, (c + 1) * GLA_CHUNK
        b_c = bcs[r0:r1]
        bl = b_c[GLA_CHUNK - 1:GLA_CHUNK, :]
        q_c = q_in[r0:r1]
        k_c = k_in[r0:r1].astype(BF16)
        k_end = k_raw[r0:r1] * jnp.exp(bl - b_c)
        v_c = p_ref[0, r0:r1, P0_VA:P0_VA + GLA_V]
        q_exp = jnp.concatenate([jnp.where(hm, q_c, 0.0) for hm in head_masks], axis=0).astype(BF16)
        k_exp = jnp.concatenate([jnp.where(hm, k_end, 0.0) for hm in head_masks], axis=0).astype(BF16)
        sc = jnp.where(tril4, _dot_nt(q_exp, k_c), 0.0)
        intra = _dot(sc.astype(BF16), v_c.astype(BF16))
        inter = _dot_nt(q_exp, st_ref[...].astype(BF16))
        o_heads = []
        for h in range(GLA_HEADS):
            h0, h1 = h * GLA_CHUNK, (h + 1) * GLA_CHUNK
            o_heads.append(intra[h0:h1, h * GLA_DV:(h + 1) * GLA_DV] + inter[h0:h1])
        oa_chunks.append(o_heads)
        v_cat = jnp.concatenate([v_c[:, h * GLA_DV:(h + 1) * GLA_DV] for h in range(GLA_HEADS)], axis=0)
        st_ref[...] = st_ref[...] * jnp.exp(bl) + _dot_tn(v_cat.astype(BF16), k_exp)
    for h in range(GLA_HEADS):
        o_h = jnp.concatenate([oc[h] for oc in oa_chunks], axis=0)
        cs = slice(h * GLA_DV, (h + 1) * GLA_DV)
        ra = p_ref[0, :, P0_RA + h * GLA_DV:P0_RA + (h + 1) * GLA_DV]
        o_ref[0, :, cs] = (_head_norm(o_h, gg_ref[:, cs]) * _silu(ra)).astype(o_ref.dtype)

    xc_ref[8:8 + L, :] = p_ref[0, :, P0_QKB:P0_QKB + 2 * ML_W]
    conv = cb_ref[...]
    for kk in range(MLSTM_CONV):
        conv = conv + cw_ref[kk:kk + 1, :] * xc_ref[8 - (MLSTM_CONV - 1) + kk:8 - (MLSTM_CONV - 1) + kk + L, :]
    xc_ref[0:8, :] = xc_ref[L:L + 8, :]
    qk_b = _silu(conv)

    gb = jnp.where(lane < G_A, g_raw + gb_ref[...], 0.0)
    ipre = gb
    logf = _log_sigmoid(pltpu.roll(gb, LANES - G_F, axis=1))
    logf = jnp.where(lane < MLSTM_HEADS, logf, 0.0)
    bcum = _dot_split(tri, logf, 3)
    blast = bcum[L - 1:L, :]
    wend = blast - bcum + ipre
    m_prev = m_ref[...]
    m_new = jnp.maximum(blast + m_prev, jnp.max(wend, axis=0, keepdims=True))
    scl = jnp.exp(blast + m_prev - m_new)
    wj = jnp.exp(wend - m_new)
    rows_src = jnp.where(lane < MLSTM_HEADS, ipre, pltpu.roll(bcum, MLSTM_HEADS, axis=1))
    rows_t = rows_src.T
    ones = jnp.ones((L, MLSTM_DH), F32)
    for h in range(MLSTM_HEADS):
        hs = slice(h * MLSTM_DH, (h + 1) * MLSTM_DH)
        wide = lambda t: jnp.concatenate([t, t], axis=1)
        b_rep = jnp.broadcast_to(bcum[:, h:h + 1], (L, LANES))
        wj_rep = jnp.broadcast_to(wj[:, h:h + 1], (L, LANES))
        ip_row = rows_t[h:h + 1, :]
        b_row = rows_t[MLSTM_HEADS + h:MLSTM_HEADS + h + 1, :]
        dlog = jnp.where(causal, wide(b_rep) - b_row + ip_row, NEG_INF)
        il = b_rep + m_prev[:, h:h + 1]
        m_i = jnp.maximum(il, jnp.broadcast_to(jnp.max(dlog, axis=1, keepdims=True), (L, LANES)))
        w_intra = jnp.exp(dlog - wide(m_i))
        s_inter = jnp.exp(il - m_i)
        q_h = qk_b[:, h * MLSTM_DH:(h + 1) * MLSTM_DH].astype(BF16)
        k_h = (qk_b[:, ML_W + h * MLSTM_DH:ML_W + (h + 1) * MLSTM_DH] * (MLSTM_DH ** -0.5)).astype(BF16)
        v_h = p_ref[0, :, P0_VB + h * MLSTM_DH:P0_VB + (h + 1) * MLSTM_DH]
        v_aug = jnp.concatenate([v_h, ones], axis=1)
        a = (w_intra * _dot_nt(q_h, k_h)).astype(BF16)
        c_prev = cst_ref[h]
        tot = _dot(a, v_aug.astype(BF16)) + wide(s_inter) * _dot(q_h, c_prev.astype(BF16))
        num = tot[:, :MLSTM_DH]
        den = tot[:, MLSTM_DH:]
        hid = num / jnp.maximum(jnp.abs(den), jnp.exp(-m_i))
        w_aug = (v_aug * wide(wj_rep)).astype(BF16)
        cst_ref[h] = scl[:, h:h + 1] * c_prev + _dot_tn(k_h, w_aug)
        ob = p_ref[0, :, P0_OB + h * MLSTM_DH:P0_OB + (h + 1) * MLSTM_DH]
        o_ref[0, :, GLA_V + h * MLSTM_DH:GLA_V + (h + 1) * MLSTM_DH] = (
            jax.nn.sigmoid(ob) * _head_norm(hid, mg_ref[:, hs])).astype(o_ref.dtype)
    m_ref[...] = m_new


def _mixer0(proj, wa_up, ba, gla_g, conv_w, conv_b, b_i, b_f, ml_g):
    bsz, s, n = proj.shape
    L = MIX_TILE
    nr = MIX_ROWS if bsz % MIX_ROWS == 0 else 1
    wa_pad = jnp.zeros((LANES, GLA_QK), F32).at[G_A:G_A + GLA_GATE_RANK].set(wa_up).astype(BF16)
    gbias = jnp.zeros((1, LANES), F32).at[0, G_I:G_I + MLSTM_HEADS].set(b_i).at[0, G_F:G_F + MLSTM_HEADS].set(b_f)
    full = lambda shape: pl.BlockSpec(shape, lambda b, i: (0,) * len(shape))
    return pl.pallas_call(
        _mixer0_kernel,
        out_shape=jax.ShapeDtypeStruct((bsz, s, GLA_V + ML_W), BF16),
        grid=(bsz // nr, s // L),
        in_specs=[
            pl.BlockSpec((nr, L, n), lambda b, i: (b, i, 0)),
            full((LANES, GLA_QK)),
            full((1, GLA_QK)),
            full((1, GLA_V)),
            full((MLSTM_CONV, 2 * ML_W)),
            full((1, 2 * ML_W)),
            full((1, LANES)),
            full((1, ML_W)),
        ],
        out_specs=pl.BlockSpec((nr, L, GLA_V + ML_W), lambda b, i: (b, i, 0)),
        scratch_shapes=[
            pltpu.VMEM((nr, GLA_DV, GLA_QK), F32),
            pltpu.VMEM((nr, MLSTM_HEADS, MLSTM_DH, 2 * MLSTM_DH), F32),
            pltpu.VMEM((nr, 1, LANES), F32),
            pltpu.VMEM((nr, 8 + L, 2 * ML_W), F32),
        ],
        compiler_params=_params("parallel", "arbitrary"),
        name="mixer0",
    )(proj, wa_pad, ba.reshape(1, GLA_QK), gla_g.reshape(1, GLA_V), conv_w, conv_b.reshape(1, 2 * ML_W),
      gbias, ml_g.reshape(1, ML_W))


def _sb_kernel(q_ref, k_ref, v_ref, o_ref, qs_ref, acc_ref, car_ref, mar_ref, kn_ref):
    TQ, SUB, WIDE = SB_TQ, SB_SUB, SB_WIDE
    ns = TQ // SUB
    RB = 2 * SUB
    i = pl.program_id(2)
    lane = _iota((1, LANES), 1)
    head0 = (lane >> _HEAD_SHIFT) == 0

    @pl.when(i == 0)
    def _():
        k2 = jnp.square(k_ref[0].astype(F32))
        for h in range(2):
            n2 = jnp.sum(jnp.where(head0 if h == 0 else ~head0, k2, 0.0), axis=1, keepdims=True)
            kn_ref[h:h + 1, :] = jnp.broadcast_to(jnp.sqrt(jnp.max(n2, axis=0, keepdims=True)), (1, LANES))

    for a in range(ns):
        qa = q_ref[0, a * SUB:(a + 1) * SUB, :].astype(F32) * (SB_DH ** -0.5 * LOG2E)
        qsq = qa * qa
        for h in range(2):
            rows = slice(a * RB + h * SUB, a * RB + (h + 1) * SUB)
            hmask = head0 if h == 0 else ~head0
            qs_ref[rows, :] = jnp.where(hmask, qa, 0.0).astype(BF16)
            n2 = jnp.sum(jnp.where(hmask, qsq, 0.0), axis=1, keepdims=True)
            qn = n2 * lax.rsqrt(n2 + 1e-30)
            mar_ref[rows, :] = qn * kn_ref[h:h + 1, :] * SB_BOUND_SLACK + SB_ZERO_BITS
    tri = jnp.where(_iota((WIDE, WIDE), 0) >= _iota((WIDE, WIDE), 1), -1.0, 0.0).astype(BF16)

    def key_rows(ku, nk):
        r0 = ku * SUB
        return pl.ds(r0 if isinstance(r0, int) else pl.multiple_of(r0, SUB), nk)

    def logits(a, ku, nk):
        return _dot_nt(qs_ref[a * RB:(a + 1) * RB, :], k_ref[0, key_rows(ku, nk), :])

    def softplus2(z, diagonal):
        sp = jnp.maximum(z, 0.0) + jnp.log(1.0 + jnp.exp2(-jnp.abs(z))) * LOG2E
        if not diagonal:
            return sp, None
        mask = _iota(z.shape, 1) < (_iota(z.shape, 0) & (SUB - 1))
        return jnp.where(mask, sp, 0.0), mask

    def suffix(sp):
        nk = sp.shape[1]
        return _dot(sp.astype(BF16), tri[:nk, :nk]), jnp.sum(sp, axis=1, keepdims=True)

    def weights(z, res, car, mask):
        att = jnp.exp2(z + res if car is None else z + res - car)
        if mask is not None:
            att = jnp.where(mask, att, 0.0)
        return att.astype(BF16)

    def far_tile(a, ku):
        rows = slice(a * RB, (a + 1) * RB)
        z = logits(a, ku, SUB)
        sp, _ = softplus2(z, False)
        res, tot = suffix(sp)
        car = car_ref[rows, :]
        acc_ref[rows, :] += _dot(weights(z, res, car, None), v_ref[0, key_rows(ku, SUB), :])
        car_ref[rows, :] = car + tot

    def near_tiles(first_block):
        chains = []
        for a in range(ns):
            g = a if first_block else ns * i + a
            chain = [(a, g, SUB, True)]
            if not first_block or a >= 2:
                chain.append((a, g - 2, WIDE, False))
            elif a == 1:
                chain.append((a, 0, SUB, False))
            chains.append(chain)
        jobs = [job for step in itertools.zip_longest(*chains) for job in step if job is not None]
        zs = [logits(a, ku, nk) for a, ku, nk, _ in jobs]
        sps = [softplus2(z, diagonal) for z, (_, _, _, diagonal) in zip(zs, jobs)]
        sufs = [suffix(sp) for sp, _ in sps]
        car = [None] * ns
        atts = []
        for z, (sp, mask), (res, tot), (a, _, _, _) in zip(zs, sps, sufs, jobs):
            atts.append(weights(z, res, car[a], mask))
            car[a] = tot if car[a] is None else car[a] + tot
        acc = [None] * ns
        for att, (a, ku, nk, _) in zip(atts, jobs):
            t = _dot(att, v_ref[0, key_rows(ku, nk), :])
            acc[a] = t if acc[a] is None else acc[a] + t
        for a in range(ns):
            acc_ref[a * RB:(a + 1) * RB, :] = acc[a]
            car_ref[a * RB:(a + 1) * RB, :] = jnp.broadcast_to(car[a], (RB, LANES))

    pl.when(i == 0)(functools.partial(near_tiles, True))
    pl.when(i > 0)(functools.partial(near_tiles, False))

    def next_unit(a, t):
        return ns * i + a - 3 - t

    def pending(t):
        gap = [jnp.where(next_unit(a, t) >= 0, mar_ref[a * RB:(a + 1) * RB, :] - car_ref[a * RB:(a + 1) * RB, :], -1.0)
               for a in range(ns)]
        return jnp.max(functools.reduce(jnp.maximum, gap)) > 0.0

    def far_body(c):
        t, _ = c
        for a in range(ns):
            ku = next_unit(a, t)
            pl.when(ku >= 0)(functools.partial(far_tile, a, ku))
        return t + 1, pending(t + 1)

    lax.while_loop(lambda c: c[1], far_body, (0, pending(0)))

    for a in range(ns):
        o_ref[0, a * SUB:(a + 1) * SUB, :] = jnp.where(
            head0, acc_ref[a * RB:a * RB + SUB, :], acc_ref[a * RB + SUB:(a + 1) * RB, :]).astype(o_ref.dtype)


def _stick_breaking(pc):
    bsz, s, _ = pc.shape
    npair = SB_W // LANES
    return pl.pallas_call(
        _sb_kernel,
        out_shape=jax.ShapeDtypeStruct((bsz, s, SB_W), BF16),
        grid=(bsz, npair, s // SB_TQ),
        in_specs=[
            pl.BlockSpec((1, SB_TQ, LANES), lambda b, p, i: (b, i, p)),
            pl.BlockSpec((1, s, LANES), lambda b, p, i: (b, 0, npair + p)),
            pl.BlockSpec((1, s, LANES), lambda b, p, i: (b, 0, 2 * npair + p)),
        ],
        out_specs=pl.BlockSpec((1, SB_TQ, LANES), lambda b, p, i: (b, i, p)),
        scratch_shapes=[
            pltpu.VMEM((2 * SB_TQ, LANES), BF16),
            pltpu.VMEM((2 * SB_TQ, LANES), F32),
            pltpu.VMEM((2 * SB_TQ, LANES), F32),
            pltpu.VMEM((2 * SB_TQ, LANES), F32),
            pltpu.VMEM((8, LANES), F32),
        ],
        compiler_params=_params("parallel", "parallel", "arbitrary"),
        name="stick_breaking",
    )(pc, pc, pc)


def _t5_bucket(dist):
    max_exact = N_REL_BUCKETS // 2
    dd = np.maximum(dist, 1).astype(np.float64)
    large = max_exact + (np.log(dd / max_exact) / np.log(REL_MAX_DIST / max_exact)
                         * (N_REL_BUCKETS - max_exact)).astype(np.int32)
    large = np.minimum(large, N_REL_BUCKETS - 1)
    return np.where(dist < max_exact, dist, large).astype(np.int32)


def _dil_buckets():
    qi = np.arange(DIL_BLK)[:, None]
    kj = np.arange(2 * DIL_BLK)[None, :]
    delta = qi - kj + DIL_BLK
    in_win = (delta >= 0) & (delta <= DIL_BLK)
    tabs = []
    for window, dil in DIL_PAIRS:
        assert window // dil == DIL_BLK
        bucket = _t5_bucket(np.clip(delta, 0, None) * dil)
        tabs.append(np.where(in_win, bucket, -1).astype(np.int32))
    return np.stack(tabs, 0)


def _dil_kernel(q_ref, k_ref, v_ref, bkt_ref, bkt0_ref, tab_ref, o_ref, m_sc, l_sc, acc_sc, bias0_sc, bias_sc):
    s_len = q_ref.shape[1]
    first_batch = pl.program_id(0) == 0
    hp = pl.program_id(1)
    g = pl.program_id(2)
    blk = DIL_BLK
    ng = len(DIL_PAIRS)
    lane = _iota((1, LANES), 1)
    lane_head = lane >> _HEAD_SHIFT
    sel0 = lane_head == 0
    kcol = _iota((2 * blk, 2 * blk), 1)

    def head_biases(gi, bkt, dst):
        @pl.when(first_batch)
        def _():
            n = bkt.shape[0]
            for hl in range(2):
                head = gi * DIL_HEADS_PER_GROUP + hp * 2 + hl
                bias = jnp.full(bkt.shape, NEG_INF, F32)
                for bk in range(N_REL_BUCKETS):
                    bias = jnp.where(bkt == bk, tab_ref[bk, head], bias)
                dst[hl * n:(hl + 1) * n, :] = bias
        return dst[...]

    def stack_heads(qb):
        return jnp.concatenate([jnp.where(sel0, qb, 0.0), jnp.where(sel0, 0.0, qb)], axis=0).astype(BF16)

    def with_ones(vv):
        return jnp.concatenate([vv, jnp.ones_like(vv)], axis=1).astype(BF16)

    def first_group(gi, dil):
        n_sub = s_len // dil
        bias2 = head_biases(gi, bkt0_ref[...], bias0_sc.at[hp])

        def residue(r):
            rows = pl.ds(r, n_sub, stride=dil)
            logits = _dot_nt(stack_heads(q_ref[0, rows, :] * (DIL_DH ** -0.5)), k_ref[0, rows, :].astype(BF16)) + bias2
            return rows, logits

        def body(tt, carry):
            s1 = [residue(tt * DIL_UNROLL + u) for u in range(DIL_UNROLL)]
            s2 = [jnp.max(lg, axis=1, keepdims=True) for _, lg in s1]
            s3 = [_dot(jnp.exp(lg - mc).astype(BF16), with_ones(v_ref[0, rows, :])) for (rows, lg), mc in zip(s1, s2)]
            for (rows, _), mc, pv in zip(s1, s2, s3):
                m_sc[rows, :] = jnp.where(sel0, mc[0:n_sub], mc[n_sub:])
                l_sc[rows, :] = jnp.where(sel0, pv[0:n_sub, LANES:], pv[n_sub:, LANES:])
                acc_sc[rows, :] = jnp.where(sel0, pv[0:n_sub, :LANES], pv[n_sub:, :LANES])
            return carry

        lax.fori_loop(0, dil // DIL_UNROLL, body, 0)

    def group(gi, dil):
        nb = s_len // (blk * dil)
        bias2 = head_biases(gi, bkt_ref[gi], bias_sc.at[hp, gi])

        def stage_logits(t):
            r = t // nb
            n = t % nb
            q_start = r + dil * blk * n
            p_start = r + dil * blk * jnp.maximum(n - 1, 0)
            rows_q = pl.ds(q_start, blk, stride=dil) if dil > 1 else pl.ds(q_start, blk)
            rows_p = pl.ds(p_start, blk, stride=dil) if dil > 1 else pl.ds(p_start, blk)
            q2 = stack_heads(q_ref[0, rows_q, :] * (DIL_DH ** -0.5))
            kk = jnp.concatenate([k_ref[0, rows_p, :], k_ref[0, rows_q, :]], axis=0).astype(BF16)
            logits = _dot_nt(q2, kk) + bias2
            logits = jnp.where(jnp.logical_and(n == 0, kcol < blk), NEG_INF, logits)
            return rows_q, rows_p, logits

        def stage_pv(rows_q, rows_p, logits, m_col):
            p = jnp.exp(logits - m_col)
            vv = with_ones(jnp.concatenate([v_ref[0, rows_p, :], v_ref[0, rows_q, :]], axis=0))
            pv = _dot(p.astype(BF16), vv)
            return pv[:, LANES:], pv[:, :LANES]

        def body(tt, carry):
            ts = [tt * DIL_UNROLL + u for u in range(DIL_UNROLL)]
            s1 = [stage_logits(t) for t in ts]
            s2 = [jnp.max(lg, axis=1, keepdims=True) for _, _, lg in s1]
            s3 = [stage_pv(rq, rp, lg, mc) for (rq, rp, lg), mc in zip(s1, s2)]
            outs = []
            for (rows_q, _, _), m_col, (ps, pv) in zip(s1, s2, s3):
                m_old = m_sc[rows_q, :]
                m_blk = jnp.where(sel0, m_col[0:blk], m_col[blk:])
                m_new = jnp.maximum(m_old, m_blk)
                a_old = jnp.exp(m_old - m_new)
                a_blk = jnp.exp(m_blk - m_new)
                l_new = a_old * l_sc[rows_q, :] + a_blk * jnp.where(sel0, ps[0:blk], ps[blk:])
                a_new = a_old * acc_sc[rows_q, :] + a_blk * jnp.where(sel0, pv[0:blk], pv[blk:])
                outs.append((rows_q, m_new, l_new, a_new))
            for rows_q, m_new, l_new, a_new in outs:
                m_sc[rows_q, :] = m_new
                l_sc[rows_q, :] = l_new
                acc_sc[rows_q, :] = a_new
            return carry

        for tt in range(s_len // (blk * DIL_UNROLL)):
            body(tt, 0)

    for gi, (_, dil) in enumerate(DIL_PAIRS):
        pl.when(g == ng - 1 - gi)(functools.partial(first_group if gi == ng - 1 else group, gi, dil))

    @pl.when(g == ng - 1)
    def _():
        o_ref[0] = (acc_sc[...] / l_sc[...]).astype(o_ref.dtype)


def _dilated(pd, rel_bias):
    bsz, s, _ = pd.shape
    ng = len(DIL_PAIRS)
    gw = DIL_HEADS_PER_GROUP * DIL_DH // LANES
    nq = DIL_W // LANES
    bkt = jnp.asarray(_dil_buckets())
    dil0 = DIL_PAIRS[-1][1]
    n_sub = s // dil0
    assert s % (DIL_BLK * dil0) == 0 and dil0 % DIL_UNROLL == 0
    delta = np.arange(n_sub)[:, None] - np.arange(n_sub)[None, :]
    bkt0 = jnp.asarray(np.where((delta >= 0) & (delta <= DIL_BLK), _t5_bucket(np.clip(delta, 0, None) * dil0), -1)
                       .astype(np.int32))
    blk_spec = lambda base: pl.BlockSpec((1, s, LANES), lambda b, p, g: (b, 0, base + (ng - 1 - g) * gw + p))
    return pl.pallas_call(
        _dil_kernel,
        out_shape=jax.ShapeDtypeStruct((bsz, s, DIL_OUT), BF16),
        grid=(bsz, gw, ng),
        in_specs=[
            blk_spec(0),
            blk_spec(nq),
            blk_spec(2 * nq),
            pl.BlockSpec((ng, DIL_BLK, 2 * DIL_BLK), lambda b, p, g: (0, 0, 0)),
            pl.BlockSpec((n_sub, n_sub), lambda b, p, g: (0, 0)),
            pl.BlockSpec(memory_space=pltpu.SMEM),
        ],
        out_specs=pl.BlockSpec((1, s, LANES), lambda b, p, g: (b, 0, p)),
        scratch_shapes=[pltpu.VMEM((s, LANES), F32)] * 3 + [
            pltpu.VMEM((gw, 2 * n_sub, n_sub), F32),
            pltpu.VMEM((gw, ng - 1, 2 * DIL_BLK, 2 * DIL_BLK), F32),
        ],
        compiler_params=_params("arbitrary", "arbitrary", "arbitrary"),
        name="dilated",
    )(pd, pd, pd, bkt, bkt0, rel_bias)


def _ab_weight(w):
    o = np.cumsum((0,) + (GLA_QK, GLA_QK, GLA_V, GLA_V, GLA_GATE_RANK, 2 * ML_W, ML_W, MLSTM_HEADS, MLSTM_HEADS, ML_W))
    qa, ka, va, ra, aa, qkb, vb, ib, fb, ob = [w[:, o[j]:o[j + 1]] for j in range(10)]
    pad = jnp.zeros((w.shape[0], LANES - 2 * MLSTM_HEADS - GLA_GATE_RANK), w.dtype)
    return jnp.concatenate([qa, ka, va, ra, qkb, vb, ob, ib, fb, aa, pad], axis=1).astype(BF16)


def kernel(x, c, ada_w, ada_b, ln_g, ln_b, ab_w_in, gla_wa_up, gla_ba, gla_norm_g, ml_conv_w, ml_conv_b,
           ml_b_i, ml_b_f, ml_norm_g, ab_w_out, cd_w_in, rel_bias, cd_w_out, ffn_w1, ffn_w3, ffn_w2):
    bsz, s, d = x.shape
    mod_all = _ada_mod(c, ada_w, ada_b).reshape(DEPTH, bsz, 6, d)
    tm = IN_TM
    for layer in range(DEPTH):
        mod = mod_all[layer]
        j = layer // 2
        if layer % 2 == 0:
            (proj,) = _inproj(x, mod, _ab_weight(ab_w_in[j]), [(P0_N, F32)], tm, 0, 1, "inproj0")
            cat = _mixer0(proj, gla_wa_up[j], gla_ba[j], gla_norm_g[j], ml_conv_w[j], ml_conv_b[j],
                          ml_b_i[j], ml_b_f[j], ml_norm_g[j])
            acts, w_outs = [cat], [ab_w_out[j].astype(BF16)]
        else:
            pc, pd = _inproj(x, mod, cd_w_in[j].astype(BF16), [(3 * SB_W, BF16), (3 * DIL_W, F32)],
                             tm, 0, 1, "inproj1")
            oc = _stick_breaking(pc)
            od = _dilated(pd, rel_bias)
            w_out = cd_w_out[j].astype(BF16)
            acts, w_outs = [oc, od], [w_out[:SB_W], w_out[SB_W:]]
        x = _post(acts, w_outs, x, mod, ffn_w1[layer], ffn_w3[layer], ffn_w2[layer], ln_g[layer], ln_b[layer], FFN_TM)
    return x
```

```python
import functools
import itertools

import numpy as np
import jax
import jax.numpy as jnp
from jax import lax
from jax.experimental import pallas as pl
from jax.experimental.pallas import tpu as pltpu

F32 = jnp.float32
BF16 = jnp.bfloat16

D_MODEL = 1024
DEPTH = 2
GLA_HEADS = 4
GLA_DK = 64
GLA_DV = 128
GLA_GATE_RANK = 16
GLA_TAU = 16.0
GLA_CHUNK = 64
MLSTM_HEADS = 4
MLSTM_DH = 128
MLSTM_CONV = 4
SB_HEADS = 8
SB_DH = 64
DIL_PAIRS = ((128, 1), (512, 4), (2048, 16))
DIL_HEADS_PER_GROUP = 4
DIL_DH = 64
DIL_BLK = 128
DIL_UNROLL = 4
N_REL_BUCKETS = 32
REL_MAX_DIST = 2048
D_FF = ((8 * D_MODEL + 3 * 256 - 1) // (3 * 256)) * 256
LN_EPS = 1e-5
RES_ALPHA = (2 * DEPTH) ** 0.25
NEG_INF = -1e30

GLA_QK = GLA_HEADS * GLA_DK
GLA_V = GLA_HEADS * GLA_DV
ML_W = MLSTM_HEADS * MLSTM_DH
SB_W = SB_HEADS * SB_DH
DIL_W = len(DIL_PAIRS) * DIL_HEADS_PER_GROUP * DIL_DH
DIL_OUT = DIL_HEADS_PER_GROUP * DIL_DH

LANES = 128
VMEM_LIMIT = 56 * 1024 * 1024

P0_QA = 0
P0_KA = P0_QA + GLA_QK
P0_VA = P0_KA + GLA_QK
P0_RA = P0_VA + GLA_V
P0_QKB = P0_RA + GLA_V
P0_VB = P0_QKB + 2 * ML_W
P0_OB = P0_VB + ML_W
P0_G = P0_OB + ML_W
P0_N = P0_G + LANES
G_I = 0
G_F = MLSTM_HEADS
G_A = 2 * MLSTM_HEADS

MIX_TILE = 256
MIX_ROWS = 4
FF_CHUNK = 256
IN_TM = 512
FFN_TM = 1024
POST_SUB = 512
SB_TQ = 4096
SB_SUB = 128
SB_WIDE = 256
SB_ZERO_BITS = 160.0
SB_BOUND_SLACK = 1.01
LOG2E = 1.4426950408889634
_CHUNK_SHIFT = 6
_HEAD_SHIFT = 6
assert GLA_CHUNK == GLA_DK == 1 << _CHUNK_SHIFT and SB_DH == DIL_DH == 1 << _HEAD_SHIFT


def _dot(a, b):
    return jnp.dot(a, b, preferred_element_type=F32)


def _dot_nt(a, b):
    return lax.dot_general(a, b, (((1,), (1,)), ((), ())), preferred_element_type=F32)


def _dot_tn(a, b):
    return lax.dot_general(a, b, (((0,), (0,)), ((), ())), preferred_element_type=F32)


def _dot_split(t, x, terms):
    acc = None
    rem = x
    for i in range(terms):
        part = rem.astype(BF16)
        d = _dot(t, part)
        acc = d if acc is None else acc + d
        if i + 1 < terms:
            rem = rem - part.astype(F32)
    return acc


def _log_sigmoid(x):
    return jnp.minimum(x, 0.0) - jnp.log(1.0 + jnp.exp(-jnp.abs(x)))


def _silu(x):
    return x * jax.nn.sigmoid(x)


def _layer_norm(r, g, b):
    mu = jnp.mean(r, axis=-1, keepdims=True)
    d = r - mu
    var = jnp.mean(d * d, axis=-1, keepdims=True)
    return d * lax.rsqrt(var + LN_EPS) * g + b


def _head_norm(x, g):
    mu = jnp.mean(x, axis=-1, keepdims=True)
    d = x - mu
    var = jnp.mean(d * d, axis=-1, keepdims=True)
    return d * lax.rsqrt(var + LN_EPS) * g


def _iota(shape, dim):
    return lax.broadcasted_iota(jnp.int32, shape, dim)


def _params(*sem):
    return pltpu.CompilerParams(dimension_semantics=sem, vmem_limit_bytes=VMEM_LIMIT)


def _ada_kernel(c_ref, w_ref, b_ref, o_ref):
    ca = _silu(c_ref[...]).astype(BF16)
    o_ref[0] = _dot(ca, w_ref[0].astype(BF16)) + b_ref[0]


def _ada_mod(c, ada_w, ada_b):
    bsz, d = c.shape
    n = ada_w.shape[-1]
    tn = n // 4
    return pl.pallas_call(
        _ada_kernel,
        out_shape=jax.ShapeDtypeStruct((DEPTH, bsz, n), F32),
        grid=(DEPTH, n // tn),
        in_specs=[
            pl.BlockSpec((bsz, d), lambda l, j: (0, 0)),
            pl.BlockSpec((1, d, tn), lambda l, j: (l, 0, j)),
            pl.BlockSpec((1, 1, tn), lambda l, j: (l, 0, j)),
        ],
        out_specs=pl.BlockSpec((1, bsz, tn), lambda l, j: (l, 0, j)),
        compiler_params=_params("arbitrary", "arbitrary"),
        name="ada_mod",
    )(c, ada_w, ada_b.reshape(DEPTH, 1, n))


def _inproj_kernel(x_ref, mod_ref, w_ref, *o_refs, shift_row, scale_row):
    sh = mod_ref[0, shift_row:shift_row + 1, :]
    sc = mod_ref[0, scale_row:scale_row + 1, :]
    hm = (x_ref[0] * (1.0 + sc) + sh).astype(BF16)
    col = 0
    for o_ref in o_refs:
        n = o_ref.shape[-1]
        o_ref[0] = _dot(hm, w_ref[:, col:col + n]).astype(o_ref.dtype)
        col += n


def _inproj(x, mod, w, outs, tm, shift_row, scale_row, name):
    bsz, s, d = x.shape
    n = w.shape[1]
    assert sum(o[0] for o in outs) == n
    return pl.pallas_call(
        functools.partial(_inproj_kernel, shift_row=shift_row, scale_row=scale_row),
        out_shape=[jax.ShapeDtypeStruct((bsz, s, o[0]), o[1]) for o in outs],
        grid=(bsz, s // tm),
        in_specs=[
            pl.BlockSpec((1, tm, d), lambda b, i: (b, i, 0)),
            pl.BlockSpec((1, 6, d), lambda b, i: (b, 0, 0)),
            pl.BlockSpec((d, n), lambda b, i: (0, 0)),
        ],
        out_specs=[pl.BlockSpec((1, tm, o[0]), lambda b, i: (b, i, 0)) for o in outs],
        compiler_params=_params("parallel", "arbitrary"),
        name=name,
    )(x, mod, w)


def _post_kernel(*refs, n_act):
    act_refs = refs[:n_act]
    wo_refs = refs[n_act:2 * n_act]
    x_ref, mod_ref, w1_ref, w3_ref, w2_ref, g_ref, b_ref, o_ref, gm_ref = refs[2 * n_act:]

    def mix(t):
        rows = slice(t * POST_SUB, (t + 1) * POST_SUB)
        y = None
        for a_ref, w_ref in zip(act_refs, wo_refs):
            part = _dot(a_ref[0, rows, :], w_ref[...])
            y = part if y is None else y + part
        r = RES_ALPHA * x_ref[0, rows, :] + (1.0 + mod_ref[0, 2:3, :]) * y
        xm = _layer_norm(r, g_ref[0:1, :], b_ref[0:1, :])
        return xm, (xm * (1.0 + mod_ref[0, 4:5, :]) + mod_ref[0, 3:4, :]).astype(BF16)

    def up(t, hf):
        for c in range(D_FF // FF_CHUNK):
            cs = slice(c * FF_CHUNK, (c + 1) * FF_CHUNK)
            gm_ref[t, :, cs] = (_silu(_dot(hf, w1_ref[:, cs])) * _dot(hf, w3_ref[:, cs])).astype(BF16)

    def down(t, xm):
        rows = slice(t * POST_SUB, (t + 1) * POST_SUB)
        r2 = RES_ALPHA * xm + (1.0 + mod_ref[0, 5:6, :]) * _dot(gm_ref[t], w2_ref[...])
        o_ref[0, rows, :] = _layer_norm(r2, g_ref[1:2, :], b_ref[1:2, :])

    n = x_ref.shape[1] // POST_SUB
    mixed = {}
    for step in range(n + 2):
        if step >= 2:
            down(step - 2, mixed[step - 2][0])
        if 1 <= step <= n:
            up(step - 1, mixed[step - 1][1])
        if step < n:
            mixed[step] = mix(step)


def _post(acts, w_outs, x, mod, w1, w3, w2, ln_g, ln_b, tm):
    bsz, s, d = x.shape
    n_act = len(acts)
    const = lambda shape: pl.BlockSpec(shape, lambda b, i: (0,) * len(shape), pipeline_mode=pl.Buffered(1))
    in_specs = [pl.BlockSpec((1, tm, a.shape[-1]), lambda b, i: (b, i, 0)) for a in acts]
    in_specs += [const(w.shape) for w in w_outs]
    in_specs += [
        pl.BlockSpec((1, tm, d), lambda b, i: (b, i, 0)),
        pl.BlockSpec((1, 6, d), lambda b, i: (b, 0, 0)),
        const((d, D_FF)),
        const((d, D_FF)),
        const((D_FF, d)),
        const((2, d)),
        const((2, d)),
    ]
    return pl.pallas_call(
        functools.partial(_post_kernel, n_act=n_act),
        out_shape=jax.ShapeDtypeStruct((bsz, s, d), F32),
        grid=(bsz, s // tm),
        in_specs=in_specs,
        out_specs=pl.BlockSpec((1, tm, d), lambda b, i: (b, i, 0)),
        scratch_shapes=[pltpu.VMEM((tm // POST_SUB, POST_SUB, D_FF), BF16)],
        compiler_params=_params("parallel", "arbitrary"),
        name="post",
    )(*acts, *w_outs, x, mod, w1.astype(BF16), w3.astype(BF16), w2.astype(BF16), ln_g, ln_b)


def _mixer0_kernel(p_ref, wa_ref, ba_ref, gg_ref, cw_ref, cb_ref, gb_ref, mg_ref, o_ref,
                   st_ref, cst_ref, m_ref, xc_ref):
    @pl.when(pl.program_id(1) == 0)
    def _():
        st_ref[...] = jnp.zeros_like(st_ref)
        cst_ref[...] = jnp.zeros_like(cst_ref)
        m_ref[...] = jnp.zeros_like(m_ref)
        xc_ref[:, 0:8, :] = jnp.zeros((xc_ref.shape[0], 8, 2 * ML_W), F32)

    L = MIX_TILE
    row = _iota((L, L), 0)
    col = _iota((L, L), 1)
    causal = col <= row
    same_chunk = (row >> _CHUNK_SHIFT) == (col >> _CHUNK_SHIFT)
    lane_qk = _iota((1, GLA_QK), 1)
    r4 = _iota((GLA_HEADS * GLA_CHUNK, GLA_CHUNK), 0)
    c4 = _iota((GLA_HEADS * GLA_CHUNK, GLA_CHUNK), 1)
    consts = dict(
        causal=causal,
        tri=jnp.where(causal, 1.0, 0.0).astype(BF16),
        tri_blk=jnp.where(same_chunk & causal, 1.0, 0.0).astype(BF16),
        head_masks=[(lane_qk >> _CHUNK_SHIFT) == h for h in range(GLA_HEADS)],
        tril4=c4 <= (r4 & (GLA_CHUNK - 1)),
    )
    for b in range(p_ref.shape[0]):
        _mixer0_row(consts, p_ref.at[pl.ds(b, 1)], wa_ref, ba_ref, gg_ref, cw_ref, cb_ref, gb_ref, mg_ref,
                    o_ref.at[pl.ds(b, 1)], st_ref.at[b], cst_ref.at[b], m_ref.at[b], xc_ref.at[b])


def _mixer0_row(consts, p_ref, wa_ref, ba_ref, gg_ref, cw_ref, cb_ref, gb_ref, mg_ref, o_ref,
                st_ref, cst_ref, m_ref, xc_ref):
    L = MIX_TILE
    causal, tri, tri_blk = consts["causal"], consts["tri"], consts["tri_blk"]
    head_masks, tril4 = consts["head_masks"], consts["tril4"]
    lane = _iota((1, LANES), 1)
    g_raw = p_ref[0, :, P0_G:P0_G + LANES]

    u = _dot(g_raw.astype(BF16), wa_ref[...]) + ba_ref[...]
    la = _log_sigmoid(u) * (1.0 / GLA_TAU)
    bcs = _dot_split(tri_blk, la, 3)
    q_in = p_ref[0, :, P0_QA:P0_QA + GLA_QK] * (GLA_DK ** -0.5) * jnp.exp(bcs)
    k_raw = p_ref[0, :, P0_KA:P0_KA + GLA_QK]
    k_in = k_raw * jnp.exp(-bcs)
    oa_chunks = []
    for c in range(L // GLA_CHUNK):
        r0, r1 = c * GLA_CHUNK, (c + 1) * GLA_CHUNK
        b_c = bcs[r0:r1]
        bl = b_c[GLA_CHUNK - 1:GLA_CHUNK, :]
        q_c = q_in[r0:r1]
        k_c = k_in[r0:r1].astype(BF16)
        k_end = k_raw[r0:r1] * jnp.exp(bl - b_c)
        v_c = p_ref[0, r0:r1, P0_VA:P0_VA + GLA_V]
        q_exp = jnp.concatenate([jnp.where(hm, q_c, 0.0) for hm in head_masks], axis=0).astype(BF16)
        k_exp = jnp.concatenate([jnp.where(hm, k_end, 0.0) for hm in head_masks], axis=0).astype(BF16)
        sc = jnp.where(tril4, _dot_nt(q_exp, k_c), 0.0)
        intra = _dot(sc.astype(BF16), v_c.astype(BF16))
        inter = _dot_nt(q_exp, st_ref[...].astype(BF16))
        o_heads = []
        for h in range(GLA_HEADS):
            h0, h1 = h * GLA_CHUNK, (h + 1) * GLA_CHUNK
            o_heads.append(intra[h0:h1, h * GLA_DV:(h + 1) * GLA_DV] + inter[h0:h1])
        oa_chunks.append(o_heads)
        v_cat = jnp.concatenate([v_c[:, h * GLA_DV:(h + 1) * GLA_DV] for h in range(GLA_HEADS)], axis=0)
        st_ref[...] = st_ref[...] * jnp.exp(bl) + _dot_tn(v_cat.astype(BF16), k_exp)
    for h in range(GLA_HEADS):
        o_h = jnp.concatenate([oc[h] for oc in oa_chunks], axis=0)
        cs = slice(h * GLA_DV, (h + 1) * GLA_DV)
        ra = p_ref[0, :, P0_RA + h * GLA_DV:P0_RA + (h + 1) * GLA_DV]
        o_ref[0, :, cs] = (_head_norm(o_h, gg_ref[:, cs]) * _silu(ra)).astype(o_ref.dtype)

    xc_ref[8:8 + L, :] = p_ref[0, :, P0_QKB:P0_QKB + 2 * ML_W]
    conv = cb_ref[...]
    for kk in range(MLSTM_CONV):
        conv = conv + cw_ref[kk:kk + 1, :] * xc_ref[8 - (MLSTM_CONV - 1) + kk:8 - (MLSTM_CONV - 1) + kk + L, :]
    xc_ref[0:8, :] = xc_ref[L:L + 8, :]
    qk_b = _silu(conv)

    gb = jnp.where(lane < G_A, g_raw + gb_ref[...], 0.0)
    ipre = gb
    logf = _log_sigmoid(pltpu.roll(gb, LANES - G_F, axis=1))
    logf = jnp.where(lane < MLSTM_HEADS, logf, 0.0)
    bcum = _dot_split(tri, logf, 3)
    blast = bcum[L - 1:L, :]
    wend = blast - bcum + ipre
    m_prev = m_ref[...]
    m_new = jnp.maximum(blast + m_prev, jnp.max(wend, axis=0, keepdims=True))
    scl = jnp.exp(blast + m_prev - m_new)
    wj = jnp.exp(wend - m_new)
    rows_src = jnp.where(lane < MLSTM_HEADS, ipre, pltpu.roll(bcum, MLSTM_HEADS, axis=1))
    rows_t = rows_src.T
    ones = jnp.ones((L, MLSTM_DH), F32)
    for h in range(MLSTM_HEADS):
        hs = slice(h * MLSTM_DH, (h + 1) * MLSTM_DH)
        wide = lambda t: jnp.concatenate([t, t], axis=1)
        b_rep = jnp.broadcast_to(bcum[:, h:h + 1], (L, LANES))
        wj_rep = jnp.broadcast_to(wj[:, h:h + 1], (L, LANES))
        ip_row = rows_t[h:h + 1, :]
        b_row = rows_t[MLSTM_HEADS + h:MLSTM_HEADS + h + 1, :]
        dlog = jnp.where(causal, wide(b_rep) - b_row + ip_row, NEG_INF)
        il = b_rep + m_prev[:, h:h + 1]
        m_i = jnp.maximum(il, jnp.broadcast_to(jnp.max(dlog, axis=1, keepdims=True), (L, LANES)))
        w_intra = jnp.exp(dlog - wide(m_i))
        s_inter = jnp.exp(il - m_i)
        q_h = qk_b[:, h * MLSTM_DH:(h + 1) * MLSTM_DH].astype(BF16)
        k_h = (qk_b[:, ML_W + h * MLSTM_DH:ML_W + (h + 1) * MLSTM_DH] * (MLSTM_DH ** -0.5)).astype(BF16)
        v_h = p_ref[0, :, P0_VB + h * MLSTM_DH:P0_VB + (h + 1) * MLSTM_DH]
        v_aug = jnp.concatenate([v_h, ones], axis=1)
        a = (w_intra * _dot_nt(q_h, k_h)).astype(BF16)
        c_prev = cst_ref[h]
        tot = _dot(a, v_aug.astype(BF16)) + wide(s_inter) * _dot(q_h, c_prev.astype(BF16))
        num = tot[:, :MLSTM_DH]
        den = tot[:, MLSTM_DH:]
        hid = num / jnp.maximum(jnp.abs(den), jnp.exp(-m_i))
        w_aug = (v_aug * wide(wj_rep)).astype(BF16)
        cst_ref[h] = scl[:, h:h + 1] * c_prev + _dot_tn(k_h, w_aug)
        ob = p_ref[0, :, P0_OB + h * MLSTM_DH:P0_OB + (h + 1) * MLSTM_DH]
        o_ref[0, :, GLA_V + h * MLSTM_DH:GLA_V + (h + 1) * MLSTM_DH] = (
            jax.nn.sigmoid(ob) * _head_norm(hid, mg_ref[:, hs])).astype(o_ref.dtype)
    m_ref[...] = m_new


def _mixer0(proj, wa_up, ba, gla_g, conv_w, conv_b, b_i, b_f, ml_g):
    bsz, s, n = proj.shape
    L = MIX_TILE
    nr = MIX_ROWS if bsz % MIX_ROWS == 0 else 1
    wa_pad = jnp.zeros((LANES, GLA_QK), F32).at[G_A:G_A + GLA_GATE_RANK].set(wa_up).astype(BF16)
    gbias = jnp.zeros((1, LANES), F32).at[0, G_I:G_I + MLSTM_HEADS].set(b_i).at[0, G_F:G_F + MLSTM_HEADS].set(b_f)
    full = lambda shape: pl.BlockSpec(shape, lambda b, i: (0,) * len(shape))
    return pl.pallas_call(
        _mixer0_kernel,
        out_shape=jax.ShapeDtypeStruct((bsz, s, GLA_V + ML_W), BF16),
        grid=(bsz // nr, s // L),
        in_specs=[
            pl.BlockSpec((nr, L, n), lambda b, i: (b, i, 0)),
            full((LANES, GLA_QK)),
            full((1, GLA_QK)),
            full((1, GLA_V)),
            full((MLSTM_CONV, 2 * ML_W)),
            full((1, 2 * ML_W)),
            full((1, LANES)),
            full((1, ML_W)),
        ],
        out_specs=pl.BlockSpec((nr, L, GLA_V + ML_W), lambda b, i: (b, i, 0)),
        scratch_shapes=[
            pltpu.VMEM((nr, GLA_DV, GLA_QK), F32),
            pltpu.VMEM((nr, MLSTM_HEADS, MLSTM_DH, 2 * MLSTM_DH), F32),
            pltpu.VMEM((nr, 1, LANES), F32),
            pltpu.VMEM((nr, 8 + L, 2 * ML_W), F32),
        ],
        compiler_params=_params("parallel", "arbitrary"),
        name="mixer0",
    )(proj, wa_pad, ba.reshape(1, GLA_QK), gla_g.reshape(1, GLA_V), conv_w, conv_b.reshape(1, 2 * ML_W),
      gbias, ml_g.reshape(1, ML_W))


def _sb_kernel(q_ref, k_ref, v_ref, o_ref, qs_ref, acc_ref, car_ref, mar_ref, kn_ref):
    TQ, SUB, WIDE = q_ref.shape[1], SB_SUB, SB_WIDE
    ns = TQ // SUB
    RB = 2 * SUB
    i = pl.program_id(2)
    lane = _iota((1, LANES), 1)
    head0 = (lane >> _HEAD_SHIFT) == 0

    @pl.when(i == 0)
    def _():
        k2 = jnp.square(k_ref[0].astype(F32))
        for h in range(2):
            n2 = jnp.sum(jnp.where(head0 if h == 0 else ~head0, k2, 0.0), axis=1, keepdims=True)
            kn_ref[h:h + 1, :] = jnp.broadcast_to(jnp.sqrt(jnp.max(n2, axis=0, keepdims=True)), (1, LANES))

    for a in range(ns):
        qa = q_ref[0, a * SUB:(a + 1) * SUB, :].astype(F32) * (SB_DH ** -0.5 * LOG2E)
        qsq = qa * qa
        for h in range(2):
            rows = slice(a * RB + h * SUB, a * RB + (h + 1) * SUB)
            hmask = head0 if h == 0 else ~head0
            qs_ref[rows, :] = jnp.where(hmask, qa, 0.0).astype(BF16)
            n2 = jnp.sum(jnp.where(hmask, qsq, 0.0), axis=1, keepdims=True)
            qn = n2 * lax.rsqrt(n2 + 1e-30)
            mar_ref[rows, :] = qn * kn_ref[h:h + 1, :] * SB_BOUND_SLACK + SB_ZERO_BITS
    tri = jnp.where(_iota((WIDE, WIDE), 0) >= _iota((WIDE, WIDE), 1), -1.0, 0.0).astype(BF16)

    def key_rows(ku, nk):
        r0 = ku * SUB
        return pl.ds(r0 if isinstance(r0, int) else pl.multiple_of(r0, SUB), nk)

    def logits(a, ku, nk):
        return _dot_nt(qs_ref[a * RB:(a + 1) * RB, :], k_ref[0, key_rows(ku, nk), :])

    def softplus2(z, diagonal):
        sp = jnp.maximum(z, 0.0) + jnp.log(1.0 + jnp.exp2(-jnp.abs(z))) * LOG2E
        if not diagonal:
            return sp, None
        mask = _iota(z.shape, 1) < (_iota(z.shape, 0) & (SUB - 1))
        return jnp.where(mask, sp, 0.0), mask

    def suffix(sp):
        nk = sp.shape[1]
        return _dot(sp.astype(BF16), tri[:nk, :nk]), jnp.sum(sp, axis=1, keepdims=True)

    def weights(z, res, car, mask):
        att = jnp.exp2(z + res if car is None else z + res - car)
        if mask is not None:
            att = jnp.where(mask, att, 0.0)
        return att.astype(BF16)

    def far_tile(a, ku):
        rows = slice(a * RB, (a + 1) * RB)
        z = logits(a, ku, SUB)
        sp, _ = softplus2(z, False)
        res, tot = suffix(sp)
        car = car_ref[rows, :]
        acc_ref[rows, :] += _dot(weights(z, res, car, None), v_ref[0, key_rows(ku, SUB), :])
        car_ref[rows, :] = car + tot

    def near_tiles(first_block):
        chains = []
        for a in range(ns):
            g = a if first_block else ns * i + a
            chain = [(a, g, SUB, True)]
            if not first_block or a >= 2:
                chain.append((a, g - 2, WIDE, False))
            elif a == 1:
                chain.append((a, 0, SUB, False))
            chains.append(chain)
        jobs = [job for step in itertools.zip_longest(*chains) for job in step if job is not None]
        zs = [logits(a, ku, nk) for a, ku, nk, _ in jobs]
        sps = [softplus2(z, diagonal) for z, (_, _, _, diagonal) in zip(zs, jobs)]
        sufs = [suffix(sp) for sp, _ in sps]
        car = [None] * ns
        atts = []
        for z, (sp, mask), (res, tot), (a, _, _, _) in zip(zs, sps, sufs, jobs):
            atts.append(weights(z, res, car[a], mask))
            car[a] = tot if car[a] is None else car[a] + tot
        acc = [None] * ns
        for att, (a, ku, nk, _) in zip(atts, jobs):
            t = _dot(att, v_ref[0, key_rows(ku, nk), :])
            acc[a] = t if acc[a] is None else acc[a] + t
        for a in range(ns):
            acc_ref[a * RB:(a + 1) * RB, :] = acc[a]
            car_ref[a * RB:(a + 1) * RB, :] = jnp.broadcast_to(car[a], (RB, LANES))

    pl.when(i == 0)(functools.partial(near_tiles, True))
    pl.when(i > 0)(functools.partial(near_tiles, False))

    def next_unit(a, t):
        return ns * i + a - 3 - t

    def pending(t):
        gap = [jnp.where(next_unit(a, t) >= 0, mar_ref[a * RB:(a + 1) * RB, :] - car_ref[a * RB:(a + 1) * RB, :], -1.0)
               for a in range(ns)]
        return jnp.max(functools.reduce(jnp.maximum, gap)) > 0.0

    def far_body(c):
        t, _ = c
        for a in range(ns):
            ku = next_unit(a, t)
            pl.when(ku >= 0)(functools.partial(far_tile, a, ku))
        return t + 1, pending(t + 1)

    lax.while_loop(lambda c: c[1], far_body, (0, pending(0)))

    for a in range(ns):
        o_ref[0, a * SUB:(a + 1) * SUB, :] = jnp.where(
            head0, acc_ref[a * RB:a * RB + SUB, :], acc_ref[a * RB + SUB:(a + 1) * RB, :]).astype(o_ref.dtype)


def _stick_breaking(pc):
    bsz, s, _ = pc.shape
    npair = SB_W // LANES
    tq = min(SB_TQ, s)
    return pl.pallas_call(
        _sb_kernel,
        out_shape=jax.ShapeDtypeStruct((bsz, s, SB_W), BF16),
        grid=(bsz, npair, s // tq),
        in_specs=[
            pl.BlockSpec((1, tq, LANES), lambda b, p, i: (b, i, p)),
            pl.BlockSpec((1, s, LANES), lambda b, p, i: (b, 0, npair + p)),
            pl.BlockSpec((1, s, LANES), lambda b, p, i: (b, 0, 2 * npair + p)),
        ],
        out_specs=pl.BlockSpec((1, tq, LANES), lambda b, p, i: (b, i, p)),
        scratch_shapes=[
            pltpu.VMEM((2 * tq, LANES), BF16),
            pltpu.VMEM((2 * tq, LANES), F32),
            pltpu.VMEM((2 * tq, LANES), F32),
            pltpu.VMEM((2 * tq, LANES), F32),
            pltpu.VMEM((8, LANES), F32),
        ],
        compiler_params=_params("parallel", "parallel", "arbitrary"),
        name="stick_breaking",
    )(pc, pc, pc)


def _t5_bucket(dist):
    max_exact = N_REL_BUCKETS // 2
    dd = np.maximum(dist, 1).astype(np.float64)
    large = max_exact + (np.log(dd / max_exact) / np.log(REL_MAX_DIST / max_exact)
                         * (N_REL_BUCKETS - max_exact)).astype(np.int32)
    large = np.minimum(large, N_REL_BUCKETS - 1)
    return np.where(dist < max_exact, dist, large).astype(np.int32)


def _dil_buckets():
    qi = np.arange(DIL_BLK)[:, None]
    kj = np.arange(2 * DIL_BLK)[None, :]
    delta = qi - kj + DIL_BLK
    in_win = (delta >= 0) & (delta <= DIL_BLK)
    tabs = []
    for window, dil in DIL_PAIRS:
        assert window // dil == DIL_BLK
        bucket = _t5_bucket(np.clip(delta, 0, None) * dil)
        tabs.append(np.where(in_win, bucket, -1).astype(np.int32))
    return np.stack(tabs, 0)


def _dil_kernel(q_ref, k_ref, v_ref, bkt_ref, bkt0_ref, tab_ref, o_ref, m_sc, l_sc, acc_sc, bias0_sc, bias_sc):
    s_len = q_ref.shape[1]
    first_batch = pl.program_id(0) == 0
    hp = pl.program_id(1)
    g = pl.program_id(2)
    blk = DIL_BLK
    ng = len(DIL_PAIRS)
    lane = _iota((1, LANES), 1)
    lane_head = lane >> _HEAD_SHIFT
    sel0 = lane_head == 0
    kcol = _iota((2 * blk, 2 * blk), 1)

    def head_biases(gi, bkt, dst):
        @pl.when(first_batch)
        def _():
            n = bkt.shape[0]
            for hl in range(2):
                head = gi * DIL_HEADS_PER_GROUP + hp * 2 + hl
                bias = jnp.full(bkt.shape, NEG_INF, F32)
                for bk in range(N_REL_BUCKETS):
                    bias = jnp.where(bkt == bk, tab_ref[bk, head], bias)
                dst[hl * n:(hl + 1) * n, :] = bias
        return dst[...]

    def stack_heads(qb):
        return jnp.concatenate([jnp.where(sel0, qb, 0.0), jnp.where(sel0, 0.0, qb)], axis=0).astype(BF16)

    def with_ones(vv):
        return jnp.concatenate([vv, jnp.ones_like(vv)], axis=1).astype(BF16)

    def first_group(gi, dil):
        n_sub = s_len // dil
        bias2 = head_biases(gi, bkt0_ref[...], bias0_sc.at[hp])

        def residue(r):
            rows = pl.ds(r, n_sub, stride=dil)
            logits = _dot_nt(stack_heads(q_ref[0, rows, :] * (DIL_DH ** -0.5)), k_ref[0, rows, :].astype(BF16)) + bias2
            return rows, logits

        def body(tt, carry):
            s1 = [residue(tt * DIL_UNROLL + u) for u in range(DIL_UNROLL)]
            s2 = [jnp.max(lg, axis=1, keepdims=True) for _, lg in s1]
            s3 = [_dot(jnp.exp(lg - mc).astype(BF16), with_ones(v_ref[0, rows, :])) for (rows, lg), mc in zip(s1, s2)]
            for (rows, _), mc, pv in zip(s1, s2, s3):
                m_sc[rows, :] = jnp.where(sel0, mc[0:n_sub], mc[n_sub:])
                l_sc[rows, :] = jnp.where(sel0, pv[0:n_sub, LANES:], pv[n_sub:, LANES:])
                acc_sc[rows, :] = jnp.where(sel0, pv[0:n_sub, :LANES], pv[n_sub:, :LANES])
            return carry

        lax.fori_loop(0, dil // DIL_UNROLL, body, 0)

    def group(gi, dil):
        nb = s_len // (blk * dil)
        bias2 = head_biases(gi, bkt_ref[gi], bias_sc.at[hp, gi])

        def stage_logits(t):
            r = t // nb
            n = t % nb
            q_start = r + dil * blk * n
            p_start = r + dil * blk * jnp.maximum(n - 1, 0)
            rows_q = pl.ds(q_start, blk, stride=dil) if dil > 1 else pl.ds(q_start, blk)
            rows_p = pl.ds(p_start, blk, stride=dil) if dil > 1 else pl.ds(p_start, blk)
            q2 = stack_heads(q_ref[0, rows_q, :] * (DIL_DH ** -0.5))
            kk = jnp.concatenate([k_ref[0, rows_p, :], k_ref[0, rows_q, :]], axis=0).astype(BF16)
            logits = _dot_nt(q2, kk) + bias2
            logits = jnp.where(jnp.logical_and(n == 0, kcol < blk), NEG_INF, logits)
            return rows_q, rows_p, logits

        def stage_pv(rows_q, rows_p, logits, m_col):
            p = jnp.exp(logits - m_col)
            vv = with_ones(jnp.concatenate([v_ref[0, rows_p, :], v_ref[0, rows_q, :]], axis=0))
            pv = _dot(p.astype(BF16), vv)
            return pv[:, LANES:], pv[:, :LANES]

        def body(tt, carry):
            ts = [tt * DIL_UNROLL + u for u in range(DIL_UNROLL)]
            s1 = [stage_logits(t) for t in ts]
            s2 = [jnp.max(lg, axis=1, keepdims=True) for _, _, lg in s1]
            s3 = [stage_pv(rq, rp, lg, mc) for (rq, rp, lg), mc in zip(s1, s2)]
            outs = []
            for (rows_q, _, _), m_col, (ps, pv) in zip(s1, s2, s3):
                m_old = m_sc[rows_q, :]
                m_blk = jnp.where(sel0, m_col[0:blk], m_col[blk:])
                m_new = jnp.maximum(m_old, m_blk)
                a_old = jnp.exp(m_old - m_new)
                a_blk = jnp.exp(m_blk - m_new)
                l_new = a_old * l_sc[rows_q, :] + a_blk * jnp.where(sel0, ps[0:blk], ps[blk:])
                a_new = a_old * acc_sc[rows_q, :] + a_blk * jnp.where(sel0, pv[0:blk], pv[blk:])
                outs.append((rows_q, m_new, l_new, a_new))
            for rows_q, m_new, l_new, a_new in outs:
                m_sc[rows_q, :] = m_new
                l_sc[rows_q, :] = l_new
                acc_sc[rows_q, :] = a_new
            return carry

        for tt in range(s_len // (blk * DIL_UNROLL)):
            body(tt, 0)

    for gi, (_, dil) in enumerate(DIL_PAIRS):
        pl.when(g == ng - 1 - gi)(functools.partial(first_group if gi == ng - 1 else group, gi, dil))

    @pl.when(g == ng - 1)
    def _():
        o_ref[0] = (acc_sc[...] / l_sc[...]).astype(o_ref.dtype)


def _dilated(pd, rel_bias):
    bsz, s, _ = pd.shape
    ng = len(DIL_PAIRS)
    gw = DIL_HEADS_PER_GROUP * DIL_DH // LANES
    nq = DIL_W // LANES
    bkt = jnp.asarray(_dil_buckets())
    dil0 = DIL_PAIRS[-1][1]
    n_sub = s // dil0
    assert s % (DIL_BLK * dil0) == 0 and dil0 % DIL_UNROLL == 0
    delta = np.arange(n_sub)[:, None] - np.arange(n_sub)[None, :]
    bkt0 = jnp.asarray(np.where((delta >= 0) & (delta <= DIL_BLK), _t5_bucket(np.clip(delta, 0, None) * dil0), -1)
                       .astype(np.int32))
    blk_spec = lambda base: pl.BlockSpec((1, s, LANES), lambda b, p, g: (b, 0, base + (ng - 1 - g) * gw + p))
    return pl.pallas_call(
        _dil_kernel,
        out_shape=jax.ShapeDtypeStruct((bsz, s, DIL_OUT), BF16),
        grid=(bsz, gw, ng),
        in_specs=[
            blk_spec(0),
            blk_spec(nq),
            blk_spec(2 * nq),
            pl.BlockSpec((ng, DIL_BLK, 2 * DIL_BLK), lambda b, p, g: (0, 0, 0)),
            pl.BlockSpec((n_sub, n_sub), lambda b, p, g: (0, 0)),
            pl.BlockSpec(memory_space=pltpu.SMEM),
        ],
        out_specs=pl.BlockSpec((1, s, LANES), lambda b, p, g: (b, 0, p)),
        scratch_shapes=[pltpu.VMEM((s, LANES), F32)] * 3 + [
            pltpu.VMEM((gw, 2 * n_sub, n_sub), F32),
            pltpu.VMEM((gw, ng - 1, 2 * DIL_BLK, 2 * DIL_BLK), F32),
        ],
        compiler_params=_params("arbitrary", "arbitrary", "arbitrary"),
        name="dilated",
    )(pd, pd, pd, bkt, bkt0, rel_bias)


def _ab_weight(w):
    o = np.cumsum((0,) + (GLA_QK, GLA_QK, GLA_V, GLA_V, GLA_GATE_RANK, 2 * ML_W, ML_W, MLSTM_HEADS, MLSTM_HEADS, ML_W))
    qa, ka, va, ra, aa, qkb, vb, ib, fb, ob = [w[:, o[j]:o[j + 1]] for j in range(10)]
    pad = jnp.zeros((w.shape[0], LANES - 2 * MLSTM_HEADS - GLA_GATE_RANK), w.dtype)
    return jnp.concatenate([qa, ka, va, ra, qkb, vb, ob, ib, fb, aa, pad], axis=1).astype(BF16)


def kernel(x, c, ada_w, ada_b, ln_g, ln_b, ab_w_in, gla_wa_up, gla_ba, gla_norm_g, ml_conv_w, ml_conv_b,
           ml_b_i, ml_b_f, ml_norm_g, ab_w_out, cd_w_in, rel_bias, cd_w_out, ffn_w1, ffn_w3, ffn_w2):
    bsz, s, d = x.shape
    mod_all = _ada_mod(c, ada_w, ada_b).reshape(DEPTH, bsz, 6, d)
    tm = IN_TM
    for layer in range(DEPTH):
        mod = mod_all[layer]
        j = layer // 2
        if layer % 2 == 0:
            (proj,) = _inproj(x, mod, _ab_weight(ab_w_in[j]), [(P0_N, F32)], tm, 0, 1, "inproj0")
            cat = _mixer0(proj, gla_wa_up[j], gla_ba[j], gla_norm_g[j], ml_conv_w[j], ml_conv_b[j],
                          ml_b_i[j], ml_b_f[j], ml_norm_g[j])
            acts, w_outs = [cat], [ab_w_out[j].astype(BF16)]
        else:
            pc, pd = _inproj(x, mod, cd_w_in[j].astype(BF16), [(3 * SB_W, BF16), (3 * DIL_W, F32)],
                             tm, 0, 1, "inproj1")
            oc = _stick_breaking(pc)
            od = _dilated(pd, rel_bias)
            w_out = cd_w_out[j].astype(BF16)
            acts, w_outs = [oc, od], [w_out[:SB_W], w_out[SB_W:]]
        x = _post(acts, w_outs, x, mod, ffn_w1[layer], ffn_w3[layer], ffn_w2[layer], ln_g[layer], ln_b[layer], FFN_TM)
    return x
```

```python
import functools
import itertools

import numpy as np
import jax
import jax.numpy as jnp
from jax import lax
from jax.experimental import pallas as pl
from jax.experimental.pallas import tpu as pltpu

F32 = jnp.float32
BF16 = jnp.bfloat16

D_MODEL = 1024
DEPTH = 2
GLA_HEADS = 4
GLA_DK = 64
GLA_DV = 128
GLA_GATE_RANK = 16
GLA_TAU = 16.0
GLA_CHUNK = 64
MLSTM_HEADS = 4
MLSTM_DH = 128
MLSTM_CONV = 4
SB_HEADS = 8
SB_DH = 64
DIL_PAIRS = ((128, 1), (512, 4), (2048, 16))
DIL_HEADS_PER_GROUP = 4
DIL_DH = 64
DIL_BLK = 128
DIL_UNROLL = 4
N_REL_BUCKETS = 32
REL_MAX_DIST = 2048
D_FF = ((8 * D_MODEL + 3 * 256 - 1) // (3 * 256)) * 256
LN_EPS = 1e-5
RES_ALPHA = (2 * DEPTH) ** 0.25
NEG_INF = -1e30

GLA_QK = GLA_HEADS * GLA_DK
GLA_V = GLA_HEADS * GLA_DV
ML_W = MLSTM_HEADS * MLSTM_DH
SB_W = SB_HEADS * SB_DH
DIL_W = len(DIL_PAIRS) * DIL_HEADS_PER_GROUP * DIL_DH
DIL_OUT = DIL_HEADS_PER_GROUP * DIL_DH

LANES = 128
VMEM_LIMIT = 56 * 1024 * 1024

P0_QA = 0
P0_KA = P0_QA + GLA_QK
P0_VA = P0_KA + GLA_QK
P0_RA = P0_VA + GLA_V
P0_QKB = P0_RA + GLA_V
P0_VB = P0_QKB + 2 * ML_W
P0_OB = P0_VB + ML_W
P0_G = P0_OB + ML_W
P0_N = P0_G + LANES
G_I = 0
G_F = MLSTM_HEADS
G_A = 2 * MLSTM_HEADS

MIX_TILE = 256
MIX_ROWS = 4
FF_CHUNK = 256
IN_TM = 512
FFN_TM = 1024
POST_SUB = 512
SB_TQ = 4096
SB_SUB = 128
SB_WIDE = 256
SB_ZERO_BITS = 160.0
SB_BOUND_SLACK = 1.01
LOG2E = 1.4426950408889634
_CHUNK_SHIFT = 6
_HEAD_SHIFT = 6
assert GLA_CHUNK == GLA_DK == 1 << _CHUNK_SHIFT and SB_DH == DIL_DH == 1 << _HEAD_SHIFT


def _dot(a, b):
    return jnp.dot(a, b, preferred_element_type=F32)


def _dot_nt(a, b):
    return lax.dot_general(a, b, (((1,), (1,)), ((), ())), preferred_element_type=F32)


def _dot_tn(a, b):
    return lax.dot_general(a, b, (((0,), (0,)), ((), ())), preferred_element_type=F32)


def _dot_split(t, x, terms):
    acc = None
    rem = x
    for i in range(terms):
        part = rem.astype(BF16)
        d = _dot(t, part)
        acc = d if acc is None else acc + d
        if i + 1 < terms:
            rem = rem - part.astype(F32)
    return acc


def _log_sigmoid(x):
    return jnp.minimum(x, 0.0) - jnp.log(1.0 + jnp.exp(-jnp.abs(x)))


def _silu(x):
    return x * jax.nn.sigmoid(x)


def _layer_norm(r, g, b):
    mu = jnp.mean(r, axis=-1, keepdims=True)
    d = r - mu
    var = jnp.mean(d * d, axis=-1, keepdims=True)
    return d * lax.rsqrt(var + LN_EPS) * g + b


def _head_norm(x, g):
    mu = jnp.mean(x, axis=-1, keepdims=True)
    d = x - mu
    var = jnp.mean(d * d, axis=-1, keepdims=True)
    return d * lax.rsqrt(var + LN_EPS) * g


def _iota(shape, dim):
    return lax.broadcasted_iota(jnp.int32, shape, dim)


def _params(*sem):
    return pltpu.CompilerParams(dimension_semantics=sem, vmem_limit_bytes=VMEM_LIMIT)


def _ada_kernel(c_ref, w_ref, b_ref, o_ref):
    ca = _silu(c_ref[...]).astype(BF16)
    o_ref[0] = _dot(ca, w_ref[0].astype(BF16)) + b_ref[0]


def _ada_mod(c, ada_w, ada_b):
    bsz, d = c.shape
    n = ada_w.shape[-1]
    tn = n // 4
    return pl.pallas_call(
        _ada_kernel,
        out_shape=jax.ShapeDtypeStruct((DEPTH, bsz, n), F32),
        grid=(DEPTH, n // tn),
        in_specs=[
            pl.BlockSpec((bsz, d), lambda l, j: (0, 0)),
            pl.BlockSpec((1, d, tn), lambda l, j: (l, 0, j)),
            pl.BlockSpec((1, 1, tn), lambda l, j: (l, 0, j)),
        ],
        out_specs=pl.BlockSpec((1, bsz, tn), lambda l, j: (l, 0, j)),
        compiler_params=_params("arbitrary", "arbitrary"),
        name="ada_mod",
    )(c, ada_w, ada_b.reshape(DEPTH, 1, n))


def _inproj_kernel(x_ref, mod_ref, w_ref, *o_refs, shift_row, scale_row):
    sh = mod_ref[0, shift_row:shift_row + 1, :]
    sc = mod_ref[0, scale_row:scale_row + 1, :]
    hm = (x_ref[0] * (1.0 + sc) + sh).astype(BF16)
    col = 0
    for o_ref in o_refs:
        n = o_ref.shape[-1]
        o_ref[0] = _dot(hm, w_ref[:, col:col + n]).astype(o_ref.dtype)
        col += n


def _inproj(x, mod, w, outs, tm, shift_row, scale_row, name):
    bsz, s, d = x.shape
    n = w.shape[1]
    assert sum(o[0] for o in outs) == n
    return pl.pallas_call(
        functools.partial(_inproj_kernel, shift_row=shift_row, scale_row=scale_row),
        out_shape=[jax.ShapeDtypeStruct((bsz, s, o[0]), o[1]) for o in outs],
        grid=(bsz, s // tm),
        in_specs=[
            pl.BlockSpec((1, tm, d), lambda b, i: (b, i, 0)),
            pl.BlockSpec((1, 6, d), lambda b, i: (b, 0, 0)),
            pl.BlockSpec((d, n), lambda b, i: (0, 0)),
        ],
        out_specs=[pl.BlockSpec((1, tm, o[0]), lambda b, i: (b, i, 0)) for o in outs],
        compiler_params=_params("parallel", "arbitrary"),
        name=name,
    )(x, mod, w)


def _post_kernel(*refs, n_act):
    act_refs = refs[:n_act]
    wo_refs = refs[n_act:2 * n_act]
    x_ref, mod_ref, w1_ref, w3_ref, w2_ref, g_ref, b_ref, o_ref, gm_ref = refs[2 * n_act:]

    def mix(t):
        rows = slice(t * POST_SUB, (t + 1) * POST_SUB)
        y = None
        for a_ref, w_ref in zip(act_refs, wo_refs):
            part = _dot(a_ref[0, rows, :], w_ref[...])
            y = part if y is None else y + part
        r = RES_ALPHA * x_ref[0, rows, :] + (1.0 + mod_ref[0, 2:3, :]) * y
        xm = _layer_norm(r, g_ref[0:1, :], b_ref[0:1, :])
        return xm, (xm * (1.0 + mod_ref[0, 4:5, :]) + mod_ref[0, 3:4, :]).astype(BF16)

    def up(t, hf):
        for c in range(D_FF // FF_CHUNK):
            cs = slice(c * FF_CHUNK, (c + 1) * FF_CHUNK)
            gm_ref[t, :, cs] = (_silu(_dot(hf, w1_ref[:, cs])) * _dot(hf, w3_ref[:, cs])).astype(BF16)

    def down(t, xm):
        rows = slice(t * POST_SUB, (t + 1) * POST_SUB)
        r2 = RES_ALPHA * xm + (1.0 + mod_ref[0, 5:6, :]) * _dot(gm_ref[t], w2_ref[...])
        o_ref[0, rows, :] = _layer_norm(r2, g_ref[1:2, :], b_ref[1:2, :])

    n = x_ref.shape[1] // POST_SUB
    mixed = {}
    for step in range(n + 2):
        if step >= 2:
            down(step - 2, mixed[step - 2][0])
        if 1 <= step <= n:
            up(step - 1, mixed[step - 1][1])
        if step < n:
            mixed[step] = mix(step)


def _post(acts, w_outs, x, mod, w1, w3, w2, layer, ln_g, ln_b, tm):
    bsz, s, d = x.shape
    n_act = len(acts)
    const = lambda shape: pl.BlockSpec(shape, lambda b, i: (0,) * len(shape), pipeline_mode=pl.Buffered(1))
    of_layer = lambda shape: pl.BlockSpec((None,) + shape, lambda b, i: (layer, 0, 0), pipeline_mode=pl.Buffered(1))
    in_specs = [pl.BlockSpec((1, tm, a.shape[-1]), lambda b, i: (b, i, 0)) for a in acts]
    in_specs += [const(w.shape) for w in w_outs]
    in_specs += [
        pl.BlockSpec((1, tm, d), lambda b, i: (b, i, 0)),
        pl.BlockSpec((1, 6, d), lambda b, i: (b, 0, 0)),
        of_layer((d, D_FF)),
        of_layer((d, D_FF)),
        of_layer((D_FF, d)),
        const((2, d)),
        const((2, d)),
    ]
    return pl.pallas_call(
        functools.partial(_post_kernel, n_act=n_act),
        out_shape=jax.ShapeDtypeStruct((bsz, s, d), F32),
        grid=(bsz, s // tm),
        in_specs=in_specs,
        out_specs=pl.BlockSpec((1, tm, d), lambda b, i: (b, i, 0)),
        scratch_shapes=[pltpu.VMEM((tm // POST_SUB, POST_SUB, D_FF), BF16)],
        compiler_params=_params("parallel", "arbitrary"),
        name="post",
    )(*acts, *w_outs, x, mod, w1, w3, w2, ln_g, ln_b)


def _mixer0_kernel(p_ref, wa_ref, ba_ref, gg_ref, cw_ref, cb_ref, gb_ref, mg_ref, o_ref,
                   st_ref, cst_ref, m_ref, xc_ref):
    @pl.when(pl.program_id(1) == 0)
    def _():
        st_ref[...] = jnp.zeros_like(st_ref)
        cst_ref[...] = jnp.zeros_like(cst_ref)
        m_ref[...] = jnp.zeros_like(m_ref)
        xc_ref[:, 0:8, :] = jnp.zeros((xc_ref.shape[0], 8, 2 * ML_W), F32)

    L = MIX_TILE
    row = _iota((L, L), 0)
    col = _iota((L, L), 1)
    causal = col <= row
    same_chunk = (row >> _CHUNK_SHIFT) == (col >> _CHUNK_SHIFT)
    lane_qk = _iota((1, GLA_QK), 1)
    r4 = _iota((GLA_HEADS * GLA_CHUNK, GLA_CHUNK), 0)
    c4 = _iota((GLA_HEADS * GLA_CHUNK, GLA_CHUNK), 1)
    consts = dict(
        causal=causal,
        tri=jnp.where(causal, 1.0, 0.0).astype(BF16),
        tri_blk=jnp.where(same_chunk & causal, 1.0, 0.0).astype(BF16),
        head_masks=[(lane_qk >> _CHUNK_SHIFT) == h for h in range(GLA_HEADS)],
        tril4=c4 <= (r4 & (GLA_CHUNK - 1)),
    )
    for b in range(p_ref.shape[0]):
        _mixer0_row(consts, p_ref.at[pl.ds(b, 1)], wa_ref, ba_ref, gg_ref, cw_ref, cb_ref, gb_ref, mg_ref,
                    o_ref.at[pl.ds(b, 1)], st_ref.at[b], cst_ref.at[b], m_ref.at[b], xc_ref.at[b])


def _mixer0_row(consts, p_ref, wa_ref, ba_ref, gg_ref, cw_ref, cb_ref, gb_ref, mg_ref, o_ref,
                st_ref, cst_ref, m_ref, xc_ref):
    L = MIX_TILE
    causal, tri, tri_blk = consts["causal"], consts["tri"], consts["tri_blk"]
    head_masks, tril4 = consts["head_masks"], consts["tril4"]
    lane = _iota((1, LANES), 1)
    g_raw = p_ref[0, :, P0_G:P0_G + LANES]

    u = _dot(g_raw.astype(BF16), wa_ref[...]) + ba_ref[...]
    la = _log_sigmoid(u) * (1.0 / GLA_TAU)
    bcs = _dot_split(tri_blk, la, 3)
    q_in = p_ref[0, :, P0_QA:P0_QA + GLA_QK] * (GLA_DK ** -0.5) * jnp.exp(bcs)
    k_raw = p_ref[0, :, P0_KA:P0_KA + GLA_QK]
    k_in = k_raw * jnp.exp(-bcs)
    oa_chunks = []
    for c in range(L // GLA_CHUNK):
        r0, r1 = c * GLA_CHUNK, (c + 1) * GLA_CHUNK
        b_c = bcs[r0:r1]
        bl = b_c[GLA_CHUNK - 1:GLA_CHUNK, :]
        q_c = q_in[r0:r1]
        k_c = k_in[r0:r1].astype(BF16)
        k_end = k_raw[r0:r1] * jnp.exp(bl - b_c)
        v_c = p_ref[0, r0:r1, P0_VA:P0_VA + GLA_V]
        q_exp = jnp.concatenate([jnp.where(hm, q_c, 0.0) for hm in head_masks], axis=0).astype(BF16)
        k_exp = jnp.concatenate([jnp.where(hm, k_end, 0.0) for hm in head_masks], axis=0).astype(BF16)
        sc = jnp.where(tril4, _dot_nt(q_exp, k_c), 0.0)
        intra = _dot(sc.astype(BF16), v_c.astype(BF16))
        inter = _dot_nt(q_exp, st_ref[...].astype(BF16))
        o_heads = []
        for h in range(GLA_HEADS):
            h0, h1 = h * GLA_CHUNK, (h + 1) * GLA_CHUNK
            o_heads.append(intra[h0:h1, h * GLA_DV:(h + 1) * GLA_DV] + inter[h0:h1])
        oa_chunks.append(o_heads)
        v_cat = jnp.concatenate([v_c[:, h * GLA_DV:(h + 1) * GLA_DV] for h in range(GLA_HEADS)], axis=0)
        st_ref[...] = st_ref[...] * jnp.exp(bl) + _dot_tn(v_cat.astype(BF16), k_exp)
    for h in range(GLA_HEADS):
        o_h = jnp.concatenate([oc[h] for oc in oa_chunks], axis=0)
        cs = slice(h * GLA_DV, (h + 1) * GLA_DV)
        ra = p_ref[0, :, P0_RA + h * GLA_DV:P0_RA + (h + 1) * GLA_DV]
        o_ref[0, :, cs] = (_head_norm(o_h, gg_ref[:, cs]) * _silu(ra)).astype(o_ref.dtype)

    xc_ref[8:8 + L, :] = p_ref[0, :, P0_QKB:P0_QKB + 2 * ML_W]
    conv = cb_ref[...]
    for kk in range(MLSTM_CONV):
        conv = conv + cw_ref[kk:kk + 1, :] * xc_ref[8 - (MLSTM_CONV - 1) + kk:8 - (MLSTM_CONV - 1) + kk + L, :]
    xc_ref[0:8, :] = xc_ref[L:L + 8, :]
    qk_b = _silu(conv)

    gb = jnp.where(lane < G_A, g_raw + gb_ref[...], 0.0)
    ipre = gb
    logf = _log_sigmoid(pltpu.roll(gb, LANES - G_F, axis=1))
    logf = jnp.where(lane < MLSTM_HEADS, logf, 0.0)
    bcum = _dot_split(tri, logf, 3)
    blast = bcum[L - 1:L, :]
    wend = blast - bcum + ipre
    m_prev = m_ref[...]
    m_new = jnp.maximum(blast + m_prev, jnp.max(wend, axis=0, keepdims=True))
    scl = jnp.exp(blast + m_prev - m_new)
    wj = jnp.exp(wend - m_new)
    rows_src = jnp.where(lane < MLSTM_HEADS, ipre, pltpu.roll(bcum, MLSTM_HEADS, axis=1))
    rows_t = rows_src.T
    ones = jnp.ones((L, MLSTM_DH), F32)
    for h in range(MLSTM_HEADS):
        hs = slice(h * MLSTM_DH, (h + 1) * MLSTM_DH)
        wide = lambda t: jnp.concatenate([t, t], axis=1)
        b_rep = jnp.broadcast_to(bcum[:, h:h + 1], (L, LANES))
        wj_rep = jnp.broadcast_to(wj[:, h:h + 1], (L, LANES))
        ip_row = rows_t[h:h + 1, :]
        b_row = rows_t[MLSTM_HEADS + h:MLSTM_HEADS + h + 1, :]
        dlog = jnp.where(causal, wide(b_rep) - b_row + ip_row, NEG_INF)
        il = b_rep + m_prev[:, h:h + 1]
        m_i = jnp.maximum(il, jnp.broadcast_to(jnp.max(dlog, axis=1, keepdims=True), (L, LANES)))
        w_intra = jnp.exp(dlog - wide(m_i))
        s_inter = jnp.exp(il - m_i)
        q_h = qk_b[:, h * MLSTM_DH:(h + 1) * MLSTM_DH].astype(BF16)
        k_h = (qk_b[:, ML_W + h * MLSTM_DH:ML_W + (h + 1) * MLSTM_DH] * (MLSTM_DH ** -0.5)).astype(BF16)
        v_h = p_ref[0, :, P0_VB + h * MLSTM_DH:P0_VB + (h + 1) * MLSTM_DH]
        v_aug = jnp.concatenate([v_h, ones], axis=1)
        a = (w_intra * _dot_nt(q_h, k_h)).astype(BF16)
        c_prev = cst_ref[h]
        tot = _dot(a, v_aug.astype(BF16)) + wide(s_inter) * _dot(q_h, c_prev.astype(BF16))
        num = tot[:, :MLSTM_DH]
        den = tot[:, MLSTM_DH:]
        hid = num / jnp.maximum(jnp.abs(den), jnp.exp(-m_i))
        w_aug = (v_aug * wide(wj_rep)).astype(BF16)
        cst_ref[h] = scl[:, h:h + 1] * c_prev + _dot_tn(k_h, w_aug)
        ob = p_ref[0, :, P0_OB + h * MLSTM_DH:P0_OB + (h + 1) * MLSTM_DH]
        o_ref[0, :, GLA_V + h * MLSTM_DH:GLA_V + (h + 1) * MLSTM_DH] = (
            jax.nn.sigmoid(ob) * _head_norm(hid, mg_ref[:, hs])).astype(o_ref.dtype)
    m_ref[...] = m_new


def _mixer0(proj, wa_up, ba, gla_g, conv_w, conv_b, b_i, b_f, ml_g):
    bsz, s, n = proj.shape
    L = MIX_TILE
    nr = MIX_ROWS if bsz % MIX_ROWS == 0 else 1
    wa_pad = jnp.zeros((LANES, GLA_QK), F32).at[G_A:G_A + GLA_GATE_RANK].set(wa_up).astype(BF16)
    gbias = jnp.zeros((1, LANES), F32).at[0, G_I:G_I + MLSTM_HEADS].set(b_i).at[0, G_F:G_F + MLSTM_HEADS].set(b_f)
    full = lambda shape: pl.BlockSpec(shape, lambda b, i: (0,) * len(shape))
    return pl.pallas_call(
        _mixer0_kernel,
        out_shape=jax.ShapeDtypeStruct((bsz, s, GLA_V + ML_W), BF16),
        grid=(bsz // nr, s // L),
        in_specs=[
            pl.BlockSpec((nr, L, n), lambda b, i: (b, i, 0)),
            full((LANES, GLA_QK)),
            full((1, GLA_QK)),
            full((1, GLA_V)),
            full((MLSTM_CONV, 2 * ML_W)),
            full((1, 2 * ML_W)),
            full((1, LANES)),
            full((1, ML_W)),
        ],
        out_specs=pl.BlockSpec((nr, L, GLA_V + ML_W), lambda b, i: (b, i, 0)),
        scratch_shapes=[
            pltpu.VMEM((nr, GLA_DV, GLA_QK), F32),
            pltpu.VMEM((nr, MLSTM_HEADS, MLSTM_DH, 2 * MLSTM_DH), F32),
            pltpu.VMEM((nr, 1, LANES), F32),
            pltpu.VMEM((nr, 8 + L, 2 * ML_W), F32),
        ],
        compiler_params=_params("parallel", "arbitrary"),
        name="mixer0",
    )(proj, wa_pad, ba.reshape(1, GLA_QK), gla_g.reshape(1, GLA_V), conv_w, conv_b.reshape(1, 2 * ML_W),
      gbias, ml_g.reshape(1, ML_W))


def _sb_kernel(q_ref, k_ref, v_ref, o_ref, qs_ref, acc_ref, car_ref, mar_ref, kn_ref):
    TQ, SUB, WIDE = q_ref.shape[1], SB_SUB, SB_WIDE
    ns = TQ // SUB
    RB = 2 * SUB
    i = pl.program_id(2)
    lane = _iota((1, LANES), 1)
    head0 = (lane >> _HEAD_SHIFT) == 0

    @pl.when(i == 0)
    def _():
        k2 = jnp.square(k_ref[0].astype(F32))
        for h in range(2):
            n2 = jnp.sum(jnp.where(head0 if h == 0 else ~head0, k2, 0.0), axis=1, keepdims=True)
            kn_ref[h:h + 1, :] = jnp.broadcast_to(jnp.sqrt(jnp.max(n2, axis=0, keepdims=True)), (1, LANES))

    for a in range(ns):
        qa = q_ref[0, a * SUB:(a + 1) * SUB, :].astype(F32) * (SB_DH ** -0.5 * LOG2E)
        qsq = qa * qa
        for h in range(2):
            rows = slice(a * RB + h * SUB, a * RB + (h + 1) * SUB)
            hmask = head0 if h == 0 else ~head0
            qs_ref[rows, :] = jnp.where(hmask, qa, 0.0).astype(BF16)
            n2 = jnp.sum(jnp.where(hmask, qsq, 0.0), axis=1, keepdims=True)
            qn = n2 * lax.rsqrt(n2 + 1e-30)
            mar_ref[rows, :] = qn * kn_ref[h:h + 1, :] * SB_BOUND_SLACK + SB_ZERO_BITS
    tri = jnp.where(_iota((WIDE, WIDE), 0) >= _iota((WIDE, WIDE), 1), -1.0, 0.0).astype(BF16)

    def key_rows(ku, nk):
        r0 = ku * SUB
        return pl.ds(r0 if isinstance(r0, int) else pl.multiple_of(r0, SUB), nk)

    def logits(a, ku, nk):
        return _dot_nt(qs_ref[a * RB:(a + 1) * RB, :], k_ref[0, key_rows(ku, nk), :])

    def softplus2(z, diagonal):
        sp = jnp.maximum(z, 0.0) + jnp.log(1.0 + jnp.exp2(-jnp.abs(z))) * LOG2E
        if not diagonal:
            return sp, None
        mask = _iota(z.shape, 1) < (_iota(z.shape, 0) & (SUB - 1))
        return jnp.where(mask, sp, 0.0), mask

    def suffix(sp):
        nk = sp.shape[1]
        return _dot(sp.astype(BF16), tri[:nk, :nk]), jnp.sum(sp, axis=1, keepdims=True)

    def weights(z, res, car, mask):
        att = jnp.exp2(z + res if car is None else z + res - car)
        if mask is not None:
            att = jnp.where(mask, att, 0.0)
        return att.astype(BF16)

    def far_tile(a, ku):
        rows = slice(a * RB, (a + 1) * RB)
        z = logits(a, ku, SUB)
        sp, _ = softplus2(z, False)
        res, tot = suffix(sp)
        car = car_ref[rows, :]
        acc_ref[rows, :] += _dot(weights(z, res, car, None), v_ref[0, key_rows(ku, SUB), :])
        car_ref[rows, :] = car + tot

    def near_tiles(first_block):
        chains = []
        for a in range(ns):
            g = a if first_block else ns * i + a
            chain = [(a, g, SUB, True)]
            if not first_block or a >= 2:
                chain.append((a, g - 2, WIDE, False))
            elif a == 1:
                chain.append((a, 0, SUB, False))
            chains.append(chain)
        jobs = [job for step in itertools.zip_longest(*chains) for job in step if job is not None]
        zs = [logits(a, ku, nk) for a, ku, nk, _ in jobs]
        sps = [softplus2(z, diagonal) for z, (_, _, _, diagonal) in zip(zs, jobs)]
        sufs = [suffix(sp) for sp, _ in sps]
        car = [None] * ns
        atts = []
        for z, (sp, mask), (res, tot), (a, _, _, _) in zip(zs, sps, sufs, jobs):
            atts.append(weights(z, res, car[a], mask))
            car[a] = tot if car[a] is None else car[a] + tot
        acc = [None] * ns
        for att, (a, ku, nk, _) in zip(atts, jobs):
            t = _dot(att, v_ref[0, key_rows(ku, nk), :])
            acc[a] = t if acc[a] is None else acc[a] + t
        for a in range(ns):
            acc_ref[a * RB:(a + 1) * RB, :] = acc[a]
            car_ref[a * RB:(a + 1) * RB, :] = jnp.broadcast_to(car[a], (RB, LANES))

    pl.when(i == 0)(functools.partial(near_tiles, True))
    pl.when(i > 0)(functools.partial(near_tiles, False))

    def next_unit(a, t):
        return ns * i + a - 3 - t

    def pending(t):
        gap = [jnp.where(next_unit(a, t) >= 0, mar_ref[a * RB:(a + 1) * RB, :] - car_ref[a * RB:(a + 1) * RB, :], -1.0)
               for a in range(ns)]
        return jnp.max(functools.reduce(jnp.maximum, gap)) > 0.0

    def far_body(c):
        t, _ = c
        for a in range(ns):
            ku = next_unit(a, t)
            pl.when(ku >= 0)(functools.partial(far_tile, a, ku))
        return t + 1, pending(t + 1)

    lax.while_loop(lambda c: c[1], far_body, (0, pending(0)))

    for a in range(ns):
        o_ref[0, a * SUB:(a + 1) * SUB, :] = jnp.where(
            head0, acc_ref[a * RB:a * RB + SUB, :], acc_ref[a * RB + SUB:(a + 1) * RB, :]).astype(o_ref.dtype)


def _stick_breaking(pc):
    bsz, s, _ = pc.shape
    npair = SB_W // LANES
    tq = min(SB_TQ, s)
    return pl.pallas_call(
        _sb_kernel,
        out_shape=jax.ShapeDtypeStruct((bsz, s, SB_W), BF16),
        grid=(bsz, npair, s // tq),
        in_specs=[
            pl.BlockSpec((1, tq, LANES), lambda b, p, i: (b, i, p)),
            pl.BlockSpec((1, s, LANES), lambda b, p, i: (b, 0, npair + p)),
            pl.BlockSpec((1, s, LANES), lambda b, p, i: (b, 0, 2 * npair + p)),
        ],
        out_specs=pl.BlockSpec((1, tq, LANES), lambda b, p, i: (b, i, p)),
        scratch_shapes=[
            pltpu.VMEM((2 * tq, LANES), BF16),
            pltpu.VMEM((2 * tq, LANES), F32),
            pltpu.VMEM((2 * tq, LANES), F32),
            pltpu.VMEM((2 * tq, LANES), F32),
            pltpu.VMEM((8, LANES), F32),
        ],
        compiler_params=_params("parallel", "parallel", "arbitrary"),
        name="stick_breaking",
    )(pc, pc, pc)


def _t5_bucket(dist):
    max_exact = N_REL_BUCKETS // 2
    dd = np.maximum(dist, 1).astype(np.float64)
    large = max_exact + (np.log(dd / max_exact) / np.log(REL_MAX_DIST / max_exact)
                         * (N_REL_BUCKETS - max_exact)).astype(np.int32)
    large = np.minimum(large, N_REL_BUCKETS - 1)
    return np.where(dist < max_exact, dist, large).astype(np.int32)


def _dil_buckets():
    qi = np.arange(DIL_BLK)[:, None]
    kj = np.arange(2 * DIL_BLK)[None, :]
    delta = qi - kj + DIL_BLK
    in_win = (delta >= 0) & (delta <= DIL_BLK)
    tabs = []
    for window, dil in DIL_PAIRS:
        assert window // dil == DIL_BLK
        bucket = _t5_bucket(np.clip(delta, 0, None) * dil)
        tabs.append(np.where(in_win, bucket, -1).astype(np.int32))
    return np.stack(tabs, 0)


def _dil_kernel(q_ref, k_ref, v_ref, bkt_ref, bkt0_ref, tab_ref, o_ref, m_sc, l_sc, acc_sc, bias0_sc, bias_sc):
    s_len = q_ref.shape[1]
    first_batch = pl.program_id(0) == 0
    hp = pl.program_id(1)
    g = pl.program_id(2)
    blk = DIL_BLK
    ng = len(DIL_PAIRS)
    lane = _iota((1, LANES), 1)
    lane_head = lane >> _HEAD_SHIFT
    sel0 = lane_head == 0
    kcol = _iota((2 * blk, 2 * blk), 1)

    def head_biases(gi, bkt, dst):
        @pl.when(first_batch)
        def _():
            n = bkt.shape[0]
            for hl in range(2):
                head = gi * DIL_HEADS_PER_GROUP + hp * 2 + hl
                bias = jnp.full(bkt.shape, NEG_INF, F32)
                for bk in range(N_REL_BUCKETS):
                    bias = jnp.where(bkt == bk, tab_ref[bk, head], bias)
                dst[hl * n:(hl + 1) * n, :] = bias
        return dst[...]

    def stack_heads(qb):
        return jnp.concatenate([jnp.where(sel0, qb, 0.0), jnp.where(sel0, 0.0, qb)], axis=0).astype(BF16)

    def with_ones(vv):
        return jnp.concatenate([vv, jnp.ones_like(vv)], axis=1).astype(BF16)

    def first_group(gi, dil):
        n_sub = s_len // dil
        bias2 = head_biases(gi, bkt0_ref[...], bias0_sc.at[hp])

        def residue(r):
            rows = pl.ds(r, n_sub, stride=dil)
            logits = _dot_nt(stack_heads(q_ref[0, rows, :] * (DIL_DH ** -0.5)), k_ref[0, rows, :].astype(BF16)) + bias2
            return rows, logits

        def body(tt, carry):
            s1 = [residue(tt * DIL_UNROLL + u) for u in range(DIL_UNROLL)]
            s2 = [jnp.max(lg, axis=1, keepdims=True) for _, lg in s1]
            s3 = [_dot(jnp.exp(lg - mc).astype(BF16), with_ones(v_ref[0, rows, :])) for (rows, lg), mc in zip(s1, s2)]
            for (rows, _), mc, pv in zip(s1, s2, s3):
                m_sc[rows, :] = jnp.where(sel0, mc[0:n_sub], mc[n_sub:])
                l_sc[rows, :] = jnp.where(sel0, pv[0:n_sub, LANES:], pv[n_sub:, LANES:])
                acc_sc[rows, :] = jnp.where(sel0, pv[0:n_sub, :LANES], pv[n_sub:, :LANES])
            return carry

        lax.fori_loop(0, dil // DIL_UNROLL, body, 0)

    def group(gi, dil):
        nb = s_len // (blk * dil)
        bias2 = head_biases(gi, bkt_ref[gi], bias_sc.at[hp, gi])

        def stage_logits(t):
            r = t // nb
            n = t % nb
            q_start = r + dil * blk * n
            p_start = r + dil * blk * jnp.maximum(n - 1, 0)
            rows_q = pl.ds(q_start, blk, stride=dil) if dil > 1 else pl.ds(q_start, blk)
            rows_p = pl.ds(p_start, blk, stride=dil) if dil > 1 else pl.ds(p_start, blk)
            q2 = stack_heads(q_ref[0, rows_q, :] * (DIL_DH ** -0.5))
            kk = jnp.concatenate([k_ref[0, rows_p, :], k_ref[0, rows_q, :]], axis=0).astype(BF16)
            logits = _dot_nt(q2, kk) + bias2
            logits = jnp.where(jnp.logical_and(n == 0, kcol < blk), NEG_INF, logits)
            return rows_q, rows_p, logits

        def stage_pv(rows_q, rows_p, logits, m_col):
            p = jnp.exp(logits - m_col)
            vv = with_ones(jnp.concatenate([v_ref[0, rows_p, :], v_ref[0, rows_q, :]], axis=0))
            pv = _dot(p.astype(BF16), vv)
            return pv[:, LANES:], pv[:, :LANES]

        def body(tt, carry):
            ts = [tt * DIL_UNROLL + u for u in range(DIL_UNROLL)]
            s1 = [stage_logits(t) for t in ts]
            s2 = [jnp.max(lg, axis=1, keepdims=True) for _, _, lg in s1]
            s3 = [stage_pv(rq, rp, lg, mc) for (rq, rp, lg), mc in zip(s1, s2)]
            outs = []
            for (rows_q, _, _), m_col, (ps, pv) in zip(s1, s2, s3):
                m_old = m_sc[rows_q, :]
                m_blk = jnp.where(sel0, m_col[0:blk], m_col[blk:])
                m_new = jnp.maximum(m_old, m_blk)
                a_old = jnp.exp(m_old - m_new)
                a_blk = jnp.exp(m_blk - m_new)
                l_new = a_old * l_sc[rows_q, :] + a_blk * jnp.where(sel0, ps[0:blk], ps[blk:])
                a_new = a_old * acc_sc[rows_q, :] + a_blk * jnp.where(sel0, pv[0:blk], pv[blk:])
                outs.append((rows_q, m_new, l_new, a_new))
            for rows_q, m_new, l_new, a_new in outs:
                m_sc[rows_q, :] = m_new
                l_sc[rows_q, :] = l_new
                acc_sc[rows_q, :] = a_new
            return carry

        for tt in range(s_len // (blk * DIL_UNROLL)):
            body(tt, 0)

    for gi, (_, dil) in enumerate(DIL_PAIRS):
        pl.when(g == ng - 1 - gi)(functools.partial(first_group if gi == ng - 1 else group, gi, dil))

    @pl.when(g == ng - 1)
    def _():
        o_ref[0] = (acc_sc[...] / l_sc[...]).astype(o_ref.dtype)


def _dilated(pd, rel_bias):
    bsz, s, _ = pd.shape
    ng = len(DIL_PAIRS)
    gw = DIL_HEADS_PER_GROUP * DIL_DH // LANES
    nq = DIL_W // LANES
    bkt = jnp.asarray(_dil_buckets())
    dil0 = DIL_PAIRS[-1][1]
    n_sub = s // dil0
    assert s % (DIL_BLK * dil0) == 0 and dil0 % DIL_UNROLL == 0
    delta = np.arange(n_sub)[:, None] - np.arange(n_sub)[None, :]
    bkt0 = jnp.asarray(np.where((delta >= 0) & (delta <= DIL_BLK), _t5_bucket(np.clip(delta, 0, None) * dil0), -1)
                       .astype(np.int32))
    blk_spec = lambda base: pl.BlockSpec((1, s, LANES), lambda b, p, g: (b, 0, base + (ng - 1 - g) * gw + p))
    return pl.pallas_call(
        _dil_kernel,
        out_shape=jax.ShapeDtypeStruct((bsz, s, DIL_OUT), BF16),
        grid=(bsz, gw, ng),
        in_specs=[
            blk_spec(0),
            blk_spec(nq),
            blk_spec(2 * nq),
            pl.BlockSpec((ng, DIL_BLK, 2 * DIL_BLK), lambda b, p, g: (0, 0, 0)),
            pl.BlockSpec((n_sub, n_sub), lambda b, p, g: (0, 0)),
            pl.BlockSpec(memory_space=pltpu.SMEM),
        ],
        out_specs=pl.BlockSpec((1, s, LANES), lambda b, p, g: (b, 0, p)),
        scratch_shapes=[pltpu.VMEM((s, LANES), F32)] * 3 + [
            pltpu.VMEM((gw, 2 * n_sub, n_sub), F32),
            pltpu.VMEM((gw, ng - 1, 2 * DIL_BLK, 2 * DIL_BLK), F32),
        ],
        compiler_params=_params("arbitrary", "arbitrary", "arbitrary"),
        name="dilated",
    )(pd, pd, pd, bkt, bkt0, rel_bias)


def _ab_weight(w):
    o = np.cumsum((0,) + (GLA_QK, GLA_QK, GLA_V, GLA_V, GLA_GATE_RANK, 2 * ML_W, ML_W, MLSTM_HEADS, MLSTM_HEADS, ML_W))
    qa, ka, va, ra, aa, qkb, vb, ib, fb, ob = [w[:, o[j]:o[j + 1]] for j in range(10)]
    pad = jnp.zeros((w.shape[0], LANES - 2 * MLSTM_HEADS - GLA_GATE_RANK), w.dtype)
    return jnp.concatenate([qa, ka, va, ra, qkb, vb, ob, ib, fb, aa, pad], axis=1).astype(BF16)


def kernel(x, c, ada_w, ada_b, ln_g, ln_b, ab_w_in, gla_wa_up, gla_ba, gla_norm_g, ml_conv_w, ml_conv_b,
           ml_b_i, ml_b_f, ml_norm_g, ab_w_out, cd_w_in, rel_bias, cd_w_out, ffn_w1, ffn_w3, ffn_w2):
    bsz, s, d = x.shape
    mod_all = _ada_mod(c, ada_w, ada_b).reshape(DEPTH, bsz, 6, d)
    w1, w3, w2 = ffn_w1.astype(BF16), ffn_w3.astype(BF16), ffn_w2.astype(BF16)
    tm = IN_TM
    for layer in range(DEPTH):
        mod = mod_all[layer]
        j = layer // 2
        if layer % 2 == 0:
            (proj,) = _inproj(x, mod, _ab_weight(ab_w_in[j]), [(P0_N, F32)], tm, 0, 1, "inproj0")
            cat = _mixer0(proj, gla_wa_up[j], gla_ba[j], gla_norm_g[j], ml_conv_w[j], ml_conv_b[j],
                          ml_b_i[j], ml_b_f[j], ml_norm_g[j])
            acts, w_outs = [cat], [ab_w_out[j].astype(BF16)]
        else:
            pc, pd = _inproj(x, mod, cd_w_in[j].astype(BF16), [(3 * SB_W, BF16), (3 * DIL_W, F32)],
                             tm, 0, 1, "inproj1")
            oc = _stick_breaking(pc)
            od = _dilated(pd, rel_bias)
            w_out = cd_w_out[j].astype(BF16)
            acts, w_outs = [oc, od], [w_out[:SB_W], w_out[SB_W:]]
        x = _post(acts, w_outs, x, mod, w1, w3, w2, layer, ln_g[layer], ln_b[layer], FFN_TM)
    return x
```

```python
import functools
import itertools

import numpy as np
import jax
import jax.numpy as jnp
from jax import lax
from jax.experimental import pallas as pl
from jax.experimental.pallas import tpu as pltpu

F32 = jnp.float32
BF16 = jnp.bfloat16

D_MODEL = 1024
DEPTH = 2
GLA_HEADS = 4
GLA_DK = 64
GLA_DV = 128
GLA_GATE_RANK = 16
GLA_TAU = 16.0
GLA_CHUNK = 64
MLSTM_HEADS = 4
MLSTM_DH = 128
MLSTM_CONV = 4
SB_HEADS = 8
SB_DH = 64
DIL_PAIRS = ((128, 1), (512, 4), (2048, 16))
DIL_HEADS_PER_GROUP = 4
DIL_DH = 64
DIL_BLK = 128
DIL_UNROLL = 4
DIL_FIRST_UNROLL = 2
N_REL_BUCKETS = 32
REL_MAX_DIST = 2048
D_FF = ((8 * D_MODEL + 3 * 256 - 1) // (3 * 256)) * 256
LN_EPS = 1e-5
RES_ALPHA = (2 * DEPTH) ** 0.25
NEG_INF = -1e30

GLA_QK = GLA_HEADS * GLA_DK
GLA_V = GLA_HEADS * GLA_DV
ML_W = MLSTM_HEADS * MLSTM_DH
SB_W = SB_HEADS * SB_DH
DIL_W = len(DIL_PAIRS) * DIL_HEADS_PER_GROUP * DIL_DH
DIL_OUT = DIL_HEADS_PER_GROUP * DIL_DH

LANES = 128
VMEM_LIMIT = 56 * 1024 * 1024

P0_QA = 0
P0_KA = P0_QA + GLA_QK
P0_VA = P0_KA + GLA_QK
P0_RA = P0_VA + GLA_V
P0_QKB = P0_RA + GLA_V
P0_VB = P0_QKB + 2 * ML_W
P0_OB = P0_VB + ML_W
P0_G = P0_OB + ML_W
P0_N = P0_G + LANES
G_I = 0
G_F = MLSTM_HEADS
G_A = 2 * MLSTM_HEADS

MIX_TILE = 256
MIX_ROWS = 4
FF_CHUNK = 256
IN_TM = 512
FFN_TM = 1024
POST_SUB = 512
SB_TQ = 4096
SB_SUB = 128
SB_WIDE = 256
SB_ZERO_BITS = 160.0
SB_BOUND_SLACK = 1.01
LOG2E = 1.4426950408889634
_CHUNK_SHIFT = 6
_HEAD_SHIFT = 6
assert GLA_CHUNK == GLA_DK == 1 << _CHUNK_SHIFT and SB_DH == DIL_DH == 1 << _HEAD_SHIFT


def _dot(a, b):
    return jnp.dot(a, b, preferred_element_type=F32)


def _dot_nt(a, b):
    return lax.dot_general(a, b, (((1,), (1,)), ((), ())), preferred_element_type=F32)


def _dot_tn(a, b):
    return lax.dot_general(a, b, (((0,), (0,)), ((), ())), preferred_element_type=F32)


def _dot_split(t, x, terms):
    acc = None
    rem = x
    for i in range(terms):
        part = rem.astype(BF16)
        d = _dot(t, part)
        acc = d if acc is None else acc + d
        if i + 1 < terms:
            rem = rem - part.astype(F32)
    return acc


def _log_sigmoid(x):
    return jnp.minimum(x, 0.0) - jnp.log(1.0 + jnp.exp(-jnp.abs(x)))


def _silu(x):
    return x * jax.nn.sigmoid(x)


def _layer_norm(r, g, b):
    mu = jnp.mean(r, axis=-1, keepdims=True)
    d = r - mu
    var = jnp.mean(d * d, axis=-1, keepdims=True)
    return d * lax.rsqrt(var + LN_EPS) * g + b


def _head_norm(x, g):
    mu = jnp.mean(x, axis=-1, keepdims=True)
    d = x - mu
    var = jnp.mean(d * d, axis=-1, keepdims=True)
    return d * lax.rsqrt(var + LN_EPS) * g


def _iota(shape, dim):
    return lax.broadcasted_iota(jnp.int32, shape, dim)


def _params(*sem):
    return pltpu.CompilerParams(dimension_semantics=sem, vmem_limit_bytes=VMEM_LIMIT)


def _ada_kernel(c_ref, w_ref, b_ref, o_ref):
    ca = _silu(c_ref[...]).astype(BF16)
    o_ref[0] = _dot(ca, w_ref[0].astype(BF16)) + b_ref[0]


def _ada_mod(c, ada_w, ada_b):
    bsz, d = c.shape
    n = ada_w.shape[-1]
    tn = n // 4
    return pl.pallas_call(
        _ada_kernel,
        out_shape=jax.ShapeDtypeStruct((DEPTH, bsz, n), F32),
        grid=(DEPTH, n // tn),
        in_specs=[
            pl.BlockSpec((bsz, d), lambda l, j: (0, 0)),
            pl.BlockSpec((1, d, tn), lambda l, j: (l, 0, j)),
            pl.BlockSpec((1, 1, tn), lambda l, j: (l, 0, j)),
        ],
        out_specs=pl.BlockSpec((1, bsz, tn), lambda l, j: (l, 0, j)),
        compiler_params=_params("arbitrary", "arbitrary"),
        name="ada_mod",
    )(c, ada_w, ada_b.reshape(DEPTH, 1, n))


def _inproj_kernel(x_ref, mod_ref, w_ref, *o_refs, shift_row, scale_row, gates):
    sh = mod_ref[0, shift_row:shift_row + 1, :]
    sc = mod_ref[0, scale_row:scale_row + 1, :]
    hm = (x_ref[0] * (1.0 + sc) + sh).astype(BF16)
    col = 0
    for o_ref in o_refs:
        n = o_ref.shape[-1]
        cuts = sorted({0, n} | {c for lo, hi, _ in gates if col <= lo and hi <= col + n for c in (lo - col, hi - col)})
        for lo, hi in zip(cuts[:-1], cuts[1:]):
            y = _dot(hm, w_ref[:, col + lo:col + hi])
            for glo, ghi, fn in gates:
                if glo == col + lo and ghi == col + hi:
                    y = fn(y)
            o_ref[0, :, lo:hi] = y.astype(o_ref.dtype)
        col += n


def _inproj(x, mod, w, outs, tm, shift_row, scale_row, name, gates=()):
    bsz, s, d = x.shape
    n = w.shape[1]
    assert sum(o[0] for o in outs) == n
    return pl.pallas_call(
        functools.partial(_inproj_kernel, shift_row=shift_row, scale_row=scale_row, gates=tuple(gates)),
        out_shape=[jax.ShapeDtypeStruct((bsz, s, o[0]), o[1]) for o in outs],
        grid=(bsz, s // tm),
        in_specs=[
            pl.BlockSpec((1, tm, d), lambda b, i: (b, i, 0)),
            pl.BlockSpec((1, 6, d), lambda b, i: (b, 0, 0)),
            pl.BlockSpec((d, n), lambda b, i: (0, 0)),
        ],
        out_specs=[pl.BlockSpec((1, tm, o[0]), lambda b, i: (b, i, 0)) for o in outs],
        compiler_params=_params("parallel", "arbitrary"),
        name=name,
    )(x, mod, w)


def _post_kernel(*refs, n_act):
    act_refs = refs[:n_act]
    wo_refs = refs[n_act:2 * n_act]
    x_ref, mod_ref, w1_ref, w3_ref, w2_ref, g_ref, b_ref, o_ref, gm_ref = refs[2 * n_act:]

    def mix(t):
        rows = slice(t * POST_SUB, (t + 1) * POST_SUB)
        y = None
        for a_ref, w_ref in zip(act_refs, wo_refs):
            part = _dot(a_ref[0, rows, :], w_ref[...])
            y = part if y is None else y + part
        r = RES_ALPHA * x_ref[0, rows, :] + (1.0 + mod_ref[0, 2:3, :]) * y
        xm = _layer_norm(r, g_ref[0:1, :], b_ref[0:1, :])
        return xm, (xm * (1.0 + mod_ref[0, 4:5, :]) + mod_ref[0, 3:4, :]).astype(BF16)

    def up(t, hf):
        for c in range(D_FF // FF_CHUNK):
            cs = slice(c * FF_CHUNK, (c + 1) * FF_CHUNK)
            gm_ref[t, :, cs] = (_silu(_dot(hf, w1_ref[:, cs])) * _dot(hf, w3_ref[:, cs])).astype(BF16)

    def down(t, xm):
        rows = slice(t * POST_SUB, (t + 1) * POST_SUB)
        r2 = RES_ALPHA * xm + (1.0 + mod_ref[0, 5:6, :]) * _dot(gm_ref[t], w2_ref[...])
        o_ref[0, rows, :] = _layer_norm(r2, g_ref[1:2, :], b_ref[1:2, :])

    n = x_ref.shape[1] // POST_SUB
    mixed = {}
    for step in range(n + 2):
        if step >= 2:
            down(step - 2, mixed[step - 2][0])
        if 1 <= step <= n:
            up(step - 1, mixed[step - 1][1])
        if step < n:
            mixed[step] = mix(step)


def _post(acts, w_outs, x, mod, w1, w3, w2, layer, ln_g, ln_b, tm):
    bsz, s, d = x.shape
    n_act = len(acts)
    const = lambda shape: pl.BlockSpec(shape, lambda b, i: (0,) * len(shape), pipeline_mode=pl.Buffered(1))
    of_layer = lambda shape: pl.BlockSpec((None,) + shape, lambda b, i: (layer, 0, 0), pipeline_mode=pl.Buffered(1))
    in_specs = [pl.BlockSpec((1, tm, a.shape[-1]), lambda b, i: (b, i, 0)) for a in acts]
    in_specs += [const(w.shape) for w in w_outs]
    in_specs += [
        pl.BlockSpec((1, tm, d), lambda b, i: (b, i, 0)),
        pl.BlockSpec((1, 6, d), lambda b, i: (b, 0, 0)),
        of_layer((d, D_FF)),
        of_layer((d, D_FF)),
        of_layer((D_FF, d)),
        const((2, d)),
        const((2, d)),
    ]
    return pl.pallas_call(
        functools.partial(_post_kernel, n_act=n_act),
        out_shape=jax.ShapeDtypeStruct((bsz, s, d), F32),
        grid=(bsz, s // tm),
        in_specs=in_specs,
        out_specs=pl.BlockSpec((1, tm, d), lambda b, i: (b, i, 0)),
        scratch_shapes=[pltpu.VMEM((tm // POST_SUB, POST_SUB, D_FF), BF16)],
        compiler_params=_params("parallel", "arbitrary"),
        name="post",
    )(*acts, *w_outs, x, mod, w1, w3, w2, ln_g, ln_b)


def _mixer0_kernel(p_ref, wa_ref, ba_ref, gg_ref, cw_ref, cb_ref, gb_ref, mg_ref, o_ref,
                   st_ref, cst_ref, m_ref, xc_ref):
    @pl.when(pl.program_id(1) == 0)
    def _():
        st_ref[...] = jnp.zeros_like(st_ref)
        cst_ref[...] = jnp.zeros_like(cst_ref)
        m_ref[...] = jnp.zeros_like(m_ref)
        xc_ref[:, 0:8, :] = jnp.zeros((xc_ref.shape[0], 8, 2 * ML_W), F32)

    L = MIX_TILE
    row = _iota((L, L), 0)
    col = _iota((L, L), 1)
    causal = col <= row
    same_chunk = (row >> _CHUNK_SHIFT) == (col >> _CHUNK_SHIFT)
    lane_qk = _iota((1, GLA_QK), 1)
    r4 = _iota((GLA_HEADS * GLA_CHUNK, GLA_CHUNK), 0)
    c4 = _iota((GLA_HEADS * GLA_CHUNK, GLA_CHUNK), 1)
    consts = dict(
        causal=causal,
        tri=jnp.where(causal, 1.0, 0.0).astype(BF16),
        tri_blk=jnp.where(same_chunk & causal, 1.0, 0.0).astype(BF16),
        head_masks=[(lane_qk >> _CHUNK_SHIFT) == h for h in range(GLA_HEADS)],
        tril4=c4 <= (r4 & (GLA_CHUNK - 1)),
    )
    for b in range(p_ref.shape[0]):
        _mixer0_row(consts, p_ref.at[pl.ds(b, 1)], wa_ref, ba_ref, gg_ref, cw_ref, cb_ref, gb_ref, mg_ref,
                    o_ref.at[pl.ds(b, 1)], st_ref.at[b], cst_ref.at[b], m_ref.at[b], xc_ref.at[b])


def _mixer0_row(consts, p_ref, wa_ref, ba_ref, gg_ref, cw_ref, cb_ref, gb_ref, mg_ref, o_ref,
                st_ref, cst_ref, m_ref, xc_ref):
    L = MIX_TILE
    causal, tri, tri_blk = consts["causal"], consts["tri"], consts["tri_blk"]
    head_masks, tril4 = consts["head_masks"], consts["tril4"]
    lane = _iota((1, LANES), 1)
    g_raw = p_ref[0, :, P0_G:P0_G + LANES]

    u = _dot(g_raw.astype(BF16), wa_ref[...]) + ba_ref[...]
    la = _log_sigmoid(u) * (1.0 / GLA_TAU)
    bcs = _dot_split(tri_blk, la, 3)
    q_in = p_ref[0, :, P0_QA:P0_QA + GLA_QK] * (GLA_DK ** -0.5) * jnp.exp(bcs)
    k_raw = p_ref[0, :, P0_KA:P0_KA + GLA_QK]
    k_in = k_raw * jnp.exp(-bcs)
    oa_chunks = []
    for c in range(L // GLA_CHUNK):
        r0, r1 = c * GLA_CHUNK, (c + 1) * GLA_CHUNK
        b_c = bcs[r0:r1]
        bl = b_c[GLA_CHUNK - 1:GLA_CHUNK, :]
        q_c = q_in[r0:r1]
        k_c = k_in[r0:r1].astype(BF16)
        k_end = k_raw[r0:r1] * jnp.exp(bl - b_c)
        v_c = p_ref[0, r0:r1, P0_VA:P0_VA + GLA_V]
        q_exp = jnp.concatenate([jnp.where(hm, q_c, 0.0) for hm in head_masks], axis=0).astype(BF16)
        k_exp = jnp.concatenate([jnp.where(hm, k_end, 0.0) for hm in head_masks], axis=0).astype(BF16)
        sc = jnp.where(tril4, _dot_nt(q_exp, k_c), 0.0)
        intra = _dot(sc.astype(BF16), v_c.astype(BF16))
        inter = _dot_nt(q_exp, st_ref[...].astype(BF16))
        o_heads = []
        for h in range(GLA_HEADS):
            h0, h1 = h * GLA_CHUNK, (h + 1) * GLA_CHUNK
            o_heads.append(intra[h0:h1, h * GLA_DV:(h + 1) * GLA_DV] + inter[h0:h1])
        oa_chunks.append(o_heads)
        v_cat = jnp.concatenate([v_c[:, h * GLA_DV:(h + 1) * GLA_DV] for h in range(GLA_HEADS)], axis=0)
        st_ref[...] = st_ref[...] * jnp.exp(bl) + _dot_tn(v_cat.astype(BF16), k_exp)
    for h in range(GLA_HEADS):
        o_h = jnp.concatenate([oc[h] for oc in oa_chunks], axis=0)
        cs = slice(h * GLA_DV, (h + 1) * GLA_DV)
        ra = p_ref[0, :, P0_RA + h * GLA_DV:P0_RA + (h + 1) * GLA_DV]
        o_ref[0, :, cs] = (_head_norm(o_h, gg_ref[:, cs]) * ra).astype(o_ref.dtype)

    xc_ref[8:8 + L, :] = p_ref[0, :, P0_QKB:P0_QKB + 2 * ML_W]
    conv = cb_ref[...]
    for kk in range(MLSTM_CONV):
        conv = conv + cw_ref[kk:kk + 1, :] * xc_ref[8 - (MLSTM_CONV - 1) + kk:8 - (MLSTM_CONV - 1) + kk + L, :]
    xc_ref[0:8, :] = xc_ref[L:L + 8, :]
    qk_b = _silu(conv)

    gb = jnp.where(lane < G_A, g_raw + gb_ref[...], 0.0)
    ipre = gb
    logf = _log_sigmoid(pltpu.roll(gb, LANES - G_F, axis=1))
    logf = jnp.where(lane < MLSTM_HEADS, logf, 0.0)
    bcum = _dot_split(tri, logf, 3)
    blast = bcum[L - 1:L, :]
    wend = blast - bcum + ipre
    m_prev = m_ref[...]
    m_new = jnp.maximum(blast + m_prev, jnp.max(wend, axis=0, keepdims=True))
    scl = jnp.exp(blast + m_prev - m_new)
    wj = jnp.exp(wend - m_new)
    rows_src = jnp.where(lane < MLSTM_HEADS, ipre, pltpu.roll(bcum, MLSTM_HEADS, axis=1))
    rows_t = rows_src.T
    ones = jnp.ones((L, MLSTM_DH), F32)
    for h in range(MLSTM_HEADS):
        hs = slice(h * MLSTM_DH, (h + 1) * MLSTM_DH)
        wide = lambda t: jnp.concatenate([t, t], axis=1)
        b_rep = jnp.broadcast_to(bcum[:, h:h + 1], (L, LANES))
        wj_rep = jnp.broadcast_to(wj[:, h:h + 1], (L, LANES))
        ip_row = rows_t[h:h + 1, :]
        b_row = rows_t[MLSTM_HEADS + h:MLSTM_HEADS + h + 1, :]
        dlog = jnp.where(causal, wide(b_rep) - b_row + ip_row, NEG_INF)
        il = b_rep + m_prev[:, h:h + 1]
        m_i = jnp.maximum(il, jnp.broadcast_to(jnp.max(dlog, axis=1, keepdims=True), (L, LANES)))
        w_intra = jnp.exp(dlog - wide(m_i))
        s_inter = jnp.exp(il - m_i)
        q_h = qk_b[:, h * MLSTM_DH:(h + 1) * MLSTM_DH].astype(BF16)
        k_h = (qk_b[:, ML_W + h * MLSTM_DH:ML_W + (h + 1) * MLSTM_DH] * (MLSTM_DH ** -0.5)).astype(BF16)
        v_h = p_ref[0, :, P0_VB + h * MLSTM_DH:P0_VB + (h + 1) * MLSTM_DH]
        v_aug = jnp.concatenate([v_h, ones], axis=1)
        a = (w_intra * _dot_nt(q_h, k_h)).astype(BF16)
        c_prev = cst_ref[h]
        tot = _dot(a, v_aug.astype(BF16)) + wide(s_inter) * _dot(q_h, c_prev.astype(BF16))
        num = tot[:, :MLSTM_DH]
        den = tot[:, MLSTM_DH:]
        hid = num / jnp.maximum(jnp.abs(den), jnp.exp(-m_i))
        w_aug = (v_aug * wide(wj_rep)).astype(BF16)
        cst_ref[h] = scl[:, h:h + 1] * c_prev + _dot_tn(k_h, w_aug)
        ob = p_ref[0, :, P0_OB + h * MLSTM_DH:P0_OB + (h + 1) * MLSTM_DH]
        o_ref[0, :, GLA_V + h * MLSTM_DH:GLA_V + (h + 1) * MLSTM_DH] = (
            ob * _head_norm(hid, mg_ref[:, hs])).astype(o_ref.dtype)
    m_ref[...] = m_new


def _mixer0(proj, wa_up, ba, gla_g, conv_w, conv_b, b_i, b_f, ml_g):
    bsz, s, n = proj.shape
    L = MIX_TILE
    nr = MIX_ROWS if bsz % MIX_ROWS == 0 else 1
    wa_pad = jnp.zeros((LANES, GLA_QK), F32).at[G_A:G_A + GLA_GATE_RANK].set(wa_up).astype(BF16)
    gbias = jnp.zeros((1, LANES), F32).at[0, G_I:G_I + MLSTM_HEADS].set(b_i).at[0, G_F:G_F + MLSTM_HEADS].set(b_f)
    full = lambda shape: pl.BlockSpec(shape, lambda b, i: (0,) * len(shape))
    return pl.pallas_call(
        _mixer0_kernel,
        out_shape=jax.ShapeDtypeStruct((bsz, s, GLA_V + ML_W), BF16),
        grid=(bsz // nr, s // L),
        in_specs=[
            pl.BlockSpec((nr, L, n), lambda b, i: (b, i, 0)),
            full((LANES, GLA_QK)),
            full((1, GLA_QK)),
            full((1, GLA_V)),
            full((MLSTM_CONV, 2 * ML_W)),
            full((1, 2 * ML_W)),
            full((1, LANES)),
            full((1, ML_W)),
        ],
        out_specs=pl.BlockSpec((nr, L, GLA_V + ML_W), lambda b, i: (b, i, 0)),
        scratch_shapes=[
            pltpu.VMEM((nr, GLA_DV, GLA_QK), F32),
            pltpu.VMEM((nr, MLSTM_HEADS, MLSTM_DH, 2 * MLSTM_DH), F32),
            pltpu.VMEM((nr, 1, LANES), F32),
            pltpu.VMEM((nr, 8 + L, 2 * ML_W), F32),
        ],
        compiler_params=_params("parallel", "arbitrary"),
        name="mixer0",
    )(proj, wa_pad, ba.reshape(1, GLA_QK), gla_g.reshape(1, GLA_V), conv_w, conv_b.reshape(1, 2 * ML_W),
      gbias, ml_g.reshape(1, ML_W))


def _sb_kernel(q_ref, k_ref, v_ref, o_ref, qs_ref, acc_ref, car_ref, mar_ref, kn_ref):
    TQ, SUB, WIDE = q_ref.shape[1], SB_SUB, SB_WIDE
    ns = TQ // SUB
    RB = 2 * SUB
    i = pl.program_id(2)
    lane = _iota((1, LANES), 1)
    head0 = (lane >> _HEAD_SHIFT) == 0

    @pl.when(i == 0)
    def _():
        k2 = jnp.square(k_ref[0].astype(F32))
        for h in range(2):
            n2 = jnp.sum(jnp.where(head0 if h == 0 else ~head0, k2, 0.0), axis=1, keepdims=True)
            kn_ref[h:h + 1, :] = jnp.broadcast_to(jnp.sqrt(jnp.max(n2, axis=0, keepdims=True)), (1, LANES))

    for a in range(ns):
        qa = q_ref[0, a * SUB:(a + 1) * SUB, :].astype(F32) * (SB_DH ** -0.5 * LOG2E)
        qsq = qa * qa
        for h in range(2):
            rows = slice(a * RB + h * SUB, a * RB + (h + 1) * SUB)
            hmask = head0 if h == 0 else ~head0
            qs_ref[rows, :] = jnp.where(hmask, qa, 0.0).astype(BF16)
            n2 = jnp.sum(jnp.where(hmask, qsq, 0.0), axis=1, keepdims=True)
            qn = n2 * lax.rsqrt(n2 + 1e-30)
            mar_ref[rows, :] = qn * kn_ref[h:h + 1, :] * SB_BOUND_SLACK + SB_ZERO_BITS
    tri = jnp.where(_iota((WIDE, WIDE), 0) >= _iota((WIDE, WIDE), 1), -1.0, 0.0).astype(BF16)

    def key_rows(ku, nk):
        r0 = ku * SUB
        return pl.ds(r0 if isinstance(r0, int) else pl.multiple_of(r0, SUB), nk)

    def logits(a, ku, nk):
        return _dot_nt(qs_ref[a * RB:(a + 1) * RB, :], k_ref[0, key_rows(ku, nk), :])

    def softplus2(z, diagonal):
        sp = jnp.maximum(z, 0.0) + jnp.log(1.0 + jnp.exp2(-jnp.abs(z))) * LOG2E
        if not diagonal:
            return sp, None
        mask = _iota(z.shape, 1) < (_iota(z.shape, 0) & (SUB - 1))
        return jnp.where(mask, sp, 0.0), mask

    def suffix(sp):
        nk = sp.shape[1]
        return _dot(sp.astype(BF16), tri[:nk, :nk]), jnp.sum(sp, axis=1, keepdims=True)

    def weights(z, res, car, mask):
        att = jnp.exp2(z + res if car is None else z + res - car)
        if mask is not None:
            att = jnp.where(mask, att, 0.0)
        return att.astype(BF16)

    def far_tile(a, ku):
        rows = slice(a * RB, (a + 1) * RB)
        z = logits(a, ku, SUB)
        sp, _ = softplus2(z, False)
        res, tot = suffix(sp)
        car = car_ref[rows, :]
        acc_ref[rows, :] += _dot(weights(z, res, car, None), v_ref[0, key_rows(ku, SUB), :])
        car_ref[rows, :] = car + tot

    def near_tiles(first_block):
        chains = []
        for a in range(ns):
            g = a if first_block else ns * i + a
            chain = [(a, g, SUB, True)]
            if not first_block or a >= 2:
                chain.append((a, g - 2, WIDE, False))
            elif a == 1:
                chain.append((a, 0, SUB, False))
            chains.append(chain)
        jobs = [job for step in itertools.zip_longest(*chains) for job in step if job is not None]
        zs = [logits(a, ku, nk) for a, ku, nk, _ in jobs]
        sps = [softplus2(z, diagonal) for z, (_, _, _, diagonal) in zip(zs, jobs)]
        sufs = [suffix(sp) for sp, _ in sps]
        car = [None] * ns
        atts = []
        for z, (sp, mask), (res, tot), (a, _, _, _) in zip(zs, sps, sufs, jobs):
            atts.append(weights(z, res, car[a], mask))
            car[a] = tot if car[a] is None else car[a] + tot
        acc = [None] * ns
        for att, (a, ku, nk, _) in zip(atts, jobs):
            t = _dot(att, v_ref[0, key_rows(ku, nk), :])
            acc[a] = t if acc[a] is None else acc[a] + t
        for a in range(ns):
            acc_ref[a * RB:(a + 1) * RB, :] = acc[a]
            car_ref[a * RB:(a + 1) * RB, :] = jnp.broadcast_to(car[a], (RB, LANES))

    pl.when(i == 0)(functools.partial(near_tiles, True))
    pl.when(i > 0)(functools.partial(near_tiles, False))

    def next_unit(a, t):
        return ns * i + a - 3 - t

    def pending(t):
        gap = [jnp.where(next_unit(a, t) >= 0, mar_ref[a * RB:(a + 1) * RB, :] - car_ref[a * RB:(a + 1) * RB, :], -1.0)
               for a in range(ns)]
        return jnp.max(functools.reduce(jnp.maximum, gap)) > 0.0

    def far_body(c):
        t, _ = c
        for a in range(ns):
            ku = next_unit(a, t)
            pl.when(ku >= 0)(functools.partial(far_tile, a, ku))
        return t + 1, pending(t + 1)

    lax.while_loop(lambda c: c[1], far_body, (0, pending(0)))

    for a in range(ns):
        o_ref[0, a * SUB:(a + 1) * SUB, :] = jnp.where(
            head0, acc_ref[a * RB:a * RB + SUB, :], acc_ref[a * RB + SUB:(a + 1) * RB, :]).astype(o_ref.dtype)


def _stick_breaking(pc):
    bsz, s, _ = pc.shape
    npair = SB_W // LANES
    tq = min(SB_TQ, s)
    return pl.pallas_call(
        _sb_kernel,
        out_shape=jax.ShapeDtypeStruct((bsz, s, SB_W), BF16),
        grid=(bsz, npair, s // tq),
        in_specs=[
            pl.BlockSpec((1, tq, LANES), lambda b, p, i: (b, i, p)),
            pl.BlockSpec((1, s, LANES), lambda b, p, i: (b, 0, npair + p)),
            pl.BlockSpec((1, s, LANES), lambda b, p, i: (b, 0, 2 * npair + p)),
        ],
        out_specs=pl.BlockSpec((1, tq, LANES), lambda b, p, i: (b, i, p)),
        scratch_shapes=[
            pltpu.VMEM((2 * tq, LANES), BF16),
            pltpu.VMEM((2 * tq, LANES), F32),
            pltpu.VMEM((2 * tq, LANES), F32),
            pltpu.VMEM((2 * tq, LANES), F32),
            pltpu.VMEM((8, LANES), F32),
        ],
        compiler_params=_params("parallel", "parallel", "arbitrary"),
        name="stick_breaking",
    )(pc, pc, pc)


def _t5_bucket(dist):
    max_exact = N_REL_BUCKETS // 2
    dd = np.maximum(dist, 1).astype(np.float64)
    large = max_exact + (np.log(dd / max_exact) / np.log(REL_MAX_DIST / max_exact)
                         * (N_REL_BUCKETS - max_exact)).astype(np.int32)
    large = np.minimum(large, N_REL_BUCKETS - 1)
    return np.where(dist < max_exact, dist, large).astype(np.int32)


def _dil_buckets():
    qi = np.arange(DIL_BLK)[:, None]
    kj = np.arange(2 * DIL_BLK)[None, :]
    delta = qi - kj + DIL_BLK
    in_win = (delta >= 0) & (delta <= DIL_BLK)
    tabs = []
    for window, dil in DIL_PAIRS:
        assert window // dil == DIL_BLK
        bucket = _t5_bucket(np.clip(delta, 0, None) * dil)
        tabs.append(np.where(in_win, bucket, -1).astype(np.int32))
    return np.stack(tabs, 0)


def _dil_kernel(q_ref, k_ref, v_ref, bkt_ref, bkt0_ref, tab_ref, o_ref, m_sc, l_sc, acc_sc, bias0_sc, bias_sc):
    s_len = q_ref.shape[1]
    first_batch = pl.program_id(0) == 0
    hp = pl.program_id(1)
    g = pl.program_id(2)
    blk = DIL_BLK
    ng = len(DIL_PAIRS)
    lane = _iota((1, LANES), 1)
    lane_head = lane >> _HEAD_SHIFT
    sel0 = lane_head == 0
    kcol = _iota((2 * blk, 2 * blk), 1)

    def head_biases(gi, bkt, dst):
        @pl.when(first_batch)
        def _():
            n = bkt.shape[0]
            for hl in range(2):
                head = gi * DIL_HEADS_PER_GROUP + hp * 2 + hl
                bias = jnp.full(bkt.shape, NEG_INF, F32)
                for bk in range(N_REL_BUCKETS):
                    bias = jnp.where(bkt == bk, tab_ref[bk, head], bias)
                dst[hl * n:(hl + 1) * n, :] = bias
        return dst[...]

    def stack_heads(qb):
        return jnp.concatenate([jnp.where(sel0, qb, 0.0), jnp.where(sel0, 0.0, qb)], axis=0).astype(BF16)

    def with_ones(vv):
        return jnp.concatenate([vv, jnp.ones_like(vv)], axis=1).astype(BF16)

    def first_group(gi, dil):
        n_sub = s_len // dil
        bias2 = head_biases(gi, bkt0_ref[...], bias0_sc.at[hp])

        def residue(r):
            rows = pl.ds(r, n_sub, stride=dil)
            logits = _dot_nt(stack_heads(q_ref[0, rows, :] * (DIL_DH ** -0.5)), k_ref[0, rows, :].astype(BF16)) + bias2
            return rows, logits

        def body(tt, carry):
            s1 = [residue(tt * DIL_FIRST_UNROLL + u) for u in range(DIL_FIRST_UNROLL)]
            s2 = [jnp.max(lg, axis=1, keepdims=True) for _, lg in s1]
            s3 = [_dot(jnp.exp(lg - mc).astype(BF16), with_ones(v_ref[0, rows, :])) for (rows, lg), mc in zip(s1, s2)]
            for (rows, _), mc, pv in zip(s1, s2, s3):
                m_sc[rows, :] = jnp.where(sel0, mc[0:n_sub], mc[n_sub:])
                l_sc[rows, :] = jnp.where(sel0, pv[0:n_sub, LANES:], pv[n_sub:, LANES:])
                acc_sc[rows, :] = jnp.where(sel0, pv[0:n_sub, :LANES], pv[n_sub:, :LANES])
            return carry

        lax.fori_loop(0, dil // DIL_FIRST_UNROLL, body, 0)

    def group(gi, dil):
        nb = s_len // (blk * dil)
        bias2 = head_biases(gi, bkt_ref[gi], bias_sc.at[hp, gi])

        def stage_logits(t):
            r = t // nb
            n = t % nb
            q_start = r + dil * blk * n
            p_start = r + dil * blk * jnp.maximum(n - 1, 0)
            rows_q = pl.ds(q_start, blk, stride=dil) if dil > 1 else pl.ds(q_start, blk)
            rows_p = pl.ds(p_start, blk, stride=dil) if dil > 1 else pl.ds(p_start, blk)
            q2 = stack_heads(q_ref[0, rows_q, :] * (DIL_DH ** -0.5))
            kk = jnp.concatenate([k_ref[0, rows_p, :], k_ref[0, rows_q, :]], axis=0).astype(BF16)
            logits = _dot_nt(q2, kk) + bias2
            logits = jnp.where(jnp.logical_and(n == 0, kcol < blk), NEG_INF, logits)
            return rows_q, rows_p, logits

        def stage_pv(rows_q, rows_p, logits, m_col):
            p = jnp.exp(logits - m_col)
            vv = with_ones(jnp.concatenate([v_ref[0, rows_p, :], v_ref[0, rows_q, :]], axis=0))
            pv = _dot(p.astype(BF16), vv)
            return pv[:, LANES:], pv[:, :LANES]

        def body(tt, carry):
            ts = [tt * DIL_UNROLL + u for u in range(DIL_UNROLL)]
            s1 = [stage_logits(t) for t in ts]
            s2 = [jnp.max(lg, axis=1, keepdims=True) for _, _, lg in s1]
            s3 = [stage_pv(rq, rp, lg, mc) for (rq, rp, lg), mc in zip(s1, s2)]
            outs = []
            for (rows_q, _, _), m_col, (ps, pv) in zip(s1, s2, s3):
                m_old = m_sc[rows_q, :]
                m_blk = jnp.where(sel0, m_col[0:blk], m_col[blk:])
                m_new = jnp.maximum(m_old, m_blk)
                a_old = jnp.exp(m_old - m_new)
                a_blk = jnp.exp(m_blk - m_new)
                l_new = a_old * l_sc[rows_q, :] + a_blk * jnp.where(sel0, ps[0:blk], ps[blk:])
                a_new = a_old * acc_sc[rows_q, :] + a_blk * jnp.where(sel0, pv[0:blk], pv[blk:])
                outs.append((rows_q, m_new, l_new, a_new))
            for rows_q, m_new, l_new, a_new in outs:
                m_sc[rows_q, :] = m_new
                l_sc[rows_q, :] = l_new
                acc_sc[rows_q, :] = a_new
            return carry

        for tt in range(s_len // (blk * DIL_UNROLL)):
            body(tt, 0)

    for gi, (_, dil) in enumerate(DIL_PAIRS):
        pl.when(g == ng - 1 - gi)(functools.partial(first_group if gi == ng - 1 else group, gi, dil))

    @pl.when(g == ng - 1)
    def _():
        o_ref[0] = (acc_sc[...] / l_sc[...]).astype(o_ref.dtype)


def _dilated(pd, rel_bias):
    bsz, s, _ = pd.shape
    ng = len(DIL_PAIRS)
    gw = DIL_HEADS_PER_GROUP * DIL_DH // LANES
    nq = DIL_W // LANES
    bkt = jnp.asarray(_dil_buckets())
    dil0 = DIL_PAIRS[-1][1]
    n_sub = s // dil0
    assert s % (DIL_BLK * dil0) == 0 and dil0 % DIL_FIRST_UNROLL == 0
    delta = np.arange(n_sub)[:, None] - np.arange(n_sub)[None, :]
    bkt0 = jnp.asarray(np.where((delta >= 0) & (delta <= DIL_BLK), _t5_bucket(np.clip(delta, 0, None) * dil0), -1)
                       .astype(np.int32))
    blk_spec = lambda base: pl.BlockSpec((1, s, LANES), lambda b, p, g: (b, 0, base + (ng - 1 - g) * gw + p))
    return pl.pallas_call(
        _dil_kernel,
        out_shape=jax.ShapeDtypeStruct((bsz, s, DIL_OUT), BF16),
        grid=(bsz, gw, ng),
        in_specs=[
            blk_spec(0),
            blk_spec(nq),
            blk_spec(2 * nq),
            pl.BlockSpec((ng, DIL_BLK, 2 * DIL_BLK), lambda b, p, g: (0, 0, 0)),
            pl.BlockSpec((n_sub, n_sub), lambda b, p, g: (0, 0)),
            pl.BlockSpec(memory_space=pltpu.SMEM),
        ],
        out_specs=pl.BlockSpec((1, s, LANES), lambda b, p, g: (b, 0, p)),
        scratch_shapes=[pltpu.VMEM((s, LANES), F32)] * 3 + [
            pltpu.VMEM((gw, 2 * n_sub, n_sub), F32),
            pltpu.VMEM((gw, ng - 1, 2 * DIL_BLK, 2 * DIL_BLK), F32),
        ],
        compiler_params=_params("arbitrary", "arbitrary", "arbitrary"),
        name="dilated",
    )(pd, pd, pd, bkt, bkt0, rel_bias)


def _ab_weight(w):
    o = np.cumsum((0,) + (GLA_QK, GLA_QK, GLA_V, GLA_V, GLA_GATE_RANK, 2 * ML_W, ML_W, MLSTM_HEADS, MLSTM_HEADS, ML_W))
    qa, ka, va, ra, aa, qkb, vb, ib, fb, ob = [w[:, o[j]:o[j + 1]] for j in range(10)]
    pad = jnp.zeros((w.shape[0], LANES - 2 * MLSTM_HEADS - GLA_GATE_RANK), w.dtype)
    return jnp.concatenate([qa, ka, va, ra, qkb, vb, ob, ib, fb, aa, pad], axis=1).astype(BF16)


def kernel(x, c, ada_w, ada_b, ln_g, ln_b, ab_w_in, gla_wa_up, gla_ba, gla_norm_g, ml_conv_w, ml_conv_b,
           ml_b_i, ml_b_f, ml_norm_g, ab_w_out, cd_w_in, rel_bias, cd_w_out, ffn_w1, ffn_w3, ffn_w2):
    bsz, s, d = x.shape
    mod_all = _ada_mod(c, ada_w, ada_b).reshape(DEPTH, bsz, 6, d)
    w1, w3, w2 = ffn_w1.astype(BF16), ffn_w3.astype(BF16), ffn_w2.astype(BF16)
    tm = IN_TM
    for layer in range(DEPTH):
        mod = mod_all[layer]
        j = layer // 2
        if layer % 2 == 0:
            gates = [(P0_RA, P0_RA + GLA_V, _silu), (P0_OB, P0_OB + ML_W, jax.nn.sigmoid)]
            (proj,) = _inproj(x, mod, _ab_weight(ab_w_in[j]), [(P0_N, F32)], tm, 0, 1, "inproj0", gates)
            cat = _mixer0(proj, gla_wa_up[j], gla_ba[j], gla_norm_g[j], ml_conv_w[j], ml_conv_b[j],
                          ml_b_i[j], ml_b_f[j], ml_norm_g[j])
            acts, w_outs = [cat], [ab_w_out[j].astype(BF16)]
        else:
            pc, pd = _inproj(x, mod, cd_w_in[j].astype(BF16), [(3 * SB_W, BF16), (3 * DIL_W, F32)],
                             tm, 0, 1, "inproj1")
            oc = _stick_breaking(pc)
            od = _dilated(pd, rel_bias)
            w_out = cd_w_out[j].astype(BF16)
            acts, w_outs = [oc, od], [w_out[:SB_W], w_out[SB_W:]]
        x = _post(acts, w_outs, x, mod, w1, w3, w2, layer, ln_g[layer], ln_b[layer], FFN_TM)
    return x
```

```python
import functools
import itertools

import numpy as np
import jax
import jax.numpy as jnp
from jax import lax
from jax.experimental import pallas as pl
from jax.experimental.pallas import tpu as pltpu

F32 = jnp.float32
BF16 = jnp.bfloat16

D_MODEL = 1024
DEPTH = 2
GLA_HEADS = 4
GLA_DK = 64
GLA_DV = 128
GLA_GATE_RANK = 16
GLA_TAU = 16.0
GLA_CHUNK = 64
MLSTM_HEADS = 4
MLSTM_DH = 128
MLSTM_CONV = 4
SB_HEADS = 8
SB_DH = 64
DIL_PAIRS = ((128, 1), (512, 4), (2048, 16))
DIL_HEADS_PER_GROUP = 4
DIL_DH = 64
DIL_BLK = 128
DIL_UNROLL = 4
DIL_FIRST_UNROLL = 2
N_REL_BUCKETS = 32
REL_MAX_DIST = 2048
D_FF = ((8 * D_MODEL + 3 * 256 - 1) // (3 * 256)) * 256
LN_EPS = 1e-5
RES_ALPHA = (2 * DEPTH) ** 0.25
NEG_INF = -1e30

GLA_QK = GLA_HEADS * GLA_DK
GLA_V = GLA_HEADS * GLA_DV
ML_W = MLSTM_HEADS * MLSTM_DH
SB_W = SB_HEADS * SB_DH
DIL_W = len(DIL_PAIRS) * DIL_HEADS_PER_GROUP * DIL_DH
DIL_OUT = DIL_HEADS_PER_GROUP * DIL_DH

LANES = 128
VMEM_LIMIT = 56 * 1024 * 1024

P0_QA = 0
P0_KA = P0_QA + GLA_QK
P0_VA = P0_KA + GLA_QK
P0_RA = P0_VA + GLA_V
P0_QKB = P0_RA + GLA_V
P0_VB = P0_QKB + 2 * ML_W
P0_OB = P0_VB + ML_W
P0_G = P0_OB + ML_W
P0_N = P0_G + LANES
G_I = 0
G_F = MLSTM_HEADS
G_A = 2 * MLSTM_HEADS

MIX_TILE = 256
MIX_ROWS = 4
FF_CHUNK = 256
IN_TM = 512
FFN_TM = 1024
POST_SUB = 512
SB_TQ = 4096
SB_SUB = 128
SB_WIDE = 256
SB_ZERO_BITS = 160.0
SB_BOUND_SLACK = 1.01
LOG2E = 1.4426950408889634
_CHUNK_SHIFT = 6
_HEAD_SHIFT = 6
assert GLA_CHUNK == GLA_DK == 1 << _CHUNK_SHIFT and SB_DH == DIL_DH == 1 << _HEAD_SHIFT


def _dot(a, b):
    return jnp.dot(a, b, preferred_element_type=F32)


def _dot_nt(a, b):
    return lax.dot_general(a, b, (((1,), (1,)), ((), ())), preferred_element_type=F32)


def _dot_tn(a, b):
    return lax.dot_general(a, b, (((0,), (0,)), ((), ())), preferred_element_type=F32)


def _dot_split(t, x, terms):
    acc = None
    rem = x
    for i in range(terms):
        part = rem.astype(BF16)
        d = _dot(t, part)
        acc = d if acc is None else acc + d
        if i + 1 < terms:
            rem = rem - part.astype(F32)
    return acc


def _log_sigmoid(x):
    return jnp.minimum(x, 0.0) - jnp.log(1.0 + jnp.exp(-jnp.abs(x)))


def _silu(x):
    return x * jax.nn.sigmoid(x)


def _layer_norm(r, g, b):
    mu = jnp.mean(r, axis=-1, keepdims=True)
    d = r - mu
    var = jnp.mean(d * d, axis=-1, keepdims=True)
    return d * lax.rsqrt(var + LN_EPS) * g + b


def _head_norm(x, g):
    mu = jnp.mean(x, axis=-1, keepdims=True)
    d = x - mu
    var = jnp.mean(d * d, axis=-1, keepdims=True)
    return d * lax.rsqrt(var + LN_EPS) * g


def _iota(shape, dim):
    return lax.broadcasted_iota(jnp.int32, shape, dim)


def _params(*sem):
    return pltpu.CompilerParams(dimension_semantics=sem, vmem_limit_bytes=VMEM_LIMIT)


def _ada_kernel(c_ref, w_ref, b_ref, o_ref):
    ca = _silu(c_ref[...]).astype(BF16)
    o_ref[0] = _dot(ca, w_ref[0].astype(BF16)) + b_ref[0]


def _ada_mod(c, ada_w, ada_b):
    bsz, d = c.shape
    n = ada_w.shape[-1]
    tn = n // 4
    return pl.pallas_call(
        _ada_kernel,
        out_shape=jax.ShapeDtypeStruct((DEPTH, bsz, n), F32),
        grid=(DEPTH, n // tn),
        in_specs=[
            pl.BlockSpec((bsz, d), lambda l, j: (0, 0)),
            pl.BlockSpec((1, d, tn), lambda l, j: (l, 0, j)),
            pl.BlockSpec((1, 1, tn), lambda l, j: (l, 0, j)),
        ],
        out_specs=pl.BlockSpec((1, bsz, tn), lambda l, j: (l, 0, j)),
        compiler_params=_params("arbitrary", "arbitrary"),
        name="ada_mod",
    )(c, ada_w, ada_b.reshape(DEPTH, 1, n))


def _inproj_kernel(x_ref, mod_ref, w_ref, *refs, shift_row, scale_row, gates, biased):
    b_ref, o_refs = (refs[0], refs[1:]) if biased else (None, refs)
    sh = mod_ref[0, shift_row:shift_row + 1, :]
    sc = mod_ref[0, scale_row:scale_row + 1, :]
    hm = (x_ref[0] * (1.0 + sc) + sh).astype(BF16)
    col = 0
    for o_ref in o_refs:
        n = o_ref.shape[-1]
        cuts = sorted({0, n} | {c for lo, hi, _ in gates if col <= lo and hi <= col + n for c in (lo - col, hi - col)})
        for lo, hi in zip(cuts[:-1], cuts[1:]):
            y = _dot(hm, w_ref[:, col + lo:col + hi])
            for glo, ghi, fn in gates:
                if glo == col + lo and ghi == col + hi:
                    y = fn(y) if b_ref is None else fn(y, b_ref[:, glo:ghi])
            o_ref[0, :, lo:hi] = y.astype(o_ref.dtype)
        col += n


def _inproj(x, mod, w, outs, tm, shift_row, scale_row, name, gates=(), bias=None):
    bsz, s, d = x.shape
    n = w.shape[1]
    assert sum(o[0] for o in outs) == n
    biased = bias is not None
    return pl.pallas_call(
        functools.partial(_inproj_kernel, shift_row=shift_row, scale_row=scale_row, gates=tuple(gates),
                          biased=biased),
        out_shape=[jax.ShapeDtypeStruct((bsz, s, o[0]), o[1]) for o in outs],
        grid=(bsz, s // tm),
        in_specs=[
            pl.BlockSpec((1, tm, d), lambda b, i: (b, i, 0)),
            pl.BlockSpec((1, 6, d), lambda b, i: (b, 0, 0)),
            pl.BlockSpec((d, n), lambda b, i: (0, 0)),
        ] + ([pl.BlockSpec((1, n), lambda b, i: (0, 0))] if biased else []),
        out_specs=[pl.BlockSpec((1, tm, o[0]), lambda b, i: (b, i, 0)) for o in outs],
        compiler_params=_params("parallel", "arbitrary"),
        name=name,
    )(*((x, mod, w) + ((bias,) if biased else ())))


def _post_kernel(*refs, n_act):
    act_refs = refs[:n_act]
    wo_refs = refs[n_act:2 * n_act]
    x_ref, mod_ref, w1_ref, w3_ref, w2_ref, g_ref, b_ref, o_ref, gm_ref = refs[2 * n_act:]

    def mix(t):
        rows = slice(t * POST_SUB, (t + 1) * POST_SUB)
        y = None
        for a_ref, w_ref in zip(act_refs, wo_refs):
            part = _dot(a_ref[0, rows, :], w_ref[...])
            y = part if y is None else y + part
        r = RES_ALPHA * x_ref[0, rows, :] + (1.0 + mod_ref[0, 2:3, :]) * y
        xm = _layer_norm(r, g_ref[0:1, :], b_ref[0:1, :])
        return xm, (xm * (1.0 + mod_ref[0, 4:5, :]) + mod_ref[0, 3:4, :]).astype(BF16)

    def up(t, hf):
        for c in range(D_FF // FF_CHUNK):
            cs = slice(c * FF_CHUNK, (c + 1) * FF_CHUNK)
            gm_ref[t, :, cs] = (_silu(_dot(hf, w1_ref[:, cs])) * _dot(hf, w3_ref[:, cs])).astype(BF16)

    def down(t, xm):
        rows = slice(t * POST_SUB, (t + 1) * POST_SUB)
        r2 = RES_ALPHA * xm + (1.0 + mod_ref[0, 5:6, :]) * _dot(gm_ref[t], w2_ref[...])
        o_ref[0, rows, :] = _layer_norm(r2, g_ref[1:2, :], b_ref[1:2, :])

    n = x_ref.shape[1] // POST_SUB
    mixed = {}
    for step in range(n + 2):
        if step >= 2:
            down(step - 2, mixed[step - 2][0])
        if 1 <= step <= n:
            up(step - 1, mixed[step - 1][1])
        if step < n:
            mixed[step] = mix(step)


def _post(acts, w_outs, x, mod, w1, w3, w2, layer, ln_g, ln_b, tm):
    bsz, s, d = x.shape
    n_act = len(acts)
    const = lambda shape: pl.BlockSpec(shape, lambda b, i: (0,) * len(shape), pipeline_mode=pl.Buffered(1))
    of_layer = lambda shape: pl.BlockSpec((None,) + shape, lambda b, i: (layer, 0, 0), pipeline_mode=pl.Buffered(1))
    in_specs = [pl.BlockSpec((1, tm, a.shape[-1]), lambda b, i: (b, i, 0)) for a in acts]
    in_specs += [const(w.shape) for w in w_outs]
    in_specs += [
        pl.BlockSpec((1, tm, d), lambda b, i: (b, i, 0)),
        pl.BlockSpec((1, 6, d), lambda b, i: (b, 0, 0)),
        of_layer((d, D_FF)),
        of_layer((d, D_FF)),
        of_layer((D_FF, d)),
        const((2, d)),
        const((2, d)),
    ]
    return pl.pallas_call(
        functools.partial(_post_kernel, n_act=n_act),
        out_shape=jax.ShapeDtypeStruct((bsz, s, d), F32),
        grid=(bsz, s // tm),
        in_specs=in_specs,
        out_specs=pl.BlockSpec((1, tm, d), lambda b, i: (b, i, 0)),
        scratch_shapes=[pltpu.VMEM((tm // POST_SUB, POST_SUB, D_FF), BF16)],
        compiler_params=_params("parallel", "arbitrary"),
        name="post",
    )(*acts, *w_outs, x, mod, w1, w3, w2, ln_g, ln_b)


def _mixer0_kernel(p_ref, wa_ref, ba_ref, gg_ref, cw_ref, cb_ref, mg_ref, o_ref,
                   st_ref, cst_ref, m_ref, xc_ref):
    @pl.when(pl.program_id(1) == 0)
    def _():
        st_ref[...] = jnp.zeros_like(st_ref)
        cst_ref[...] = jnp.zeros_like(cst_ref)
        m_ref[...] = jnp.zeros_like(m_ref)
        xc_ref[:, 0:8, :] = jnp.zeros((xc_ref.shape[0], 8, 2 * ML_W), F32)

    L = MIX_TILE
    row = _iota((L, L), 0)
    col = _iota((L, L), 1)
    causal = col <= row
    same_chunk = (row >> _CHUNK_SHIFT) == (col >> _CHUNK_SHIFT)
    lane_qk = _iota((1, GLA_QK), 1)
    r4 = _iota((GLA_HEADS * GLA_CHUNK, GLA_CHUNK), 0)
    c4 = _iota((GLA_HEADS * GLA_CHUNK, GLA_CHUNK), 1)
    consts = dict(
        causal=causal,
        tri=jnp.where(causal, 1.0, 0.0).astype(BF16),
        tri_blk=jnp.where(same_chunk & causal, 1.0, 0.0).astype(BF16),
        head_masks=[(lane_qk >> _CHUNK_SHIFT) == h for h in range(GLA_HEADS)],
        tril4=c4 <= (r4 & (GLA_CHUNK - 1)),
    )
    for b in range(p_ref.shape[0]):
        _mixer0_row(consts, p_ref.at[pl.ds(b, 1)], wa_ref, ba_ref, gg_ref, cw_ref, cb_ref, mg_ref,
                    o_ref.at[pl.ds(b, 1)], st_ref.at[b], cst_ref.at[b], m_ref.at[b], xc_ref.at[b])


def _mixer0_row(consts, p_ref, wa_ref, ba_ref, gg_ref, cw_ref, cb_ref, mg_ref, o_ref,
                st_ref, cst_ref, m_ref, xc_ref):
    L = MIX_TILE
    causal, tri, tri_blk = consts["causal"], consts["tri"], consts["tri_blk"]
    head_masks, tril4 = consts["head_masks"], consts["tril4"]
    lane = _iota((1, LANES), 1)
    g_raw = p_ref[0, :, P0_G:P0_G + LANES]

    u = _dot(g_raw.astype(BF16), wa_ref[...]) + ba_ref[...]
    la = _log_sigmoid(u) * (1.0 / GLA_TAU)
    bcs = _dot_split(tri_blk, la, 3)
    q_in = p_ref[0, :, P0_QA:P0_QA + GLA_QK] * (GLA_DK ** -0.5) * jnp.exp(bcs)
    k_raw = p_ref[0, :, P0_KA:P0_KA + GLA_QK]
    k_in = k_raw * jnp.exp(-bcs)
    oa_chunks = []
    for c in range(L // GLA_CHUNK):
        r0, r1 = c * GLA_CHUNK, (c + 1) * GLA_CHUNK
        b_c = bcs[r0:r1]
        bl = b_c[GLA_CHUNK - 1:GLA_CHUNK, :]
        q_c = q_in[r0:r1]
        k_c = k_in[r0:r1].astype(BF16)
        k_end = k_raw[r0:r1] * jnp.exp(bl - b_c)
        v_c = p_ref[0, r0:r1, P0_VA:P0_VA + GLA_V]
        q_exp = jnp.concatenate([jnp.where(hm, q_c, 0.0) for hm in head_masks], axis=0).astype(BF16)
        k_exp = jnp.concatenate([jnp.where(hm, k_end, 0.0) for hm in head_masks], axis=0).astype(BF16)
        sc = jnp.where(tril4, _dot_nt(q_exp, k_c), 0.0)
        intra = _dot(sc.astype(BF16), v_c.astype(BF16))
        inter = _dot_nt(q_exp, st_ref[...].astype(BF16))
        o_heads = []
        for h in range(GLA_HEADS):
            h0, h1 = h * GLA_CHUNK, (h + 1) * GLA_CHUNK
            o_heads.append(intra[h0:h1, h * GLA_DV:(h + 1) * GLA_DV] + inter[h0:h1])
        oa_chunks.append(o_heads)
        v_cat = jnp.concatenate([v_c[:, h * GLA_DV:(h + 1) * GLA_DV] for h in range(GLA_HEADS)], axis=0)
        st_ref[...] = st_ref[...] * jnp.exp(bl) + _dot_tn(v_cat.astype(BF16), k_exp)
    for h in range(GLA_HEADS):
        o_h = jnp.concatenate([oc[h] for oc in oa_chunks], axis=0)
        cs = slice(h * GLA_DV, (h + 1) * GLA_DV)
        ra = p_ref[0, :, P0_RA + h * GLA_DV:P0_RA + (h + 1) * GLA_DV]
        o_ref[0, :, cs] = (_head_norm(o_h, gg_ref[:, cs]) * ra).astype(o_ref.dtype)

    xc_ref[8:8 + L, :] = p_ref[0, :, P0_QKB:P0_QKB + 2 * ML_W]
    conv = cb_ref[...]
    for kk in range(MLSTM_CONV):
        conv = conv + cw_ref[kk:kk + 1, :] * xc_ref[8 - (MLSTM_CONV - 1) + kk:8 - (MLSTM_CONV - 1) + kk + L, :]
    xc_ref[0:8, :] = xc_ref[L:L + 8, :]
    qk_b = _silu(conv)

    gb = jnp.where(lane < G_A, g_raw, 0.0)
    ipre = gb
    logf = jnp.where(lane < MLSTM_HEADS, pltpu.roll(gb, LANES - G_F, axis=1), 0.0)
    bcum = _dot_split(tri, logf, 3)
    blast = bcum[L - 1:L, :]
    wend = blast - bcum + ipre
    m_prev = m_ref[...]
    m_new = jnp.maximum(blast + m_prev, jnp.max(wend, axis=0, keepdims=True))
    scl = jnp.exp(blast + m_prev - m_new)
    wj = jnp.exp(wend - m_new)
    rows_src = jnp.where(lane < MLSTM_HEADS, ipre, pltpu.roll(bcum, MLSTM_HEADS, axis=1))
    rows_t = rows_src.T
    ones = jnp.ones((L, MLSTM_DH), F32)
    for h in range(MLSTM_HEADS):
        hs = slice(h * MLSTM_DH, (h + 1) * MLSTM_DH)
        wide = lambda t: jnp.concatenate([t, t], axis=1)
        b_rep = jnp.broadcast_to(bcum[:, h:h + 1], (L, LANES))
        wj_rep = jnp.broadcast_to(wj[:, h:h + 1], (L, LANES))
        ip_row = rows_t[h:h + 1, :]
        b_row = rows_t[MLSTM_HEADS + h:MLSTM_HEADS + h + 1, :]
        dlog = jnp.where(causal, wide(b_rep) - b_row + ip_row, NEG_INF)
        il = b_rep + m_prev[:, h:h + 1]
        m_i = jnp.maximum(il, jnp.broadcast_to(jnp.max(dlog, axis=1, keepdims=True), (L, LANES)))
        w_intra = jnp.exp(dlog - wide(m_i))
        s_inter = jnp.exp(il - m_i)
        q_h = qk_b[:, h * MLSTM_DH:(h + 1) * MLSTM_DH].astype(BF16)
        k_h = (qk_b[:, ML_W + h * MLSTM_DH:ML_W + (h + 1) * MLSTM_DH] * (MLSTM_DH ** -0.5)).astype(BF16)
        v_h = p_ref[0, :, P0_VB + h * MLSTM_DH:P0_VB + (h + 1) * MLSTM_DH]
        v_aug = jnp.concatenate([v_h, ones], axis=1)
        a = (w_intra * _dot_nt(q_h, k_h)).astype(BF16)
        c_prev = cst_ref[h]
        tot = _dot(a, v_aug.astype(BF16)) + wide(s_inter) * _dot(q_h, c_prev.astype(BF16))
        num = tot[:, :MLSTM_DH]
        den = tot[:, MLSTM_DH:]
        hid = num / jnp.maximum(jnp.abs(den), jnp.exp(-m_i))
        w_aug = (v_aug * wide(wj_rep)).astype(BF16)
        cst_ref[h] = scl[:, h:h + 1] * c_prev + _dot_tn(k_h, w_aug)
        ob = p_ref[0, :, P0_OB + h * MLSTM_DH:P0_OB + (h + 1) * MLSTM_DH]
        o_ref[0, :, GLA_V + h * MLSTM_DH:GLA_V + (h + 1) * MLSTM_DH] = (
            ob * _head_norm(hid, mg_ref[:, hs])).astype(o_ref.dtype)
    m_ref[...] = m_new


def _gate_prep(y, b):
    y = y + b
    lane = _iota(y.shape, 1)
    return jnp.where((lane >= G_F) & (lane < G_A), _log_sigmoid(y), y)


def _gate_bias(b_i, b_f):
    return (jnp.zeros((1, P0_N), F32).at[0, P0_G + G_I:P0_G + G_I + MLSTM_HEADS].set(b_i)
            .at[0, P0_G + G_F:P0_G + G_F + MLSTM_HEADS].set(b_f))


def _mixer0(proj, wa_up, ba, gla_g, conv_w, conv_b, ml_g):
    bsz, s, n = proj.shape
    L = MIX_TILE
    nr = MIX_ROWS if bsz % MIX_ROWS == 0 else 1
    wa_pad = jnp.zeros((LANES, GLA_QK), F32).at[G_A:G_A + GLA_GATE_RANK].set(wa_up).astype(BF16)
    full = lambda shape: pl.BlockSpec(shape, lambda b, i: (0,) * len(shape))
    return pl.pallas_call(
        _mixer0_kernel,
        out_shape=jax.ShapeDtypeStruct((bsz, s, GLA_V + ML_W), BF16),
        grid=(bsz // nr, s // L),
        in_specs=[
            pl.BlockSpec((nr, L, n), lambda b, i: (b, i, 0)),
            full((LANES, GLA_QK)),
            full((1, GLA_QK)),
            full((1, GLA_V)),
            full((MLSTM_CONV, 2 * ML_W)),
            full((1, 2 * ML_W)),
            full((1, ML_W)),
        ],
        out_specs=pl.BlockSpec((nr, L, GLA_V + ML_W), lambda b, i: (b, i, 0)),
        scratch_shapes=[
            pltpu.VMEM((nr, GLA_DV, GLA_QK), F32),
            pltpu.VMEM((nr, MLSTM_HEADS, MLSTM_DH, 2 * MLSTM_DH), F32),
            pltpu.VMEM((nr, 1, LANES), F32),
            pltpu.VMEM((nr, 8 + L, 2 * ML_W), F32),
        ],
        compiler_params=_params("parallel", "arbitrary"),
        name="mixer0",
    )(proj, wa_pad, ba.reshape(1, GLA_QK), gla_g.reshape(1, GLA_V), conv_w, conv_b.reshape(1, 2 * ML_W),
      ml_g.reshape(1, ML_W))


def _sb_kernel(q_ref, k_ref, v_ref, o_ref, qs_ref, acc_ref, car_ref, mar_ref, kn_ref):
    TQ, SUB, WIDE = q_ref.shape[1], SB_SUB, SB_WIDE
    ns = TQ // SUB
    RB = 2 * SUB
    i = pl.program_id(2)
    lane = _iota((1, LANES), 1)
    head0 = (lane >> _HEAD_SHIFT) == 0

    @pl.when(i == 0)
    def _():
        k2 = jnp.square(k_ref[0].astype(F32))
        for h in range(2):
            n2 = jnp.sum(jnp.where(head0 if h == 0 else ~head0, k2, 0.0), axis=1, keepdims=True)
            kn_ref[h:h + 1, :] = jnp.broadcast_to(jnp.sqrt(jnp.max(n2, axis=0, keepdims=True)), (1, LANES))

    for a in range(ns):
        qa = q_ref[0, a * SUB:(a + 1) * SUB, :].astype(F32) * (SB_DH ** -0.5 * LOG2E)
        qsq = qa * qa
        for h in range(2):
            rows = slice(a * RB + h * SUB, a * RB + (h + 1) * SUB)
            hmask = head0 if h == 0 else ~head0
            qs_ref[rows, :] = jnp.where(hmask, qa, 0.0).astype(BF16)
            n2 = jnp.sum(jnp.where(hmask, qsq, 0.0), axis=1, keepdims=True)
            qn = n2 * lax.rsqrt(n2 + 1e-30)
            mar_ref[rows, :] = qn * kn_ref[h:h + 1, :] * SB_BOUND_SLACK + SB_ZERO_BITS
    tri = jnp.where(_iota((WIDE, WIDE), 0) >= _iota((WIDE, WIDE), 1), -1.0, 0.0).astype(BF16)

    def key_rows(ku, nk):
        r0 = ku * SUB
        return pl.ds(r0 if isinstance(r0, int) else pl.multiple_of(r0, SUB), nk)

    def logits(a, ku, nk):
        return _dot_nt(qs_ref[a * RB:(a + 1) * RB, :], k_ref[0, key_rows(ku, nk), :])

    def softplus2(z, diagonal):
        sp = jnp.maximum(z, 0.0) + jnp.log(1.0 + jnp.exp2(-jnp.abs(z))) * LOG2E
        if not diagonal:
            return sp, None
        mask = _iota(z.shape, 1) < (_iota(z.shape, 0) & (SUB - 1))
        return jnp.where(mask, sp, 0.0), mask

    def suffix(sp):
        nk = sp.shape[1]
        return _dot(sp.astype(BF16), tri[:nk, :nk]), jnp.sum(sp, axis=1, keepdims=True)

    def weights(z, res, car, mask):
        att = jnp.exp2(z + res if car is None else z + res - car)
        if mask is not None:
            att = jnp.where(mask, att, 0.0)
        return att.astype(BF16)

    def far_tile(a, ku):
        rows = slice(a * RB, (a + 1) * RB)
        z = logits(a, ku, SUB)
        sp, _ = softplus2(z, False)
        res, tot = suffix(sp)
        car = car_ref[rows, :]
        acc_ref[rows, :] += _dot(weights(z, res, car, None), v_ref[0, key_rows(ku, SUB), :])
        car_ref[rows, :] = car + tot

    def near_tiles(first_block):
        chains = []
        for a in range(ns):
            g = a if first_block else ns * i + a
            chain = [(a, g, SUB, True)]
            if not first_block or a >= 2:
                chain.append((a, g - 2, WIDE, False))
            elif a == 1:
                chain.append((a, 0, SUB, False))
            chains.append(chain)
        jobs = [job for step in itertools.zip_longest(*chains) for job in step if job is not None]
        zs = [logits(a, ku, nk) for a, ku, nk, _ in jobs]
        sps = [softplus2(z, diagonal) for z, (_, _, _, diagonal) in zip(zs, jobs)]
        sufs = [suffix(sp) for sp, _ in sps]
        car = [None] * ns
        atts = []
        for z, (sp, mask), (res, tot), (a, _, _, _) in zip(zs, sps, sufs, jobs):
            atts.append(weights(z, res, car[a], mask))
            car[a] = tot if car[a] is None else car[a] + tot
        acc = [None] * ns
        for att, (a, ku, nk, _) in zip(atts, jobs):
            t = _dot(att, v_ref[0, key_rows(ku, nk), :])
            acc[a] = t if acc[a] is None else acc[a] + t
        for a in range(ns):
            acc_ref[a * RB:(a + 1) * RB, :] = acc[a]
            car_ref[a * RB:(a + 1) * RB, :] = jnp.broadcast_to(car[a], (RB, LANES))

    pl.when(i == 0)(functools.partial(near_tiles, True))
    pl.when(i > 0)(functools.partial(near_tiles, False))

    def next_unit(a, t):
        return ns * i + a - 3 - t

    def pending(t):
        gap = [jnp.where(next_unit(a, t) >= 0, mar_ref[a * RB:(a + 1) * RB, :] - car_ref[a * RB:(a + 1) * RB, :], -1.0)
               for a in range(ns)]
        return jnp.max(functools.reduce(jnp.maximum, gap)) > 0.0

    def far_body(c):
        t, _ = c
        for a in range(ns):
            ku = next_unit(a, t)
            pl.when(ku >= 0)(functools.partial(far_tile, a, ku))
        return t + 1, pending(t + 1)

    lax.while_loop(lambda c: c[1], far_body, (0, pending(0)))

    for a in range(ns):
        o_ref[0, a * SUB:(a + 1) * SUB, :] = jnp.where(
            head0, acc_ref[a * RB:a * RB + SUB, :], acc_ref[a * RB + SUB:(a + 1) * RB, :]).astype(o_ref.dtype)


def _stick_breaking(pc):
    bsz, s, _ = pc.shape
    npair = SB_W // LANES
    tq = min(SB_TQ, s)
    return pl.pallas_call(
        _sb_kernel,
        out_shape=jax.ShapeDtypeStruct((bsz, s, SB_W), BF16),
        grid=(bsz, npair, s // tq),
        in_specs=[
            pl.BlockSpec((1, tq, LANES), lambda b, p, i: (b, i, p)),
            pl.BlockSpec((1, s, LANES), lambda b, p, i: (b, 0, npair + p)),
            pl.BlockSpec((1, s, LANES), lambda b, p, i: (b, 0, 2 * npair + p)),
        ],
        out_specs=pl.BlockSpec((1, tq, LANES), lambda b, p, i: (b, i, p)),
        scratch_shapes=[
            pltpu.VMEM((2 * tq, LANES), BF16),
            pltpu.VMEM((2 * tq, LANES), F32),
            pltpu.VMEM((2 * tq, LANES), F32),
            pltpu.VMEM((2 * tq, LANES), F32),
            pltpu.VMEM((8, LANES), F32),
        ],
        compiler_params=_params("parallel", "parallel", "arbitrary"),
        name="stick_breaking",
    )(pc, pc, pc)


def _t5_bucket(dist):
    max_exact = N_REL_BUCKETS // 2
    dd = np.maximum(dist, 1).astype(np.float64)
    large = max_exact + (np.log(dd / max_exact) / np.log(REL_MAX_DIST / max_exact)
                         * (N_REL_BUCKETS - max_exact)).astype(np.int32)
    large = np.minimum(large, N_REL_BUCKETS - 1)
    return np.where(dist < max_exact, dist, large).astype(np.int32)


def _dil_buckets():
    qi = np.arange(DIL_BLK)[:, None]
    kj = np.arange(2 * DIL_BLK)[None, :]
    delta = qi - kj + DIL_BLK
    in_win = (delta >= 0) & (delta <= DIL_BLK)
    tabs = []
    for window, dil in DIL_PAIRS:
        assert window // dil == DIL_BLK
        bucket = _t5_bucket(np.clip(delta, 0, None) * dil)
        tabs.append(np.where(in_win, bucket, -1).astype(np.int32))
    return np.stack(tabs, 0)


def _dil_kernel(q_ref, k_ref, v_ref, bkt_ref, bkt0_ref, tab_ref, o_ref, m_sc, l_sc, acc_sc, bias0_sc, bias_sc):
    s_len = q_ref.shape[1]
    first_batch = pl.program_id(0) == 0
    hp = pl.program_id(1)
    g = pl.program_id(2)
    blk = DIL_BLK
    ng = len(DIL_PAIRS)
    lane = _iota((1, LANES), 1)
    lane_head = lane >> _HEAD_SHIFT
    sel0 = lane_head == 0
    kcol = _iota((2 * blk, 2 * blk), 1)

    def head_biases(gi, bkt, dst):
        @pl.when(first_batch)
        def _():
            n = bkt.shape[0]
            for hl in range(2):
                head = gi * DIL_HEADS_PER_GROUP + hp * 2 + hl
                bias = jnp.full(bkt.shape, NEG_INF, F32)
                for bk in range(N_REL_BUCKETS):
                    bias = jnp.where(bkt == bk, tab_ref[bk, head], bias)
                dst[hl * n:(hl + 1) * n, :] = bias
        return dst[...]

    def stack_heads(qb):
        return jnp.concatenate([jnp.where(sel0, qb, 0.0), jnp.where(sel0, 0.0, qb)], axis=0).astype(BF16)

    def with_ones(vv):
        return jnp.concatenate([vv, jnp.ones_like(vv)], axis=1).astype(BF16)

    def first_group(gi, dil):
        n_sub = s_len // dil
        bias2 = head_biases(gi, bkt0_ref[...], bias0_sc.at[hp])

        def residue(r):
            rows = pl.ds(r, n_sub, stride=dil)
            logits = _dot_nt(stack_heads(q_ref[0, rows, :] * (DIL_DH ** -0.5)), k_ref[0, rows, :].astype(BF16)) + bias2
            return rows, logits

        def body(tt, carry):
            s1 = [residue(tt * DIL_FIRST_UNROLL + u) for u in range(DIL_FIRST_UNROLL)]
            s2 = [jnp.max(lg, axis=1, keepdims=True) for _, lg in s1]
            s3 = [_dot(jnp.exp(lg - mc).astype(BF16), with_ones(v_ref[0, rows, :])) for (rows, lg), mc in zip(s1, s2)]
            for (rows, _), mc, pv in zip(s1, s2, s3):
                m_sc[rows, :] = jnp.where(sel0, mc[0:n_sub], mc[n_sub:])
                l_sc[rows, :] = jnp.where(sel0, pv[0:n_sub, LANES:], pv[n_sub:, LANES:])
                acc_sc[rows, :] = jnp.where(sel0, pv[0:n_sub, :LANES], pv[n_sub:, :LANES])
            return carry

        lax.fori_loop(0, dil // DIL_FIRST_UNROLL, body, 0)

    def group(gi, dil):
        nb = s_len // (blk * dil)
        bias2 = head_biases(gi, bkt_ref[gi], bias_sc.at[hp, gi])

        def stage_logits(t):
            r = t // nb
            n = t % nb
            q_start = r + dil * blk * n
            p_start = r + dil * blk * jnp.maximum(n - 1, 0)
            rows_q = pl.ds(q_start, blk, stride=dil) if dil > 1 else pl.ds(q_start, blk)
            rows_p = pl.ds(p_start, blk, stride=dil) if dil > 1 else pl.ds(p_start, blk)
            q2 = stack_heads(q_ref[0, rows_q, :] * (DIL_DH ** -0.5))
            kk = jnp.concatenate([k_ref[0, rows_p, :], k_ref[0, rows_q, :]], axis=0).astype(BF16)
            logits = _dot_nt(q2, kk) + bias2
            logits = jnp.where(jnp.logical_and(n == 0, kcol < blk), NEG_INF, logits)
            return rows_q, rows_p, logits

        def stage_pv(rows_q, rows_p, logits, m_col):
            p = jnp.exp(logits - m_col)
            vv = with_ones(jnp.concatenate([v_ref[0, rows_p, :], v_ref[0, rows_q, :]], axis=0))
            pv = _dot(p.astype(BF16), vv)
            return pv[:, LANES:], pv[:, :LANES]

        def body(tt, carry):
            ts = [tt * DIL_UNROLL + u for u in range(DIL_UNROLL)]
            s1 = [stage_logits(t) for t in ts]
            s2 = [jnp.max(lg, axis=1, keepdims=True) for _, _, lg in s1]
            s3 = [stage_pv(rq, rp, lg, mc) for (rq, rp, lg), mc in zip(s1, s2)]
            outs = []
            for (rows_q, _, _), m_col, (ps, pv) in zip(s1, s2, s3):
                m_old = m_sc[rows_q, :]
                m_blk = jnp.where(sel0, m_col[0:blk], m_col[blk:])
                m_new = jnp.maximum(m_old, m_blk)
                a_old = jnp.exp(m_old - m_new)
                a_blk = jnp.exp(m_blk - m_new)
                l_new = a_old * l_sc[rows_q, :] + a_blk * jnp.where(sel0, ps[0:blk], ps[blk:])
                a_new = a_old * acc_sc[rows_q, :] + a_blk * jnp.where(sel0, pv[0:blk], pv[blk:])
                outs.append((rows_q, m_new, l_new, a_new))
            for rows_q, m_new, l_new, a_new in outs:
                m_sc[rows_q, :] = m_new
                l_sc[rows_q, :] = l_new
                acc_sc[rows_q, :] = a_new
            return carry

        for tt in range(s_len // (blk * DIL_UNROLL)):
            body(tt, 0)

    for gi, (_, dil) in enumerate(DIL_PAIRS):
        pl.when(g == ng - 1 - gi)(functools.partial(first_group if gi == ng - 1 else group, gi, dil))

    @pl.when(g == ng - 1)
    def _():
        o_ref[0] = (acc_sc[...] / l_sc[...]).astype(o_ref.dtype)


def _dilated(pd, rel_bias):
    bsz, s, _ = pd.shape
    ng = len(DIL_PAIRS)
    gw = DIL_HEADS_PER_GROUP * DIL_DH // LANES
    nq = DIL_W // LANES
    bkt = jnp.asarray(_dil_buckets())
    dil0 = DIL_PAIRS[-1][1]
    n_sub = s // dil0
    assert s % (DIL_BLK * dil0) == 0 and dil0 % DIL_FIRST_UNROLL == 0
    delta = np.arange(n_sub)[:, None] - np.arange(n_sub)[None, :]
    bkt0 = jnp.asarray(np.where((delta >= 0) & (delta <= DIL_BLK), _t5_bucket(np.clip(delta, 0, None) * dil0), -1)
                       .astype(np.int32))
    blk_spec = lambda base: pl.BlockSpec((1, s, LANES), lambda b, p, g: (b, 0, base + (ng - 1 - g) * gw + p))
    return pl.pallas_call(
        _dil_kernel,
        out_shape=jax.ShapeDtypeStruct((bsz, s, DIL_OUT), BF16),
        grid=(bsz, gw, ng),
        in_specs=[
            blk_spec(0),
            blk_spec(nq),
            blk_spec(2 * nq),
            pl.BlockSpec((ng, DIL_BLK, 2 * DIL_BLK), lambda b, p, g: (0, 0, 0)),
            pl.BlockSpec((n_sub, n_sub), lambda b, p, g: (0, 0)),
            pl.BlockSpec(memory_space=pltpu.SMEM),
        ],
        out_specs=pl.BlockSpec((1, s, LANES), lambda b, p, g: (b, 0, p)),
        scratch_shapes=[pltpu.VMEM((s, LANES), F32)] * 3 + [
            pltpu.VMEM((gw, 2 * n_sub, n_sub), F32),
            pltpu.VMEM((gw, ng - 1, 2 * DIL_BLK, 2 * DIL_BLK), F32),
        ],
        compiler_params=_params("arbitrary", "arbitrary", "arbitrary"),
        name="dilated",
    )(pd, pd, pd, bkt, bkt0, rel_bias)


def _ab_weight(w):
    o = np.cumsum((0,) + (GLA_QK, GLA_QK, GLA_V, GLA_V, GLA_GATE_RANK, 2 * ML_W, ML_W, MLSTM_HEADS, MLSTM_HEADS, ML_W))
    qa, ka, va, ra, aa, qkb, vb, ib, fb, ob = [w[:, o[j]:o[j + 1]] for j in range(10)]
    pad = jnp.zeros((w.shape[0], LANES - 2 * MLSTM_HEADS - GLA_GATE_RANK), w.dtype)
    return jnp.concatenate([qa, ka, va, ra, qkb, vb, ob, ib, fb, aa, pad], axis=1).astype(BF16)


def kernel(x, c, ada_w, ada_b, ln_g, ln_b, ab_w_in, gla_wa_up, gla_ba, gla_norm_g, ml_conv_w, ml_conv_b,
           ml_b_i, ml_b_f, ml_norm_g, ab_w_out, cd_w_in, rel_bias, cd_w_out, ffn_w1, ffn_w3, ffn_w2):
    bsz, s, d = x.shape
    mod_all = _ada_mod(c, ada_w, ada_b).reshape(DEPTH, bsz, 6, d)
    w1, w3, w2 = ffn_w1.astype(BF16), ffn_w3.astype(BF16), ffn_w2.astype(BF16)
    tm = IN_TM
    for layer in range(DEPTH):
        mod = mod_all[layer]
        j = layer // 2
        if layer % 2 == 0:
            gates = [(P0_RA, P0_RA + GLA_V, lambda y, b: _silu(y)),
                     (P0_OB, P0_OB + ML_W, lambda y, b: jax.nn.sigmoid(y)),
                     (P0_G, P0_G + LANES, _gate_prep)]
            (proj,) = _inproj(x, mod, _ab_weight(ab_w_in[j]), [(P0_N, F32)], tm, 0, 1, "inproj0", gates,
                              _gate_bias(ml_b_i[j], ml_b_f[j]))
            cat = _mixer0(proj, gla_wa_up[j], gla_ba[j], gla_norm_g[j], ml_conv_w[j], ml_conv_b[j], ml_norm_g[j])
            acts, w_outs = [cat], [ab_w_out[j].astype(BF16)]
        else:
            pc, pd = _inproj(x, mod, cd_w_in[j].astype(BF16), [(3 * SB_W, BF16), (3 * DIL_W, F32)],
                             tm, 0, 1, "inproj1")
            oc = _stick_breaking(pc)
            od = _dilated(pd, rel_bias)
            w_out = cd_w_out[j].astype(BF16)
            acts, w_outs = [oc, od], [w_out[:SB_W], w_out[SB_W:]]
        x = _post(acts, w_outs, x, mod, w1, w3, w2, layer, ln_g[layer], ln_b[layer], FFN_TM)
    return x
```

```python
import functools
import itertools

import numpy as np
import jax
import jax.numpy as jnp
from jax import lax
from jax.experimental import pallas as pl
from jax.experimental.pallas import tpu as pltpu

F32 = jnp.float32
BF16 = jnp.bfloat16

D_MODEL = 1024
DEPTH = 2
GLA_HEADS = 4
GLA_DK = 64
GLA_DV = 128
GLA_GATE_RANK = 16
GLA_TAU = 16.0
GLA_CHUNK = 64
MLSTM_HEADS = 4
MLSTM_DH = 128
MLSTM_CONV = 4
SB_HEADS = 8
SB_DH = 64
DIL_PAIRS = ((128, 1), (512, 4), (2048, 16))
DIL_HEADS_PER_GROUP = 4
DIL_DH = 64
DIL_BLK = 128
DIL_UNROLL = 4
DIL_FIRST_UNROLL = 2
N_REL_BUCKETS = 32
REL_MAX_DIST = 2048
D_FF = ((8 * D_MODEL + 3 * 256 - 1) // (3 * 256)) * 256
LN_EPS = 1e-5
RES_ALPHA = (2 * DEPTH) ** 0.25
NEG_INF = -1e30

GLA_QK = GLA_HEADS * GLA_DK
GLA_V = GLA_HEADS * GLA_DV
ML_W = MLSTM_HEADS * MLSTM_DH
SB_W = SB_HEADS * SB_DH
DIL_W = len(DIL_PAIRS) * DIL_HEADS_PER_GROUP * DIL_DH
DIL_OUT = DIL_HEADS_PER_GROUP * DIL_DH

LANES = 128
VMEM_LIMIT = 56 * 1024 * 1024

P0_QA = 0
P0_KA = P0_QA + GLA_QK
P0_VA = P0_KA + GLA_QK
P0_RA = P0_VA + GLA_V
P0_QKB = P0_RA + GLA_V
P0_VB = P0_QKB + 2 * ML_W
P0_OB = P0_VB + ML_W
P0_G = P0_OB + ML_W
P0_N = P0_G + LANES
G_I = 0
G_F = MLSTM_HEADS
G_A = 2 * MLSTM_HEADS

MIX_TILE = 256
MIX_ROWS = 4
FF_CHUNK = 256
IN_TM = 512
FFN_TM = 1024
POST_SUB = 512
SB_TQ = 4096
SB_SUB = 128
SB_WIDE = 256
SB_ZERO_BITS = 160.0
SB_BOUND_SLACK = 1.01
LOG2E = 1.4426950408889634
_CHUNK_SHIFT = 6
_HEAD_SHIFT = 6
assert GLA_CHUNK == GLA_DK == 1 << _CHUNK_SHIFT and SB_DH == DIL_DH == 1 << _HEAD_SHIFT


def _dot(a, b):
    return jnp.dot(a, b, preferred_element_type=F32)


def _dot_nt(a, b):
    return lax.dot_general(a, b, (((1,), (1,)), ((), ())), preferred_element_type=F32)


def _dot_tn(a, b):
    return lax.dot_general(a, b, (((0,), (0,)), ((), ())), preferred_element_type=F32)


def _dot_split(t, x, terms):
    acc = None
    rem = x
    for i in range(terms):
        part = rem.astype(BF16)
        d = _dot(t, part)
        acc = d if acc is None else acc + d
        if i + 1 < terms:
            rem = rem - part.astype(F32)
    return acc


def _log_sigmoid(x):
    return jnp.minimum(x, 0.0) - jnp.log(1.0 + jnp.exp(-jnp.abs(x)))


def _silu(x):
    return x * jax.nn.sigmoid(x)


def _layer_norm(r, g, b):
    mu = jnp.mean(r, axis=-1, keepdims=True)
    d = r - mu
    var = jnp.mean(d * d, axis=-1, keepdims=True)
    return d * lax.rsqrt(var + LN_EPS) * g + b


def _head_norm(x, g):
    mu = jnp.mean(x, axis=-1, keepdims=True)
    d = x - mu
    var = jnp.mean(d * d, axis=-1, keepdims=True)
    return d * lax.rsqrt(var + LN_EPS) * g


def _iota(shape, dim):
    return lax.broadcasted_iota(jnp.int32, shape, dim)


def _params(*sem):
    return pltpu.CompilerParams(dimension_semantics=sem, vmem_limit_bytes=VMEM_LIMIT)


def _ada_kernel(c_ref, w_ref, b_ref, o_ref):
    ca = _silu(c_ref[...]).astype(BF16)
    o_ref[0] = _dot(ca, w_ref[0].astype(BF16)) + b_ref[0]


def _ada_mod(c, ada_w, ada_b):
    bsz, d = c.shape
    n = ada_w.shape[-1]
    tn = n // 4
    return pl.pallas_call(
        _ada_kernel,
        out_shape=jax.ShapeDtypeStruct((DEPTH, bsz, n), F32),
        grid=(DEPTH, n // tn),
        in_specs=[
            pl.BlockSpec((bsz, d), lambda l, j: (0, 0)),
            pl.BlockSpec((1, d, tn), lambda l, j: (l, 0, j)),
            pl.BlockSpec((1, 1, tn), lambda l, j: (l, 0, j)),
        ],
        out_specs=pl.BlockSpec((1, bsz, tn), lambda l, j: (l, 0, j)),
        compiler_params=_params("arbitrary", "arbitrary"),
        name="ada_mod",
    )(c, ada_w, ada_b.reshape(DEPTH, 1, n))


def _inproj_kernel(x_ref, mod_ref, w_ref, *o_refs, shift_row, scale_row, gates):
    sh = mod_ref[0, shift_row:shift_row + 1, :]
    sc = mod_ref[0, scale_row:scale_row + 1, :]
    hm = (x_ref[0] * (1.0 + sc) + sh).astype(BF16)
    col = 0
    for o_ref in o_refs:
        n = o_ref.shape[-1]
        cuts = sorted({0, n} | {c for lo, hi, _ in gates if col <= lo and hi <= col + n for c in (lo - col, hi - col)})
        for lo, hi in zip(cuts[:-1], cuts[1:]):
            y = _dot(hm, w_ref[:, col + lo:col + hi])
            for glo, ghi, fn in gates:
                if glo == col + lo and ghi == col + hi:
                    y = fn(y)
            o_ref[0, :, lo:hi] = y.astype(o_ref.dtype)
        col += n


def _inproj(x, mod, w, outs, tm, shift_row, scale_row, name, gates=()):
    bsz, s, d = x.shape
    n = w.shape[1]
    assert sum(o[0] for o in outs) == n
    return pl.pallas_call(
        functools.partial(_inproj_kernel, shift_row=shift_row, scale_row=scale_row, gates=tuple(gates)),
        out_shape=[jax.ShapeDtypeStruct((bsz, s, o[0]), o[1]) for o in outs],
        grid=(bsz, s // tm),
        in_specs=[
            pl.BlockSpec((1, tm, d), lambda b, i: (b, i, 0)),
            pl.BlockSpec((1, 6, d), lambda b, i: (b, 0, 0)),
            pl.BlockSpec((d, n), lambda b, i: (0, 0)),
        ],
        out_specs=[pl.BlockSpec((1, tm, o[0]), lambda b, i: (b, i, 0)) for o in outs],
        compiler_params=_params("parallel", "arbitrary"),
        name=name,
    )(x, mod, w)


def _post_kernel(*refs, n_act):
    act_refs = refs[:n_act]
    wo_refs = refs[n_act:2 * n_act]
    x_ref, mod_ref, w1_ref, w3_ref, w2_ref, g_ref, b_ref, o_ref, gm_ref = refs[2 * n_act:]

    def mix(t):
        rows = slice(t * POST_SUB, (t + 1) * POST_SUB)
        y = None
        for a_ref, w_ref in zip(act_refs, wo_refs):
            part = _dot(a_ref[0, rows, :], w_ref[...])
            y = part if y is None else y + part
        r = RES_ALPHA * x_ref[0, rows, :] + (1.0 + mod_ref[0, 2:3, :]) * y
        xm = _layer_norm(r, g_ref[0:1, :], b_ref[0:1, :])
        return xm, (xm * (1.0 + mod_ref[0, 4:5, :]) + mod_ref[0, 3:4, :]).astype(BF16)

    def up(t, hf):
        for c in range(D_FF // FF_CHUNK):
            cs = slice(c * FF_CHUNK, (c + 1) * FF_CHUNK)
            gm_ref[t, :, cs] = (_silu(_dot(hf, w1_ref[:, cs])) * _dot(hf, w3_ref[:, cs])).astype(BF16)

    def down(t, xm):
        rows = slice(t * POST_SUB, (t + 1) * POST_SUB)
        r2 = RES_ALPHA * xm + (1.0 + mod_ref[0, 5:6, :]) * _dot(gm_ref[t], w2_ref[...])
        o_ref[0, rows, :] = _layer_norm(r2, g_ref[1:2, :], b_ref[1:2, :])

    n = x_ref.shape[1] // POST_SUB
    mixed = {}
    for step in range(n + 2):
        if step >= 2:
            down(step - 2, mixed[step - 2][0])
        if 1 <= step <= n:
            up(step - 1, mixed[step - 1][1])
        if step < n:
            mixed[step] = mix(step)


def _post(acts, w_outs, x, mod, w1, w3, w2, layer, ln_g, ln_b, tm):
    bsz, s, d = x.shape
    n_act = len(acts)
    const = lambda shape: pl.BlockSpec(shape, lambda b, i: (0,) * len(shape), pipeline_mode=pl.Buffered(1))
    of_layer = lambda shape: pl.BlockSpec((None,) + shape, lambda b, i: (layer, 0, 0), pipeline_mode=pl.Buffered(1))
    in_specs = [pl.BlockSpec((1, tm, a.shape[-1]), lambda b, i: (b, i, 0)) for a in acts]
    in_specs += [const(w.shape) for w in w_outs]
    in_specs += [
        pl.BlockSpec((1, tm, d), lambda b, i: (b, i, 0)),
        pl.BlockSpec((1, 6, d), lambda b, i: (b, 0, 0)),
        of_layer((d, D_FF)),
        of_layer((d, D_FF)),
        of_layer((D_FF, d)),
        const((2, d)),
        const((2, d)),
    ]
    return pl.pallas_call(
        functools.partial(_post_kernel, n_act=n_act),
        out_shape=jax.ShapeDtypeStruct((bsz, s, d), F32),
        grid=(bsz, s // tm),
        in_specs=in_specs,
        out_specs=pl.BlockSpec((1, tm, d), lambda b, i: (b, i, 0)),
        scratch_shapes=[pltpu.VMEM((tm // POST_SUB, POST_SUB, D_FF), BF16)],
        compiler_params=_params("parallel", "arbitrary"),
        name="post",
    )(*acts, *w_outs, x, mod, w1, w3, w2, ln_g, ln_b)


def _mixer0_kernel(p_ref, wa_ref, ba_ref, gg_ref, cw_ref, cb_ref, gb_ref, mg_ref, o_ref,
                   st_ref, cst_ref, m_ref, xc_ref):
    @pl.when(pl.program_id(1) == 0)
    def _():
        st_ref[...] = jnp.zeros_like(st_ref)
        cst_ref[...] = jnp.zeros_like(cst_ref)
        m_ref[...] = jnp.zeros_like(m_ref)
        xc_ref[:, 0:8, :] = jnp.zeros((xc_ref.shape[0], 8, 2 * ML_W), F32)

    L = MIX_TILE
    row = _iota((L, L), 0)
    col = _iota((L, L), 1)
    causal = col <= row
    same_chunk = (row >> _CHUNK_SHIFT) == (col >> _CHUNK_SHIFT)
    lane_qk = _iota((1, GLA_QK), 1)
    r4 = _iota((GLA_HEADS * GLA_CHUNK, GLA_CHUNK), 0)
    c4 = _iota((GLA_HEADS * GLA_CHUNK, GLA_CHUNK), 1)
    consts = dict(
        causal=causal,
        tri=jnp.where(causal, 1.0, 0.0).astype(BF16),
        tri_blk=jnp.where(same_chunk & causal, 1.0, 0.0).astype(BF16),
        head_masks=[(lane_qk >> _CHUNK_SHIFT) == h for h in range(GLA_HEADS)],
        tril4=c4 <= (r4 & (GLA_CHUNK - 1)),
    )
    for b in range(p_ref.shape[0]):
        _mixer0_row(consts, p_ref.at[pl.ds(b, 1)], wa_ref, ba_ref, gg_ref, cw_ref, cb_ref, gb_ref, mg_ref,
                    o_ref.at[pl.ds(b, 1)], st_ref.at[b], cst_ref.at[b], m_ref.at[b], xc_ref.at[b])


def _mixer0_row(consts, p_ref, wa_ref, ba_ref, gg_ref, cw_ref, cb_ref, gb_ref, mg_ref, o_ref,
                st_ref, cst_ref, m_ref, xc_ref):
    L = MIX_TILE
    causal, tri, tri_blk = consts["causal"], consts["tri"], consts["tri_blk"]
    head_masks, tril4 = consts["head_masks"], consts["tril4"]
    lane = _iota((1, LANES), 1)
    g_raw = p_ref[0, :, P0_G:P0_G + LANES]

    u = _dot(g_raw.astype(BF16), wa_ref[...]) + ba_ref[...]
    la = _log_sigmoid(u) * (1.0 / GLA_TAU)
    bcs = _dot_split(tri_blk, la, 3)
    q_in = p_ref[0, :, P0_QA:P0_QA + GLA_QK] * jnp.exp(bcs)
    k_raw = p_ref[0, :, P0_KA:P0_KA + GLA_QK]
    k_in = k_raw * jnp.exp(-bcs)
    oa_chunks = []
    for c in range(L // GLA_CHUNK):
        r0, r1 = c * GLA_CHUNK, (c + 1) * GLA_CHUNK
        b_c = bcs[r0:r1]
        bl = b_c[GLA_CHUNK - 1:GLA_CHUNK, :]
        q_c = q_in[r0:r1]
        k_c = k_in[r0:r1].astype(BF16)
        k_end = k_raw[r0:r1] * jnp.exp(bl - b_c)
        v_c = p_ref[0, r0:r1, P0_VA:P0_VA + GLA_V]
        q_exp = jnp.concatenate([jnp.where(hm, q_c, 0.0) for hm in head_masks], axis=0).astype(BF16)
        k_exp = jnp.concatenate([jnp.where(hm, k_end, 0.0) for hm in head_masks], axis=0).astype(BF16)
        sc = jnp.where(tril4, _dot_nt(q_exp, k_c), 0.0)
        intra = _dot(sc.astype(BF16), v_c.astype(BF16))
        inter = _dot_nt(q_exp, st_ref[...].astype(BF16))
        o_heads = []
        for h in range(GLA_HEADS):
            h0, h1 = h * GLA_CHUNK, (h + 1) * GLA_CHUNK
            o_heads.append(intra[h0:h1, h * GLA_DV:(h + 1) * GLA_DV] + inter[h0:h1])
        oa_chunks.append(o_heads)
        v_cat = jnp.concatenate([v_c[:, h * GLA_DV:(h + 1) * GLA_DV] for h in range(GLA_HEADS)], axis=0)
        st_ref[...] = st_ref[...] * jnp.exp(bl) + _dot_tn(v_cat.astype(BF16), k_exp)
    for h in range(GLA_HEADS):
        o_h = jnp.concatenate([oc[h] for oc in oa_chunks], axis=0)
        cs = slice(h * GLA_DV, (h + 1) * GLA_DV)
        ra = p_ref[0, :, P0_RA + h * GLA_DV:P0_RA + (h + 1) * GLA_DV]
        o_ref[0, :, cs] = (_head_norm(o_h, gg_ref[:, cs]) * ra).astype(o_ref.dtype)

    xc_ref[8:8 + L, :] = p_ref[0, :, P0_QKB:P0_QKB + 2 * ML_W]
    conv = cb_ref[...]
    for kk in range(MLSTM_CONV):
        conv = conv + cw_ref[kk:kk + 1, :] * xc_ref[8 - (MLSTM_CONV - 1) + kk:8 - (MLSTM_CONV - 1) + kk + L, :]
    xc_ref[0:8, :] = xc_ref[L:L + 8, :]
    qk_b = _silu(conv)

    gb = jnp.where(lane < G_A, g_raw + gb_ref[...], 0.0)
    ipre = gb
    logf = _log_sigmoid(pltpu.roll(gb, LANES - G_F, axis=1))
    logf = jnp.where(lane < MLSTM_HEADS, logf, 0.0)
    bcum = _dot_split(tri, logf, 3)
    blast = bcum[L - 1:L, :]
    wend = blast - bcum + ipre
    m_prev = m_ref[...]
    m_new = jnp.maximum(blast + m_prev, jnp.max(wend, axis=0, keepdims=True))
    scl = jnp.exp(blast + m_prev - m_new)
    wj = jnp.exp(wend - m_new)
    rows_src = jnp.where(lane < MLSTM_HEADS, ipre, pltpu.roll(bcum, MLSTM_HEADS, axis=1))
    rows_t = rows_src.T
    ones = jnp.ones((L, MLSTM_DH), F32)
    for h in range(MLSTM_HEADS):
        hs = slice(h * MLSTM_DH, (h + 1) * MLSTM_DH)
        wide = lambda t: jnp.concatenate([t, t], axis=1)
        b_rep = jnp.broadcast_to(bcum[:, h:h + 1], (L, LANES))
        wj_rep = jnp.broadcast_to(wj[:, h:h + 1], (L, LANES))
        ip_row = rows_t[h:h + 1, :]
        b_row = rows_t[MLSTM_HEADS + h:MLSTM_HEADS + h + 1, :]
        dlog = jnp.where(causal, wide(b_rep) - b_row + ip_row, NEG_INF)
        il = b_rep + m_prev[:, h:h + 1]
        m_i = jnp.maximum(il, jnp.broadcast_to(jnp.max(dlog, axis=1, keepdims=True), (L, LANES)))
        w_intra = jnp.exp(dlog - wide(m_i))
        s_inter = jnp.exp(il - m_i)
        q_h = qk_b[:, h * MLSTM_DH:(h + 1) * MLSTM_DH].astype(BF16)
        k_h = (qk_b[:, ML_W + h * MLSTM_DH:ML_W + (h + 1) * MLSTM_DH] * (MLSTM_DH ** -0.5)).astype(BF16)
        v_h = p_ref[0, :, P0_VB + h * MLSTM_DH:P0_VB + (h + 1) * MLSTM_DH]
        v_aug = jnp.concatenate([v_h, ones], axis=1)
        a = (w_intra * _dot_nt(q_h, k_h)).astype(BF16)
        c_prev = cst_ref[h]
        tot = _dot(a, v_aug.astype(BF16)) + wide(s_inter) * _dot(q_h, c_prev.astype(BF16))
        num = tot[:, :MLSTM_DH]
        den = tot[:, MLSTM_DH:]
        hid = num / jnp.maximum(jnp.abs(den), jnp.exp(-m_i))
        w_aug = (v_aug * wide(wj_rep)).astype(BF16)
        cst_ref[h] = scl[:, h:h + 1] * c_prev + _dot_tn(k_h, w_aug)
        ob = p_ref[0, :, P0_OB + h * MLSTM_DH:P0_OB + (h + 1) * MLSTM_DH]
        o_ref[0, :, GLA_V + h * MLSTM_DH:GLA_V + (h + 1) * MLSTM_DH] = (
            ob * _head_norm(hid, mg_ref[:, hs])).astype(o_ref.dtype)
    m_ref[...] = m_new


def _mixer0(proj, wa_up, ba, gla_g, conv_w, conv_b, b_i, b_f, ml_g):
    bsz, s, n = proj.shape
    L = MIX_TILE
    nr = MIX_ROWS if bsz % MIX_ROWS == 0 else 1
    wa_pad = jnp.zeros((LANES, GLA_QK), F32).at[G_A:G_A + GLA_GATE_RANK].set(wa_up).astype(BF16)
    gbias = jnp.zeros((1, LANES), F32).at[0, G_I:G_I + MLSTM_HEADS].set(b_i).at[0, G_F:G_F + MLSTM_HEADS].set(b_f)
    full = lambda shape: pl.BlockSpec(shape, lambda b, i: (0,) * len(shape))
    return pl.pallas_call(
        _mixer0_kernel,
        out_shape=jax.ShapeDtypeStruct((bsz, s, GLA_V + ML_W), BF16),
        grid=(bsz // nr, s // L),
        in_specs=[
            pl.BlockSpec((nr, L, n), lambda b, i: (b, i, 0)),
            full((LANES, GLA_QK)),
            full((1, GLA_QK)),
            full((1, GLA_V)),
            full((MLSTM_CONV, 2 * ML_W)),
            full((1, 2 * ML_W)),
            full((1, LANES)),
            full((1, ML_W)),
        ],
        out_specs=pl.BlockSpec((nr, L, GLA_V + ML_W), lambda b, i: (b, i, 0)),
        scratch_shapes=[
            pltpu.VMEM((nr, GLA_DV, GLA_QK), F32),
            pltpu.VMEM((nr, MLSTM_HEADS, MLSTM_DH, 2 * MLSTM_DH), F32),
            pltpu.VMEM((nr, 1, LANES), F32),
            pltpu.VMEM((nr, 8 + L, 2 * ML_W), F32),
        ],
        compiler_params=_params("parallel", "arbitrary"),
        name="mixer0",
    )(proj, wa_pad, ba.reshape(1, GLA_QK), gla_g.reshape(1, GLA_V), conv_w, conv_b.reshape(1, 2 * ML_W),
      gbias, ml_g.reshape(1, ML_W))


def _sb_kernel(q_ref, k_ref, v_ref, o_ref, qs_ref, acc_ref, car_ref, mar_ref, kn_ref):
    TQ, SUB, WIDE = q_ref.shape[1], SB_SUB, SB_WIDE
    ns = TQ // SUB
    RB = 2 * SUB
    i = pl.program_id(2)
    lane = _iota((1, LANES), 1)
    head0 = (lane >> _HEAD_SHIFT) == 0

    @pl.when(i == 0)
    def _():
        k2 = jnp.square(k_ref[0].astype(F32))
        for h in range(2):
            n2 = jnp.sum(jnp.where(head0 if h == 0 else ~head0, k2, 0.0), axis=1, keepdims=True)
            kn_ref[h:h + 1, :] = jnp.broadcast_to(jnp.sqrt(jnp.max(n2, axis=0, keepdims=True)), (1, LANES))

    for a in range(ns):
        qa = q_ref[0, a * SUB:(a + 1) * SUB, :].astype(F32) * (SB_DH ** -0.5 * LOG2E)
        qsq = qa * qa
        for h in range(2):
            rows = slice(a * RB + h * SUB, a * RB + (h + 1) * SUB)
            hmask = head0 if h == 0 else ~head0
            qs_ref[rows, :] = jnp.where(hmask, qa, 0.0).astype(BF16)
            n2 = jnp.sum(jnp.where(hmask, qsq, 0.0), axis=1, keepdims=True)
            qn = n2 * lax.rsqrt(n2 + 1e-30)
            mar_ref[rows, :] = qn * kn_ref[h:h + 1, :] * SB_BOUND_SLACK + SB_ZERO_BITS
    tri = jnp.where(_iota((WIDE, WIDE), 0) >= _iota((WIDE, WIDE), 1), -1.0, 0.0).astype(BF16)

    def key_rows(ku, nk):
        r0 = ku * SUB
        return pl.ds(r0 if isinstance(r0, int) else pl.multiple_of(r0, SUB), nk)

    def logits(a, ku, nk):
        return _dot_nt(qs_ref[a * RB:(a + 1) * RB, :], k_ref[0, key_rows(ku, nk), :])

    def softplus2(z, diagonal):
        sp = jnp.maximum(z, 0.0) + jnp.log(1.0 + jnp.exp2(-jnp.abs(z))) * LOG2E
        if not diagonal:
            return sp, None
        mask = _iota(z.shape, 1) < (_iota(z.shape, 0) & (SUB - 1))
        return jnp.where(mask, sp, 0.0), mask

    def suffix(sp):
        nk = sp.shape[1]
        return _dot(sp.astype(BF16), tri[:nk, :nk]), jnp.sum(sp, axis=1, keepdims=True)

    def weights(z, res, car, mask):
        att = jnp.exp2(z + res if car is None else z + res - car)
        if mask is not None:
            att = jnp.where(mask, att, 0.0)
        return att.astype(BF16)

    def far_tile(a, ku):
        rows = slice(a * RB, (a + 1) * RB)
        z = logits(a, ku, SUB)
        sp, _ = softplus2(z, False)
        res, tot = suffix(sp)
        car = car_ref[rows, :]
        acc_ref[rows, :] += _dot(weights(z, res, car, None), v_ref[0, key_rows(ku, SUB), :])
        car_ref[rows, :] = car + tot

    def near_tiles(first_block):
        chains = []
        for a in range(ns):
            g = a if first_block else ns * i + a
            chain = [(a, g, SUB, True)]
            if not first_block or a >= 2:
                chain.append((a, g - 2, WIDE, False))
            elif a == 1:
                chain.append((a, 0, SUB, False))
            chains.append(chain)
        jobs = [job for step in itertools.zip_longest(*chains) for job in step if job is not None]
        zs = [logits(a, ku, nk) for a, ku, nk, _ in jobs]
        sps = [softplus2(z, diagonal) for z, (_, _, _, diagonal) in zip(zs, jobs)]
        sufs = [suffix(sp) for sp, _ in sps]
        car = [None] * ns
        atts = []
        for z, (sp, mask), (res, tot), (a, _, _, _) in zip(zs, sps, sufs, jobs):
            atts.append(weights(z, res, car[a], mask))
            car[a] = tot if car[a] is None else car[a] + tot
        acc = [None] * ns
        for att, (a, ku, nk, _) in zip(atts, jobs):
            t = _dot(att, v_ref[0, key_rows(ku, nk), :])
            acc[a] = t if acc[a] is None else acc[a] + t
        for a in range(ns):
            acc_ref[a * RB:(a + 1) * RB, :] = acc[a]
            car_ref[a * RB:(a + 1) * RB, :] = jnp.broadcast_to(car[a], (RB, LANES))

    pl.when(i == 0)(functools.partial(near_tiles, True))
    pl.when(i > 0)(functools.partial(near_tiles, False))

    def next_unit(a, t):
        return ns * i + a - 3 - t

    def pending(t):
        gap = [jnp.where(next_unit(a, t) >= 0, mar_ref[a * RB:(a + 1) * RB, :] - car_ref[a * RB:(a + 1) * RB, :], -1.0)
               for a in range(ns)]
        return jnp.max(functools.reduce(jnp.maximum, gap)) > 0.0

    def far_body(c):
        t, _ = c
        for a in range(ns):
            ku = next_unit(a, t)
            pl.when(ku >= 0)(functools.partial(far_tile, a, ku))
        return t + 1, pending(t + 1)

    lax.while_loop(lambda c: c[1], far_body, (0, pending(0)))

    for a in range(ns):
        o_ref[0, a * SUB:(a + 1) * SUB, :] = jnp.where(
            head0, acc_ref[a * RB:a * RB + SUB, :], acc_ref[a * RB + SUB:(a + 1) * RB, :]).astype(o_ref.dtype)


def _stick_breaking(pc):
    bsz, s, _ = pc.shape
    npair = SB_W // LANES
    tq = min(SB_TQ, s)
    return pl.pallas_call(
        _sb_kernel,
        out_shape=jax.ShapeDtypeStruct((bsz, s, SB_W), BF16),
        grid=(bsz, npair, s // tq),
        in_specs=[
            pl.BlockSpec((1, tq, LANES), lambda b, p, i: (b, i, p)),
            pl.BlockSpec((1, s, LANES), lambda b, p, i: (b, 0, npair + p)),
            pl.BlockSpec((1, s, LANES), lambda b, p, i: (b, 0, 2 * npair + p)),
        ],
        out_specs=pl.BlockSpec((1, tq, LANES), lambda b, p, i: (b, i, p)),
        scratch_shapes=[
            pltpu.VMEM((2 * tq, LANES), BF16),
            pltpu.VMEM((2 * tq, LANES), F32),
            pltpu.VMEM((2 * tq, LANES), F32),
            pltpu.VMEM((2 * tq, LANES), F32),
            pltpu.VMEM((8, LANES), F32),
        ],
        compiler_params=_params("parallel", "parallel", "arbitrary"),
        name="stick_breaking",
    )(pc, pc, pc)


def _t5_bucket(dist):
    max_exact = N_REL_BUCKETS // 2
    dd = np.maximum(dist, 1).astype(np.float64)
    large = max_exact + (np.log(dd / max_exact) / np.log(REL_MAX_DIST / max_exact)
                         * (N_REL_BUCKETS - max_exact)).astype(np.int32)
    large = np.minimum(large, N_REL_BUCKETS - 1)
    return np.where(dist < max_exact, dist, large).astype(np.int32)


def _dil_buckets():
    qi = np.arange(DIL_BLK)[:, None]
    kj = np.arange(2 * DIL_BLK)[None, :]
    delta = qi - kj + DIL_BLK
    in_win = (delta >= 0) & (delta <= DIL_BLK)
    tabs = []
    for window, dil in DIL_PAIRS:
        assert window // dil == DIL_BLK
        bucket = _t5_bucket(np.clip(delta, 0, None) * dil)
        tabs.append(np.where(in_win, bucket, -1).astype(np.int32))
    return np.stack(tabs, 0)


def _dil_kernel(q_ref, k_ref, v_ref, bkt_ref, bkt0_ref, tab_ref, o_ref, m_sc, l_sc, acc_sc, bias0_sc, bias_sc):
    s_len = q_ref.shape[1]
    first_batch = pl.program_id(0) == 0
    hp = pl.program_id(1)
    g = pl.program_id(2)
    blk = DIL_BLK
    ng = len(DIL_PAIRS)
    lane = _iota((1, LANES), 1)
    lane_head = lane >> _HEAD_SHIFT
    sel0 = lane_head == 0
    kcol = _iota((2 * blk, 2 * blk), 1)

    def head_biases(gi, bkt, dst):
        @pl.when(first_batch)
        def _():
            n = bkt.shape[0]
            for hl in range(2):
                head = gi * DIL_HEADS_PER_GROUP + hp * 2 + hl
                bias = jnp.full(bkt.shape, NEG_INF, F32)
                for bk in range(N_REL_BUCKETS):
                    bias = jnp.where(bkt == bk, tab_ref[bk, head], bias)
                dst[hl * n:(hl + 1) * n, :] = bias
        return dst[...]

    def stack_heads(qb):
        return jnp.concatenate([jnp.where(sel0, qb, 0.0), jnp.where(sel0, 0.0, qb)], axis=0).astype(BF16)

    def with_ones(vv):
        return jnp.concatenate([vv, jnp.ones_like(vv)], axis=1).astype(BF16)

    def first_group(gi, dil):
        n_sub = s_len // dil
        bias2 = head_biases(gi, bkt0_ref[...], bias0_sc.at[hp])

        def residue(r):
            rows = pl.ds(r, n_sub, stride=dil)
            logits = _dot_nt(stack_heads(q_ref[0, rows, :]), k_ref[0, rows, :].astype(BF16)) + bias2
            return rows, logits

        def body(tt, carry):
            s1 = [residue(tt * DIL_FIRST_UNROLL + u) for u in range(DIL_FIRST_UNROLL)]
            s2 = [jnp.max(lg, axis=1, keepdims=True) for _, lg in s1]
            s3 = [_dot(jnp.exp(lg - mc).astype(BF16), with_ones(v_ref[0, rows, :])) for (rows, lg), mc in zip(s1, s2)]
            for (rows, _), mc, pv in zip(s1, s2, s3):
                m_sc[rows, :] = jnp.where(sel0, mc[0:n_sub], mc[n_sub:])
                l_sc[rows, :] = jnp.where(sel0, pv[0:n_sub, LANES:], pv[n_sub:, LANES:])
                acc_sc[rows, :] = jnp.where(sel0, pv[0:n_sub, :LANES], pv[n_sub:, :LANES])
            return carry

        lax.fori_loop(0, dil // DIL_FIRST_UNROLL, body, 0)

    def group(gi, dil):
        nb = s_len // (blk * dil)
        bias2 = head_biases(gi, bkt_ref[gi], bias_sc.at[hp, gi])

        def stage_logits(t):
            r = t // nb
            n = t % nb
            q_start = r + dil * blk * n
            p_start = r + dil * blk * jnp.maximum(n - 1, 0)
            rows_q = pl.ds(q_start, blk, stride=dil) if dil > 1 else pl.ds(q_start, blk)
            rows_p = pl.ds(p_start, blk, stride=dil) if dil > 1 else pl.ds(p_start, blk)
            q2 = stack_heads(q_ref[0, rows_q, :])
            kk = jnp.concatenate([k_ref[0, rows_p, :], k_ref[0, rows_q, :]], axis=0).astype(BF16)
            logits = _dot_nt(q2, kk) + bias2
            logits = jnp.where(jnp.logical_and(n == 0, kcol < blk), NEG_INF, logits)
            return rows_q, rows_p, logits

        def stage_pv(rows_q, rows_p, logits, m_col):
            p = jnp.exp(logits - m_col)
            vv = with_ones(jnp.concatenate([v_ref[0, rows_p, :], v_ref[0, rows_q, :]], axis=0))
            pv = _dot(p.astype(BF16), vv)
            return pv[:, LANES:], pv[:, :LANES]

        def body(tt, carry):
            ts = [tt * DIL_UNROLL + u for u in range(DIL_UNROLL)]
            s1 = [stage_logits(t) for t in ts]
            s2 = [jnp.max(lg, axis=1, keepdims=True) for _, _, lg in s1]
            s3 = [stage_pv(rq, rp, lg, mc) for (rq, rp, lg), mc in zip(s1, s2)]
            outs = []
            for (rows_q, _, _), m_col, (ps, pv) in zip(s1, s2, s3):
                m_old = m_sc[rows_q, :]
                m_blk = jnp.where(sel0, m_col[0:blk], m_col[blk:])
                m_new = jnp.maximum(m_old, m_blk)
                a_old = jnp.exp(m_old - m_new)
                a_blk = jnp.exp(m_blk - m_new)
                l_new = a_old * l_sc[rows_q, :] + a_blk * jnp.where(sel0, ps[0:blk], ps[blk:])
                a_new = a_old * acc_sc[rows_q, :] + a_blk * jnp.where(sel0, pv[0:blk], pv[blk:])
                outs.append((rows_q, m_new, l_new, a_new))
            for rows_q, m_new, l_new, a_new in outs:
                m_sc[rows_q, :] = m_new
                l_sc[rows_q, :] = l_new
                acc_sc[rows_q, :] = a_new
            return carry

        for tt in range(s_len // (blk * DIL_UNROLL)):
            body(tt, 0)

    for gi, (_, dil) in enumerate(DIL_PAIRS):
        pl.when(g == ng - 1 - gi)(functools.partial(first_group if gi == ng - 1 else group, gi, dil))

    @pl.when(g == ng - 1)
    def _():
        o_ref[0] = (acc_sc[...] / l_sc[...]).astype(o_ref.dtype)


def _dilated(pd, rel_bias):
    bsz, s, _ = pd.shape
    ng = len(DIL_PAIRS)
    gw = DIL_HEADS_PER_GROUP * DIL_DH // LANES
    nq = DIL_W // LANES
    bkt = jnp.asarray(_dil_buckets())
    dil0 = DIL_PAIRS[-1][1]
    n_sub = s // dil0
    assert s % (DIL_BLK * dil0) == 0 and dil0 % DIL_FIRST_UNROLL == 0
    delta = np.arange(n_sub)[:, None] - np.arange(n_sub)[None, :]
    bkt0 = jnp.asarray(np.where((delta >= 0) & (delta <= DIL_BLK), _t5_bucket(np.clip(delta, 0, None) * dil0), -1)
                       .astype(np.int32))
    blk_spec = lambda base: pl.BlockSpec((1, s, LANES), lambda b, p, g: (b, 0, base + (ng - 1 - g) * gw + p))
    return pl.pallas_call(
        _dil_kernel,
        out_shape=jax.ShapeDtypeStruct((bsz, s, DIL_OUT), BF16),
        grid=(bsz, gw, ng),
        in_specs=[
            blk_spec(0),
            blk_spec(nq),
            blk_spec(2 * nq),
            pl.BlockSpec((ng, DIL_BLK, 2 * DIL_BLK), lambda b, p, g: (0, 0, 0)),
            pl.BlockSpec((n_sub, n_sub), lambda b, p, g: (0, 0)),
            pl.BlockSpec(memory_space=pltpu.SMEM),
        ],
        out_specs=pl.BlockSpec((1, s, LANES), lambda b, p, g: (b, 0, p)),
        scratch_shapes=[pltpu.VMEM((s, LANES), F32)] * 3 + [
            pltpu.VMEM((gw, 2 * n_sub, n_sub), F32),
            pltpu.VMEM((gw, ng - 1, 2 * DIL_BLK, 2 * DIL_BLK), F32),
        ],
        compiler_params=_params("arbitrary", "arbitrary", "arbitrary"),
        name="dilated",
    )(pd, pd, pd, bkt, bkt0, rel_bias)


def _ab_weight(w):
    o = np.cumsum((0,) + (GLA_QK, GLA_QK, GLA_V, GLA_V, GLA_GATE_RANK, 2 * ML_W, ML_W, MLSTM_HEADS, MLSTM_HEADS, ML_W))
    qa, ka, va, ra, aa, qkb, vb, ib, fb, ob = [w[:, o[j]:o[j + 1]] for j in range(10)]
    pad = jnp.zeros((w.shape[0], LANES - 2 * MLSTM_HEADS - GLA_GATE_RANK), w.dtype)
    qa = qa * (GLA_DK ** -0.5)
    return jnp.concatenate([qa, ka, va, ra, qkb, vb, ob, ib, fb, aa, pad], axis=1).astype(BF16)


def _cd_weight(w):
    assert float(np.log2(DIL_DH ** -0.5)).is_integer() and float(np.log2(GLA_DK ** -0.5)).is_integer()
    scale = np.ones((1, w.shape[1]), np.float32)
    scale[0, 3 * SB_W:3 * SB_W + DIL_W] = DIL_DH ** -0.5
    return (w * scale).astype(BF16)


def kernel(x, c, ada_w, ada_b, ln_g, ln_b, ab_w_in, gla_wa_up, gla_ba, gla_norm_g, ml_conv_w, ml_conv_b,
           ml_b_i, ml_b_f, ml_norm_g, ab_w_out, cd_w_in, rel_bias, cd_w_out, ffn_w1, ffn_w3, ffn_w2):
    bsz, s, d = x.shape
    mod_all = _ada_mod(c, ada_w, ada_b).reshape(DEPTH, bsz, 6, d)
    w1, w3, w2 = ffn_w1.astype(BF16), ffn_w3.astype(BF16), ffn_w2.astype(BF16)
    tm = IN_TM
    for layer in range(DEPTH):
        mod = mod_all[layer]
        j = layer // 2
        if layer % 2 == 0:
            gates = [(P0_RA, P0_RA + GLA_V, _silu), (P0_OB, P0_OB + ML_W, jax.nn.sigmoid)]
            (proj,) = _inproj(x, mod, _ab_weight(ab_w_in[j]), [(P0_N, F32)], tm, 0, 1, "inproj0", gates)
            cat = _mixer0(proj, gla_wa_up[j], gla_ba[j], gla_norm_g[j], ml_conv_w[j], ml_conv_b[j],
                          ml_b_i[j], ml_b_f[j], ml_norm_g[j])
            acts, w_outs = [cat], [ab_w_out[j].astype(BF16)]
        else:
            pc, pd = _inproj(x, mod, _cd_weight(cd_w_in[j]), [(3 * SB_W, BF16), (3 * DIL_W, F32)],
                             tm, 0, 1, "inproj1")
            oc = _stick_breaking(pc)
            od = _dilated(pd, rel_bias)
            w_out = cd_w_out[j].astype(BF16)
            acts, w_outs = [oc, od], [w_out[:SB_W], w_out[SB_W:]]
        x = _post(acts, w_outs, x, mod, w1, w3, w2, layer, ln_g[layer], ln_b[layer], FFN_TM)
    return x
```
